```python
import jax, jax.numpy as jnp
from jax import lax
import numpy as np

D_MODEL = 1024
BATCH = 8
SEQ = 4096
DEPTH = 4

CHUNK = 64
N_MIXERS = 2
N_ATTN_LAYERS = (DEPTH + 1) // 2
N_RNN_LAYERS = DEPTH // 2

ATTN_HEADS = 16
ATTN_HEAD_DIM = D_MODEL // ATTN_HEADS
ATTN_WIDTH = ATTN_HEADS * ATTN_HEAD_DIM
Q_BLOCK = 128

RNN_WIDTH = D_MODEL
RNN_BLOCK_WIDTH = 256
RNN_BLOCKS = RNN_WIDTH // RNN_BLOCK_WIDTH
CONV_WIDTH = 4
LRU_C = 8.0

DEEPNORM_ALPHA = (2.0 * DEPTH) ** 0.25
DEEPNORM_BETA = (8.0 * DEPTH) ** -0.25
LN_EPS = 1e-5

kernel_name = "fox_rglru_deepnorm_hybrid"


def layer_norm(x, g, b):
    xf = x.astype(jnp.float32)
    mu = jnp.mean(xf, axis=-1, keepdims=True)
    var = jnp.mean(jnp.square(xf - mu), axis=-1, keepdims=True)
    y = (xf - mu) * lax.rsqrt(var + LN_EPS) * g.astype(jnp.float32) + b.astype(jnp.float32)
    return y.astype(x.dtype)


def forgetting_attention(x, w_in, b_f, w_out):
    B, S, _ = x.shape
    H, Dh = ATTN_HEADS, ATTN_HEAD_DIM
    proj = jnp.einsum("bsd,de->bse", x, w_in)
    q, k, v, gate = jnp.split(proj[..., :4 * ATTN_WIDTH], 4, axis=-1)
    f_logit = proj[..., 4 * ATTN_WIDTH:].astype(jnp.float32) + b_f.astype(jnp.float32)
    cum = jnp.cumsum(jax.nn.log_sigmoid(f_logit), axis=1).transpose(0, 2, 1)

    def heads(t):
        return t.reshape(B, S, H, Dh).transpose(0, 2, 1, 3)

    q = heads(q) * (Dh ** -0.5)
    k = heads(k)
    v = heads(v)
    outs = []
    for blk in range(S // Q_BLOCK):
        q0 = blk * Q_BLOCK
        q1 = q0 + Q_BLOCK
        s = jnp.einsum("bhqd,bhkd->bhqk", q[:, :, q0:q1], k[:, :, :q1]).astype(jnp.float32)
        s = s + cum[:, :, q0:q1, None] - cum[:, :, None, :q1]
        causal = (q0 + jnp.arange(Q_BLOCK))[:, None] >= jnp.arange(q1)[None, :]
        p = jax.nn.softmax(jnp.where(causal, s, -jnp.inf), axis=-1)
        outs.append(jnp.einsum("bhqk,bhkd->bhqd", p.astype(v.dtype), v[:, :, :q1]))
    o = jnp.concatenate(outs, axis=2).transpose(0, 2, 1, 3).reshape(B, S, ATTN_WIDTH)
    return jnp.einsum("bse,ed->bsd", o * jax.nn.silu(gate), w_out)


def rglru_block(x, w_in, conv_w, conv_b, w_a, b_a, w_i, b_i, lam, w_out):
    B, S, _ = x.shape
    proj = jnp.einsum("bsd,de->bse", x, w_in)
    u, gate = jnp.split(proj, 2, axis=-1)
    u_pad = jnp.pad(u, ((0, 0), (CONV_WIDTH - 1, 0), (0, 0)))
    u = conv_b + sum(u_pad[:, tap:tap + S] * conv_w[tap] for tap in range(CONV_WIDTH))
    ub = u.reshape(B, S, RNN_BLOCKS, RNN_BLOCK_WIDTH)
    r = jax.nn.sigmoid(jnp.einsum("bsnc,ncd->bsnd", ub, w_a).reshape(B, S, RNN_WIDTH) + b_a)
    i = jax.nn.sigmoid(jnp.einsum("bsnc,ncd->bsnd", ub, w_i).reshape(B, S, RNN_WIDTH) + b_i)
    log_a = -LRU_C * r.astype(jnp.float32) * jax.nn.softplus(-lam.astype(jnp.float32))
    a = jnp.exp(log_a)
    bterm = jnp.sqrt(-jnp.expm1(2.0 * log_a)) * (i * u).astype(jnp.float32)

    def combine(left, right):
        a1, b1 = left
        a2, b2 = right
        return a1 * a2, a2 * b1 + b2

    _, h = lax.associative_scan(combine, (a, bterm), axis=1)
    y = h.astype(x.dtype) * jax.nn.silu(gate)
    return jnp.einsum("bse,ed->bsd", y, w_out)


def _fwd_setup_inputs(seed: int = 0) -> dict:
    key = jax.random.key(seed)
    ks = jax.random.split(key, 16)
    nA, nR = N_ATTN_LAYERS, N_RNN_LAYERS
    f32 = jnp.float32
    x = jax.random.normal(ks[0], (BATCH, SEQ, D_MODEL), f32)
    ln_g = 1.0 + 0.02 * jax.random.normal(ks[1], (DEPTH, D_MODEL), f32)
    ln_b = 0.02 * jax.random.normal(ks[2], (DEPTH, D_MODEL), f32)
    attn_w_in = jax.random.normal(ks[3], (nA, D_MODEL, 4 * ATTN_WIDTH + ATTN_HEADS), f32) * D_MODEL ** -0.5
    attn_b_f = jax.random.uniform(ks[4], (nA, ATTN_HEADS), f32, 1.0, 4.0)
    attn_w_out = jax.random.normal(ks[5], (nA, ATTN_WIDTH, D_MODEL), f32) * (ATTN_WIDTH ** -0.5 * DEEPNORM_BETA)
    rnn_w_in = jax.random.normal(ks[6], (nR, D_MODEL, 2 * RNN_WIDTH), f32) * D_MODEL ** -0.5
    rnn_conv_w = jax.random.normal(ks[7], (nR, CONV_WIDTH, RNN_WIDTH), f32) * CONV_WIDTH ** -0.5
    rnn_conv_b = 0.02 * jax.random.normal(ks[8], (nR, RNN_WIDTH), f32)
    rnn_w_a = jax.random.normal(ks[9], (nR, RNN_BLOCKS, RNN_BLOCK_WIDTH, RNN_BLOCK_WIDTH), f32) * RNN_BLOCK_WIDTH ** -0.5
    rnn_b_a = 0.02 * jax.random.normal(ks[10], (nR, RNN_WIDTH), f32)
    rnn_w_i = jax.random.normal(ks[11], (nR, RNN_BLOCKS, RNN_BLOCK_WIDTH, RNN_BLOCK_WIDTH), f32) * RNN_BLOCK_WIDTH ** -0.5
    rnn_b_i = 0.02 * jax.random.normal(ks[12], (nR, RNN_WIDTH), f32)
    a0 = jax.random.uniform(ks[13], (nR, RNN_WIDTH), f32, 0.9, 0.999)
    rnn_lambda = jnp.log(a0) - jnp.log1p(-a0)
    rnn_w_out = jax.random.normal(ks[14], (nR, RNN_WIDTH, D_MODEL), f32) * (RNN_WIDTH ** -0.5 * DEEPNORM_BETA)
    return {
        "x": x, "ln_g": ln_g, "ln_b": ln_b,
        "attn_w_in": attn_w_in, "attn_b_f": attn_b_f, "attn_w_out": attn_w_out,
        "rnn_w_in": rnn_w_in, "rnn_conv_w": rnn_conv_w, "rnn_conv_b": rnn_conv_b,
        "rnn_w_a": rnn_w_a, "rnn_b_a": rnn_b_a, "rnn_w_i": rnn_w_i, "rnn_b_i": rnn_b_i,
        "rnn_lambda": rnn_lambda, "rnn_w_out": rnn_w_out,
    }


def _fwd_reference(x, ln_g, ln_b, attn_w_in, attn_b_f, attn_w_out, rnn_w_in, rnn_conv_w,
              rnn_conv_b, rnn_w_a, rnn_b_a, rnn_w_i, rnn_b_i, rnn_lambda, rnn_w_out):
    for layer in range(DEPTH):
        idx = layer // N_MIXERS
        if layer % N_MIXERS == 0:
            h = forgetting_attention(x, attn_w_in[idx], attn_b_f[idx], attn_w_out[idx])
        else:
            h = rglru_block(x, rnn_w_in[idx], rnn_conv_w[idx], rnn_conv_b[idx],
                            rnn_w_a[idx], rnn_b_a[idx], rnn_w_i[idx], rnn_b_i[idx],
                            rnn_lambda[idx], rnn_w_out[idx])
        x = layer_norm(DEEPNORM_ALPHA * x + h, ln_g[layer], ln_b[layer])
    return x


import jax as _jax
import jax.numpy as _jnp

TWIN_FORMAT = 'train_step'
FWD_PARAMS = ['x', 'ln_g', 'ln_b', 'attn_w_in', 'attn_b_f', 'attn_w_out', 'rnn_w_in', 'rnn_conv_w', 'rnn_conv_b', 'rnn_w_a', 'rnn_b_a', 'rnn_w_i', 'rnn_b_i', 'rnn_lambda', 'rnn_w_out']
TWIN_WEIGHTS = ['ln_g', 'ln_b', 'attn_w_in', 'attn_b_f', 'attn_w_out', 'rnn_w_in', 'rnn_conv_w', 'rnn_conv_b', 'rnn_w_a', 'rnn_b_a', 'rnn_w_i', 'rnn_b_i', 'rnn_lambda', 'rnn_w_out']
TWIN_DIFF_INPUT = 'x'
TWIN_INPUTS = ['x', 'ln_g', 'ln_b', 'attn_w_in', 'attn_b_f', 'attn_w_out', 'rnn_w_in', 'rnn_conv_w', 'rnn_conv_b', 'rnn_w_a', 'rnn_b_a', 'rnn_w_i', 'rnn_b_i', 'rnn_lambda', 'rnn_w_out', 'loss_target', 'm_ln_g', 'm_ln_b', 'm_attn_w_in', 'm_attn_b_f', 'm_attn_w_out', 'm_rnn_w_in', 'm_rnn_conv_w', 'm_rnn_conv_b', 'm_rnn_w_a', 'm_rnn_b_a', 'm_rnn_w_i', 'm_rnn_b_i', 'm_rnn_lambda', 'm_rnn_w_out', 'v_ln_g', 'v_ln_b', 'v_attn_w_in', 'v_attn_b_f', 'v_attn_w_out', 'v_rnn_w_in', 'v_rnn_conv_w', 'v_rnn_conv_b', 'v_rnn_w_a', 'v_rnn_b_a', 'v_rnn_w_i', 'v_rnn_b_i', 'v_rnn_lambda', 'v_rnn_w_out']
TWIN_OUTPUTS = ['loss', 'grad_x', 'grad_ln_g', 'grad_ln_b', 'grad_attn_w_in', 'grad_attn_b_f', 'grad_attn_w_out', 'grad_rnn_w_in', 'grad_rnn_conv_w', 'grad_rnn_conv_b', 'grad_rnn_w_a', 'grad_rnn_b_a', 'grad_rnn_w_i', 'grad_rnn_b_i', 'grad_rnn_lambda', 'grad_rnn_w_out', 'delta_ln_g', 'delta_ln_b', 'delta_attn_w_in', 'delta_attn_b_f', 'delta_attn_w_out', 'delta_rnn_w_in', 'delta_rnn_conv_w', 'delta_rnn_conv_b', 'delta_rnn_w_a', 'delta_rnn_b_a', 'delta_rnn_w_i', 'delta_rnn_b_i', 'delta_rnn_lambda', 'delta_rnn_w_out', 'new_m_ln_g', 'new_m_ln_b', 'new_m_attn_w_in', 'new_m_attn_b_f', 'new_m_attn_w_out', 'new_m_rnn_w_in', 'new_m_rnn_conv_w', 'new_m_rnn_conv_b', 'new_m_rnn_w_a', 'new_m_rnn_b_a', 'new_m_rnn_w_i', 'new_m_rnn_b_i', 'new_m_rnn_lambda', 'new_m_rnn_w_out', 'new_v_ln_g', 'new_v_ln_b', 'new_v_attn_w_in', 'new_v_attn_b_f', 'new_v_attn_w_out', 'new_v_rnn_w_in', 'new_v_rnn_conv_w', 'new_v_rnn_conv_b', 'new_v_rnn_w_a', 'new_v_rnn_b_a', 'new_v_rnn_w_i', 'new_v_rnn_b_i', 'new_v_rnn_lambda', 'new_v_rnn_w_out']
TWIN_LEAF_KINDS = {'loss': 'loss', 'grad_x': 'grad_x', 'grad_ln_g': 'grad_w', 'grad_ln_b': 'grad_w', 'grad_attn_w_in': 'grad_w', 'grad_attn_b_f': 'grad_w', 'grad_attn_w_out': 'grad_w', 'grad_rnn_w_in': 'grad_w', 'grad_rnn_conv_w': 'grad_w', 'grad_rnn_conv_b': 'grad_w', 'grad_rnn_w_a': 'grad_w', 'grad_rnn_b_a': 'grad_w', 'grad_rnn_w_i': 'grad_w', 'grad_rnn_b_i': 'grad_w', 'grad_rnn_lambda': 'grad_w', 'grad_rnn_w_out': 'grad_w', 'delta_ln_g': 'delta_w', 'delta_ln_b': 'delta_w', 'delta_attn_w_in': 'delta_w', 'delta_attn_b_f': 'delta_w', 'delta_attn_w_out': 'delta_w', 'delta_rnn_w_in': 'delta_w', 'delta_rnn_conv_w': 'delta_w', 'delta_rnn_conv_b': 'delta_w', 'delta_rnn_w_a': 'delta_w', 'delta_rnn_b_a': 'delta_w', 'delta_rnn_w_i': 'delta_w', 'delta_rnn_b_i': 'delta_w', 'delta_rnn_lambda': 'delta_w', 'delta_rnn_w_out': 'delta_w', 'new_m_ln_g': 'new_m', 'new_m_ln_b': 'new_m', 'new_m_attn_w_in': 'new_m', 'new_m_attn_b_f': 'new_m', 'new_m_attn_w_out': 'new_m', 'new_m_rnn_w_in': 'new_m', 'new_m_rnn_conv_w': 'new_m', 'new_m_rnn_conv_b': 'new_m', 'new_m_rnn_w_a': 'new_m', 'new_m_rnn_b_a': 'new_m', 'new_m_rnn_w_i': 'new_m', 'new_m_rnn_b_i': 'new_m', 'new_m_rnn_lambda': 'new_m', 'new_m_rnn_w_out': 'new_m', 'new_v_ln_g': 'new_v', 'new_v_ln_b': 'new_v', 'new_v_attn_w_in': 'new_v', 'new_v_attn_b_f': 'new_v', 'new_v_attn_w_out': 'new_v', 'new_v_rnn_w_in': 'new_v', 'new_v_rnn_conv_w': 'new_v', 'new_v_rnn_conv_b': 'new_v', 'new_v_rnn_w_a': 'new_v', 'new_v_rnn_b_a': 'new_v', 'new_v_rnn_w_i': 'new_v', 'new_v_rnn_b_i': 'new_v', 'new_v_rnn_lambda': 'new_v', 'new_v_rnn_w_out': 'new_v'}


def _forward(args):
    return _fwd_reference(*[args[k] for k in FWD_PARAMS])


def _output_shape():
    out = _jax.eval_shape(lambda: _forward(_fwd_setup_inputs(0)))
    return out.shape, out.dtype

N_MICROBATCH = 1
ADAM_LR = 0.001
ADAM_B1 = 0.9
ADAM_B2 = 0.999
ADAM_EPS = 1e-08
ADAM_WD = 0.01
ADAM_STEP = 10
PER_EXAMPLE_BATCH_AXIS = {'x': 0, 'loss_target': 0}
SHARED_INPUTS = []
_WEIGHT_DTYPES = {'ln_g': _jnp.float32, 'ln_b': _jnp.float32, 'attn_w_in': _jnp.float32, 'attn_b_f': _jnp.float32, 'attn_w_out': _jnp.float32, 'rnn_w_in': _jnp.float32, 'rnn_conv_w': _jnp.float32, 'rnn_conv_b': _jnp.float32, 'rnn_w_a': _jnp.float32, 'rnn_b_a': _jnp.float32, 'rnn_w_i': _jnp.float32, 'rnn_b_i': _jnp.float32, 'rnn_lambda': _jnp.float32, 'rnn_w_out': _jnp.float32}
MOMENT_SCALE = {'ln_g': 1.603828e+01, 'ln_b': 6.445010e-01, 'attn_w_in': 9.328257e-03, 'attn_b_f': 5.851789e-02, 'attn_w_out': 2.474930e-02, 'rnn_w_in': 1.568997e-02, 'rnn_conv_w': 1.603137e-02, 'rnn_conv_b': 8.903256e-02, 'rnn_w_a': 3.647904e-03, 'rnn_b_a': 3.639656e-03, 'rnn_w_i': 6.228375e-03, 'rnn_b_i': 5.652416e-03, 'rnn_lambda': 7.557964e-03, 'rnn_w_out': 3.551603e-02}


def _to_microbatches(a, axis):
    t = _jnp.moveaxis(a, axis, 0)
    t = t.reshape((N_MICROBATCH, t.shape[0] // N_MICROBATCH) + t.shape[1:])
    return _jnp.moveaxis(t, 1, axis + 1)


def setup_inputs(seed: int = 0) -> dict:
    inp = _fwd_setup_inputs(seed)
    key = _jax.random.fold_in(_jax.random.key(seed), 7919)
    shape, _ = _output_shape()
    out = dict(inp)
    out["loss_target"] = _jax.random.normal(_jax.random.fold_in(key, 0), shape, _jnp.float32)
    for i, name in enumerate(TWIN_WEIGHTS):
        w = inp[name].astype(_jnp.float32)
        if MOMENT_SCALE is None:
            s = _jnp.sqrt(_jnp.mean(_jnp.square(w)) + 1e-30)
        else:
            s = MOMENT_SCALE[name]
        km, kv = _jax.random.split(_jax.random.fold_in(key, i + 1))
        out[name] = w
        out["m_" + name] = s * _jax.random.normal(km, w.shape, _jnp.float32)
        out["v_" + name] = (s * s) * _jax.random.uniform(kv, w.shape, _jnp.float32, 0.5, 1.5)
    if N_MICROBATCH > 1:
        for name, axis in PER_EXAMPLE_BATCH_AXIS.items():
            out[name] = _to_microbatches(out[name], axis)
    return {'x': out['x'], 'ln_g': out['ln_g'], 'ln_b': out['ln_b'], 'attn_w_in': out['attn_w_in'], 'attn_b_f': out['attn_b_f'], 'attn_w_out': out['attn_w_out'], 'rnn_w_in': out['rnn_w_in'], 'rnn_conv_w': out['rnn_conv_w'], 'rnn_conv_b': out['rnn_conv_b'], 'rnn_w_a': out['rnn_w_a'], 'rnn_b_a': out['rnn_b_a'], 'rnn_w_i': out['rnn_w_i'], 'rnn_b_i': out['rnn_b_i'], 'rnn_lambda': out['rnn_lambda'], 'rnn_w_out': out['rnn_w_out'], 'loss_target': out['loss_target'], 'm_ln_g': out['m_ln_g'], 'm_ln_b': out['m_ln_b'], 'm_attn_w_in': out['m_attn_w_in'], 'm_attn_b_f': out['m_attn_b_f'], 'm_attn_w_out': out['m_attn_w_out'], 'm_rnn_w_in': out['m_rnn_w_in'], 'm_rnn_conv_w': out['m_rnn_conv_w'], 'm_rnn_conv_b': out['m_rnn_conv_b'], 'm_rnn_w_a': out['m_rnn_w_a'], 'm_rnn_b_a': out['m_rnn_b_a'], 'm_rnn_w_i': out['m_rnn_w_i'], 'm_rnn_b_i': out['m_rnn_b_i'], 'm_rnn_lambda': out['m_rnn_lambda'], 'm_rnn_w_out': out['m_rnn_w_out'], 'v_ln_g': out['v_ln_g'], 'v_ln_b': out['v_ln_b'], 'v_attn_w_in': out['v_attn_w_in'], 'v_attn_b_f': out['v_attn_b_f'], 'v_attn_w_out': out['v_attn_w_out'], 'v_rnn_w_in': out['v_rnn_w_in'], 'v_rnn_conv_w': out['v_rnn_conv_w'], 'v_rnn_conv_b': out['v_rnn_conv_b'], 'v_rnn_w_a': out['v_rnn_w_a'], 'v_rnn_b_a': out['v_rnn_b_a'], 'v_rnn_w_i': out['v_rnn_w_i'], 'v_rnn_b_i': out['v_rnn_b_i'], 'v_rnn_lambda': out['v_rnn_lambda'], 'v_rnn_w_out': out['v_rnn_w_out']}


def _loss(weights, diff, rest, loss_target):
    with _jax.named_scope("forward"):
        args = {**rest, TWIN_DIFF_INPUT: diff, **{k: w.astype(_WEIGHT_DTYPES[k]) for k, w in weights.items()}}
        y = _forward(args)
    with _jax.named_scope("loss_head"):
        err = _jnp.square(y.astype(_jnp.float32) - loss_target)
        return 0.5 * _jnp.sum(_jnp.mean(err, axis=-1)) if err.ndim else 0.5 * err


def _adamw(w, g, m, v):
    m = ADAM_B1 * m + (1.0 - ADAM_B1) * g
    v = ADAM_B2 * v + (1.0 - ADAM_B2) * _jnp.square(g)
    m_hat = m / (1.0 - ADAM_B1 ** ADAM_STEP)
    v_hat = v / (1.0 - ADAM_B2 ** ADAM_STEP)
    delta = -ADAM_LR * (m_hat / (_jnp.sqrt(v_hat) + ADAM_EPS) + ADAM_WD * w)
    return delta, m, v


def reference(x, ln_g, ln_b, attn_w_in, attn_b_f, attn_w_out, rnn_w_in, rnn_conv_w, rnn_conv_b, rnn_w_a, rnn_b_a, rnn_w_i, rnn_b_i, rnn_lambda, rnn_w_out, loss_target, m_ln_g, m_ln_b, m_attn_w_in, m_attn_b_f, m_attn_w_out, m_rnn_w_in, m_rnn_conv_w, m_rnn_conv_b, m_rnn_w_a, m_rnn_b_a, m_rnn_w_i, m_rnn_b_i, m_rnn_lambda, m_rnn_w_out, v_ln_g, v_ln_b, v_attn_w_in, v_attn_b_f, v_attn_w_out, v_rnn_w_in, v_rnn_conv_w, v_rnn_conv_b, v_rnn_w_a, v_rnn_b_a, v_rnn_w_i, v_rnn_b_i, v_rnn_lambda, v_rnn_w_out):
    given = dict(x=x, ln_g=ln_g, ln_b=ln_b, attn_w_in=attn_w_in, attn_b_f=attn_b_f, attn_w_out=attn_w_out, rnn_w_in=rnn_w_in, rnn_conv_w=rnn_conv_w, rnn_conv_b=rnn_conv_b, rnn_w_a=rnn_w_a, rnn_b_a=rnn_b_a, rnn_w_i=rnn_w_i, rnn_b_i=rnn_b_i, rnn_lambda=rnn_lambda, rnn_w_out=rnn_w_out, loss_target=loss_target, m_ln_g=m_ln_g, m_ln_b=m_ln_b, m_attn_w_in=m_attn_w_in, m_attn_b_f=m_attn_b_f, m_attn_w_out=m_attn_w_out, m_rnn_w_in=m_rnn_w_in, m_rnn_conv_w=m_rnn_conv_w, m_rnn_conv_b=m_rnn_conv_b, m_rnn_w_a=m_rnn_w_a, m_rnn_b_a=m_rnn_b_a, m_rnn_w_i=m_rnn_w_i, m_rnn_b_i=m_rnn_b_i, m_rnn_lambda=m_rnn_lambda, m_rnn_w_out=m_rnn_w_out, v_ln_g=v_ln_g, v_ln_b=v_ln_b, v_attn_w_in=v_attn_w_in, v_attn_b_f=v_attn_b_f, v_attn_w_out=v_attn_w_out, v_rnn_w_in=v_rnn_w_in, v_rnn_conv_w=v_rnn_conv_w, v_rnn_conv_b=v_rnn_conv_b, v_rnn_w_a=v_rnn_w_a, v_rnn_b_a=v_rnn_b_a, v_rnn_w_i=v_rnn_w_i, v_rnn_b_i=v_rnn_b_i, v_rnn_lambda=v_rnn_lambda, v_rnn_w_out=v_rnn_w_out)
    weights = {n: given[n] for n in TWIN_WEIGHTS}
    shared = {n: given[n] for n in SHARED_INPUTS}
    per_example = {n: given[n] for n in ['x']}
    grad_fn = _jax.value_and_grad(_loss, argnums=(0, 1))

    def one_microbatch(ex, loss_target):
        ex = dict(ex)
        diff = ex.pop(TWIN_DIFF_INPUT)
        return grad_fn(weights, diff, {**shared, **ex}, loss_target)

    if N_MICROBATCH == 1:
        loss, (grad_w, grad_x) = one_microbatch(per_example, given["loss_target"])
    else:
        def body(carry, xs):
            loss_sum, grad_sum = carry
            l_k, (gw_k, gx_k) = one_microbatch(xs[0], xs[1])
            with _jax.named_scope("update"):
                return (loss_sum + l_k, _jax.tree.map(_jnp.add, grad_sum, gw_k)), gx_k

        init = (_jnp.zeros((), _jnp.float32), _jax.tree.map(_jnp.zeros_like, weights))
        (loss, grad_w), grad_x = _jax.lax.scan(body, init, (per_example, given["loss_target"]))
    with _jax.named_scope("update"):
        delta_w, new_m, new_v = {}, {}, {}
        for n in TWIN_WEIGHTS:
            delta_w[n], new_m[n], new_v[n] = _adamw(weights[n], grad_w[n], given["m_" + n], given["v_" + n])
    return (loss, grad_x, *[grad_w[n] for n in TWIN_WEIGHTS], *[delta_w[n] for n in TWIN_WEIGHTS],
            *[new_m[n] for n in TWIN_WEIGHTS], *[new_v[n] for n in TWIN_WEIGHTS])
```

```python
import functools

import jax
import jax.numpy as jnp
from jax import lax
from jax.experimental import pallas as pl
from jax.experimental.pallas import tpu as pltpu

F32 = jnp.float32
BF16 = jnp.bfloat16

DEPTH = 4
N_HEADS = 16
HEAD_DIM = 64
N_PAIRS = N_HEADS // 2
RNN_BLOCKS = 4
RNN_BLOCK_WIDTH = 256
CONV_WIDTH = 4
LRU_C = 8.0
ALPHA = (2.0 * DEPTH) ** 0.25
LN_EPS = 1e-5
ADAM_LR, ADAM_B1, ADAM_B2, ADAM_EPS, ADAM_WD, ADAM_STEP = 0.001, 0.9, 0.999, 1e-8, 0.01, 10

LANES = 128
SUBLANES = 8
VMEM_LIMIT = 48 * 1024 * 1024

MESH = pl.DeviceIdType.MESH
HBM_SPEC = pl.BlockSpec(memory_space=pltpu.HBM)


def _cparams(*sem):
    return pltpu.CompilerParams(dimension_semantics=sem, vmem_limit_bytes=VMEM_LIMIT)


def _sigmoid(x):
    return 1.0 / (1.0 + jnp.exp(-x))


def _softplus(x):
    return jnp.maximum(x, 0.0) + jnp.log(1.0 + jnp.exp(-jnp.abs(x)))


def _a2a(src, *, group, bcast, name):
    n = 2 if group == "c" else 4
    blk = tuple(src.shape) if bcast else tuple(src.shape[1:])

    def body(src_ref, out_ref, send_sems, recv_sems, local_sem):
        x, y, c = lax.axis_index("x"), lax.axis_index("y"), lax.axis_index("c")
        if group == "c":
            me = c

            def peer(d):
                return (x, y, 1 - c), 1 - c
        else:
            me = 2 * x + y

            def peer(d):
                px, py = x ^ (d >> 1), y ^ (d & 1)
                return (px, py, c), 2 * px + py

        def block_for(k):
            return src_ref if bcast else src_ref.at[k]

        local = pltpu.make_async_copy(block_for(me), out_ref.at[me], local_sem)
        local.start()
        sends = []
        for d in range(1, n):
            dev, idx = peer(d)
            cp = pltpu.make_async_remote_copy(
                src_ref=block_for(idx), dst_ref=out_ref.at[me],
                send_sem=send_sems.at[d], recv_sem=recv_sems.at[d],
                device_id=dev, device_id_type=MESH)
            cp.start()
            sends.append(cp)
        for d in range(1, n):
            dev, idx = peer(d)
            pltpu.make_async_remote_copy(
                src_ref=block_for(idx), dst_ref=out_ref.at[idx],
                send_sem=send_sems.at[d], recv_sem=recv_sems.at[d],
                device_id=dev, device_id_type=MESH).wait_recv()
        for cp in sends:
            cp.wait_send()
        local.wait()

    return pl.pallas_call(
        body, name=name,
        out_shape=jax.ShapeDtypeStruct((n,) + blk, src.dtype),
        in_specs=[HBM_SPEC], out_specs=HBM_SPEC,
        scratch_shapes=[pltpu.SemaphoreType.DMA((n,)), pltpu.SemaphoreType.DMA((n,)),
                        pltpu.SemaphoreType.DMA],
    )(src)


def _all_gather(piece, name):
    return _a2a(_a2a(piece, group="xy", bcast=True, name=name + "_xy"),
                group="c", bcast=True, name=name + "_c")


def _matmul(a, b, *, trans_b, tm, tn, name, add=None, add_scale=1.0):
    M, K = a.shape
    N = b.shape[0] if trans_b else b.shape[1]
    tm, tn = min(tm, M), min(tn, N)
    assert M % tm == 0 and N % tn == 0
    dn = (((1,), (1,)), ((), ())) if trans_b else (((1,), (0,)), ((), ()))

    def body(*refs):
        if add is None:
            a_ref, b_ref, o_ref = refs
        else:
            a_ref, b_ref, add_ref, o_ref = refs
        r = lax.dot_general(a_ref[...].astype(BF16), b_ref[...].astype(BF16), dn,
                            preferred_element_type=F32)
        if add is not None:
            r = r + add_scale * add_ref[...]
        o_ref[...] = r

    b_spec = (pl.BlockSpec((tn, K), lambda j, i: (j, 0)) if trans_b
              else pl.BlockSpec((K, tn), lambda j, i: (0, j)))
    in_specs = [pl.BlockSpec((tm, K), lambda j, i: (i, 0)), b_spec]
    args = [a, b]
    if add is not None:
        in_specs.append(pl.BlockSpec((tm, tn), lambda j, i: (i, j)))
        args.append(add)
    return pl.pallas_call(
        body, name=name, grid=(N // tn, M // tm),
        in_specs=in_specs, out_specs=pl.BlockSpec((tm, tn), lambda j, i: (i, j)),
        out_shape=jax.ShapeDtypeStruct((M, N), F32),
        compiler_params=_cparams("parallel", "parallel"),
    )(*args)


def _matmul_tn(a, b, *, tm, tn, tk, name):
    T, M = a.shape
    N = b.shape[1]
    tm, tn, tk = min(tm, M), min(tn, N), min(tk, T)
    assert M % tm == 0 and N % tn == 0 and T % tk == 0

    def body(a_ref, b_ref, o_ref):
        @pl.when(pl.program_id(2) == 0)
        def _():
            o_ref[...] = jnp.zeros_like(o_ref)

        o_ref[...] += lax.dot_general(a_ref[...].astype(BF16), b_ref[...].astype(BF16),
                                      (((0,), (0,)), ((), ())), preferred_element_type=F32)

    return pl.pallas_call(
        body, name=name, grid=(M // tm, N // tn, T // tk),
        in_specs=[pl.BlockSpec((tk, tm), lambda i, j, k: (k, i)),
                  pl.BlockSpec((tk, tn), lambda i, j, k: (k, j))],
        out_specs=pl.BlockSpec((tm, tn), lambda i, j, k: (i, j)),
        out_shape=jax.ShapeDtypeStruct((M, N), F32),
        compiler_params=_cparams("parallel", "parallel", "arbitrary"),
    )(a, b)


def _head_masks(rows):
    lane = lax.broadcasted_iota(jnp.int32, (rows, LANES), 1)
    return lane < HEAD_DIM, lane >= HEAD_DIM


def _causal(i_q, i_k, tq, tk):
    row = i_q * tq + lax.broadcasted_iota(jnp.int32, (tq, tk), 0)
    col = i_k * tk + lax.broadcasted_iota(jnp.int32, (tq, tk), 1)
    return row >= col


def _flash_fwd(proj, cum4, *, tb, name):
    T = proj.shape[0]
    D = N_HEADS * HEAD_DIM
    nb = T // tb
    cb = D // LANES

    def body(q_ref, k_ref, v_ref, g_ref, cum_ref, o_ref, og_ref, lp_ref, kb_ref, vb_ref):
        i = pl.program_id(1)

        @pl.when(i == 0)
        def _():
            kb_ref[...] = k_ref[...].astype(BF16)
            vb_ref[...] = v_ref[...].astype(BF16)

        q = q_ref[...] * (HEAD_DIM ** -0.5)
        masks = _head_masks(tb)
        outs = []
        for h in range(2):
            qh = jnp.where(masks[h], q, 0.0).astype(BF16)
            cref = cum_ref[0, h, pl.ds(i, 1), :][:, 0:1]

            def step(kbi, carry, masked, qh=qh, cref=cref, h=h):
                m, l, acc = carry
                k0 = pl.multiple_of(kbi * tb, tb)
                s = lax.dot_general(qh, kb_ref[pl.ds(k0, tb), :], (((1,), (1,)), ((), ())),
                                    preferred_element_type=F32)
                s = s + (cref - cum_ref[0, h, pl.ds(kbi, 1), :])
                if masked:
                    s = jnp.where(_causal(i, kbi, tb, tb), s, -jnp.inf)
                m_new = jnp.maximum(m, jnp.max(s, axis=-1, keepdims=True))
                alpha = jnp.exp(m - m_new)
                p = jnp.exp(s - m_new)
                l = alpha * l + jnp.sum(p, axis=-1, keepdims=True)
                acc = alpha * acc + jnp.dot(p.astype(BF16), vb_ref[pl.ds(k0, tb), :],
                                            preferred_element_type=F32)
                return m_new, l, acc

            init = (jnp.full((tb, 1), -jnp.inf, F32), jnp.zeros((tb, 1), F32),
                    jnp.zeros((tb, LANES), F32))
            carry = lax.fori_loop(0, i, lambda kbi, c: step(kbi, c, False), init)
            m, l, acc = step(i, carry, True)
            outs.append(acc / l)
            lp_ref[h] = jnp.broadcast_to(m + jnp.log(l) - cref, (tb, LANES))
        o = jnp.where(masks[0], outs[0], outs[1])
        o_ref[...] = o
        gate = g_ref[...]
        og_ref[...] = o * (gate * _sigmoid(gate))

    return pl.pallas_call(
        body, name=name, grid=(N_PAIRS, nb),
        in_specs=[pl.BlockSpec((tb, LANES), lambda j, i: (i, j)),
                  pl.BlockSpec((T, LANES), lambda j, i: (0, cb + j)),
                  pl.BlockSpec((T, LANES), lambda j, i: (0, 2 * cb + j)),
                  pl.BlockSpec((tb, LANES), lambda j, i: (i, 3 * cb + j)),
                  pl.BlockSpec((1, 2, nb, tb), lambda j, i: (j, 0, 0, 0))],
        out_specs=[pl.BlockSpec((tb, LANES), lambda j, i: (i, j)),
                   pl.BlockSpec((tb, LANES), lambda j, i: (i, j)),
                   pl.BlockSpec((2, tb, LANES), lambda j, i: (j, i, 0))],
        out_shape=[jax.ShapeDtypeStruct((T, D), F32), jax.ShapeDtypeStruct((T, D), F32),
                   jax.ShapeDtypeStruct((N_HEADS, T, LANES), F32)],
        scratch_shapes=[pltpu.VMEM((T, LANES), BF16), pltpu.VMEM((T, LANES), BF16)],
        compiler_params=_cparams("parallel", "arbitrary"),
    )(proj, proj, proj, proj, cum4)


def _flash_bwd_dq(proj, cum4, o, dog, lp, *, tb, name):
    T = proj.shape[0]
    D = N_HEADS * HEAD_DIM
    nb = T // tb
    cb = D // LANES

    def body(q_ref, k_ref, v_ref, g_ref, cum_ref, o_ref, dog_ref, lp_ref,
             dq_ref, dg_ref, do_ref, dl_ref, dc_ref, kb_ref, vb_ref):
        i = pl.program_id(1)

        @pl.when(i == 0)
        def _():
            kb_ref[...] = k_ref[...].astype(BF16)
            vb_ref[...] = v_ref[...].astype(BF16)

        gate = g_ref[...]
        sg = _sigmoid(gate)
        o = o_ref[...]
        dog = dog_ref[...]
        do = dog * (gate * sg)
        dg_ref[...] = dog * o * (sg * (1.0 + gate * (1.0 - sg)))
        do_ref[...] = do.astype(BF16)
        q = q_ref[...] * (HEAD_DIM ** -0.5)
        masks = _head_masks(tb)
        dqs = []
        for h in range(2):
            qh = jnp.where(masks[h], q, 0.0).astype(BF16)
            doh = jnp.where(masks[h], do, 0.0).astype(BF16)
            delta = jnp.sum(jnp.where(masks[h], do * o, 0.0), axis=-1, keepdims=True)
            dl_ref[h] = jnp.broadcast_to(delta, (tb, LANES))
            lph = lp_ref[h][:, 0:1]

            def step(kbi, carry, masked, qh=qh, doh=doh, delta=delta, lph=lph, h=h):
                acc, rs = carry
                k0 = pl.multiple_of(kbi * tb, tb)
                kblk = kb_ref[pl.ds(k0, tb), :]
                s = lax.dot_general(qh, kblk, (((1,), (1,)), ((), ())), preferred_element_type=F32)
                p = jnp.exp(s - cum_ref[0, h, pl.ds(kbi, 1), :] - lph)
                if masked:
                    p = jnp.where(_causal(i, kbi, tb, tb), p, 0.0)
                dp = lax.dot_general(doh, vb_ref[pl.ds(k0, tb), :], (((1,), (1,)), ((), ())),
                                     preferred_element_type=F32)
                ds = p * (dp - delta)
                return (acc + jnp.dot(ds.astype(BF16), kblk, preferred_element_type=F32),
                        rs + jnp.sum(ds, axis=-1, keepdims=True))

            carry = lax.fori_loop(0, i, lambda kbi, c: step(kbi, c, False),
                                  (jnp.zeros((tb, LANES), F32), jnp.zeros((tb, 1), F32)))
            acc, rs = step(i, carry, True)
            dqs.append(acc)
            dc_ref[0, 0, pl.ds(h, 1), :] = jnp.broadcast_to(rs, (tb, LANES)).T[0:1, :]
        dq_ref[...] = jnp.where(masks[0], dqs[0], dqs[1]) * (HEAD_DIM ** -0.5)

    blk = pl.BlockSpec((tb, LANES), lambda j, i: (i, j))
    stat = pl.BlockSpec((2, tb, LANES), lambda j, i: (j, i, 0))
    return pl.pallas_call(
        body, name=name, grid=(N_PAIRS, nb),
        in_specs=[blk,
                  pl.BlockSpec((T, LANES), lambda j, i: (0, cb + j)),
                  pl.BlockSpec((T, LANES), lambda j, i: (0, 2 * cb + j)),
                  pl.BlockSpec((tb, LANES), lambda j, i: (i, 3 * cb + j)),
                  pl.BlockSpec((1, 2, nb, tb), lambda j, i: (j, 0, 0, 0)),
                  blk, blk, stat],
        out_specs=[blk, blk, blk, stat, pl.BlockSpec((1, 1, 2, tb), lambda j, i: (j, i, 0, 0))],
        out_shape=[jax.ShapeDtypeStruct((T, D), F32), jax.ShapeDtypeStruct((T, D), F32),
                   jax.ShapeDtypeStruct((T, D), BF16),
                   jax.ShapeDtypeStruct((N_HEADS, T, LANES), F32),
                   jax.ShapeDtypeStruct((N_PAIRS, nb, 2, tb), F32)],
        scratch_shapes=[pltpu.VMEM((T, LANES), BF16), pltpu.VMEM((T, LANES), BF16)],
        compiler_params=_cparams("parallel", "arbitrary"),
    )(proj, proj, proj, proj, cum4, o, dog, lp)


def _flash_bwd_dkv(proj, cum4, do, lp, delta, *, tb, name):
    T = proj.shape[0]
    D = N_HEADS * HEAD_DIM
    nb = T // tb
    cb = D // LANES

    def body(q_ref, k_ref, v_ref, cum_ref, do_ref, lp_ref, dl_ref, dk_ref, dv_ref, dc_ref):
        kbi = pl.program_id(1)
        k = k_ref[...] * (HEAD_DIM ** -0.5)
        v = v_ref[...]
        masks = _head_masks(tb)
        dks, dvs = [], []
        for h in range(2):
            kh = jnp.where(masks[h], k, 0.0).astype(BF16)
            vh = jnp.where(masks[h], v, 0.0).astype(BF16)
            ck = cum_ref[0, h, pl.ds(kbi, 1), :]

            def step(i, carry, masked, kh=kh, vh=vh, ck=ck, h=h):
                dk, dv, dc = carry
                q0 = pl.multiple_of(i * tb, tb)
                qb = q_ref[pl.ds(q0, tb), :].astype(BF16)
                dob = do_ref[pl.ds(q0, tb), :]
                s = lax.dot_general(qb, kh, (((1,), (1,)), ((), ())), preferred_element_type=F32)
                p = jnp.exp(s - ck - lp_ref[h, pl.ds(q0, tb), :][:, 0:1])
                if masked:
                    p = jnp.where(_causal(i, kbi, tb, tb), p, 0.0)
                dp = lax.dot_general(dob, vh, (((1,), (1,)), ((), ())), preferred_element_type=F32)
                ds = p * (dp - dl_ref[h, pl.ds(q0, tb), :][:, 0:1])
                dv = dv + lax.dot_general(p.astype(BF16), dob, (((0,), (0,)), ((), ())),
                                          preferred_element_type=F32)
                dk = dk + lax.dot_general(ds.astype(BF16), qb, (((0,), (0,)), ((), ())),
                                          preferred_element_type=F32)
                dc = dc - jnp.sum(ds, axis=0, keepdims=True)
                return dk, dv, dc

            init = (jnp.zeros((tb, LANES), F32), jnp.zeros((tb, LANES), F32),
                    jnp.zeros((1, tb), F32))
            carry = step(kbi, init, True)
            dk, dv, dc = lax.fori_loop(kbi + 1, nb, lambda i, c: step(i, c, False), carry)
            dks.append(dk)
            dvs.append(dv)
            dc_ref[0, 0, pl.ds(h, 1), :] = dc
        dk_ref[...] = jnp.where(masks[0], dks[0], dks[1]) * (HEAD_DIM ** -0.5)
        dv_ref[...] = jnp.where(masks[0], dvs[0], dvs[1])

    full = pl.BlockSpec((T, LANES), lambda j, i: (0, j))
    stat = pl.BlockSpec((2, T, LANES), lambda j, i: (j, 0, 0))
    blk = pl.BlockSpec((tb, LANES), lambda j, i: (i, j))
    return pl.pallas_call(
        body, name=name, grid=(N_PAIRS, nb),
        in_specs=[full,
                  pl.BlockSpec((tb, LANES), lambda j, i: (i, cb + j)),
                  pl.BlockSpec((tb, LANES), lambda j, i: (i, 2 * cb + j)),
                  pl.BlockSpec((1, 2, nb, tb), lambda j, i: (j, 0, 0, 0)),
                  full, stat, stat],
        out_specs=[blk, blk, pl.BlockSpec((1, 1, 2, tb), lambda j, i: (j, i, 0, 0))],
        out_shape=[jax.ShapeDtypeStruct((T, D), F32), jax.ShapeDtypeStruct((T, D), F32),
                   jax.ShapeDtypeStruct((N_PAIRS, nb, 2, tb), F32)],
        compiler_params=_cparams("parallel", "arbitrary"),
    )(proj, proj, proj, cum4, do, lp, delta)


def _cumsum_fwd(proj, bf_row, *, tt, name):
    T = proj.shape[0]
    cb = (proj.shape[1] - LANES) // LANES

    def body(f_ref, b_ref, out_ref, carry_ref):
        i = pl.program_id(0)

        @pl.when(i == 0)
        def _():
            carry_ref[...] = jnp.zeros_like(carry_ref)

        ls = -_softplus(-(f_ref[...] + b_ref[...]))
        tri = (lax.broadcasted_iota(jnp.int32, (tt, tt), 0)
               >= lax.broadcasted_iota(jnp.int32, (tt, tt), 1)).astype(F32)
        cum = jnp.dot(tri, ls, preferred_element_type=F32,
                      precision=lax.Precision.HIGHEST) + carry_ref[...]
        carry_ref[...] = cum[tt - 1:tt, :]
        out_ref[...] = cum.T

    return pl.pallas_call(
        body, name=name, grid=(T // tt,),
        in_specs=[pl.BlockSpec((tt, LANES), lambda i: (i, cb)),
                  pl.BlockSpec((1, LANES), lambda i: (0, 0))],
        out_specs=pl.BlockSpec((LANES, tt), lambda i: (0, i)),
        out_shape=jax.ShapeDtypeStruct((LANES, T), F32),
        scratch_shapes=[pltpu.VMEM((1, LANES), F32)],
        compiler_params=_cparams("arbitrary"),
    )(proj, bf_row)


def _cumsum_bwd(dcum_t, proj, bf_row, *, tt, name):
    T = proj.shape[0]
    cb = (proj.shape[1] - LANES) // LANES
    nt = T // tt

    def body(dc_ref, f_ref, b_ref, df_ref, db_ref, carry_ref):
        i = pl.program_id(0)

        @pl.when(i == 0)
        def _():
            carry_ref[...] = jnp.zeros_like(carry_ref)
            db_ref[...] = jnp.zeros_like(db_ref)

        dc = dc_ref[...].T
        tri = (lax.broadcasted_iota(jnp.int32, (tt, tt), 0)
               <= lax.broadcasted_iota(jnp.int32, (tt, tt), 1)).astype(F32)
        rev = jnp.dot(tri, dc, preferred_element_type=F32,
                      precision=lax.Precision.HIGHEST) + carry_ref[...]
        carry_ref[...] = rev[0:1, :]
        df = rev * _sigmoid(-(f_ref[...] + b_ref[...]))
        df_ref[...] = df
        db_ref[...] += jnp.sum(df, axis=0, keepdims=True)

    return pl.pallas_call(
        body, name=name, grid=(nt,),
        in_specs=[pl.BlockSpec((LANES, tt), lambda i: (0, nt - 1 - i)),
                  pl.BlockSpec((tt, LANES), lambda i: (nt - 1 - i, cb)),
                  pl.BlockSpec((1, LANES), lambda i: (0, 0))],
        out_specs=[pl.BlockSpec((tt, LANES), lambda i: (nt - 1 - i, 0)),
                   pl.BlockSpec((1, LANES), lambda i: (0, 0))],
        out_shape=[jax.ShapeDtypeStruct((T, LANES), F32), jax.ShapeDtypeStruct((1, LANES), F32)],
        scratch_shapes=[pltpu.VMEM((1, LANES), F32)],
        compiler_params=_cparams("arbitrary"),
    )(dcum_t, proj, bf_row)


def _rg_gates(upad_ref, small_ref, wa_ref, wi_ref, tt):
    off = SUBLANES - (CONV_WIDTH - 1)
    u = small_ref[4:5, :]
    for tap in range(CONV_WIDTH):
        u = u + upad_ref[off + tap:off + tap + tt, :] * small_ref[tap:tap + 1, :]
    pa, pi = [], []
    for n in range(RNN_BLOCKS):
        ub = u[:, n * RNN_BLOCK_WIDTH:(n + 1) * RNN_BLOCK_WIDTH].astype(BF16)
        pa.append(jnp.dot(ub, wa_ref[n], preferred_element_type=F32))
        pi.append(jnp.dot(ub, wi_ref[n], preferred_element_type=F32))
    r = _sigmoid(jnp.concatenate(pa, axis=-1) + small_ref[5:6, :])
    ig = _sigmoid(jnp.concatenate(pi, axis=-1) + small_ref[6:7, :])
    spl = _softplus(-small_ref[7:8, :])
    log_a = (-LRU_C) * r * spl
    a = jnp.exp(log_a)
    s = jnp.sqrt(jnp.tanh(-log_a) * (a * a + 1.0))
    return u, r, ig, spl, a, s


def _rg_fwd(proj, small, wa, wi, *, tt, name):
    T = proj.shape[0]
    D = RNN_BLOCKS * RNN_BLOCK_WIDTH
    hb = tt // SUBLANES

    def body(u0_ref, halo_ref, g_ref, small_ref, wa_ref, wi_ref, h_ref, y_ref,
             upad_ref, a_ref, b_ref, carry_ref):
        i = pl.program_id(0)

        @pl.when(i == 0)
        def _():
            carry_ref[...] = jnp.zeros_like(carry_ref)

        upad_ref[0:SUBLANES, :] = jnp.where(i == 0, 0.0, halo_ref[...])
        upad_ref[SUBLANES:, :] = u0_ref[...]
        u, r, ig, spl, a, s = _rg_gates(upad_ref, small_ref, wa_ref, wi_ref, tt)
        a_ref[...] = a
        b_ref[...] = s * (ig * u)

        def row(t, h):
            h = a_ref[pl.ds(t, 1), :] * h + b_ref[pl.ds(t, 1), :]
            h_ref[pl.ds(t, 1), :] = h
            return h

        carry_ref[...] = lax.fori_loop(0, tt, row, carry_ref[...], unroll=8)
        gate = g_ref[...]
        y_ref[...] = h_ref[...] * (gate * _sigmoid(gate))

    return pl.pallas_call(
        body, name=name, grid=(T // tt,),
        in_specs=[pl.BlockSpec((tt, D), lambda i: (i, 0)),
                  pl.BlockSpec((SUBLANES, D), lambda i: (jnp.maximum(i * hb - 1, 0), 0)),
                  pl.BlockSpec((tt, D), lambda i: (i, 1)),
                  pl.BlockSpec((SUBLANES, D), lambda i: (0, 0)),
                  pl.BlockSpec((RNN_BLOCKS, RNN_BLOCK_WIDTH, RNN_BLOCK_WIDTH), lambda i: (0, 0, 0)),
                  pl.BlockSpec((RNN_BLOCKS, RNN_BLOCK_WIDTH, RNN_BLOCK_WIDTH), lambda i: (0, 0, 0))],
        out_specs=[pl.BlockSpec((tt, D), lambda i: (i, 0)), pl.BlockSpec((tt, D), lambda i: (i, 0))],
        out_shape=[jax.ShapeDtypeStruct((T, D), F32), jax.ShapeDtypeStruct((T, D), F32)],
        scratch_shapes=[pltpu.VMEM((tt + SUBLANES, D), F32), pltpu.VMEM((tt, D), F32),
                        pltpu.VMEM((tt, D), F32), pltpu.VMEM((1, D), F32)],
        compiler_params=_cparams("arbitrary"),
    )(proj, proj, proj, small, wa, wi)


def _rg_bwd(proj, hs, dy, small, wa, wi, *, tt, name):
    T = proj.shape[0]
    D = RNN_BLOCKS * RNN_BLOCK_WIDTH
    W = RNN_BLOCK_WIDTH
    hb = tt // SUBLANES
    nt = T // tt

    def body(u0_ref, uhalo_ref, g_ref, h_ref, hhalo_ref, dy_ref, small_ref, wa_ref, wi_ref,
             dp_ref, dwa_ref, dwi_ref, ds_ref,
             upad_ref, hpad_ref, a_ref, g_s_ref, duext_ref, carry_ref):
        i = pl.program_id(0)
        first_chunk = i == nt - 1

        @pl.when(i == 0)
        def _():
            carry_ref[...] = jnp.zeros_like(carry_ref)
            duext_ref[...] = jnp.zeros_like(duext_ref)
            dwa_ref[...] = jnp.zeros_like(dwa_ref)
            dwi_ref[...] = jnp.zeros_like(dwi_ref)
            ds_ref[...] = jnp.zeros_like(ds_ref)

        upad_ref[0:SUBLANES, :] = jnp.where(first_chunk, 0.0, uhalo_ref[...])
        upad_ref[SUBLANES:, :] = u0_ref[...]
        hpad_ref[0:SUBLANES, :] = jnp.where(first_chunk, 0.0, hhalo_ref[...])
        hpad_ref[SUBLANES:, :] = h_ref[...]
        u, r, ig, spl, a, s = _rg_gates(upad_ref, small_ref, wa_ref, wi_ref, tt)
        gate = g_ref[...]
        sg = _sigmoid(gate)
        dy = dy_ref[...]
        dp_ref[:, D:] = dy * h_ref[...] * (sg * (1.0 + gate * (1.0 - sg)))
        a_ref[...] = a
        g_s_ref[...] = dy * (gate * sg)

        def row(k, c):
            t = tt - 1 - k
            g = g_s_ref[pl.ds(t, 1), :] + c
            g_s_ref[pl.ds(t, 1), :] = g
            return a_ref[pl.ds(t, 1), :] * g

        carry_ref[...] = lax.fori_loop(0, tt, row, carry_ref[...], unroll=8)
        g = g_s_ref[...]
        h_prev = hpad_ref[SUBLANES - 1:SUBLANES - 1 + tt, :]
        iu = ig * u
        d_iu = g * s
        dlog_a = (g * h_prev) * a - (g * iu) * (a * a) / s
        dpre_a = (dlog_a * ((-LRU_C) * spl)) * r * (1.0 - r)
        dpre_i = (d_iu * u) * ig * (1.0 - ig)
        dlam = jnp.sum(dlog_a * r, axis=0, keepdims=True) * (LRU_C * _sigmoid(-small_ref[7:8, :]))
        du_parts = []
        for n in range(RNN_BLOCKS):
            sl = slice(n * W, (n + 1) * W)
            ub = u[:, sl].astype(BF16)
            da_n = dpre_a[:, sl].astype(BF16)
            di_n = dpre_i[:, sl].astype(BF16)
            dwa_ref[n] += lax.dot_general(ub, da_n, (((0,), (0,)), ((), ())),
                                          preferred_element_type=F32)
            dwi_ref[n] += lax.dot_general(ub, di_n, (((0,), (0,)), ((), ())),
                                          preferred_element_type=F32)
            du_parts.append(
                lax.dot_general(da_n, wa_ref[n], (((1,), (1,)), ((), ())), preferred_element_type=F32)
                + lax.dot_general(di_n, wi_ref[n], (((1,), (1,)), ((), ())), preferred_element_type=F32))
        du = d_iu * ig + jnp.concatenate(du_parts, axis=-1)
        off = SUBLANES - (CONV_WIDTH - 1)
        for tap in range(CONV_WIDTH):
            ds_ref[tap:tap + 1, :] += jnp.sum(du * upad_ref[off + tap:off + tap + tt, :],
                                              axis=0, keepdims=True)
        ds_ref[4:5, :] += jnp.sum(du, axis=0, keepdims=True)
        ds_ref[5:6, :] += jnp.sum(dpre_a, axis=0, keepdims=True)
        ds_ref[6:7, :] += jnp.sum(dpre_i, axis=0, keepdims=True)
        ds_ref[7:8, :] += dlam
        duext_ref[0:tt, :] = du
        du0 = jnp.zeros((tt, D), F32)
        for tap in range(CONV_WIDTH):
            sh = CONV_WIDTH - 1 - tap
            du0 = du0 + duext_ref[sh:sh + tt, :] * small_ref[tap:tap + 1, :]
        dp_ref[:, :D] = du0
        duext_ref[tt:, :] = du[0:SUBLANES, :]

    rev = lambda i: nt - 1 - i
    wspec = pl.BlockSpec((RNN_BLOCKS, W, W), lambda i: (0, 0, 0))
    return pl.pallas_call(
        body, name=name, grid=(nt,),
        in_specs=[pl.BlockSpec((tt, D), lambda i: (rev(i), 0)),
                  pl.BlockSpec((SUBLANES, D), lambda i: (jnp.maximum(rev(i) * hb - 1, 0), 0)),
                  pl.BlockSpec((tt, D), lambda i: (rev(i), 1)),
                  pl.BlockSpec((tt, D), lambda i: (rev(i), 0)),
                  pl.BlockSpec((SUBLANES, D), lambda i: (jnp.maximum(rev(i) * hb - 1, 0), 0)),
                  pl.BlockSpec((tt, D), lambda i: (rev(i), 0)),
                  pl.BlockSpec((SUBLANES, D), lambda i: (0, 0)),
                  wspec, wspec],
        out_specs=[pl.BlockSpec((tt, 2 * D), lambda i: (rev(i), 0)),
                   wspec, wspec, pl.BlockSpec((SUBLANES, D), lambda i: (0, 0))],
        out_shape=[jax.ShapeDtypeStruct((T, 2 * D), F32),
                   jax.ShapeDtypeStruct((RNN_BLOCKS, W, W), F32),
                   jax.ShapeDtypeStruct((RNN_BLOCKS, W, W), F32),
                   jax.ShapeDtypeStruct((SUBLANES, D), F32)],
        scratch_shapes=[pltpu.VMEM((tt + SUBLANES, D), F32), pltpu.VMEM((tt + SUBLANES, D), F32),
                        pltpu.VMEM((tt, D), F32), pltpu.VMEM((tt, D), F32),
                        pltpu.VMEM((tt + SUBLANES, D), F32), pltpu.VMEM((1, D), F32)],
        compiler_params=_cparams("arbitrary"),
    )(proj, proj, proj, hs, hs, dy, small, wa, wi)


def _ln_fwd(x, h, g, b, *, tt, name):
    T, D = x.shape

    def body(x_ref, h_ref, g_ref, b_ref, y_ref, zh_ref, rs_ref):
        z = ALPHA * x_ref[...] + h_ref[...]
        mu = jnp.mean(z, axis=-1, keepdims=True)
        zc = z - mu
        rstd = lax.rsqrt(jnp.mean(zc * zc, axis=-1, keepdims=True) + LN_EPS)
        zh = zc * rstd
        zh_ref[...] = zh
        rs_ref[...] = rstd
        y_ref[...] = zh * g_ref[...] + b_ref[...]

    blk = pl.BlockSpec((tt, D), lambda i: (i, 0))
    row = pl.BlockSpec((1, D), lambda i: (0, 0))
    return pl.pallas_call(
        body, name=name, grid=(T // tt,),
        in_specs=[blk, blk, row, row],
        out_specs=[blk, blk, pl.BlockSpec((tt, 1), lambda i: (i, 0))],
        out_shape=[jax.ShapeDtypeStruct((T, D), F32), jax.ShapeDtypeStruct((T, D), F32),
                   jax.ShapeDtypeStruct((T, 1), F32)],
        compiler_params=_cparams("parallel"),
    )(x, h, g, b)


def _ln_bwd(dy, zh, rstd, g, *, tt, name):
    T, D = dy.shape

    def body(dy_ref, zh_ref, rs_ref, g_ref, dz_ref, dg_ref, db_ref):
        @pl.when(pl.program_id(0) == 0)
        def _():
            dg_ref[...] = jnp.zeros_like(dg_ref)
            db_ref[...] = jnp.zeros_like(db_ref)

        dy = dy_ref[...]
        zh = zh_ref[...]
        dg_ref[...] += jnp.sum(dy * zh, axis=0, keepdims=True)
        db_ref[...] += jnp.sum(dy, axis=0, keepdims=True)
        dzh = dy * g_ref[...]
        m1 = jnp.mean(dzh, axis=-1, keepdims=True)
        m2 = jnp.mean(dzh * zh, axis=-1, keepdims=True)
        dz_ref[...] = rs_ref[...] * (dzh - m1 - zh * m2)

    blk = pl.BlockSpec((tt, D), lambda i: (i, 0))
    row = pl.BlockSpec((1, D), lambda i: (0, 0))
    return pl.pallas_call(
        body, name=name, grid=(T // tt,),
        in_specs=[blk, blk, pl.BlockSpec((tt, 1), lambda i: (i, 0)), row],
        out_specs=[blk, row, row],
        out_shape=[jax.ShapeDtypeStruct((T, D), F32), jax.ShapeDtypeStruct((1, D), F32),
                   jax.ShapeDtypeStruct((1, D), F32)],
        compiler_params=_cparams("arbitrary"),
    )(dy, zh, rstd, g)


def _loss(y, tgt, *, tt, name):
    T, D = y.shape

    def body(y_ref, t_ref, l_ref, dy_ref):
        @pl.when(pl.program_id(0) == 0)
        def _():
            l_ref[...] = jnp.zeros_like(l_ref)

        e = y_ref[...] - t_ref[...]
        dy_ref[...] = e * (1.0 / D)
        l_ref[...] += jnp.sum(e * e, axis=0, keepdims=True) * (0.5 / D)

    blk = pl.BlockSpec((tt, D), lambda i: (i, 0))
    return pl.pallas_call(
        body, name=name, grid=(T // tt,),
        in_specs=[blk, blk], out_specs=[pl.BlockSpec((1, D), lambda i: (0, 0)), blk],
        out_shape=[jax.ShapeDtypeStruct((1, D), F32), jax.ShapeDtypeStruct((T, D), F32)],
        compiler_params=_cparams("arbitrary"),
    )(y, tgt)


def _row_tile(rows, target):
    best = SUBLANES
    for t in range(SUBLANES, target + 1, SUBLANES):
        if rows % t == 0:
            best = t
    return best


def _add2(parts, *, tr, name):
    _, R, C = parts.shape

    def body(p_ref, o_ref):
        o_ref[...] = p_ref[0] + p_ref[1]

    return pl.pallas_call(
        body, name=name, grid=(R // tr,),
        in_specs=[pl.BlockSpec((2, tr, C), lambda i: (0, i, 0))],
        out_specs=pl.BlockSpec((tr, C), lambda i: (i, 0)),
        out_shape=jax.ShapeDtypeStruct((R, C), F32),
        compiler_params=_cparams("parallel"),
    )(parts)


def _adamw(parts, w, m, v, *, tr, name):
    n, R, C = parts.shape
    tr = min(tr, R)

    def body(p_ref, w_ref, m_ref, v_ref, g_ref, d_ref, nm_ref, nv_ref):
        g = p_ref[0]
        for k in range(1, n):
            g = g + p_ref[k]
        nm = ADAM_B1 * m_ref[...] + (1.0 - ADAM_B1) * g
        nv = ADAM_B2 * v_ref[...] + (1.0 - ADAM_B2) * (g * g)
        m_hat = nm / (1.0 - ADAM_B1 ** ADAM_STEP)
        v_hat = nv / (1.0 - ADAM_B2 ** ADAM_STEP)
        g_ref[...] = g
        nm_ref[...] = nm
        nv_ref[...] = nv
        d_ref[...] = (-ADAM_LR) * (m_hat / (jnp.sqrt(v_hat) + ADAM_EPS) + ADAM_WD * w_ref[...])

    blk = pl.BlockSpec((tr, C), lambda i: (i, 0))
    out = jax.ShapeDtypeStruct((R, C), F32)
    return pl.pallas_call(
        body, name=name, grid=(R // tr,),
        in_specs=[pl.BlockSpec((n, tr, C), lambda i: (0, i, 0)), blk, blk, blk],
        out_specs=[blk, blk, blk, blk], out_shape=[out, out, out, out],
        compiler_params=_cparams("parallel"),
    )(parts, w, m, v)


BIG = ("attn_w_in", "attn_w_out", "rnn_w_in", "rnn_w_a", "rnn_w_i", "rnn_w_out")
SMALL = ("rnn_conv_w", "rnn_conv_b", "rnn_b_a", "rnn_b_i", "rnn_lambda")
PACK_C = 1024


def _rows(shape):
    n = 1
    for s in shape:
        n *= s
    assert n % PACK_C == 0
    return n // PACK_C


def _pack_local(p, dtype, pad_to):
    parts = [p[k].astype(dtype).reshape(-1, PACK_C) for k in BIG]
    small = jnp.concatenate([p[k].reshape(-1) for k in SMALL])
    if dtype == BF16:
        small = lax.bitcast_convert_type(small, BF16)
    parts.append(small.reshape(-1, PACK_C))
    flat = jnp.concatenate(parts, axis=0)
    pad = (-flat.shape[0]) % pad_to
    return jnp.pad(flat, ((0, pad), (0, 0)))


def _unpack_local(flat, shapes):
    out, r = {}, 0
    for k in BIG:
        n = _rows(shapes[k])
        out[k] = flat[r:r + n].reshape(shapes[k])
        r += n
    n_small = sum(_rows_elems(shapes[k]) for k in SMALL)
    small = flat[r:r + n_small // PACK_C].reshape(-1)
    o = 0
    for k in SMALL:
        n = _rows_elems(shapes[k])
        out[k] = small[o:o + n].reshape(shapes[k])
        o += n
    return out


def _to_full(g, k, sh):
    nd = len(sh)
    if k in ("attn_w_in", "rnn_w_in", "rnn_conv_w"):
        perm = tuple(range(2, 2 + nd - 1)) + (1, 0, 2 + nd - 1)
        t = g.transpose(perm)
        return t.reshape(sh[:-1] + (8 * sh[-1],))
    if k in ("attn_w_out", "rnn_w_out", "rnn_conv_b", "rnn_b_a", "rnn_b_i", "rnn_lambda"):
        perm = (2, 1, 0) + tuple(range(3, 2 + nd))
        t = g.transpose(perm)
        return t.reshape((sh[0], 8 * sh[1]) + sh[2:])
    perm = (2, 3, 1, 0, 4, 5)
    t = g.transpose(perm)
    return t.reshape((sh[0], sh[1], 8 * sh[2], sh[3]))


def _from_full(full, k, sh):
    nd = len(sh)
    if k in ("attn_w_in", "rnn_w_in", "rnn_conv_w"):
        t = full.reshape(sh[:-1] + (4, 2, sh[-1]))
        perm = (nd, nd - 1) + tuple(range(nd - 1)) + (nd + 1,)
        return t.transpose(perm)
    if k in ("attn_w_out", "rnn_w_out", "rnn_conv_b", "rnn_b_a", "rnn_b_i", "rnn_lambda"):
        t = full.reshape((sh[0], 4, 2) + sh[1:])
        perm = (2, 1, 0) + tuple(range(3, 2 + nd))
        return t.transpose(perm)
    t = full.reshape((sh[0], sh[1], 4, 2, sh[2], sh[3]))
    return t.transpose((3, 2, 0, 1, 4, 5))


def _unpack_gathered(g, shapes):
    out, r = {}, 0
    for k in BIG:
        n = _rows(shapes[k])
        out[k] = _to_full(g[:, :, r:r + n].reshape((2, 4) + shapes[k]), k, shapes[k])
        r += n
    n_small = sum(_rows_elems(shapes[k]) for k in SMALL)
    nr = 2 * n_small // PACK_C
    small = lax.bitcast_convert_type(g[:, :, r:r + nr].reshape(2, 4, n_small, 2), F32)
    o = 0
    for k in SMALL:
        n = _rows_elems(shapes[k])
        out[k] = _to_full(small[:, :, o:o + n].reshape((2, 4) + shapes[k]), k, shapes[k])
        o += n
    return out


def _rows_elems(shape):
    n = 1
    for s in shape:
        n *= s
    return n


def _pack_grads(full, shapes, pad_to):
    parts = [_from_full(full[k], k, shapes[k]).reshape(2, 4, -1, PACK_C) for k in BIG]
    small = jnp.concatenate(
        [_from_full(full[k], k, shapes[k]).reshape(2, 4, -1) for k in SMALL], axis=-1)
    parts.append(small.reshape(2, 4, -1, PACK_C))
    flat = jnp.concatenate(parts, axis=2)
    pad = (-flat.shape[2]) % pad_to
    return jnp.pad(flat, ((0, 0), (0, 0), (0, pad), (0, 0)))


def kernel(x, ln_g, ln_b, attn_w_in, attn_b_f, attn_w_out, rnn_w_in, rnn_conv_w, rnn_conv_b, rnn_w_a, rnn_b_a, rnn_w_i, rnn_b_i, rnn_lambda, rnn_w_out, loss_target, m_ln_g, m_ln_b, m_attn_w_in, m_attn_b_f, m_attn_w_out, m_rnn_w_in, m_rnn_conv_w, m_rnn_conv_b, m_rnn_w_a, m_rnn_b_a, m_rnn_w_i, m_rnn_b_i, m_rnn_lambda, m_rnn_w_out, v_ln_g, v_ln_b, v_attn_w_in, v_attn_b_f, v_attn_w_out, v_rnn_w_in, v_rnn_conv_w, v_rnn_conv_b, v_rnn_w_a, v_rnn_b_a, v_rnn_w_i, v_rnn_b_i, v_rnn_lambda, v_rnn_w_out):
    w_loc = dict(attn_w_in=attn_w_in, attn_w_out=attn_w_out, rnn_w_in=rnn_w_in, rnn_w_a=rnn_w_a,
                 rnn_w_i=rnn_w_i, rnn_w_out=rnn_w_out, rnn_conv_w=rnn_conv_w, rnn_conv_b=rnn_conv_b,
                 rnn_b_a=rnn_b_a, rnn_b_i=rnn_b_i, rnn_lambda=rnn_lambda)
    m_loc = dict(attn_w_in=m_attn_w_in, attn_w_out=m_attn_w_out, rnn_w_in=m_rnn_w_in,
                 rnn_w_a=m_rnn_w_a, rnn_w_i=m_rnn_w_i, rnn_w_out=m_rnn_w_out,
                 rnn_conv_w=m_rnn_conv_w, rnn_conv_b=m_rnn_conv_b, rnn_b_a=m_rnn_b_a,
                 rnn_b_i=m_rnn_b_i, rnn_lambda=m_rnn_lambda)
    v_loc = dict(attn_w_in=v_attn_w_in, attn_w_out=v_attn_w_out, rnn_w_in=v_rnn_w_in,
                 rnn_w_a=v_rnn_w_a, rnn_w_i=v_rnn_w_i, rnn_w_out=v_rnn_w_out,
                 rnn_conv_w=v_rnn_conv_w, rnn_conv_b=v_rnn_conv_b, rnn_b_a=v_rnn_b_a,
                 rnn_b_i=v_rnn_b_i, rnn_lambda=v_rnn_lambda)
    shapes = {k: tuple(a.shape) for k, a in w_loc.items()}
    T, D = x.shape[1], x.shape[2]
    n_f = attn_b_f.shape[1]
    tb = min(512, T)
    tt_rg = min(128, T)
    tt_ln = min(256, T)

    gathered = _all_gather(_pack_local(w_loc, BF16, 16), "ag_w")
    W = _unpack_gathered(gathered, shapes)
    w_in_a = jnp.pad(W["attn_w_in"], ((0, 0), (0, 0), (0, LANES - n_f)))
    small_r = jnp.concatenate([W["rnn_conv_w"], W["rnn_conv_b"][:, None], W["rnn_b_a"][:, None],
                               W["rnn_b_i"][:, None], W["rnn_lambda"][:, None]], axis=1)
    bf_rows = jnp.pad(attn_b_f, ((0, 0), (0, LANES - n_f)))[:, None, :]

    xs, saved = [x[0]], []
    for layer in range(DEPTH):
        idx, xl = layer // 2, xs[-1]
        if layer % 2 == 0:
            proj = _matmul(xl, w_in_a[idx], trans_b=False, tm=512, tn=1408, name=f"a_proj{layer}")
            cum_t = _cumsum_fwd(proj, bf_rows[idx], tt=min(512, T), name=f"a_cum{layer}")
            cum4 = cum_t[:N_HEADS].reshape(N_PAIRS, 2, T // tb, tb)
            o, og, lp = _flash_fwd(proj, cum4, tb=tb, name=f"a_fwd{layer}")
            hbr = _matmul(og, W["attn_w_out"][idx], trans_b=False, tm=512, tn=1024,
                          name=f"a_out{layer}")
            saved.append((proj, cum4, o, og, lp))
        else:
            proj = _matmul(xl, W["rnn_w_in"][idx], trans_b=False, tm=512, tn=1024,
                           name=f"r_proj{layer}")
            hs, yr = _rg_fwd(proj, small_r[idx], W["rnn_w_a"][idx], W["rnn_w_i"][idx],
                             tt=tt_rg, name=f"r_fwd{layer}")
            hbr = _matmul(yr, W["rnn_w_out"][idx], trans_b=False, tm=512, tn=1024,
                          name=f"r_out{layer}")
            saved.append((proj, hs, yr))
        y, zh, rstd = _ln_fwd(xl, hbr, ln_g[layer][None], ln_b[layer][None], tt=tt_ln,
                              name=f"ln_fwd{layer}")
        saved[-1] = saved[-1] + (zh, rstd)
        xs.append(y)

    loss_lanes, dy = _loss(xs[-1], loss_target[0], tt=tt_ln, name="loss")
    loss = lax.psum(jnp.sum(loss_lanes), ("x", "y", "c"))

    full_g = {k: [None, None] for k in w_loc}
    d_ln_g, d_ln_b, d_bf = [None] * DEPTH, [None] * DEPTH, [None, None]
    for layer in reversed(range(DEPTH)):
        idx, xl = layer // 2, xs[layer]
        zh, rstd = saved[layer][-2:]
        dz, dg, db = _ln_bwd(dy, zh, rstd, ln_g[layer][None], tt=tt_ln, name=f"ln_bwd{layer}")
        d_ln_g[layer], d_ln_b[layer] = dg[0], db[0]
        if layer % 2 == 0:
            proj, cum4, o, og, lp = saved[layer][:5]
            dog = _matmul(dz, W["attn_w_out"][idx], trans_b=True, tm=512, tn=1024,
                          name=f"a_dog{layer}")
            full_g["attn_w_out"][idx] = _matmul_tn(og, dz, tm=512, tn=1024, tk=512,
                                                   name=f"a_dwo{layer}")
            dq, dgate, do, delta, dcum_q = _flash_bwd_dq(proj, cum4, o, dog, lp, tb=tb,
                                                         name=f"a_dq{layer}")
            dk, dv, dcum_k = _flash_bwd_dkv(proj, cum4, do, lp, delta, tb=tb,
                                            name=f"a_dkv{layer}")
            dcum_t = (dcum_q + dcum_k).transpose(0, 2, 1, 3).reshape(N_HEADS, T)
            dcum_t = jnp.pad(dcum_t, ((0, LANES - N_HEADS), (0, 0)))
            df, dbf = _cumsum_bwd(dcum_t, proj, bf_rows[idx], tt=min(512, T), name=f"a_dcum{layer}")
            d_bf[idx] = dbf[0, :n_f]
            dproj = jnp.concatenate([dq, dk, dv, dgate, df], axis=1)
            dwi = _matmul_tn(xl, dproj, tm=512, tn=1408, tk=512, name=f"a_dwi{layer}")
            full_g["attn_w_in"][idx] = dwi[:, :4 * D + n_f]
            dy = _matmul(dproj, w_in_a[idx], trans_b=True, tm=256, tn=512, name=f"a_dx{layer}",
                         add=dz, add_scale=ALPHA)
        else:
            proj, hs, yr = saved[layer][:3]
            dyr = _matmul(dz, W["rnn_w_out"][idx], trans_b=True, tm=512, tn=1024,
                          name=f"r_dy{layer}")
            full_g["rnn_w_out"][idx] = _matmul_tn(yr, dz, tm=512, tn=1024, tk=512,
                                                  name=f"r_dwo{layer}")
            dproj, dwa, dwi_, dsm = _rg_bwd(proj, hs, dyr, small_r[idx], W["rnn_w_a"][idx],
                                            W["rnn_w_i"][idx], tt=tt_rg, name=f"r_bwd{layer}")
            full_g["rnn_w_a"][idx], full_g["rnn_w_i"][idx] = dwa, dwi_
            full_g["rnn_conv_w"][idx] = dsm[0:4]
            for r, k in enumerate(("rnn_conv_b", "rnn_b_a", "rnn_b_i", "rnn_lambda")):
                full_g[k][idx] = dsm[4 + r]
            full_g["rnn_w_in"][idx] = _matmul_tn(xl, dproj, tm=512, tn=1024, tk=512,
                                                 name=f"r_dwi{layer}")
            dy = _matmul(dproj, W["rnn_w_in"][idx], trans_b=True, tm=512, tn=512,
                         name=f"r_dx{layer}", add=dz, add_scale=ALPHA)
    grad_x = dy[None]

    G = _pack_grads({k: jnp.stack(v) for k, v in full_g.items()}, shapes, 8)
    R = G.shape[2]
    pair = _a2a(G.reshape(2, 4 * R, PACK_C), group="c", bcast=False, name="rs_c")
    H = _add2(pair, tr=_row_tile(4 * R, 1024), name="rs_add")
    quad = _a2a(H.reshape(4, R, PACK_C), group="xy", bcast=False, name="rs_xy")
    g_f, d_f, nm_f, nv_f = _adamw(quad, _pack_local(w_loc, F32, 8), _pack_local(m_loc, F32, 8),
                                  _pack_local(v_loc, F32, 8), tr=_row_tile(R, 320), name="adamw")
    g_sh, d_sh = _unpack_local(g_f, shapes), _unpack_local(d_f, shapes)
    nm_sh, nv_sh = _unpack_local(nm_f, shapes), _unpack_local(nv_f, shapes)

    def rep_pack(lg, lb, bf):
        rows = jnp.concatenate([lg, lb, jnp.pad(bf.reshape(1, -1), ((0, 0), (0, D - 2 * n_f)))])
        return jnp.pad(rows, ((0, 16 - rows.shape[0]), (0, 0)))

    rep = _all_gather(rep_pack(jnp.stack(d_ln_g), jnp.stack(d_ln_b), jnp.stack(d_bf)), "ag_rep")
    rg, rd, rm, rv = _adamw(rep.reshape(8, 16, D), rep_pack(ln_g, ln_b, attn_b_f),
                            rep_pack(m_ln_g, m_ln_b, m_attn_b_f),
                            rep_pack(v_ln_g, v_ln_b, v_attn_b_f), tr=16, name="adamw_rep")

    def rep_unpack(a):
        return dict(ln_g=a[0:DEPTH], ln_b=a[DEPTH:2 * DEPTH],
                    attn_b_f=a[2 * DEPTH, :2 * n_f].reshape(2, n_f))

    order = ("ln_g", "ln_b", "attn_w_in", "attn_b_f", "attn_w_out", "rnn_w_in", "rnn_conv_w",
             "rnn_conv_b", "rnn_w_a", "rnn_b_a", "rnn_w_i", "rnn_b_i", "rnn_lambda", "rnn_w_out")
    outs = [loss, grad_x]
    for sh, rp in ((g_sh, rg), (d_sh, rd), (nm_sh, rm), (nv_sh, rv)):
        allp = {**sh, **rep_unpack(rp)}
        outs.extend(allp[k] for k in order)
    return tuple(outs)
```

```python
import functools

import jax
import jax.numpy as jnp
from jax import lax
from jax.experimental import pallas as pl
from jax.experimental.pallas import tpu as pltpu

F32 = jnp.float32
BF16 = jnp.bfloat16

DEPTH = 4
N_HEADS = 16
HEAD_DIM = 64
N_PAIRS = N_HEADS // 2
RNN_BLOCKS = 4
RNN_BLOCK_WIDTH = 256
CONV_WIDTH = 4
LRU_C = 8.0
ALPHA = (2.0 * DEPTH) ** 0.25
LN_EPS = 1e-5
ADAM_LR, ADAM_B1, ADAM_B2, ADAM_EPS, ADAM_WD, ADAM_STEP = 0.001, 0.9, 0.999, 1e-8, 0.01, 10

LANES = 128
SUBLANES = 8
VMEM_LIMIT = 48 * 1024 * 1024

MESH = pl.DeviceIdType.MESH
HBM_SPEC = pl.BlockSpec(memory_space=pltpu.HBM)


def _cparams(*sem):
    return pltpu.CompilerParams(dimension_semantics=sem, vmem_limit_bytes=VMEM_LIMIT)


def _sigmoid(x):
    return 1.0 / (1.0 + jnp.exp(-x))


def _softplus(x):
    return jnp.maximum(x, 0.0) + jnp.log(1.0 + jnp.exp(-jnp.abs(x)))


def _a2a(src, *, group, bcast, name):
    n = 2 if group == "c" else 4
    blk = tuple(src.shape) if bcast else tuple(src.shape[1:])

    def body(src_ref, out_ref, send_sems, recv_sems, local_sem):
        x, y, c = lax.axis_index("x"), lax.axis_index("y"), lax.axis_index("c")
        if group == "c":
            me = c

            def peer(d):
                return (x, y, 1 - c), 1 - c
        else:
            me = 2 * x + y

            def peer(d):
                px, py = x ^ (d >> 1), y ^ (d & 1)
                return (px, py, c), 2 * px + py

        def block_for(k):
            return src_ref if bcast else src_ref.at[k]

        local = pltpu.make_async_copy(block_for(me), out_ref.at[me], local_sem)
        local.start()
        sends = []
        for d in range(1, n):
            dev, idx = peer(d)
            cp = pltpu.make_async_remote_copy(
                src_ref=block_for(idx), dst_ref=out_ref.at[me],
                send_sem=send_sems.at[d], recv_sem=recv_sems.at[d],
                device_id=dev, device_id_type=MESH)
            cp.start()
            sends.append(cp)
        for d in range(1, n):
            dev, idx = peer(d)
            pltpu.make_async_remote_copy(
                src_ref=block_for(idx), dst_ref=out_ref.at[idx],
                send_sem=send_sems.at[d], recv_sem=recv_sems.at[d],
                device_id=dev, device_id_type=MESH).wait_recv()
        for cp in sends:
            cp.wait_send()
        local.wait()

    return pl.pallas_call(
        body, name=name,
        out_shape=jax.ShapeDtypeStruct((n,) + blk, src.dtype),
        in_specs=[HBM_SPEC], out_specs=HBM_SPEC,
        scratch_shapes=[pltpu.SemaphoreType.DMA((n,)), pltpu.SemaphoreType.DMA((n,)),
                        pltpu.SemaphoreType.DMA],
    )(src)


def _all_gather(piece, name):
    return _a2a(_a2a(piece, group="xy", bcast=True, name=name + "_xy"),
                group="c", bcast=True, name=name + "_c")


D2D_CHUNKS = 16
ICI_CHUNKS = 8


def _row_chunks(rows, dtype, k):
    unit = SUBLANES * (4 // jnp.dtype(dtype).itemsize)
    assert rows % unit == 0
    units = rows // unit
    k = min(k, units)
    base, rem = divmod(units, k)
    out, r = [], 0
    for i in range(k):
        n = (base + (1 if i < rem else 0)) * unit
        out.append((r, n))
        r += n
    return out


def _mesh_place():
    x, y, c = lax.axis_index("x"), lax.axis_index("y"), lax.axis_index("c")
    return x, y, c, 2 * x + y


def _chip_peer(x, y, c, d):
    px, py = x ^ (d >> 1), y ^ (d & 1)
    return (px, py, c), 2 * px + py


def _remote(src, dst, send_sem, recv_sem, dev):
    return pltpu.make_async_remote_copy(src_ref=src, dst_ref=dst, send_sem=send_sem,
                                        recv_sem=recv_sem, device_id=dev, device_id_type=MESH)


def _ag_xy(piece, name):
    R, C = piece.shape
    chunks = _row_chunks(R, piece.dtype, ICI_CHUNKS)

    def body(src_ref, out_ref, send_sems, recv_sems, local_sem):
        x, y, c, me = _mesh_place()
        for r0, n in chunks:
            pltpu.make_async_copy(src_ref.at[pl.ds(r0, n)], out_ref.at[c, me, pl.ds(r0, n)],
                                  local_sem).start()
        for d in range(1, 4):
            dev, _ = _chip_peer(x, y, c, d)
            for r0, n in chunks:
                _remote(src_ref.at[pl.ds(r0, n)], out_ref.at[c, me, pl.ds(r0, n)],
                        send_sems.at[d], recv_sems.at[d], dev).start()
        for d in range(1, 4):
            dev, idx = _chip_peer(x, y, c, d)
            _remote(src_ref, out_ref.at[c, idx], send_sems.at[d], recv_sems.at[d], dev).wait_recv()
        for d in range(1, 4):
            dev, idx = _chip_peer(x, y, c, d)
            _remote(src_ref, out_ref.at[c, idx], send_sems.at[d], recv_sems.at[d], dev).wait_send()
        pltpu.make_async_copy(src_ref, out_ref.at[c, me], local_sem).wait()

    return pl.pallas_call(
        body, name=name,
        out_shape=jax.ShapeDtypeStruct((2, 4, R, C), piece.dtype),
        in_specs=[HBM_SPEC], out_specs=HBM_SPEC,
        scratch_shapes=[pltpu.SemaphoreType.DMA((4,)), pltpu.SemaphoreType.DMA((4,)),
                        pltpu.SemaphoreType.DMA],
    )(piece)


def _ag_c(buf, name):
    _, nchip, R, C = buf.shape
    chunks = _row_chunks(R, buf.dtype, D2D_CHUNKS // nchip)

    def body(src_ref, out_ref, send_sem, recv_sem):
        x, y, c, _ = _mesh_place()
        sib = (x, y, 1 - c)
        for k in range(nchip):
            for r0, n in chunks:
                _remote(src_ref.at[c, k, pl.ds(r0, n)], out_ref.at[c, k, pl.ds(r0, n)],
                        send_sem, recv_sem, sib).start()
        _remote(src_ref.at[c], out_ref.at[1 - c], send_sem, recv_sem, sib).wait_recv()
        _remote(src_ref.at[c], out_ref.at[1 - c], send_sem, recv_sem, sib).wait_send()

    return pl.pallas_call(
        body, name=name,
        out_shape=jax.ShapeDtypeStruct(buf.shape, buf.dtype),
        in_specs=[HBM_SPEC], out_specs=HBM_SPEC, input_output_aliases={0: 0},
        scratch_shapes=[pltpu.SemaphoreType.DMA, pltpu.SemaphoreType.DMA],
    )(buf)


def _rs_c(g, name):
    _, N, C = g.shape
    chunks = _row_chunks(N, g.dtype, D2D_CHUNKS)

    def body(src_ref, out_ref, send_sem, recv_sem):
        x, y, c, _ = _mesh_place()
        sib = (x, y, 1 - c)
        for r0, n in chunks:
            _remote(src_ref.at[1 - c, pl.ds(r0, n)], out_ref.at[pl.ds(r0, n)],
                    send_sem, recv_sem, sib).start()
        _remote(src_ref.at[1 - c], out_ref, send_sem, recv_sem, sib).wait_recv()
        _remote(src_ref.at[1 - c], out_ref, send_sem, recv_sem, sib).wait_send()

    return pl.pallas_call(
        body, name=name,
        out_shape=jax.ShapeDtypeStruct((N, C), g.dtype),
        in_specs=[HBM_SPEC], out_specs=HBM_SPEC,
        scratch_shapes=[pltpu.SemaphoreType.DMA, pltpu.SemaphoreType.DMA],
    )(g)


def _rs_xy(h, name):
    _, R, C = h.shape
    chunks = _row_chunks(R, h.dtype, ICI_CHUNKS)

    def body(src_ref, out_ref, send_sems, recv_sems):
        x, y, c, me = _mesh_place()
        for d in range(1, 4):
            dev, idx = _chip_peer(x, y, c, d)
            for r0, n in chunks:
                _remote(src_ref.at[idx, pl.ds(r0, n)], out_ref.at[me, pl.ds(r0, n)],
                        send_sems.at[d], recv_sems.at[d], dev).start()
        for d in range(1, 4):
            dev, idx = _chip_peer(x, y, c, d)
            _remote(src_ref.at[idx], out_ref.at[idx], send_sems.at[d], recv_sems.at[d],
                    dev).wait_recv()
        for d in range(1, 4):
            dev, idx = _chip_peer(x, y, c, d)
            _remote(src_ref.at[idx], out_ref.at[idx], send_sems.at[d], recv_sems.at[d],
                    dev).wait_send()

    return pl.pallas_call(
        body, name=name,
        out_shape=jax.ShapeDtypeStruct(h.shape, h.dtype),
        in_specs=[HBM_SPEC], out_specs=HBM_SPEC,
        scratch_shapes=[pltpu.SemaphoreType.DMA((4,)), pltpu.SemaphoreType.DMA((4,))],
    )(h)


def _matmul(a, b, *, trans_b, tm, tn, name, add=None, add_scale=1.0):
    M, K = a.shape
    N = b.shape[0] if trans_b else b.shape[1]
    tm, tn = min(tm, M), min(tn, N)
    assert M % tm == 0 and N % tn == 0
    dn = (((1,), (1,)), ((), ())) if trans_b else (((1,), (0,)), ((), ()))

    def body(*refs):
        if add is None:
            a_ref, b_ref, o_ref = refs
        else:
            a_ref, b_ref, add_ref, o_ref = refs
        r = lax.dot_general(a_ref[...].astype(BF16), b_ref[...].astype(BF16), dn,
                            preferred_element_type=F32)
        if add is not None:
            r = r + add_scale * add_ref[...]
        o_ref[...] = r

    b_spec = (pl.BlockSpec((tn, K), lambda j, i: (j, 0)) if trans_b
              else pl.BlockSpec((K, tn), lambda j, i: (0, j)))
    in_specs = [pl.BlockSpec((tm, K), lambda j, i: (i, 0)), b_spec]
    args = [a, b]
    if add is not None:
        in_specs.append(pl.BlockSpec((tm, tn), lambda j, i: (i, j)))
        args.append(add)
    return pl.pallas_call(
        body, name=name, grid=(N // tn, M // tm),
        in_specs=in_specs, out_specs=pl.BlockSpec((tm, tn), lambda j, i: (i, j)),
        out_shape=jax.ShapeDtypeStruct((M, N), F32),
        compiler_params=_cparams("parallel", "parallel"),
    )(*args)


def _matmul_tn(a, b, *, tm, tn, tk, name):
    T, M = a.shape
    N = b.shape[1]
    tm, tn, tk = min(tm, M), min(tn, N), min(tk, T)
    assert M % tm == 0 and N % tn == 0 and T % tk == 0

    def body(a_ref, b_ref, o_ref):
        @pl.when(pl.program_id(2) == 0)
        def _():
            o_ref[...] = jnp.zeros_like(o_ref)

        o_ref[...] += lax.dot_general(a_ref[...].astype(BF16), b_ref[...].astype(BF16),
                                      (((0,), (0,)), ((), ())), preferred_element_type=F32)

    return pl.pallas_call(
        body, name=name, grid=(M // tm, N // tn, T // tk),
        in_specs=[pl.BlockSpec((tk, tm), lambda i, j, k: (k, i)),
                  pl.BlockSpec((tk, tn), lambda i, j, k: (k, j))],
        out_specs=pl.BlockSpec((tm, tn), lambda i, j, k: (i, j)),
        out_shape=jax.ShapeDtypeStruct((M, N), F32),
        compiler_params=_cparams("parallel", "parallel", "arbitrary"),
    )(a, b)


def _head_masks(rows):
    lane = lax.broadcasted_iota(jnp.int32, (rows, LANES), 1)
    return lane < HEAD_DIM, lane >= HEAD_DIM


def _causal(i_q, i_k, tq, tk):
    row = i_q * tq + lax.broadcasted_iota(jnp.int32, (tq, tk), 0)
    col = i_k * tk + lax.broadcasted_iota(jnp.int32, (tq, tk), 1)
    return row >= col


def _flash_fwd(proj, cum4, *, tb, name):
    T = proj.shape[0]
    D = N_HEADS * HEAD_DIM
    nb = T // tb
    cb = D // LANES

    def body(q_ref, k_ref, v_ref, g_ref, cum_ref, o_ref, og_ref, lp_ref, kb_ref, vb_ref):
        i = pl.program_id(1)

        @pl.when(i == 0)
        def _():
            kb_ref[...] = k_ref[...].astype(BF16)
            vb_ref[...] = v_ref[...].astype(BF16)

        q = q_ref[...] * (HEAD_DIM ** -0.5)
        masks = _head_masks(tb)
        outs = []
        for h in range(2):
            qh = jnp.where(masks[h], q, 0.0).astype(BF16)
            cref = cum_ref[0, h, pl.ds(i, 1), :][:, 0:1]

            def step(kbi, carry, masked, qh=qh, cref=cref, h=h):
                m, l, acc = carry
                k0 = pl.multiple_of(kbi * tb, tb)
                s = lax.dot_general(qh, kb_ref[pl.ds(k0, tb), :], (((1,), (1,)), ((), ())),
                                    preferred_element_type=F32)
                s = s + (cref - cum_ref[0, h, pl.ds(kbi, 1), :])
                if masked:
                    s = jnp.where(_causal(i, kbi, tb, tb), s, -jnp.inf)
                m_new = jnp.maximum(m, jnp.max(s, axis=-1, keepdims=True))
                alpha = jnp.exp(m - m_new)
                p = jnp.exp(s - m_new)
                l = alpha * l + jnp.sum(p, axis=-1, keepdims=True)
                acc = alpha * acc + jnp.dot(p.astype(BF16), vb_ref[pl.ds(k0, tb), :],
                                            preferred_element_type=F32)
                return m_new, l, acc

            init = (jnp.full((tb, 1), -jnp.inf, F32), jnp.zeros((tb, 1), F32),
                    jnp.zeros((tb, LANES), F32))
            carry = lax.fori_loop(0, i, lambda kbi, c: step(kbi, c, False), init)
            m, l, acc = step(i, carry, True)
            outs.append(acc / l)
            lp_ref[h] = jnp.broadcast_to(m + jnp.log(l) - cref, (tb, LANES))
        o = jnp.where(masks[0], outs[0], outs[1])
        o_ref[...] = o
        gate = g_ref[...]
        og_ref[...] = o * (gate * _sigmoid(gate))

    return pl.pallas_call(
        body, name=name, grid=(N_PAIRS, nb),
        in_specs=[pl.BlockSpec((tb, LANES), lambda j, i: (i, j)),
                  pl.BlockSpec((T, LANES), lambda j, i: (0, cb + j)),
                  pl.BlockSpec((T, LANES), lambda j, i: (0, 2 * cb + j)),
                  pl.BlockSpec((tb, LANES), lambda j, i: (i, 3 * cb + j)),
                  pl.BlockSpec((1, 2, nb, tb), lambda j, i: (j, 0, 0, 0))],
        out_specs=[pl.BlockSpec((tb, LANES), lambda j, i: (i, j)),
                   pl.BlockSpec((tb, LANES), lambda j, i: (i, j)),
                   pl.BlockSpec((2, tb, LANES), lambda j, i: (j, i, 0))],
        out_shape=[jax.ShapeDtypeStruct((T, D), F32), jax.ShapeDtypeStruct((T, D), F32),
                   jax.ShapeDtypeStruct((N_HEADS, T, LANES), F32)],
        scratch_shapes=[pltpu.VMEM((T, LANES), BF16), pltpu.VMEM((T, LANES), BF16)],
        compiler_params=_cparams("parallel", "arbitrary"),
    )(proj, proj, proj, proj, cum4)


def _flash_bwd_dq(proj, cum4, o, dog, lp, *, tb, name):
    T = proj.shape[0]
    D = N_HEADS * HEAD_DIM
    nb = T // tb
    cb = D // LANES

    def body(q_ref, k_ref, v_ref, g_ref, cum_ref, o_ref, dog_ref, lp_ref,
             dq_ref, dg_ref, do_ref, dl_ref, dc_ref, kb_ref, vb_ref):
        i = pl.program_id(1)

        @pl.when(i == 0)
        def _():
            kb_ref[...] = k_ref[...].astype(BF16)
            vb_ref[...] = v_ref[...].astype(BF16)

        gate = g_ref[...]
        sg = _sigmoid(gate)
        o = o_ref[...]
        dog = dog_ref[...]
        do = dog * (gate * sg)
        dg_ref[...] = dog * o * (sg * (1.0 + gate * (1.0 - sg)))
        do_ref[...] = do.astype(BF16)
        q = q_ref[...] * (HEAD_DIM ** -0.5)
        masks = _head_masks(tb)
        dqs = []
        for h in range(2):
            qh = jnp.where(masks[h], q, 0.0).astype(BF16)
            doh = jnp.where(masks[h], do, 0.0).astype(BF16)
            delta = jnp.sum(jnp.where(masks[h], do * o, 0.0), axis=-1, keepdims=True)
            dl_ref[h] = jnp.broadcast_to(delta, (tb, LANES))
            lph = lp_ref[h][:, 0:1]

            def step(kbi, carry, masked, qh=qh, doh=doh, delta=delta, lph=lph, h=h):
                acc, rs = carry
                k0 = pl.multiple_of(kbi * tb, tb)
                kblk = kb_ref[pl.ds(k0, tb), :]
                s = lax.dot_general(qh, kblk, (((1,), (1,)), ((), ())), preferred_element_type=F32)
                p = jnp.exp(s - cum_ref[0, h, pl.ds(kbi, 1), :] - lph)
                if masked:
                    p = jnp.where(_causal(i, kbi, tb, tb), p, 0.0)
                dp = lax.dot_general(doh, vb_ref[pl.ds(k0, tb), :], (((1,), (1,)), ((), ())),
                                     preferred_element_type=F32)
                ds = p * (dp - delta)
                return (acc + jnp.dot(ds.astype(BF16), kblk, preferred_element_type=F32),
                        rs + jnp.sum(ds, axis=-1, keepdims=True))

            carry = lax.fori_loop(0, i, lambda kbi, c: step(kbi, c, False),
                                  (jnp.zeros((tb, LANES), F32), jnp.zeros((tb, 1), F32)))
            acc, rs = step(i, carry, True)
            dqs.append(acc)
            dc_ref[0, 0, pl.ds(h, 1), :] = jnp.broadcast_to(rs, (tb, LANES)).T[0:1, :]
        dq_ref[...] = jnp.where(masks[0], dqs[0], dqs[1]) * (HEAD_DIM ** -0.5)

    blk = pl.BlockSpec((tb, LANES), lambda j, i: (i, j))
    stat = pl.BlockSpec((2, tb, LANES), lambda j, i: (j, i, 0))
    return pl.pallas_call(
        body, name=name, grid=(N_PAIRS, nb),
        in_specs=[blk,
                  pl.BlockSpec((T, LANES), lambda j, i: (0, cb + j)),
                  pl.BlockSpec((T, LANES), lambda j, i: (0, 2 * cb + j)),
                  pl.BlockSpec((tb, LANES), lambda j, i: (i, 3 * cb + j)),
                  pl.BlockSpec((1, 2, nb, tb), lambda j, i: (j, 0, 0, 0)),
                  blk, blk, stat],
        out_specs=[blk, blk, blk, stat, pl.BlockSpec((1, 1, 2, tb), lambda j, i: (j, i, 0, 0))],
        out_shape=[jax.ShapeDtypeStruct((T, D), F32), jax.ShapeDtypeStruct((T, D), F32),
                   jax.ShapeDtypeStruct((T, D), BF16),
                   jax.ShapeDtypeStruct((N_HEADS, T, LANES), F32),
                   jax.ShapeDtypeStruct((N_PAIRS, nb, 2, tb), F32)],
        scratch_shapes=[pltpu.VMEM((T, LANES), BF16), pltpu.VMEM((T, LANES), BF16)],
        compiler_params=_cparams("parallel", "arbitrary"),
    )(proj, proj, proj, proj, cum4, o, dog, lp)


def _flash_bwd_dkv(proj, cum4, do, lp, delta, *, tb, name):
    T = proj.shape[0]
    D = N_HEADS * HEAD_DIM
    nb = T // tb
    cb = D // LANES

    def body(q_ref, k_ref, v_ref, cum_ref, do_ref, lp_ref, dl_ref, dk_ref, dv_ref, dc_ref):
        kbi = pl.program_id(1)
        k = k_ref[...] * (HEAD_DIM ** -0.5)
        v = v_ref[...]
        masks = _head_masks(tb)
        dks, dvs = [], []
        for h in range(2):
            kh = jnp.where(masks[h], k, 0.0).astype(BF16)
            vh = jnp.where(masks[h], v, 0.0).astype(BF16)
            ck = cum_ref[0, h, pl.ds(kbi, 1), :]

            def step(i, carry, masked, kh=kh, vh=vh, ck=ck, h=h):
                dk, dv, dc = carry
                q0 = pl.multiple_of(i * tb, tb)
                qb = q_ref[pl.ds(q0, tb), :].astype(BF16)
                dob = do_ref[pl.ds(q0, tb), :]
                s = lax.dot_general(qb, kh, (((1,), (1,)), ((), ())), preferred_element_type=F32)
                p = jnp.exp(s - ck - lp_ref[h, pl.ds(q0, tb), :][:, 0:1])
                if masked:
                    p = jnp.where(_causal(i, kbi, tb, tb), p, 0.0)
                dp = lax.dot_general(dob, vh, (((1,), (1,)), ((), ())), preferred_element_type=F32)
                ds = p * (dp - dl_ref[h, pl.ds(q0, tb), :][:, 0:1])
                dv = dv + lax.dot_general(p.astype(BF16), dob, (((0,), (0,)), ((), ())),
                                          preferred_element_type=F32)
                dk = dk + lax.dot_general(ds.astype(BF16), qb, (((0,), (0,)), ((), ())),
                                          preferred_element_type=F32)
                dc = dc - jnp.sum(ds, axis=0, keepdims=True)
                return dk, dv, dc

            init = (jnp.zeros((tb, LANES), F32), jnp.zeros((tb, LANES), F32),
                    jnp.zeros((1, tb), F32))
            carry = step(kbi, init, True)
            dk, dv, dc = lax.fori_loop(kbi + 1, nb, lambda i, c: step(i, c, False), carry)
            dks.append(dk)
            dvs.append(dv)
            dc_ref[0, 0, pl.ds(h, 1), :] = dc
        dk_ref[...] = jnp.where(masks[0], dks[0], dks[1]) * (HEAD_DIM ** -0.5)
        dv_ref[...] = jnp.where(masks[0], dvs[0], dvs[1])

    full = pl.BlockSpec((T, LANES), lambda j, i: (0, j))
    stat = pl.BlockSpec((2, T, LANES), lambda j, i: (j, 0, 0))
    blk = pl.BlockSpec((tb, LANES), lambda j, i: (i, j))
    return pl.pallas_call(
        body, name=name, grid=(N_PAIRS, nb),
        in_specs=[full,
                  pl.BlockSpec((tb, LANES), lambda j, i: (i, cb + j)),
                  pl.BlockSpec((tb, LANES), lambda j, i: (i, 2 * cb + j)),
                  pl.BlockSpec((1, 2, nb, tb), lambda j, i: (j, 0, 0, 0)),
                  full, stat, stat],
        out_specs=[blk, blk, pl.BlockSpec((1, 1, 2, tb), lambda j, i: (j, i, 0, 0))],
        out_shape=[jax.ShapeDtypeStruct((T, D), F32), jax.ShapeDtypeStruct((T, D), F32),
                   jax.ShapeDtypeStruct((N_PAIRS, nb, 2, tb), F32)],
        compiler_params=_cparams("parallel", "arbitrary"),
    )(proj, proj, proj, cum4, do, lp, delta)


def _cumsum_fwd(proj, bf_row, *, tt, name):
    T = proj.shape[0]
    cb = (proj.shape[1] - LANES) // LANES

    def body(f_ref, b_ref, out_ref, carry_ref):
        i = pl.program_id(0)

        @pl.when(i == 0)
        def _():
            carry_ref[...] = jnp.zeros_like(carry_ref)

        ls = -_softplus(-(f_ref[...] + b_ref[...]))
        tri = (lax.broadcasted_iota(jnp.int32, (tt, tt), 0)
               >= lax.broadcasted_iota(jnp.int32, (tt, tt), 1)).astype(F32)
        cum = jnp.dot(tri, ls, preferred_element_type=F32,
                      precision=lax.Precision.HIGHEST) + carry_ref[...]
        carry_ref[...] = cum[tt - 1:tt, :]
        out_ref[...] = cum.T

    return pl.pallas_call(
        body, name=name, grid=(T // tt,),
        in_specs=[pl.BlockSpec((tt, LANES), lambda i: (i, cb)),
                  pl.BlockSpec((1, LANES), lambda i: (0, 0))],
        out_specs=pl.BlockSpec((LANES, tt), lambda i: (0, i)),
        out_shape=jax.ShapeDtypeStruct((LANES, T), F32),
        scratch_shapes=[pltpu.VMEM((1, LANES), F32)],
        compiler_params=_cparams("arbitrary"),
    )(proj, bf_row)


def _cumsum_bwd(dcum_t, proj, bf_row, *, tt, name):
    T = proj.shape[0]
    cb = (proj.shape[1] - LANES) // LANES
    nt = T // tt

    def body(dc_ref, f_ref, b_ref, df_ref, db_ref, carry_ref):
        i = pl.program_id(0)

        @pl.when(i == 0)
        def _():
            carry_ref[...] = jnp.zeros_like(carry_ref)
            db_ref[...] = jnp.zeros_like(db_ref)

        dc = dc_ref[...].T
        tri = (lax.broadcasted_iota(jnp.int32, (tt, tt), 0)
               <= lax.broadcasted_iota(jnp.int32, (tt, tt), 1)).astype(F32)
        rev = jnp.dot(tri, dc, preferred_element_type=F32,
                      precision=lax.Precision.HIGHEST) + carry_ref[...]
        carry_ref[...] = rev[0:1, :]
        df = rev * _sigmoid(-(f_ref[...] + b_ref[...]))
        df_ref[...] = df
        db_ref[...] += jnp.sum(df, axis=0, keepdims=True)

    return pl.pallas_call(
        body, name=name, grid=(nt,),
        in_specs=[pl.BlockSpec((LANES, tt), lambda i: (0, nt - 1 - i)),
                  pl.BlockSpec((tt, LANES), lambda i: (nt - 1 - i, cb)),
                  pl.BlockSpec((1, LANES), lambda i: (0, 0))],
        out_specs=[pl.BlockSpec((tt, LANES), lambda i: (nt - 1 - i, 0)),
                   pl.BlockSpec((1, LANES), lambda i: (0, 0))],
        out_shape=[jax.ShapeDtypeStruct((T, LANES), F32), jax.ShapeDtypeStruct((1, LANES), F32)],
        scratch_shapes=[pltpu.VMEM((1, LANES), F32)],
        compiler_params=_cparams("arbitrary"),
    )(dcum_t, proj, bf_row)


def _rg_gates(upad_ref, small_ref, wa_ref, wi_ref, tt):
    off = SUBLANES - (CONV_WIDTH - 1)
    u = small_ref[4:5, :]
    for tap in range(CONV_WIDTH):
        u = u + upad_ref[off + tap:off + tap + tt, :] * small_ref[tap:tap + 1, :]
    pa, pi = [], []
    for n in range(RNN_BLOCKS):
        ub = u[:, n * RNN_BLOCK_WIDTH:(n + 1) * RNN_BLOCK_WIDTH].astype(BF16)
        pa.append(jnp.dot(ub, wa_ref[n], preferred_element_type=F32))
        pi.append(jnp.dot(ub, wi_ref[n], preferred_element_type=F32))
    r = _sigmoid(jnp.concatenate(pa, axis=-1) + small_ref[5:6, :])
    ig = _sigmoid(jnp.concatenate(pi, axis=-1) + small_ref[6:7, :])
    spl = _softplus(-small_ref[7:8, :])
    log_a = (-LRU_C) * r * spl
    a = jnp.exp(log_a)
    s = jnp.sqrt(jnp.tanh(-log_a) * (a * a + 1.0))
    return u, r, ig, spl, a, s


def _rg_fwd(proj, small, wa, wi, *, tt, name):
    T = proj.shape[0]
    D = RNN_BLOCKS * RNN_BLOCK_WIDTH
    hb = tt // SUBLANES

    def body(u0_ref, halo_ref, g_ref, small_ref, wa_ref, wi_ref, h_ref, y_ref,
             upad_ref, a_ref, b_ref, carry_ref):
        i = pl.program_id(0)

        @pl.when(i == 0)
        def _():
            carry_ref[...] = jnp.zeros_like(carry_ref)

        upad_ref[0:SUBLANES, :] = jnp.where(i == 0, 0.0, halo_ref[...])
        upad_ref[SUBLANES:, :] = u0_ref[...]
        u, r, ig, spl, a, s = _rg_gates(upad_ref, small_ref, wa_ref, wi_ref, tt)
        a_ref[...] = a
        b_ref[...] = s * (ig * u)

        def row(t, h):
            h = a_ref[pl.ds(t, 1), :] * h + b_ref[pl.ds(t, 1), :]
            h_ref[pl.ds(t, 1), :] = h
            return h

        carry_ref[...] = lax.fori_loop(0, tt, row, carry_ref[...], unroll=8)
        gate = g_ref[...]
        y_ref[...] = h_ref[...] * (gate * _sigmoid(gate))

    return pl.pallas_call(
        body, name=name, grid=(T // tt,),
        in_specs=[pl.BlockSpec((tt, D), lambda i: (i, 0)),
                  pl.BlockSpec((SUBLANES, D), lambda i: (jnp.maximum(i * hb - 1, 0), 0)),
                  pl.BlockSpec((tt, D), lambda i: (i, 1)),
                  pl.BlockSpec((SUBLANES, D), lambda i: (0, 0)),
                  pl.BlockSpec((RNN_BLOCKS, RNN_BLOCK_WIDTH, RNN_BLOCK_WIDTH), lambda i: (0, 0, 0)),
                  pl.BlockSpec((RNN_BLOCKS, RNN_BLOCK_WIDTH, RNN_BLOCK_WIDTH), lambda i: (0, 0, 0))],
        out_specs=[pl.BlockSpec((tt, D), lambda i: (i, 0)), pl.BlockSpec((tt, D), lambda i: (i, 0))],
        out_shape=[jax.ShapeDtypeStruct((T, D), F32), jax.ShapeDtypeStruct((T, D), F32)],
        scratch_shapes=[pltpu.VMEM((tt + SUBLANES, D), F32), pltpu.VMEM((tt, D), F32),
                        pltpu.VMEM((tt, D), F32), pltpu.VMEM((1, D), F32)],
        compiler_params=_cparams("arbitrary"),
    )(proj, proj, proj, small, wa, wi)


def _rg_bwd(proj, hs, dy, small, wa, wi, *, tt, name):
    T = proj.shape[0]
    D = RNN_BLOCKS * RNN_BLOCK_WIDTH
    W = RNN_BLOCK_WIDTH
    hb = tt // SUBLANES
    nt = T // tt

    def body(u0_ref, uhalo_ref, g_ref, h_ref, hhalo_ref, dy_ref, small_ref, wa_ref, wi_ref,
             dp_ref, dwa_ref, dwi_ref, ds_ref,
             upad_ref, hpad_ref, a_ref, g_s_ref, duext_ref, carry_ref):
        i = pl.program_id(0)
        first_chunk = i == nt - 1

        @pl.when(i == 0)
        def _():
            carry_ref[...] = jnp.zeros_like(carry_ref)
            duext_ref[...] = jnp.zeros_like(duext_ref)
            dwa_ref[...] = jnp.zeros_like(dwa_ref)
            dwi_ref[...] = jnp.zeros_like(dwi_ref)
            ds_ref[...] = jnp.zeros_like(ds_ref)

        upad_ref[0:SUBLANES, :] = jnp.where(first_chunk, 0.0, uhalo_ref[...])
        upad_ref[SUBLANES:, :] = u0_ref[...]
        hpad_ref[0:SUBLANES, :] = jnp.where(first_chunk, 0.0, hhalo_ref[...])
        hpad_ref[SUBLANES:, :] = h_ref[...]
        u, r, ig, spl, a, s = _rg_gates(upad_ref, small_ref, wa_ref, wi_ref, tt)
        gate = g_ref[...]
        sg = _sigmoid(gate)
        dy = dy_ref[...]
        dp_ref[:, D:] = dy * h_ref[...] * (sg * (1.0 + gate * (1.0 - sg)))
        a_ref[...] = a
        g_s_ref[...] = dy * (gate * sg)

        def row(k, c):
            t = tt - 1 - k
            g = g_s_ref[pl.ds(t, 1), :] + c
            g_s_ref[pl.ds(t, 1), :] = g
            return a_ref[pl.ds(t, 1), :] * g

        carry_ref[...] = lax.fori_loop(0, tt, row, carry_ref[...], unroll=8)
        g = g_s_ref[...]
        h_prev = hpad_ref[SUBLANES - 1:SUBLANES - 1 + tt, :]
        iu = ig * u
        d_iu = g * s
        dlog_a = (g * h_prev) * a - (g * iu) * (a * a) / s
        dpre_a = (dlog_a * ((-LRU_C) * spl)) * r * (1.0 - r)
        dpre_i = (d_iu * u) * ig * (1.0 - ig)
        dlam = jnp.sum(dlog_a * r, axis=0, keepdims=True) * (LRU_C * _sigmoid(-small_ref[7:8, :]))
        du_parts = []
        for n in range(RNN_BLOCKS):
            sl = slice(n * W, (n + 1) * W)
            ub = u[:, sl].astype(BF16)
            da_n = dpre_a[:, sl].astype(BF16)
            di_n = dpre_i[:, sl].astype(BF16)
            dwa_ref[n] += lax.dot_general(ub, da_n, (((0,), (0,)), ((), ())),
                                          preferred_element_type=F32)
            dwi_ref[n] += lax.dot_general(ub, di_n, (((0,), (0,)), ((), ())),
                                          preferred_element_type=F32)
            du_parts.append(
                lax.dot_general(da_n, wa_ref[n], (((1,), (1,)), ((), ())), preferred_element_type=F32)
                + lax.dot_general(di_n, wi_ref[n], (((1,), (1,)), ((), ())), preferred_element_type=F32))
        du = d_iu * ig + jnp.concatenate(du_parts, axis=-1)
        off = SUBLANES - (CONV_WIDTH - 1)
        for tap in range(CONV_WIDTH):
            ds_ref[tap:tap + 1, :] += jnp.sum(du * upad_ref[off + tap:off + tap + tt, :],
                                              axis=0, keepdims=True)
        ds_ref[4:5, :] += jnp.sum(du, axis=0, keepdims=True)
        ds_ref[5:6, :] += jnp.sum(dpre_a, axis=0, keepdims=True)
        ds_ref[6:7, :] += jnp.sum(dpre_i, axis=0, keepdims=True)
        ds_ref[7:8, :] += dlam
        duext_ref[0:tt, :] = du
        du0 = jnp.zeros((tt, D), F32)
        for tap in range(CONV_WIDTH):
            sh = CONV_WIDTH - 1 - tap
            du0 = du0 + duext_ref[sh:sh + tt, :] * small_ref[tap:tap + 1, :]
        dp_ref[:, :D] = du0
        duext_ref[tt:, :] = du[0:SUBLANES, :]

    rev = lambda i: nt - 1 - i
    wspec = pl.BlockSpec((RNN_BLOCKS, W, W), lambda i: (0, 0, 0))
    return pl.pallas_call(
        body, name=name, grid=(nt,),
        in_specs=[pl.BlockSpec((tt, D), lambda i: (rev(i), 0)),
                  pl.BlockSpec((SUBLANES, D), lambda i: (jnp.maximum(rev(i) * hb - 1, 0), 0)),
                  pl.BlockSpec((tt, D), lambda i: (rev(i), 1)),
                  pl.BlockSpec((tt, D), lambda i: (rev(i), 0)),
                  pl.BlockSpec((SUBLANES, D), lambda i: (jnp.maximum(rev(i) * hb - 1, 0), 0)),
                  pl.BlockSpec((tt, D), lambda i: (rev(i), 0)),
                  pl.BlockSpec((SUBLANES, D), lambda i: (0, 0)),
                  wspec, wspec],
        out_specs=[pl.BlockSpec((tt, 2 * D), lambda i: (rev(i), 0)),
                   wspec, wspec, pl.BlockSpec((SUBLANES, D), lambda i: (0, 0))],
        out_shape=[jax.ShapeDtypeStruct((T, 2 * D), F32),
                   jax.ShapeDtypeStruct((RNN_BLOCKS, W, W), F32),
                   jax.ShapeDtypeStruct((RNN_BLOCKS, W, W), F32),
                   jax.ShapeDtypeStruct((SUBLANES, D), F32)],
        scratch_shapes=[pltpu.VMEM((tt + SUBLANES, D), F32), pltpu.VMEM((tt + SUBLANES, D), F32),
                        pltpu.VMEM((tt, D), F32), pltpu.VMEM((tt, D), F32),
                        pltpu.VMEM((tt + SUBLANES, D), F32), pltpu.VMEM((1, D), F32)],
        compiler_params=_cparams("arbitrary"),
    )(proj, proj, proj, hs, hs, dy, small, wa, wi)


def _ln_fwd(x, h, g, b, *, tt, name):
    T, D = x.shape

    def body(x_ref, h_ref, g_ref, b_ref, y_ref, zh_ref, rs_ref):
        z = ALPHA * x_ref[...] + h_ref[...]
        mu = jnp.mean(z, axis=-1, keepdims=True)
        zc = z - mu
        rstd = lax.rsqrt(jnp.mean(zc * zc, axis=-1, keepdims=True) + LN_EPS)
        zh = zc * rstd
        zh_ref[...] = zh
        rs_ref[...] = rstd
        y_ref[...] = zh * g_ref[...] + b_ref[...]

    blk = pl.BlockSpec((tt, D), lambda i: (i, 0))
    row = pl.BlockSpec((1, D), lambda i: (0, 0))
    return pl.pallas_call(
        body, name=name, grid=(T // tt,),
        in_specs=[blk, blk, row, row],
        out_specs=[blk, blk, pl.BlockSpec((tt, 1), lambda i: (i, 0))],
        out_shape=[jax.ShapeDtypeStruct((T, D), F32), jax.ShapeDtypeStruct((T, D), F32),
                   jax.ShapeDtypeStruct((T, 1), F32)],
        compiler_params=_cparams("parallel"),
    )(x, h, g, b)


def _ln_bwd(dy, zh, rstd, g, *, tt, name):
    T, D = dy.shape

    def body(dy_ref, zh_ref, rs_ref, g_ref, dz_ref, dg_ref, db_ref):
        @pl.when(pl.program_id(0) == 0)
        def _():
            dg_ref[...] = jnp.zeros_like(dg_ref)
            db_ref[...] = jnp.zeros_like(db_ref)

        dy = dy_ref[...]
        zh = zh_ref[...]
        dg_ref[...] += jnp.sum(dy * zh, axis=0, keepdims=True)
        db_ref[...] += jnp.sum(dy, axis=0, keepdims=True)
        dzh = dy * g_ref[...]
        m1 = jnp.mean(dzh, axis=-1, keepdims=True)
        m2 = jnp.mean(dzh * zh, axis=-1, keepdims=True)
        dz_ref[...] = rs_ref[...] * (dzh - m1 - zh * m2)

    blk = pl.BlockSpec((tt, D), lambda i: (i, 0))
    row = pl.BlockSpec((1, D), lambda i: (0, 0))
    return pl.pallas_call(
        body, name=name, grid=(T // tt,),
        in_specs=[blk, blk, pl.BlockSpec((tt, 1), lambda i: (i, 0)), row],
        out_specs=[blk, row, row],
        out_shape=[jax.ShapeDtypeStruct((T, D), F32), jax.ShapeDtypeStruct((1, D), F32),
                   jax.ShapeDtypeStruct((1, D), F32)],
        compiler_params=_cparams("arbitrary"),
    )(dy, zh, rstd, g)


def _loss(y, tgt, *, tt, name):
    T, D = y.shape

    def body(y_ref, t_ref, l_ref, dy_ref):
        @pl.when(pl.program_id(0) == 0)
        def _():
            l_ref[...] = jnp.zeros_like(l_ref)

        e = y_ref[...] - t_ref[...]
        dy_ref[...] = e * (1.0 / D)
        l_ref[...] += jnp.sum(e * e, axis=0, keepdims=True) * (0.5 / D)

    blk = pl.BlockSpec((tt, D), lambda i: (i, 0))
    return pl.pallas_call(
        body, name=name, grid=(T // tt,),
        in_specs=[blk, blk], out_specs=[pl.BlockSpec((1, D), lambda i: (0, 0)), blk],
        out_shape=[jax.ShapeDtypeStruct((1, D), F32), jax.ShapeDtypeStruct((T, D), F32)],
        compiler_params=_cparams("arbitrary"),
    )(y, tgt)


def _row_tile(rows, target):
    best = SUBLANES
    for t in range(SUBLANES, target + 1, SUBLANES):
        if rows % t == 0:
            best = t
    return best


def _add_own(g, recv, c_idx, *, tr, name):
    _, N, C = g.shape

    def body(c_ref, g_ref, r_ref, o_ref):
        o_ref[...] = g_ref[0] + r_ref[...]

    return pl.pallas_call(
        body, name=name,
        grid_spec=pltpu.PrefetchScalarGridSpec(
            num_scalar_prefetch=1, grid=(N // tr,),
            in_specs=[pl.BlockSpec((1, tr, C), lambda i, c: (c[0], i, 0)),
                      pl.BlockSpec((tr, C), lambda i, c: (i, 0))],
            out_specs=pl.BlockSpec((tr, C), lambda i, c: (i, 0))),
        out_shape=jax.ShapeDtypeStruct((N, C), F32),
        compiler_params=_cparams("parallel"),
    )(c_idx, g, recv)


def _adamw_math(g, w_ref, m_ref, v_ref, g_ref, d_ref, nm_ref, nv_ref):
    nm = ADAM_B1 * m_ref[...] + (1.0 - ADAM_B1) * g
    nv = ADAM_B2 * v_ref[...] + (1.0 - ADAM_B2) * (g * g)
    m_hat = nm / (1.0 - ADAM_B1 ** ADAM_STEP)
    v_hat = nv / (1.0 - ADAM_B2 ** ADAM_STEP)
    g_ref[...] = g
    nm_ref[...] = nm
    nv_ref[...] = nv
    d_ref[...] = (-ADAM_LR) * (m_hat / (jnp.sqrt(v_hat) + ADAM_EPS) + ADAM_WD * w_ref[...])


def _adamw(parts, w, m, v, *, tr, name):
    n, R, C = parts.shape
    tr = min(tr, R)

    def body(p_ref, w_ref, m_ref, v_ref, *out_refs):
        g = p_ref[0]
        for k in range(1, n):
            g = g + p_ref[k]
        _adamw_math(g, w_ref, m_ref, v_ref, *out_refs)

    blk = pl.BlockSpec((tr, C), lambda i: (i, 0))
    out = jax.ShapeDtypeStruct((R, C), F32)
    return pl.pallas_call(
        body, name=name, grid=(R // tr,),
        in_specs=[pl.BlockSpec((n, tr, C), lambda i: (0, i, 0)), blk, blk, blk],
        out_specs=[blk, blk, blk, blk], out_shape=[out, out, out, out],
        compiler_params=_cparams("parallel"),
    )(parts, w, m, v)


def _adamw_shard(h, recv, me_idx, w, m, v, *, tr, name):
    _, R, C = h.shape

    def body(me_ref, h_ref, r1_ref, r2_ref, r3_ref, w_ref, m_ref, v_ref, *out_refs):
        g = ((h_ref[0] + r1_ref[0]) + r2_ref[0]) + r3_ref[0]
        _adamw_math(g, w_ref, m_ref, v_ref, *out_refs)

    blk = pl.BlockSpec((tr, C), lambda i, me: (i, 0))

    def slot(d):
        return pl.BlockSpec((1, tr, C), lambda i, me: (me[0] ^ d, i, 0))

    out = jax.ShapeDtypeStruct((R, C), F32)
    return pl.pallas_call(
        body, name=name,
        grid_spec=pltpu.PrefetchScalarGridSpec(
            num_scalar_prefetch=1, grid=(R // tr,),
            in_specs=[slot(0), slot(1), slot(2), slot(3), blk, blk, blk],
            out_specs=[blk, blk, blk, blk]),
        out_shape=[out, out, out, out],
        compiler_params=_cparams("parallel"),
    )(me_idx, h, recv, recv, recv, w, m, v)


BIG = ("attn_w_in", "attn_w_out", "rnn_w_in", "rnn_w_a", "rnn_w_i", "rnn_w_out")
SMALL = ("rnn_conv_w", "rnn_conv_b", "rnn_b_a", "rnn_b_i", "rnn_lambda")
PACK_C = 1024


def _rows(shape):
    n = 1
    for s in shape:
        n *= s
    assert n % PACK_C == 0
    return n // PACK_C


def _pack_local(p, dtype, pad_to):
    parts = [p[k].astype(dtype).reshape(-1, PACK_C) for k in BIG]
    small = jnp.concatenate([p[k].reshape(-1) for k in SMALL])
    if dtype == BF16:
        small = lax.bitcast_convert_type(small, BF16)
    parts.append(small.reshape(-1, PACK_C))
    flat = jnp.concatenate(parts, axis=0)
    pad = (-flat.shape[0]) % pad_to
    return jnp.pad(flat, ((0, pad), (0, 0)))


def _unpack_local(flat, shapes):
    out, r = {}, 0
    for k in BIG:
        n = _rows(shapes[k])
        out[k] = flat[r:r + n].reshape(shapes[k])
        r += n
    n_small = sum(_rows_elems(shapes[k]) for k in SMALL)
    small = flat[r:r + n_small // PACK_C].reshape(-1)
    o = 0
    for k in SMALL:
        n = _rows_elems(shapes[k])
        out[k] = small[o:o + n].reshape(shapes[k])
        o += n
    return out


def _to_full(g, k, sh):
    nd = len(sh)
    if k in ("attn_w_in", "rnn_w_in", "rnn_conv_w"):
        perm = tuple(range(2, 2 + nd - 1)) + (1, 0, 2 + nd - 1)
        t = g.transpose(perm)
        return t.reshape(sh[:-1] + (8 * sh[-1],))
    if k in ("attn_w_out", "rnn_w_out", "rnn_conv_b", "rnn_b_a", "rnn_b_i", "rnn_lambda"):
        perm = (2, 1, 0) + tuple(range(3, 2 + nd))
        t = g.transpose(perm)
        return t.reshape((sh[0], 8 * sh[1]) + sh[2:])
    perm = (2, 3, 1, 0, 4, 5)
    t = g.transpose(perm)
    return t.reshape((sh[0], sh[1], 8 * sh[2], sh[3]))


def _from_full(full, k, sh):
    nd = len(sh)
    if k in ("attn_w_in", "rnn_w_in", "rnn_conv_w"):
        t = full.reshape(sh[:-1] + (4, 2, sh[-1]))
        perm = (nd, nd - 1) + tuple(range(nd - 1)) + (nd + 1,)
        return t.transpose(perm)
    if k in ("attn_w_out", "rnn_w_out", "rnn_conv_b", "rnn_b_a", "rnn_b_i", "rnn_lambda"):
        t = full.reshape((sh[0], 4, 2) + sh[1:])
        perm = (2, 1, 0) + tuple(range(3, 2 + nd))
        return t.transpose(perm)
    t = full.reshape((sh[0], sh[1], 4, 2, sh[2], sh[3]))
    return t.transpose((3, 2, 0, 1, 4, 5))


def _unpack_gathered(g, shapes):
    out, r = {}, 0
    for k in BIG:
        n = _rows(shapes[k])
        out[k] = _to_full(g[:, :, r:r + n].reshape((2, 4) + shapes[k]), k, shapes[k])
        r += n
    n_small = sum(_rows_elems(shapes[k]) for k in SMALL)
    nr = 2 * n_small // PACK_C
    small = lax.bitcast_convert_type(g[:, :, r:r + nr].reshape(2, 4, n_small, 2), F32)
    o = 0
    for k in SMALL:
        n = _rows_elems(shapes[k])
        out[k] = _to_full(small[:, :, o:o + n].reshape((2, 4) + shapes[k]), k, shapes[k])
        o += n
    return out


def _rows_elems(shape):
    n = 1
    for s in shape:
        n *= s
    return n


def _pack_grads(full, shapes, pad_to):
    parts = [_from_full(full[k], k, shapes[k]).reshape(2, 4, -1, PACK_C) for k in BIG]
    small = jnp.concatenate(
        [_from_full(full[k], k, shapes[k]).reshape(2, 4, -1) for k in SMALL], axis=-1)
    parts.append(small.reshape(2, 4, -1, PACK_C))
    flat = jnp.concatenate(parts, axis=2)
    pad = (-flat.shape[2]) % pad_to
    return jnp.pad(flat, ((0, 0), (0, 0), (0, pad), (0, 0)))


def kernel(x, ln_g, ln_b, attn_w_in, attn_b_f, attn_w_out, rnn_w_in, rnn_conv_w, rnn_conv_b, rnn_w_a, rnn_b_a, rnn_w_i, rnn_b_i, rnn_lambda, rnn_w_out, loss_target, m_ln_g, m_ln_b, m_attn_w_in, m_attn_b_f, m_attn_w_out, m_rnn_w_in, m_rnn_conv_w, m_rnn_conv_b, m_rnn_w_a, m_rnn_b_a, m_rnn_w_i, m_rnn_b_i, m_rnn_lambda, m_rnn_w_out, v_ln_g, v_ln_b, v_attn_w_in, v_attn_b_f, v_attn_w_out, v_rnn_w_in, v_rnn_conv_w, v_rnn_conv_b, v_rnn_w_a, v_rnn_b_a, v_rnn_w_i, v_rnn_b_i, v_rnn_lambda, v_rnn_w_out):
    w_loc = dict(attn_w_in=attn_w_in, attn_w_out=attn_w_out, rnn_w_in=rnn_w_in, rnn_w_a=rnn_w_a,
                 rnn_w_i=rnn_w_i, rnn_w_out=rnn_w_out, rnn_conv_w=rnn_conv_w, rnn_conv_b=rnn_conv_b,
                 rnn_b_a=rnn_b_a, rnn_b_i=rnn_b_i, rnn_lambda=rnn_lambda)
    m_loc = dict(attn_w_in=m_attn_w_in, attn_w_out=m_attn_w_out, rnn_w_in=m_rnn_w_in,
                 rnn_w_a=m_rnn_w_a, rnn_w_i=m_rnn_w_i, rnn_w_out=m_rnn_w_out,
                 rnn_conv_w=m_rnn_conv_w, rnn_conv_b=m_rnn_conv_b, rnn_b_a=m_rnn_b_a,
                 rnn_b_i=m_rnn_b_i, rnn_lambda=m_rnn_lambda)
    v_loc = dict(attn_w_in=v_attn_w_in, attn_w_out=v_attn_w_out, rnn_w_in=v_rnn_w_in,
                 rnn_w_a=v_rnn_w_a, rnn_w_i=v_rnn_w_i, rnn_w_out=v_rnn_w_out,
                 rnn_conv_w=v_rnn_conv_w, rnn_conv_b=v_rnn_conv_b, rnn_b_a=v_rnn_b_a,
                 rnn_b_i=v_rnn_b_i, rnn_lambda=v_rnn_lambda)
    shapes = {k: tuple(a.shape) for k, a in w_loc.items()}
    T, D = x.shape[1], x.shape[2]
    n_f = attn_b_f.shape[1]
    tb = min(512, T)
    tt_rg = min(128, T)
    tt_ln = min(256, T)

    gathered = _ag_c(_ag_xy(_pack_local(w_loc, BF16, 16), "ag_w_xy"), "ag_w_c")
    W = _unpack_gathered(gathered, shapes)
    w_in_a = jnp.pad(W["attn_w_in"], ((0, 0), (0, 0), (0, LANES - n_f)))
    small_r = jnp.concatenate([W["rnn_conv_w"], W["rnn_conv_b"][:, None], W["rnn_b_a"][:, None],
                               W["rnn_b_i"][:, None], W["rnn_lambda"][:, None]], axis=1)
    bf_rows = jnp.pad(attn_b_f, ((0, 0), (0, LANES - n_f)))[:, None, :]

    xs, saved = [x[0]], []
    for layer in range(DEPTH):
        idx, xl = layer // 2, xs[-1]
        if layer % 2 == 0:
            proj = _matmul(xl, w_in_a[idx], trans_b=False, tm=512, tn=1408, name=f"a_proj{layer}")
            cum_t = _cumsum_fwd(proj, bf_rows[idx], tt=min(512, T), name=f"a_cum{layer}")
            cum4 = cum_t[:N_HEADS].reshape(N_PAIRS, 2, T // tb, tb)
            o, og, lp = _flash_fwd(proj, cum4, tb=tb, name=f"a_fwd{layer}")
            hbr = _matmul(og, W["attn_w_out"][idx], trans_b=False, tm=512, tn=1024,
                          name=f"a_out{layer}")
            saved.append((proj, cum4, o, og, lp))
        else:
            proj = _matmul(xl, W["rnn_w_in"][idx], trans_b=False, tm=512, tn=1024,
                           name=f"r_proj{layer}")
            hs, yr = _rg_fwd(proj, small_r[idx], W["rnn_w_a"][idx], W["rnn_w_i"][idx],
                             tt=tt_rg, name=f"r_fwd{layer}")
            hbr = _matmul(yr, W["rnn_w_out"][idx], trans_b=False, tm=512, tn=1024,
                          name=f"r_out{layer}")
            saved.append((proj, hs, yr))
        y, zh, rstd = _ln_fwd(xl, hbr, ln_g[layer][None], ln_b[layer][None], tt=tt_ln,
                              name=f"ln_fwd{layer}")
        saved[-1] = saved[-1] + (zh, rstd)
        xs.append(y)

    loss_lanes, dy = _loss(xs[-1], loss_target[0], tt=tt_ln, name="loss")
    loss = lax.psum(jnp.sum(loss_lanes), ("x", "y", "c"))

    full_g = {k: [None, None] for k in w_loc}
    d_ln_g, d_ln_b, d_bf = [None] * DEPTH, [None] * DEPTH, [None, None]
    for layer in reversed(range(DEPTH)):
        idx, xl = layer // 2, xs[layer]
        zh, rstd = saved[layer][-2:]
        dz, dg, db = _ln_bwd(dy, zh, rstd, ln_g[layer][None], tt=tt_ln, name=f"ln_bwd{layer}")
        d_ln_g[layer], d_ln_b[layer] = dg[0], db[0]
        if layer % 2 == 0:
            proj, cum4, o, og, lp = saved[layer][:5]
            dog = _matmul(dz, W["attn_w_out"][idx], trans_b=True, tm=512, tn=1024,
                          name=f"a_dog{layer}")
            full_g["attn_w_out"][idx] = _matmul_tn(og, dz, tm=512, tn=1024, tk=512,
                                                   name=f"a_dwo{layer}")
            dq, dgate, do, delta, dcum_q = _flash_bwd_dq(proj, cum4, o, dog, lp, tb=tb,
                                                         name=f"a_dq{layer}")
            dk, dv, dcum_k = _flash_bwd_dkv(proj, cum4, do, lp, delta, tb=tb,
                                            name=f"a_dkv{layer}")
            dcum_t = (dcum_q + dcum_k).transpose(0, 2, 1, 3).reshape(N_HEADS, T)
            dcum_t = jnp.pad(dcum_t, ((0, LANES - N_HEADS), (0, 0)))
            df, dbf = _cumsum_bwd(dcum_t, proj, bf_rows[idx], tt=min(512, T), name=f"a_dcum{layer}")
            d_bf[idx] = dbf[0, :n_f]
            dproj = jnp.concatenate([dq, dk, dv, dgate, df], axis=1)
            dwi = _matmul_tn(xl, dproj, tm=512, tn=1408, tk=512, name=f"a_dwi{layer}")
            full_g["attn_w_in"][idx] = dwi[:, :4 * D + n_f]
            dy = _matmul(dproj, w_in_a[idx], trans_b=True, tm=256, tn=512, name=f"a_dx{layer}",
                         add=dz, add_scale=ALPHA)
        else:
            proj, hs, yr = saved[layer][:3]
            dyr = _matmul(dz, W["rnn_w_out"][idx], trans_b=True, tm=512, tn=1024,
                          name=f"r_dy{layer}")
            full_g["rnn_w_out"][idx] = _matmul_tn(yr, dz, tm=512, tn=1024, tk=512,
                                                  name=f"r_dwo{layer}")
            dproj, dwa, dwi_, dsm = _rg_bwd(proj, hs, dyr, small_r[idx], W["rnn_w_a"][idx],
                                            W["rnn_w_i"][idx], tt=tt_rg, name=f"r_bwd{layer}")
            full_g["rnn_w_a"][idx], full_g["rnn_w_i"][idx] = dwa, dwi_
            full_g["rnn_conv_w"][idx] = dsm[0:4]
            for r, k in enumerate(("rnn_conv_b", "rnn_b_a", "rnn_b_i", "rnn_lambda")):
                full_g[k][idx] = dsm[4 + r]
            full_g["rnn_w_in"][idx] = _matmul_tn(xl, dproj, tm=512, tn=1024, tk=512,
                                                 name=f"r_dwi{layer}")
            dy = _matmul(dproj, W["rnn_w_in"][idx], trans_b=True, tm=512, tn=512,
                         name=f"r_dx{layer}", add=dz, add_scale=ALPHA)
    grad_x = dy[None]

    G = _pack_grads({k: jnp.stack(v) for k, v in full_g.items()}, shapes, 8)
    R = G.shape[2]
    c_idx = lax.axis_index("c").astype(jnp.int32).reshape(1)
    me_idx = (2 * lax.axis_index("x") + lax.axis_index("y")).astype(jnp.int32).reshape(1)
    G2 = G.reshape(2, 4 * R, PACK_C)
    H = _add_own(G2, _rs_c(G2, "rs_c"), c_idx, tr=_row_tile(4 * R, 1024), name="rs_add")
    H = H.reshape(4, R, PACK_C)
    g_f, d_f, nm_f, nv_f = _adamw_shard(H, _rs_xy(H, "rs_xy"), me_idx, _pack_local(w_loc, F32, 8),
                                        _pack_local(m_loc, F32, 8), _pack_local(v_loc, F32, 8),
                                        tr=_row_tile(R, 320), name="adamw")
    g_sh, d_sh = _unpack_local(g_f, shapes), _unpack_local(d_f, shapes)
    nm_sh, nv_sh = _unpack_local(nm_f, shapes), _unpack_local(nv_f, shapes)

    def rep_pack(lg, lb, bf):
        rows = jnp.concatenate([lg, lb, jnp.pad(bf.reshape(1, -1), ((0, 0), (0, D - 2 * n_f)))])
        return jnp.pad(rows, ((0, 16 - rows.shape[0]), (0, 0)))

    rep = _all_gather(rep_pack(jnp.stack(d_ln_g), jnp.stack(d_ln_b), jnp.stack(d_bf)), "ag_rep")
    rg, rd, rm, rv = _adamw(rep.reshape(8, 16, D), rep_pack(ln_g, ln_b, attn_b_f),
                            rep_pack(m_ln_g, m_ln_b, m_attn_b_f),
                            rep_pack(v_ln_g, v_ln_b, v_attn_b_f), tr=16, name="adamw_rep")

    def rep_unpack(a):
        return dict(ln_g=a[0:DEPTH], ln_b=a[DEPTH:2 * DEPTH],
                    attn_b_f=a[2 * DEPTH, :2 * n_f].reshape(2, n_f))

    order = ("ln_g", "ln_b", "attn_w_in", "attn_b_f", "attn_w_out", "rnn_w_in", "rnn_conv_w",
             "rnn_conv_b", "rnn_w_a", "rnn_b_a", "rnn_w_i", "rnn_b_i", "rnn_lambda", "rnn_w_out")
    outs = [loss, grad_x]
    for sh, rp in ((g_sh, rg), (d_sh, rd), (nm_sh, rm), (nv_sh, rv)):
        allp = {**sh, **rep_unpack(rp)}
        outs.extend(allp[k] for k in order)
    return tuple(outs)
```

```python
import functools

import jax
import jax.numpy as jnp
from jax import lax
from jax.experimental import pallas as pl
from jax.experimental.pallas import tpu as pltpu

F32 = jnp.float32
BF16 = jnp.bfloat16

DEPTH = 4
N_HEADS = 16
HEAD_DIM = 64
N_PAIRS = N_HEADS // 2
RNN_BLOCKS = 4
RNN_BLOCK_WIDTH = 256
CONV_WIDTH = 4
LRU_C = 8.0
ALPHA = (2.0 * DEPTH) ** 0.25
LN_EPS = 1e-5
ADAM_LR, ADAM_B1, ADAM_B2, ADAM_EPS, ADAM_WD, ADAM_STEP = 0.001, 0.9, 0.999, 1e-8, 0.01, 10

LANES = 128
SUBLANES = 8
VMEM_LIMIT = 48 * 1024 * 1024

MESH = pl.DeviceIdType.MESH
HBM_SPEC = pl.BlockSpec(memory_space=pltpu.HBM)


def _cparams(*sem):
    return pltpu.CompilerParams(dimension_semantics=sem, vmem_limit_bytes=VMEM_LIMIT)


def _sigmoid(x):
    return 1.0 / (1.0 + jnp.exp(-x))


def _softplus(x):
    return jnp.maximum(x, 0.0) + jnp.log(1.0 + jnp.exp(-jnp.abs(x)))


def _a2a(src, *, group, bcast, name):
    n = 2 if group == "c" else 4
    blk = tuple(src.shape) if bcast else tuple(src.shape[1:])

    def body(src_ref, out_ref, send_sems, recv_sems, local_sem):
        x, y, c = lax.axis_index("x"), lax.axis_index("y"), lax.axis_index("c")
        if group == "c":
            me = c

            def peer(d):
                return (x, y, 1 - c), 1 - c
        else:
            me = 2 * x + y

            def peer(d):
                px, py = x ^ (d >> 1), y ^ (d & 1)
                return (px, py, c), 2 * px + py

        def block_for(k):
            return src_ref if bcast else src_ref.at[k]

        local = pltpu.make_async_copy(block_for(me), out_ref.at[me], local_sem)
        local.start()
        sends = []
        for d in range(1, n):
            dev, idx = peer(d)
            cp = pltpu.make_async_remote_copy(
                src_ref=block_for(idx), dst_ref=out_ref.at[me],
                send_sem=send_sems.at[d], recv_sem=recv_sems.at[d],
                device_id=dev, device_id_type=MESH)
            cp.start()
            sends.append(cp)
        for d in range(1, n):
            dev, idx = peer(d)
            pltpu.make_async_remote_copy(
                src_ref=block_for(idx), dst_ref=out_ref.at[idx],
                send_sem=send_sems.at[d], recv_sem=recv_sems.at[d],
                device_id=dev, device_id_type=MESH).wait_recv()
        for cp in sends:
            cp.wait_send()
        local.wait()

    return pl.pallas_call(
        body, name=name,
        out_shape=jax.ShapeDtypeStruct((n,) + blk, src.dtype),
        in_specs=[HBM_SPEC], out_specs=HBM_SPEC,
        scratch_shapes=[pltpu.SemaphoreType.DMA((n,)), pltpu.SemaphoreType.DMA((n,)),
                        pltpu.SemaphoreType.DMA],
    )(src)


def _all_gather(piece, name):
    return _a2a(_a2a(piece, group="xy", bcast=True, name=name + "_xy"),
                group="c", bcast=True, name=name + "_c")


D2D_CHUNKS = 16
ICI_CHUNKS = 8


def _row_chunks(rows, dtype, k):
    unit = SUBLANES * (4 // jnp.dtype(dtype).itemsize)
    assert rows % unit == 0
    units = rows // unit
    k = max(1, min(k, units))
    base, rem = divmod(units, k)
    out, r = [], 0
    for i in range(k):
        n = (base + (1 if i < rem else 0)) * unit
        out.append((r, n))
        r += n
    return out


def _chunks(shape, dtype, k):
    if len(shape) == 2:
        return [(pl.ds(r0, n),) for r0, n in _row_chunks(shape[0], dtype, k)]
    per = max(1, k // shape[0])
    return [(l, pl.ds(r0, n)) for l in range(shape[0]) for r0, n in _row_chunks(shape[1], dtype, per)]


def _mesh_place():
    x, y, c = lax.axis_index("x"), lax.axis_index("y"), lax.axis_index("c")
    return x, y, c, 2 * x + y


def _chip_peer(x, y, c, d):
    px, py = x ^ (d >> 1), y ^ (d & 1)
    return (px, py, c), 2 * px + py


def _remote(src, dst, send_sem, recv_sem, dev):
    return pltpu.make_async_remote_copy(src_ref=src, dst_ref=dst, send_sem=send_sem,
                                        recv_sem=recv_sem, device_id=dev, device_id_type=MESH)


def _comm_call(body, name, ins, out_shapes, n_sems, aliases=None):
    n = len(ins)
    return pl.pallas_call(
        body, name=name,
        out_shape=out_shapes, in_specs=[HBM_SPEC] * n, out_specs=[HBM_SPEC] * n,
        input_output_aliases=aliases or {},
        scratch_shapes=[pltpu.SemaphoreType.DMA((n_sems, n)), pltpu.SemaphoreType.DMA((n_sems, n))],
    )(*ins)


def _ag_xy(pieces, name):
    n = len(pieces)
    chunks = [_chunks(p.shape, p.dtype, ICI_CHUNKS) for p in pieces]

    def body(*refs):
        srcs, outs, send_sems, recv_sems = refs[:n], refs[n:2 * n], refs[2 * n], refs[2 * n + 1]
        x, y, c, me = _mesh_place()
        for o in range(n):
            for idx in chunks[o]:
                pltpu.make_async_copy(srcs[o].at[idx], outs[o].at[(c, me) + idx],
                                      send_sems.at[0, o]).start()
        for d in range(1, 4):
            dev, _ = _chip_peer(x, y, c, d)
            for o in range(n):
                for idx in chunks[o]:
                    _remote(srcs[o].at[idx], outs[o].at[(c, me) + idx],
                            send_sems.at[d, o], recv_sems.at[d, o], dev).start()
        for d in range(1, 4):
            dev, pidx = _chip_peer(x, y, c, d)
            for o in range(n):
                _remote(srcs[o], outs[o].at[c, pidx], send_sems.at[d, o], recv_sems.at[d, o],
                        dev).wait_recv()
        for d in range(1, 4):
            dev, pidx = _chip_peer(x, y, c, d)
            for o in range(n):
                _remote(srcs[o], outs[o].at[c, pidx], send_sems.at[d, o], recv_sems.at[d, o],
                        dev).wait_send()
        for o in range(n):
            pltpu.make_async_copy(srcs[o], outs[o].at[c, me], send_sems.at[0, o]).wait()

    shapes = [jax.ShapeDtypeStruct((2, 4) + tuple(p.shape), p.dtype) for p in pieces]
    return _comm_call(body, name, pieces, shapes, 4)


def _ag_c(bufs, name):
    n = len(bufs)
    chunks = [_chunks(b.shape[2:], b.dtype, D2D_CHUNKS // 4) for b in bufs]

    def body(*refs):
        srcs, outs, send_sems, recv_sems = refs[:n], refs[n:2 * n], refs[2 * n], refs[2 * n + 1]
        x, y, c, _ = _mesh_place()
        sib = (x, y, 1 - c)
        for o in range(n):
            for k in range(4):
                for idx in chunks[o]:
                    _remote(srcs[o].at[(c, k) + idx], outs[o].at[(c, k) + idx],
                            send_sems.at[0, o], recv_sems.at[0, o], sib).start()
        for o in range(n):
            _remote(srcs[o].at[c], outs[o].at[1 - c], send_sems.at[0, o], recv_sems.at[0, o],
                    sib).wait_recv()
        for o in range(n):
            _remote(srcs[o].at[c], outs[o].at[1 - c], send_sems.at[0, o], recv_sems.at[0, o],
                    sib).wait_send()

    shapes = [jax.ShapeDtypeStruct(b.shape, b.dtype) for b in bufs]
    return _comm_call(body, name, bufs, shapes, 1, aliases={i: i for i in range(n)})


def _rs_c(gs, name):
    n = len(gs)
    chunks = [_chunks(g.shape[2:], g.dtype, max(1, D2D_CHUNKS // g.shape[1])) for g in gs]

    def body(*refs):
        srcs, outs, send_sems, recv_sems = refs[:n], refs[n:2 * n], refs[2 * n], refs[2 * n + 1]
        x, y, c, _ = _mesh_place()
        sib = (x, y, 1 - c)
        for o in range(n):
            for k in range(gs[o].shape[1]):
                for idx in chunks[o]:
                    _remote(srcs[o].at[(1 - c, k) + idx], outs[o].at[(k,) + idx],
                            send_sems.at[0, o], recv_sems.at[0, o], sib).start()
        for o in range(n):
            _remote(srcs[o].at[1 - c], outs[o], send_sems.at[0, o], recv_sems.at[0, o],
                    sib).wait_recv()
        for o in range(n):
            _remote(srcs[o].at[1 - c], outs[o], send_sems.at[0, o], recv_sems.at[0, o],
                    sib).wait_send()

    shapes = [jax.ShapeDtypeStruct(g.shape[1:], g.dtype) for g in gs]
    return _comm_call(body, name, gs, shapes, 1)


def _rs_xy(hs, name):
    n = len(hs)
    chunks = [_chunks(h.shape[1:], h.dtype, ICI_CHUNKS) for h in hs]

    def body(*refs):
        srcs, outs, send_sems, recv_sems = refs[:n], refs[n:2 * n], refs[2 * n], refs[2 * n + 1]
        x, y, c, me = _mesh_place()
        for d in range(1, 4):
            dev, pidx = _chip_peer(x, y, c, d)
            for o in range(n):
                for idx in chunks[o]:
                    _remote(srcs[o].at[(pidx,) + idx], outs[o].at[(me,) + idx],
                            send_sems.at[d, o], recv_sems.at[d, o], dev).start()
        for d in range(1, 4):
            dev, pidx = _chip_peer(x, y, c, d)
            for o in range(n):
                _remote(srcs[o].at[pidx], outs[o].at[pidx], send_sems.at[d, o], recv_sems.at[d, o],
                        dev).wait_recv()
        for d in range(1, 4):
            dev, pidx = _chip_peer(x, y, c, d)
            for o in range(n):
                _remote(srcs[o].at[pidx], outs[o].at[pidx], send_sems.at[d, o], recv_sems.at[d, o],
                        dev).wait_send()

    shapes = [jax.ShapeDtypeStruct(h.shape, h.dtype) for h in hs]
    return _comm_call(body, name, hs, shapes, 4)


def _matmul(a, b, *, trans_b, tm, tn, name, add=None, add_scale=1.0):
    M, K = a.shape
    N = b.shape[0] if trans_b else b.shape[1]
    tm, tn = min(tm, M), min(tn, N)
    assert M % tm == 0 and N % tn == 0
    dn = (((1,), (1,)), ((), ())) if trans_b else (((1,), (0,)), ((), ()))

    def body(*refs):
        if add is None:
            a_ref, b_ref, o_ref = refs
        else:
            a_ref, b_ref, add_ref, o_ref = refs
        r = lax.dot_general(a_ref[...].astype(BF16), b_ref[...].astype(BF16), dn,
                            preferred_element_type=F32)
        if add is not None:
            r = r + add_scale * add_ref[...]
        o_ref[...] = r

    b_spec = (pl.BlockSpec((tn, K), lambda j, i: (j, 0)) if trans_b
              else pl.BlockSpec((K, tn), lambda j, i: (0, j)))
    in_specs = [pl.BlockSpec((tm, K), lambda j, i: (i, 0)), b_spec]
    args = [a, b]
    if add is not None:
        in_specs.append(pl.BlockSpec((tm, tn), lambda j, i: (i, j)))
        args.append(add)
    return pl.pallas_call(
        body, name=name, grid=(N // tn, M // tm),
        in_specs=in_specs, out_specs=pl.BlockSpec((tm, tn), lambda j, i: (i, j)),
        out_shape=jax.ShapeDtypeStruct((M, N), F32),
        compiler_params=_cparams("parallel", "parallel"),
    )(*args)


def _matmul_tn(a, b, *, tm, tn, tk, name):
    T, M = a.shape
    N = b.shape[1]
    tm, tn, tk = min(tm, M), min(tn, N), min(tk, T)
    assert M % tm == 0 and N % tn == 0 and T % tk == 0

    def body(a_ref, b_ref, o_ref):
        @pl.when(pl.program_id(2) == 0)
        def _():
            o_ref[...] = jnp.zeros_like(o_ref)

        o_ref[...] += lax.dot_general(a_ref[...].astype(BF16), b_ref[...].astype(BF16),
                                      (((0,), (0,)), ((), ())), preferred_element_type=F32)

    return pl.pallas_call(
        body, name=name, grid=(M // tm, N // tn, T // tk),
        in_specs=[pl.BlockSpec((tk, tm), lambda i, j, k: (k, i)),
                  pl.BlockSpec((tk, tn), lambda i, j, k: (k, j))],
        out_specs=pl.BlockSpec((tm, tn), lambda i, j, k: (i, j)),
        out_shape=jax.ShapeDtypeStruct((M, N), F32),
        compiler_params=_cparams("parallel", "parallel", "arbitrary"),
    )(a, b)


def _head_masks(rows):
    lane = lax.broadcasted_iota(jnp.int32, (rows, LANES), 1)
    return lane < HEAD_DIM, lane >= HEAD_DIM


def _causal(i_q, i_k, tq, tk):
    row = i_q * tq + lax.broadcasted_iota(jnp.int32, (tq, tk), 0)
    col = i_k * tk + lax.broadcasted_iota(jnp.int32, (tq, tk), 1)
    return row >= col


def _flash_fwd(proj, cum4, *, tb, name):
    T = proj.shape[0]
    D = N_HEADS * HEAD_DIM
    nb = T // tb
    cb = D // LANES

    def body(q_ref, k_ref, v_ref, g_ref, cum_ref, o_ref, og_ref, lp_ref, kb_ref, vb_ref):
        i = pl.program_id(1)

        @pl.when(i == 0)
        def _():
            kb_ref[...] = k_ref[...].astype(BF16)
            vb_ref[...] = v_ref[...].astype(BF16)

        q = q_ref[...] * (HEAD_DIM ** -0.5)
        masks = _head_masks(tb)
        qh = [jnp.where(masks[h], q, 0.0).astype(BF16) for h in range(2)]
        cref = [cum_ref[0, h, pl.ds(i, 1), :][:, 0:1] for h in range(2)]

        def step(kbi, carry, masked):
            k0 = pl.multiple_of(kbi * tb, tb)
            kblk = kb_ref[pl.ds(k0, tb), :]
            vblk = vb_ref[pl.ds(k0, tb), :]
            new = []
            for h in range(2):
                m, l, acc = carry[h]
                s = lax.dot_general(qh[h], kblk, (((1,), (1,)), ((), ())),
                                    preferred_element_type=F32)
                s = s + (cref[h] - cum_ref[0, h, pl.ds(kbi, 1), :])
                if masked:
                    s = jnp.where(_causal(i, kbi, tb, tb), s, -jnp.inf)
                m_new = jnp.maximum(m, jnp.max(s, axis=-1, keepdims=True))
                alpha = jnp.exp(m - m_new)
                p = jnp.exp(s - m_new)
                l = alpha * l + jnp.sum(p, axis=-1, keepdims=True)
                acc = alpha * acc + jnp.dot(p.astype(BF16), vblk, preferred_element_type=F32)
                new.append((m_new, l, acc))
            return tuple(new)

        init1 = (jnp.full((tb, 1), -jnp.inf, F32), jnp.zeros((tb, 1), F32),
                 jnp.zeros((tb, LANES), F32))
        carry = lax.fori_loop(0, i, lambda kbi, c: step(kbi, c, False), (init1, init1))
        outs = []
        for h, (m, l, acc) in enumerate(step(i, carry, True)):
            outs.append(acc / l)
            lp_ref[h] = jnp.broadcast_to(m + jnp.log(l) - cref[h], (tb, LANES))
        o = jnp.where(masks[0], outs[0], outs[1])
        o_ref[...] = o
        gate = g_ref[...]
        og_ref[...] = o * (gate * _sigmoid(gate))

    return pl.pallas_call(
        body, name=name, grid=(N_PAIRS, nb),
        in_specs=[pl.BlockSpec((tb, LANES), lambda j, i: (i, j)),
                  pl.BlockSpec((T, LANES), lambda j, i: (0, cb + j)),
                  pl.BlockSpec((T, LANES), lambda j, i: (0, 2 * cb + j)),
                  pl.BlockSpec((tb, LANES), lambda j, i: (i, 3 * cb + j)),
                  pl.BlockSpec((1, 2, nb, tb), lambda j, i: (j, 0, 0, 0))],
        out_specs=[pl.BlockSpec((tb, LANES), lambda j, i: (i, j)),
                   pl.BlockSpec((tb, LANES), lambda j, i: (i, j)),
                   pl.BlockSpec((2, tb, LANES), lambda j, i: (j, i, 0))],
        out_shape=[jax.ShapeDtypeStruct((T, D), F32), jax.ShapeDtypeStruct((T, D), F32),
                   jax.ShapeDtypeStruct((N_HEADS, T, LANES), F32)],
        scratch_shapes=[pltpu.VMEM((T, LANES), BF16), pltpu.VMEM((T, LANES), BF16)],
        compiler_params=_cparams("parallel", "arbitrary"),
    )(proj, proj, proj, proj, cum4)


def _flash_bwd_dq(proj, cum4, o, dog, lp, *, tb, name):
    T = proj.shape[0]
    D = N_HEADS * HEAD_DIM
    nb = T // tb
    cb = D // LANES

    def body(q_ref, k_ref, v_ref, g_ref, cum_ref, o_ref, dog_ref, lp_ref,
             dq_ref, dg_ref, do_ref, dl_ref, dc_ref, kb_ref, vb_ref):
        i = pl.program_id(1)

        @pl.when(i == 0)
        def _():
            kb_ref[...] = k_ref[...].astype(BF16)
            vb_ref[...] = v_ref[...].astype(BF16)

        gate = g_ref[...]
        sg = _sigmoid(gate)
        o = o_ref[...]
        dog = dog_ref[...]
        do = dog * (gate * sg)
        dg_ref[...] = dog * o * (sg * (1.0 + gate * (1.0 - sg)))
        do_ref[...] = do.astype(BF16)
        q = q_ref[...] * (HEAD_DIM ** -0.5)
        masks = _head_masks(tb)
        qh = [jnp.where(masks[h], q, 0.0).astype(BF16) for h in range(2)]
        doh = [jnp.where(masks[h], do, 0.0).astype(BF16) for h in range(2)]
        delta = [jnp.sum(jnp.where(masks[h], do * o, 0.0), axis=-1, keepdims=True) for h in range(2)]
        lph = [lp_ref[h][:, 0:1] for h in range(2)]
        for h in range(2):
            dl_ref[h] = jnp.broadcast_to(delta[h], (tb, LANES))

        def step(kbi, carry, masked):
            k0 = pl.multiple_of(kbi * tb, tb)
            kblk = kb_ref[pl.ds(k0, tb), :]
            vblk = vb_ref[pl.ds(k0, tb), :]
            new = []
            for h in range(2):
                acc, rs = carry[h]
                s = lax.dot_general(qh[h], kblk, (((1,), (1,)), ((), ())), preferred_element_type=F32)
                p = jnp.exp(s - cum_ref[0, h, pl.ds(kbi, 1), :] - lph[h])
                if masked:
                    p = jnp.where(_causal(i, kbi, tb, tb), p, 0.0)
                dp = lax.dot_general(doh[h], vblk, (((1,), (1,)), ((), ())),
                                     preferred_element_type=F32)
                ds = p * (dp - delta[h])
                new.append((acc + jnp.dot(ds.astype(BF16), kblk, preferred_element_type=F32),
                            rs + jnp.sum(ds, axis=-1, keepdims=True)))
            return tuple(new)

        init1 = (jnp.zeros((tb, LANES), F32), jnp.zeros((tb, 1), F32))
        carry = lax.fori_loop(0, i, lambda kbi, c: step(kbi, c, False), (init1, init1))
        dqs = []
        for h, (acc, rs) in enumerate(step(i, carry, True)):
            dqs.append(acc)
            dc_ref[0, 0, pl.ds(h, 1), :] = jnp.broadcast_to(rs, (tb, LANES)).T[0:1, :]
        dq_ref[...] = jnp.where(masks[0], dqs[0], dqs[1]) * (HEAD_DIM ** -0.5)

    blk = pl.BlockSpec((tb, LANES), lambda j, i: (i, j))
    stat = pl.BlockSpec((2, tb, LANES), lambda j, i: (j, i, 0))
    return pl.pallas_call(
        body, name=name, grid=(N_PAIRS, nb),
        in_specs=[blk,
                  pl.BlockSpec((T, LANES), lambda j, i: (0, cb + j)),
                  pl.BlockSpec((T, LANES), lambda j, i: (0, 2 * cb + j)),
                  pl.BlockSpec((tb, LANES), lambda j, i: (i, 3 * cb + j)),
                  pl.BlockSpec((1, 2, nb, tb), lambda j, i: (j, 0, 0, 0)),
                  blk, blk, stat],
        out_specs=[blk, blk, blk, stat, pl.BlockSpec((1, 1, 2, tb), lambda j, i: (j, i, 0, 0))],
        out_shape=[jax.ShapeDtypeStruct((T, D), F32), jax.ShapeDtypeStruct((T, D), F32),
                   jax.ShapeDtypeStruct((T, D), BF16),
                   jax.ShapeDtypeStruct((N_HEADS, T, LANES), F32),
                   jax.ShapeDtypeStruct((N_PAIRS, nb, 2, tb), F32)],
        scratch_shapes=[pltpu.VMEM((T, LANES), BF16), pltpu.VMEM((T, LANES), BF16)],
        compiler_params=_cparams("parallel", "arbitrary"),
    )(proj, proj, proj, proj, cum4, o, dog, lp)


def _flash_bwd_dkv(proj, cum4, do, lp, delta, *, tb, name):
    T = proj.shape[0]
    D = N_HEADS * HEAD_DIM
    nb = T // tb
    cb = D // LANES

    def body(q_ref, k_ref, v_ref, cum_ref, do_ref, lp_ref, dl_ref, dk_ref, dv_ref, dc_ref):
        kbi = pl.program_id(1)
        k = k_ref[...] * (HEAD_DIM ** -0.5)
        v = v_ref[...]
        masks = _head_masks(tb)
        kh = [jnp.where(masks[h], k, 0.0).astype(BF16) for h in range(2)]
        vh = [jnp.where(masks[h], v, 0.0).astype(BF16) for h in range(2)]
        ck = [cum_ref[0, h, pl.ds(kbi, 1), :] for h in range(2)]

        def step(i, carry, masked):
            q0 = pl.multiple_of(i * tb, tb)
            qb = q_ref[pl.ds(q0, tb), :].astype(BF16)
            dob = do_ref[pl.ds(q0, tb), :]
            new = []
            for h in range(2):
                dk, dv, dc = carry[h]
                s = lax.dot_general(qb, kh[h], (((1,), (1,)), ((), ())), preferred_element_type=F32)
                p = jnp.exp(s - ck[h] - lp_ref[h, pl.ds(q0, tb), :][:, 0:1])
                if masked:
                    p = jnp.where(_causal(i, kbi, tb, tb), p, 0.0)
                dp = lax.dot_general(dob, vh[h], (((1,), (1,)), ((), ())), preferred_element_type=F32)
                ds = p * (dp - dl_ref[h, pl.ds(q0, tb), :][:, 0:1])
                dv = dv + lax.dot_general(p.astype(BF16), dob, (((0,), (0,)), ((), ())),
                                          preferred_element_type=F32)
                dk = dk + lax.dot_general(ds.astype(BF16), qb, (((0,), (0,)), ((), ())),
                                          preferred_element_type=F32)
                new.append((dk, dv, dc - jnp.sum(ds, axis=0, keepdims=True)))
            return tuple(new)

        init1 = (jnp.zeros((tb, LANES), F32), jnp.zeros((tb, LANES), F32), jnp.zeros((1, tb), F32))
        carry = step(kbi, (init1, init1), True)
        carry = lax.fori_loop(kbi + 1, nb, lambda i, c: step(i, c, False), carry)
        dks, dvs = [], []
        for h, (dk, dv, dc) in enumerate(carry):
            dks.append(dk)
            dvs.append(dv)
            dc_ref[0, 0, pl.ds(h, 1), :] = dc
        dk_ref[...] = jnp.where(masks[0], dks[0], dks[1]) * (HEAD_DIM ** -0.5)
        dv_ref[...] = jnp.where(masks[0], dvs[0], dvs[1])

    full = pl.BlockSpec((T, LANES), lambda j, i: (0, j))
    stat = pl.BlockSpec((2, T, LANES), lambda j, i: (j, 0, 0))
    blk = pl.BlockSpec((tb, LANES), lambda j, i: (i, j))
    return pl.pallas_call(
        body, name=name, grid=(N_PAIRS, nb),
        in_specs=[full,
                  pl.BlockSpec((tb, LANES), lambda j, i: (i, cb + j)),
                  pl.BlockSpec((tb, LANES), lambda j, i: (i, 2 * cb + j)),
                  pl.BlockSpec((1, 2, nb, tb), lambda j, i: (j, 0, 0, 0)),
                  full, stat, stat],
        out_specs=[blk, blk, pl.BlockSpec((1, 1, 2, tb), lambda j, i: (j, i, 0, 0))],
        out_shape=[jax.ShapeDtypeStruct((T, D), F32), jax.ShapeDtypeStruct((T, D), F32),
                   jax.ShapeDtypeStruct((N_PAIRS, nb, 2, tb), F32)],
        compiler_params=_cparams("parallel", "arbitrary"),
    )(proj, proj, proj, cum4, do, lp, delta)


def _cumsum_fwd(proj, bf_row, *, tt, name):
    T = proj.shape[0]
    cb = (proj.shape[1] - LANES) // LANES

    def body(f_ref, b_ref, out_ref, carry_ref):
        i = pl.program_id(0)

        @pl.when(i == 0)
        def _():
            carry_ref[...] = jnp.zeros_like(carry_ref)

        ls = -_softplus(-(f_ref[...] + b_ref[...]))
        tri = (lax.broadcasted_iota(jnp.int32, (tt, tt), 0)
               >= lax.broadcasted_iota(jnp.int32, (tt, tt), 1)).astype(F32)
        cum = jnp.dot(tri, ls, preferred_element_type=F32,
                      precision=lax.Precision.HIGHEST) + carry_ref[...]
        carry_ref[...] = cum[tt - 1:tt, :]
        out_ref[...] = cum.T

    return pl.pallas_call(
        body, name=name, grid=(T // tt,),
        in_specs=[pl.BlockSpec((tt, LANES), lambda i: (i, cb)),
                  pl.BlockSpec((1, LANES), lambda i: (0, 0))],
        out_specs=pl.BlockSpec((LANES, tt), lambda i: (0, i)),
        out_shape=jax.ShapeDtypeStruct((LANES, T), F32),
        scratch_shapes=[pltpu.VMEM((1, LANES), F32)],
        compiler_params=_cparams("arbitrary"),
    )(proj, bf_row)


def _cumsum_bwd(dcum_t, proj, bf_row, *, tt, name):
    T = proj.shape[0]
    cb = (proj.shape[1] - LANES) // LANES
    nt = T // tt

    def body(dc_ref, f_ref, b_ref, df_ref, db_ref, carry_ref):
        i = pl.program_id(0)

        @pl.when(i == 0)
        def _():
            carry_ref[...] = jnp.zeros_like(carry_ref)
            db_ref[...] = jnp.zeros_like(db_ref)

        dc = dc_ref[...].T
        tri = (lax.broadcasted_iota(jnp.int32, (tt, tt), 0)
               <= lax.broadcasted_iota(jnp.int32, (tt, tt), 1)).astype(F32)
        rev = jnp.dot(tri, dc, preferred_element_type=F32,
                      precision=lax.Precision.HIGHEST) + carry_ref[...]
        carry_ref[...] = rev[0:1, :]
        df = rev * _sigmoid(-(f_ref[...] + b_ref[...]))
        df_ref[...] = df
        db_ref[...] += jnp.sum(df, axis=0, keepdims=True)

    return pl.pallas_call(
        body, name=name, grid=(nt,),
        in_specs=[pl.BlockSpec((LANES, tt), lambda i: (0, nt - 1 - i)),
                  pl.BlockSpec((tt, LANES), lambda i: (nt - 1 - i, cb)),
                  pl.BlockSpec((1, LANES), lambda i: (0, 0))],
        out_specs=[pl.BlockSpec((tt, LANES), lambda i: (nt - 1 - i, 0)),
                   pl.BlockSpec((1, LANES), lambda i: (0, 0))],
        out_shape=[jax.ShapeDtypeStruct((T, LANES), F32), jax.ShapeDtypeStruct((1, LANES), F32)],
        scratch_shapes=[pltpu.VMEM((1, LANES), F32)],
        compiler_params=_cparams("arbitrary"),
    )(dcum_t, proj, bf_row)


def _rg_gates(upad_ref, small_ref, wa_ref, wi_ref, tt):
    off = SUBLANES - (CONV_WIDTH - 1)
    u = small_ref[4:5, :]
    for tap in range(CONV_WIDTH):
        u = u + upad_ref[off + tap:off + tap + tt, :] * small_ref[tap:tap + 1, :]
    pa, pi = [], []
    for n in range(RNN_BLOCKS):
        ub = u[:, n * RNN_BLOCK_WIDTH:(n + 1) * RNN_BLOCK_WIDTH].astype(BF16)
        pa.append(jnp.dot(ub, wa_ref[n], preferred_element_type=F32))
        pi.append(jnp.dot(ub, wi_ref[n], preferred_element_type=F32))
    r = _sigmoid(jnp.concatenate(pa, axis=-1) + small_ref[5:6, :])
    ig = _sigmoid(jnp.concatenate(pi, axis=-1) + small_ref[6:7, :])
    spl = _softplus(-small_ref[7:8, :])
    log_a = (-LRU_C) * r * spl
    a = jnp.exp(log_a)
    s = jnp.sqrt(jnp.tanh(-log_a) * (a * a + 1.0))
    return u, r, ig, spl, a, s


def _rg_fwd(proj, small, wa, wi, *, tt, name):
    T = proj.shape[0]
    D = RNN_BLOCKS * RNN_BLOCK_WIDTH
    hb = tt // SUBLANES

    def body(u0_ref, halo_ref, g_ref, small_ref, wa_ref, wi_ref, h_ref, y_ref,
             upad_ref, a_ref, b_ref, carry_ref):
        i = pl.program_id(0)

        @pl.when(i == 0)
        def _():
            carry_ref[...] = jnp.zeros_like(carry_ref)

        upad_ref[0:SUBLANES, :] = jnp.where(i == 0, 0.0, halo_ref[...])
        upad_ref[SUBLANES:, :] = u0_ref[...]
        u, r, ig, spl, a, s = _rg_gates(upad_ref, small_ref, wa_ref, wi_ref, tt)
        a_ref[...] = a
        b_ref[...] = s * (ig * u)

        def row(t, h):
            h = a_ref[pl.ds(t, 1), :] * h + b_ref[pl.ds(t, 1), :]
            h_ref[pl.ds(t, 1), :] = h
            return h

        carry_ref[...] = lax.fori_loop(0, tt, row, carry_ref[...], unroll=8)
        gate = g_ref[...]
        y_ref[...] = h_ref[...] * (gate * _sigmoid(gate))

    return pl.pallas_call(
        body, name=name, grid=(T // tt,),
        in_specs=[pl.BlockSpec((tt, D), lambda i: (i, 0)),
                  pl.BlockSpec((SUBLANES, D), lambda i: (jnp.maximum(i * hb - 1, 0), 0)),
                  pl.BlockSpec((tt, D), lambda i: (i, 1)),
                  pl.BlockSpec((SUBLANES, D), lambda i: (0, 0)),
                  pl.BlockSpec((RNN_BLOCKS, RNN_BLOCK_WIDTH, RNN_BLOCK_WIDTH), lambda i: (0, 0, 0)),
                  pl.BlockSpec((RNN_BLOCKS, RNN_BLOCK_WIDTH, RNN_BLOCK_WIDTH), lambda i: (0, 0, 0))],
        out_specs=[pl.BlockSpec((tt, D), lambda i: (i, 0)), pl.BlockSpec((tt, D), lambda i: (i, 0))],
        out_shape=[jax.ShapeDtypeStruct((T, D), F32), jax.ShapeDtypeStruct((T, D), F32)],
        scratch_shapes=[pltpu.VMEM((tt + SUBLANES, D), F32), pltpu.VMEM((tt, D), F32),
                        pltpu.VMEM((tt, D), F32), pltpu.VMEM((1, D), F32)],
        compiler_params=_cparams("arbitrary"),
    )(proj, proj, proj, small, wa, wi)


def _rg_bwd(proj, hs, dy, small, wa, wi, *, tt, name):
    T = proj.shape[0]
    D = RNN_BLOCKS * RNN_BLOCK_WIDTH
    W = RNN_BLOCK_WIDTH
    hb = tt // SUBLANES
    nt = T // tt

    def body(u0_ref, uhalo_ref, g_ref, h_ref, hhalo_ref, dy_ref, small_ref, wa_ref, wi_ref,
             dp_ref, dwa_ref, dwi_ref, ds_ref,
             upad_ref, hpad_ref, a_ref, g_s_ref, duext_ref, carry_ref):
        i = pl.program_id(0)
        first_chunk = i == nt - 1

        @pl.when(i == 0)
        def _():
            carry_ref[...] = jnp.zeros_like(carry_ref)
            duext_ref[...] = jnp.zeros_like(duext_ref)
            dwa_ref[...] = jnp.zeros_like(dwa_ref)
            dwi_ref[...] = jnp.zeros_like(dwi_ref)
            ds_ref[...] = jnp.zeros_like(ds_ref)

        upad_ref[0:SUBLANES, :] = jnp.where(first_chunk, 0.0, uhalo_ref[...])
        upad_ref[SUBLANES:, :] = u0_ref[...]
        hpad_ref[0:SUBLANES, :] = jnp.where(first_chunk, 0.0, hhalo_ref[...])
        hpad_ref[SUBLANES:, :] = h_ref[...]
        u, r, ig, spl, a, s = _rg_gates(upad_ref, small_ref, wa_ref, wi_ref, tt)
        gate = g_ref[...]
        sg = _sigmoid(gate)
        dy = dy_ref[...]
        dp_ref[:, D:] = dy * h_ref[...] * (sg * (1.0 + gate * (1.0 - sg)))
        a_ref[...] = a
        g_s_ref[...] = dy * (gate * sg)

        def row(k, c):
            t = tt - 1 - k
            g = g_s_ref[pl.ds(t, 1), :] + c
            g_s_ref[pl.ds(t, 1), :] = g
            return a_ref[pl.ds(t, 1), :] * g

        carry_ref[...] = lax.fori_loop(0, tt, row, carry_ref[...], unroll=8)
        g = g_s_ref[...]
        h_prev = hpad_ref[SUBLANES - 1:SUBLANES - 1 + tt, :]
        iu = ig * u
        d_iu = g * s
        dlog_a = (g * h_prev) * a - (g * iu) * (a * a) / s
        dpre_a = (dlog_a * ((-LRU_C) * spl)) * r * (1.0 - r)
        dpre_i = (d_iu * u) * ig * (1.0 - ig)
        dlam = jnp.sum(dlog_a * r, axis=0, keepdims=True) * (LRU_C * _sigmoid(-small_ref[7:8, :]))
        du_parts = []
        for n in range(RNN_BLOCKS):
            sl = slice(n * W, (n + 1) * W)
            ub = u[:, sl].astype(BF16)
            da_n = dpre_a[:, sl].astype(BF16)
            di_n = dpre_i[:, sl].astype(BF16)
            dwa_ref[n] += lax.dot_general(ub, da_n, (((0,), (0,)), ((), ())),
                                          preferred_element_type=F32)
            dwi_ref[n] += lax.dot_general(ub, di_n, (((0,), (0,)), ((), ())),
                                          preferred_element_type=F32)
            du_parts.append(
                lax.dot_general(da_n, wa_ref[n], (((1,), (1,)), ((), ())), preferred_element_type=F32)
                + lax.dot_general(di_n, wi_ref[n], (((1,), (1,)), ((), ())), preferred_element_type=F32))
        du = d_iu * ig + jnp.concatenate(du_parts, axis=-1)
        off = SUBLANES - (CONV_WIDTH - 1)
        for tap in range(CONV_WIDTH):
            ds_ref[tap:tap + 1, :] += jnp.sum(du * upad_ref[off + tap:off + tap + tt, :],
                                              axis=0, keepdims=True)
        ds_ref[4:5, :] += jnp.sum(du, axis=0, keepdims=True)
        ds_ref[5:6, :] += jnp.sum(dpre_a, axis=0, keepdims=True)
        ds_ref[6:7, :] += jnp.sum(dpre_i, axis=0, keepdims=True)
        ds_ref[7:8, :] += dlam
        duext_ref[0:tt, :] = du
        du0 = jnp.zeros((tt, D), F32)
        for tap in range(CONV_WIDTH):
            sh = CONV_WIDTH - 1 - tap
            du0 = du0 + duext_ref[sh:sh + tt, :] * small_ref[tap:tap + 1, :]
        dp_ref[:, :D] = du0
        duext_ref[tt:, :] = du[0:SUBLANES, :]

    rev = lambda i: nt - 1 - i
    wspec = pl.BlockSpec((RNN_BLOCKS, W, W), lambda i: (0, 0, 0))
    return pl.pallas_call(
        body, name=name, grid=(nt,),
        in_specs=[pl.BlockSpec((tt, D), lambda i: (rev(i), 0)),
                  pl.BlockSpec((SUBLANES, D), lambda i: (jnp.maximum(rev(i) * hb - 1, 0), 0)),
                  pl.BlockSpec((tt, D), lambda i: (rev(i), 1)),
                  pl.BlockSpec((tt, D), lambda i: (rev(i), 0)),
                  pl.BlockSpec((SUBLANES, D), lambda i: (jnp.maximum(rev(i) * hb - 1, 0), 0)),
                  pl.BlockSpec((tt, D), lambda i: (rev(i), 0)),
                  pl.BlockSpec((SUBLANES, D), lambda i: (0, 0)),
                  wspec, wspec],
        out_specs=[pl.BlockSpec((tt, 2 * D), lambda i: (rev(i), 0)),
                   wspec, wspec, pl.BlockSpec((SUBLANES, D), lambda i: (0, 0))],
        out_shape=[jax.ShapeDtypeStruct((T, 2 * D), F32),
                   jax.ShapeDtypeStruct((RNN_BLOCKS, W, W), F32),
                   jax.ShapeDtypeStruct((RNN_BLOCKS, W, W), F32),
                   jax.ShapeDtypeStruct((SUBLANES, D), F32)],
        scratch_shapes=[pltpu.VMEM((tt + SUBLANES, D), F32), pltpu.VMEM((tt + SUBLANES, D), F32),
                        pltpu.VMEM((tt, D), F32), pltpu.VMEM((tt, D), F32),
                        pltpu.VMEM((tt + SUBLANES, D), F32), pltpu.VMEM((1, D), F32)],
        compiler_params=_cparams("arbitrary"),
    )(proj, proj, proj, hs, hs, dy, small, wa, wi)


def _ln_fwd(x, h, g, b, *, tt, name):
    T, D = x.shape

    def body(x_ref, h_ref, g_ref, b_ref, y_ref, zh_ref, rs_ref):
        z = ALPHA * x_ref[...] + h_ref[...]
        mu = jnp.mean(z, axis=-1, keepdims=True)
        zc = z - mu
        rstd = lax.rsqrt(jnp.mean(zc * zc, axis=-1, keepdims=True) + LN_EPS)
        zh = zc * rstd
        zh_ref[...] = zh
        rs_ref[...] = rstd
        y_ref[...] = zh * g_ref[...] + b_ref[...]

    blk = pl.BlockSpec((tt, D), lambda i: (i, 0))
    row = pl.BlockSpec((1, D), lambda i: (0, 0))
    return pl.pallas_call(
        body, name=name, grid=(T // tt,),
        in_specs=[blk, blk, row, row],
        out_specs=[blk, blk, pl.BlockSpec((tt, 1), lambda i: (i, 0))],
        out_shape=[jax.ShapeDtypeStruct((T, D), F32), jax.ShapeDtypeStruct((T, D), F32),
                   jax.ShapeDtypeStruct((T, 1), F32)],
        compiler_params=_cparams("parallel"),
    )(x, h, g, b)


def _ln_bwd(dy, zh, rstd, g, *, tt, name):
    T, D = dy.shape

    def body(dy_ref, zh_ref, rs_ref, g_ref, dz_ref, dg_ref, db_ref):
        @pl.when(pl.program_id(0) == 0)
        def _():
            dg_ref[...] = jnp.zeros_like(dg_ref)
            db_ref[...] = jnp.zeros_like(db_ref)

        dy = dy_ref[...]
        zh = zh_ref[...]
        dg_ref[...] += jnp.sum(dy * zh, axis=0, keepdims=True)
        db_ref[...] += jnp.sum(dy, axis=0, keepdims=True)
        dzh = dy * g_ref[...]
        m1 = jnp.mean(dzh, axis=-1, keepdims=True)
        m2 = jnp.mean(dzh * zh, axis=-1, keepdims=True)
        dz_ref[...] = rs_ref[...] * (dzh - m1 - zh * m2)

    blk = pl.BlockSpec((tt, D), lambda i: (i, 0))
    row = pl.BlockSpec((1, D), lambda i: (0, 0))
    return pl.pallas_call(
        body, name=name, grid=(T // tt,),
        in_specs=[blk, blk, pl.BlockSpec((tt, 1), lambda i: (i, 0)), row],
        out_specs=[blk, row, row],
        out_shape=[jax.ShapeDtypeStruct((T, D), F32), jax.ShapeDtypeStruct((1, D), F32),
                   jax.ShapeDtypeStruct((1, D), F32)],
        compiler_params=_cparams("arbitrary"),
    )(dy, zh, rstd, g)


def _loss(y, tgt, *, tt, name):
    T, D = y.shape

    def body(y_ref, t_ref, l_ref, dy_ref):
        @pl.when(pl.program_id(0) == 0)
        def _():
            l_ref[...] = jnp.zeros_like(l_ref)

        e = y_ref[...] - t_ref[...]
        dy_ref[...] = e * (1.0 / D)
        l_ref[...] += jnp.sum(e * e, axis=0, keepdims=True) * (0.5 / D)

    blk = pl.BlockSpec((tt, D), lambda i: (i, 0))
    return pl.pallas_call(
        body, name=name, grid=(T // tt,),
        in_specs=[blk, blk], out_specs=[pl.BlockSpec((1, D), lambda i: (0, 0)), blk],
        out_shape=[jax.ShapeDtypeStruct((1, D), F32), jax.ShapeDtypeStruct((T, D), F32)],
        compiler_params=_cparams("arbitrary"),
    )(y, tgt)


def _row_tile(rows, target):
    best = SUBLANES
    for t in range(SUBLANES, target + 1, SUBLANES):
        if rows % t == 0:
            best = t
    return best


def _add_own(g, recv, c_idx, *, tr, name):
    _, M, R, C = g.shape

    def body(c_ref, g_ref, r_ref, o_ref):
        o_ref[...] = g_ref[0] + r_ref[...]

    return pl.pallas_call(
        body, name=name,
        grid_spec=pltpu.PrefetchScalarGridSpec(
            num_scalar_prefetch=1, grid=(M, R // tr),
            in_specs=[pl.BlockSpec((1, 1, tr, C), lambda k, i, c: (c[0], k, i, 0)),
                      pl.BlockSpec((1, tr, C), lambda k, i, c: (k, i, 0))],
            out_specs=pl.BlockSpec((1, tr, C), lambda k, i, c: (k, i, 0))),
        out_shape=jax.ShapeDtypeStruct((M, R, C), F32),
        compiler_params=_cparams("parallel", "parallel"),
    )(c_idx, g, recv)


def _adamw_math(g, w_ref, m_ref, v_ref, g_ref, d_ref, nm_ref, nv_ref):
    nm = ADAM_B1 * m_ref[...] + (1.0 - ADAM_B1) * g
    nv = ADAM_B2 * v_ref[...] + (1.0 - ADAM_B2) * (g * g)
    m_hat = nm / (1.0 - ADAM_B1 ** ADAM_STEP)
    v_hat = nv / (1.0 - ADAM_B2 ** ADAM_STEP)
    g_ref[...] = g
    nm_ref[...] = nm
    nv_ref[...] = nv
    d_ref[...] = (-ADAM_LR) * (m_hat / (jnp.sqrt(v_hat) + ADAM_EPS) + ADAM_WD * w_ref[...])


def _adamw(parts, w, m, v, *, tr, name):
    n, R, C = parts.shape
    tr = min(tr, R)

    def body(p_ref, w_ref, m_ref, v_ref, *out_refs):
        g = p_ref[0]
        for k in range(1, n):
            g = g + p_ref[k]
        _adamw_math(g, w_ref, m_ref, v_ref, *out_refs)

    blk = pl.BlockSpec((tr, C), lambda i: (i, 0))
    out = jax.ShapeDtypeStruct((R, C), F32)
    return pl.pallas_call(
        body, name=name, grid=(R // tr,),
        in_specs=[pl.BlockSpec((n, tr, C), lambda i: (0, i, 0)), blk, blk, blk],
        out_specs=[blk, blk, blk, blk], out_shape=[out, out, out, out],
        compiler_params=_cparams("parallel"),
    )(parts, w, m, v)


def _adamw_shard(h, recv, me_idx, w, m, v, *, tr, name):
    _, L, R, C = h.shape

    def body(me_ref, h_ref, r1_ref, r2_ref, r3_ref, w_ref, m_ref, v_ref, *out_refs):
        g = ((h_ref[0] + r1_ref[0]) + r2_ref[0]) + r3_ref[0]
        _adamw_math(g, w_ref, m_ref, v_ref, *out_refs)

    blk = pl.BlockSpec((1, tr, C), lambda l, i, me: (l, i, 0))

    def slot(d):
        return pl.BlockSpec((1, 1, tr, C), lambda l, i, me: (me[0] ^ d, l, i, 0))

    out = jax.ShapeDtypeStruct((L, R, C), F32)
    return pl.pallas_call(
        body, name=name,
        grid_spec=pltpu.PrefetchScalarGridSpec(
            num_scalar_prefetch=1, grid=(L, R // tr),
            in_specs=[slot(0), slot(1), slot(2), slot(3), blk, blk, blk],
            out_specs=[blk, blk, blk, blk]),
        out_shape=[out, out, out, out],
        compiler_params=_cparams("parallel", "parallel"),
    )(me_idx, h, recv, recv, recv, w, m, v)


ROWED = ("attn_w_out", "rnn_w_out", "rnn_w_a", "rnn_w_i")
SMALL = ("rnn_conv_w", "rnn_conv_b", "rnn_b_a", "rnn_b_i", "rnn_lambda")
PACK_C = 1024


def _elems(shape):
    n = 1
    for s in shape:
        n *= s
    return n


def _pack_rows(p, dtype):
    parts = [p[k].astype(dtype).reshape(-1, PACK_C) for k in ROWED]
    small = jnp.concatenate([p[k].reshape(-1) for k in SMALL])
    tile_rows = SUBLANES * (4 // jnp.dtype(dtype).itemsize)
    if dtype == BF16:
        small = lax.bitcast_convert_type(small, BF16)
    small = small.reshape(-1, PACK_C)
    parts.append(jnp.pad(small, ((0, tile_rows - small.shape[0]), (0, 0))))
    return jnp.concatenate(parts, axis=0)


def _unpack_rows(flat, shapes):
    out, r = {}, 0
    for k in ROWED:
        n = _elems(shapes[k]) // PACK_C
        out[k] = flat[r:r + n].reshape(shapes[k])
        r += n
    n_small = sum(_elems(shapes[k]) for k in SMALL)
    small = flat[r:r + n_small // PACK_C].reshape(-1)
    o = 0
    for k in SMALL:
        n = _elems(shapes[k])
        out[k] = small[o:o + n].reshape(shapes[k])
        o += n
    return out


def _to_full(g, k, sh):
    nd = len(sh)
    if k in ("attn_w_in", "rnn_w_in", "rnn_conv_w"):
        perm = tuple(range(2, 2 + nd - 1)) + (1, 0, 2 + nd - 1)
        t = g.transpose(perm)
        return t.reshape(sh[:-1] + (8 * sh[-1],))
    if k in ("attn_w_out", "rnn_w_out", "rnn_conv_b", "rnn_b_a", "rnn_b_i", "rnn_lambda"):
        perm = (2, 1, 0) + tuple(range(3, 2 + nd))
        t = g.transpose(perm)
        return t.reshape((sh[0], 8 * sh[1]) + sh[2:])
    perm = (2, 3, 1, 0, 4, 5)
    t = g.transpose(perm)
    return t.reshape((sh[0], sh[1], 8 * sh[2], sh[3]))


def _from_full(full, k, sh):
    nd = len(sh)
    if k in ("attn_w_in", "rnn_w_in", "rnn_conv_w"):
        t = full.reshape(sh[:-1] + (4, 2, sh[-1]))
        perm = (nd, nd - 1) + tuple(range(nd - 1)) + (nd + 1,)
        return t.transpose(perm)
    if k in ("attn_w_out", "rnn_w_out", "rnn_conv_b", "rnn_b_a", "rnn_b_i", "rnn_lambda"):
        t = full.reshape((sh[0], 4, 2) + sh[1:])
        perm = (2, 1, 0) + tuple(range(3, 2 + nd))
        return t.transpose(perm)
    t = full.reshape((sh[0], sh[1], 4, 2, sh[2], sh[3]))
    return t.transpose((3, 2, 0, 1, 4, 5))


def _unpack_gathered_rows(g, shapes):
    out, r = {}, 0
    for k in ROWED:
        n = _elems(shapes[k]) // PACK_C
        out[k] = _to_full(g[:, :, r:r + n].reshape((2, 4) + shapes[k]), k, shapes[k])
        r += n
    n_small = sum(_elems(shapes[k]) for k in SMALL)
    nr = 2 * n_small // PACK_C
    small = lax.bitcast_convert_type(g[:, :, r:r + nr].reshape(2, 4, n_small, 2), F32)
    o = 0
    for k in SMALL:
        n = _elems(shapes[k])
        out[k] = _to_full(small[:, :, o:o + n].reshape((2, 4) + shapes[k]), k, shapes[k])
        o += n
    return out


def _pack_grad_rows(full, shapes):
    parts = [_from_full(full[k], k, shapes[k]).reshape(2, 4, -1, PACK_C) for k in ROWED]
    small = jnp.concatenate(
        [_from_full(full[k], k, shapes[k]).reshape(2, 4, -1) for k in SMALL], axis=-1)
    small = small.reshape(2, 4, -1, PACK_C)
    parts.append(jnp.pad(small, ((0, 0), (0, 0), (0, SUBLANES - small.shape[2]), (0, 0))))
    return jnp.concatenate(parts, axis=2)


def kernel(x, ln_g, ln_b, attn_w_in, attn_b_f, attn_w_out, rnn_w_in, rnn_conv_w, rnn_conv_b, rnn_w_a, rnn_b_a, rnn_w_i, rnn_b_i, rnn_lambda, rnn_w_out, loss_target, m_ln_g, m_ln_b, m_attn_w_in, m_attn_b_f, m_attn_w_out, m_rnn_w_in, m_rnn_conv_w, m_rnn_conv_b, m_rnn_w_a, m_rnn_b_a, m_rnn_w_i, m_rnn_b_i, m_rnn_lambda, m_rnn_w_out, v_ln_g, v_ln_b, v_attn_w_in, v_attn_b_f, v_attn_w_out, v_rnn_w_in, v_rnn_conv_w, v_rnn_conv_b, v_rnn_w_a, v_rnn_b_a, v_rnn_w_i, v_rnn_b_i, v_rnn_lambda, v_rnn_w_out):
    w_loc = dict(attn_w_in=attn_w_in, attn_w_out=attn_w_out, rnn_w_in=rnn_w_in, rnn_w_a=rnn_w_a,
                 rnn_w_i=rnn_w_i, rnn_w_out=rnn_w_out, rnn_conv_w=rnn_conv_w, rnn_conv_b=rnn_conv_b,
                 rnn_b_a=rnn_b_a, rnn_b_i=rnn_b_i, rnn_lambda=rnn_lambda)
    m_loc = dict(attn_w_in=m_attn_w_in, attn_w_out=m_attn_w_out, rnn_w_in=m_rnn_w_in,
                 rnn_w_a=m_rnn_w_a, rnn_w_i=m_rnn_w_i, rnn_w_out=m_rnn_w_out,
                 rnn_conv_w=m_rnn_conv_w, rnn_conv_b=m_rnn_conv_b, rnn_b_a=m_rnn_b_a,
                 rnn_b_i=m_rnn_b_i, rnn_lambda=m_rnn_lambda)
    v_loc = dict(attn_w_in=v_attn_w_in, attn_w_out=v_attn_w_out, rnn_w_in=v_rnn_w_in,
                 rnn_w_a=v_rnn_w_a, rnn_w_i=v_rnn_w_i, rnn_w_out=v_rnn_w_out,
                 rnn_conv_w=v_rnn_conv_w, rnn_conv_b=v_rnn_conv_b, rnn_b_a=v_rnn_b_a,
                 rnn_b_i=v_rnn_b_i, rnn_lambda=v_rnn_lambda)
    shapes = {k: tuple(a.shape) for k, a in w_loc.items()}
    T, D = x.shape[1], x.shape[2]
    n_f = attn_b_f.shape[1]
    tb = min(512, T)
    tt_rg = min(128, T)
    tt_ln = min(256, T)

    g_ain, g_rin, g_rows = _ag_c(_ag_xy([attn_w_in.astype(BF16), rnn_w_in.astype(BF16),
                                         _pack_rows(w_loc, BF16)], "ag_w_xy"), "ag_w_c")
    W = _unpack_gathered_rows(g_rows, shapes)
    W["attn_w_in"] = _to_full(g_ain, "attn_w_in", shapes["attn_w_in"])
    W["rnn_w_in"] = _to_full(g_rin, "rnn_w_in", shapes["rnn_w_in"])
    w_in_a = jnp.pad(W["attn_w_in"], ((0, 0), (0, 0), (0, LANES - n_f)))
    small_r = jnp.concatenate([W["rnn_conv_w"], W["rnn_conv_b"][:, None], W["rnn_b_a"][:, None],
                               W["rnn_b_i"][:, None], W["rnn_lambda"][:, None]], axis=1)
    bf_rows = jnp.pad(attn_b_f, ((0, 0), (0, LANES - n_f)))[:, None, :]

    xs, saved = [x[0]], []
    for layer in range(DEPTH):
        idx, xl = layer // 2, xs[-1]
        if layer % 2 == 0:
            proj = _matmul(xl, w_in_a[idx], trans_b=False, tm=512, tn=1408, name=f"a_proj{layer}")
            cum_t = _cumsum_fwd(proj, bf_rows[idx], tt=min(512, T), name=f"a_cum{layer}")
            cum4 = cum_t[:N_HEADS].reshape(N_PAIRS, 2, T // tb, tb)
            o, og, lp = _flash_fwd(proj, cum4, tb=tb, name=f"a_fwd{layer}")
            hbr = _matmul(og, W["attn_w_out"][idx], trans_b=False, tm=512, tn=1024,
                          name=f"a_out{layer}")
            saved.append((proj, cum4, o, og, lp))
        else:
            proj = _matmul(xl, W["rnn_w_in"][idx], trans_b=False, tm=512, tn=1024,
                           name=f"r_proj{layer}")
            hs, yr = _rg_fwd(proj, small_r[idx], W["rnn_w_a"][idx], W["rnn_w_i"][idx],
                             tt=tt_rg, name=f"r_fwd{layer}")
            hbr = _matmul(yr, W["rnn_w_out"][idx], trans_b=False, tm=512, tn=1024,
                          name=f"r_out{layer}")
            saved.append((proj, hs, yr))
        y, zh, rstd = _ln_fwd(xl, hbr, ln_g[layer][None], ln_b[layer][None], tt=tt_ln,
                              name=f"ln_fwd{layer}")
        saved[-1] = saved[-1] + (zh, rstd)
        xs.append(y)

    loss_lanes, dy = _loss(xs[-1], loss_target[0], tt=tt_ln, name="loss")
    loss = lax.psum(jnp.sum(loss_lanes), ("x", "y", "c"))

    full_g = {k: [None, None] for k in w_loc}
    d_ln_g, d_ln_b, d_bf = [None] * DEPTH, [None] * DEPTH, [None, None]
    for layer in reversed(range(DEPTH)):
        idx, xl = layer // 2, xs[layer]
        zh, rstd = saved[layer][-2:]
        dz, dg, db = _ln_bwd(dy, zh, rstd, ln_g[layer][None], tt=tt_ln, name=f"ln_bwd{layer}")
        d_ln_g[layer], d_ln_b[layer] = dg[0], db[0]
        if layer % 2 == 0:
            proj, cum4, o, og, lp = saved[layer][:5]
            dog = _matmul(dz, W["attn_w_out"][idx], trans_b=True, tm=512, tn=1024,
                          name=f"a_dog{layer}")
            full_g["attn_w_out"][idx] = _matmul_tn(og, dz, tm=512, tn=1024, tk=512,
                                                   name=f"a_dwo{layer}")
            dq, dgate, do, delta, dcum_q = _flash_bwd_dq(proj, cum4, o, dog, lp, tb=tb,
                                                         name=f"a_dq{layer}")
            dk, dv, dcum_k = _flash_bwd_dkv(proj, cum4, do, lp, delta, tb=tb,
                                            name=f"a_dkv{layer}")
            dcum_t = (dcum_q + dcum_k).transpose(0, 2, 1, 3).reshape(N_HEADS, T)
            dcum_t = jnp.pad(dcum_t, ((0, LANES - N_HEADS), (0, 0)))
            df, dbf = _cumsum_bwd(dcum_t, proj, bf_rows[idx], tt=min(512, T), name=f"a_dcum{layer}")
            d_bf[idx] = dbf[0, :n_f]
            dproj = jnp.concatenate([dq, dk, dv, dgate, df], axis=1)
            dwi = _matmul_tn(xl, dproj, tm=512, tn=1408, tk=512, name=f"a_dwi{layer}")
            full_g["attn_w_in"][idx] = dwi[:, :4 * D + n_f]
            dy = _matmul(dproj, w_in_a[idx], trans_b=True, tm=256, tn=512, name=f"a_dx{layer}",
                         add=dz, add_scale=ALPHA)
        else:
            proj, hs, yr = saved[layer][:3]
            dyr = _matmul(dz, W["rnn_w_out"][idx], trans_b=True, tm=512, tn=1024,
                          name=f"r_dy{layer}")
            full_g["rnn_w_out"][idx] = _matmul_tn(yr, dz, tm=512, tn=1024, tk=512,
                                                  name=f"r_dwo{layer}")
            dproj, dwa, dwi_, dsm = _rg_bwd(proj, hs, dyr, small_r[idx], W["rnn_w_a"][idx],
                                            W["rnn_w_i"][idx], tt=tt_rg, name=f"r_bwd{layer}")
            full_g["rnn_w_a"][idx], full_g["rnn_w_i"][idx] = dwa, dwi_
            full_g["rnn_conv_w"][idx] = dsm[0:4]
            for r, k in enumerate(("rnn_conv_b", "rnn_b_a", "rnn_b_i", "rnn_lambda")):
                full_g[k][idx] = dsm[4 + r]
            full_g["rnn_w_in"][idx] = _matmul_tn(xl, dproj, tm=512, tn=1024, tk=512,
                                                 name=f"r_dwi{layer}")
            dy = _matmul(dproj, W["rnn_w_in"][idx], trans_b=True, tm=512, tn=512,
                         name=f"r_dx{layer}", add=dz, add_scale=ALPHA)
    grad_x = dy[None]

    fg = {k: jnp.stack(v) for k, v in full_g.items()}
    c_idx = lax.axis_index("c").astype(jnp.int32).reshape(1)
    me_idx = (2 * lax.axis_index("x") + lax.axis_index("y")).astype(jnp.int32).reshape(1)
    big = ("attn_w_in", "rnn_w_in")
    gs = [_from_full(fg[k], k, shapes[k]) for k in big] + [_pack_grad_rows(fg, shapes)]
    gs = [g.reshape((2, -1) + g.shape[-2:]) for g in gs]
    recv = _rs_c(gs, "rs_c")
    hs = [_add_own(g, r, c_idx, tr=_row_tile(g.shape[2], 512), name=f"rs_add{n}")
          for n, (g, r) in enumerate(zip(gs, recv))]
    hs = [h.reshape((4, -1) + h.shape[-2:]) for h in hs]
    quad = _rs_xy(hs, "rs_xy")
    wmv = [[d[k] for k in big] + [_pack_rows(d, F32)[None]] for d in (w_loc, m_loc, v_loc)]
    res = [_adamw_shard(hs[n], quad[n], me_idx, wmv[0][n], wmv[1][n], wmv[2][n],
                        tr=_row_tile(hs[n].shape[2], 256), name=f"adamw{n}") for n in range(3)]
    shard_outs = []
    for j in range(4):
        d = _unpack_rows(res[2][j][0], shapes)
        d["attn_w_in"], d["rnn_w_in"] = res[0][j], res[1][j]
        shard_outs.append(d)
    g_sh, d_sh, nm_sh, nv_sh = shard_outs

    def rep_pack(lg, lb, bf):
        rows = jnp.concatenate([lg, lb, jnp.pad(bf.reshape(1, -1), ((0, 0), (0, D - 2 * n_f)))])
        return jnp.pad(rows, ((0, 16 - rows.shape[0]), (0, 0)))

    rep = _all_gather(rep_pack(jnp.stack(d_ln_g), jnp.stack(d_ln_b), jnp.stack(d_bf)), "ag_rep")
    rg, rd, rm, rv = _adamw(rep.reshape(8, 16, D), rep_pack(ln_g, ln_b, attn_b_f),
                            rep_pack(m_ln_g, m_ln_b, m_attn_b_f),
                            rep_pack(v_ln_g, v_ln_b, v_attn_b_f), tr=16, name="adamw_rep")

    def rep_unpack(a):
        return dict(ln_g=a[0:DEPTH], ln_b=a[DEPTH:2 * DEPTH],
                    attn_b_f=a[2 * DEPTH, :2 * n_f].reshape(2, n_f))

    order = ("ln_g", "ln_b", "attn_w_in", "attn_b_f", "attn_w_out", "rnn_w_in", "rnn_conv_w",
             "rnn_conv_b", "rnn_w_a", "rnn_b_a", "rnn_w_i", "rnn_b_i", "rnn_lambda", "rnn_w_out")
    outs = [loss, grad_x]
    for sh, rp in ((g_sh, rg), (d_sh, rd), (nm_sh, rm), (nv_sh, rv)):
        allp = {**sh, **rep_unpack(rp)}
        outs.extend(allp[k] for k in order)
    return tuple(outs)
```

```python
import functools

import jax
import jax.numpy as jnp
from jax import lax
from jax.experimental import pallas as pl
from jax.experimental.pallas import tpu as pltpu

F32 = jnp.float32
BF16 = jnp.bfloat16

DEPTH = 4
N_HEADS = 16
HEAD_DIM = 64
N_PAIRS = N_HEADS // 2
RNN_BLOCKS = 4
RNN_BLOCK_WIDTH = 256
CONV_WIDTH = 4
LRU_C = 8.0
ALPHA = (2.0 * DEPTH) ** 0.25
LN_EPS = 1e-5
ADAM_LR, ADAM_B1, ADAM_B2, ADAM_EPS, ADAM_WD, ADAM_STEP = 0.001, 0.9, 0.999, 1e-8, 0.01, 10

LANES = 128
SUBLANES = 8
VMEM_LIMIT = 48 * 1024 * 1024

MESH = pl.DeviceIdType.MESH
HBM_SPEC = pl.BlockSpec(memory_space=pltpu.HBM)


def _cparams(*sem):
    return pltpu.CompilerParams(dimension_semantics=sem, vmem_limit_bytes=VMEM_LIMIT)


def _sigmoid(x):
    return 1.0 / (1.0 + jnp.exp(-x))


def _softplus(x):
    return jnp.maximum(x, 0.0) + jnp.log(1.0 + jnp.exp(-jnp.abs(x)))


def _a2a(src, *, group, bcast, name):
    n = 2 if group == "c" else 4
    blk = tuple(src.shape) if bcast else tuple(src.shape[1:])

    def body(src_ref, out_ref, send_sems, recv_sems, local_sem):
        x, y, c = lax.axis_index("x"), lax.axis_index("y"), lax.axis_index("c")
        if group == "c":
            me = c

            def peer(d):
                return (x, y, 1 - c), 1 - c
        else:
            me = 2 * x + y

            def peer(d):
                px, py = x ^ (d >> 1), y ^ (d & 1)
                return (px, py, c), 2 * px + py

        def block_for(k):
            return src_ref if bcast else src_ref.at[k]

        local = pltpu.make_async_copy(block_for(me), out_ref.at[me], local_sem)
        local.start()
        sends = []
        for d in range(1, n):
            dev, idx = peer(d)
            cp = pltpu.make_async_remote_copy(
                src_ref=block_for(idx), dst_ref=out_ref.at[me],
                send_sem=send_sems.at[d], recv_sem=recv_sems.at[d],
                device_id=dev, device_id_type=MESH)
            cp.start()
            sends.append(cp)
        for d in range(1, n):
            dev, idx = peer(d)
            pltpu.make_async_remote_copy(
                src_ref=block_for(idx), dst_ref=out_ref.at[idx],
                send_sem=send_sems.at[d], recv_sem=recv_sems.at[d],
                device_id=dev, device_id_type=MESH).wait_recv()
        for cp in sends:
            cp.wait_send()
        local.wait()

    return pl.pallas_call(
        body, name=name,
        out_shape=jax.ShapeDtypeStruct((n,) + blk, src.dtype),
        in_specs=[HBM_SPEC], out_specs=HBM_SPEC,
        scratch_shapes=[pltpu.SemaphoreType.DMA((n,)), pltpu.SemaphoreType.DMA((n,)),
                        pltpu.SemaphoreType.DMA],
    )(src)


def _all_gather(piece, name):
    return _a2a(_a2a(piece, group="xy", bcast=True, name=name + "_xy"),
                group="c", bcast=True, name=name + "_c")


D2D_CHUNKS = 16
ICI_CHUNKS = 8


def _row_chunks(rows, dtype, k):
    unit = SUBLANES * (4 // jnp.dtype(dtype).itemsize)
    assert rows % unit == 0
    units = rows // unit
    k = max(1, min(k, units))
    base, rem = divmod(units, k)
    out, r = [], 0
    for i in range(k):
        n = (base + (1 if i < rem else 0)) * unit
        out.append((r, n))
        r += n
    return out


def _chunks(shape, dtype, k):
    if len(shape) == 2:
        return [(pl.ds(r0, n),) for r0, n in _row_chunks(shape[0], dtype, k)]
    per = max(1, k // shape[0])
    return [(l, pl.ds(r0, n)) for l in range(shape[0]) for r0, n in _row_chunks(shape[1], dtype, per)]


def _mesh_place():
    x, y, c = lax.axis_index("x"), lax.axis_index("y"), lax.axis_index("c")
    return x, y, c, 2 * x + y


def _chip_peer(x, y, c, d):
    px, py = x ^ (d >> 1), y ^ (d & 1)
    return (px, py, c), 2 * px + py


def _remote(src, dst, send_sem, recv_sem, dev):
    return pltpu.make_async_remote_copy(src_ref=src, dst_ref=dst, send_sem=send_sem,
                                        recv_sem=recv_sem, device_id=dev, device_id_type=MESH)


def _comm_call(body, name, ins, out_shapes, n_sems, aliases=None):
    n = len(ins)
    return pl.pallas_call(
        body, name=name,
        out_shape=out_shapes, in_specs=[HBM_SPEC] * n, out_specs=[HBM_SPEC] * n,
        input_output_aliases=aliases or {},
        scratch_shapes=[pltpu.SemaphoreType.DMA((n_sems, n)), pltpu.SemaphoreType.DMA((n_sems, n))],
    )(*ins)


def _ag_xy(pieces, name):
    n = len(pieces)
    chunks = [_chunks(p.shape, p.dtype, ICI_CHUNKS) for p in pieces]

    def body(*refs):
        srcs, outs, send_sems, recv_sems = refs[:n], refs[n:2 * n], refs[2 * n], refs[2 * n + 1]
        x, y, c, me = _mesh_place()
        for o in range(n):
            for idx in chunks[o]:
                pltpu.make_async_copy(srcs[o].at[idx], outs[o].at[(c, me) + idx],
                                      send_sems.at[0, o]).start()
        for d in range(1, 4):
            dev, _ = _chip_peer(x, y, c, d)
            for o in range(n):
                for idx in chunks[o]:
                    _remote(srcs[o].at[idx], outs[o].at[(c, me) + idx],
                            send_sems.at[d, o], recv_sems.at[d, o], dev).start()
        for d in range(1, 4):
            dev, pidx = _chip_peer(x, y, c, d)
            for o in range(n):
                _remote(srcs[o], outs[o].at[c, pidx], send_sems.at[d, o], recv_sems.at[d, o],
                        dev).wait_recv()
        for d in range(1, 4):
            dev, pidx = _chip_peer(x, y, c, d)
            for o in range(n):
                _remote(srcs[o], outs[o].at[c, pidx], send_sems.at[d, o], recv_sems.at[d, o],
                        dev).wait_send()
        for o in range(n):
            pltpu.make_async_copy(srcs[o], outs[o].at[c, me], send_sems.at[0, o]).wait()

    shapes = [jax.ShapeDtypeStruct((2, 4) + tuple(p.shape), p.dtype) for p in pieces]
    return _comm_call(body, name, pieces, shapes, 4)


def _ag_c(bufs, name):
    n = len(bufs)
    chunks = [_chunks(b.shape[2:], b.dtype, D2D_CHUNKS // 4) for b in bufs]

    def body(*refs):
        srcs, outs, send_sems, recv_sems = refs[:n], refs[n:2 * n], refs[2 * n], refs[2 * n + 1]
        x, y, c, _ = _mesh_place()
        sib = (x, y, 1 - c)
        for o in range(n):
            for k in range(4):
                for idx in chunks[o]:
                    _remote(srcs[o].at[(c, k) + idx], outs[o].at[(c, k) + idx],
                            send_sems.at[0, o], recv_sems.at[0, o], sib).start()
        for o in range(n):
            _remote(srcs[o].at[c], outs[o].at[1 - c], send_sems.at[0, o], recv_sems.at[0, o],
                    sib).wait_recv()
        for o in range(n):
            _remote(srcs[o].at[c], outs[o].at[1 - c], send_sems.at[0, o], recv_sems.at[0, o],
                    sib).wait_send()

    shapes = [jax.ShapeDtypeStruct(b.shape, b.dtype) for b in bufs]
    return _comm_call(body, name, bufs, shapes, 1, aliases={i: i for i in range(n)})


def _rs_c(gs, name):
    n = len(gs)
    chunks = [_chunks(g.shape[2:], g.dtype, max(1, D2D_CHUNKS // g.shape[1])) for g in gs]

    def body(*refs):
        srcs, outs, send_sems, recv_sems = refs[:n], refs[n:2 * n], refs[2 * n], refs[2 * n + 1]
        x, y, c, _ = _mesh_place()
        sib = (x, y, 1 - c)
        for o in range(n):
            for k in range(gs[o].shape[1]):
                for idx in chunks[o]:
                    _remote(srcs[o].at[(1 - c, k) + idx], outs[o].at[(k,) + idx],
                            send_sems.at[0, o], recv_sems.at[0, o], sib).start()
        for o in range(n):
            _remote(srcs[o].at[1 - c], outs[o], send_sems.at[0, o], recv_sems.at[0, o],
                    sib).wait_recv()
        for o in range(n):
            _remote(srcs[o].at[1 - c], outs[o], send_sems.at[0, o], recv_sems.at[0, o],
                    sib).wait_send()

    shapes = [jax.ShapeDtypeStruct(g.shape[1:], g.dtype) for g in gs]
    return _comm_call(body, name, gs, shapes, 1)


def _rs_xy(hs, name):
    n = len(hs)
    chunks = [_chunks(h.shape[1:], h.dtype, ICI_CHUNKS) for h in hs]

    def body(*refs):
        srcs, outs, send_sems, recv_sems = refs[:n], refs[n:2 * n], refs[2 * n], refs[2 * n + 1]
        x, y, c, me = _mesh_place()
        for d in range(1, 4):
            dev, pidx = _chip_peer(x, y, c, d)
            for o in range(n):
                for idx in chunks[o]:
                    _remote(srcs[o].at[(pidx,) + idx], outs[o].at[(me,) + idx],
                            send_sems.at[d, o], recv_sems.at[d, o], dev).start()
        for d in range(1, 4):
            dev, pidx = _chip_peer(x, y, c, d)
            for o in range(n):
                _remote(srcs[o].at[pidx], outs[o].at[pidx], send_sems.at[d, o], recv_sems.at[d, o],
                        dev).wait_recv()
        for d in range(1, 4):
            dev, pidx = _chip_peer(x, y, c, d)
            for o in range(n):
                _remote(srcs[o].at[pidx], outs[o].at[pidx], send_sems.at[d, o], recv_sems.at[d, o],
                        dev).wait_send()

    shapes = [jax.ShapeDtypeStruct(h.shape, h.dtype) for h in hs]
    return _comm_call(body, name, hs, shapes, 4)


def _matmul(a, b, *, trans_b, tm, tn, name, add=None, add_scale=1.0):
    M, K = a.shape
    N = b.shape[0] if trans_b else b.shape[1]
    tm, tn = min(tm, M), min(tn, N)
    assert M % tm == 0 and N % tn == 0
    dn = (((1,), (1,)), ((), ())) if trans_b else (((1,), (0,)), ((), ()))

    def body(*refs):
        if add is None:
            a_ref, b_ref, o_ref = refs
        else:
            a_ref, b_ref, add_ref, o_ref = refs
        r = lax.dot_general(a_ref[...].astype(BF16), b_ref[...].astype(BF16), dn,
                            preferred_element_type=F32)
        if add is not None:
            r = r + add_scale * add_ref[...]
        o_ref[...] = r

    b_spec = (pl.BlockSpec((tn, K), lambda j, i: (j, 0)) if trans_b
              else pl.BlockSpec((K, tn), lambda j, i: (0, j)))
    in_specs = [pl.BlockSpec((tm, K), lambda j, i: (i, 0)), b_spec]
    args = [a, b]
    if add is not None:
        in_specs.append(pl.BlockSpec((tm, tn), lambda j, i: (i, j)))
        args.append(add)
    return pl.pallas_call(
        body, name=name, grid=(N // tn, M // tm),
        in_specs=in_specs, out_specs=pl.BlockSpec((tm, tn), lambda j, i: (i, j)),
        out_shape=jax.ShapeDtypeStruct((M, N), F32),
        compiler_params=_cparams("parallel", "parallel"),
    )(*args)


def _matmul_tn(a, b, *, tm, tn, tk, name):
    T, M = a.shape
    N = b.shape[1]
    tm, tn, tk = min(tm, M), min(tn, N), min(tk, T)
    assert M % tm == 0 and N % tn == 0 and T % tk == 0

    def body(a_ref, b_ref, o_ref):
        @pl.when(pl.program_id(2) == 0)
        def _():
            o_ref[...] = jnp.zeros_like(o_ref)

        o_ref[...] += lax.dot_general(a_ref[...].astype(BF16), b_ref[...].astype(BF16),
                                      (((0,), (0,)), ((), ())), preferred_element_type=F32)

    return pl.pallas_call(
        body, name=name, grid=(M // tm, N // tn, T // tk),
        in_specs=[pl.BlockSpec((tk, tm), lambda i, j, k: (k, i)),
                  pl.BlockSpec((tk, tn), lambda i, j, k: (k, j))],
        out_specs=pl.BlockSpec((tm, tn), lambda i, j, k: (i, j)),
        out_shape=jax.ShapeDtypeStruct((M, N), F32),
        compiler_params=_cparams("parallel", "parallel", "arbitrary"),
    )(a, b)


def _head_masks(rows):
    lane = lax.broadcasted_iota(jnp.int32, (rows, LANES), 1)
    return lane < HEAD_DIM, lane >= HEAD_DIM


def _causal(i_q, i_k, tq, tk):
    row = i_q * tq + lax.broadcasted_iota(jnp.int32, (tq, tk), 0)
    col = i_k * tk + lax.broadcasted_iota(jnp.int32, (tq, tk), 1)
    return row >= col


def _flash_fwd(proj, cum4, *, tb, name):
    T = proj.shape[0]
    D = N_HEADS * HEAD_DIM
    nb = T // tb
    cb = D // LANES

    def body(q_ref, k_ref, v_ref, g_ref, cum_ref, o_ref, og_ref, lp_ref, kb_ref, vb_ref):
        i = pl.program_id(1)

        @pl.when(i == 0)
        def _():
            kb_ref[...] = k_ref[...].astype(BF16)
            vb_ref[...] = v_ref[...].astype(BF16)

        q = q_ref[...] * (HEAD_DIM ** -0.5)
        masks = _head_masks(tb)
        qh = [jnp.where(masks[h], q, 0.0).astype(BF16) for h in range(2)]
        cref = [cum_ref[0, h, pl.ds(i, 1), :][:, 0:1] for h in range(2)]

        def step(kbi, carry, masked):
            k0 = pl.multiple_of(kbi * tb, tb)
            kblk = kb_ref[pl.ds(k0, tb), :]
            vblk = vb_ref[pl.ds(k0, tb), :]
            new = []
            for h in range(2):
                m, l, acc = carry[h]
                s = lax.dot_general(qh[h], kblk, (((1,), (1,)), ((), ())),
                                    preferred_element_type=F32)
                s = s + (cref[h] - cum_ref[0, h, pl.ds(kbi, 1), :])
                if masked:
                    s = jnp.where(_causal(i, kbi, tb, tb), s, -jnp.inf)
                m_new = jnp.maximum(m, jnp.max(s, axis=-1, keepdims=True))
                alpha = jnp.exp(m - m_new)
                p = jnp.exp(s - m_new)
                l = alpha * l + jnp.sum(p, axis=-1, keepdims=True)
                acc = alpha * acc + jnp.dot(p.astype(BF16), vblk, preferred_element_type=F32)
                new.append((m_new, l, acc))
            return tuple(new)

        init1 = (jnp.full((tb, 1), -jnp.inf, F32), jnp.zeros((tb, 1), F32),
                 jnp.zeros((tb, LANES), F32))
        carry = lax.fori_loop(0, i, lambda kbi, c: step(kbi, c, False), (init1, init1))
        outs = []
        for h, (m, l, acc) in enumerate(step(i, carry, True)):
            outs.append(acc / l)
            lp_ref[h] = jnp.broadcast_to(m + jnp.log(l) - cref[h], (tb, LANES))
        o = jnp.where(masks[0], outs[0], outs[1])
        o_ref[...] = o
        gate = g_ref[...]
        og_ref[...] = o * (gate * _sigmoid(gate))

    return pl.pallas_call(
        body, name=name, grid=(N_PAIRS, nb),
        in_specs=[pl.BlockSpec((tb, LANES), lambda j, i: (i, j)),
                  pl.BlockSpec((T, LANES), lambda j, i: (0, cb + j)),
                  pl.BlockSpec((T, LANES), lambda j, i: (0, 2 * cb + j)),
                  pl.BlockSpec((tb, LANES), lambda j, i: (i, 3 * cb + j)),
                  pl.BlockSpec((1, 2, nb, tb), lambda j, i: (j, 0, 0, 0))],
        out_specs=[pl.BlockSpec((tb, LANES), lambda j, i: (i, j)),
                   pl.BlockSpec((tb, LANES), lambda j, i: (i, j)),
                   pl.BlockSpec((2, tb, LANES), lambda j, i: (j, i, 0))],
        out_shape=[jax.ShapeDtypeStruct((T, D), F32), jax.ShapeDtypeStruct((T, D), F32),
                   jax.ShapeDtypeStruct((N_HEADS, T, LANES), F32)],
        scratch_shapes=[pltpu.VMEM((T, LANES), BF16), pltpu.VMEM((T, LANES), BF16)],
        compiler_params=_cparams("parallel", "arbitrary"),
    )(proj, proj, proj, proj, cum4)


def _flash_bwd_dq(proj, cum4, o, dog, lp, *, tb, name):
    T = proj.shape[0]
    D = N_HEADS * HEAD_DIM
    nb = T // tb
    cb = D // LANES

    def body(q_ref, k_ref, v_ref, g_ref, cum_ref, o_ref, dog_ref, lp_ref,
             dq_ref, dg_ref, do_ref, dl_ref, dc_ref, kb_ref, vb_ref):
        i = pl.program_id(1)

        @pl.when(i == 0)
        def _():
            kb_ref[...] = k_ref[...].astype(BF16)
            vb_ref[...] = v_ref[...].astype(BF16)

        gate = g_ref[...]
        sg = _sigmoid(gate)
        o = o_ref[...]
        dog = dog_ref[...]
        do = dog * (gate * sg)
        dg_ref[...] = dog * o * (sg * (1.0 + gate * (1.0 - sg)))
        do_ref[...] = do.astype(BF16)
        q = q_ref[...] * (HEAD_DIM ** -0.5)
        masks = _head_masks(tb)
        qh = [jnp.where(masks[h], q, 0.0).astype(BF16) for h in range(2)]
        doh = [jnp.where(masks[h], do, 0.0).astype(BF16) for h in range(2)]
        delta = [jnp.sum(jnp.where(masks[h], do * o, 0.0), axis=-1, keepdims=True) for h in range(2)]
        lph = [lp_ref[h][:, 0:1] for h in range(2)]
        for h in range(2):
            dl_ref[h] = jnp.broadcast_to(delta[h], (tb, LANES))

        def step(kbi, carry, masked):
            k0 = pl.multiple_of(kbi * tb, tb)
            kblk = kb_ref[pl.ds(k0, tb), :]
            vblk = vb_ref[pl.ds(k0, tb), :]
            new = []
            for h in range(2):
                acc, rs = carry[h]
                s = lax.dot_general(qh[h], kblk, (((1,), (1,)), ((), ())), preferred_element_type=F32)
                p = jnp.exp(s - cum_ref[0, h, pl.ds(kbi, 1), :] - lph[h])
                if masked:
                    p = jnp.where(_causal(i, kbi, tb, tb), p, 0.0)
                dp = lax.dot_general(doh[h], vblk, (((1,), (1,)), ((), ())),
                                     preferred_element_type=F32)
                ds = p * (dp - delta[h])
                new.append((acc + jnp.dot(ds.astype(BF16), kblk, preferred_element_type=F32),
                            rs + jnp.sum(ds, axis=-1, keepdims=True)))
            return tuple(new)

        init1 = (jnp.zeros((tb, LANES), F32), jnp.zeros((tb, 1), F32))
        carry = lax.fori_loop(0, i, lambda kbi, c: step(kbi, c, False), (init1, init1))
        dqs = []
        for h, (acc, rs) in enumerate(step(i, carry, True)):
            dqs.append(acc)
            dc_ref[0, 0, pl.ds(h, 1), :] = jnp.broadcast_to(rs, (tb, LANES)).T[0:1, :]
        dq_ref[...] = jnp.where(masks[0], dqs[0], dqs[1]) * (HEAD_DIM ** -0.5)

    blk = pl.BlockSpec((tb, LANES), lambda j, i: (i, j))
    stat = pl.BlockSpec((2, tb, LANES), lambda j, i: (j, i, 0))
    return pl.pallas_call(
        body, name=name, grid=(N_PAIRS, nb),
        in_specs=[blk,
                  pl.BlockSpec((T, LANES), lambda j, i: (0, cb + j)),
                  pl.BlockSpec((T, LANES), lambda j, i: (0, 2 * cb + j)),
                  pl.BlockSpec((tb, LANES), lambda j, i: (i, 3 * cb + j)),
                  pl.BlockSpec((1, 2, nb, tb), lambda j, i: (j, 0, 0, 0)),
                  blk, blk, stat],
        out_specs=[blk, blk, blk, stat, pl.BlockSpec((1, 1, 2, tb), lambda j, i: (j, i, 0, 0))],
        out_shape=[jax.ShapeDtypeStruct((T, D), F32), jax.ShapeDtypeStruct((T, D), F32),
                   jax.ShapeDtypeStruct((T, D), BF16),
                   jax.ShapeDtypeStruct((N_HEADS, T, LANES), F32),
                   jax.ShapeDtypeStruct((N_PAIRS, nb, 2, tb), F32)],
        scratch_shapes=[pltpu.VMEM((T, LANES), BF16), pltpu.VMEM((T, LANES), BF16)],
        compiler_params=_cparams("parallel", "arbitrary"),
    )(proj, proj, proj, proj, cum4, o, dog, lp)


def _flash_bwd_dkv(proj, cum4, do, lp, delta, *, tb, name):
    T = proj.shape[0]
    D = N_HEADS * HEAD_DIM
    nb = T // tb
    cb = D // LANES

    def body(q_ref, k_ref, v_ref, cum_ref, do_ref, lp_ref, dl_ref, dk_ref, dv_ref, dc_ref):
        kbi = pl.program_id(1)
        k = k_ref[...] * (HEAD_DIM ** -0.5)
        v = v_ref[...]
        masks = _head_masks(tb)
        kh = [jnp.where(masks[h], k, 0.0).astype(BF16) for h in range(2)]
        vh = [jnp.where(masks[h], v, 0.0).astype(BF16) for h in range(2)]
        ck = [cum_ref[0, h, pl.ds(kbi, 1), :] for h in range(2)]

        def step(i, carry, masked):
            q0 = pl.multiple_of(i * tb, tb)
            qb = q_ref[pl.ds(q0, tb), :].astype(BF16)
            dob = do_ref[pl.ds(q0, tb), :]
            new = []
            for h in range(2):
                dk, dv, dc = carry[h]
                s = lax.dot_general(qb, kh[h], (((1,), (1,)), ((), ())), preferred_element_type=F32)
                p = jnp.exp(s - ck[h] - lp_ref[h, pl.ds(q0, tb), :][:, 0:1])
                if masked:
                    p = jnp.where(_causal(i, kbi, tb, tb), p, 0.0)
                dp = lax.dot_general(dob, vh[h], (((1,), (1,)), ((), ())), preferred_element_type=F32)
                ds = p * (dp - dl_ref[h, pl.ds(q0, tb), :][:, 0:1])
                dv = dv + lax.dot_general(p.astype(BF16), dob, (((0,), (0,)), ((), ())),
                                          preferred_element_type=F32)
                dk = dk + lax.dot_general(ds.astype(BF16), qb, (((0,), (0,)), ((), ())),
                                          preferred_element_type=F32)
                new.append((dk, dv, dc - jnp.sum(ds, axis=0, keepdims=True)))
            return tuple(new)

        init1 = (jnp.zeros((tb, LANES), F32), jnp.zeros((tb, LANES), F32), jnp.zeros((1, tb), F32))
        carry = step(kbi, (init1, init1), True)
        carry = lax.fori_loop(kbi + 1, nb, lambda i, c: step(i, c, False), carry)
        dks, dvs = [], []
        for h, (dk, dv, dc) in enumerate(carry):
            dks.append(dk)
            dvs.append(dv)
            dc_ref[0, 0, pl.ds(h, 1), :] = dc
        dk_ref[...] = jnp.where(masks[0], dks[0], dks[1]) * (HEAD_DIM ** -0.5)
        dv_ref[...] = jnp.where(masks[0], dvs[0], dvs[1])

    full = pl.BlockSpec((T, LANES), lambda j, i: (0, j))
    stat = pl.BlockSpec((2, T, LANES), lambda j, i: (j, 0, 0))
    blk = pl.BlockSpec((tb, LANES), lambda j, i: (i, j))
    return pl.pallas_call(
        body, name=name, grid=(N_PAIRS, nb),
        in_specs=[full,
                  pl.BlockSpec((tb, LANES), lambda j, i: (i, cb + j)),
                  pl.BlockSpec((tb, LANES), lambda j, i: (i, 2 * cb + j)),
                  pl.BlockSpec((1, 2, nb, tb), lambda j, i: (j, 0, 0, 0)),
                  full, stat, stat],
        out_specs=[blk, blk, pl.BlockSpec((1, 1, 2, tb), lambda j, i: (j, i, 0, 0))],
        out_shape=[jax.ShapeDtypeStruct((T, D), F32), jax.ShapeDtypeStruct((T, D), F32),
                   jax.ShapeDtypeStruct((N_PAIRS, nb, 2, tb), F32)],
        compiler_params=_cparams("parallel", "arbitrary"),
    )(proj, proj, proj, cum4, do, lp, delta)


def _cumsum_fwd(proj, bf_row, *, tt, name):
    T = proj.shape[0]
    cb = (proj.shape[1] - LANES) // LANES

    def body(f_ref, b_ref, out_ref, carry_ref):
        i = pl.program_id(0)

        @pl.when(i == 0)
        def _():
            carry_ref[...] = jnp.zeros_like(carry_ref)

        ls = -_softplus(-(f_ref[...] + b_ref[...]))
        tri = (lax.broadcasted_iota(jnp.int32, (tt, tt), 0)
               >= lax.broadcasted_iota(jnp.int32, (tt, tt), 1)).astype(F32)
        cum = jnp.dot(tri, ls, preferred_element_type=F32,
                      precision=lax.Precision.HIGHEST) + carry_ref[...]
        carry_ref[...] = cum[tt - 1:tt, :]
        out_ref[...] = cum.T

    return pl.pallas_call(
        body, name=name, grid=(T // tt,),
        in_specs=[pl.BlockSpec((tt, LANES), lambda i: (i, cb)),
                  pl.BlockSpec((1, LANES), lambda i: (0, 0))],
        out_specs=pl.BlockSpec((LANES, tt), lambda i: (0, i)),
        out_shape=jax.ShapeDtypeStruct((LANES, T), F32),
        scratch_shapes=[pltpu.VMEM((1, LANES), F32)],
        compiler_params=_cparams("arbitrary"),
    )(proj, bf_row)


def _cumsum_bwd(dcum_t, proj, bf_row, *, tt, name):
    T = proj.shape[0]
    cb = (proj.shape[1] - LANES) // LANES
    nt = T // tt

    def body(dc_ref, f_ref, b_ref, df_ref, db_ref, carry_ref):
        i = pl.program_id(0)

        @pl.when(i == 0)
        def _():
            carry_ref[...] = jnp.zeros_like(carry_ref)
            db_ref[...] = jnp.zeros_like(db_ref)

        dc = dc_ref[...].T
        tri = (lax.broadcasted_iota(jnp.int32, (tt, tt), 0)
               <= lax.broadcasted_iota(jnp.int32, (tt, tt), 1)).astype(F32)
        rev = jnp.dot(tri, dc, preferred_element_type=F32,
                      precision=lax.Precision.HIGHEST) + carry_ref[...]
        carry_ref[...] = rev[0:1, :]
        df = rev * _sigmoid(-(f_ref[...] + b_ref[...]))
        df_ref[...] = df
        db_ref[...] += jnp.sum(df, axis=0, keepdims=True)

    return pl.pallas_call(
        body, name=name, grid=(nt,),
        in_specs=[pl.BlockSpec((LANES, tt), lambda i: (0, nt - 1 - i)),
                  pl.BlockSpec((tt, LANES), lambda i: (nt - 1 - i, cb)),
                  pl.BlockSpec((1, LANES), lambda i: (0, 0))],
        out_specs=[pl.BlockSpec((tt, LANES), lambda i: (nt - 1 - i, 0)),
                   pl.BlockSpec((1, LANES), lambda i: (0, 0))],
        out_shape=[jax.ShapeDtypeStruct((T, LANES), F32), jax.ShapeDtypeStruct((1, LANES), F32)],
        scratch_shapes=[pltpu.VMEM((1, LANES), F32)],
        compiler_params=_cparams("arbitrary"),
    )(dcum_t, proj, bf_row)


def _rg_gates(upad_ref, small_ref, wa_ref, wi_ref, tt):
    off = SUBLANES - (CONV_WIDTH - 1)
    u = small_ref[4:5, :]
    for tap in range(CONV_WIDTH):
        u = u + upad_ref[off + tap:off + tap + tt, :] * small_ref[tap:tap + 1, :]
    pa, pi = [], []
    for n in range(RNN_BLOCKS):
        ub = u[:, n * RNN_BLOCK_WIDTH:(n + 1) * RNN_BLOCK_WIDTH].astype(BF16)
        pa.append(jnp.dot(ub, wa_ref[n], preferred_element_type=F32))
        pi.append(jnp.dot(ub, wi_ref[n], preferred_element_type=F32))
    r = _sigmoid(jnp.concatenate(pa, axis=-1) + small_ref[5:6, :])
    ig = _sigmoid(jnp.concatenate(pi, axis=-1) + small_ref[6:7, :])
    spl = _softplus(-small_ref[7:8, :])
    log_a = (-LRU_C) * r * spl
    a = jnp.exp(log_a)
    s = jnp.sqrt(jnp.tanh(-log_a) * (a * a + 1.0))
    return u, r, ig, spl, a, s


def _rg_fwd(proj, small, wa, wi, *, tt, name):
    T = proj.shape[0]
    D = RNN_BLOCKS * RNN_BLOCK_WIDTH
    hb = tt // SUBLANES

    def body(u0_ref, halo_ref, g_ref, small_ref, wa_ref, wi_ref, h_ref, y_ref,
             upad_ref, a_ref, b_ref, carry_ref):
        i = pl.program_id(0)

        @pl.when(i == 0)
        def _():
            carry_ref[...] = jnp.zeros_like(carry_ref)

        upad_ref[0:SUBLANES, :] = jnp.where(i == 0, 0.0, halo_ref[...])
        upad_ref[SUBLANES:, :] = u0_ref[...]
        u, r, ig, spl, a, s = _rg_gates(upad_ref, small_ref, wa_ref, wi_ref, tt)
        a_ref[...] = a
        b_ref[...] = s * (ig * u)

        def row(t, h):
            h = a_ref[pl.ds(t, 1), :] * h + b_ref[pl.ds(t, 1), :]
            h_ref[pl.ds(t, 1), :] = h
            return h

        carry_ref[...] = lax.fori_loop(0, tt, row, carry_ref[...], unroll=8)
        gate = g_ref[...]
        y_ref[...] = h_ref[...] * (gate * _sigmoid(gate))

    return pl.pallas_call(
        body, name=name, grid=(T // tt,),
        in_specs=[pl.BlockSpec((tt, D), lambda i: (i, 0)),
                  pl.BlockSpec((SUBLANES, D), lambda i: (jnp.maximum(i * hb - 1, 0), 0)),
                  pl.BlockSpec((tt, D), lambda i: (i, 1)),
                  pl.BlockSpec((SUBLANES, D), lambda i: (0, 0)),
                  pl.BlockSpec((RNN_BLOCKS, RNN_BLOCK_WIDTH, RNN_BLOCK_WIDTH), lambda i: (0, 0, 0)),
                  pl.BlockSpec((RNN_BLOCKS, RNN_BLOCK_WIDTH, RNN_BLOCK_WIDTH), lambda i: (0, 0, 0))],
        out_specs=[pl.BlockSpec((tt, D), lambda i: (i, 0)), pl.BlockSpec((tt, D), lambda i: (i, 0))],
        out_shape=[jax.ShapeDtypeStruct((T, D), F32), jax.ShapeDtypeStruct((T, D), F32)],
        scratch_shapes=[pltpu.VMEM((tt + SUBLANES, D), F32), pltpu.VMEM((tt, D), F32),
                        pltpu.VMEM((tt, D), F32), pltpu.VMEM((1, D), F32)],
        compiler_params=_cparams("arbitrary"),
    )(proj, proj, proj, small, wa, wi)


def _rg_bwd(proj, hs, dy, small, wa, wi, *, tt, name):
    T = proj.shape[0]
    D = RNN_BLOCKS * RNN_BLOCK_WIDTH
    W = RNN_BLOCK_WIDTH
    hb = tt // SUBLANES
    nt = T // tt

    def body(u0_ref, uhalo_ref, g_ref, h_ref, hhalo_ref, dy_ref, small_ref, wa_ref, wi_ref,
             dp_ref, dwa_ref, dwi_ref, ds_ref,
             upad_ref, hpad_ref, a_ref, g_s_ref, duext_ref, carry_ref):
        i = pl.program_id(0)
        first_chunk = i == nt - 1

        @pl.when(i == 0)
        def _():
            carry_ref[...] = jnp.zeros_like(carry_ref)
            duext_ref[...] = jnp.zeros_like(duext_ref)
            dwa_ref[...] = jnp.zeros_like(dwa_ref)
            dwi_ref[...] = jnp.zeros_like(dwi_ref)
            ds_ref[...] = jnp.zeros_like(ds_ref)

        upad_ref[0:SUBLANES, :] = jnp.where(first_chunk, 0.0, uhalo_ref[...])
        upad_ref[SUBLANES:, :] = u0_ref[...]
        hpad_ref[0:SUBLANES, :] = jnp.where(first_chunk, 0.0, hhalo_ref[...])
        hpad_ref[SUBLANES:, :] = h_ref[...]
        u, r, ig, spl, a, s = _rg_gates(upad_ref, small_ref, wa_ref, wi_ref, tt)
        gate = g_ref[...]
        sg = _sigmoid(gate)
        dy = dy_ref[...]
        dp_ref[:, D:] = dy * h_ref[...] * (sg * (1.0 + gate * (1.0 - sg)))
        a_ref[...] = a
        g_s_ref[...] = dy * (gate * sg)

        def row(k, c):
            t = tt - 1 - k
            g = g_s_ref[pl.ds(t, 1), :] + c
            g_s_ref[pl.ds(t, 1), :] = g
            return a_ref[pl.ds(t, 1), :] * g

        carry_ref[...] = lax.fori_loop(0, tt, row, carry_ref[...], unroll=8)
        g = g_s_ref[...]
        h_prev = hpad_ref[SUBLANES - 1:SUBLANES - 1 + tt, :]
        iu = ig * u
        d_iu = g * s
        dlog_a = (g * h_prev) * a - (g * iu) * (a * a) / s
        dpre_a = (dlog_a * ((-LRU_C) * spl)) * r * (1.0 - r)
        dpre_i = (d_iu * u) * ig * (1.0 - ig)
        dlam = jnp.sum(dlog_a * r, axis=0, keepdims=True) * (LRU_C * _sigmoid(-small_ref[7:8, :]))
        du_parts = []
        for n in range(RNN_BLOCKS):
            sl = slice(n * W, (n + 1) * W)
            ub = u[:, sl].astype(BF16)
            da_n = dpre_a[:, sl].astype(BF16)
            di_n = dpre_i[:, sl].astype(BF16)
            dwa_ref[n] += lax.dot_general(ub, da_n, (((0,), (0,)), ((), ())),
                                          preferred_element_type=F32)
            dwi_ref[n] += lax.dot_general(ub, di_n, (((0,), (0,)), ((), ())),
                                          preferred_element_type=F32)
            du_parts.append(
                lax.dot_general(da_n, wa_ref[n], (((1,), (1,)), ((), ())), preferred_element_type=F32)
                + lax.dot_general(di_n, wi_ref[n], (((1,), (1,)), ((), ())), preferred_element_type=F32))
        du = d_iu * ig + jnp.concatenate(du_parts, axis=-1)
        off = SUBLANES - (CONV_WIDTH - 1)
        for tap in range(CONV_WIDTH):
            ds_ref[tap:tap + 1, :] += jnp.sum(du * upad_ref[off + tap:off + tap + tt, :],
                                              axis=0, keepdims=True)
        ds_ref[4:5, :] += jnp.sum(du, axis=0, keepdims=True)
        ds_ref[5:6, :] += jnp.sum(dpre_a, axis=0, keepdims=True)
        ds_ref[6:7, :] += jnp.sum(dpre_i, axis=0, keepdims=True)
        ds_ref[7:8, :] += dlam
        duext_ref[0:tt, :] = du
        du0 = jnp.zeros((tt, D), F32)
        for tap in range(CONV_WIDTH):
            sh = CONV_WIDTH - 1 - tap
            du0 = du0 + duext_ref[sh:sh + tt, :] * small_ref[tap:tap + 1, :]
        dp_ref[:, :D] = du0
        duext_ref[tt:, :] = du[0:SUBLANES, :]

    rev = lambda i: nt - 1 - i
    wspec = pl.BlockSpec((RNN_BLOCKS, W, W), lambda i: (0, 0, 0))
    return pl.pallas_call(
        body, name=name, grid=(nt,),
        in_specs=[pl.BlockSpec((tt, D), lambda i: (rev(i), 0)),
                  pl.BlockSpec((SUBLANES, D), lambda i: (jnp.maximum(rev(i) * hb - 1, 0), 0)),
                  pl.BlockSpec((tt, D), lambda i: (rev(i), 1)),
                  pl.BlockSpec((tt, D), lambda i: (rev(i), 0)),
                  pl.BlockSpec((SUBLANES, D), lambda i: (jnp.maximum(rev(i) * hb - 1, 0), 0)),
                  pl.BlockSpec((tt, D), lambda i: (rev(i), 0)),
                  pl.BlockSpec((SUBLANES, D), lambda i: (0, 0)),
                  wspec, wspec],
        out_specs=[pl.BlockSpec((tt, 2 * D), lambda i: (rev(i), 0)),
                   wspec, wspec, pl.BlockSpec((SUBLANES, D), lambda i: (0, 0))],
        out_shape=[jax.ShapeDtypeStruct((T, 2 * D), F32),
                   jax.ShapeDtypeStruct((RNN_BLOCKS, W, W), F32),
                   jax.ShapeDtypeStruct((RNN_BLOCKS, W, W), F32),
                   jax.ShapeDtypeStruct((SUBLANES, D), F32)],
        scratch_shapes=[pltpu.VMEM((tt + SUBLANES, D), F32), pltpu.VMEM((tt + SUBLANES, D), F32),
                        pltpu.VMEM((tt, D), F32), pltpu.VMEM((tt, D), F32),
                        pltpu.VMEM((tt + SUBLANES, D), F32), pltpu.VMEM((1, D), F32)],
        compiler_params=_cparams("arbitrary"),
    )(proj, proj, proj, hs, hs, dy, small, wa, wi)


def _ln_fwd(x, h, g, b, *, tt, name):
    T, D = x.shape

    def body(x_ref, h_ref, g_ref, b_ref, y_ref, zh_ref, rs_ref):
        z = ALPHA * x_ref[...] + h_ref[...]
        mu = jnp.mean(z, axis=-1, keepdims=True)
        zc = z - mu
        rstd = lax.rsqrt(jnp.mean(zc * zc, axis=-1, keepdims=True) + LN_EPS)
        zh = zc * rstd
        zh_ref[...] = zh
        rs_ref[...] = rstd
        y_ref[...] = zh * g_ref[...] + b_ref[...]

    blk = pl.BlockSpec((tt, D), lambda i: (i, 0))
    row = pl.BlockSpec((1, D), lambda i: (0, 0))
    return pl.pallas_call(
        body, name=name, grid=(T // tt,),
        in_specs=[blk, blk, row, row],
        out_specs=[blk, blk, pl.BlockSpec((tt, 1), lambda i: (i, 0))],
        out_shape=[jax.ShapeDtypeStruct((T, D), F32), jax.ShapeDtypeStruct((T, D), F32),
                   jax.ShapeDtypeStruct((T, 1), F32)],
        compiler_params=_cparams("parallel"),
    )(x, h, g, b)


def _ln_bwd(dy, zh, rstd, g, *, tt, name):
    T, D = dy.shape

    def body(dy_ref, zh_ref, rs_ref, g_ref, dz_ref, dg_ref, db_ref):
        @pl.when(pl.program_id(0) == 0)
        def _():
            dg_ref[...] = jnp.zeros_like(dg_ref)
            db_ref[...] = jnp.zeros_like(db_ref)

        dy = dy_ref[...]
        zh = zh_ref[...]
        dg_ref[...] += jnp.sum(dy * zh, axis=0, keepdims=True)
        db_ref[...] += jnp.sum(dy, axis=0, keepdims=True)
        dzh = dy * g_ref[...]
        m1 = jnp.mean(dzh, axis=-1, keepdims=True)
        m2 = jnp.mean(dzh * zh, axis=-1, keepdims=True)
        dz_ref[...] = rs_ref[...] * (dzh - m1 - zh * m2)

    blk = pl.BlockSpec((tt, D), lambda i: (i, 0))
    row = pl.BlockSpec((1, D), lambda i: (0, 0))
    return pl.pallas_call(
        body, name=name, grid=(T // tt,),
        in_specs=[blk, blk, pl.BlockSpec((tt, 1), lambda i: (i, 0)), row],
        out_specs=[blk, row, row],
        out_shape=[jax.ShapeDtypeStruct((T, D), F32), jax.ShapeDtypeStruct((1, D), F32),
                   jax.ShapeDtypeStruct((1, D), F32)],
        compiler_params=_cparams("arbitrary"),
    )(dy, zh, rstd, g)


def _loss(y, tgt, *, tt, name):
    T, D = y.shape

    def body(y_ref, t_ref, l_ref, dy_ref):
        @pl.when(pl.program_id(0) == 0)
        def _():
            l_ref[...] = jnp.zeros_like(l_ref)

        e = y_ref[...] - t_ref[...]
        dy_ref[...] = e * (1.0 / D)
        l_ref[...] += jnp.sum(e * e, axis=0, keepdims=True) * (0.5 / D)

    blk = pl.BlockSpec((tt, D), lambda i: (i, 0))
    return pl.pallas_call(
        body, name=name, grid=(T // tt,),
        in_specs=[blk, blk], out_specs=[pl.BlockSpec((1, D), lambda i: (0, 0)), blk],
        out_shape=[jax.ShapeDtypeStruct((1, D), F32), jax.ShapeDtypeStruct((T, D), F32)],
        compiler_params=_cparams("arbitrary"),
    )(y, tgt)


def _row_tile(rows, target):
    best = SUBLANES
    for t in range(SUBLANES, target + 1, SUBLANES):
        if rows % t == 0:
            best = t
    return best


def _add_own(g, recv, c_idx, *, tr, name):
    _, M, R, C = g.shape

    def body(c_ref, g_ref, r_ref, o_ref):
        o_ref[...] = g_ref[0] + r_ref[...]

    return pl.pallas_call(
        body, name=name,
        grid_spec=pltpu.PrefetchScalarGridSpec(
            num_scalar_prefetch=1, grid=(M, R // tr),
            in_specs=[pl.BlockSpec((1, 1, tr, C), lambda k, i, c: (c[0], k, i, 0)),
                      pl.BlockSpec((1, tr, C), lambda k, i, c: (k, i, 0))],
            out_specs=pl.BlockSpec((1, tr, C), lambda k, i, c: (k, i, 0))),
        out_shape=jax.ShapeDtypeStruct((M, R, C), F32),
        compiler_params=_cparams("parallel", "parallel"),
    )(c_idx, g, recv)


def _adamw_math(g, w_ref, m_ref, v_ref, g_ref, d_ref, nm_ref, nv_ref):
    nm = ADAM_B1 * m_ref[...] + (1.0 - ADAM_B1) * g
    nv = ADAM_B2 * v_ref[...] + (1.0 - ADAM_B2) * (g * g)
    m_hat = nm / (1.0 - ADAM_B1 ** ADAM_STEP)
    v_hat = nv / (1.0 - ADAM_B2 ** ADAM_STEP)
    g_ref[...] = g
    nm_ref[...] = nm
    nv_ref[...] = nv
    d_ref[...] = (-ADAM_LR) * (m_hat / (jnp.sqrt(v_hat) + ADAM_EPS) + ADAM_WD * w_ref[...])


def _adamw(parts, w, m, v, *, tr, name):
    n, R, C = parts.shape
    tr = min(tr, R)

    def body(p_ref, w_ref, m_ref, v_ref, *out_refs):
        g = p_ref[0]
        for k in range(1, n):
            g = g + p_ref[k]
        _adamw_math(g, w_ref, m_ref, v_ref, *out_refs)

    blk = pl.BlockSpec((tr, C), lambda i: (i, 0))
    out = jax.ShapeDtypeStruct((R, C), F32)
    return pl.pallas_call(
        body, name=name, grid=(R // tr,),
        in_specs=[pl.BlockSpec((n, tr, C), lambda i: (0, i, 0)), blk, blk, blk],
        out_specs=[blk, blk, blk, blk], out_shape=[out, out, out, out],
        compiler_params=_cparams("parallel"),
    )(parts, w, m, v)


def _adamw_shard(h, recv, me_idx, w, m, v, *, tr, name):
    _, L, R, C = h.shape

    def body(me_ref, h_ref, r1_ref, r2_ref, r3_ref, w_ref, m_ref, v_ref, *out_refs):
        g = ((h_ref[0] + r1_ref[0]) + r2_ref[0]) + r3_ref[0]
        _adamw_math(g, w_ref, m_ref, v_ref, *out_refs)

    blk = pl.BlockSpec((1, tr, C), lambda l, i, me: (l, i, 0))

    def slot(d):
        return pl.BlockSpec((1, 1, tr, C), lambda l, i, me: (me[0] ^ d, l, i, 0))

    out = jax.ShapeDtypeStruct((L, R, C), F32)
    return pl.pallas_call(
        body, name=name,
        grid_spec=pltpu.PrefetchScalarGridSpec(
            num_scalar_prefetch=1, grid=(L, R // tr),
            in_specs=[slot(0), slot(1), slot(2), slot(3), blk, blk, blk],
            out_specs=[blk, blk, blk, blk]),
        out_shape=[out, out, out, out],
        compiler_params=_cparams("parallel", "parallel"),
    )(me_idx, h, recv, recv, recv, w, m, v)


ROWED = ("attn_w_out", "rnn_w_out", "rnn_w_a", "rnn_w_i")
SMALL = ("rnn_conv_w", "rnn_conv_b", "rnn_b_a", "rnn_b_i", "rnn_lambda")
PACK_C = 1024


def _elems(shape):
    n = 1
    for s in shape:
        n *= s
    return n


def _pack_rows(p, dtype):
    parts = [p[k].astype(dtype).reshape(-1, PACK_C) for k in ROWED]
    small = jnp.concatenate([p[k].reshape(-1) for k in SMALL])
    tile_rows = SUBLANES * (4 // jnp.dtype(dtype).itemsize)
    if dtype == BF16:
        small = lax.bitcast_convert_type(small, BF16)
    small = small.reshape(-1, PACK_C)
    parts.append(jnp.pad(small, ((0, tile_rows - small.shape[0]), (0, 0))))
    return jnp.concatenate(parts, axis=0)


def _unpack_rows(flat, shapes):
    out, r = {}, 0
    for k in ROWED:
        n = _elems(shapes[k]) // PACK_C
        out[k] = flat[r:r + n].reshape(shapes[k])
        r += n
    n_small = sum(_elems(shapes[k]) for k in SMALL)
    small = flat[r:r + n_small // PACK_C].reshape(-1)
    o = 0
    for k in SMALL:
        n = _elems(shapes[k])
        out[k] = small[o:o + n].reshape(shapes[k])
        o += n
    return out


def _to_full(g, k, sh):
    nd = len(sh)
    if k in ("attn_w_in", "rnn_w_in", "rnn_conv_w"):
        perm = tuple(range(2, 2 + nd - 1)) + (1, 0, 2 + nd - 1)
        t = g.transpose(perm)
        return t.reshape(sh[:-1] + (8 * sh[-1],))
    if k in ("attn_w_out", "rnn_w_out", "rnn_conv_b", "rnn_b_a", "rnn_b_i", "rnn_lambda"):
        perm = (2, 1, 0) + tuple(range(3, 2 + nd))
        t = g.transpose(perm)
        return t.reshape((sh[0], 8 * sh[1]) + sh[2:])
    perm = (2, 3, 1, 0, 4, 5)
    t = g.transpose(perm)
    return t.reshape((sh[0], sh[1], 8 * sh[2], sh[3]))


def _from_full(full, k, sh):
    nd = len(sh)
    if k in ("attn_w_in", "rnn_w_in", "rnn_conv_w"):
        t = full.reshape(sh[:-1] + (4, 2, sh[-1]))
        perm = (nd, nd - 1) + tuple(range(nd - 1)) + (nd + 1,)
        return t.transpose(perm)
    if k in ("attn_w_out", "rnn_w_out", "rnn_conv_b", "rnn_b_a", "rnn_b_i", "rnn_lambda"):
        t = full.reshape((sh[0], 4, 2) + sh[1:])
        perm = (2, 1, 0) + tuple(range(3, 2 + nd))
        return t.transpose(perm)
    t = full.reshape((sh[0], sh[1], 4, 2, sh[2], sh[3]))
    return t.transpose((3, 2, 0, 1, 4, 5))


def _unpack_gathered_rows(g, shapes):
    out, r = {}, 0
    for k in ROWED:
        n = _elems(shapes[k]) // PACK_C
        out[k] = _to_full(g[:, :, r:r + n].reshape((2, 4) + shapes[k]), k, shapes[k])
        r += n
    n_small = sum(_elems(shapes[k]) for k in SMALL)
    nr = 2 * n_small // PACK_C
    small = lax.bitcast_convert_type(g[:, :, r:r + nr].reshape(2, 4, n_small, 2), F32)
    o = 0
    for k in SMALL:
        n = _elems(shapes[k])
        out[k] = _to_full(small[:, :, o:o + n].reshape((2, 4) + shapes[k]), k, shapes[k])
        o += n
    return out


def _pack_grad_rows(full, shapes):
    parts = [_from_full(full[k], k, shapes[k]).reshape(2, 4, -1, PACK_C) for k in ROWED]
    small = jnp.concatenate(
        [_from_full(full[k], k, shapes[k]).reshape(2, 4, -1) for k in SMALL], axis=-1)
    small = small.reshape(2, 4, -1, PACK_C)
    parts.append(jnp.pad(small, ((0, 0), (0, 0), (0, SUBLANES - small.shape[2]), (0, 0))))
    return jnp.concatenate(parts, axis=2)


def kernel(x, ln_g, ln_b, attn_w_in, attn_b_f, attn_w_out, rnn_w_in, rnn_conv_w, rnn_conv_b, rnn_w_a, rnn_b_a, rnn_w_i, rnn_b_i, rnn_lambda, rnn_w_out, loss_target, m_ln_g, m_ln_b, m_attn_w_in, m_attn_b_f, m_attn_w_out, m_rnn_w_in, m_rnn_conv_w, m_rnn_conv_b, m_rnn_w_a, m_rnn_b_a, m_rnn_w_i, m_rnn_b_i, m_rnn_lambda, m_rnn_w_out, v_ln_g, v_ln_b, v_attn_w_in, v_attn_b_f, v_attn_w_out, v_rnn_w_in, v_rnn_conv_w, v_rnn_conv_b, v_rnn_w_a, v_rnn_b_a, v_rnn_w_i, v_rnn_b_i, v_rnn_lambda, v_rnn_w_out):
    w_loc = dict(attn_w_in=attn_w_in, attn_w_out=attn_w_out, rnn_w_in=rnn_w_in, rnn_w_a=rnn_w_a,
                 rnn_w_i=rnn_w_i, rnn_w_out=rnn_w_out, rnn_conv_w=rnn_conv_w, rnn_conv_b=rnn_conv_b,
                 rnn_b_a=rnn_b_a, rnn_b_i=rnn_b_i, rnn_lambda=rnn_lambda)
    m_loc = dict(attn_w_in=m_attn_w_in, attn_w_out=m_attn_w_out, rnn_w_in=m_rnn_w_in,
                 rnn_w_a=m_rnn_w_a, rnn_w_i=m_rnn_w_i, rnn_w_out=m_rnn_w_out,
                 rnn_conv_w=m_rnn_conv_w, rnn_conv_b=m_rnn_conv_b, rnn_b_a=m_rnn_b_a,
                 rnn_b_i=m_rnn_b_i, rnn_lambda=m_rnn_lambda)
    v_loc = dict(attn_w_in=v_attn_w_in, attn_w_out=v_attn_w_out, rnn_w_in=v_rnn_w_in,
                 rnn_w_a=v_rnn_w_a, rnn_w_i=v_rnn_w_i, rnn_w_out=v_rnn_w_out,
                 rnn_conv_w=v_rnn_conv_w, rnn_conv_b=v_rnn_conv_b, rnn_b_a=v_rnn_b_a,
                 rnn_b_i=v_rnn_b_i, rnn_lambda=v_rnn_lambda)
    shapes = {k: tuple(a.shape) for k, a in w_loc.items()}
    T, D = x.shape[1], x.shape[2]
    n_f = attn_b_f.shape[1]
    tb = min(1024, T)
    tt_rg = min(128, T)
    tt_ln = min(256, T)

    g_ain, g_rin, g_rows = _ag_c(_ag_xy([attn_w_in.astype(BF16), rnn_w_in.astype(BF16),
                                         _pack_rows(w_loc, BF16)], "ag_w_xy"), "ag_w_c")
    W = _unpack_gathered_rows(g_rows, shapes)
    W["attn_w_in"] = _to_full(g_ain, "attn_w_in", shapes["attn_w_in"])
    W["rnn_w_in"] = _to_full(g_rin, "rnn_w_in", shapes["rnn_w_in"])
    w_in_a = jnp.pad(W["attn_w_in"], ((0, 0), (0, 0), (0, LANES - n_f)))
    small_r = jnp.concatenate([W["rnn_conv_w"], W["rnn_conv_b"][:, None], W["rnn_b_a"][:, None],
                               W["rnn_b_i"][:, None], W["rnn_lambda"][:, None]], axis=1)
    bf_rows = jnp.pad(attn_b_f, ((0, 0), (0, LANES - n_f)))[:, None, :]

    xs, saved = [x[0]], []
    for layer in range(DEPTH):
        idx, xl = layer // 2, xs[-1]
        if layer % 2 == 0:
            proj = _matmul(xl, w_in_a[idx], trans_b=False, tm=512, tn=1408, name=f"a_proj{layer}")
            cum_t = _cumsum_fwd(proj, bf_rows[idx], tt=min(512, T), name=f"a_cum{layer}")
            cum4 = cum_t[:N_HEADS].reshape(N_PAIRS, 2, T // tb, tb)
            o, og, lp = _flash_fwd(proj, cum4, tb=tb, name=f"a_fwd{layer}")
            hbr = _matmul(og, W["attn_w_out"][idx], trans_b=False, tm=512, tn=1024,
                          name=f"a_out{layer}")
            saved.append((proj, cum4, o, og, lp))
        else:
            proj = _matmul(xl, W["rnn_w_in"][idx], trans_b=False, tm=512, tn=1024,
                           name=f"r_proj{layer}")
            hs, yr = _rg_fwd(proj, small_r[idx], W["rnn_w_a"][idx], W["rnn_w_i"][idx],
                             tt=tt_rg, name=f"r_fwd{layer}")
            hbr = _matmul(yr, W["rnn_w_out"][idx], trans_b=False, tm=512, tn=1024,
                          name=f"r_out{layer}")
            saved.append((proj, hs, yr))
        y, zh, rstd = _ln_fwd(xl, hbr, ln_g[layer][None], ln_b[layer][None], tt=tt_ln,
                              name=f"ln_fwd{layer}")
        saved[-1] = saved[-1] + (zh, rstd)
        xs.append(y)

    loss_lanes, dy = _loss(xs[-1], loss_target[0], tt=tt_ln, name="loss")
    loss = lax.psum(jnp.sum(loss_lanes), ("x", "y", "c"))

    full_g = {k: [None, None] for k in w_loc}
    d_ln_g, d_ln_b, d_bf = [None] * DEPTH, [None] * DEPTH, [None, None]
    for layer in reversed(range(DEPTH)):
        idx, xl = layer // 2, xs[layer]
        zh, rstd = saved[layer][-2:]
        dz, dg, db = _ln_bwd(dy, zh, rstd, ln_g[layer][None], tt=tt_ln, name=f"ln_bwd{layer}")
        d_ln_g[layer], d_ln_b[layer] = dg[0], db[0]
        if layer % 2 == 0:
            proj, cum4, o, og, lp = saved[layer][:5]
            dog = _matmul(dz, W["attn_w_out"][idx], trans_b=True, tm=512, tn=1024,
                          name=f"a_dog{layer}")
            full_g["attn_w_out"][idx] = _matmul_tn(og, dz, tm=512, tn=1024, tk=512,
                                                   name=f"a_dwo{layer}")
            dq, dgate, do, delta, dcum_q = _flash_bwd_dq(proj, cum4, o, dog, lp, tb=tb,
                                                         name=f"a_dq{layer}")
            dk, dv, dcum_k = _flash_bwd_dkv(proj, cum4, do, lp, delta, tb=tb,
                                            name=f"a_dkv{layer}")
            dcum_t = (dcum_q + dcum_k).transpose(0, 2, 1, 3).reshape(N_HEADS, T)
            dcum_t = jnp.pad(dcum_t, ((0, LANES - N_HEADS), (0, 0)))
            df, dbf = _cumsum_bwd(dcum_t, proj, bf_rows[idx], tt=min(512, T), name=f"a_dcum{layer}")
            d_bf[idx] = dbf[0, :n_f]
            dproj = jnp.concatenate([dq, dk, dv, dgate, df], axis=1)
            dwi = _matmul_tn(xl, dproj, tm=512, tn=1408, tk=512, name=f"a_dwi{layer}")
            full_g["attn_w_in"][idx] = dwi[:, :4 * D + n_f]
            dy = _matmul(dproj, w_in_a[idx], trans_b=True, tm=256, tn=512, name=f"a_dx{layer}",
                         add=dz, add_scale=ALPHA)
        else:
            proj, hs, yr = saved[layer][:3]
            dyr = _matmul(dz, W["rnn_w_out"][idx], trans_b=True, tm=512, tn=1024,
                          name=f"r_dy{layer}")
            full_g["rnn_w_out"][idx] = _matmul_tn(yr, dz, tm=512, tn=1024, tk=512,
                                                  name=f"r_dwo{layer}")
            dproj, dwa, dwi_, dsm = _rg_bwd(proj, hs, dyr, small_r[idx], W["rnn_w_a"][idx],
                                            W["rnn_w_i"][idx], tt=tt_rg, name=f"r_bwd{layer}")
            full_g["rnn_w_a"][idx], full_g["rnn_w_i"][idx] = dwa, dwi_
            full_g["rnn_conv_w"][idx] = dsm[0:4]
            for r, k in enumerate(("rnn_conv_b", "rnn_b_a", "rnn_b_i", "rnn_lambda")):
                full_g[k][idx] = dsm[4 + r]
            full_g["rnn_w_in"][idx] = _matmul_tn(xl, dproj, tm=512, tn=1024, tk=512,
                                                 name=f"r_dwi{layer}")
            dy = _matmul(dproj, W["rnn_w_in"][idx], trans_b=True, tm=512, tn=512,
                         name=f"r_dx{layer}", add=dz, add_scale=ALPHA)
    grad_x = dy[None]

    fg = {k: jnp.stack(v) for k, v in full_g.items()}
    c_idx = lax.axis_index("c").astype(jnp.int32).reshape(1)
    me_idx = (2 * lax.axis_index("x") + lax.axis_index("y")).astype(jnp.int32).reshape(1)
    big = ("attn_w_in", "rnn_w_in")
    gs = [_from_full(fg[k], k, shapes[k]) for k in big] + [_pack_grad_rows(fg, shapes)]
    gs = [g.reshape((2, -1) + g.shape[-2:]) for g in gs]
    recv = _rs_c(gs, "rs_c")
    hs = [_add_own(g, r, c_idx, tr=_row_tile(g.shape[2], 512), name=f"rs_add{n}")
          for n, (g, r) in enumerate(zip(gs, recv))]
    hs = [h.reshape((4, -1) + h.shape[-2:]) for h in hs]
    quad = _rs_xy(hs, "rs_xy")
    wmv = [[d[k] for k in big] + [_pack_rows(d, F32)[None]] for d in (w_loc, m_loc, v_loc)]
    res = [_adamw_shard(hs[n], quad[n], me_idx, wmv[0][n], wmv[1][n], wmv[2][n],
                        tr=_row_tile(hs[n].shape[2], 256), name=f"adamw{n}") for n in range(3)]
    shard_outs = []
    for j in range(4):
        d = _unpack_rows(res[2][j][0], shapes)
        d["attn_w_in"], d["rnn_w_in"] = res[0][j], res[1][j]
        shard_outs.append(d)
    g_sh, d_sh, nm_sh, nv_sh = shard_outs

    def rep_pack(lg, lb, bf):
        rows = jnp.concatenate([lg, lb, jnp.pad(bf.reshape(1, -1), ((0, 0), (0, D - 2 * n_f)))])
        return jnp.pad(rows, ((0, 16 - rows.shape[0]), (0, 0)))

    rep = _all_gather(rep_pack(jnp.stack(d_ln_g), jnp.stack(d_ln_b), jnp.stack(d_bf)), "ag_rep")
    rg, rd, rm, rv = _adamw(rep.reshape(8, 16, D), rep_pack(ln_g, ln_b, attn_b_f),
                            rep_pack(m_ln_g, m_ln_b, m_attn_b_f),
                            rep_pack(v_ln_g, v_ln_b, v_attn_b_f), tr=16, name="adamw_rep")

    def rep_unpack(a):
        return dict(ln_g=a[0:DEPTH], ln_b=a[DEPTH:2 * DEPTH],
                    attn_b_f=a[2 * DEPTH, :2 * n_f].reshape(2, n_f))

    order = ("ln_g", "ln_b", "attn_w_in", "attn_b_f", "attn_w_out", "rnn_w_in", "rnn_conv_w",
             "rnn_conv_b", "rnn_w_a", "rnn_b_a", "rnn_w_i", "rnn_b_i", "rnn_lambda", "rnn_w_out")
    outs = [loss, grad_x]
    for sh, rp in ((g_sh, rg), (d_sh, rd), (nm_sh, rm), (nv_sh, rv)):
        allp = {**sh, **rep_unpack(rp)}
        outs.extend(allp[k] for k in order)
    return tuple(outs)
```

```python
import functools

import jax
import jax.numpy as jnp
from jax import lax
from jax.experimental import pallas as pl
from jax.experimental.pallas import tpu as pltpu

F32 = jnp.float32
BF16 = jnp.bfloat16

DEPTH = 4
N_HEADS = 16
HEAD_DIM = 64
N_PAIRS = N_HEADS // 2
RNN_BLOCKS = 4
RNN_BLOCK_WIDTH = 256
CONV_WIDTH = 4
LRU_C = 8.0
ALPHA = (2.0 * DEPTH) ** 0.25
LN_EPS = 1e-5
ADAM_LR, ADAM_B1, ADAM_B2, ADAM_EPS, ADAM_WD, ADAM_STEP = 0.001, 0.9, 0.999, 1e-8, 0.01, 10

LANES = 128
SUBLANES = 8
VMEM_LIMIT = 48 * 1024 * 1024

MESH = pl.DeviceIdType.MESH
HBM_SPEC = pl.BlockSpec(memory_space=pltpu.HBM)


def _cparams(*sem):
    return pltpu.CompilerParams(dimension_semantics=sem, vmem_limit_bytes=VMEM_LIMIT)


def _sigmoid(x):
    return 1.0 / (1.0 + jnp.exp(-x))


def _softplus(x):
    return jnp.maximum(x, 0.0) + jnp.log(1.0 + jnp.exp(-jnp.abs(x)))


def _a2a(src, *, group, bcast, name):
    n = 2 if group == "c" else 4
    blk = tuple(src.shape) if bcast else tuple(src.shape[1:])

    def body(src_ref, out_ref, send_sems, recv_sems, local_sem):
        x, y, c = lax.axis_index("x"), lax.axis_index("y"), lax.axis_index("c")
        if group == "c":
            me = c

            def peer(d):
                return (x, y, 1 - c), 1 - c
        else:
            me = 2 * x + y

            def peer(d):
                px, py = x ^ (d >> 1), y ^ (d & 1)
                return (px, py, c), 2 * px + py

        def block_for(k):
            return src_ref if bcast else src_ref.at[k]

        local = pltpu.make_async_copy(block_for(me), out_ref.at[me], local_sem)
        local.start()
        sends = []
        for d in range(1, n):
            dev, idx = peer(d)
            cp = pltpu.make_async_remote_copy(
                src_ref=block_for(idx), dst_ref=out_ref.at[me],
                send_sem=send_sems.at[d], recv_sem=recv_sems.at[d],
                device_id=dev, device_id_type=MESH)
            cp.start()
            sends.append(cp)
        for d in range(1, n):
            dev, idx = peer(d)
            pltpu.make_async_remote_copy(
                src_ref=block_for(idx), dst_ref=out_ref.at[idx],
                send_sem=send_sems.at[d], recv_sem=recv_sems.at[d],
                device_id=dev, device_id_type=MESH).wait_recv()
        for cp in sends:
            cp.wait_send()
        local.wait()

    return pl.pallas_call(
        body, name=name,
        out_shape=jax.ShapeDtypeStruct((n,) + blk, src.dtype),
        in_specs=[HBM_SPEC], out_specs=HBM_SPEC,
        scratch_shapes=[pltpu.SemaphoreType.DMA((n,)), pltpu.SemaphoreType.DMA((n,)),
                        pltpu.SemaphoreType.DMA],
    )(src)


def _all_gather(piece, name):
    return _a2a(_a2a(piece, group="xy", bcast=True, name=name + "_xy"),
                group="c", bcast=True, name=name + "_c")


D2D_CHUNKS = 16
ICI_CHUNKS = 8


def _row_chunks(rows, dtype, k):
    unit = SUBLANES * (4 // jnp.dtype(dtype).itemsize)
    assert rows % unit == 0
    units = rows // unit
    k = max(1, min(k, units))
    base, rem = divmod(units, k)
    out, r = [], 0
    for i in range(k):
        n = (base + (1 if i < rem else 0)) * unit
        out.append((r, n))
        r += n
    return out


def _chunks(shape, dtype, k):
    if len(shape) == 2:
        return [(pl.ds(r0, n),) for r0, n in _row_chunks(shape[0], dtype, k)]
    per = max(1, k // shape[0])
    return [(l, pl.ds(r0, n)) for l in range(shape[0]) for r0, n in _row_chunks(shape[1], dtype, per)]


def _mesh_place():
    x, y, c = lax.axis_index("x"), lax.axis_index("y"), lax.axis_index("c")
    return x, y, c, 2 * x + y


def _chip_peer(x, y, c, d):
    px, py = x ^ (d >> 1), y ^ (d & 1)
    return (px, py, c), 2 * px + py


def _remote(src, dst, send_sem, recv_sem, dev):
    return pltpu.make_async_remote_copy(src_ref=src, dst_ref=dst, send_sem=send_sem,
                                        recv_sem=recv_sem, device_id=dev, device_id_type=MESH)


def _comm_call(body, name, ins, out_shapes, n_sems, aliases=None):
    n = len(ins)
    return pl.pallas_call(
        body, name=name,
        out_shape=out_shapes, in_specs=[HBM_SPEC] * n, out_specs=[HBM_SPEC] * n,
        input_output_aliases=aliases or {},
        scratch_shapes=[pltpu.SemaphoreType.DMA((n_sems, n)), pltpu.SemaphoreType.DMA((n_sems, n))],
    )(*ins)


class _Exchange:
    def __init__(self, kind, arrays):
        self.kind, self.arrays, self.n = kind, list(arrays), len(arrays)
        if kind == "gather":
            self.chunks = [_chunks(a.shape, a.dtype, ICI_CHUNKS) for a in arrays]
            self.out_shapes = [jax.ShapeDtypeStruct((2, 4) + tuple(a.shape), a.dtype) for a in arrays]
        else:
            self.chunks = [_chunks(a.shape[1:], a.dtype, ICI_CHUNKS) for a in arrays]
            self.out_shapes = [jax.ShapeDtypeStruct(a.shape, a.dtype) for a in arrays]
        self.sem_shapes = [pltpu.SemaphoreType.DMA((4, self.n)), pltpu.SemaphoreType.DMA((4, self.n))]

    def _blocks(self, srcs, outs, o, c, me, pidx):
        if self.kind == "gather":
            return srcs[o], outs[o].at[c, me], outs[o].at[c, pidx]
        return srcs[o].at[pidx], outs[o].at[me], outs[o].at[pidx]

    def start(self, srcs, outs, send_sems, recv_sems):
        x, y, c, me = _mesh_place()
        if self.kind == "gather":
            for o in range(self.n):
                for idx in self.chunks[o]:
                    pltpu.make_async_copy(srcs[o].at[idx], outs[o].at[(c, me) + idx],
                                          send_sems.at[0, o]).start()
        for d in range(1, 4):
            dev, pidx = _chip_peer(x, y, c, d)
            for o in range(self.n):
                src, dst, _ = self._blocks(srcs, outs, o, c, me, pidx)
                for idx in self.chunks[o]:
                    _remote(src.at[idx], dst.at[idx], send_sems.at[d, o], recv_sems.at[d, o],
                            dev).start()

    def wait(self, srcs, outs, send_sems, recv_sems):
        x, y, c, me = _mesh_place()
        for d in range(1, 4):
            dev, pidx = _chip_peer(x, y, c, d)
            for o in range(self.n):
                src, _, land = self._blocks(srcs, outs, o, c, me, pidx)
                _remote(src, land, send_sems.at[d, o], recv_sems.at[d, o], dev).wait_recv()
        for d in range(1, 4):
            dev, pidx = _chip_peer(x, y, c, d)
            for o in range(self.n):
                src, _, land = self._blocks(srcs, outs, o, c, me, pidx)
                _remote(src, land, send_sems.at[d, o], recv_sems.at[d, o], dev).wait_send()
        if self.kind == "gather":
            for o in range(self.n):
                pltpu.make_async_copy(srcs[o], outs[o].at[c, me], send_sems.at[0, o]).wait()


def _run_exchange(ex, name):
    n = ex.n

    def body(*refs):
        srcs, outs, send_sems, recv_sems = refs[:n], refs[n:2 * n], refs[2 * n], refs[2 * n + 1]
        ex.start(srcs, outs, send_sems, recv_sems)
        ex.wait(srcs, outs, send_sems, recv_sems)

    return _comm_call(body, name, ex.arrays, ex.out_shapes, 4)


def _ag_c(bufs, name):
    n = len(bufs)
    chunks = [_chunks(b.shape[2:], b.dtype, D2D_CHUNKS // 4) for b in bufs]

    def body(*refs):
        srcs, outs, send_sems, recv_sems = refs[:n], refs[n:2 * n], refs[2 * n], refs[2 * n + 1]
        x, y, c, _ = _mesh_place()
        sib = (x, y, 1 - c)
        for o in range(n):
            for k in range(4):
                for idx in chunks[o]:
                    _remote(srcs[o].at[(c, k) + idx], outs[o].at[(c, k) + idx],
                            send_sems.at[0, o], recv_sems.at[0, o], sib).start()
        for o in range(n):
            _remote(srcs[o].at[c], outs[o].at[1 - c], send_sems.at[0, o], recv_sems.at[0, o],
                    sib).wait_recv()
        for o in range(n):
            _remote(srcs[o].at[c], outs[o].at[1 - c], send_sems.at[0, o], recv_sems.at[0, o],
                    sib).wait_send()

    shapes = [jax.ShapeDtypeStruct(b.shape, b.dtype) for b in bufs]
    return _comm_call(body, name, bufs, shapes, 1, aliases={i: i for i in range(n)})


def _rs_c(gs, name):
    n = len(gs)
    chunks = [_chunks(g.shape[2:], g.dtype, max(1, D2D_CHUNKS // g.shape[1])) for g in gs]

    def body(*refs):
        srcs, outs, send_sems, recv_sems = refs[:n], refs[n:2 * n], refs[2 * n], refs[2 * n + 1]
        x, y, c, _ = _mesh_place()
        sib = (x, y, 1 - c)
        for o in range(n):
            for k in range(gs[o].shape[1]):
                for idx in chunks[o]:
                    _remote(srcs[o].at[(1 - c, k) + idx], outs[o].at[(k,) + idx],
                            send_sems.at[0, o], recv_sems.at[0, o], sib).start()
        for o in range(n):
            _remote(srcs[o].at[1 - c], outs[o], send_sems.at[0, o], recv_sems.at[0, o],
                    sib).wait_recv()
        for o in range(n):
            _remote(srcs[o].at[1 - c], outs[o], send_sems.at[0, o], recv_sems.at[0, o],
                    sib).wait_send()

    shapes = [jax.ShapeDtypeStruct(g.shape[1:], g.dtype) for g in gs]
    return _comm_call(body, name, gs, shapes, 1)


def _matmul(a, b, *, trans_b, tm, tn, name, add=None, add_scale=1.0):
    M, K = a.shape
    N = b.shape[0] if trans_b else b.shape[1]
    tm, tn = min(tm, M), min(tn, N)
    assert M % tm == 0 and N % tn == 0
    dn = (((1,), (1,)), ((), ())) if trans_b else (((1,), (0,)), ((), ()))

    def body(*refs):
        if add is None:
            a_ref, b_ref, o_ref = refs
        else:
            a_ref, b_ref, add_ref, o_ref = refs
        r = lax.dot_general(a_ref[...].astype(BF16), b_ref[...].astype(BF16), dn,
                            preferred_element_type=F32)
        if add is not None:
            r = r + add_scale * add_ref[...]
        o_ref[...] = r

    b_spec = (pl.BlockSpec((tn, K), lambda j, i: (j, 0)) if trans_b
              else pl.BlockSpec((K, tn), lambda j, i: (0, j)))
    in_specs = [pl.BlockSpec((tm, K), lambda j, i: (i, 0)), b_spec]
    args = [a, b]
    if add is not None:
        in_specs.append(pl.BlockSpec((tm, tn), lambda j, i: (i, j)))
        args.append(add)
    return pl.pallas_call(
        body, name=name, grid=(N // tn, M // tm),
        in_specs=in_specs, out_specs=pl.BlockSpec((tm, tn), lambda j, i: (i, j)),
        out_shape=jax.ShapeDtypeStruct((M, N), F32),
        compiler_params=_cparams("parallel", "parallel"),
    )(*args)


def _matmul_tn(a, b, *, tm, tn, tk, name):
    T, M = a.shape
    N = b.shape[1]
    tm, tn, tk = min(tm, M), min(tn, N), min(tk, T)
    assert M % tm == 0 and N % tn == 0 and T % tk == 0

    def body(a_ref, b_ref, o_ref):
        @pl.when(pl.program_id(2) == 0)
        def _():
            o_ref[...] = jnp.zeros_like(o_ref)

        o_ref[...] += lax.dot_general(a_ref[...].astype(BF16), b_ref[...].astype(BF16),
                                      (((0,), (0,)), ((), ())), preferred_element_type=F32)

    return pl.pallas_call(
        body, name=name, grid=(M // tm, N // tn, T // tk),
        in_specs=[pl.BlockSpec((tk, tm), lambda i, j, k: (k, i)),
                  pl.BlockSpec((tk, tn), lambda i, j, k: (k, j))],
        out_specs=pl.BlockSpec((tm, tn), lambda i, j, k: (i, j)),
        out_shape=jax.ShapeDtypeStruct((M, N), F32),
        compiler_params=_cparams("parallel", "parallel", "arbitrary"),
    )(a, b)


def _head_masks(rows):
    lane = lax.broadcasted_iota(jnp.int32, (rows, LANES), 1)
    return lane < HEAD_DIM, lane >= HEAD_DIM


def _causal(i_q, i_k, tq, tk):
    row = i_q * tq + lax.broadcasted_iota(jnp.int32, (tq, tk), 0)
    col = i_k * tk + lax.broadcasted_iota(jnp.int32, (tq, tk), 1)
    return row >= col


def _hosted(body, n_in, n_out, n_scratch, host, grid):
    if host is None:
        return body
    nx = host.n

    def wrapped(*refs):
        ins, xsrcs = refs[:n_in], refs[n_in:n_in + nx]
        outs = refs[n_in + nx:n_in + nx + n_out]
        xouts = refs[n_in + nx + n_out:n_in + 2 * nx + n_out]
        scratch = refs[n_in + 2 * nx + n_out:n_in + 2 * nx + n_out + n_scratch]
        xsems = refs[n_in + 2 * nx + n_out + n_scratch:]
        step = pl.program_id(0) * grid[1] + pl.program_id(1)

        @pl.when(step == 0)
        def _():
            host.start(xsrcs, xouts, *xsems)

        body(*ins, *outs, *scratch)

        @pl.when(step == grid[0] * grid[1] - 1)
        def _():
            host.wait(xsrcs, xouts, *xsems)

    return wrapped


def _host_specs(host):
    if host is None:
        return [], [], [], [], []
    return ([HBM_SPEC] * host.n, [HBM_SPEC] * host.n, host.out_shapes, host.sem_shapes, host.arrays)


def _flash_fwd(proj, cum4, *, tb, name, host=None):
    T = proj.shape[0]
    D = N_HEADS * HEAD_DIM
    nb = T // tb
    cb = D // LANES
    x_in, x_out, x_shapes, x_scratch, x_args = _host_specs(host)

    def body(q_ref, k_ref, v_ref, g_ref, cum_ref, o_ref, og_ref, lp_ref, kb_ref, vb_ref):
        i = pl.program_id(1)

        @pl.when(i == 0)
        def _():
            kb_ref[...] = k_ref[...].astype(BF16)
            vb_ref[...] = v_ref[...].astype(BF16)

        q = q_ref[...] * (HEAD_DIM ** -0.5)
        masks = _head_masks(tb)
        qh = [jnp.where(masks[h], q, 0.0).astype(BF16) for h in range(2)]
        cref = [cum_ref[0, h, pl.ds(i, 1), :][:, 0:1] for h in range(2)]

        def step(kbi, carry, masked):
            k0 = pl.multiple_of(kbi * tb, tb)
            kblk = kb_ref[pl.ds(k0, tb), :]
            vblk = vb_ref[pl.ds(k0, tb), :]
            new = []
            for h in range(2):
                m, l, acc = carry[h]
                s = lax.dot_general(qh[h], kblk, (((1,), (1,)), ((), ())),
                                    preferred_element_type=F32)
                s = s + (cref[h] - cum_ref[0, h, pl.ds(kbi, 1), :])
                if masked:
                    s = jnp.where(_causal(i, kbi, tb, tb), s, -jnp.inf)
                m_new = jnp.maximum(m, jnp.max(s, axis=-1, keepdims=True))
                alpha = jnp.exp(m - m_new)
                p = jnp.exp(s - m_new)
                l = alpha * l + jnp.sum(p, axis=-1, keepdims=True)
                acc = alpha * acc + jnp.dot(p.astype(BF16), vblk, preferred_element_type=F32)
                new.append((m_new, l, acc))
            return tuple(new)

        init1 = (jnp.full((tb, 1), -jnp.inf, F32), jnp.zeros((tb, 1), F32),
                 jnp.zeros((tb, LANES), F32))
        carry = lax.fori_loop(0, i, lambda kbi, c: step(kbi, c, False), (init1, init1))
        outs = []
        for h, (m, l, acc) in enumerate(step(i, carry, True)):
            outs.append(acc / l)
            lp_ref[h] = jnp.broadcast_to(m + jnp.log(l) - cref[h], (tb, LANES))
        o = jnp.where(masks[0], outs[0], outs[1])
        o_ref[...] = o
        gate = g_ref[...]
        og_ref[...] = o * (gate * _sigmoid(gate))

    body = _hosted(body, 5, 3, 2, host, (N_PAIRS, nb))
    return pl.pallas_call(
        body, name=name, grid=(N_PAIRS, nb),
        in_specs=[pl.BlockSpec((tb, LANES), lambda j, i: (i, j)),
                  pl.BlockSpec((T, LANES), lambda j, i: (0, cb + j)),
                  pl.BlockSpec((T, LANES), lambda j, i: (0, 2 * cb + j)),
                  pl.BlockSpec((tb, LANES), lambda j, i: (i, 3 * cb + j)),
                  pl.BlockSpec((1, 2, nb, tb), lambda j, i: (j, 0, 0, 0))] + x_in,
        out_specs=[pl.BlockSpec((tb, LANES), lambda j, i: (i, j)),
                   pl.BlockSpec((tb, LANES), lambda j, i: (i, j)),
                   pl.BlockSpec((2, tb, LANES), lambda j, i: (j, i, 0))] + x_out,
        out_shape=[jax.ShapeDtypeStruct((T, D), F32), jax.ShapeDtypeStruct((T, D), F32),
                   jax.ShapeDtypeStruct((N_HEADS, T, LANES), F32)] + x_shapes,
        scratch_shapes=[pltpu.VMEM((T, LANES), BF16), pltpu.VMEM((T, LANES), BF16)] + x_scratch,
        compiler_params=_cparams("arbitrary", "arbitrary"),
    )(proj, proj, proj, proj, cum4, *x_args)


def _flash_bwd_dq(proj, cum4, o, dog, lp, *, tb, name, host=None):
    T = proj.shape[0]
    D = N_HEADS * HEAD_DIM
    nb = T // tb
    cb = D // LANES
    x_in, x_out, x_shapes, x_scratch, x_args = _host_specs(host)

    def body(q_ref, k_ref, v_ref, g_ref, cum_ref, o_ref, dog_ref, lp_ref,
             dq_ref, dg_ref, do_ref, dl_ref, dc_ref, kb_ref, vb_ref):
        i = pl.program_id(1)

        @pl.when(i == 0)
        def _():
            kb_ref[...] = k_ref[...].astype(BF16)
            vb_ref[...] = v_ref[...].astype(BF16)

        gate = g_ref[...]
        sg = _sigmoid(gate)
        o = o_ref[...]
        dog = dog_ref[...]
        do = dog * (gate * sg)
        dg_ref[...] = dog * o * (sg * (1.0 + gate * (1.0 - sg)))
        do_ref[...] = do.astype(BF16)
        q = q_ref[...] * (HEAD_DIM ** -0.5)
        masks = _head_masks(tb)
        qh = [jnp.where(masks[h], q, 0.0).astype(BF16) for h in range(2)]
        doh = [jnp.where(masks[h], do, 0.0).astype(BF16) for h in range(2)]
        delta = [jnp.sum(jnp.where(masks[h], do * o, 0.0), axis=-1, keepdims=True) for h in range(2)]
        lph = [lp_ref[h][:, 0:1] for h in range(2)]
        for h in range(2):
            dl_ref[h] = jnp.broadcast_to(delta[h], (tb, LANES))

        def step(kbi, carry, masked):
            k0 = pl.multiple_of(kbi * tb, tb)
            kblk = kb_ref[pl.ds(k0, tb), :]
            vblk = vb_ref[pl.ds(k0, tb), :]
            new = []
            for h in range(2):
                acc, rs = carry[h]
                s = lax.dot_general(qh[h], kblk, (((1,), (1,)), ((), ())), preferred_element_type=F32)
                p = jnp.exp(s - cum_ref[0, h, pl.ds(kbi, 1), :] - lph[h])
                if masked:
                    p = jnp.where(_causal(i, kbi, tb, tb), p, 0.0)
                dp = lax.dot_general(doh[h], vblk, (((1,), (1,)), ((), ())),
                                     preferred_element_type=F32)
                ds = p * (dp - delta[h])
                new.append((acc + jnp.dot(ds.astype(BF16), kblk, preferred_element_type=F32),
                            rs + jnp.sum(ds, axis=-1, keepdims=True)))
            return tuple(new)

        init1 = (jnp.zeros((tb, LANES), F32), jnp.zeros((tb, 1), F32))
        carry = lax.fori_loop(0, i, lambda kbi, c: step(kbi, c, False), (init1, init1))
        dqs = []
        for h, (acc, rs) in enumerate(step(i, carry, True)):
            dqs.append(acc)
            dc_ref[0, 0, pl.ds(h, 1), :] = jnp.broadcast_to(rs, (tb, LANES)).T[0:1, :]
        dq_ref[...] = jnp.where(masks[0], dqs[0], dqs[1]) * (HEAD_DIM ** -0.5)

    blk = pl.BlockSpec((tb, LANES), lambda j, i: (i, j))
    stat = pl.BlockSpec((2, tb, LANES), lambda j, i: (j, i, 0))
    body = _hosted(body, 8, 5, 2, host, (N_PAIRS, nb))
    return pl.pallas_call(
        body, name=name, grid=(N_PAIRS, nb),
        in_specs=[blk,
                  pl.BlockSpec((T, LANES), lambda j, i: (0, cb + j)),
                  pl.BlockSpec((T, LANES), lambda j, i: (0, 2 * cb + j)),
                  pl.BlockSpec((tb, LANES), lambda j, i: (i, 3 * cb + j)),
                  pl.BlockSpec((1, 2, nb, tb), lambda j, i: (j, 0, 0, 0)),
                  blk, blk, stat] + x_in,
        out_specs=[blk, blk, blk, stat,
                   pl.BlockSpec((1, 1, 2, tb), lambda j, i: (j, i, 0, 0))] + x_out,
        out_shape=[jax.ShapeDtypeStruct((T, D), F32), jax.ShapeDtypeStruct((T, D), F32),
                   jax.ShapeDtypeStruct((T, D), BF16),
                   jax.ShapeDtypeStruct((N_HEADS, T, LANES), F32),
                   jax.ShapeDtypeStruct((N_PAIRS, nb, 2, tb), F32)] + x_shapes,
        scratch_shapes=[pltpu.VMEM((T, LANES), BF16), pltpu.VMEM((T, LANES), BF16)] + x_scratch,
        compiler_params=_cparams("arbitrary", "arbitrary"),
    )(proj, proj, proj, proj, cum4, o, dog, lp, *x_args)


def _flash_bwd_dkv(proj, cum4, do, lp, delta, *, tb, name):
    T = proj.shape[0]
    D = N_HEADS * HEAD_DIM
    nb = T // tb
    cb = D // LANES

    def body(q_ref, k_ref, v_ref, cum_ref, do_ref, lp_ref, dl_ref, dk_ref, dv_ref, dc_ref):
        kbi = pl.program_id(1)
        k = k_ref[...] * (HEAD_DIM ** -0.5)
        v = v_ref[...]
        masks = _head_masks(tb)
        kh = [jnp.where(masks[h], k, 0.0).astype(BF16) for h in range(2)]
        vh = [jnp.where(masks[h], v, 0.0).astype(BF16) for h in range(2)]
        ck = [cum_ref[0, h, pl.ds(kbi, 1), :] for h in range(2)]

        def step(i, carry, masked):
            q0 = pl.multiple_of(i * tb, tb)
            qb = q_ref[pl.ds(q0, tb), :].astype(BF16)
            dob = do_ref[pl.ds(q0, tb), :]
            new = []
            for h in range(2):
                dk, dv, dc = carry[h]
                s = lax.dot_general(qb, kh[h], (((1,), (1,)), ((), ())), preferred_element_type=F32)
                p = jnp.exp(s - ck[h] - lp_ref[h, pl.ds(q0, tb), :][:, 0:1])
                if masked:
                    p = jnp.where(_causal(i, kbi, tb, tb), p, 0.0)
                dp = lax.dot_general(dob, vh[h], (((1,), (1,)), ((), ())), preferred_element_type=F32)
                ds = p * (dp - dl_ref[h, pl.ds(q0, tb), :][:, 0:1])
                dv = dv + lax.dot_general(p.astype(BF16), dob, (((0,), (0,)), ((), ())),
                                          preferred_element_type=F32)
                dk = dk + lax.dot_general(ds.astype(BF16), qb, (((0,), (0,)), ((), ())),
                                          preferred_element_type=F32)
                new.append((dk, dv, dc - jnp.sum(ds, axis=0, keepdims=True)))
            return tuple(new)

        init1 = (jnp.zeros((tb, LANES), F32), jnp.zeros((tb, LANES), F32), jnp.zeros((1, tb), F32))
        carry = step(kbi, (init1, init1), True)
        carry = lax.fori_loop(kbi + 1, nb, lambda i, c: step(i, c, False), carry)
        dks, dvs = [], []
        for h, (dk, dv, dc) in enumerate(carry):
            dks.append(dk)
            dvs.append(dv)
            dc_ref[0, 0, pl.ds(h, 1), :] = dc
        dk_ref[...] = jnp.where(masks[0], dks[0], dks[1]) * (HEAD_DIM ** -0.5)
        dv_ref[...] = jnp.where(masks[0], dvs[0], dvs[1])

    full = pl.BlockSpec((T, LANES), lambda j, i: (0, j))
    stat = pl.BlockSpec((2, T, LANES), lambda j, i: (j, 0, 0))
    blk = pl.BlockSpec((tb, LANES), lambda j, i: (i, j))
    return pl.pallas_call(
        body, name=name, grid=(N_PAIRS, nb),
        in_specs=[full,
                  pl.BlockSpec((tb, LANES), lambda j, i: (i, cb + j)),
                  pl.BlockSpec((tb, LANES), lambda j, i: (i, 2 * cb + j)),
                  pl.BlockSpec((1, 2, nb, tb), lambda j, i: (j, 0, 0, 0)),
                  full, stat, stat],
        out_specs=[blk, blk, pl.BlockSpec((1, 1, 2, tb), lambda j, i: (j, i, 0, 0))],
        out_shape=[jax.ShapeDtypeStruct((T, D), F32), jax.ShapeDtypeStruct((T, D), F32),
                   jax.ShapeDtypeStruct((N_PAIRS, nb, 2, tb), F32)],
        compiler_params=_cparams("parallel", "arbitrary"),
    )(proj, proj, proj, cum4, do, lp, delta)


def _cumsum_fwd(proj, bf_row, *, tt, name):
    T = proj.shape[0]
    cb = (proj.shape[1] - LANES) // LANES

    def body(f_ref, b_ref, out_ref, carry_ref):
        i = pl.program_id(0)

        @pl.when(i == 0)
        def _():
            carry_ref[...] = jnp.zeros_like(carry_ref)

        ls = -_softplus(-(f_ref[...] + b_ref[...]))
        tri = (lax.broadcasted_iota(jnp.int32, (tt, tt), 0)
               >= lax.broadcasted_iota(jnp.int32, (tt, tt), 1)).astype(F32)
        cum = jnp.dot(tri, ls, preferred_element_type=F32,
                      precision=lax.Precision.HIGHEST) + carry_ref[...]
        carry_ref[...] = cum[tt - 1:tt, :]
        out_ref[...] = cum.T

    return pl.pallas_call(
        body, name=name, grid=(T // tt,),
        in_specs=[pl.BlockSpec((tt, LANES), lambda i: (i, cb)),
                  pl.BlockSpec((1, LANES), lambda i: (0, 0))],
        out_specs=pl.BlockSpec((LANES, tt), lambda i: (0, i)),
        out_shape=jax.ShapeDtypeStruct((LANES, T), F32),
        scratch_shapes=[pltpu.VMEM((1, LANES), F32)],
        compiler_params=_cparams("arbitrary"),
    )(proj, bf_row)


def _cumsum_bwd(dcum_t, proj, bf_row, *, tt, name):
    T = proj.shape[0]
    cb = (proj.shape[1] - LANES) // LANES
    nt = T // tt

    def body(dc_ref, f_ref, b_ref, df_ref, db_ref, carry_ref):
        i = pl.program_id(0)

        @pl.when(i == 0)
        def _():
            carry_ref[...] = jnp.zeros_like(carry_ref)
            db_ref[...] = jnp.zeros_like(db_ref)

        dc = dc_ref[...].T
        tri = (lax.broadcasted_iota(jnp.int32, (tt, tt), 0)
               <= lax.broadcasted_iota(jnp.int32, (tt, tt), 1)).astype(F32)
        rev = jnp.dot(tri, dc, preferred_element_type=F32,
                      precision=lax.Precision.HIGHEST) + carry_ref[...]
        carry_ref[...] = rev[0:1, :]
        df = rev * _sigmoid(-(f_ref[...] + b_ref[...]))
        df_ref[...] = df
        db_ref[...] += jnp.sum(df, axis=0, keepdims=True)

    return pl.pallas_call(
        body, name=name, grid=(nt,),
        in_specs=[pl.BlockSpec((LANES, tt), lambda i: (0, nt - 1 - i)),
                  pl.BlockSpec((tt, LANES), lambda i: (nt - 1 - i, cb)),
                  pl.BlockSpec((1, LANES), lambda i: (0, 0))],
        out_specs=[pl.BlockSpec((tt, LANES), lambda i: (nt - 1 - i, 0)),
                   pl.BlockSpec((1, LANES), lambda i: (0, 0))],
        out_shape=[jax.ShapeDtypeStruct((T, LANES), F32), jax.ShapeDtypeStruct((1, LANES), F32)],
        scratch_shapes=[pltpu.VMEM((1, LANES), F32)],
        compiler_params=_cparams("arbitrary"),
    )(dcum_t, proj, bf_row)


def _rg_gates(upad_ref, small_ref, wa_ref, wi_ref, tt):
    off = SUBLANES - (CONV_WIDTH - 1)
    u = small_ref[4:5, :]
    for tap in range(CONV_WIDTH):
        u = u + upad_ref[off + tap:off + tap + tt, :] * small_ref[tap:tap + 1, :]
    pa, pi = [], []
    for n in range(RNN_BLOCKS):
        ub = u[:, n * RNN_BLOCK_WIDTH:(n + 1) * RNN_BLOCK_WIDTH].astype(BF16)
        pa.append(jnp.dot(ub, wa_ref[n], preferred_element_type=F32))
        pi.append(jnp.dot(ub, wi_ref[n], preferred_element_type=F32))
    r = _sigmoid(jnp.concatenate(pa, axis=-1) + small_ref[5:6, :])
    ig = _sigmoid(jnp.concatenate(pi, axis=-1) + small_ref[6:7, :])
    spl = _softplus(-small_ref[7:8, :])
    log_a = (-LRU_C) * r * spl
    a = jnp.exp(log_a)
    s = jnp.sqrt(jnp.tanh(-log_a) * (a * a + 1.0))
    return u, r, ig, spl, a, s


def _rg_fwd(proj, small, wa, wi, *, tt, name):
    T = proj.shape[0]
    D = RNN_BLOCKS * RNN_BLOCK_WIDTH
    hb = tt // SUBLANES

    def body(u0_ref, halo_ref, g_ref, small_ref, wa_ref, wi_ref, h_ref, y_ref,
             upad_ref, a_ref, b_ref, carry_ref):
        i = pl.program_id(0)

        @pl.when(i == 0)
        def _():
            carry_ref[...] = jnp.zeros_like(carry_ref)

        upad_ref[0:SUBLANES, :] = jnp.where(i == 0, 0.0, halo_ref[...])
        upad_ref[SUBLANES:, :] = u0_ref[...]
        u, r, ig, spl, a, s = _rg_gates(upad_ref, small_ref, wa_ref, wi_ref, tt)
        a_ref[...] = a
        b_ref[...] = s * (ig * u)

        def row(t, h):
            h = a_ref[pl.ds(t, 1), :] * h + b_ref[pl.ds(t, 1), :]
            h_ref[pl.ds(t, 1), :] = h
            return h

        carry_ref[...] = lax.fori_loop(0, tt, row, carry_ref[...], unroll=8)
        gate = g_ref[...]
        y_ref[...] = h_ref[...] * (gate * _sigmoid(gate))

    return pl.pallas_call(
        body, name=name, grid=(T // tt,),
        in_specs=[pl.BlockSpec((tt, D), lambda i: (i, 0)),
                  pl.BlockSpec((SUBLANES, D), lambda i: (jnp.maximum(i * hb - 1, 0), 0)),
                  pl.BlockSpec((tt, D), lambda i: (i, 1)),
                  pl.BlockSpec((SUBLANES, D), lambda i: (0, 0)),
                  pl.BlockSpec((RNN_BLOCKS, RNN_BLOCK_WIDTH, RNN_BLOCK_WIDTH), lambda i: (0, 0, 0)),
                  pl.BlockSpec((RNN_BLOCKS, RNN_BLOCK_WIDTH, RNN_BLOCK_WIDTH), lambda i: (0, 0, 0))],
        out_specs=[pl.BlockSpec((tt, D), lambda i: (i, 0)), pl.BlockSpec((tt, D), lambda i: (i, 0))],
        out_shape=[jax.ShapeDtypeStruct((T, D), F32), jax.ShapeDtypeStruct((T, D), F32)],
        scratch_shapes=[pltpu.VMEM((tt + SUBLANES, D), F32), pltpu.VMEM((tt, D), F32),
                        pltpu.VMEM((tt, D), F32), pltpu.VMEM((1, D), F32)],
        compiler_params=_cparams("arbitrary"),
    )(proj, proj, proj, small, wa, wi)


def _rg_bwd(proj, hs, dy, small, wa, wi, *, tt, name):
    T = proj.shape[0]
    D = RNN_BLOCKS * RNN_BLOCK_WIDTH
    W = RNN_BLOCK_WIDTH
    hb = tt // SUBLANES
    nt = T // tt

    def body(u0_ref, uhalo_ref, g_ref, h_ref, hhalo_ref, dy_ref, small_ref, wa_ref, wi_ref,
             dp_ref, dwa_ref, dwi_ref, ds_ref,
             upad_ref, hpad_ref, a_ref, g_s_ref, duext_ref, carry_ref):
        i = pl.program_id(0)
        first_chunk = i == nt - 1

        @pl.when(i == 0)
        def _():
            carry_ref[...] = jnp.zeros_like(carry_ref)
            duext_ref[...] = jnp.zeros_like(duext_ref)
            dwa_ref[...] = jnp.zeros_like(dwa_ref)
            dwi_ref[...] = jnp.zeros_like(dwi_ref)
            ds_ref[...] = jnp.zeros_like(ds_ref)

        upad_ref[0:SUBLANES, :] = jnp.where(first_chunk, 0.0, uhalo_ref[...])
        upad_ref[SUBLANES:, :] = u0_ref[...]
        hpad_ref[0:SUBLANES, :] = jnp.where(first_chunk, 0.0, hhalo_ref[...])
        hpad_ref[SUBLANES:, :] = h_ref[...]
        u, r, ig, spl, a, s = _rg_gates(upad_ref, small_ref, wa_ref, wi_ref, tt)
        gate = g_ref[...]
        sg = _sigmoid(gate)
        dy = dy_ref[...]
        dp_ref[:, D:] = dy * h_ref[...] * (sg * (1.0 + gate * (1.0 - sg)))
        a_ref[...] = a
        g_s_ref[...] = dy * (gate * sg)

        def row(k, c):
            t = tt - 1 - k
            g = g_s_ref[pl.ds(t, 1), :] + c
            g_s_ref[pl.ds(t, 1), :] = g
            return a_ref[pl.ds(t, 1), :] * g

        carry_ref[...] = lax.fori_loop(0, tt, row, carry_ref[...], unroll=8)
        g = g_s_ref[...]
        h_prev = hpad_ref[SUBLANES - 1:SUBLANES - 1 + tt, :]
        iu = ig * u
        d_iu = g * s
        dlog_a = (g * h_prev) * a - (g * iu) * (a * a) / s
        dpre_a = (dlog_a * ((-LRU_C) * spl)) * r * (1.0 - r)
        dpre_i = (d_iu * u) * ig * (1.0 - ig)
        dlam = jnp.sum(dlog_a * r, axis=0, keepdims=True) * (LRU_C * _sigmoid(-small_ref[7:8, :]))
        du_parts = []
        for n in range(RNN_BLOCKS):
            sl = slice(n * W, (n + 1) * W)
            ub = u[:, sl].astype(BF16)
            da_n = dpre_a[:, sl].astype(BF16)
            di_n = dpre_i[:, sl].astype(BF16)
            dwa_ref[n] += lax.dot_general(ub, da_n, (((0,), (0,)), ((), ())),
                                          preferred_element_type=F32)
            dwi_ref[n] += lax.dot_general(ub, di_n, (((0,), (0,)), ((), ())),
                                          preferred_element_type=F32)
            du_parts.append(
                lax.dot_general(da_n, wa_ref[n], (((1,), (1,)), ((), ())), preferred_element_type=F32)
                + lax.dot_general(di_n, wi_ref[n], (((1,), (1,)), ((), ())), preferred_element_type=F32))
        du = d_iu * ig + jnp.concatenate(du_parts, axis=-1)
        off = SUBLANES - (CONV_WIDTH - 1)
        for tap in range(CONV_WIDTH):
            ds_ref[tap:tap + 1, :] += jnp.sum(du * upad_ref[off + tap:off + tap + tt, :],
                                              axis=0, keepdims=True)
        ds_ref[4:5, :] += jnp.sum(du, axis=0, keepdims=True)
        ds_ref[5:6, :] += jnp.sum(dpre_a, axis=0, keepdims=True)
        ds_ref[6:7, :] += jnp.sum(dpre_i, axis=0, keepdims=True)
        ds_ref[7:8, :] += dlam
        duext_ref[0:tt, :] = du
        du0 = jnp.zeros((tt, D), F32)
        for tap in range(CONV_WIDTH):
            sh = CONV_WIDTH - 1 - tap
            du0 = du0 + duext_ref[sh:sh + tt, :] * small_ref[tap:tap + 1, :]
        dp_ref[:, :D] = du0
        duext_ref[tt:, :] = du[0:SUBLANES, :]

    rev = lambda i: nt - 1 - i
    wspec = pl.BlockSpec((RNN_BLOCKS, W, W), lambda i: (0, 0, 0))
    return pl.pallas_call(
        body, name=name, grid=(nt,),
        in_specs=[pl.BlockSpec((tt, D), lambda i: (rev(i), 0)),
                  pl.BlockSpec((SUBLANES, D), lambda i: (jnp.maximum(rev(i) * hb - 1, 0), 0)),
                  pl.BlockSpec((tt, D), lambda i: (rev(i), 1)),
                  pl.BlockSpec((tt, D), lambda i: (rev(i), 0)),
                  pl.BlockSpec((SUBLANES, D), lambda i: (jnp.maximum(rev(i) * hb - 1, 0), 0)),
                  pl.BlockSpec((tt, D), lambda i: (rev(i), 0)),
                  pl.BlockSpec((SUBLANES, D), lambda i: (0, 0)),
                  wspec, wspec],
        out_specs=[pl.BlockSpec((tt, 2 * D), lambda i: (rev(i), 0)),
                   wspec, wspec, pl.BlockSpec((SUBLANES, D), lambda i: (0, 0))],
        out_shape=[jax.ShapeDtypeStruct((T, 2 * D), F32),
                   jax.ShapeDtypeStruct((RNN_BLOCKS, W, W), F32),
                   jax.ShapeDtypeStruct((RNN_BLOCKS, W, W), F32),
                   jax.ShapeDtypeStruct((SUBLANES, D), F32)],
        scratch_shapes=[pltpu.VMEM((tt + SUBLANES, D), F32), pltpu.VMEM((tt + SUBLANES, D), F32),
                        pltpu.VMEM((tt, D), F32), pltpu.VMEM((tt, D), F32),
                        pltpu.VMEM((tt + SUBLANES, D), F32), pltpu.VMEM((1, D), F32)],
        compiler_params=_cparams("arbitrary"),
    )(proj, proj, proj, hs, hs, dy, small, wa, wi)


def _ln_fwd(x, h, g, b, *, tt, name):
    T, D = x.shape

    def body(x_ref, h_ref, g_ref, b_ref, y_ref, zh_ref, rs_ref):
        z = ALPHA * x_ref[...] + h_ref[...]
        mu = jnp.mean(z, axis=-1, keepdims=True)
        zc = z - mu
        rstd = lax.rsqrt(jnp.mean(zc * zc, axis=-1, keepdims=True) + LN_EPS)
        zh = zc * rstd
        zh_ref[...] = zh
        rs_ref[...] = rstd
        y_ref[...] = zh * g_ref[...] + b_ref[...]

    blk = pl.BlockSpec((tt, D), lambda i: (i, 0))
    row = pl.BlockSpec((1, D), lambda i: (0, 0))
    return pl.pallas_call(
        body, name=name, grid=(T // tt,),
        in_specs=[blk, blk, row, row],
        out_specs=[blk, blk, pl.BlockSpec((tt, 1), lambda i: (i, 0))],
        out_shape=[jax.ShapeDtypeStruct((T, D), F32), jax.ShapeDtypeStruct((T, D), F32),
                   jax.ShapeDtypeStruct((T, 1), F32)],
        compiler_params=_cparams("parallel"),
    )(x, h, g, b)


def _ln_bwd(dy, zh, rstd, g, *, tt, name):
    T, D = dy.shape

    def body(dy_ref, zh_ref, rs_ref, g_ref, dz_ref, dg_ref, db_ref):
        @pl.when(pl.program_id(0) == 0)
        def _():
            dg_ref[...] = jnp.zeros_like(dg_ref)
            db_ref[...] = jnp.zeros_like(db_ref)

        dy = dy_ref[...]
        zh = zh_ref[...]
        dg_ref[...] += jnp.sum(dy * zh, axis=0, keepdims=True)
        db_ref[...] += jnp.sum(dy, axis=0, keepdims=True)
        dzh = dy * g_ref[...]
        m1 = jnp.mean(dzh, axis=-1, keepdims=True)
        m2 = jnp.mean(dzh * zh, axis=-1, keepdims=True)
        dz_ref[...] = rs_ref[...] * (dzh - m1 - zh * m2)

    blk = pl.BlockSpec((tt, D), lambda i: (i, 0))
    row = pl.BlockSpec((1, D), lambda i: (0, 0))
    return pl.pallas_call(
        body, name=name, grid=(T // tt,),
        in_specs=[blk, blk, pl.BlockSpec((tt, 1), lambda i: (i, 0)), row],
        out_specs=[blk, row, row],
        out_shape=[jax.ShapeDtypeStruct((T, D), F32), jax.ShapeDtypeStruct((1, D), F32),
                   jax.ShapeDtypeStruct((1, D), F32)],
        compiler_params=_cparams("arbitrary"),
    )(dy, zh, rstd, g)


def _loss(y, tgt, *, tt, name):
    T, D = y.shape

    def body(y_ref, t_ref, l_ref, dy_ref):
        @pl.when(pl.program_id(0) == 0)
        def _():
            l_ref[...] = jnp.zeros_like(l_ref)

        e = y_ref[...] - t_ref[...]
        dy_ref[...] = e * (1.0 / D)
        l_ref[...] += jnp.sum(e * e, axis=0, keepdims=True) * (0.5 / D)

    blk = pl.BlockSpec((tt, D), lambda i: (i, 0))
    return pl.pallas_call(
        body, name=name, grid=(T // tt,),
        in_specs=[blk, blk], out_specs=[pl.BlockSpec((1, D), lambda i: (0, 0)), blk],
        out_shape=[jax.ShapeDtypeStruct((1, D), F32), jax.ShapeDtypeStruct((T, D), F32)],
        compiler_params=_cparams("arbitrary"),
    )(y, tgt)


def _row_tile(rows, target):
    best = SUBLANES
    for t in range(SUBLANES, target + 1, SUBLANES):
        if rows % t == 0:
            best = t
    return best


def _add_own(g, recv, c_idx, *, tr, name):
    _, M, R, C = g.shape

    def body(c_ref, g_ref, r_ref, o_ref):
        o_ref[...] = g_ref[0] + r_ref[...]

    return pl.pallas_call(
        body, name=name,
        grid_spec=pltpu.PrefetchScalarGridSpec(
            num_scalar_prefetch=1, grid=(M, R // tr),
            in_specs=[pl.BlockSpec((1, 1, tr, C), lambda k, i, c: (c[0], k, i, 0)),
                      pl.BlockSpec((1, tr, C), lambda k, i, c: (k, i, 0))],
            out_specs=pl.BlockSpec((1, tr, C), lambda k, i, c: (k, i, 0))),
        out_shape=jax.ShapeDtypeStruct((M, R, C), F32),
        compiler_params=_cparams("parallel", "parallel"),
    )(c_idx, g, recv)


def _adamw_math(g, w_ref, m_ref, v_ref, g_ref, d_ref, nm_ref, nv_ref):
    nm = ADAM_B1 * m_ref[...] + (1.0 - ADAM_B1) * g
    nv = ADAM_B2 * v_ref[...] + (1.0 - ADAM_B2) * (g * g)
    m_hat = nm / (1.0 - ADAM_B1 ** ADAM_STEP)
    v_hat = nv / (1.0 - ADAM_B2 ** ADAM_STEP)
    g_ref[...] = g
    nm_ref[...] = nm
    nv_ref[...] = nv
    d_ref[...] = (-ADAM_LR) * (m_hat / (jnp.sqrt(v_hat) + ADAM_EPS) + ADAM_WD * w_ref[...])


def _adamw(parts, w, m, v, *, tr, name):
    n, R, C = parts.shape
    tr = min(tr, R)

    def body(p_ref, w_ref, m_ref, v_ref, *out_refs):
        g = p_ref[0]
        for k in range(1, n):
            g = g + p_ref[k]
        _adamw_math(g, w_ref, m_ref, v_ref, *out_refs)

    blk = pl.BlockSpec((tr, C), lambda i: (i, 0))
    out = jax.ShapeDtypeStruct((R, C), F32)
    return pl.pallas_call(
        body, name=name, grid=(R // tr,),
        in_specs=[pl.BlockSpec((n, tr, C), lambda i: (0, i, 0)), blk, blk, blk],
        out_specs=[blk, blk, blk, blk], out_shape=[out, out, out, out],
        compiler_params=_cparams("parallel"),
    )(parts, w, m, v)


def _adamw_shard(h, recv, me_idx, w, m, v, *, idx, prev, tr, name):
    _, _, R, C = h.shape
    n_prev = 0 if prev is None else 4

    def body(me_ref, h_ref, r1_ref, r2_ref, r3_ref, w_ref, m_ref, v_ref, *rest):
        g = ((h_ref[0] + r1_ref[0]) + r2_ref[0]) + r3_ref[0]
        _adamw_math(g, w_ref, m_ref, v_ref, *rest[n_prev:])

    blk = pl.BlockSpec((1, tr, C), lambda i, me: (idx, i, 0))

    def slot(d):
        return pl.BlockSpec((1, 1, tr, C), lambda i, me: (me[0] ^ d, 0, i, 0))

    out = jax.ShapeDtypeStruct(w.shape, F32)
    return pl.pallas_call(
        body, name=name,
        grid_spec=pltpu.PrefetchScalarGridSpec(
            num_scalar_prefetch=1, grid=(R // tr,),
            in_specs=[slot(0), slot(1), slot(2), slot(3), blk, blk, blk]
            + [pl.BlockSpec(memory_space=pl.ANY)] * n_prev,
            out_specs=[blk, blk, blk, blk]),
        out_shape=[out, out, out, out],
        input_output_aliases={8 + j: j for j in range(n_prev)},
        compiler_params=_cparams("parallel"),
    )(me_idx, h, recv, recv, recv, w, m, v, *(prev or ()))


SHARD_AXIS = dict(attn_w_in=1, attn_w_out=0, rnn_w_in=1, rnn_w_out=0, rnn_w_a=1, rnn_w_i=1,
                  rnn_conv_w=1, rnn_conv_b=0, rnn_b_a=0, rnn_b_i=0, rnn_lambda=0)
RNN_ROWED = ("rnn_w_out", "rnn_w_a", "rnn_w_i")
SMALL = ("rnn_conv_w", "rnn_conv_b", "rnn_b_a", "rnn_b_i", "rnn_lambda")
PACK_C = 1024


def _elems(shape):
    n = 1
    for s in shape:
        n *= s
    return n


def _pack_rows(p, idx, dtype):
    parts = [p[k][idx].astype(dtype).reshape(-1, PACK_C) for k in RNN_ROWED]
    small = jnp.concatenate([p[k][idx].reshape(-1) for k in SMALL])
    tile_rows = SUBLANES * (4 // jnp.dtype(dtype).itemsize)
    if dtype == BF16:
        small = lax.bitcast_convert_type(small, BF16)
    small = small.reshape(-1, PACK_C)
    parts.append(jnp.pad(small, ((0, tile_rows - small.shape[0]), (0, 0))))
    return jnp.concatenate(parts, axis=0)


def _unpack_rows(flat, shapes):
    out, r = {}, 0
    for k in RNN_ROWED:
        n = _elems(shapes[k]) // PACK_C
        out[k] = flat[r:r + n].reshape(shapes[k])
        r += n
    n_small = sum(_elems(shapes[k]) for k in SMALL)
    small = flat[r:r + n_small // PACK_C].reshape(-1)
    o = 0
    for k in SMALL:
        n = _elems(shapes[k])
        out[k] = small[o:o + n].reshape(shapes[k])
        o += n
    return out


def _to_full(g, k, sh):
    ax, nd = SHARD_AXIS[k], len(sh)
    perm = tuple(range(2, 2 + ax)) + (1, 0) + tuple(range(2 + ax, 2 + nd))
    return g.transpose(perm).reshape(sh[:ax] + (8 * sh[ax],) + sh[ax + 1:])


def _from_full(full, k, sh):
    ax, nd = SHARD_AXIS[k], len(sh)
    t = full.reshape(sh[:ax] + (4, 2, sh[ax]) + sh[ax + 1:])
    return t.transpose((ax + 1, ax) + tuple(range(ax)) + tuple(range(ax + 2, nd + 2)))


def _unpack_gathered_rows(g, shapes):
    out, r = {}, 0
    for k in RNN_ROWED:
        n = _elems(shapes[k]) // PACK_C
        out[k] = _to_full(g[:, :, r:r + n].reshape((2, 4) + shapes[k]), k, shapes[k])
        r += n
    n_small = sum(_elems(shapes[k]) for k in SMALL)
    nr = 2 * n_small // PACK_C
    small = lax.bitcast_convert_type(g[:, :, r:r + nr].reshape(2, 4, n_small, 2), F32)
    o = 0
    for k in SMALL:
        n = _elems(shapes[k])
        out[k] = _to_full(small[:, :, o:o + n].reshape((2, 4) + shapes[k]), k, shapes[k])
        o += n
    return out


def _pack_grad_rows(full, shapes):
    parts = [_from_full(full[k], k, shapes[k]).reshape(2, 4, -1, PACK_C) for k in RNN_ROWED]
    small = jnp.concatenate(
        [_from_full(full[k], k, shapes[k]).reshape(2, 4, -1) for k in SMALL], axis=-1)
    small = small.reshape(2, 4, -1, PACK_C)
    parts.append(jnp.pad(small, ((0, 0), (0, 0), (0, SUBLANES - small.shape[2]), (0, 0))))
    return jnp.concatenate(parts, axis=2)


def kernel(x, ln_g, ln_b, attn_w_in, attn_b_f, attn_w_out, rnn_w_in, rnn_conv_w, rnn_conv_b, rnn_w_a, rnn_b_a, rnn_w_i, rnn_b_i, rnn_lambda, rnn_w_out, loss_target, m_ln_g, m_ln_b, m_attn_w_in, m_attn_b_f, m_attn_w_out, m_rnn_w_in, m_rnn_conv_w, m_rnn_conv_b, m_rnn_w_a, m_rnn_b_a, m_rnn_w_i, m_rnn_b_i, m_rnn_lambda, m_rnn_w_out, v_ln_g, v_ln_b, v_attn_w_in, v_attn_b_f, v_attn_w_out, v_rnn_w_in, v_rnn_conv_w, v_rnn_conv_b, v_rnn_w_a, v_rnn_b_a, v_rnn_w_i, v_rnn_b_i, v_rnn_lambda, v_rnn_w_out):
    w_loc = dict(attn_w_in=attn_w_in, attn_w_out=attn_w_out, rnn_w_in=rnn_w_in, rnn_w_a=rnn_w_a,
                 rnn_w_i=rnn_w_i, rnn_w_out=rnn_w_out, rnn_conv_w=rnn_conv_w, rnn_conv_b=rnn_conv_b,
                 rnn_b_a=rnn_b_a, rnn_b_i=rnn_b_i, rnn_lambda=rnn_lambda)
    m_loc = dict(attn_w_in=m_attn_w_in, attn_w_out=m_attn_w_out, rnn_w_in=m_rnn_w_in,
                 rnn_w_a=m_rnn_w_a, rnn_w_i=m_rnn_w_i, rnn_w_out=m_rnn_w_out,
                 rnn_conv_w=m_rnn_conv_w, rnn_conv_b=m_rnn_conv_b, rnn_b_a=m_rnn_b_a,
                 rnn_b_i=m_rnn_b_i, rnn_lambda=m_rnn_lambda)
    v_loc = dict(attn_w_in=v_attn_w_in, attn_w_out=v_attn_w_out, rnn_w_in=v_rnn_w_in,
                 rnn_w_a=v_rnn_w_a, rnn_w_i=v_rnn_w_i, rnn_w_out=v_rnn_w_out,
                 rnn_conv_w=v_rnn_conv_w, rnn_conv_b=v_rnn_conv_b, rnn_b_a=v_rnn_b_a,
                 rnn_b_i=v_rnn_b_i, rnn_lambda=v_rnn_lambda)
    shapes = {k: tuple(a.shape[1:]) for k, a in w_loc.items()}
    T, D = x.shape[1], x.shape[2]
    n_f = attn_b_f.shape[1]
    tb = min(1024, T)
    tt_rg = min(128, T)
    tt_ln = min(256, T)
    c_idx = lax.axis_index("c").astype(jnp.int32).reshape(1)
    me_idx = (2 * lax.axis_index("x") + lax.axis_index("y")).astype(jnp.int32).reshape(1)

    def attn_shards(idx):
        return [attn_w_in[idx].astype(BF16), attn_w_out[idx].astype(BF16)]

    def attn_weights(g_in, g_out):
        w_in = _to_full(g_in, "attn_w_in", shapes["attn_w_in"])
        return (jnp.pad(w_in, ((0, 0), (0, LANES - n_f))),
                _to_full(g_out, "attn_w_out", shapes["attn_w_out"]))

    def rnn_weights(g_in, g_rows):
        w = _unpack_gathered_rows(g_rows, shapes)
        w["rnn_w_in"] = _to_full(g_in, "rnn_w_in", shapes["rnn_w_in"])
        w["small"] = jnp.concatenate([w["rnn_conv_w"], w["rnn_conv_b"][None], w["rnn_b_a"][None],
                                      w["rnn_b_i"][None], w["rnn_lambda"][None]])
        return w

    g0 = _ag_c(_run_exchange(_Exchange("gather", attn_shards(0)), "ag_w0_xy"), "ag_w0_c")
    later = _Exchange("gather", attn_shards(1) + [
        rnn_w_in.astype(BF16), jnp.stack([_pack_rows(w_loc, i, BF16) for i in range(2)])])
    w_attn, w_rnn = [attn_weights(*g0), None], [None, None]
    bf_rows = jnp.pad(attn_b_f, ((0, 0), (0, LANES - n_f)))[:, None, :]

    xs, saved = [x[0]], []
    for layer in range(DEPTH):
        idx, xl = layer // 2, xs[-1]
        if layer % 2 == 0:
            w_in, w_out = w_attn[idx]
            proj = _matmul(xl, w_in, trans_b=False, tm=512, tn=1408, name=f"a_proj{layer}")
            cum_t = _cumsum_fwd(proj, bf_rows[idx], tt=min(512, T), name=f"a_cum{layer}")
            cum4 = cum_t[:N_HEADS].reshape(N_PAIRS, 2, T // tb, tb)
            o, og, lp, *got = _flash_fwd(proj, cum4, tb=tb, name=f"a_fwd{layer}",
                                         host=later if layer == 0 else None)
            if layer == 0:
                g1 = _ag_c(got, "ag_w1_c")
                w_attn[1] = attn_weights(g1[0], g1[1])
                w_rnn = [rnn_weights(g1[2][:, :, i], g1[3][:, :, i]) for i in range(2)]
            hbr = _matmul(og, w_out, trans_b=False, tm=512, tn=1024, name=f"a_out{layer}")
            saved.append((proj, cum4, o, og, lp))
        else:
            w = w_rnn[idx]
            proj = _matmul(xl, w["rnn_w_in"], trans_b=False, tm=512, tn=1024,
                           name=f"r_proj{layer}")
            hs, yr = _rg_fwd(proj, w["small"], w["rnn_w_a"], w["rnn_w_i"], tt=tt_rg,
                             name=f"r_fwd{layer}")
            hbr = _matmul(yr, w["rnn_w_out"], trans_b=False, tm=512, tn=1024,
                          name=f"r_out{layer}")
            saved.append((proj, hs, yr))
        y, zh, rstd = _ln_fwd(xl, hbr, ln_g[layer][None], ln_b[layer][None], tt=tt_ln,
                              name=f"ln_fwd{layer}")
        saved[-1] = saved[-1] + (zh, rstd)
        xs.append(y)

    loss_lanes, dy = _loss(xs[-1], loss_target[0], tt=tt_ln, name="loss")
    loss = lax.psum(jnp.sum(loss_lanes), ("x", "y", "c"))

    def reduce_pair(gs, layer):
        recv = _rs_c(gs, f"rs_c{layer}")
        return [_add_own(g, r, c_idx, tr=_row_tile(g.shape[2], 512), name=f"rs_add{layer}_{n}")[:, None]
                for n, (g, r) in enumerate(zip(gs, recv))]

    half, quad = [None] * DEPTH, [None] * DEPTH
    d_ln_g, d_ln_b, d_bf = [None] * DEPTH, [None] * DEPTH, [None, None]
    for layer in reversed(range(DEPTH)):
        idx, xl = layer // 2, xs[layer]
        zh, rstd = saved[layer][-2:]
        dz, dg, db = _ln_bwd(dy, zh, rstd, ln_g[layer][None], tt=tt_ln, name=f"ln_bwd{layer}")
        d_ln_g[layer], d_ln_b[layer] = dg[0], db[0]
        if layer % 2 == 0:
            w_in, w_out = w_attn[idx]
            proj, cum4, o, og, lp = saved[layer][:5]
            dog = _matmul(dz, w_out, trans_b=True, tm=512, tn=1024, name=f"a_dog{layer}")
            dwo = _matmul_tn(og, dz, tm=512, tn=1024, tk=512, name=f"a_dwo{layer}")
            riders = [l for l in range(layer + 1, DEPTH) if quad[l] is None]
            host = _Exchange("scatter", [h for l in riders for h in half[l]]) if riders else None
            dq, dgate, do, delta, dcum_q, *got = _flash_bwd_dq(proj, cum4, o, dog, lp, tb=tb,
                                                               name=f"a_dq{layer}", host=host)
            for l in riders:
                quad[l], got = got[:len(half[l])], got[len(half[l]):]
            dk, dv, dcum_k = _flash_bwd_dkv(proj, cum4, do, lp, delta, tb=tb,
                                            name=f"a_dkv{layer}")
            dcum_t = (dcum_q + dcum_k).transpose(0, 2, 1, 3).reshape(N_HEADS, T)
            dcum_t = jnp.pad(dcum_t, ((0, LANES - N_HEADS), (0, 0)))
            df, dbf = _cumsum_bwd(dcum_t, proj, bf_rows[idx], tt=min(512, T), name=f"a_dcum{layer}")
            d_bf[idx] = dbf[0, :n_f]
            dproj = jnp.concatenate([dq, dk, dv, dgate, df], axis=1)
            dwi = _matmul_tn(xl, dproj, tm=512, tn=1408, tk=512, name=f"a_dwi{layer}")
            dy = _matmul(dproj, w_in, trans_b=True, tm=256, tn=512, name=f"a_dx{layer}",
                         add=dz, add_scale=ALPHA)
            half[layer] = reduce_pair(
                [_from_full(dwi[:, :4 * D + n_f], "attn_w_in", shapes["attn_w_in"]),
                 _from_full(dwo, "attn_w_out", shapes["attn_w_out"])], layer)
        else:
            w = w_rnn[idx]
            proj, hs, yr = saved[layer][:3]
            dyr = _matmul(dz, w["rnn_w_out"], trans_b=True, tm=512, tn=1024, name=f"r_dy{layer}")
            dwo = _matmul_tn(yr, dz, tm=512, tn=1024, tk=512, name=f"r_dwo{layer}")
            dproj, dwa, dwi_, dsm = _rg_bwd(proj, hs, dyr, w["small"], w["rnn_w_a"], w["rnn_w_i"],
                                            tt=tt_rg, name=f"r_bwd{layer}")
            dwin = _matmul_tn(xl, dproj, tm=512, tn=1024, tk=512, name=f"r_dwi{layer}")
            dy = _matmul(dproj, w["rnn_w_in"], trans_b=True, tm=512, tn=512, name=f"r_dx{layer}",
                         add=dz, add_scale=ALPHA)
            full = dict(rnn_w_out=dwo, rnn_w_a=dwa, rnn_w_i=dwi_, rnn_conv_w=dsm[0:4],
                        rnn_conv_b=dsm[4], rnn_b_a=dsm[5], rnn_b_i=dsm[6], rnn_lambda=dsm[7])
            half[layer] = reduce_pair([_from_full(dwin, "rnn_w_in", shapes["rnn_w_in"]),
                                       _pack_grad_rows(full, shapes)], layer)
    grad_x = dy[None]
    quad[0] = _run_exchange(_Exchange("scatter", half[0]), "rs_xy0")

    def update(k, n):
        res = None
        for idx in (1, 0):
            layer = 2 * idx + (0 if k.startswith("attn") else 1)
            res = _adamw_shard(half[layer][n], quad[layer][n], me_idx, w_loc[k], m_loc[k], v_loc[k],
                               idx=idx, prev=res, tr=_row_tile(shapes[k][0], 256),
                               name=f"adamw_{k}{idx}")
        return res

    shard_outs = [dict() for _ in range(4)]
    for k, n in (("attn_w_in", 0), ("attn_w_out", 1), ("rnn_w_in", 0)):
        for j, a in enumerate(update(k, n)):
            shard_outs[j][k] = a
    rows = []
    for idx in range(2):
        layer = 2 * idx + 1
        wmv = [_pack_rows(d, idx, F32)[None] for d in (w_loc, m_loc, v_loc)]
        res = _adamw_shard(half[layer][1], quad[layer][1], me_idx, *wmv, idx=0, prev=None,
                           tr=_row_tile(wmv[0].shape[1], 256), name=f"adamw_rows{idx}")
        rows.append([_unpack_rows(a[0], shapes) for a in res])
    for j in range(4):
        for k in RNN_ROWED + SMALL:
            shard_outs[j][k] = jnp.stack([rows[0][j][k], rows[1][j][k]])
    g_sh, d_sh, nm_sh, nv_sh = shard_outs

    def rep_pack(lg, lb, bf):
        rows = jnp.concatenate([lg, lb, jnp.pad(bf.reshape(1, -1), ((0, 0), (0, D - 2 * n_f)))])
        return jnp.pad(rows, ((0, 16 - rows.shape[0]), (0, 0)))

    rep = _all_gather(rep_pack(jnp.stack(d_ln_g), jnp.stack(d_ln_b), jnp.stack(d_bf)), "ag_rep")
    rg, rd, rm, rv = _adamw(rep.reshape(8, 16, D), rep_pack(ln_g, ln_b, attn_b_f),
                            rep_pack(m_ln_g, m_ln_b, m_attn_b_f),
                            rep_pack(v_ln_g, v_ln_b, v_attn_b_f), tr=16, name="adamw_rep")

    def rep_unpack(a):
        return dict(ln_g=a[0:DEPTH], ln_b=a[DEPTH:2 * DEPTH],
                    attn_b_f=a[2 * DEPTH, :2 * n_f].reshape(2, n_f))

    order = ("ln_g", "ln_b", "attn_w_in", "attn_b_f", "attn_w_out", "rnn_w_in", "rnn_conv_w",
             "rnn_conv_b", "rnn_w_a", "rnn_b_a", "rnn_w_i", "rnn_b_i", "rnn_lambda", "rnn_w_out")
    outs = [loss, grad_x]
    for sh, rp in ((g_sh, rg), (d_sh, rd), (nm_sh, rm), (nv_sh, rv)):
        allp = {**sh, **rep_unpack(rp)}
        outs.extend(allp[k] for k in order)
    return tuple(outs)
```

```python
import functools

import jax
import jax.numpy as jnp
from jax import lax
from jax.experimental import pallas as pl
from jax.experimental.pallas import tpu as pltpu

F32 = jnp.float32
BF16 = jnp.bfloat16

DEPTH = 4
N_HEADS = 16
HEAD_DIM = 64
N_PAIRS = N_HEADS // 2
RNN_BLOCKS = 4
RNN_BLOCK_WIDTH = 256
CONV_WIDTH = 4
LRU_C = 8.0
ALPHA = (2.0 * DEPTH) ** 0.25
LN_EPS = 1e-5
ADAM_LR, ADAM_B1, ADAM_B2, ADAM_EPS, ADAM_WD, ADAM_STEP = 0.001, 0.9, 0.999, 1e-8, 0.01, 10

LANES = 128
SUBLANES = 8
VMEM_LIMIT = 48 * 1024 * 1024

MESH = pl.DeviceIdType.MESH
HBM_SPEC = pl.BlockSpec(memory_space=pltpu.HBM)


def _cparams(*sem):
    return pltpu.CompilerParams(dimension_semantics=sem, vmem_limit_bytes=VMEM_LIMIT)


def _sigmoid(x):
    return 1.0 / (1.0 + jnp.exp(-x))


def _softplus(x):
    return jnp.maximum(x, 0.0) + jnp.log(1.0 + jnp.exp(-jnp.abs(x)))


def _a2a(src, *, group, bcast, name):
    n = 2 if group == "c" else 4
    blk = tuple(src.shape) if bcast else tuple(src.shape[1:])

    def body(src_ref, out_ref, send_sems, recv_sems, local_sem):
        x, y, c = lax.axis_index("x"), lax.axis_index("y"), lax.axis_index("c")
        if group == "c":
            me = c

            def peer(d):
                return (x, y, 1 - c), 1 - c
        else:
            me = 2 * x + y

            def peer(d):
                px, py = x ^ (d >> 1), y ^ (d & 1)
                return (px, py, c), 2 * px + py

        def block_for(k):
            return src_ref if bcast else src_ref.at[k]

        local = pltpu.make_async_copy(block_for(me), out_ref.at[me], local_sem)
        local.start()
        sends = []
        for d in range(1, n):
            dev, idx = peer(d)
            cp = pltpu.make_async_remote_copy(
                src_ref=block_for(idx), dst_ref=out_ref.at[me],
                send_sem=send_sems.at[d], recv_sem=recv_sems.at[d],
                device_id=dev, device_id_type=MESH)
            cp.start()
            sends.append(cp)
        for d in range(1, n):
            dev, idx = peer(d)
            pltpu.make_async_remote_copy(
                src_ref=block_for(idx), dst_ref=out_ref.at[idx],
                send_sem=send_sems.at[d], recv_sem=recv_sems.at[d],
                device_id=dev, device_id_type=MESH).wait_recv()
        for cp in sends:
            cp.wait_send()
        local.wait()

    return pl.pallas_call(
        body, name=name,
        out_shape=jax.ShapeDtypeStruct((n,) + blk, src.dtype),
        in_specs=[HBM_SPEC], out_specs=HBM_SPEC,
        scratch_shapes=[pltpu.SemaphoreType.DMA((n,)), pltpu.SemaphoreType.DMA((n,)),
                        pltpu.SemaphoreType.DMA],
    )(src)


def _all_gather(piece, name):
    return _a2a(_a2a(piece, group="xy", bcast=True, name=name + "_xy"),
                group="c", bcast=True, name=name + "_c")


D2D_CHUNKS = 16
ICI_CHUNKS = 8


def _row_chunks(rows, dtype, k):
    unit = SUBLANES * (4 // jnp.dtype(dtype).itemsize)
    assert rows % unit == 0
    units = rows // unit
    k = max(1, min(k, units))
    base, rem = divmod(units, k)
    out, r = [], 0
    for i in range(k):
        n = (base + (1 if i < rem else 0)) * unit
        out.append((r, n))
        r += n
    return out


def _chunks(shape, dtype, k):
    if len(shape) == 2:
        return [(pl.ds(r0, n),) for r0, n in _row_chunks(shape[0], dtype, k)]
    per = max(1, k // shape[0])
    return [(l, pl.ds(r0, n)) for l in range(shape[0]) for r0, n in _row_chunks(shape[1], dtype, per)]


def _mesh_place():
    x, y, c = lax.axis_index("x"), lax.axis_index("y"), lax.axis_index("c")
    return x, y, c, 2 * x + y


def _chip_peer(x, y, c, d):
    px, py = x ^ (d >> 1), y ^ (d & 1)
    return (px, py, c), 2 * px + py


def _remote(src, dst, send_sem, recv_sem, dev):
    return pltpu.make_async_remote_copy(src_ref=src, dst_ref=dst, send_sem=send_sem,
                                        recv_sem=recv_sem, device_id=dev, device_id_type=MESH)


def _comm_call(body, name, ins, out_shapes, n_sems, aliases=None):
    n = len(ins)
    return pl.pallas_call(
        body, name=name,
        out_shape=out_shapes, in_specs=[HBM_SPEC] * n, out_specs=[HBM_SPEC] * n,
        input_output_aliases=aliases or {},
        scratch_shapes=[pltpu.SemaphoreType.DMA((n_sems, n)), pltpu.SemaphoreType.DMA((n_sems, n))],
    )(*ins)


class _Exchange:
    def __init__(self, kind, arrays):
        self.kind, self.arrays, self.n = kind, list(arrays), len(arrays)
        if kind == "gather":
            self.chunks = [_chunks(a.shape, a.dtype, ICI_CHUNKS) for a in arrays]
            self.out_shapes = [jax.ShapeDtypeStruct((2, 4) + tuple(a.shape), a.dtype) for a in arrays]
        else:
            self.chunks = [_chunks(a.shape[1:], a.dtype, ICI_CHUNKS) for a in arrays]
            self.out_shapes = [jax.ShapeDtypeStruct(a.shape, a.dtype) for a in arrays]
        self.sem_shapes = [pltpu.SemaphoreType.DMA((4, self.n)), pltpu.SemaphoreType.DMA((4, self.n))]

    def _blocks(self, srcs, outs, o, c, me, pidx):
        if self.kind == "gather":
            return srcs[o], outs[o].at[c, me], outs[o].at[c, pidx]
        return srcs[o].at[pidx], outs[o].at[me], outs[o].at[pidx]

    def start(self, srcs, outs, send_sems, recv_sems):
        x, y, c, me = _mesh_place()
        if self.kind == "gather":
            for o in range(self.n):
                for idx in self.chunks[o]:
                    pltpu.make_async_copy(srcs[o].at[idx], outs[o].at[(c, me) + idx],
                                          send_sems.at[0, o]).start()
        for d in range(1, 4):
            dev, pidx = _chip_peer(x, y, c, d)
            for o in range(self.n):
                src, dst, _ = self._blocks(srcs, outs, o, c, me, pidx)
                for idx in self.chunks[o]:
                    _remote(src.at[idx], dst.at[idx], send_sems.at[d, o], recv_sems.at[d, o],
                            dev).start()

    def wait(self, srcs, outs, send_sems, recv_sems):
        x, y, c, me = _mesh_place()
        for d in range(1, 4):
            dev, pidx = _chip_peer(x, y, c, d)
            for o in range(self.n):
                src, _, land = self._blocks(srcs, outs, o, c, me, pidx)
                _remote(src, land, send_sems.at[d, o], recv_sems.at[d, o], dev).wait_recv()
        for d in range(1, 4):
            dev, pidx = _chip_peer(x, y, c, d)
            for o in range(self.n):
                src, _, land = self._blocks(srcs, outs, o, c, me, pidx)
                _remote(src, land, send_sems.at[d, o], recv_sems.at[d, o], dev).wait_send()
        if self.kind == "gather":
            for o in range(self.n):
                pltpu.make_async_copy(srcs[o], outs[o].at[c, me], send_sems.at[0, o]).wait()


def _run_exchange(ex, name):
    n = ex.n

    def body(*refs):
        srcs, outs, send_sems, recv_sems = refs[:n], refs[n:2 * n], refs[2 * n], refs[2 * n + 1]
        ex.start(srcs, outs, send_sems, recv_sems)
        ex.wait(srcs, outs, send_sems, recv_sems)

    return _comm_call(body, name, ex.arrays, ex.out_shapes, 4)


def _ag_c(bufs, name):
    n = len(bufs)
    chunks = [_chunks(b.shape[2:], b.dtype, D2D_CHUNKS // 4) for b in bufs]

    def body(*refs):
        srcs, outs, send_sems, recv_sems = refs[:n], refs[n:2 * n], refs[2 * n], refs[2 * n + 1]
        x, y, c, _ = _mesh_place()
        sib = (x, y, 1 - c)
        for o in range(n):
            for k in range(4):
                for idx in chunks[o]:
                    _remote(srcs[o].at[(c, k) + idx], outs[o].at[(c, k) + idx],
                            send_sems.at[0, o], recv_sems.at[0, o], sib).start()
        for o in range(n):
            _remote(srcs[o].at[c], outs[o].at[1 - c], send_sems.at[0, o], recv_sems.at[0, o],
                    sib).wait_recv()
        for o in range(n):
            _remote(srcs[o].at[c], outs[o].at[1 - c], send_sems.at[0, o], recv_sems.at[0, o],
                    sib).wait_send()

    shapes = [jax.ShapeDtypeStruct(b.shape, b.dtype) for b in bufs]
    return _comm_call(body, name, bufs, shapes, 1, aliases={i: i for i in range(n)})


def _rs_c(gs, name):
    n = len(gs)
    chunks = [_chunks(g.shape[2:], g.dtype, max(1, D2D_CHUNKS // g.shape[1])) for g in gs]

    def body(*refs):
        srcs, outs, send_sems, recv_sems = refs[:n], refs[n:2 * n], refs[2 * n], refs[2 * n + 1]
        x, y, c, _ = _mesh_place()
        sib = (x, y, 1 - c)
        for o in range(n):
            for k in range(gs[o].shape[1]):
                for idx in chunks[o]:
                    _remote(srcs[o].at[(1 - c, k) + idx], outs[o].at[(k,) + idx],
                            send_sems.at[0, o], recv_sems.at[0, o], sib).start()
        for o in range(n):
            _remote(srcs[o].at[1 - c], outs[o], send_sems.at[0, o], recv_sems.at[0, o],
                    sib).wait_recv()
        for o in range(n):
            _remote(srcs[o].at[1 - c], outs[o], send_sems.at[0, o], recv_sems.at[0, o],
                    sib).wait_send()

    shapes = [jax.ShapeDtypeStruct(g.shape[1:], g.dtype) for g in gs]
    return _comm_call(body, name, gs, shapes, 1)


def _matmul(a, b, *, trans_b, tm, tn, name, add=None, add_scale=1.0):
    a_parts = list(a) if isinstance(a, (list, tuple)) else [a]
    M, K = a_parts[0].shape[0], sum(p.shape[1] for p in a_parts)
    N = b.shape[0] if trans_b else b.shape[1]
    tm, tn = min(tm, M), min(tn, N)
    assert M % tm == 0 and N % tn == 0
    dn = (((1,), (1,)), ((), ())) if trans_b else (((1,), (0,)), ((), ()))
    na = len(a_parts)

    def body(*refs):
        a_refs, b_ref, o_ref = refs[:na], refs[na], refs[-1]
        av = [r[...].astype(BF16) for r in a_refs]
        av = av[0] if na == 1 else jnp.concatenate(av, axis=1)
        r = lax.dot_general(av, b_ref[...].astype(BF16), dn, preferred_element_type=F32)
        if add is not None:
            r = r + add_scale * refs[na + 1][...]
        o_ref[...] = r

    b_spec = (pl.BlockSpec((tn, K), lambda j, i: (j, 0)) if trans_b
              else pl.BlockSpec((K, tn), lambda j, i: (0, j)))
    in_specs = [pl.BlockSpec((tm, p.shape[1]), lambda j, i: (i, 0)) for p in a_parts] + [b_spec]
    args = a_parts + [b]
    if add is not None:
        in_specs.append(pl.BlockSpec((tm, tn), lambda j, i: (i, j)))
        args.append(add)
    return pl.pallas_call(
        body, name=name, grid=(N // tn, M // tm),
        in_specs=in_specs, out_specs=pl.BlockSpec((tm, tn), lambda j, i: (i, j)),
        out_shape=jax.ShapeDtypeStruct((M, N), F32),
        compiler_params=_cparams("parallel", "parallel"),
    )(*args)


def _matmul_tn(a, b, *, tm, tn, tk, name):
    T, M = a.shape
    N = b.shape[1]
    tm, tn, tk = min(tm, M), min(tn, N), min(tk, T)
    assert M % tm == 0 and N % tn == 0 and T % tk == 0

    def body(a_ref, b_ref, o_ref):
        @pl.when(pl.program_id(2) == 0)
        def _():
            o_ref[...] = jnp.zeros_like(o_ref)

        o_ref[...] += lax.dot_general(a_ref[...].astype(BF16), b_ref[...].astype(BF16),
                                      (((0,), (0,)), ((), ())), preferred_element_type=F32)

    return pl.pallas_call(
        body, name=name, grid=(M // tm, N // tn, T // tk),
        in_specs=[pl.BlockSpec((tk, tm), lambda i, j, k: (k, i)),
                  pl.BlockSpec((tk, tn), lambda i, j, k: (k, j))],
        out_specs=pl.BlockSpec((tm, tn), lambda i, j, k: (i, j)),
        out_shape=jax.ShapeDtypeStruct((M, N), F32),
        compiler_params=_cparams("parallel", "parallel", "arbitrary"),
    )(a, b)


def _matmul_tn_parts(a, parts, *, tm, tk, name):
    T, M = a.shape
    tm, tk = min(tm, M), min(tk, T)
    assert M % tm == 0 and T % tk == 0
    n = len(parts)

    def body(*refs):
        a_ref, b_refs, o_refs = refs[0], refs[1:1 + n], refs[1 + n:]
        av = a_ref[...].astype(BF16)
        for b_ref, o_ref in zip(b_refs, o_refs):
            @pl.when(pl.program_id(1) == 0)
            def _(o_ref=o_ref):
                o_ref[...] = jnp.zeros_like(o_ref)

            o_ref[...] += lax.dot_general(av, b_ref[...].astype(BF16), (((0,), (0,)), ((), ())),
                                          preferred_element_type=F32)

    return pl.pallas_call(
        body, name=name, grid=(M // tm, T // tk),
        in_specs=[pl.BlockSpec((tk, tm), lambda i, k: (k, i))]
        + [pl.BlockSpec((tk, p.shape[1]), lambda i, k: (k, 0)) for p in parts],
        out_specs=[pl.BlockSpec((tm, p.shape[1]), lambda i, k: (i, 0)) for p in parts],
        out_shape=[jax.ShapeDtypeStruct((M, p.shape[1]), F32) for p in parts],
        compiler_params=_cparams("parallel", "arbitrary"),
    )(a, *parts)


def _head_masks(rows):
    lane = lax.broadcasted_iota(jnp.int32, (rows, LANES), 1)
    return lane < HEAD_DIM, lane >= HEAD_DIM


def _causal(i_q, i_k, tq, tk):
    row = i_q * tq + lax.broadcasted_iota(jnp.int32, (tq, tk), 0)
    col = i_k * tk + lax.broadcasted_iota(jnp.int32, (tq, tk), 1)
    return row >= col


def _hosted(body, n_in, n_out, n_scratch, host, grid):
    if host is None:
        return body
    nx = host.n

    def wrapped(*refs):
        ins, xsrcs = refs[:n_in], refs[n_in:n_in + nx]
        outs = refs[n_in + nx:n_in + nx + n_out]
        xouts = refs[n_in + nx + n_out:n_in + 2 * nx + n_out]
        scratch = refs[n_in + 2 * nx + n_out:n_in + 2 * nx + n_out + n_scratch]
        xsems = refs[n_in + 2 * nx + n_out + n_scratch:]
        step = pl.program_id(0) * grid[1] + pl.program_id(1)

        @pl.when(step == 0)
        def _():
            host.start(xsrcs, xouts, *xsems)

        body(*ins, *outs, *scratch)

        @pl.when(step == grid[0] * grid[1] - 1)
        def _():
            host.wait(xsrcs, xouts, *xsems)

    return wrapped


def _host_specs(host):
    if host is None:
        return [], [], [], [], []
    return ([HBM_SPEC] * host.n, [HBM_SPEC] * host.n, host.out_shapes, host.sem_shapes, host.arrays)


def _flash_fwd(proj, cum4, *, tb, name, host=None):
    T = proj.shape[0]
    D = N_HEADS * HEAD_DIM
    nb = T // tb
    cb = D // LANES
    x_in, x_out, x_shapes, x_scratch, x_args = _host_specs(host)

    def body(q_ref, k_ref, v_ref, g_ref, cum_ref, o_ref, og_ref, lp_ref, kb_ref, vb_ref):
        i = pl.program_id(1)

        @pl.when(i == 0)
        def _():
            kb_ref[...] = k_ref[...].astype(BF16)
            vb_ref[...] = v_ref[...].astype(BF16)

        q = q_ref[...] * (HEAD_DIM ** -0.5)
        masks = _head_masks(tb)
        qh = [jnp.where(masks[h], q, 0.0).astype(BF16) for h in range(2)]
        cref = [cum_ref[0, h, pl.ds(i, 1), :][:, 0:1] for h in range(2)]

        def step(kbi, carry, masked):
            k0 = pl.multiple_of(kbi * tb, tb)
            kblk = kb_ref[pl.ds(k0, tb), :]
            vblk = vb_ref[pl.ds(k0, tb), :]
            new = []
            for h in range(2):
                m, l, acc = carry[h]
                s = lax.dot_general(qh[h], kblk, (((1,), (1,)), ((), ())),
                                    preferred_element_type=F32)
                s = s + (cref[h] - cum_ref[0, h, pl.ds(kbi, 1), :])
                if masked:
                    s = jnp.where(_causal(i, kbi, tb, tb), s, -jnp.inf)
                m_new = jnp.maximum(m, jnp.max(s, axis=-1, keepdims=True))
                alpha = jnp.exp(m - m_new)
                p = jnp.exp(s - m_new)
                l = alpha * l + jnp.sum(p, axis=-1, keepdims=True)
                acc = alpha * acc + jnp.dot(p.astype(BF16), vblk, preferred_element_type=F32)
                new.append((m_new, l, acc))
            return tuple(new)

        init1 = (jnp.full((tb, 1), -jnp.inf, F32), jnp.zeros((tb, 1), F32),
                 jnp.zeros((tb, LANES), F32))
        carry = lax.fori_loop(0, i, lambda kbi, c: step(kbi, c, False), (init1, init1))
        outs = []
        for h, (m, l, acc) in enumerate(step(i, carry, True)):
            outs.append(acc / l)
            lp_ref[h] = jnp.broadcast_to(m + jnp.log(l) - cref[h], (tb, LANES))
        o = jnp.where(masks[0], outs[0], outs[1])
        o_ref[...] = o
        gate = g_ref[...]
        og_ref[...] = o * (gate * _sigmoid(gate))

    body = _hosted(body, 5, 3, 2, host, (N_PAIRS, nb))
    return pl.pallas_call(
        body, name=name, grid=(N_PAIRS, nb),
        in_specs=[pl.BlockSpec((tb, LANES), lambda j, i: (i, j)),
                  pl.BlockSpec((T, LANES), lambda j, i: (0, cb + j)),
                  pl.BlockSpec((T, LANES), lambda j, i: (0, 2 * cb + j)),
                  pl.BlockSpec((tb, LANES), lambda j, i: (i, 3 * cb + j)),
                  pl.BlockSpec((1, 2, nb, tb), lambda j, i: (j, 0, 0, 0))] + x_in,
        out_specs=[pl.BlockSpec((tb, LANES), lambda j, i: (i, j)),
                   pl.BlockSpec((tb, LANES), lambda j, i: (i, j)),
                   pl.BlockSpec((2, tb, LANES), lambda j, i: (j, i, 0))] + x_out,
        out_shape=[jax.ShapeDtypeStruct((T, D), F32), jax.ShapeDtypeStruct((T, D), F32),
                   jax.ShapeDtypeStruct((N_HEADS, T, LANES), F32)] + x_shapes,
        scratch_shapes=[pltpu.VMEM((T, LANES), BF16), pltpu.VMEM((T, LANES), BF16)] + x_scratch,
        compiler_params=_cparams("arbitrary", "arbitrary"),
    )(proj, proj, proj, proj, cum4, *x_args)


def _flash_bwd_dq(proj, cum4, o, dog, lp, *, tb, name, host=None):
    T = proj.shape[0]
    D = N_HEADS * HEAD_DIM
    nb = T // tb
    cb = D // LANES
    x_in, x_out, x_shapes, x_scratch, x_args = _host_specs(host)

    def body(q_ref, k_ref, v_ref, g_ref, cum_ref, o_ref, dog_ref, lp_ref,
             dq_ref, dg_ref, do_ref, dl_ref, dc_ref, kb_ref, vb_ref):
        i = pl.program_id(1)

        @pl.when(i == 0)
        def _():
            kb_ref[...] = k_ref[...].astype(BF16)
            vb_ref[...] = v_ref[...].astype(BF16)

        gate = g_ref[...]
        sg = _sigmoid(gate)
        o = o_ref[...]
        dog = dog_ref[...]
        do = dog * (gate * sg)
        dg_ref[...] = dog * o * (sg * (1.0 + gate * (1.0 - sg)))
        do_ref[...] = do.astype(BF16)
        q = q_ref[...] * (HEAD_DIM ** -0.5)
        masks = _head_masks(tb)
        qh = [jnp.where(masks[h], q, 0.0).astype(BF16) for h in range(2)]
        doh = [jnp.where(masks[h], do, 0.0).astype(BF16) for h in range(2)]
        delta = [jnp.sum(jnp.where(masks[h], do * o, 0.0), axis=-1, keepdims=True) for h in range(2)]
        lph = [lp_ref[h][:, 0:1] for h in range(2)]
        for h in range(2):
            dl_ref[h] = jnp.broadcast_to(delta[h], (tb, LANES))

        def step(kbi, carry, masked):
            k0 = pl.multiple_of(kbi * tb, tb)
            kblk = kb_ref[pl.ds(k0, tb), :]
            vblk = vb_ref[pl.ds(k0, tb), :]
            new = []
            for h in range(2):
                acc, rs = carry[h]
                s = lax.dot_general(qh[h], kblk, (((1,), (1,)), ((), ())), preferred_element_type=F32)
                p = jnp.exp(s - cum_ref[0, h, pl.ds(kbi, 1), :] - lph[h])
                if masked:
                    p = jnp.where(_causal(i, kbi, tb, tb), p, 0.0)
                dp = lax.dot_general(doh[h], vblk, (((1,), (1,)), ((), ())),
                                     preferred_element_type=F32)
                ds = p * (dp - delta[h])
                new.append((acc + jnp.dot(ds.astype(BF16), kblk, preferred_element_type=F32),
                            rs + jnp.sum(ds, axis=-1, keepdims=True)))
            return tuple(new)

        init1 = (jnp.zeros((tb, LANES), F32), jnp.zeros((tb, 1), F32))
        carry = lax.fori_loop(0, i, lambda kbi, c: step(kbi, c, False), (init1, init1))
        dqs = []
        for h, (acc, rs) in enumerate(step(i, carry, True)):
            dqs.append(acc)
            dc_ref[0, 0, pl.ds(h, 1), :] = jnp.broadcast_to(rs, (tb, LANES)).T[0:1, :]
        dq_ref[...] = jnp.where(masks[0], dqs[0], dqs[1]) * (HEAD_DIM ** -0.5)

    blk = pl.BlockSpec((tb, LANES), lambda j, i: (i, j))
    stat = pl.BlockSpec((2, tb, LANES), lambda j, i: (j, i, 0))
    body = _hosted(body, 8, 5, 2, host, (N_PAIRS, nb))
    return pl.pallas_call(
        body, name=name, grid=(N_PAIRS, nb),
        in_specs=[blk,
                  pl.BlockSpec((T, LANES), lambda j, i: (0, cb + j)),
                  pl.BlockSpec((T, LANES), lambda j, i: (0, 2 * cb + j)),
                  pl.BlockSpec((tb, LANES), lambda j, i: (i, 3 * cb + j)),
                  pl.BlockSpec((1, 2, nb, tb), lambda j, i: (j, 0, 0, 0)),
                  blk, blk, stat] + x_in,
        out_specs=[blk, blk, blk, stat,
                   pl.BlockSpec((1, 1, 2, tb), lambda j, i: (j, i, 0, 0))] + x_out,
        out_shape=[jax.ShapeDtypeStruct((T, D), F32), jax.ShapeDtypeStruct((T, D), F32),
                   jax.ShapeDtypeStruct((T, D), BF16),
                   jax.ShapeDtypeStruct((N_HEADS, T, LANES), F32),
                   jax.ShapeDtypeStruct((N_PAIRS, nb, 2, tb), F32)] + x_shapes,
        scratch_shapes=[pltpu.VMEM((T, LANES), BF16), pltpu.VMEM((T, LANES), BF16)] + x_scratch,
        compiler_params=_cparams("arbitrary", "arbitrary"),
    )(proj, proj, proj, proj, cum4, o, dog, lp, *x_args)


def _flash_bwd_dkv(proj, cum4, do, lp, delta, *, tb, name):
    T = proj.shape[0]
    D = N_HEADS * HEAD_DIM
    nb = T // tb
    cb = D // LANES

    def body(q_ref, k_ref, v_ref, cum_ref, do_ref, lp_ref, dl_ref, dk_ref, dv_ref, dc_ref):
        kbi = pl.program_id(1)
        k = k_ref[...] * (HEAD_DIM ** -0.5)
        v = v_ref[...]
        masks = _head_masks(tb)
        kh = [jnp.where(masks[h], k, 0.0).astype(BF16) for h in range(2)]
        vh = [jnp.where(masks[h], v, 0.0).astype(BF16) for h in range(2)]
        ck = [cum_ref[0, h, pl.ds(kbi, 1), :] for h in range(2)]

        def step(i, carry, masked):
            q0 = pl.multiple_of(i * tb, tb)
            qb = q_ref[pl.ds(q0, tb), :].astype(BF16)
            dob = do_ref[pl.ds(q0, tb), :]
            new = []
            for h in range(2):
                dk, dv, dc = carry[h]
                s = lax.dot_general(qb, kh[h], (((1,), (1,)), ((), ())), preferred_element_type=F32)
                p = jnp.exp(s - ck[h] - lp_ref[h, pl.ds(q0, tb), :][:, 0:1])
                if masked:
                    p = jnp.where(_causal(i, kbi, tb, tb), p, 0.0)
                dp = lax.dot_general(dob, vh[h], (((1,), (1,)), ((), ())), preferred_element_type=F32)
                ds = p * (dp - dl_ref[h, pl.ds(q0, tb), :][:, 0:1])
                dv = dv + lax.dot_general(p.astype(BF16), dob, (((0,), (0,)), ((), ())),
                                          preferred_element_type=F32)
                dk = dk + lax.dot_general(ds.astype(BF16), qb, (((0,), (0,)), ((), ())),
                                          preferred_element_type=F32)
                new.append((dk, dv, dc - jnp.sum(ds, axis=0, keepdims=True)))
            return tuple(new)

        init1 = (jnp.zeros((tb, LANES), F32), jnp.zeros((tb, LANES), F32), jnp.zeros((1, tb), F32))
        carry = step(kbi, (init1, init1), True)
        carry = lax.fori_loop(kbi + 1, nb, lambda i, c: step(i, c, False), carry)
        dks, dvs = [], []
        for h, (dk, dv, dc) in enumerate(carry):
            dks.append(dk)
            dvs.append(dv)
            dc_ref[0, 0, pl.ds(h, 1), :] = dc
        dk_ref[...] = jnp.where(masks[0], dks[0], dks[1]) * (HEAD_DIM ** -0.5)
        dv_ref[...] = jnp.where(masks[0], dvs[0], dvs[1])

    full = pl.BlockSpec((T, LANES), lambda j, i: (0, j))
    stat = pl.BlockSpec((2, T, LANES), lambda j, i: (j, 0, 0))
    blk = pl.BlockSpec((tb, LANES), lambda j, i: (i, j))
    return pl.pallas_call(
        body, name=name, grid=(N_PAIRS, nb),
        in_specs=[full,
                  pl.BlockSpec((tb, LANES), lambda j, i: (i, cb + j)),
                  pl.BlockSpec((tb, LANES), lambda j, i: (i, 2 * cb + j)),
                  pl.BlockSpec((1, 2, nb, tb), lambda j, i: (j, 0, 0, 0)),
                  full, stat, stat],
        out_specs=[blk, blk, pl.BlockSpec((1, 1, 2, tb), lambda j, i: (j, i, 0, 0))],
        out_shape=[jax.ShapeDtypeStruct((T, D), F32), jax.ShapeDtypeStruct((T, D), F32),
                   jax.ShapeDtypeStruct((N_PAIRS, nb, 2, tb), F32)],
        compiler_params=_cparams("parallel", "arbitrary"),
    )(proj, proj, proj, cum4, do, lp, delta)


def _cumsum_fwd(proj, bf_row, *, tt, name):
    T = proj.shape[0]
    cb = (proj.shape[1] - LANES) // LANES

    def body(f_ref, b_ref, out_ref, carry_ref):
        i = pl.program_id(0)

        @pl.when(i == 0)
        def _():
            carry_ref[...] = jnp.zeros_like(carry_ref)

        ls = -_softplus(-(f_ref[...] + b_ref[...]))
        tri = (lax.broadcasted_iota(jnp.int32, (tt, tt), 0)
               >= lax.broadcasted_iota(jnp.int32, (tt, tt), 1)).astype(F32)
        cum = jnp.dot(tri, ls, preferred_element_type=F32,
                      precision=lax.Precision.HIGHEST) + carry_ref[...]
        carry_ref[...] = cum[tt - 1:tt, :]
        out_ref[...] = cum.T

    return pl.pallas_call(
        body, name=name, grid=(T // tt,),
        in_specs=[pl.BlockSpec((tt, LANES), lambda i: (i, cb)),
                  pl.BlockSpec((1, LANES), lambda i: (0, 0))],
        out_specs=pl.BlockSpec((LANES, tt), lambda i: (0, i)),
        out_shape=jax.ShapeDtypeStruct((LANES, T), F32),
        scratch_shapes=[pltpu.VMEM((1, LANES), F32)],
        compiler_params=_cparams("arbitrary"),
    )(proj, bf_row)


def _cumsum_bwd(dcum_t, proj, bf_row, *, tt, name):
    T = proj.shape[0]
    cb = (proj.shape[1] - LANES) // LANES
    nt = T // tt

    def body(dc_ref, f_ref, b_ref, df_ref, db_ref, carry_ref):
        i = pl.program_id(0)

        @pl.when(i == 0)
        def _():
            carry_ref[...] = jnp.zeros_like(carry_ref)
            db_ref[...] = jnp.zeros_like(db_ref)

        dc = dc_ref[...].T
        tri = (lax.broadcasted_iota(jnp.int32, (tt, tt), 0)
               <= lax.broadcasted_iota(jnp.int32, (tt, tt), 1)).astype(F32)
        rev = jnp.dot(tri, dc, preferred_element_type=F32,
                      precision=lax.Precision.HIGHEST) + carry_ref[...]
        carry_ref[...] = rev[0:1, :]
        df = rev * _sigmoid(-(f_ref[...] + b_ref[...]))
        df_ref[...] = df
        db_ref[...] += jnp.sum(df, axis=0, keepdims=True)

    return pl.pallas_call(
        body, name=name, grid=(nt,),
        in_specs=[pl.BlockSpec((LANES, tt), lambda i: (0, nt - 1 - i)),
                  pl.BlockSpec((tt, LANES), lambda i: (nt - 1 - i, cb)),
                  pl.BlockSpec((1, LANES), lambda i: (0, 0))],
        out_specs=[pl.BlockSpec((tt, LANES), lambda i: (nt - 1 - i, 0)),
                   pl.BlockSpec((1, LANES), lambda i: (0, 0))],
        out_shape=[jax.ShapeDtypeStruct((T, LANES), F32), jax.ShapeDtypeStruct((1, LANES), F32)],
        scratch_shapes=[pltpu.VMEM((1, LANES), F32)],
        compiler_params=_cparams("arbitrary"),
    )(dcum_t, proj, bf_row)


def _rg_gates(upad_ref, small_ref, wa_ref, wi_ref, tt):
    off = SUBLANES - (CONV_WIDTH - 1)
    u = small_ref[4:5, :]
    for tap in range(CONV_WIDTH):
        u = u + upad_ref[off + tap:off + tap + tt, :] * small_ref[tap:tap + 1, :]
    pa, pi = [], []
    for n in range(RNN_BLOCKS):
        ub = u[:, n * RNN_BLOCK_WIDTH:(n + 1) * RNN_BLOCK_WIDTH].astype(BF16)
        pa.append(jnp.dot(ub, wa_ref[n], preferred_element_type=F32))
        pi.append(jnp.dot(ub, wi_ref[n], preferred_element_type=F32))
    r = _sigmoid(jnp.concatenate(pa, axis=-1) + small_ref[5:6, :])
    ig = _sigmoid(jnp.concatenate(pi, axis=-1) + small_ref[6:7, :])
    spl = _softplus(-small_ref[7:8, :])
    log_a = (-LRU_C) * r * spl
    a = jnp.exp(log_a)
    s = jnp.sqrt(jnp.tanh(-log_a) * (a * a + 1.0))
    return u, r, ig, spl, a, s


def _rg_fwd(proj, small, wa, wi, *, tt, name):
    T = proj.shape[0]
    D = RNN_BLOCKS * RNN_BLOCK_WIDTH
    hb = tt // SUBLANES

    def body(u0_ref, halo_ref, g_ref, small_ref, wa_ref, wi_ref, h_ref, y_ref,
             upad_ref, a_ref, b_ref, carry_ref):
        i = pl.program_id(0)

        @pl.when(i == 0)
        def _():
            carry_ref[...] = jnp.zeros_like(carry_ref)

        upad_ref[0:SUBLANES, :] = jnp.where(i == 0, 0.0, halo_ref[...])
        upad_ref[SUBLANES:, :] = u0_ref[...]
        u, r, ig, spl, a, s = _rg_gates(upad_ref, small_ref, wa_ref, wi_ref, tt)
        a_ref[...] = a
        b_ref[...] = s * (ig * u)

        def row(t, h):
            h = a_ref[pl.ds(t, 1), :] * h + b_ref[pl.ds(t, 1), :]
            h_ref[pl.ds(t, 1), :] = h
            return h

        carry_ref[...] = lax.fori_loop(0, tt, row, carry_ref[...], unroll=8)
        gate = g_ref[...]
        y_ref[...] = h_ref[...] * (gate * _sigmoid(gate))

    return pl.pallas_call(
        body, name=name, grid=(T // tt,),
        in_specs=[pl.BlockSpec((tt, D), lambda i: (i, 0)),
                  pl.BlockSpec((SUBLANES, D), lambda i: (jnp.maximum(i * hb - 1, 0), 0)),
                  pl.BlockSpec((tt, D), lambda i: (i, 1)),
                  pl.BlockSpec((SUBLANES, D), lambda i: (0, 0)),
                  pl.BlockSpec((RNN_BLOCKS, RNN_BLOCK_WIDTH, RNN_BLOCK_WIDTH), lambda i: (0, 0, 0)),
                  pl.BlockSpec((RNN_BLOCKS, RNN_BLOCK_WIDTH, RNN_BLOCK_WIDTH), lambda i: (0, 0, 0))],
        out_specs=[pl.BlockSpec((tt, D), lambda i: (i, 0)), pl.BlockSpec((tt, D), lambda i: (i, 0))],
        out_shape=[jax.ShapeDtypeStruct((T, D), F32), jax.ShapeDtypeStruct((T, D), F32)],
        scratch_shapes=[pltpu.VMEM((tt + SUBLANES, D), F32), pltpu.VMEM((tt, D), F32),
                        pltpu.VMEM((tt, D), F32), pltpu.VMEM((1, D), F32)],
        compiler_params=_cparams("arbitrary"),
    )(proj, proj, proj, small, wa, wi)


def _rg_bwd(proj, hs, dy, small, wa, wi, *, tt, name):
    T = proj.shape[0]
    D = RNN_BLOCKS * RNN_BLOCK_WIDTH
    W = RNN_BLOCK_WIDTH
    hb = tt // SUBLANES
    nt = T // tt

    def body(u0_ref, uhalo_ref, g_ref, h_ref, hhalo_ref, dy_ref, small_ref, wa_ref, wi_ref,
             dp_ref, dwa_ref, dwi_ref, ds_ref,
             upad_ref, hpad_ref, a_ref, g_s_ref, duext_ref, carry_ref):
        i = pl.program_id(0)
        first_chunk = i == nt - 1

        @pl.when(i == 0)
        def _():
            carry_ref[...] = jnp.zeros_like(carry_ref)
            duext_ref[...] = jnp.zeros_like(duext_ref)
            dwa_ref[...] = jnp.zeros_like(dwa_ref)
            dwi_ref[...] = jnp.zeros_like(dwi_ref)
            ds_ref[...] = jnp.zeros_like(ds_ref)

        upad_ref[0:SUBLANES, :] = jnp.where(first_chunk, 0.0, uhalo_ref[...])
        upad_ref[SUBLANES:, :] = u0_ref[...]
        hpad_ref[0:SUBLANES, :] = jnp.where(first_chunk, 0.0, hhalo_ref[...])
        hpad_ref[SUBLANES:, :] = h_ref[...]
        u, r, ig, spl, a, s = _rg_gates(upad_ref, small_ref, wa_ref, wi_ref, tt)
        gate = g_ref[...]
        sg = _sigmoid(gate)
        dy = dy_ref[...]
        dp_ref[:, D:] = dy * h_ref[...] * (sg * (1.0 + gate * (1.0 - sg)))
        a_ref[...] = a
        g_s_ref[...] = dy * (gate * sg)

        def row(k, c):
            t = tt - 1 - k
            g = g_s_ref[pl.ds(t, 1), :] + c
            g_s_ref[pl.ds(t, 1), :] = g
            return a_ref[pl.ds(t, 1), :] * g

        carry_ref[...] = lax.fori_loop(0, tt, row, carry_ref[...], unroll=8)
        g = g_s_ref[...]
        h_prev = hpad_ref[SUBLANES - 1:SUBLANES - 1 + tt, :]
        iu = ig * u
        d_iu = g * s
        dlog_a = (g * h_prev) * a - (g * iu) * (a * a) / s
        dpre_a = (dlog_a * ((-LRU_C) * spl)) * r * (1.0 - r)
        dpre_i = (d_iu * u) * ig * (1.0 - ig)
        dlam = jnp.sum(dlog_a * r, axis=0, keepdims=True) * (LRU_C * _sigmoid(-small_ref[7:8, :]))
        du_parts = []
        for n in range(RNN_BLOCKS):
            sl = slice(n * W, (n + 1) * W)
            ub = u[:, sl].astype(BF16)
            da_n = dpre_a[:, sl].astype(BF16)
            di_n = dpre_i[:, sl].astype(BF16)
            dwa_ref[n] += lax.dot_general(ub, da_n, (((0,), (0,)), ((), ())),
                                          preferred_element_type=F32)
            dwi_ref[n] += lax.dot_general(ub, di_n, (((0,), (0,)), ((), ())),
                                          preferred_element_type=F32)
            du_parts.append(
                lax.dot_general(da_n, wa_ref[n], (((1,), (1,)), ((), ())), preferred_element_type=F32)
                + lax.dot_general(di_n, wi_ref[n], (((1,), (1,)), ((), ())), preferred_element_type=F32))
        du = d_iu * ig + jnp.concatenate(du_parts, axis=-1)
        off = SUBLANES - (CONV_WIDTH - 1)
        for tap in range(CONV_WIDTH):
            ds_ref[tap:tap + 1, :] += jnp.sum(du * upad_ref[off + tap:off + tap + tt, :],
                                              axis=0, keepdims=True)
        ds_ref[4:5, :] += jnp.sum(du, axis=0, keepdims=True)
        ds_ref[5:6, :] += jnp.sum(dpre_a, axis=0, keepdims=True)
        ds_ref[6:7, :] += jnp.sum(dpre_i, axis=0, keepdims=True)
        ds_ref[7:8, :] += dlam
        duext_ref[0:tt, :] = du
        du0 = jnp.zeros((tt, D), F32)
        for tap in range(CONV_WIDTH):
            sh = CONV_WIDTH - 1 - tap
            du0 = du0 + duext_ref[sh:sh + tt, :] * small_ref[tap:tap + 1, :]
        dp_ref[:, :D] = du0
        duext_ref[tt:, :] = du[0:SUBLANES, :]

    rev = lambda i: nt - 1 - i
    wspec = pl.BlockSpec((RNN_BLOCKS, W, W), lambda i: (0, 0, 0))
    return pl.pallas_call(
        body, name=name, grid=(nt,),
        in_specs=[pl.BlockSpec((tt, D), lambda i: (rev(i), 0)),
                  pl.BlockSpec((SUBLANES, D), lambda i: (jnp.maximum(rev(i) * hb - 1, 0), 0)),
                  pl.BlockSpec((tt, D), lambda i: (rev(i), 1)),
                  pl.BlockSpec((tt, D), lambda i: (rev(i), 0)),
                  pl.BlockSpec((SUBLANES, D), lambda i: (jnp.maximum(rev(i) * hb - 1, 0), 0)),
                  pl.BlockSpec((tt, D), lambda i: (rev(i), 0)),
                  pl.BlockSpec((SUBLANES, D), lambda i: (0, 0)),
                  wspec, wspec],
        out_specs=[pl.BlockSpec((tt, 2 * D), lambda i: (rev(i), 0)),
                   wspec, wspec, pl.BlockSpec((SUBLANES, D), lambda i: (0, 0))],
        out_shape=[jax.ShapeDtypeStruct((T, 2 * D), F32),
                   jax.ShapeDtypeStruct((RNN_BLOCKS, W, W), F32),
                   jax.ShapeDtypeStruct((RNN_BLOCKS, W, W), F32),
                   jax.ShapeDtypeStruct((SUBLANES, D), F32)],
        scratch_shapes=[pltpu.VMEM((tt + SUBLANES, D), F32), pltpu.VMEM((tt + SUBLANES, D), F32),
                        pltpu.VMEM((tt, D), F32), pltpu.VMEM((tt, D), F32),
                        pltpu.VMEM((tt + SUBLANES, D), F32), pltpu.VMEM((1, D), F32)],
        compiler_params=_cparams("arbitrary"),
    )(proj, proj, proj, hs, hs, dy, small, wa, wi)


def _ln_fwd(x, h, g, b, *, tt, name):
    T, D = x.shape

    def body(x_ref, h_ref, g_ref, b_ref, y_ref, zh_ref, rs_ref):
        z = ALPHA * x_ref[...] + h_ref[...]
        mu = jnp.mean(z, axis=-1, keepdims=True)
        zc = z - mu
        rstd = lax.rsqrt(jnp.mean(zc * zc, axis=-1, keepdims=True) + LN_EPS)
        zh = zc * rstd
        zh_ref[...] = zh
        rs_ref[...] = rstd
        y_ref[...] = zh * g_ref[...] + b_ref[...]

    blk = pl.BlockSpec((tt, D), lambda i: (i, 0))
    row = pl.BlockSpec((1, D), lambda i: (0, 0))
    return pl.pallas_call(
        body, name=name, grid=(T // tt,),
        in_specs=[blk, blk, row, row],
        out_specs=[blk, blk, pl.BlockSpec((tt, 1), lambda i: (i, 0))],
        out_shape=[jax.ShapeDtypeStruct((T, D), F32), jax.ShapeDtypeStruct((T, D), F32),
                   jax.ShapeDtypeStruct((T, 1), F32)],
        compiler_params=_cparams("parallel"),
    )(x, h, g, b)


def _ln_bwd(dy, zh, rstd, g, *, tt, name):
    T, D = dy.shape

    def body(dy_ref, zh_ref, rs_ref, g_ref, dz_ref, dg_ref, db_ref):
        @pl.when(pl.program_id(0) == 0)
        def _():
            dg_ref[...] = jnp.zeros_like(dg_ref)
            db_ref[...] = jnp.zeros_like(db_ref)

        dy = dy_ref[...]
        zh = zh_ref[...]
        dg_ref[...] += jnp.sum(dy * zh, axis=0, keepdims=True)
        db_ref[...] += jnp.sum(dy, axis=0, keepdims=True)
        dzh = dy * g_ref[...]
        m1 = jnp.mean(dzh, axis=-1, keepdims=True)
        m2 = jnp.mean(dzh * zh, axis=-1, keepdims=True)
        dz_ref[...] = rs_ref[...] * (dzh - m1 - zh * m2)

    blk = pl.BlockSpec((tt, D), lambda i: (i, 0))
    row = pl.BlockSpec((1, D), lambda i: (0, 0))
    return pl.pallas_call(
        body, name=name, grid=(T // tt,),
        in_specs=[blk, blk, pl.BlockSpec((tt, 1), lambda i: (i, 0)), row],
        out_specs=[blk, row, row],
        out_shape=[jax.ShapeDtypeStruct((T, D), F32), jax.ShapeDtypeStruct((1, D), F32),
                   jax.ShapeDtypeStruct((1, D), F32)],
        compiler_params=_cparams("arbitrary"),
    )(dy, zh, rstd, g)


def _loss(y, tgt, *, tt, name):
    T, D = y.shape

    def body(y_ref, t_ref, l_ref, dy_ref):
        @pl.when(pl.program_id(0) == 0)
        def _():
            l_ref[...] = jnp.zeros_like(l_ref)

        e = y_ref[...] - t_ref[...]
        dy_ref[...] = e * (1.0 / D)
        l_ref[...] += jnp.sum(e * e, axis=0, keepdims=True) * (0.5 / D)

    blk = pl.BlockSpec((tt, D), lambda i: (i, 0))
    return pl.pallas_call(
        body, name=name, grid=(T // tt,),
        in_specs=[blk, blk], out_specs=[pl.BlockSpec((1, D), lambda i: (0, 0)), blk],
        out_shape=[jax.ShapeDtypeStruct((1, D), F32), jax.ShapeDtypeStruct((T, D), F32)],
        compiler_params=_cparams("arbitrary"),
    )(y, tgt)


def _row_tile(rows, target):
    best = SUBLANES
    for t in range(SUBLANES, target + 1, SUBLANES):
        if rows % t == 0:
            best = t
    return best


def _add_own(g, recv, c_idx, *, tr, name):
    _, M, R, C = g.shape

    def body(c_ref, g_ref, r_ref, o_ref):
        o_ref[...] = g_ref[0] + r_ref[...]

    return pl.pallas_call(
        body, name=name,
        grid_spec=pltpu.PrefetchScalarGridSpec(
            num_scalar_prefetch=1, grid=(M, R // tr),
            in_specs=[pl.BlockSpec((1, 1, tr, C), lambda k, i, c: (c[0], k, i, 0)),
                      pl.BlockSpec((1, tr, C), lambda k, i, c: (k, i, 0))],
            out_specs=pl.BlockSpec((1, tr, C), lambda k, i, c: (k, i, 0))),
        out_shape=jax.ShapeDtypeStruct((M, R, C), F32),
        compiler_params=_cparams("parallel", "parallel"),
    )(c_idx, g, recv)


def _adamw_math(g, w_ref, m_ref, v_ref, g_ref, d_ref, nm_ref, nv_ref):
    nm = ADAM_B1 * m_ref[...] + (1.0 - ADAM_B1) * g
    nv = ADAM_B2 * v_ref[...] + (1.0 - ADAM_B2) * (g * g)
    m_hat = nm / (1.0 - ADAM_B1 ** ADAM_STEP)
    v_hat = nv / (1.0 - ADAM_B2 ** ADAM_STEP)
    g_ref[...] = g
    nm_ref[...] = nm
    nv_ref[...] = nv
    d_ref[...] = (-ADAM_LR) * (m_hat / (jnp.sqrt(v_hat) + ADAM_EPS) + ADAM_WD * w_ref[...])


def _adamw(parts, w, m, v, *, tr, name):
    n, R, C = parts.shape
    tr = min(tr, R)

    def body(p_ref, w_ref, m_ref, v_ref, *out_refs):
        g = p_ref[0]
        for k in range(1, n):
            g = g + p_ref[k]
        _adamw_math(g, w_ref, m_ref, v_ref, *out_refs)

    blk = pl.BlockSpec((tr, C), lambda i: (i, 0))
    out = jax.ShapeDtypeStruct((R, C), F32)
    return pl.pallas_call(
        body, name=name, grid=(R // tr,),
        in_specs=[pl.BlockSpec((n, tr, C), lambda i: (0, i, 0)), blk, blk, blk],
        out_specs=[blk, blk, blk, blk], out_shape=[out, out, out, out],
        compiler_params=_cparams("parallel"),
    )(parts, w, m, v)


def _adamw_shard(h, recv, me_idx, w, m, v, *, idx, prev, tr, name):
    _, _, R, C = h.shape
    n_prev = 0 if prev is None else 4

    def body(me_ref, h_ref, r1_ref, r2_ref, r3_ref, w_ref, m_ref, v_ref, *rest):
        g = ((h_ref[0] + r1_ref[0]) + r2_ref[0]) + r3_ref[0]
        _adamw_math(g, w_ref, m_ref, v_ref, *rest[n_prev:])

    blk = pl.BlockSpec((1, tr, C), lambda i, me: (idx, i, 0))

    def slot(d):
        return pl.BlockSpec((1, 1, tr, C), lambda i, me: (me[0] ^ d, 0, i, 0))

    out = jax.ShapeDtypeStruct(w.shape, F32)
    return pl.pallas_call(
        body, name=name,
        grid_spec=pltpu.PrefetchScalarGridSpec(
            num_scalar_prefetch=1, grid=(R // tr,),
            in_specs=[slot(0), slot(1), slot(2), slot(3), blk, blk, blk]
            + [pl.BlockSpec(memory_space=pl.ANY)] * n_prev,
            out_specs=[blk, blk, blk, blk]),
        out_shape=[out, out, out, out],
        input_output_aliases={8 + j: j for j in range(n_prev)},
        compiler_params=_cparams("parallel"),
    )(me_idx, h, recv, recv, recv, w, m, v, *(prev or ()))


SHARD_AXIS = dict(attn_w_in=1, attn_w_out=0, rnn_w_in=1, rnn_w_out=0, rnn_w_a=1, rnn_w_i=1,
                  rnn_conv_w=1, rnn_conv_b=0, rnn_b_a=0, rnn_b_i=0, rnn_lambda=0)
RNN_ROWED = ("rnn_w_out", "rnn_w_a", "rnn_w_i")
SMALL = ("rnn_conv_w", "rnn_conv_b", "rnn_b_a", "rnn_b_i", "rnn_lambda")
PACK_C = 1024


def _elems(shape):
    n = 1
    for s in shape:
        n *= s
    return n


def _pack_rows(p, idx, dtype):
    parts = [p[k][idx].astype(dtype).reshape(-1, PACK_C) for k in RNN_ROWED]
    small = jnp.concatenate([p[k][idx].reshape(-1) for k in SMALL])
    tile_rows = SUBLANES * (4 // jnp.dtype(dtype).itemsize)
    if dtype == BF16:
        small = lax.bitcast_convert_type(small, BF16)
    small = small.reshape(-1, PACK_C)
    parts.append(jnp.pad(small, ((0, tile_rows - small.shape[0]), (0, 0))))
    return jnp.concatenate(parts, axis=0)


def _unpack_rows(flat, shapes):
    out, r = {}, 0
    for k in RNN_ROWED:
        n = _elems(shapes[k]) // PACK_C
        out[k] = flat[r:r + n].reshape(shapes[k])
        r += n
    n_small = sum(_elems(shapes[k]) for k in SMALL)
    small = flat[r:r + n_small // PACK_C].reshape(-1)
    o = 0
    for k in SMALL:
        n = _elems(shapes[k])
        out[k] = small[o:o + n].reshape(shapes[k])
        o += n
    return out


def _join_columns(g, width, *, tr, name):
    _, _, R, S = g.shape

    def body(*refs):
        o_ref = refs[8]
        parts = [refs[r][0, 0].astype(F32) for r in range(8)]
        parts.append(jnp.zeros((tr, width - 8 * S), F32))
        o_ref[...] = jnp.concatenate(parts, axis=-1).astype(o_ref.dtype)

    def shard(r):
        return pl.BlockSpec((1, 1, tr, S), lambda i: (r % 2, r // 2, i, 0))

    return pl.pallas_call(
        body, name=name, grid=(R // tr,),
        in_specs=[shard(r) for r in range(8)],
        out_specs=pl.BlockSpec((tr, width), lambda i: (i, 0)),
        out_shape=jax.ShapeDtypeStruct((R, width), g.dtype),
        compiler_params=_cparams("parallel"),
    )(*([g] * 8))


def _split_columns(parts, S, *, tr, name):
    R = parts[0].shape[0]
    n = len(parts)

    def body(*refs):
        o_ref = refs[n]
        x = jnp.concatenate([r[...] for r in refs[:n]], axis=1)
        for r in range(8):
            o_ref[r % 2, r // 2] = x[:, r * S:(r + 1) * S]

    return pl.pallas_call(
        body, name=name, grid=(R // tr,),
        in_specs=[pl.BlockSpec((tr, p.shape[1]), lambda i: (i, 0)) for p in parts],
        out_specs=pl.BlockSpec((2, 4, tr, S), lambda i: (0, 0, i, 0)),
        out_shape=jax.ShapeDtypeStruct((2, 4, R, S), parts[0].dtype),
        compiler_params=_cparams("parallel"),
    )(*parts)


def _to_full(g, k, sh):
    ax, nd = SHARD_AXIS[k], len(sh)
    perm = tuple(range(2, 2 + ax)) + (1, 0) + tuple(range(2 + ax, 2 + nd))
    return g.transpose(perm).reshape(sh[:ax] + (8 * sh[ax],) + sh[ax + 1:])


def _from_full(full, k, sh):
    ax, nd = SHARD_AXIS[k], len(sh)
    t = full.reshape(sh[:ax] + (4, 2, sh[ax]) + sh[ax + 1:])
    return t.transpose((ax + 1, ax) + tuple(range(ax)) + tuple(range(ax + 2, nd + 2)))


def _unpack_gathered_rows(g, shapes):
    out, r = {}, 0
    for k in RNN_ROWED:
        n = _elems(shapes[k]) // PACK_C
        out[k] = _to_full(g[:, :, r:r + n].reshape((2, 4) + shapes[k]), k, shapes[k])
        r += n
    n_small = sum(_elems(shapes[k]) for k in SMALL)
    nr = 2 * n_small // PACK_C
    small = lax.bitcast_convert_type(g[:, :, r:r + nr].reshape(2, 4, n_small, 2), F32)
    o = 0
    for k in SMALL:
        n = _elems(shapes[k])
        out[k] = _to_full(small[:, :, o:o + n].reshape((2, 4) + shapes[k]), k, shapes[k])
        o += n
    return out


def _pack_grad_rows(full, shapes):
    parts = [_from_full(full[k], k, shapes[k]).reshape(2, 4, -1, PACK_C) for k in RNN_ROWED]
    small = jnp.concatenate(
        [_from_full(full[k], k, shapes[k]).reshape(2, 4, -1) for k in SMALL], axis=-1)
    small = small.reshape(2, 4, -1, PACK_C)
    parts.append(jnp.pad(small, ((0, 0), (0, 0), (0, SUBLANES - small.shape[2]), (0, 0))))
    return jnp.concatenate(parts, axis=2)


def kernel(x, ln_g, ln_b, attn_w_in, attn_b_f, attn_w_out, rnn_w_in, rnn_conv_w, rnn_conv_b, rnn_w_a, rnn_b_a, rnn_w_i, rnn_b_i, rnn_lambda, rnn_w_out, loss_target, m_ln_g, m_ln_b, m_attn_w_in, m_attn_b_f, m_attn_w_out, m_rnn_w_in, m_rnn_conv_w, m_rnn_conv_b, m_rnn_w_a, m_rnn_b_a, m_rnn_w_i, m_rnn_b_i, m_rnn_lambda, m_rnn_w_out, v_ln_g, v_ln_b, v_attn_w_in, v_attn_b_f, v_attn_w_out, v_rnn_w_in, v_rnn_conv_w, v_rnn_conv_b, v_rnn_w_a, v_rnn_b_a, v_rnn_w_i, v_rnn_b_i, v_rnn_lambda, v_rnn_w_out):
    w_loc = dict(attn_w_in=attn_w_in, attn_w_out=attn_w_out, rnn_w_in=rnn_w_in, rnn_w_a=rnn_w_a,
                 rnn_w_i=rnn_w_i, rnn_w_out=rnn_w_out, rnn_conv_w=rnn_conv_w, rnn_conv_b=rnn_conv_b,
                 rnn_b_a=rnn_b_a, rnn_b_i=rnn_b_i, rnn_lambda=rnn_lambda)
    m_loc = dict(attn_w_in=m_attn_w_in, attn_w_out=m_attn_w_out, rnn_w_in=m_rnn_w_in,
                 rnn_w_a=m_rnn_w_a, rnn_w_i=m_rnn_w_i, rnn_w_out=m_rnn_w_out,
                 rnn_conv_w=m_rnn_conv_w, rnn_conv_b=m_rnn_conv_b, rnn_b_a=m_rnn_b_a,
                 rnn_b_i=m_rnn_b_i, rnn_lambda=m_rnn_lambda)
    v_loc = dict(attn_w_in=v_attn_w_in, attn_w_out=v_attn_w_out, rnn_w_in=v_rnn_w_in,
                 rnn_w_a=v_rnn_w_a, rnn_w_i=v_rnn_w_i, rnn_w_out=v_rnn_w_out,
                 rnn_conv_w=v_rnn_conv_w, rnn_conv_b=v_rnn_conv_b, rnn_b_a=v_rnn_b_a,
                 rnn_b_i=v_rnn_b_i, rnn_lambda=v_rnn_lambda)
    shapes = {k: tuple(a.shape[1:]) for k, a in w_loc.items()}
    T, D = x.shape[1], x.shape[2]
    n_f = attn_b_f.shape[1]
    tb = min(1024, T)
    tt_rg = min(128, T)
    tt_ln = min(256, T)
    c_idx = lax.axis_index("c").astype(jnp.int32).reshape(1)
    me_idx = (2 * lax.axis_index("x") + lax.axis_index("y")).astype(jnp.int32).reshape(1)

    def attn_shards(idx):
        return [attn_w_in[idx].astype(BF16), attn_w_out[idx].astype(BF16)]

    def attn_weights(g_in, g_out, idx):
        return (_join_columns(g_in, 4 * D + LANES, tr=256, name=f"a_join{idx}"),
                _to_full(g_out, "attn_w_out", shapes["attn_w_out"]))

    def rnn_weights(g_in, g_rows):
        w = _unpack_gathered_rows(g_rows, shapes)
        w["rnn_w_in"] = _to_full(g_in, "rnn_w_in", shapes["rnn_w_in"])
        w["small"] = jnp.concatenate([w["rnn_conv_w"], w["rnn_conv_b"][None], w["rnn_b_a"][None],
                                      w["rnn_b_i"][None], w["rnn_lambda"][None]])
        return w

    g0 = _ag_c(_run_exchange(_Exchange("gather", attn_shards(0)), "ag_w0_xy"), "ag_w0_c")
    later = _Exchange("gather", attn_shards(1) + [
        rnn_w_in.astype(BF16), jnp.stack([_pack_rows(w_loc, i, BF16) for i in range(2)])])
    w_attn, w_rnn = [attn_weights(g0[0], g0[1], 0), None], [None, None]
    bf_rows = jnp.pad(attn_b_f, ((0, 0), (0, LANES - n_f)))[:, None, :]

    xs, saved = [x[0]], []
    for layer in range(DEPTH):
        idx, xl = layer // 2, xs[-1]
        if layer % 2 == 0:
            w_in, w_out = w_attn[idx]
            proj = _matmul(xl, w_in, trans_b=False, tm=512, tn=1408, name=f"a_proj{layer}")
            cum_t = _cumsum_fwd(proj, bf_rows[idx], tt=min(512, T), name=f"a_cum{layer}")
            cum4 = cum_t[:N_HEADS].reshape(N_PAIRS, 2, T // tb, tb)
            o, og, lp, *got = _flash_fwd(proj, cum4, tb=tb, name=f"a_fwd{layer}",
                                         host=later if layer == 0 else None)
            if layer == 0:
                g1 = _ag_c(got, "ag_w1_c")
                w_attn[1] = attn_weights(g1[0], g1[1], 1)
                w_rnn = [rnn_weights(g1[2][:, :, i], g1[3][:, :, i]) for i in range(2)]
            hbr = _matmul(og, w_out, trans_b=False, tm=512, tn=1024, name=f"a_out{layer}")
            saved.append((proj, cum4, o, og, lp))
        else:
            w = w_rnn[idx]
            proj = _matmul(xl, w["rnn_w_in"], trans_b=False, tm=512, tn=1024,
                           name=f"r_proj{layer}")
            hs, yr = _rg_fwd(proj, w["small"], w["rnn_w_a"], w["rnn_w_i"], tt=tt_rg,
                             name=f"r_fwd{layer}")
            hbr = _matmul(yr, w["rnn_w_out"], trans_b=False, tm=512, tn=1024,
                          name=f"r_out{layer}")
            saved.append((proj, hs, yr))
        y, zh, rstd = _ln_fwd(xl, hbr, ln_g[layer][None], ln_b[layer][None], tt=tt_ln,
                              name=f"ln_fwd{layer}")
        saved[-1] = saved[-1] + (zh, rstd)
        xs.append(y)

    loss_lanes, dy = _loss(xs[-1], loss_target[0], tt=tt_ln, name="loss")
    loss = lax.psum(jnp.sum(loss_lanes), ("x", "y", "c"))

    def reduce_pair(gs, layer):
        recv = _rs_c(gs, f"rs_c{layer}")
        return [_add_own(g, r, c_idx, tr=_row_tile(g.shape[2], 512), name=f"rs_add{layer}_{n}")[:, None]
                for n, (g, r) in enumerate(zip(gs, recv))]

    half, quad = [None] * DEPTH, [None] * DEPTH
    d_ln_g, d_ln_b, d_bf = [None] * DEPTH, [None] * DEPTH, [None, None]
    for layer in reversed(range(DEPTH)):
        idx, xl = layer // 2, xs[layer]
        zh, rstd = saved[layer][-2:]
        dz, dg, db = _ln_bwd(dy, zh, rstd, ln_g[layer][None], tt=tt_ln, name=f"ln_bwd{layer}")
        d_ln_g[layer], d_ln_b[layer] = dg[0], db[0]
        if layer % 2 == 0:
            w_in, w_out = w_attn[idx]
            proj, cum4, o, og, lp = saved[layer][:5]
            dog = _matmul(dz, w_out, trans_b=True, tm=512, tn=1024, name=f"a_dog{layer}")
            dwo = _matmul_tn(og, dz, tm=512, tn=1024, tk=512, name=f"a_dwo{layer}")
            riders = [l for l in range(layer + 1, DEPTH) if quad[l] is None]
            host = _Exchange("scatter", [h for l in riders for h in half[l]]) if riders else None
            dq, dgate, do, delta, dcum_q, *got = _flash_bwd_dq(proj, cum4, o, dog, lp, tb=tb,
                                                               name=f"a_dq{layer}", host=host)
            for l in riders:
                quad[l], got = got[:len(half[l])], got[len(half[l]):]
            dk, dv, dcum_k = _flash_bwd_dkv(proj, cum4, do, lp, delta, tb=tb,
                                            name=f"a_dkv{layer}")
            dcum_t = (dcum_q + dcum_k).transpose(0, 2, 1, 3).reshape(N_HEADS, T)
            dcum_t = jnp.pad(dcum_t, ((0, LANES - N_HEADS), (0, 0)))
            df, dbf = _cumsum_bwd(dcum_t, proj, bf_rows[idx], tt=min(512, T), name=f"a_dcum{layer}")
            d_bf[idx] = dbf[0, :n_f]
            dproj = [dq, dk, dv, dgate, df]
            dwi = _matmul_tn_parts(xl, dproj, tm=512, tk=512, name=f"a_dwi{layer}")
            dy = _matmul(dproj, w_in, trans_b=True, tm=256, tn=512, name=f"a_dx{layer}",
                         add=dz, add_scale=ALPHA)
            half[layer] = reduce_pair(
                [_split_columns(dwi, shapes["attn_w_in"][1], tr=256, name=f"a_split{layer}"),
                 _from_full(dwo, "attn_w_out", shapes["attn_w_out"])], layer)
        else:
            w = w_rnn[idx]
            proj, hs, yr = saved[layer][:3]
            dyr = _matmul(dz, w["rnn_w_out"], trans_b=True, tm=512, tn=1024, name=f"r_dy{layer}")
            dwo = _matmul_tn(yr, dz, tm=512, tn=1024, tk=512, name=f"r_dwo{layer}")
            dproj, dwa, dwi_, dsm = _rg_bwd(proj, hs, dyr, w["small"], w["rnn_w_a"], w["rnn_w_i"],
                                            tt=tt_rg, name=f"r_bwd{layer}")
            dwin = _matmul_tn(xl, dproj, tm=512, tn=1024, tk=512, name=f"r_dwi{layer}")
            dy = _matmul(dproj, w["rnn_w_in"], trans_b=True, tm=512, tn=512, name=f"r_dx{layer}",
                         add=dz, add_scale=ALPHA)
            full = dict(rnn_w_out=dwo, rnn_w_a=dwa, rnn_w_i=dwi_, rnn_conv_w=dsm[0:4],
                        rnn_conv_b=dsm[4], rnn_b_a=dsm[5], rnn_b_i=dsm[6], rnn_lambda=dsm[7])
            half[layer] = reduce_pair([_from_full(dwin, "rnn_w_in", shapes["rnn_w_in"]),
                                       _pack_grad_rows(full, shapes)], layer)
    grad_x = dy[None]
    quad[0] = _run_exchange(_Exchange("scatter", half[0]), "rs_xy0")

    def update(k, n):
        res = None
        for idx in (1, 0):
            layer = 2 * idx + (0 if k.startswith("attn") else 1)
            res = _adamw_shard(half[layer][n], quad[layer][n], me_idx, w_loc[k], m_loc[k], v_loc[k],
                               idx=idx, prev=res, tr=_row_tile(shapes[k][0], 256),
                               name=f"adamw_{k}{idx}")
        return res

    shard_outs = [dict() for _ in range(4)]
    for k, n in (("attn_w_in", 0), ("attn_w_out", 1), ("rnn_w_in", 0)):
        for j, a in enumerate(update(k, n)):
            shard_outs[j][k] = a
    rows = []
    for idx in range(2):
        layer = 2 * idx + 1
        wmv = [_pack_rows(d, idx, F32)[None] for d in (w_loc, m_loc, v_loc)]
        res = _adamw_shard(half[layer][1], quad[layer][1], me_idx, *wmv, idx=0, prev=None,
                           tr=_row_tile(wmv[0].shape[1], 256), name=f"adamw_rows{idx}")
        rows.append([_unpack_rows(a[0], shapes) for a in res])
    for j in range(4):
        for k in RNN_ROWED + SMALL:
            shard_outs[j][k] = jnp.stack([rows[0][j][k], rows[1][j][k]])
    g_sh, d_sh, nm_sh, nv_sh = shard_outs

    def rep_pack(lg, lb, bf):
        rows = jnp.concatenate([lg, lb, jnp.pad(bf.reshape(1, -1), ((0, 0), (0, D - 2 * n_f)))])
        return jnp.pad(rows, ((0, 16 - rows.shape[0]), (0, 0)))

    rep = _all_gather(rep_pack(jnp.stack(d_ln_g), jnp.stack(d_ln_b), jnp.stack(d_bf)), "ag_rep")
    rg, rd, rm, rv = _adamw(rep.reshape(8, 16, D), rep_pack(ln_g, ln_b, attn_b_f),
                            rep_pack(m_ln_g, m_ln_b, m_attn_b_f),
                            rep_pack(v_ln_g, v_ln_b, v_attn_b_f), tr=16, name="adamw_rep")

    def rep_unpack(a):
        return dict(ln_g=a[0:DEPTH], ln_b=a[DEPTH:2 * DEPTH],
                    attn_b_f=a[2 * DEPTH, :2 * n_f].reshape(2, n_f))

    order = ("ln_g", "ln_b", "attn_w_in", "attn_b_f", "attn_w_out", "rnn_w_in", "rnn_conv_w",
             "rnn_conv_b", "rnn_w_a", "rnn_b_a", "rnn_w_i", "rnn_b_i", "rnn_lambda", "rnn_w_out")
    outs = [loss, grad_x]
    for sh, rp in ((g_sh, rg), (d_sh, rd), (nm_sh, rm), (nv_sh, rv)):
        allp = {**sh, **rep_unpack(rp)}
        outs.extend(allp[k] for k in order)
    return tuple(outs)
```

```python
import functools

import jax
import jax.numpy as jnp
from jax import lax
from jax.experimental import pallas as pl
from jax.experimental.pallas import tpu as pltpu

F32 = jnp.float32
BF16 = jnp.bfloat16

DEPTH = 4
N_HEADS = 16
HEAD_DIM = 64
N_PAIRS = N_HEADS // 2
RNN_BLOCKS = 4
RNN_BLOCK_WIDTH = 256
CONV_WIDTH = 4
LRU_C = 8.0
ALPHA = (2.0 * DEPTH) ** 0.25
LN_EPS = 1e-5
ADAM_LR, ADAM_B1, ADAM_B2, ADAM_EPS, ADAM_WD, ADAM_STEP = 0.001, 0.9, 0.999, 1e-8, 0.01, 10

LANES = 128
SUBLANES = 8
VMEM_LIMIT = 48 * 1024 * 1024

MESH = pl.DeviceIdType.MESH
HBM_SPEC = pl.BlockSpec(memory_space=pltpu.HBM)


def _cparams(*sem):
    return pltpu.CompilerParams(dimension_semantics=sem, vmem_limit_bytes=VMEM_LIMIT)


def _sigmoid(x):
    return 1.0 / (1.0 + jnp.exp(-x))


def _softplus(x):
    return jnp.maximum(x, 0.0) + jnp.log(1.0 + jnp.exp(-jnp.abs(x)))


def _a2a(src, *, group, bcast, name):
    n = 2 if group == "c" else 4
    blk = tuple(src.shape) if bcast else tuple(src.shape[1:])

    def body(src_ref, out_ref, send_sems, recv_sems, local_sem):
        x, y, c = lax.axis_index("x"), lax.axis_index("y"), lax.axis_index("c")
        if group == "c":
            me = c

            def peer(d):
                return (x, y, 1 - c), 1 - c
        else:
            me = 2 * x + y

            def peer(d):
                px, py = x ^ (d >> 1), y ^ (d & 1)
                return (px, py, c), 2 * px + py

        def block_for(k):
            return src_ref if bcast else src_ref.at[k]

        local = pltpu.make_async_copy(block_for(me), out_ref.at[me], local_sem)
        local.start()
        sends = []
        for d in range(1, n):
            dev, idx = peer(d)
            cp = pltpu.make_async_remote_copy(
                src_ref=block_for(idx), dst_ref=out_ref.at[me],
                send_sem=send_sems.at[d], recv_sem=recv_sems.at[d],
                device_id=dev, device_id_type=MESH)
            cp.start()
            sends.append(cp)
        for d in range(1, n):
            dev, idx = peer(d)
            pltpu.make_async_remote_copy(
                src_ref=block_for(idx), dst_ref=out_ref.at[idx],
                send_sem=send_sems.at[d], recv_sem=recv_sems.at[d],
                device_id=dev, device_id_type=MESH).wait_recv()
        for cp in sends:
            cp.wait_send()
        local.wait()

    return pl.pallas_call(
        body, name=name,
        out_shape=jax.ShapeDtypeStruct((n,) + blk, src.dtype),
        in_specs=[HBM_SPEC], out_specs=HBM_SPEC,
        scratch_shapes=[pltpu.SemaphoreType.DMA((n,)), pltpu.SemaphoreType.DMA((n,)),
                        pltpu.SemaphoreType.DMA],
    )(src)


def _all_gather(piece, name):
    return _a2a(_a2a(piece, group="xy", bcast=True, name=name + "_xy"),
                group="c", bcast=True, name=name + "_c")


D2D_CHUNKS = 16
ICI_CHUNKS = 8


def _row_chunks(rows, dtype, k):
    unit = SUBLANES * (4 // jnp.dtype(dtype).itemsize)
    assert rows % unit == 0
    units = rows // unit
    k = max(1, min(k, units))
    base, rem = divmod(units, k)
    out, r = [], 0
    for i in range(k):
        n = (base + (1 if i < rem else 0)) * unit
        out.append((r, n))
        r += n
    return out


def _chunks(shape, dtype, k):
    if len(shape) == 2:
        return [(pl.ds(r0, n),) for r0, n in _row_chunks(shape[0], dtype, k)]
    per = max(1, k // shape[0])
    return [(l, pl.ds(r0, n)) for l in range(shape[0]) for r0, n in _row_chunks(shape[1], dtype, per)]


def _mesh_place():
    x, y, c = lax.axis_index("x"), lax.axis_index("y"), lax.axis_index("c")
    return x, y, c, 2 * x + y


def _chip_peer(x, y, c, d):
    px, py = x ^ (d >> 1), y ^ (d & 1)
    return (px, py, c), 2 * px + py


def _remote(src, dst, send_sem, recv_sem, dev):
    return pltpu.make_async_remote_copy(src_ref=src, dst_ref=dst, send_sem=send_sem,
                                        recv_sem=recv_sem, device_id=dev, device_id_type=MESH)


def _comm_call(body, name, ins, out_shapes, n_sems, aliases=None):
    n = len(ins)
    return pl.pallas_call(
        body, name=name,
        out_shape=out_shapes, in_specs=[HBM_SPEC] * n, out_specs=[HBM_SPEC] * n,
        input_output_aliases=aliases or {},
        scratch_shapes=[pltpu.SemaphoreType.DMA((n_sems, n)), pltpu.SemaphoreType.DMA((n_sems, n))],
    )(*ins)


class _Exchange:
    def __init__(self, kind, arrays):
        self.kind, self.arrays, self.n = kind, list(arrays), len(arrays)
        if kind == "gather":
            self.chunks = [_chunks(a.shape, a.dtype, ICI_CHUNKS) for a in arrays]
            self.out_shapes = [jax.ShapeDtypeStruct((2, 4) + tuple(a.shape), a.dtype) for a in arrays]
        else:
            self.chunks = [_chunks(a.shape[1:], a.dtype, ICI_CHUNKS) for a in arrays]
            self.out_shapes = [jax.ShapeDtypeStruct(a.shape, a.dtype) for a in arrays]
        self.sem_shapes = [pltpu.SemaphoreType.DMA((4, self.n)), pltpu.SemaphoreType.DMA((4, self.n))]

    def _blocks(self, srcs, outs, o, c, me, pidx):
        if self.kind == "gather":
            return srcs[o], outs[o].at[c, me], outs[o].at[c, pidx]
        return srcs[o].at[pidx], outs[o].at[me], outs[o].at[pidx]

    def start(self, srcs, outs, send_sems, recv_sems):
        x, y, c, me = _mesh_place()
        if self.kind == "gather":
            for o in range(self.n):
                for idx in self.chunks[o]:
                    pltpu.make_async_copy(srcs[o].at[idx], outs[o].at[(c, me) + idx],
                                          send_sems.at[0, o]).start()
        for d in range(1, 4):
            dev, pidx = _chip_peer(x, y, c, d)
            for o in range(self.n):
                src, dst, _ = self._blocks(srcs, outs, o, c, me, pidx)
                for idx in self.chunks[o]:
                    _remote(src.at[idx], dst.at[idx], send_sems.at[d, o], recv_sems.at[d, o],
                            dev).start()

    def wait(self, srcs, outs, send_sems, recv_sems):
        x, y, c, me = _mesh_place()
        for d in range(1, 4):
            dev, pidx = _chip_peer(x, y, c, d)
            for o in range(self.n):
                src, _, land = self._blocks(srcs, outs, o, c, me, pidx)
                _remote(src, land, send_sems.at[d, o], recv_sems.at[d, o], dev).wait_recv()
        for d in range(1, 4):
            dev, pidx = _chip_peer(x, y, c, d)
            for o in range(self.n):
                src, _, land = self._blocks(srcs, outs, o, c, me, pidx)
                _remote(src, land, send_sems.at[d, o], recv_sems.at[d, o], dev).wait_send()
        if self.kind == "gather":
            for o in range(self.n):
                pltpu.make_async_copy(srcs[o], outs[o].at[c, me], send_sems.at[0, o]).wait()


def _run_exchange(ex, name):
    n = ex.n

    def body(*refs):
        srcs, outs, send_sems, recv_sems = refs[:n], refs[n:2 * n], refs[2 * n], refs[2 * n + 1]
        ex.start(srcs, outs, send_sems, recv_sems)
        ex.wait(srcs, outs, send_sems, recv_sems)

    return _comm_call(body, name, ex.arrays, ex.out_shapes, 4)


def _ag_c(bufs, name):
    n = len(bufs)
    chunks = [_chunks(b.shape[2:], b.dtype, D2D_CHUNKS // 4) for b in bufs]

    def body(*refs):
        srcs, outs, send_sems, recv_sems = refs[:n], refs[n:2 * n], refs[2 * n], refs[2 * n + 1]
        x, y, c, _ = _mesh_place()
        sib = (x, y, 1 - c)
        for o in range(n):
            for k in range(4):
                for idx in chunks[o]:
                    _remote(srcs[o].at[(c, k) + idx], outs[o].at[(c, k) + idx],
                            send_sems.at[0, o], recv_sems.at[0, o], sib).start()
        for o in range(n):
            _remote(srcs[o].at[c], outs[o].at[1 - c], send_sems.at[0, o], recv_sems.at[0, o],
                    sib).wait_recv()
        for o in range(n):
            _remote(srcs[o].at[c], outs[o].at[1 - c], send_sems.at[0, o], recv_sems.at[0, o],
                    sib).wait_send()

    shapes = [jax.ShapeDtypeStruct(b.shape, b.dtype) for b in bufs]
    return _comm_call(body, name, bufs, shapes, 1, aliases={i: i for i in range(n)})


def _rs_c(gs, name):
    n = len(gs)
    chunks = [_chunks(g.shape[2:], g.dtype, max(1, D2D_CHUNKS // g.shape[1])) for g in gs]

    def body(*refs):
        srcs, outs, send_sems, recv_sems = refs[:n], refs[n:2 * n], refs[2 * n], refs[2 * n + 1]
        x, y, c, _ = _mesh_place()
        sib = (x, y, 1 - c)
        for o in range(n):
            for k in range(gs[o].shape[1]):
                for idx in chunks[o]:
                    _remote(srcs[o].at[(1 - c, k) + idx], outs[o].at[(k,) + idx],
                            send_sems.at[0, o], recv_sems.at[0, o], sib).start()
        for o in range(n):
            _remote(srcs[o].at[1 - c], outs[o], send_sems.at[0, o], recv_sems.at[0, o],
                    sib).wait_recv()
        for o in range(n):
            _remote(srcs[o].at[1 - c], outs[o], send_sems.at[0, o], recv_sems.at[0, o],
                    sib).wait_send()

    shapes = [jax.ShapeDtypeStruct(g.shape[1:], g.dtype) for g in gs]
    return _comm_call(body, name, gs, shapes, 1)


def _matmul(a, b, *, trans_b, tm, tn, name, add=None, add_scale=1.0):
    a_parts = list(a) if isinstance(a, (list, tuple)) else [a]
    M, K = a_parts[0].shape[0], sum(p.shape[1] for p in a_parts)
    N = b.shape[0] if trans_b else b.shape[1]
    tm, tn = min(tm, M), min(tn, N)
    assert M % tm == 0 and N % tn == 0
    dn = (((1,), (1,)), ((), ())) if trans_b else (((1,), (0,)), ((), ()))
    na = len(a_parts)

    def body(*refs):
        a_refs, b_ref, o_ref = refs[:na], refs[na], refs[-1]
        av = [r[...].astype(BF16) for r in a_refs]
        av = av[0] if na == 1 else jnp.concatenate(av, axis=1)
        r = lax.dot_general(av, b_ref[...].astype(BF16), dn, preferred_element_type=F32)
        if add is not None:
            r = r + add_scale * refs[na + 1][...]
        o_ref[...] = r

    b_spec = (pl.BlockSpec((tn, K), lambda j, i: (j, 0)) if trans_b
              else pl.BlockSpec((K, tn), lambda j, i: (0, j)))
    in_specs = [pl.BlockSpec((tm, p.shape[1]), lambda j, i: (i, 0)) for p in a_parts] + [b_spec]
    args = a_parts + [b]
    if add is not None:
        in_specs.append(pl.BlockSpec((tm, tn), lambda j, i: (i, j)))
        args.append(add)
    return pl.pallas_call(
        body, name=name, grid=(N // tn, M // tm),
        in_specs=in_specs, out_specs=pl.BlockSpec((tm, tn), lambda j, i: (i, j)),
        out_shape=jax.ShapeDtypeStruct((M, N), F32),
        compiler_params=_cparams("parallel", "parallel"),
    )(*args)


def _matmul_tn(a, b, *, tm, tn, tk, name):
    T, M = a.shape
    N = b.shape[1]
    tm, tn, tk = min(tm, M), min(tn, N), min(tk, T)
    assert M % tm == 0 and N % tn == 0 and T % tk == 0

    def body(a_ref, b_ref, o_ref):
        @pl.when(pl.program_id(2) == 0)
        def _():
            o_ref[...] = jnp.zeros_like(o_ref)

        o_ref[...] += lax.dot_general(a_ref[...].astype(BF16), b_ref[...].astype(BF16),
                                      (((0,), (0,)), ((), ())), preferred_element_type=F32)

    return pl.pallas_call(
        body, name=name, grid=(M // tm, N // tn, T // tk),
        in_specs=[pl.BlockSpec((tk, tm), lambda i, j, k: (k, i)),
                  pl.BlockSpec((tk, tn), lambda i, j, k: (k, j))],
        out_specs=pl.BlockSpec((tm, tn), lambda i, j, k: (i, j)),
        out_shape=jax.ShapeDtypeStruct((M, N), F32),
        compiler_params=_cparams("parallel", "parallel", "arbitrary"),
    )(a, b)


def _matmul_tn_parts(a, parts, *, tm, tk, name):
    T, M = a.shape
    tm, tk = min(tm, M), min(tk, T)
    assert M % tm == 0 and T % tk == 0
    n = len(parts)

    def body(*refs):
        a_ref, b_refs, o_refs = refs[0], refs[1:1 + n], refs[1 + n:]
        av = a_ref[...].astype(BF16)
        for b_ref, o_ref in zip(b_refs, o_refs):
            @pl.when(pl.program_id(1) == 0)
            def _(o_ref=o_ref):
                o_ref[...] = jnp.zeros_like(o_ref)

            o_ref[...] += lax.dot_general(av, b_ref[...].astype(BF16), (((0,), (0,)), ((), ())),
                                          preferred_element_type=F32)

    return pl.pallas_call(
        body, name=name, grid=(M // tm, T // tk),
        in_specs=[pl.BlockSpec((tk, tm), lambda i, k: (k, i))]
        + [pl.BlockSpec((tk, p.shape[1]), lambda i, k: (k, 0)) for p in parts],
        out_specs=[pl.BlockSpec((tm, p.shape[1]), lambda i, k: (i, 0)) for p in parts],
        out_shape=[jax.ShapeDtypeStruct((M, p.shape[1]), F32) for p in parts],
        compiler_params=_cparams("parallel", "arbitrary"),
    )(a, *parts)


def _head_masks(rows):
    lane = lax.broadcasted_iota(jnp.int32, (rows, LANES), 1)
    return lane < HEAD_DIM, lane >= HEAD_DIM


def _causal(i_q, i_k, tq, tk):
    row = i_q * tq + lax.broadcasted_iota(jnp.int32, (tq, tk), 0)
    col = i_k * tk + lax.broadcasted_iota(jnp.int32, (tq, tk), 1)
    return row >= col


def _hosted(body, n_in, n_out, n_scratch, host, grid):
    if host is None:
        return body
    nx = host.n

    def wrapped(*refs):
        ins, xsrcs = refs[:n_in], refs[n_in:n_in + nx]
        outs = refs[n_in + nx:n_in + nx + n_out]
        xouts = refs[n_in + nx + n_out:n_in + 2 * nx + n_out]
        scratch = refs[n_in + 2 * nx + n_out:n_in + 2 * nx + n_out + n_scratch]
        xsems = refs[n_in + 2 * nx + n_out + n_scratch:]
        step = pl.program_id(0) * grid[1] + pl.program_id(1)

        @pl.when(step == 0)
        def _():
            host.start(xsrcs, xouts, *xsems)

        body(*ins, *outs, *scratch)

        @pl.when(step == grid[0] * grid[1] - 1)
        def _():
            host.wait(xsrcs, xouts, *xsems)

    return wrapped


def _host_specs(host):
    if host is None:
        return [], [], [], [], []
    return ([HBM_SPEC] * host.n, [HBM_SPEC] * host.n, host.out_shapes, host.sem_shapes, host.arrays)


def _flash_fwd(proj, cum4, *, tb, name, host=None):
    T = proj.shape[0]
    D = N_HEADS * HEAD_DIM
    nb = T // tb
    cb = D // LANES
    x_in, x_out, x_shapes, x_scratch, x_args = _host_specs(host)

    def body(q_ref, k_ref, v_ref, g_ref, cum_ref, o_ref, og_ref, lp_ref, kb_ref, vb_ref):
        i = pl.program_id(1)

        @pl.when(i == 0)
        def _():
            kb_ref[...] = k_ref[...].astype(BF16)
            vb_ref[...] = v_ref[...].astype(BF16)

        q = q_ref[...] * (HEAD_DIM ** -0.5)
        masks = _head_masks(tb)
        qh = [jnp.where(masks[h], q, 0.0).astype(BF16) for h in range(2)]
        cref = [cum_ref[0, h, pl.ds(i, 1), :][:, 0:1] for h in range(2)]

        def step(kbi, carry, masked):
            k0 = pl.multiple_of(kbi * tb, tb)
            kblk = kb_ref[pl.ds(k0, tb), :]
            vblk = vb_ref[pl.ds(k0, tb), :]
            new = []
            for h in range(2):
                m, l, acc = carry[h]
                s = lax.dot_general(qh[h], kblk, (((1,), (1,)), ((), ())),
                                    preferred_element_type=F32)
                s = s + (cref[h] - cum_ref[0, h, pl.ds(kbi, 1), :])
                if masked:
                    s = jnp.where(_causal(i, kbi, tb, tb), s, -jnp.inf)
                m_new = jnp.maximum(m, jnp.max(s, axis=-1, keepdims=True))
                alpha = jnp.exp(m - m_new)
                p = jnp.exp(s - m_new)
                l = alpha * l + jnp.sum(p, axis=-1, keepdims=True)
                acc = alpha * acc + jnp.dot(p.astype(BF16), vblk, preferred_element_type=F32)
                new.append((m_new, l, acc))
            return tuple(new)

        init1 = (jnp.full((tb, 1), -jnp.inf, F32), jnp.zeros((tb, 1), F32),
                 jnp.zeros((tb, LANES), F32))
        carry = lax.fori_loop(0, i, lambda kbi, c: step(kbi, c, False), (init1, init1))
        outs = []
        for h, (m, l, acc) in enumerate(step(i, carry, True)):
            outs.append(acc / l)
            lp_ref[h] = jnp.broadcast_to(m + jnp.log(l) - cref[h], (tb, LANES))
        o = jnp.where(masks[0], outs[0], outs[1])
        o_ref[...] = o
        gate = g_ref[...]
        og_ref[...] = (o * (gate * _sigmoid(gate))).astype(BF16)

    body = _hosted(body, 5, 3, 2, host, (N_PAIRS, nb))
    return pl.pallas_call(
        body, name=name, grid=(N_PAIRS, nb),
        in_specs=[pl.BlockSpec((tb, LANES), lambda j, i: (i, j)),
                  pl.BlockSpec((T, LANES), lambda j, i: (0, cb + j)),
                  pl.BlockSpec((T, LANES), lambda j, i: (0, 2 * cb + j)),
                  pl.BlockSpec((tb, LANES), lambda j, i: (i, 3 * cb + j)),
                  pl.BlockSpec((1, 2, nb, tb), lambda j, i: (j, 0, 0, 0))] + x_in,
        out_specs=[pl.BlockSpec((tb, LANES), lambda j, i: (i, j)),
                   pl.BlockSpec((tb, LANES), lambda j, i: (i, j)),
                   pl.BlockSpec((2, tb, LANES), lambda j, i: (j, i, 0))] + x_out,
        out_shape=[jax.ShapeDtypeStruct((T, D), F32), jax.ShapeDtypeStruct((T, D), BF16),
                   jax.ShapeDtypeStruct((N_HEADS, T, LANES), F32)] + x_shapes,
        scratch_shapes=[pltpu.VMEM((T, LANES), BF16), pltpu.VMEM((T, LANES), BF16)] + x_scratch,
        compiler_params=_cparams("arbitrary", "arbitrary"),
    )(proj, proj, proj, proj, cum4, *x_args)


def _flash_bwd_dq(proj, cum4, o, dog, lp, *, tb, name, host=None):
    T = proj.shape[0]
    D = N_HEADS * HEAD_DIM
    nb = T // tb
    cb = D // LANES
    x_in, x_out, x_shapes, x_scratch, x_args = _host_specs(host)

    def body(q_ref, k_ref, v_ref, g_ref, cum_ref, o_ref, dog_ref, lp_ref,
             dq_ref, dg_ref, do_ref, dl_ref, dc_ref, kb_ref, vb_ref):
        i = pl.program_id(1)

        @pl.when(i == 0)
        def _():
            kb_ref[...] = k_ref[...].astype(BF16)
            vb_ref[...] = v_ref[...].astype(BF16)

        gate = g_ref[...]
        sg = _sigmoid(gate)
        o = o_ref[...]
        dog = dog_ref[...]
        do = dog * (gate * sg)
        dg_ref[...] = (dog * o * (sg * (1.0 + gate * (1.0 - sg)))).astype(BF16)
        do_ref[...] = do.astype(BF16)
        q = q_ref[...] * (HEAD_DIM ** -0.5)
        masks = _head_masks(tb)
        qh = [jnp.where(masks[h], q, 0.0).astype(BF16) for h in range(2)]
        doh = [jnp.where(masks[h], do, 0.0).astype(BF16) for h in range(2)]
        delta = [jnp.sum(jnp.where(masks[h], do * o, 0.0), axis=-1, keepdims=True) for h in range(2)]
        lph = [lp_ref[h][:, 0:1] for h in range(2)]
        for h in range(2):
            dl_ref[h] = jnp.broadcast_to(delta[h], (tb, LANES))

        def step(kbi, carry, masked):
            k0 = pl.multiple_of(kbi * tb, tb)
            kblk = kb_ref[pl.ds(k0, tb), :]
            vblk = vb_ref[pl.ds(k0, tb), :]
            new = []
            for h in range(2):
                acc, rs = carry[h]
                s = lax.dot_general(qh[h], kblk, (((1,), (1,)), ((), ())), preferred_element_type=F32)
                p = jnp.exp(s - cum_ref[0, h, pl.ds(kbi, 1), :] - lph[h])
                if masked:
                    p = jnp.where(_causal(i, kbi, tb, tb), p, 0.0)
                dp = lax.dot_general(doh[h], vblk, (((1,), (1,)), ((), ())),
                                     preferred_element_type=F32)
                ds = p * (dp - delta[h])
                new.append((acc + jnp.dot(ds.astype(BF16), kblk, preferred_element_type=F32),
                            rs + jnp.sum(ds, axis=-1, keepdims=True)))
            return tuple(new)

        init1 = (jnp.zeros((tb, LANES), F32), jnp.zeros((tb, 1), F32))
        carry = lax.fori_loop(0, i, lambda kbi, c: step(kbi, c, False), (init1, init1))
        dqs = []
        for h, (acc, rs) in enumerate(step(i, carry, True)):
            dqs.append(acc)
            dc_ref[0, 0, pl.ds(h, 1), :] = jnp.broadcast_to(rs, (tb, LANES)).T[0:1, :]
        dq_ref[...] = (jnp.where(masks[0], dqs[0], dqs[1]) * (HEAD_DIM ** -0.5)).astype(BF16)

    blk = pl.BlockSpec((tb, LANES), lambda j, i: (i, j))
    stat = pl.BlockSpec((2, tb, LANES), lambda j, i: (j, i, 0))
    body = _hosted(body, 8, 5, 2, host, (N_PAIRS, nb))
    return pl.pallas_call(
        body, name=name, grid=(N_PAIRS, nb),
        in_specs=[blk,
                  pl.BlockSpec((T, LANES), lambda j, i: (0, cb + j)),
                  pl.BlockSpec((T, LANES), lambda j, i: (0, 2 * cb + j)),
                  pl.BlockSpec((tb, LANES), lambda j, i: (i, 3 * cb + j)),
                  pl.BlockSpec((1, 2, nb, tb), lambda j, i: (j, 0, 0, 0)),
                  blk, blk, stat] + x_in,
        out_specs=[blk, blk, blk, stat,
                   pl.BlockSpec((1, 1, 2, tb), lambda j, i: (j, i, 0, 0))] + x_out,
        out_shape=[jax.ShapeDtypeStruct((T, D), BF16), jax.ShapeDtypeStruct((T, D), BF16),
                   jax.ShapeDtypeStruct((T, D), BF16),
                   jax.ShapeDtypeStruct((N_HEADS, T, LANES), F32),
                   jax.ShapeDtypeStruct((N_PAIRS, nb, 2, tb), F32)] + x_shapes,
        scratch_shapes=[pltpu.VMEM((T, LANES), BF16), pltpu.VMEM((T, LANES), BF16)] + x_scratch,
        compiler_params=_cparams("arbitrary", "arbitrary"),
    )(proj, proj, proj, proj, cum4, o, dog, lp, *x_args)


def _flash_bwd_dkv(proj, cum4, do, lp, delta, *, tb, name):
    T = proj.shape[0]
    D = N_HEADS * HEAD_DIM
    nb = T // tb
    cb = D // LANES

    def body(q_ref, k_ref, v_ref, cum_ref, do_ref, lp_ref, dl_ref, dk_ref, dv_ref, dc_ref):
        kbi = pl.program_id(1)
        k = k_ref[...] * (HEAD_DIM ** -0.5)
        v = v_ref[...]
        masks = _head_masks(tb)
        kh = [jnp.where(masks[h], k, 0.0).astype(BF16) for h in range(2)]
        vh = [jnp.where(masks[h], v, 0.0).astype(BF16) for h in range(2)]
        ck = [cum_ref[0, h, pl.ds(kbi, 1), :] for h in range(2)]

        def step(i, carry, masked):
            q0 = pl.multiple_of(i * tb, tb)
            qb = q_ref[pl.ds(q0, tb), :].astype(BF16)
            dob = do_ref[pl.ds(q0, tb), :]
            new = []
            for h in range(2):
                dk, dv, dc = carry[h]
                s = lax.dot_general(qb, kh[h], (((1,), (1,)), ((), ())), preferred_element_type=F32)
                p = jnp.exp(s - ck[h] - lp_ref[h, pl.ds(q0, tb), :][:, 0:1])
                if masked:
                    p = jnp.where(_causal(i, kbi, tb, tb), p, 0.0)
                dp = lax.dot_general(dob, vh[h], (((1,), (1,)), ((), ())), preferred_element_type=F32)
                ds = p * (dp - dl_ref[h, pl.ds(q0, tb), :][:, 0:1])
                dv = dv + lax.dot_general(p.astype(BF16), dob, (((0,), (0,)), ((), ())),
                                          preferred_element_type=F32)
                dk = dk + lax.dot_general(ds.astype(BF16), qb, (((0,), (0,)), ((), ())),
                                          preferred_element_type=F32)
                new.append((dk, dv, dc - jnp.sum(ds, axis=0, keepdims=True)))
            return tuple(new)

        init1 = (jnp.zeros((tb, LANES), F32), jnp.zeros((tb, LANES), F32), jnp.zeros((1, tb), F32))
        carry = step(kbi, (init1, init1), True)
        carry = lax.fori_loop(kbi + 1, nb, lambda i, c: step(i, c, False), carry)
        dks, dvs = [], []
        for h, (dk, dv, dc) in enumerate(carry):
            dks.append(dk)
            dvs.append(dv)
            dc_ref[0, 0, pl.ds(h, 1), :] = dc
        dk_ref[...] = (jnp.where(masks[0], dks[0], dks[1]) * (HEAD_DIM ** -0.5)).astype(BF16)
        dv_ref[...] = jnp.where(masks[0], dvs[0], dvs[1]).astype(BF16)

    full = pl.BlockSpec((T, LANES), lambda j, i: (0, j))
    stat = pl.BlockSpec((2, T, LANES), lambda j, i: (j, 0, 0))
    blk = pl.BlockSpec((tb, LANES), lambda j, i: (i, j))
    return pl.pallas_call(
        body, name=name, grid=(N_PAIRS, nb),
        in_specs=[full,
                  pl.BlockSpec((tb, LANES), lambda j, i: (i, cb + j)),
                  pl.BlockSpec((tb, LANES), lambda j, i: (i, 2 * cb + j)),
                  pl.BlockSpec((1, 2, nb, tb), lambda j, i: (j, 0, 0, 0)),
                  full, stat, stat],
        out_specs=[blk, blk, pl.BlockSpec((1, 1, 2, tb), lambda j, i: (j, i, 0, 0))],
        out_shape=[jax.ShapeDtypeStruct((T, D), BF16), jax.ShapeDtypeStruct((T, D), BF16),
                   jax.ShapeDtypeStruct((N_PAIRS, nb, 2, tb), F32)],
        compiler_params=_cparams("parallel", "arbitrary"),
    )(proj, proj, proj, cum4, do, lp, delta)


def _flash_bwd(proj, cum4, o, dog, lp, *, tb, name, host=None):
    T = proj.shape[0]
    D = N_HEADS * HEAD_DIM
    nb = T // tb
    cb = D // LANES
    x_in, x_out, x_shapes, x_scratch, x_args = _host_specs(host)

    def body(q_ref, k_ref, v_ref, g_ref, cum_ref, o_ref, dog_ref, lp_ref,
             dq_ref, dg_ref, dk_ref, dv_ref, dcq_ref, dck_ref,
             kb_ref, vb_ref, dka_ref, dva_ref, dca_ref):
        i = pl.program_id(1)

        @pl.when(i == 0)
        def _():
            kb_ref[...] = k_ref[...].astype(BF16)
            vb_ref[...] = v_ref[...].astype(BF16)
            dka_ref[...] = jnp.zeros_like(dka_ref)
            dva_ref[...] = jnp.zeros_like(dva_ref)
            dca_ref[...] = jnp.zeros_like(dca_ref)

        gate = g_ref[...]
        sg = _sigmoid(gate)
        o = o_ref[...]
        dog = dog_ref[...]
        do = dog * (gate * sg)
        dg_ref[...] = (dog * o * (sg * (1.0 + gate * (1.0 - sg)))).astype(BF16)
        q = q_ref[...] * (HEAD_DIM ** -0.5)
        masks = _head_masks(tb)
        qh = [jnp.where(masks[h], q, 0.0).astype(BF16) for h in range(2)]
        doh = [jnp.where(masks[h], do, 0.0).astype(BF16) for h in range(2)]
        delta = [jnp.sum(jnp.where(masks[h], do * o, 0.0), axis=-1, keepdims=True) for h in range(2)]
        lph = [lp_ref[h][:, 0:1] for h in range(2)]

        def step(kbi, carry, masked):
            k0 = pl.multiple_of(kbi * tb, tb)
            kblk = kb_ref[pl.ds(k0, tb), :]
            vblk = vb_ref[pl.ds(k0, tb), :]
            new, dk, dv = [], None, None
            for h in range(2):
                acc, rs = carry[h]
                s = lax.dot_general(qh[h], kblk, (((1,), (1,)), ((), ())), preferred_element_type=F32)
                p = jnp.exp(s - cum_ref[0, h, pl.ds(kbi, 1), :] - lph[h])
                if masked:
                    p = jnp.where(_causal(i, kbi, tb, tb), p, 0.0)
                dp = lax.dot_general(doh[h], vblk, (((1,), (1,)), ((), ())),
                                     preferred_element_type=F32)
                ds = p * (dp - delta[h])
                pb, dsb = p.astype(BF16), ds.astype(BF16)
                dv_h = lax.dot_general(pb, doh[h], (((0,), (0,)), ((), ())),
                                       preferred_element_type=F32)
                dk_h = lax.dot_general(dsb, qh[h], (((0,), (0,)), ((), ())),
                                       preferred_element_type=F32)
                dv = dv_h if dv is None else dv + dv_h
                dk = dk_h if dk is None else dk + dk_h
                dca_ref[h, pl.ds(kbi, 1), :] -= jnp.sum(ds, axis=0, keepdims=True)
                new.append((acc + jnp.dot(dsb, kblk, preferred_element_type=F32),
                            rs + jnp.sum(ds, axis=-1, keepdims=True)))
            dka_ref[pl.ds(k0, tb), :] += dk
            dva_ref[pl.ds(k0, tb), :] += dv
            return tuple(new)

        init1 = (jnp.zeros((tb, LANES), F32), jnp.zeros((tb, 1), F32))
        carry = lax.fori_loop(0, i, lambda kbi, c: step(kbi, c, False), (init1, init1))
        dqs = []
        for h, (acc, rs) in enumerate(step(i, carry, True)):
            dqs.append(acc)
            dcq_ref[0, 0, pl.ds(h, 1), :] = jnp.broadcast_to(rs, (tb, LANES)).T[0:1, :]
        dq_ref[...] = (jnp.where(masks[0], dqs[0], dqs[1]) * (HEAD_DIM ** -0.5)).astype(BF16)

        @pl.when(i == nb - 1)
        def _():
            dk_ref[...] = dka_ref[...].astype(BF16)
            dv_ref[...] = dva_ref[...].astype(BF16)
            dck_ref[0] = dca_ref[...]

    blk = pl.BlockSpec((tb, LANES), lambda j, i: (i, j))
    full = pl.BlockSpec((T, LANES), lambda j, i: (0, j))
    body = _hosted(body, 8, 6, 5, host, (N_PAIRS, nb))
    return pl.pallas_call(
        body, name=name, grid=(N_PAIRS, nb),
        in_specs=[blk,
                  pl.BlockSpec((T, LANES), lambda j, i: (0, cb + j)),
                  pl.BlockSpec((T, LANES), lambda j, i: (0, 2 * cb + j)),
                  pl.BlockSpec((tb, LANES), lambda j, i: (i, 3 * cb + j)),
                  pl.BlockSpec((1, 2, nb, tb), lambda j, i: (j, 0, 0, 0)),
                  blk, blk, pl.BlockSpec((2, tb, LANES), lambda j, i: (j, i, 0))] + x_in,
        out_specs=[blk, blk, full, full,
                   pl.BlockSpec((1, 1, 2, tb), lambda j, i: (j, i, 0, 0)),
                   pl.BlockSpec((1, 2, nb, tb), lambda j, i: (j, 0, 0, 0))] + x_out,
        out_shape=[jax.ShapeDtypeStruct((T, D), BF16)] * 4
        + [jax.ShapeDtypeStruct((N_PAIRS, nb, 2, tb), F32),
           jax.ShapeDtypeStruct((N_PAIRS, 2, nb, tb), F32)] + x_shapes,
        scratch_shapes=[pltpu.VMEM((T, LANES), BF16), pltpu.VMEM((T, LANES), BF16),
                        pltpu.VMEM((T, LANES), F32), pltpu.VMEM((T, LANES), F32),
                        pltpu.VMEM((2, nb, tb), F32)] + x_scratch,
        compiler_params=_cparams("arbitrary", "arbitrary"),
    )(proj, proj, proj, proj, cum4, o, dog, lp, *x_args)


def _cumsum_fwd(proj, bf_row, *, tt, name):
    T = proj.shape[0]
    cb = (proj.shape[1] - LANES) // LANES

    def body(f_ref, b_ref, out_ref, carry_ref):
        i = pl.program_id(0)

        @pl.when(i == 0)
        def _():
            carry_ref[...] = jnp.zeros_like(carry_ref)

        ls = -_softplus(-(f_ref[...] + b_ref[...]))
        tri = (lax.broadcasted_iota(jnp.int32, (tt, tt), 0)
               >= lax.broadcasted_iota(jnp.int32, (tt, tt), 1)).astype(F32)
        cum = jnp.dot(tri, ls, preferred_element_type=F32,
                      precision=lax.Precision.HIGHEST) + carry_ref[...]
        carry_ref[...] = cum[tt - 1:tt, :]
        out_ref[...] = cum.T

    return pl.pallas_call(
        body, name=name, grid=(T // tt,),
        in_specs=[pl.BlockSpec((tt, LANES), lambda i: (i, cb)),
                  pl.BlockSpec((1, LANES), lambda i: (0, 0))],
        out_specs=pl.BlockSpec((LANES, tt), lambda i: (0, i)),
        out_shape=jax.ShapeDtypeStruct((LANES, T), F32),
        scratch_shapes=[pltpu.VMEM((1, LANES), F32)],
        compiler_params=_cparams("arbitrary"),
    )(proj, bf_row)


def _cumsum_bwd(dcum_t, proj, bf_row, *, tt, name):
    T = proj.shape[0]
    cb = (proj.shape[1] - LANES) // LANES
    nt = T // tt

    def body(dc_ref, f_ref, b_ref, df_ref, db_ref, carry_ref):
        i = pl.program_id(0)

        @pl.when(i == 0)
        def _():
            carry_ref[...] = jnp.zeros_like(carry_ref)
            db_ref[...] = jnp.zeros_like(db_ref)

        dc = dc_ref[...].T
        tri = (lax.broadcasted_iota(jnp.int32, (tt, tt), 0)
               <= lax.broadcasted_iota(jnp.int32, (tt, tt), 1)).astype(F32)
        rev = jnp.dot(tri, dc, preferred_element_type=F32,
                      precision=lax.Precision.HIGHEST) + carry_ref[...]
        carry_ref[...] = rev[0:1, :]
        df = rev * _sigmoid(-(f_ref[...] + b_ref[...]))
        df_ref[...] = df.astype(BF16)
        db_ref[...] += jnp.sum(df, axis=0, keepdims=True)

    return pl.pallas_call(
        body, name=name, grid=(nt,),
        in_specs=[pl.BlockSpec((LANES, tt), lambda i: (0, nt - 1 - i)),
                  pl.BlockSpec((tt, LANES), lambda i: (nt - 1 - i, cb)),
                  pl.BlockSpec((1, LANES), lambda i: (0, 0))],
        out_specs=[pl.BlockSpec((tt, LANES), lambda i: (nt - 1 - i, 0)),
                   pl.BlockSpec((1, LANES), lambda i: (0, 0))],
        out_shape=[jax.ShapeDtypeStruct((T, LANES), BF16), jax.ShapeDtypeStruct((1, LANES), F32)],
        scratch_shapes=[pltpu.VMEM((1, LANES), F32)],
        compiler_params=_cparams("arbitrary"),
    )(dcum_t, proj, bf_row)


def _rg_gates(upad_ref, small_ref, wa_ref, wi_ref, tt):
    off = SUBLANES - (CONV_WIDTH - 1)
    u = small_ref[4:5, :]
    for tap in range(CONV_WIDTH):
        u = u + upad_ref[off + tap:off + tap + tt, :] * small_ref[tap:tap + 1, :]
    pa, pi = [], []
    for n in range(RNN_BLOCKS):
        ub = u[:, n * RNN_BLOCK_WIDTH:(n + 1) * RNN_BLOCK_WIDTH].astype(BF16)
        pa.append(jnp.dot(ub, wa_ref[n], preferred_element_type=F32))
        pi.append(jnp.dot(ub, wi_ref[n], preferred_element_type=F32))
    r = _sigmoid(jnp.concatenate(pa, axis=-1) + small_ref[5:6, :])
    ig = _sigmoid(jnp.concatenate(pi, axis=-1) + small_ref[6:7, :])
    spl = _softplus(-small_ref[7:8, :])
    log_a = (-LRU_C) * r * spl
    a = jnp.exp(log_a)
    s = jnp.sqrt(jnp.tanh(-log_a) * (a * a + 1.0))
    return u, r, ig, spl, a, s


def _rg_fwd(proj, small, wa, wi, *, tt, name):
    T = proj.shape[0]
    D = RNN_BLOCKS * RNN_BLOCK_WIDTH
    hb = tt // SUBLANES

    def body(u0_ref, halo_ref, g_ref, small_ref, wa_ref, wi_ref, h_ref, y_ref,
             upad_ref, a_ref, b_ref, carry_ref):
        i = pl.program_id(0)

        @pl.when(i == 0)
        def _():
            carry_ref[...] = jnp.zeros_like(carry_ref)

        upad_ref[0:SUBLANES, :] = jnp.where(i == 0, 0.0, halo_ref[...])
        upad_ref[SUBLANES:, :] = u0_ref[...]
        u, r, ig, spl, a, s = _rg_gates(upad_ref, small_ref, wa_ref, wi_ref, tt)
        a_ref[...] = a
        b_ref[...] = s * (ig * u)

        def row(t, h):
            h = a_ref[pl.ds(t, 1), :] * h + b_ref[pl.ds(t, 1), :]
            h_ref[pl.ds(t, 1), :] = h
            return h

        carry_ref[...] = lax.fori_loop(0, tt, row, carry_ref[...], unroll=8)
        gate = g_ref[...]
        y_ref[...] = (h_ref[...] * (gate * _sigmoid(gate))).astype(BF16)

    return pl.pallas_call(
        body, name=name, grid=(T // tt,),
        in_specs=[pl.BlockSpec((tt, D), lambda i: (i, 0)),
                  pl.BlockSpec((SUBLANES, D), lambda i: (jnp.maximum(i * hb - 1, 0), 0)),
                  pl.BlockSpec((tt, D), lambda i: (i, 1)),
                  pl.BlockSpec((SUBLANES, D), lambda i: (0, 0)),
                  pl.BlockSpec((RNN_BLOCKS, RNN_BLOCK_WIDTH, RNN_BLOCK_WIDTH), lambda i: (0, 0, 0)),
                  pl.BlockSpec((RNN_BLOCKS, RNN_BLOCK_WIDTH, RNN_BLOCK_WIDTH), lambda i: (0, 0, 0))],
        out_specs=[pl.BlockSpec((tt, D), lambda i: (i, 0)), pl.BlockSpec((tt, D), lambda i: (i, 0))],
        out_shape=[jax.ShapeDtypeStruct((T, D), F32), jax.ShapeDtypeStruct((T, D), BF16)],
        scratch_shapes=[pltpu.VMEM((tt + SUBLANES, D), F32), pltpu.VMEM((tt, D), F32),
                        pltpu.VMEM((tt, D), F32), pltpu.VMEM((1, D), F32)],
        compiler_params=_cparams("arbitrary"),
    )(proj, proj, proj, small, wa, wi)


def _rg_bwd(proj, hs, dy, small, wa, wi, *, tt, name):
    T = proj.shape[0]
    D = RNN_BLOCKS * RNN_BLOCK_WIDTH
    W = RNN_BLOCK_WIDTH
    hb = tt // SUBLANES
    nt = T // tt

    def body(u0_ref, uhalo_ref, g_ref, h_ref, hhalo_ref, dy_ref, small_ref, wa_ref, wi_ref,
             dp_ref, dwa_ref, dwi_ref, ds_ref,
             upad_ref, hpad_ref, a_ref, g_s_ref, duext_ref, carry_ref):
        i = pl.program_id(0)
        first_chunk = i == nt - 1

        @pl.when(i == 0)
        def _():
            carry_ref[...] = jnp.zeros_like(carry_ref)
            duext_ref[...] = jnp.zeros_like(duext_ref)
            dwa_ref[...] = jnp.zeros_like(dwa_ref)
            dwi_ref[...] = jnp.zeros_like(dwi_ref)
            ds_ref[...] = jnp.zeros_like(ds_ref)

        upad_ref[0:SUBLANES, :] = jnp.where(first_chunk, 0.0, uhalo_ref[...])
        upad_ref[SUBLANES:, :] = u0_ref[...]
        hpad_ref[0:SUBLANES, :] = jnp.where(first_chunk, 0.0, hhalo_ref[...])
        hpad_ref[SUBLANES:, :] = h_ref[...]
        u, r, ig, spl, a, s = _rg_gates(upad_ref, small_ref, wa_ref, wi_ref, tt)
        gate = g_ref[...]
        sg = _sigmoid(gate)
        dy = dy_ref[...]
        dp_ref[:, D:] = (dy * h_ref[...] * (sg * (1.0 + gate * (1.0 - sg)))).astype(BF16)
        a_ref[...] = a
        g_s_ref[...] = dy * (gate * sg)

        def row(k, c):
            t = tt - 1 - k
            g = g_s_ref[pl.ds(t, 1), :] + c
            g_s_ref[pl.ds(t, 1), :] = g
            return a_ref[pl.ds(t, 1), :] * g

        carry_ref[...] = lax.fori_loop(0, tt, row, carry_ref[...], unroll=8)
        g = g_s_ref[...]
        h_prev = hpad_ref[SUBLANES - 1:SUBLANES - 1 + tt, :]
        iu = ig * u
        d_iu = g * s
        dlog_a = (g * h_prev) * a - (g * iu) * (a * a) / s
        dpre_a = (dlog_a * ((-LRU_C) * spl)) * r * (1.0 - r)
        dpre_i = (d_iu * u) * ig * (1.0 - ig)
        dlam = jnp.sum(dlog_a * r, axis=0, keepdims=True) * (LRU_C * _sigmoid(-small_ref[7:8, :]))
        du_parts = []
        for n in range(RNN_BLOCKS):
            sl = slice(n * W, (n + 1) * W)
            ub = u[:, sl].astype(BF16)
            da_n = dpre_a[:, sl].astype(BF16)
            di_n = dpre_i[:, sl].astype(BF16)
            dwa_ref[n] += lax.dot_general(ub, da_n, (((0,), (0,)), ((), ())),
                                          preferred_element_type=F32)
            dwi_ref[n] += lax.dot_general(ub, di_n, (((0,), (0,)), ((), ())),
                                          preferred_element_type=F32)
            du_parts.append(
                lax.dot_general(da_n, wa_ref[n], (((1,), (1,)), ((), ())), preferred_element_type=F32)
                + lax.dot_general(di_n, wi_ref[n], (((1,), (1,)), ((), ())), preferred_element_type=F32))
        du = d_iu * ig + jnp.concatenate(du_parts, axis=-1)
        off = SUBLANES - (CONV_WIDTH - 1)
        for tap in range(CONV_WIDTH):
            ds_ref[tap:tap + 1, :] += jnp.sum(du * upad_ref[off + tap:off + tap + tt, :],
                                              axis=0, keepdims=True)
        ds_ref[4:5, :] += jnp.sum(du, axis=0, keepdims=True)
        ds_ref[5:6, :] += jnp.sum(dpre_a, axis=0, keepdims=True)
        ds_ref[6:7, :] += jnp.sum(dpre_i, axis=0, keepdims=True)
        ds_ref[7:8, :] += dlam
        duext_ref[0:tt, :] = du
        du0 = jnp.zeros((tt, D), F32)
        for tap in range(CONV_WIDTH):
            sh = CONV_WIDTH - 1 - tap
            du0 = du0 + duext_ref[sh:sh + tt, :] * small_ref[tap:tap + 1, :]
        dp_ref[:, :D] = du0.astype(BF16)
        duext_ref[tt:, :] = du[0:SUBLANES, :]

    rev = lambda i: nt - 1 - i
    wspec = pl.BlockSpec((RNN_BLOCKS, W, W), lambda i: (0, 0, 0))
    return pl.pallas_call(
        body, name=name, grid=(nt,),
        in_specs=[pl.BlockSpec((tt, D), lambda i: (rev(i), 0)),
                  pl.BlockSpec((SUBLANES, D), lambda i: (jnp.maximum(rev(i) * hb - 1, 0), 0)),
                  pl.BlockSpec((tt, D), lambda i: (rev(i), 1)),
                  pl.BlockSpec((tt, D), lambda i: (rev(i), 0)),
                  pl.BlockSpec((SUBLANES, D), lambda i: (jnp.maximum(rev(i) * hb - 1, 0), 0)),
                  pl.BlockSpec((tt, D), lambda i: (rev(i), 0)),
                  pl.BlockSpec((SUBLANES, D), lambda i: (0, 0)),
                  wspec, wspec],
        out_specs=[pl.BlockSpec((tt, 2 * D), lambda i: (rev(i), 0)),
                   wspec, wspec, pl.BlockSpec((SUBLANES, D), lambda i: (0, 0))],
        out_shape=[jax.ShapeDtypeStruct((T, 2 * D), BF16),
                   jax.ShapeDtypeStruct((RNN_BLOCKS, W, W), F32),
                   jax.ShapeDtypeStruct((RNN_BLOCKS, W, W), F32),
                   jax.ShapeDtypeStruct((SUBLANES, D), F32)],
        scratch_shapes=[pltpu.VMEM((tt + SUBLANES, D), F32), pltpu.VMEM((tt + SUBLANES, D), F32),
                        pltpu.VMEM((tt, D), F32), pltpu.VMEM((tt, D), F32),
                        pltpu.VMEM((tt + SUBLANES, D), F32), pltpu.VMEM((1, D), F32)],
        compiler_params=_cparams("arbitrary"),
    )(proj, proj, proj, hs, hs, dy, small, wa, wi)


def _ln_fwd(x, h, g, b, *, tt, name):
    T, D = x.shape

    def body(x_ref, h_ref, g_ref, b_ref, y_ref, yb_ref, zh_ref, rs_ref):
        z = ALPHA * x_ref[...] + h_ref[...]
        mu = jnp.mean(z, axis=-1, keepdims=True)
        zc = z - mu
        rstd = lax.rsqrt(jnp.mean(zc * zc, axis=-1, keepdims=True) + LN_EPS)
        zh = zc * rstd
        zh_ref[...] = zh
        rs_ref[...] = rstd
        y = zh * g_ref[...] + b_ref[...]
        y_ref[...] = y
        yb_ref[...] = y.astype(BF16)

    blk = pl.BlockSpec((tt, D), lambda i: (i, 0))
    row = pl.BlockSpec((1, D), lambda i: (0, 0))
    return pl.pallas_call(
        body, name=name, grid=(T // tt,),
        in_specs=[blk, blk, row, row],
        out_specs=[blk, blk, blk, pl.BlockSpec((tt, 1), lambda i: (i, 0))],
        out_shape=[jax.ShapeDtypeStruct((T, D), F32), jax.ShapeDtypeStruct((T, D), BF16),
                   jax.ShapeDtypeStruct((T, D), F32), jax.ShapeDtypeStruct((T, 1), F32)],
        compiler_params=_cparams("parallel"),
    )(x, h, g, b)


def _ln_bwd(dy, zh, rstd, g, *, tt, name):
    T, D = dy.shape

    def body(dy_ref, zh_ref, rs_ref, g_ref, dz_ref, dzb_ref, dg_ref, db_ref):
        @pl.when(pl.program_id(0) == 0)
        def _():
            dg_ref[...] = jnp.zeros_like(dg_ref)
            db_ref[...] = jnp.zeros_like(db_ref)

        dy = dy_ref[...]
        zh = zh_ref[...]
        dg_ref[...] += jnp.sum(dy * zh, axis=0, keepdims=True)
        db_ref[...] += jnp.sum(dy, axis=0, keepdims=True)
        dzh = dy * g_ref[...]
        m1 = jnp.mean(dzh, axis=-1, keepdims=True)
        m2 = jnp.mean(dzh * zh, axis=-1, keepdims=True)
        dz = rs_ref[...] * (dzh - m1 - zh * m2)
        dz_ref[...] = dz
        dzb_ref[...] = dz.astype(BF16)

    blk = pl.BlockSpec((tt, D), lambda i: (i, 0))
    row = pl.BlockSpec((1, D), lambda i: (0, 0))
    return pl.pallas_call(
        body, name=name, grid=(T // tt,),
        in_specs=[blk, blk, pl.BlockSpec((tt, 1), lambda i: (i, 0)), row],
        out_specs=[blk, blk, row, row],
        out_shape=[jax.ShapeDtypeStruct((T, D), F32), jax.ShapeDtypeStruct((T, D), BF16),
                   jax.ShapeDtypeStruct((1, D), F32), jax.ShapeDtypeStruct((1, D), F32)],
        compiler_params=_cparams("arbitrary"),
    )(dy, zh, rstd, g)


def _loss(y, tgt, *, tt, name):
    T, D = y.shape

    def body(y_ref, t_ref, l_ref, dy_ref):
        @pl.when(pl.program_id(0) == 0)
        def _():
            l_ref[...] = jnp.zeros_like(l_ref)

        e = y_ref[...] - t_ref[...]
        dy_ref[...] = e * (1.0 / D)
        l_ref[...] += jnp.sum(e * e, axis=0, keepdims=True) * (0.5 / D)

    blk = pl.BlockSpec((tt, D), lambda i: (i, 0))
    return pl.pallas_call(
        body, name=name, grid=(T // tt,),
        in_specs=[blk, blk], out_specs=[pl.BlockSpec((1, D), lambda i: (0, 0)), blk],
        out_shape=[jax.ShapeDtypeStruct((1, D), F32), jax.ShapeDtypeStruct((T, D), F32)],
        compiler_params=_cparams("arbitrary"),
    )(y, tgt)


def _row_tile(rows, target):
    best = SUBLANES
    for t in range(SUBLANES, target + 1, SUBLANES):
        if rows % t == 0:
            best = t
    return best


def _add_own(g, recv, c_idx, *, tr, name):
    _, M, R, C = g.shape

    def body(c_ref, g_ref, r_ref, o_ref):
        o_ref[...] = g_ref[0] + r_ref[...]

    return pl.pallas_call(
        body, name=name,
        grid_spec=pltpu.PrefetchScalarGridSpec(
            num_scalar_prefetch=1, grid=(M, R // tr),
            in_specs=[pl.BlockSpec((1, 1, tr, C), lambda k, i, c: (c[0], k, i, 0)),
                      pl.BlockSpec((1, tr, C), lambda k, i, c: (k, i, 0))],
            out_specs=pl.BlockSpec((1, tr, C), lambda k, i, c: (k, i, 0))),
        out_shape=jax.ShapeDtypeStruct((M, R, C), F32),
        compiler_params=_cparams("parallel", "parallel"),
    )(c_idx, g, recv)


def _adamw_math(g, w_ref, m_ref, v_ref, g_ref, d_ref, nm_ref, nv_ref):
    nm = ADAM_B1 * m_ref[...] + (1.0 - ADAM_B1) * g
    nv = ADAM_B2 * v_ref[...] + (1.0 - ADAM_B2) * (g * g)
    m_hat = nm / (1.0 - ADAM_B1 ** ADAM_STEP)
    v_hat = nv / (1.0 - ADAM_B2 ** ADAM_STEP)
    g_ref[...] = g
    nm_ref[...] = nm
    nv_ref[...] = nv
    d_ref[...] = (-ADAM_LR) * (m_hat / (jnp.sqrt(v_hat) + ADAM_EPS) + ADAM_WD * w_ref[...])


def _adamw(parts, w, m, v, *, tr, name):
    n, R, C = parts.shape
    tr = min(tr, R)

    def body(p_ref, w_ref, m_ref, v_ref, *out_refs):
        g = p_ref[0]
        for k in range(1, n):
            g = g + p_ref[k]
        _adamw_math(g, w_ref, m_ref, v_ref, *out_refs)

    blk = pl.BlockSpec((tr, C), lambda i: (i, 0))
    out = jax.ShapeDtypeStruct((R, C), F32)
    return pl.pallas_call(
        body, name=name, grid=(R // tr,),
        in_specs=[pl.BlockSpec((n, tr, C), lambda i: (0, i, 0)), blk, blk, blk],
        out_specs=[blk, blk, blk, blk], out_shape=[out, out, out, out],
        compiler_params=_cparams("parallel"),
    )(parts, w, m, v)


def _adamw_shard(h, recv, me_idx, w, m, v, *, idx, prev, tr, name):
    _, _, R, C = h.shape
    n_prev = 0 if prev is None else 4

    def body(me_ref, h_ref, r1_ref, r2_ref, r3_ref, w_ref, m_ref, v_ref, *rest):
        g = ((h_ref[0] + r1_ref[0]) + r2_ref[0]) + r3_ref[0]
        _adamw_math(g, w_ref, m_ref, v_ref, *rest[n_prev:])

    blk = pl.BlockSpec((1, tr, C), lambda i, me: (idx, i, 0))

    def slot(d):
        return pl.BlockSpec((1, 1, tr, C), lambda i, me: (me[0] ^ d, 0, i, 0))

    out = jax.ShapeDtypeStruct(w.shape, F32)
    return pl.pallas_call(
        body, name=name,
        grid_spec=pltpu.PrefetchScalarGridSpec(
            num_scalar_prefetch=1, grid=(R // tr,),
            in_specs=[slot(0), slot(1), slot(2), slot(3), blk, blk, blk]
            + [pl.BlockSpec(memory_space=pl.ANY)] * n_prev,
            out_specs=[blk, blk, blk, blk]),
        out_shape=[out, out, out, out],
        input_output_aliases={8 + j: j for j in range(n_prev)},
        compiler_params=_cparams("parallel"),
    )(me_idx, h, recv, recv, recv, w, m, v, *(prev or ()))


SHARD_AXIS = dict(attn_w_in=1, attn_w_out=0, rnn_w_in=1, rnn_w_out=0, rnn_w_a=1, rnn_w_i=1,
                  rnn_conv_w=1, rnn_conv_b=0, rnn_b_a=0, rnn_b_i=0, rnn_lambda=0)
RNN_ROWED = ("rnn_w_out", "rnn_w_a", "rnn_w_i")
SMALL = ("rnn_conv_w", "rnn_conv_b", "rnn_b_a", "rnn_b_i", "rnn_lambda")
PACK_C = 1024


def _elems(shape):
    n = 1
    for s in shape:
        n *= s
    return n


def _pack_rows(p, idx, dtype):
    parts = [p[k][idx].astype(dtype).reshape(-1, PACK_C) for k in RNN_ROWED]
    small = jnp.concatenate([p[k][idx].reshape(-1) for k in SMALL])
    tile_rows = SUBLANES * (4 // jnp.dtype(dtype).itemsize)
    if dtype == BF16:
        small = lax.bitcast_convert_type(small, BF16)
    small = small.reshape(-1, PACK_C)
    parts.append(jnp.pad(small, ((0, tile_rows - small.shape[0]), (0, 0))))
    return jnp.concatenate(parts, axis=0)


def _unpack_rows(flat, shapes):
    out, r = {}, 0
    for k in RNN_ROWED:
        n = _elems(shapes[k]) // PACK_C
        out[k] = flat[r:r + n].reshape(shapes[k])
        r += n
    n_small = sum(_elems(shapes[k]) for k in SMALL)
    small = flat[r:r + n_small // PACK_C].reshape(-1)
    o = 0
    for k in SMALL:
        n = _elems(shapes[k])
        out[k] = small[o:o + n].reshape(shapes[k])
        o += n
    return out


def _join_columns(g, width, *, tr, name):
    _, _, R, S = g.shape

    def body(*refs):
        o_ref = refs[8]
        parts = [refs[r][0, 0].astype(F32) for r in range(8)]
        parts.append(jnp.zeros((tr, width - 8 * S), F32))
        o_ref[...] = jnp.concatenate(parts, axis=-1).astype(o_ref.dtype)

    def shard(r):
        return pl.BlockSpec((1, 1, tr, S), lambda i: (r % 2, r // 2, i, 0))

    return pl.pallas_call(
        body, name=name, grid=(R // tr,),
        in_specs=[shard(r) for r in range(8)],
        out_specs=pl.BlockSpec((tr, width), lambda i: (i, 0)),
        out_shape=jax.ShapeDtypeStruct((R, width), g.dtype),
        compiler_params=_cparams("parallel"),
    )(*([g] * 8))


def _split_columns(parts, S, *, tr, name):
    R = parts[0].shape[0]
    n = len(parts)

    def body(*refs):
        o_ref = refs[n]
        x = jnp.concatenate([r[...] for r in refs[:n]], axis=1)
        for r in range(8):
            o_ref[r % 2, r // 2] = x[:, r * S:(r + 1) * S]

    return pl.pallas_call(
        body, name=name, grid=(R // tr,),
        in_specs=[pl.BlockSpec((tr, p.shape[1]), lambda i: (i, 0)) for p in parts],
        out_specs=pl.BlockSpec((2, 4, tr, S), lambda i: (0, 0, i, 0)),
        out_shape=jax.ShapeDtypeStruct((2, 4, R, S), parts[0].dtype),
        compiler_params=_cparams("parallel"),
    )(*parts)


def _to_full(g, k, sh):
    ax, nd = SHARD_AXIS[k], len(sh)
    perm = tuple(range(2, 2 + ax)) + (1, 0) + tuple(range(2 + ax, 2 + nd))
    return g.transpose(perm).reshape(sh[:ax] + (8 * sh[ax],) + sh[ax + 1:])


def _from_full(full, k, sh):
    ax, nd = SHARD_AXIS[k], len(sh)
    t = full.reshape(sh[:ax] + (4, 2, sh[ax]) + sh[ax + 1:])
    return t.transpose((ax + 1, ax) + tuple(range(ax)) + tuple(range(ax + 2, nd + 2)))


def _unpack_gathered_rows(g, shapes):
    out, r = {}, 0
    for k in RNN_ROWED:
        n = _elems(shapes[k]) // PACK_C
        out[k] = _to_full(g[:, :, r:r + n].reshape((2, 4) + shapes[k]), k, shapes[k])
        r += n
    n_small = sum(_elems(shapes[k]) for k in SMALL)
    nr = 2 * n_small // PACK_C
    small = lax.bitcast_convert_type(g[:, :, r:r + nr].reshape(2, 4, n_small, 2), F32)
    o = 0
    for k in SMALL:
        n = _elems(shapes[k])
        out[k] = _to_full(small[:, :, o:o + n].reshape((2, 4) + shapes[k]), k, shapes[k])
        o += n
    return out


def _pack_grad_rows(full, shapes):
    parts = [_from_full(full[k], k, shapes[k]).reshape(2, 4, -1, PACK_C) for k in RNN_ROWED]
    small = jnp.concatenate(
        [_from_full(full[k], k, shapes[k]).reshape(2, 4, -1) for k in SMALL], axis=-1)
    small = small.reshape(2, 4, -1, PACK_C)
    parts.append(jnp.pad(small, ((0, 0), (0, 0), (0, SUBLANES - small.shape[2]), (0, 0))))
    return jnp.concatenate(parts, axis=2)


def kernel(x, ln_g, ln_b, attn_w_in, attn_b_f, attn_w_out, rnn_w_in, rnn_conv_w, rnn_conv_b, rnn_w_a, rnn_b_a, rnn_w_i, rnn_b_i, rnn_lambda, rnn_w_out, loss_target, m_ln_g, m_ln_b, m_attn_w_in, m_attn_b_f, m_attn_w_out, m_rnn_w_in, m_rnn_conv_w, m_rnn_conv_b, m_rnn_w_a, m_rnn_b_a, m_rnn_w_i, m_rnn_b_i, m_rnn_lambda, m_rnn_w_out, v_ln_g, v_ln_b, v_attn_w_in, v_attn_b_f, v_attn_w_out, v_rnn_w_in, v_rnn_conv_w, v_rnn_conv_b, v_rnn_w_a, v_rnn_b_a, v_rnn_w_i, v_rnn_b_i, v_rnn_lambda, v_rnn_w_out):
    w_loc = dict(attn_w_in=attn_w_in, attn_w_out=attn_w_out, rnn_w_in=rnn_w_in, rnn_w_a=rnn_w_a,
                 rnn_w_i=rnn_w_i, rnn_w_out=rnn_w_out, rnn_conv_w=rnn_conv_w, rnn_conv_b=rnn_conv_b,
                 rnn_b_a=rnn_b_a, rnn_b_i=rnn_b_i, rnn_lambda=rnn_lambda)
    m_loc = dict(attn_w_in=m_attn_w_in, attn_w_out=m_attn_w_out, rnn_w_in=m_rnn_w_in,
                 rnn_w_a=m_rnn_w_a, rnn_w_i=m_rnn_w_i, rnn_w_out=m_rnn_w_out,
                 rnn_conv_w=m_rnn_conv_w, rnn_conv_b=m_rnn_conv_b, rnn_b_a=m_rnn_b_a,
                 rnn_b_i=m_rnn_b_i, rnn_lambda=m_rnn_lambda)
    v_loc = dict(attn_w_in=v_attn_w_in, attn_w_out=v_attn_w_out, rnn_w_in=v_rnn_w_in,
                 rnn_w_a=v_rnn_w_a, rnn_w_i=v_rnn_w_i, rnn_w_out=v_rnn_w_out,
                 rnn_conv_w=v_rnn_conv_w, rnn_conv_b=v_rnn_conv_b, rnn_b_a=v_rnn_b_a,
                 rnn_b_i=v_rnn_b_i, rnn_lambda=v_rnn_lambda)
    shapes = {k: tuple(a.shape[1:]) for k, a in w_loc.items()}
    T, D = x.shape[1], x.shape[2]
    n_f = attn_b_f.shape[1]
    tb = min(1024, T)
    tb_bwd = min(512, T)
    tt_rg = min(128, T)
    tt_ln = min(256, T)
    c_idx = lax.axis_index("c").astype(jnp.int32).reshape(1)
    me_idx = (2 * lax.axis_index("x") + lax.axis_index("y")).astype(jnp.int32).reshape(1)

    def attn_shards(idx):
        return [attn_w_in[idx].astype(BF16), attn_w_out[idx].astype(BF16)]

    def attn_weights(g_in, g_out, idx):
        return (_join_columns(g_in, 4 * D + LANES, tr=256, name=f"a_join{idx}"),
                _to_full(g_out, "attn_w_out", shapes["attn_w_out"]))

    def rnn_weights(g_in, g_rows):
        w = _unpack_gathered_rows(g_rows, shapes)
        w["rnn_w_in"] = _to_full(g_in, "rnn_w_in", shapes["rnn_w_in"])
        w["small"] = jnp.concatenate([w["rnn_conv_w"], w["rnn_conv_b"][None], w["rnn_b_a"][None],
                                      w["rnn_b_i"][None], w["rnn_lambda"][None]])
        return w

    g0 = _ag_c(_run_exchange(_Exchange("gather", attn_shards(0)), "ag_w0_xy"), "ag_w0_c")
    later = _Exchange("gather", attn_shards(1) + [
        rnn_w_in.astype(BF16), jnp.stack([_pack_rows(w_loc, i, BF16) for i in range(2)])])
    w_attn, w_rnn = [attn_weights(g0[0], g0[1], 0), None], [None, None]
    bf_rows = jnp.pad(attn_b_f, ((0, 0), (0, LANES - n_f)))[:, None, :]

    xs, xb, saved = [x[0]], [x[0]], []
    for layer in range(DEPTH):
        idx, xl, xm = layer // 2, xs[-1], xb[-1]
        if layer % 2 == 0:
            w_in, w_out = w_attn[idx]
            proj = _matmul(xm, w_in, trans_b=False, tm=512, tn=1408, name=f"a_proj{layer}")
            cum_t = _cumsum_fwd(proj, bf_rows[idx], tt=min(512, T), name=f"a_cum{layer}")
            cum2 = cum_t[:N_HEADS].reshape(N_PAIRS, 2, T)
            o, og, lp, *got = _flash_fwd(proj, cum2.reshape(N_PAIRS, 2, T // tb, tb), tb=tb,
                                         name=f"a_fwd{layer}", host=later if layer == 0 else None)
            cum4 = cum2.reshape(N_PAIRS, 2, T // tb_bwd, tb_bwd)
            if layer == 0:
                g1 = _ag_c(got, "ag_w1_c")
                w_attn[1] = attn_weights(g1[0], g1[1], 1)
                w_rnn = [rnn_weights(g1[2][:, :, i], g1[3][:, :, i]) for i in range(2)]
            hbr = _matmul(og, w_out, trans_b=False, tm=512, tn=1024, name=f"a_out{layer}")
            saved.append((proj, cum4, o, og, lp))
        else:
            w = w_rnn[idx]
            proj = _matmul(xm, w["rnn_w_in"], trans_b=False, tm=512, tn=1024,
                           name=f"r_proj{layer}")
            hs, yr = _rg_fwd(proj, w["small"], w["rnn_w_a"], w["rnn_w_i"], tt=tt_rg,
                             name=f"r_fwd{layer}")
            hbr = _matmul(yr, w["rnn_w_out"], trans_b=False, tm=512, tn=1024,
                          name=f"r_out{layer}")
            saved.append((proj, hs, yr))
        y, yb, zh, rstd = _ln_fwd(xl, hbr, ln_g[layer][None], ln_b[layer][None], tt=tt_ln,
                                  name=f"ln_fwd{layer}")
        saved[-1] = saved[-1] + (zh, rstd)
        xs.append(y)
        xb.append(yb)

    loss_lanes, dy = _loss(xs[-1], loss_target[0], tt=tt_ln, name="loss")
    loss = lax.psum(jnp.sum(loss_lanes), ("x", "y", "c"))

    def reduce_pair(gs, layer):
        recv = _rs_c(gs, f"rs_c{layer}")
        return [_add_own(g, r, c_idx, tr=_row_tile(g.shape[2], 512), name=f"rs_add{layer}_{n}")[:, None]
                for n, (g, r) in enumerate(zip(gs, recv))]

    half, quad = [None] * DEPTH, [None] * DEPTH
    d_ln_g, d_ln_b, d_bf = [None] * DEPTH, [None] * DEPTH, [None, None]
    for layer in reversed(range(DEPTH)):
        idx, xm = layer // 2, xb[layer]
        zh, rstd = saved[layer][-2:]
        dz, dzb, dg, db = _ln_bwd(dy, zh, rstd, ln_g[layer][None], tt=tt_ln,
                                  name=f"ln_bwd{layer}")
        d_ln_g[layer], d_ln_b[layer] = dg[0], db[0]
        if layer % 2 == 0:
            w_in, w_out = w_attn[idx]
            proj, cum4, o, og, lp = saved[layer][:5]
            dog = _matmul(dzb, w_out, trans_b=True, tm=512, tn=1024, name=f"a_dog{layer}")
            dwo = _matmul_tn(og, dzb, tm=512, tn=1024, tk=512, name=f"a_dwo{layer}")
            riders = [l for l in range(layer + 1, DEPTH) if quad[l] is None]
            host = _Exchange("scatter", [h for l in riders for h in half[l]]) if riders else None
            dq, dgate, dk, dv, dcum_q, dcum_k, *got = _flash_bwd(proj, cum4, o, dog, lp, tb=tb_bwd,
                                                                 name=f"a_bwd{layer}", host=host)
            for l in riders:
                quad[l], got = got[:len(half[l])], got[len(half[l]):]
            dcum_t = (dcum_q.transpose(0, 2, 1, 3) + dcum_k).reshape(N_HEADS, T)
            dcum_t = jnp.pad(dcum_t, ((0, LANES - N_HEADS), (0, 0)))
            df, dbf = _cumsum_bwd(dcum_t, proj, bf_rows[idx], tt=min(512, T), name=f"a_dcum{layer}")
            d_bf[idx] = dbf[0, :n_f]
            dproj = [dq, dk, dv, dgate, df]
            dwi = _matmul_tn_parts(xm, dproj, tm=512, tk=512, name=f"a_dwi{layer}")
            dy = _matmul(dproj, w_in, trans_b=True, tm=512, tn=1024, name=f"a_dx{layer}",
                         add=dz, add_scale=ALPHA)
            half[layer] = reduce_pair(
                [_split_columns(dwi, shapes["attn_w_in"][1], tr=256, name=f"a_split{layer}"),
                 _from_full(dwo, "attn_w_out", shapes["attn_w_out"])], layer)
        else:
            w = w_rnn[idx]
            proj, hs, yr = saved[layer][:3]
            dyr = _matmul(dzb, w["rnn_w_out"], trans_b=True, tm=512, tn=1024, name=f"r_dy{layer}")
            dwo = _matmul_tn(yr, dzb, tm=512, tn=1024, tk=512, name=f"r_dwo{layer}")
            dproj, dwa, dwi_, dsm = _rg_bwd(proj, hs, dyr, w["small"], w["rnn_w_a"], w["rnn_w_i"],
                                            tt=tt_rg, name=f"r_bwd{layer}")
            dwin = _matmul_tn(xm, dproj, tm=512, tn=1024, tk=512, name=f"r_dwi{layer}")
            dy = _matmul(dproj, w["rnn_w_in"], trans_b=True, tm=512, tn=1024, name=f"r_dx{layer}",
                         add=dz, add_scale=ALPHA)
            full = dict(rnn_w_out=dwo, rnn_w_a=dwa, rnn_w_i=dwi_, rnn_conv_w=dsm[0:4],
                        rnn_conv_b=dsm[4], rnn_b_a=dsm[5], rnn_b_i=dsm[6], rnn_lambda=dsm[7])
            half[layer] = reduce_pair([_from_full(dwin, "rnn_w_in", shapes["rnn_w_in"]),
                                       _pack_grad_rows(full, shapes)], layer)
    grad_x = dy[None]
    quad[0] = _run_exchange(_Exchange("scatter", half[0]), "rs_xy0")

    def update(k, n):
        res = None
        for idx in (1, 0):
            layer = 2 * idx + (0 if k.startswith("attn") else 1)
            res = _adamw_shard(half[layer][n], quad[layer][n], me_idx, w_loc[k], m_loc[k], v_loc[k],
                               idx=idx, prev=res, tr=_row_tile(shapes[k][0], 256),
                               name=f"adamw_{k}{idx}")
        return res

    shard_outs = [dict() for _ in range(4)]
    for k, n in (("attn_w_in", 0), ("attn_w_out", 1), ("rnn_w_in", 0)):
        for j, a in enumerate(update(k, n)):
            shard_outs[j][k] = a
    rows = []
    for idx in range(2):
        layer = 2 * idx + 1
        wmv = [_pack_rows(d, idx, F32)[None] for d in (w_loc, m_loc, v_loc)]
        res = _adamw_shard(half[layer][1], quad[layer][1], me_idx, *wmv, idx=0, prev=None,
                           tr=_row_tile(wmv[0].shape[1], 256), name=f"adamw_rows{idx}")
        rows.append([_unpack_rows(a[0], shapes) for a in res])
    for j in range(4):
        for k in RNN_ROWED + SMALL:
            shard_outs[j][k] = jnp.stack([rows[0][j][k], rows[1][j][k]])
    g_sh, d_sh, nm_sh, nv_sh = shard_outs

    def rep_pack(lg, lb, bf):
        rows = jnp.concatenate([lg, lb, jnp.pad(bf.reshape(1, -1), ((0, 0), (0, D - 2 * n_f)))])
        return jnp.pad(rows, ((0, 16 - rows.shape[0]), (0, 0)))

    rep = _all_gather(rep_pack(jnp.stack(d_ln_g), jnp.stack(d_ln_b), jnp.stack(d_bf)), "ag_rep")
    rg, rd, rm, rv = _adamw(rep.reshape(8, 16, D), rep_pack(ln_g, ln_b, attn_b_f),
                            rep_pack(m_ln_g, m_ln_b, m_attn_b_f),
                            rep_pack(v_ln_g, v_ln_b, v_attn_b_f), tr=16, name="adamw_rep")

    def rep_unpack(a):
        return dict(ln_g=a[0:DEPTH], ln_b=a[DEPTH:2 * DEPTH],
                    attn_b_f=a[2 * DEPTH, :2 * n_f].reshape(2, n_f))

    order = ("ln_g", "ln_b", "attn_w_in", "attn_b_f", "attn_w_out", "rnn_w_in", "rnn_conv_w",
             "rnn_conv_b", "rnn_w_a", "rnn_b_a", "rnn_w_i", "rnn_b_i", "rnn_lambda", "rnn_w_out")
    outs = [loss, grad_x]
    for sh, rp in ((g_sh, rg), (d_sh, rd), (nm_sh, rm), (nv_sh, rv)):
        allp = {**sh, **rep_unpack(rp)}
        outs.extend(allp[k] for k in order)
    return tuple(outs)
```

```python
import functools

import jax
import jax.numpy as jnp
from jax import lax
from jax.experimental import pallas as pl
from jax.experimental.pallas import tpu as pltpu

F32 = jnp.float32
BF16 = jnp.bfloat16

DEPTH = 4
N_HEADS = 16
HEAD_DIM = 64
N_PAIRS = N_HEADS // 2
RNN_BLOCKS = 4
RNN_BLOCK_WIDTH = 256
CONV_WIDTH = 4
LRU_C = 8.0
ALPHA = (2.0 * DEPTH) ** 0.25
LN_EPS = 1e-5
ADAM_LR, ADAM_B1, ADAM_B2, ADAM_EPS, ADAM_WD, ADAM_STEP = 0.001, 0.9, 0.999, 1e-8, 0.01, 10

LANES = 128
SUBLANES = 8
VMEM_LIMIT = 48 * 1024 * 1024

MESH = pl.DeviceIdType.MESH
HBM_SPEC = pl.BlockSpec(memory_space=pltpu.HBM)


def _cparams(*sem):
    return pltpu.CompilerParams(dimension_semantics=sem, vmem_limit_bytes=VMEM_LIMIT)


def _sigmoid(x):
    return 1.0 / (1.0 + jnp.exp(-x))


def _softplus(x):
    return jnp.maximum(x, 0.0) + jnp.log(1.0 + jnp.exp(-jnp.abs(x)))


def _a2a(src, *, group, bcast, name):
    n = 2 if group == "c" else 4
    blk = tuple(src.shape) if bcast else tuple(src.shape[1:])

    def body(src_ref, out_ref, send_sems, recv_sems, local_sem):
        x, y, c = lax.axis_index("x"), lax.axis_index("y"), lax.axis_index("c")
        if group == "c":
            me = c

            def peer(d):
                return (x, y, 1 - c), 1 - c
        else:
            me = 2 * x + y

            def peer(d):
                px, py = x ^ (d >> 1), y ^ (d & 1)
                return (px, py, c), 2 * px + py

        def block_for(k):
            return src_ref if bcast else src_ref.at[k]

        local = pltpu.make_async_copy(block_for(me), out_ref.at[me], local_sem)
        local.start()
        sends = []
        for d in range(1, n):
            dev, idx = peer(d)
            cp = pltpu.make_async_remote_copy(
                src_ref=block_for(idx), dst_ref=out_ref.at[me],
                send_sem=send_sems.at[d], recv_sem=recv_sems.at[d],
                device_id=dev, device_id_type=MESH)
            cp.start()
            sends.append(cp)
        for d in range(1, n):
            dev, idx = peer(d)
            pltpu.make_async_remote_copy(
                src_ref=block_for(idx), dst_ref=out_ref.at[idx],
                send_sem=send_sems.at[d], recv_sem=recv_sems.at[d],
                device_id=dev, device_id_type=MESH).wait_recv()
        for cp in sends:
            cp.wait_send()
        local.wait()

    return pl.pallas_call(
        body, name=name,
        out_shape=jax.ShapeDtypeStruct((n,) + blk, src.dtype),
        in_specs=[HBM_SPEC], out_specs=HBM_SPEC,
        scratch_shapes=[pltpu.SemaphoreType.DMA((n,)), pltpu.SemaphoreType.DMA((n,)),
                        pltpu.SemaphoreType.DMA],
    )(src)


def _all_gather(piece, name):
    return _a2a(_a2a(piece, group="xy", bcast=True, name=name + "_xy"),
                group="c", bcast=True, name=name + "_c")


D2D_CHUNKS = 16
ICI_CHUNKS = 8


def _row_chunks(rows, dtype, k):
    unit = SUBLANES * (4 // jnp.dtype(dtype).itemsize)
    assert rows % unit == 0
    units = rows // unit
    k = max(1, min(k, units))
    base, rem = divmod(units, k)
    out, r = [], 0
    for i in range(k):
        n = (base + (1 if i < rem else 0)) * unit
        out.append((r, n))
        r += n
    return out


def _chunks(shape, dtype, k):
    if len(shape) == 2:
        return [(pl.ds(r0, n),) for r0, n in _row_chunks(shape[0], dtype, k)]
    per = max(1, k // shape[0])
    return [(l, pl.ds(r0, n)) for l in range(shape[0]) for r0, n in _row_chunks(shape[1], dtype, per)]


def _mesh_place():
    x, y, c = lax.axis_index("x"), lax.axis_index("y"), lax.axis_index("c")
    return x, y, c, 2 * x + y


def _chip_peer(x, y, c, d):
    px, py = x ^ (d >> 1), y ^ (d & 1)
    return (px, py, c), 2 * px + py


def _remote(src, dst, send_sem, recv_sem, dev):
    return pltpu.make_async_remote_copy(src_ref=src, dst_ref=dst, send_sem=send_sem,
                                        recv_sem=recv_sem, device_id=dev, device_id_type=MESH)


def _comm_call(body, name, ins, out_shapes, n_sems, aliases=None):
    n = len(ins)
    return pl.pallas_call(
        body, name=name,
        out_shape=out_shapes, in_specs=[HBM_SPEC] * n, out_specs=[HBM_SPEC] * n,
        input_output_aliases=aliases or {},
        scratch_shapes=[pltpu.SemaphoreType.DMA((n_sems, n)), pltpu.SemaphoreType.DMA((n_sems, n))],
    )(*ins)


class _Exchange:
    def __init__(self, kind, arrays):
        self.kind, self.arrays, self.n = kind, list(arrays), len(arrays)
        if kind == "gather":
            self.chunks = [_chunks(a.shape, a.dtype, ICI_CHUNKS) for a in arrays]
            self.out_shapes = [jax.ShapeDtypeStruct((2, 4) + tuple(a.shape), a.dtype) for a in arrays]
        else:
            self.chunks = [_chunks(a.shape[1:], a.dtype, ICI_CHUNKS) for a in arrays]
            self.out_shapes = [jax.ShapeDtypeStruct(a.shape, a.dtype) for a in arrays]
        self.sem_shapes = [pltpu.SemaphoreType.DMA((4, self.n)), pltpu.SemaphoreType.DMA((4, self.n))]

    def _blocks(self, srcs, outs, o, c, me, pidx):
        if self.kind == "gather":
            return srcs[o], outs[o].at[c, me], outs[o].at[c, pidx]
        return srcs[o].at[pidx], outs[o].at[me], outs[o].at[pidx]

    def start(self, srcs, outs, send_sems, recv_sems):
        x, y, c, me = _mesh_place()
        if self.kind == "gather":
            for o in range(self.n):
                for idx in self.chunks[o]:
                    pltpu.make_async_copy(srcs[o].at[idx], outs[o].at[(c, me) + idx],
                                          send_sems.at[0, o]).start()
        for d in range(1, 4):
            dev, pidx = _chip_peer(x, y, c, d)
            for o in range(self.n):
                src, dst, _ = self._blocks(srcs, outs, o, c, me, pidx)
                for idx in self.chunks[o]:
                    _remote(src.at[idx], dst.at[idx], send_sems.at[d, o], recv_sems.at[d, o],
                            dev).start()

    def wait(self, srcs, outs, send_sems, recv_sems):
        x, y, c, me = _mesh_place()
        for d in range(1, 4):
            dev, pidx = _chip_peer(x, y, c, d)
            for o in range(self.n):
                src, _, land = self._blocks(srcs, outs, o, c, me, pidx)
                _remote(src, land, send_sems.at[d, o], recv_sems.at[d, o], dev).wait_recv()
        for d in range(1, 4):
            dev, pidx = _chip_peer(x, y, c, d)
            for o in range(self.n):
                src, _, land = self._blocks(srcs, outs, o, c, me, pidx)
                _remote(src, land, send_sems.at[d, o], recv_sems.at[d, o], dev).wait_send()
        if self.kind == "gather":
            for o in range(self.n):
                pltpu.make_async_copy(srcs[o], outs[o].at[c, me], send_sems.at[0, o]).wait()


def _run_exchange(ex, name):
    n = ex.n

    def body(*refs):
        srcs, outs, send_sems, recv_sems = refs[:n], refs[n:2 * n], refs[2 * n], refs[2 * n + 1]
        ex.start(srcs, outs, send_sems, recv_sems)
        ex.wait(srcs, outs, send_sems, recv_sems)

    return _comm_call(body, name, ex.arrays, ex.out_shapes, 4)


def _ag_c(bufs, name):
    n = len(bufs)
    chunks = [_chunks(b.shape[2:], b.dtype, D2D_CHUNKS // 4) for b in bufs]

    def body(*refs):
        srcs, outs, send_sems, recv_sems = refs[:n], refs[n:2 * n], refs[2 * n], refs[2 * n + 1]
        x, y, c, _ = _mesh_place()
        sib = (x, y, 1 - c)
        for o in range(n):
            for k in range(4):
                for idx in chunks[o]:
                    _remote(srcs[o].at[(c, k) + idx], outs[o].at[(c, k) + idx],
                            send_sems.at[0, o], recv_sems.at[0, o], sib).start()
        for o in range(n):
            _remote(srcs[o].at[c], outs[o].at[1 - c], send_sems.at[0, o], recv_sems.at[0, o],
                    sib).wait_recv()
        for o in range(n):
            _remote(srcs[o].at[c], outs[o].at[1 - c], send_sems.at[0, o], recv_sems.at[0, o],
                    sib).wait_send()

    shapes = [jax.ShapeDtypeStruct(b.shape, b.dtype) for b in bufs]
    return _comm_call(body, name, bufs, shapes, 1, aliases={i: i for i in range(n)})


def _rs_c(gs, name):
    n = len(gs)
    chunks = [_chunks(g.shape[2:], g.dtype, max(1, D2D_CHUNKS // g.shape[1])) for g in gs]

    def body(*refs):
        srcs, outs, send_sems, recv_sems = refs[:n], refs[n:2 * n], refs[2 * n], refs[2 * n + 1]
        x, y, c, _ = _mesh_place()
        sib = (x, y, 1 - c)
        for o in range(n):
            for k in range(gs[o].shape[1]):
                for idx in chunks[o]:
                    _remote(srcs[o].at[(1 - c, k) + idx], outs[o].at[(k,) + idx],
                            send_sems.at[0, o], recv_sems.at[0, o], sib).start()
        for o in range(n):
            _remote(srcs[o].at[1 - c], outs[o], send_sems.at[0, o], recv_sems.at[0, o],
                    sib).wait_recv()
        for o in range(n):
            _remote(srcs[o].at[1 - c], outs[o], send_sems.at[0, o], recv_sems.at[0, o],
                    sib).wait_send()

    shapes = [jax.ShapeDtypeStruct(g.shape[1:], g.dtype) for g in gs]
    return _comm_call(body, name, gs, shapes, 1)


def _matmul(a, b, *, trans_b, tm, tn, name, add=None, add_scale=1.0, host=None):
    a_parts = list(a) if isinstance(a, (list, tuple)) else [a]
    M, K = a_parts[0].shape[0], sum(p.shape[1] for p in a_parts)
    N = b.shape[0] if trans_b else b.shape[1]
    tm, tn = min(tm, M), min(tn, N)
    assert M % tm == 0 and N % tn == 0
    dn = (((1,), (1,)), ((), ())) if trans_b else (((1,), (0,)), ((), ()))
    na = len(a_parts)

    def body(*refs):
        a_refs, b_ref, o_ref = refs[:na], refs[na], refs[-1]
        av = [r[...].astype(BF16) for r in a_refs]
        av = av[0] if na == 1 else jnp.concatenate(av, axis=1)
        r = lax.dot_general(av, b_ref[...].astype(BF16), dn, preferred_element_type=F32)
        if add is not None:
            r = r + add_scale * refs[na + 1][...]
        o_ref[...] = r

    b_spec = (pl.BlockSpec((tn, K), lambda j, i: (j, 0)) if trans_b
              else pl.BlockSpec((K, tn), lambda j, i: (0, j)))
    in_specs = [pl.BlockSpec((tm, p.shape[1]), lambda j, i: (i, 0)) for p in a_parts] + [b_spec]
    args = a_parts + [b]
    if add is not None:
        in_specs.append(pl.BlockSpec((tm, tn), lambda j, i: (i, j)))
        args.append(add)
    grid = (N // tn, M // tm)
    x_in, x_out, x_shapes, x_scratch, x_args = _host_specs(host)
    body = _hosted(body, len(args), 1, 0, host, grid)
    outs = pl.pallas_call(
        body, name=name, grid=grid,
        in_specs=in_specs + x_in,
        out_specs=[pl.BlockSpec((tm, tn), lambda j, i: (i, j))] + x_out,
        out_shape=[jax.ShapeDtypeStruct((M, N), F32)] + x_shapes,
        scratch_shapes=x_scratch,
        compiler_params=_cparams(*(("arbitrary",) * 2 if host else ("parallel",) * 2)),
    )(*args, *x_args)
    return outs if host else outs[0]


def _matmul_tn(a, b, *, tm, tn, tk, name):
    T, M = a.shape
    N = b.shape[1]
    tm, tn, tk = min(tm, M), min(tn, N), min(tk, T)
    assert M % tm == 0 and N % tn == 0 and T % tk == 0

    def body(a_ref, b_ref, o_ref):
        @pl.when(pl.program_id(2) == 0)
        def _():
            o_ref[...] = jnp.zeros_like(o_ref)

        o_ref[...] += lax.dot_general(a_ref[...].astype(BF16), b_ref[...].astype(BF16),
                                      (((0,), (0,)), ((), ())), preferred_element_type=F32)

    return pl.pallas_call(
        body, name=name, grid=(M // tm, N // tn, T // tk),
        in_specs=[pl.BlockSpec((tk, tm), lambda i, j, k: (k, i)),
                  pl.BlockSpec((tk, tn), lambda i, j, k: (k, j))],
        out_specs=pl.BlockSpec((tm, tn), lambda i, j, k: (i, j)),
        out_shape=jax.ShapeDtypeStruct((M, N), F32),
        compiler_params=_cparams("parallel", "parallel", "arbitrary"),
    )(a, b)


def _matmul_tn_parts(a, parts, *, tm, tk, name):
    T, M = a.shape
    tm, tk = min(tm, M), min(tk, T)
    assert M % tm == 0 and T % tk == 0
    n = len(parts)

    def body(*refs):
        a_ref, b_refs, o_refs = refs[0], refs[1:1 + n], refs[1 + n:]
        av = a_ref[...].astype(BF16)
        for b_ref, o_ref in zip(b_refs, o_refs):
            @pl.when(pl.program_id(1) == 0)
            def _(o_ref=o_ref):
                o_ref[...] = jnp.zeros_like(o_ref)

            o_ref[...] += lax.dot_general(av, b_ref[...].astype(BF16), (((0,), (0,)), ((), ())),
                                          preferred_element_type=F32)

    return pl.pallas_call(
        body, name=name, grid=(M // tm, T // tk),
        in_specs=[pl.BlockSpec((tk, tm), lambda i, k: (k, i))]
        + [pl.BlockSpec((tk, p.shape[1]), lambda i, k: (k, 0)) for p in parts],
        out_specs=[pl.BlockSpec((tm, p.shape[1]), lambda i, k: (i, 0)) for p in parts],
        out_shape=[jax.ShapeDtypeStruct((M, p.shape[1]), F32) for p in parts],
        compiler_params=_cparams("parallel", "arbitrary"),
    )(a, *parts)


def _head_masks(rows):
    lane = lax.broadcasted_iota(jnp.int32, (rows, LANES), 1)
    return lane < HEAD_DIM, lane >= HEAD_DIM


def _causal(i_q, i_k, tq, tk):
    row = i_q * tq + lax.broadcasted_iota(jnp.int32, (tq, tk), 0)
    col = i_k * tk + lax.broadcasted_iota(jnp.int32, (tq, tk), 1)
    return row >= col


def _hosted(body, n_in, n_out, n_scratch, host, grid):
    if host is None:
        return body
    nx = host.n

    def wrapped(*refs):
        ins, xsrcs = refs[:n_in], refs[n_in:n_in + nx]
        outs = refs[n_in + nx:n_in + nx + n_out]
        xouts = refs[n_in + nx + n_out:n_in + 2 * nx + n_out]
        scratch = refs[n_in + 2 * nx + n_out:n_in + 2 * nx + n_out + n_scratch]
        xsems = refs[n_in + 2 * nx + n_out + n_scratch:]
        step = pl.program_id(0) * grid[1] + pl.program_id(1)

        @pl.when(step == 0)
        def _():
            host.start(xsrcs, xouts, *xsems)

        body(*ins, *outs, *scratch)

        @pl.when(step == grid[0] * grid[1] - 1)
        def _():
            host.wait(xsrcs, xouts, *xsems)

    return wrapped


def _host_specs(host):
    if host is None:
        return [], [], [], [], []
    return ([HBM_SPEC] * host.n, [HBM_SPEC] * host.n, host.out_shapes, host.sem_shapes, host.arrays)


def _flash_fwd(proj, cum4, *, tb, name, host=None):
    T = proj.shape[0]
    D = N_HEADS * HEAD_DIM
    nb = T // tb
    cb = D // LANES
    x_in, x_out, x_shapes, x_scratch, x_args = _host_specs(host)

    def body(q_ref, k_ref, v_ref, g_ref, cum_ref, o_ref, og_ref, lp_ref, kb_ref, vb_ref):
        i = pl.program_id(1)

        @pl.when(i == 0)
        def _():
            kb_ref[...] = k_ref[...].astype(BF16)
            vb_ref[...] = v_ref[...].astype(BF16)

        q = q_ref[...] * (HEAD_DIM ** -0.5)
        masks = _head_masks(tb)
        qh = [jnp.where(masks[h], q, 0.0).astype(BF16) for h in range(2)]
        cref = [cum_ref[0, h, pl.ds(i, 1), :][:, 0:1] for h in range(2)]

        def step(kbi, carry, masked):
            k0 = pl.multiple_of(kbi * tb, tb)
            kblk = kb_ref[pl.ds(k0, tb), :]
            vblk = vb_ref[pl.ds(k0, tb), :]
            new = []
            for h in range(2):
                m, l, acc = carry[h]
                s = lax.dot_general(qh[h], kblk, (((1,), (1,)), ((), ())),
                                    preferred_element_type=F32)
                s = s + (cref[h] - cum_ref[0, h, pl.ds(kbi, 1), :])
                if masked:
                    s = jnp.where(_causal(i, kbi, tb, tb), s, -jnp.inf)
                m_new = jnp.maximum(m, jnp.max(s, axis=-1, keepdims=True))
                alpha = jnp.exp(m - m_new)
                p = jnp.exp(s - m_new)
                l = alpha * l + jnp.sum(p, axis=-1, keepdims=True)
                acc = alpha * acc + jnp.dot(p.astype(BF16), vblk, preferred_element_type=F32)
                new.append((m_new, l, acc))
            return tuple(new)

        init1 = (jnp.full((tb, 1), -jnp.inf, F32), jnp.zeros((tb, 1), F32),
                 jnp.zeros((tb, LANES), F32))
        carry = lax.fori_loop(0, i, lambda kbi, c: step(kbi, c, False), (init1, init1))
        outs = []
        for h, (m, l, acc) in enumerate(step(i, carry, True)):
            outs.append(acc / l)
            lp_ref[h] = jnp.broadcast_to(m + jnp.log(l) - cref[h], (tb, LANES))
        o = jnp.where(masks[0], outs[0], outs[1])
        o_ref[...] = o
        gate = g_ref[...]
        og_ref[...] = (o * (gate * _sigmoid(gate))).astype(BF16)

    body = _hosted(body, 5, 3, 2, host, (N_PAIRS, nb))
    return pl.pallas_call(
        body, name=name, grid=(N_PAIRS, nb),
        in_specs=[pl.BlockSpec((tb, LANES), lambda j, i: (i, j)),
                  pl.BlockSpec((T, LANES), lambda j, i: (0, cb + j)),
                  pl.BlockSpec((T, LANES), lambda j, i: (0, 2 * cb + j)),
                  pl.BlockSpec((tb, LANES), lambda j, i: (i, 3 * cb + j)),
                  pl.BlockSpec((1, 2, nb, tb), lambda j, i: (j, 0, 0, 0))] + x_in,
        out_specs=[pl.BlockSpec((tb, LANES), lambda j, i: (i, j)),
                   pl.BlockSpec((tb, LANES), lambda j, i: (i, j)),
                   pl.BlockSpec((2, tb, LANES), lambda j, i: (j, i, 0))] + x_out,
        out_shape=[jax.ShapeDtypeStruct((T, D), F32), jax.ShapeDtypeStruct((T, D), BF16),
                   jax.ShapeDtypeStruct((N_HEADS, T, LANES), F32)] + x_shapes,
        scratch_shapes=[pltpu.VMEM((T, LANES), BF16), pltpu.VMEM((T, LANES), BF16)] + x_scratch,
        compiler_params=_cparams("arbitrary", "arbitrary"),
    )(proj, proj, proj, proj, cum4, *x_args)


def _flash_bwd_dq(proj, cum4, o, dog, lp, *, tb, name, host=None):
    T = proj.shape[0]
    D = N_HEADS * HEAD_DIM
    nb = T // tb
    cb = D // LANES
    x_in, x_out, x_shapes, x_scratch, x_args = _host_specs(host)

    def body(q_ref, k_ref, v_ref, g_ref, cum_ref, o_ref, dog_ref, lp_ref,
             dq_ref, dg_ref, do_ref, dl_ref, dc_ref, kb_ref, vb_ref):
        i = pl.program_id(1)

        @pl.when(i == 0)
        def _():
            kb_ref[...] = k_ref[...].astype(BF16)
            vb_ref[...] = v_ref[...].astype(BF16)

        gate = g_ref[...]
        sg = _sigmoid(gate)
        o = o_ref[...]
        dog = dog_ref[...]
        do = dog * (gate * sg)
        dg_ref[...] = (dog * o * (sg * (1.0 + gate * (1.0 - sg)))).astype(BF16)
        do_ref[...] = do.astype(BF16)
        q = q_ref[...] * (HEAD_DIM ** -0.5)
        masks = _head_masks(tb)
        qh = [jnp.where(masks[h], q, 0.0).astype(BF16) for h in range(2)]
        doh = [jnp.where(masks[h], do, 0.0).astype(BF16) for h in range(2)]
        delta = [jnp.sum(jnp.where(masks[h], do * o, 0.0), axis=-1, keepdims=True) for h in range(2)]
        lph = [lp_ref[h][:, 0:1] for h in range(2)]
        for h in range(2):
            dl_ref[h] = jnp.broadcast_to(delta[h], (tb, LANES))

        def step(kbi, carry, masked):
            k0 = pl.multiple_of(kbi * tb, tb)
            kblk = kb_ref[pl.ds(k0, tb), :]
            vblk = vb_ref[pl.ds(k0, tb), :]
            new = []
            for h in range(2):
                acc, rs = carry[h]
                s = lax.dot_general(qh[h], kblk, (((1,), (1,)), ((), ())), preferred_element_type=F32)
                p = jnp.exp(s - cum_ref[0, h, pl.ds(kbi, 1), :] - lph[h])
                if masked:
                    p = jnp.where(_causal(i, kbi, tb, tb), p, 0.0)
                dp = lax.dot_general(doh[h], vblk, (((1,), (1,)), ((), ())),
                                     preferred_element_type=F32)
                ds = p * (dp - delta[h])
                new.append((acc + jnp.dot(ds.astype(BF16), kblk, preferred_element_type=F32),
                            rs + jnp.sum(ds, axis=-1, keepdims=True)))
            return tuple(new)

        init1 = (jnp.zeros((tb, LANES), F32), jnp.zeros((tb, 1), F32))
        carry = lax.fori_loop(0, i, lambda kbi, c: step(kbi, c, False), (init1, init1))
        dqs = []
        for h, (acc, rs) in enumerate(step(i, carry, True)):
            dqs.append(acc)
            dc_ref[0, 0, pl.ds(h, 1), :] = jnp.broadcast_to(rs, (tb, LANES)).T[0:1, :]
        dq_ref[...] = (jnp.where(masks[0], dqs[0], dqs[1]) * (HEAD_DIM ** -0.5)).astype(BF16)

    blk = pl.BlockSpec((tb, LANES), lambda j, i: (i, j))
    stat = pl.BlockSpec((2, tb, LANES), lambda j, i: (j, i, 0))
    body = _hosted(body, 8, 5, 2, host, (N_PAIRS, nb))
    return pl.pallas_call(
        body, name=name, grid=(N_PAIRS, nb),
        in_specs=[blk,
                  pl.BlockSpec((T, LANES), lambda j, i: (0, cb + j)),
                  pl.BlockSpec((T, LANES), lambda j, i: (0, 2 * cb + j)),
                  pl.BlockSpec((tb, LANES), lambda j, i: (i, 3 * cb + j)),
                  pl.BlockSpec((1, 2, nb, tb), lambda j, i: (j, 0, 0, 0)),
                  blk, blk, stat] + x_in,
        out_specs=[blk, blk, blk, stat,
                   pl.BlockSpec((1, 1, 2, tb), lambda j, i: (j, i, 0, 0))] + x_out,
        out_shape=[jax.ShapeDtypeStruct((T, D), BF16), jax.ShapeDtypeStruct((T, D), BF16),
                   jax.ShapeDtypeStruct((T, D), BF16),
                   jax.ShapeDtypeStruct((N_HEADS, T, LANES), F32),
                   jax.ShapeDtypeStruct((N_PAIRS, nb, 2, tb), F32)] + x_shapes,
        scratch_shapes=[pltpu.VMEM((T, LANES), BF16), pltpu.VMEM((T, LANES), BF16)] + x_scratch,
        compiler_params=_cparams("arbitrary", "arbitrary"),
    )(proj, proj, proj, proj, cum4, o, dog, lp, *x_args)


def _flash_bwd_dkv(proj, cum4, do, lp, delta, *, tb, name):
    T = proj.shape[0]
    D = N_HEADS * HEAD_DIM
    nb = T // tb
    cb = D // LANES

    def body(q_ref, k_ref, v_ref, cum_ref, do_ref, lp_ref, dl_ref, dk_ref, dv_ref, dc_ref):
        kbi = pl.program_id(1)
        k = k_ref[...] * (HEAD_DIM ** -0.5)
        v = v_ref[...]
        masks = _head_masks(tb)
        kh = [jnp.where(masks[h], k, 0.0).astype(BF16) for h in range(2)]
        vh = [jnp.where(masks[h], v, 0.0).astype(BF16) for h in range(2)]
        ck = [cum_ref[0, h, pl.ds(kbi, 1), :] for h in range(2)]

        def step(i, carry, masked):
            q0 = pl.multiple_of(i * tb, tb)
            qb = q_ref[pl.ds(q0, tb), :].astype(BF16)
            dob = do_ref[pl.ds(q0, tb), :]
            new = []
            for h in range(2):
                dk, dv, dc = carry[h]
                s = lax.dot_general(qb, kh[h], (((1,), (1,)), ((), ())), preferred_element_type=F32)
                p = jnp.exp(s - ck[h] - lp_ref[h, pl.ds(q0, tb), :][:, 0:1])
                if masked:
                    p = jnp.where(_causal(i, kbi, tb, tb), p, 0.0)
                dp = lax.dot_general(dob, vh[h], (((1,), (1,)), ((), ())), preferred_element_type=F32)
                ds = p * (dp - dl_ref[h, pl.ds(q0, tb), :][:, 0:1])
                dv = dv + lax.dot_general(p.astype(BF16), dob, (((0,), (0,)), ((), ())),
                                          preferred_element_type=F32)
                dk = dk + lax.dot_general(ds.astype(BF16), qb, (((0,), (0,)), ((), ())),
                                          preferred_element_type=F32)
                new.append((dk, dv, dc - jnp.sum(ds, axis=0, keepdims=True)))
            return tuple(new)

        init1 = (jnp.zeros((tb, LANES), F32), jnp.zeros((tb, LANES), F32), jnp.zeros((1, tb), F32))
        carry = step(kbi, (init1, init1), True)
        carry = lax.fori_loop(kbi + 1, nb, lambda i, c: step(i, c, False), carry)
        dks, dvs = [], []
        for h, (dk, dv, dc) in enumerate(carry):
            dks.append(dk)
            dvs.append(dv)
            dc_ref[0, 0, pl.ds(h, 1), :] = dc
        dk_ref[...] = (jnp.where(masks[0], dks[0], dks[1]) * (HEAD_DIM ** -0.5)).astype(BF16)
        dv_ref[...] = jnp.where(masks[0], dvs[0], dvs[1]).astype(BF16)

    full = pl.BlockSpec((T, LANES), lambda j, i: (0, j))
    stat = pl.BlockSpec((2, T, LANES), lambda j, i: (j, 0, 0))
    blk = pl.BlockSpec((tb, LANES), lambda j, i: (i, j))
    return pl.pallas_call(
        body, name=name, grid=(N_PAIRS, nb),
        in_specs=[full,
                  pl.BlockSpec((tb, LANES), lambda j, i: (i, cb + j)),
                  pl.BlockSpec((tb, LANES), lambda j, i: (i, 2 * cb + j)),
                  pl.BlockSpec((1, 2, nb, tb), lambda j, i: (j, 0, 0, 0)),
                  full, stat, stat],
        out_specs=[blk, blk, pl.BlockSpec((1, 1, 2, tb), lambda j, i: (j, i, 0, 0))],
        out_shape=[jax.ShapeDtypeStruct((T, D), BF16), jax.ShapeDtypeStruct((T, D), BF16),
                   jax.ShapeDtypeStruct((N_PAIRS, nb, 2, tb), F32)],
        compiler_params=_cparams("parallel", "arbitrary"),
    )(proj, proj, proj, cum4, do, lp, delta)


def _flash_bwd(proj, cum4, o, dog, lp, *, tb, name, host=None):
    T = proj.shape[0]
    D = N_HEADS * HEAD_DIM
    nb = T // tb
    cb = D // LANES
    x_in, x_out, x_shapes, x_scratch, x_args = _host_specs(host)

    def body(q_ref, k_ref, v_ref, g_ref, cum_ref, o_ref, dog_ref, lp_ref,
             dq_ref, dg_ref, dk_ref, dv_ref, dcq_ref, dck_ref,
             kb_ref, vb_ref, dka_ref, dva_ref, dca_ref):
        i = pl.program_id(1)

        @pl.when(i == 0)
        def _():
            kb_ref[...] = k_ref[...].astype(BF16)
            vb_ref[...] = v_ref[...].astype(BF16)
            dka_ref[...] = jnp.zeros_like(dka_ref)
            dva_ref[...] = jnp.zeros_like(dva_ref)
            dca_ref[...] = jnp.zeros_like(dca_ref)

        gate = g_ref[...]
        sg = _sigmoid(gate)
        o = o_ref[...]
        dog = dog_ref[...]
        do = dog * (gate * sg)
        dg_ref[...] = (dog * o * (sg * (1.0 + gate * (1.0 - sg)))).astype(BF16)
        q = q_ref[...] * (HEAD_DIM ** -0.5)
        masks = _head_masks(tb)
        qh = [jnp.where(masks[h], q, 0.0).astype(BF16) for h in range(2)]
        doh = [jnp.where(masks[h], do, 0.0).astype(BF16) for h in range(2)]
        delta = [jnp.sum(jnp.where(masks[h], do * o, 0.0), axis=-1, keepdims=True) for h in range(2)]
        lph = [lp_ref[h][:, 0:1] for h in range(2)]

        def step(kbi, carry, masked):
            k0 = pl.multiple_of(kbi * tb, tb)
            kblk = kb_ref[pl.ds(k0, tb), :]
            vblk = vb_ref[pl.ds(k0, tb), :]
            new, dk, dv = [], None, None
            for h in range(2):
                acc, rs = carry[h]
                s = lax.dot_general(qh[h], kblk, (((1,), (1,)), ((), ())), preferred_element_type=F32)
                p = jnp.exp(s - cum_ref[0, h, pl.ds(kbi, 1), :] - lph[h])
                if masked:
                    p = jnp.where(_causal(i, kbi, tb, tb), p, 0.0)
                dp = lax.dot_general(doh[h], vblk, (((1,), (1,)), ((), ())),
                                     preferred_element_type=F32)
                ds = p * (dp - delta[h])
                pb, dsb = p.astype(BF16), ds.astype(BF16)
                dv_h = lax.dot_general(pb, doh[h], (((0,), (0,)), ((), ())),
                                       preferred_element_type=F32)
                dk_h = lax.dot_general(dsb, qh[h], (((0,), (0,)), ((), ())),
                                       preferred_element_type=F32)
                dv = dv_h if dv is None else dv + dv_h
                dk = dk_h if dk is None else dk + dk_h
                dca_ref[h, pl.ds(kbi, 1), :] -= jnp.sum(ds, axis=0, keepdims=True)
                new.append((acc + jnp.dot(dsb, kblk, preferred_element_type=F32),
                            rs + jnp.sum(ds, axis=-1, keepdims=True)))
            dka_ref[pl.ds(k0, tb), :] += dk
            dva_ref[pl.ds(k0, tb), :] += dv
            return tuple(new)

        init1 = (jnp.zeros((tb, LANES), F32), jnp.zeros((tb, 1), F32))
        carry = lax.fori_loop(0, i, lambda kbi, c: step(kbi, c, False), (init1, init1))
        dqs = []
        for h, (acc, rs) in enumerate(step(i, carry, True)):
            dqs.append(acc)
            dcq_ref[0, 0, pl.ds(h, 1), :] = jnp.broadcast_to(rs, (tb, LANES)).T[0:1, :]
        dq_ref[...] = (jnp.where(masks[0], dqs[0], dqs[1]) * (HEAD_DIM ** -0.5)).astype(BF16)

        @pl.when(i == nb - 1)
        def _():
            dk_ref[...] = dka_ref[...].astype(BF16)
            dv_ref[...] = dva_ref[...].astype(BF16)
            dck_ref[0] = dca_ref[...]

    blk = pl.BlockSpec((tb, LANES), lambda j, i: (i, j))
    full = pl.BlockSpec((T, LANES), lambda j, i: (0, j))
    body = _hosted(body, 8, 6, 5, host, (N_PAIRS, nb))
    return pl.pallas_call(
        body, name=name, grid=(N_PAIRS, nb),
        in_specs=[blk,
                  pl.BlockSpec((T, LANES), lambda j, i: (0, cb + j)),
                  pl.BlockSpec((T, LANES), lambda j, i: (0, 2 * cb + j)),
                  pl.BlockSpec((tb, LANES), lambda j, i: (i, 3 * cb + j)),
                  pl.BlockSpec((1, 2, nb, tb), lambda j, i: (j, 0, 0, 0)),
                  blk, blk, pl.BlockSpec((2, tb, LANES), lambda j, i: (j, i, 0))] + x_in,
        out_specs=[blk, blk, full, full,
                   pl.BlockSpec((1, 1, 2, tb), lambda j, i: (j, i, 0, 0)),
                   pl.BlockSpec((1, 2, nb, tb), lambda j, i: (j, 0, 0, 0))] + x_out,
        out_shape=[jax.ShapeDtypeStruct((T, D), BF16)] * 4
        + [jax.ShapeDtypeStruct((N_PAIRS, nb, 2, tb), F32),
           jax.ShapeDtypeStruct((N_PAIRS, 2, nb, tb), F32)] + x_shapes,
        scratch_shapes=[pltpu.VMEM((T, LANES), BF16), pltpu.VMEM((T, LANES), BF16),
                        pltpu.VMEM((T, LANES), F32), pltpu.VMEM((T, LANES), F32),
                        pltpu.VMEM((2, nb, tb), F32)] + x_scratch,
        compiler_params=_cparams("arbitrary", "arbitrary"),
    )(proj, proj, proj, proj, cum4, o, dog, lp, *x_args)


def _cumsum_fwd(proj, bf_row, *, tt, name):
    T = proj.shape[0]
    cb = (proj.shape[1] - LANES) // LANES

    def body(f_ref, b_ref, out_ref, carry_ref):
        i = pl.program_id(0)

        @pl.when(i == 0)
        def _():
            carry_ref[...] = jnp.zeros_like(carry_ref)

        ls = -_softplus(-(f_ref[...] + b_ref[...]))
        tri = (lax.broadcasted_iota(jnp.int32, (tt, tt), 0)
               >= lax.broadcasted_iota(jnp.int32, (tt, tt), 1)).astype(F32)
        cum = jnp.dot(tri, ls, preferred_element_type=F32,
                      precision=lax.Precision.HIGHEST) + carry_ref[...]
        carry_ref[...] = cum[tt - 1:tt, :]
        out_ref[...] = cum.T

    return pl.pallas_call(
        body, name=name, grid=(T // tt,),
        in_specs=[pl.BlockSpec((tt, LANES), lambda i: (i, cb)),
                  pl.BlockSpec((1, LANES), lambda i: (0, 0))],
        out_specs=pl.BlockSpec((LANES, tt), lambda i: (0, i)),
        out_shape=jax.ShapeDtypeStruct((LANES, T), F32),
        scratch_shapes=[pltpu.VMEM((1, LANES), F32)],
        compiler_params=_cparams("arbitrary"),
    )(proj, bf_row)


def _cumsum_bwd(dcum_t, proj, bf_row, *, tt, name):
    T = proj.shape[0]
    cb = (proj.shape[1] - LANES) // LANES
    nt = T // tt

    def body(dc_ref, f_ref, b_ref, df_ref, db_ref, carry_ref):
        i = pl.program_id(0)

        @pl.when(i == 0)
        def _():
            carry_ref[...] = jnp.zeros_like(carry_ref)
            db_ref[...] = jnp.zeros_like(db_ref)

        dc = dc_ref[...].T
        tri = (lax.broadcasted_iota(jnp.int32, (tt, tt), 0)
               <= lax.broadcasted_iota(jnp.int32, (tt, tt), 1)).astype(F32)
        rev = jnp.dot(tri, dc, preferred_element_type=F32,
                      precision=lax.Precision.HIGHEST) + carry_ref[...]
        carry_ref[...] = rev[0:1, :]
        df = rev * _sigmoid(-(f_ref[...] + b_ref[...]))
        df_ref[...] = df.astype(BF16)
        db_ref[...] += jnp.sum(df, axis=0, keepdims=True)

    return pl.pallas_call(
        body, name=name, grid=(nt,),
        in_specs=[pl.BlockSpec((LANES, tt), lambda i: (0, nt - 1 - i)),
                  pl.BlockSpec((tt, LANES), lambda i: (nt - 1 - i, cb)),
                  pl.BlockSpec((1, LANES), lambda i: (0, 0))],
        out_specs=[pl.BlockSpec((tt, LANES), lambda i: (nt - 1 - i, 0)),
                   pl.BlockSpec((1, LANES), lambda i: (0, 0))],
        out_shape=[jax.ShapeDtypeStruct((T, LANES), BF16), jax.ShapeDtypeStruct((1, LANES), F32)],
        scratch_shapes=[pltpu.VMEM((1, LANES), F32)],
        compiler_params=_cparams("arbitrary"),
    )(dcum_t, proj, bf_row)


def _rg_gates(upad_ref, small_ref, wa_ref, wi_ref, tt):
    off = SUBLANES - (CONV_WIDTH - 1)
    u = small_ref[4:5, :]
    for tap in range(CONV_WIDTH):
        u = u + upad_ref[off + tap:off + tap + tt, :] * small_ref[tap:tap + 1, :]
    pa, pi = [], []
    for n in range(RNN_BLOCKS):
        ub = u[:, n * RNN_BLOCK_WIDTH:(n + 1) * RNN_BLOCK_WIDTH].astype(BF16)
        pa.append(jnp.dot(ub, wa_ref[n], preferred_element_type=F32))
        pi.append(jnp.dot(ub, wi_ref[n], preferred_element_type=F32))
    r = _sigmoid(jnp.concatenate(pa, axis=-1) + small_ref[5:6, :])
    ig = _sigmoid(jnp.concatenate(pi, axis=-1) + small_ref[6:7, :])
    spl = _softplus(-small_ref[7:8, :])
    log_a = (-LRU_C) * r * spl
    a = jnp.exp(log_a)
    s = jnp.sqrt(jnp.tanh(-log_a) * (a * a + 1.0))
    return u, r, ig, spl, a, s


def _rg_fwd(proj, small, wa, wi, *, tt, name):
    T = proj.shape[0]
    D = RNN_BLOCKS * RNN_BLOCK_WIDTH
    hb = tt // SUBLANES

    def body(u0_ref, halo_ref, g_ref, small_ref, wa_ref, wi_ref, h_ref, y_ref,
             upad_ref, a_ref, b_ref, carry_ref):
        i = pl.program_id(0)

        @pl.when(i == 0)
        def _():
            carry_ref[...] = jnp.zeros_like(carry_ref)

        upad_ref[0:SUBLANES, :] = jnp.where(i == 0, 0.0, halo_ref[...])
        upad_ref[SUBLANES:, :] = u0_ref[...]
        u, r, ig, spl, a, s = _rg_gates(upad_ref, small_ref, wa_ref, wi_ref, tt)
        a_ref[...] = a
        b_ref[...] = s * (ig * u)

        def row(t, h):
            h = a_ref[pl.ds(t, 1), :] * h + b_ref[pl.ds(t, 1), :]
            h_ref[pl.ds(t, 1), :] = h
            return h

        carry_ref[...] = lax.fori_loop(0, tt, row, carry_ref[...], unroll=8)
        gate = g_ref[...]
        y_ref[...] = (h_ref[...] * (gate * _sigmoid(gate))).astype(BF16)

    return pl.pallas_call(
        body, name=name, grid=(T // tt,),
        in_specs=[pl.BlockSpec((tt, D), lambda i: (i, 0)),
                  pl.BlockSpec((SUBLANES, D), lambda i: (jnp.maximum(i * hb - 1, 0), 0)),
                  pl.BlockSpec((tt, D), lambda i: (i, 1)),
                  pl.BlockSpec((SUBLANES, D), lambda i: (0, 0)),
                  pl.BlockSpec((RNN_BLOCKS, RNN_BLOCK_WIDTH, RNN_BLOCK_WIDTH), lambda i: (0, 0, 0)),
                  pl.BlockSpec((RNN_BLOCKS, RNN_BLOCK_WIDTH, RNN_BLOCK_WIDTH), lambda i: (0, 0, 0))],
        out_specs=[pl.BlockSpec((tt, D), lambda i: (i, 0)), pl.BlockSpec((tt, D), lambda i: (i, 0))],
        out_shape=[jax.ShapeDtypeStruct((T, D), F32), jax.ShapeDtypeStruct((T, D), BF16)],
        scratch_shapes=[pltpu.VMEM((tt + SUBLANES, D), F32), pltpu.VMEM((tt, D), F32),
                        pltpu.VMEM((tt, D), F32), pltpu.VMEM((1, D), F32)],
        compiler_params=_cparams("arbitrary"),
    )(proj, proj, proj, small, wa, wi)


def _rg_bwd(proj, hs, dy, small, wa, wi, *, tt, name):
    T = proj.shape[0]
    D = RNN_BLOCKS * RNN_BLOCK_WIDTH
    W = RNN_BLOCK_WIDTH
    hb = tt // SUBLANES
    nt = T // tt

    def body(u0_ref, uhalo_ref, g_ref, h_ref, hhalo_ref, dy_ref, small_ref, wa_ref, wi_ref,
             dp_ref, dwa_ref, dwi_ref, ds_ref,
             upad_ref, hpad_ref, a_ref, g_s_ref, duext_ref, carry_ref):
        i = pl.program_id(0)
        first_chunk = i == nt - 1

        @pl.when(i == 0)
        def _():
            carry_ref[...] = jnp.zeros_like(carry_ref)
            duext_ref[...] = jnp.zeros_like(duext_ref)
            dwa_ref[...] = jnp.zeros_like(dwa_ref)
            dwi_ref[...] = jnp.zeros_like(dwi_ref)
            ds_ref[...] = jnp.zeros_like(ds_ref)

        upad_ref[0:SUBLANES, :] = jnp.where(first_chunk, 0.0, uhalo_ref[...])
        upad_ref[SUBLANES:, :] = u0_ref[...]
        hpad_ref[0:SUBLANES, :] = jnp.where(first_chunk, 0.0, hhalo_ref[...])
        hpad_ref[SUBLANES:, :] = h_ref[...]
        u, r, ig, spl, a, s = _rg_gates(upad_ref, small_ref, wa_ref, wi_ref, tt)
        gate = g_ref[...]
        sg = _sigmoid(gate)
        dy = dy_ref[...]
        dp_ref[:, D:] = (dy * h_ref[...] * (sg * (1.0 + gate * (1.0 - sg)))).astype(BF16)
        a_ref[...] = a
        g_s_ref[...] = dy * (gate * sg)

        def row(k, c):
            t = tt - 1 - k
            g = g_s_ref[pl.ds(t, 1), :] + c
            g_s_ref[pl.ds(t, 1), :] = g
            return a_ref[pl.ds(t, 1), :] * g

        carry_ref[...] = lax.fori_loop(0, tt, row, carry_ref[...], unroll=8)
        g = g_s_ref[...]
        h_prev = hpad_ref[SUBLANES - 1:SUBLANES - 1 + tt, :]
        iu = ig * u
        d_iu = g * s
        dlog_a = (g * h_prev) * a - (g * iu) * (a * a) / s
        dpre_a = (dlog_a * ((-LRU_C) * spl)) * r * (1.0 - r)
        dpre_i = (d_iu * u) * ig * (1.0 - ig)
        dlam = jnp.sum(dlog_a * r, axis=0, keepdims=True) * (LRU_C * _sigmoid(-small_ref[7:8, :]))
        du_parts = []
        for n in range(RNN_BLOCKS):
            sl = slice(n * W, (n + 1) * W)
            ub = u[:, sl].astype(BF16)
            da_n = dpre_a[:, sl].astype(BF16)
            di_n = dpre_i[:, sl].astype(BF16)
            dwa_ref[n] += lax.dot_general(ub, da_n, (((0,), (0,)), ((), ())),
                                          preferred_element_type=F32)
            dwi_ref[n] += lax.dot_general(ub, di_n, (((0,), (0,)), ((), ())),
                                          preferred_element_type=F32)
            du_parts.append(
                lax.dot_general(da_n, wa_ref[n], (((1,), (1,)), ((), ())), preferred_element_type=F32)
                + lax.dot_general(di_n, wi_ref[n], (((1,), (1,)), ((), ())), preferred_element_type=F32))
        du = d_iu * ig + jnp.concatenate(du_parts, axis=-1)
        off = SUBLANES - (CONV_WIDTH - 1)
        for tap in range(CONV_WIDTH):
            ds_ref[tap:tap + 1, :] += jnp.sum(du * upad_ref[off + tap:off + tap + tt, :],
                                              axis=0, keepdims=True)
        ds_ref[4:5, :] += jnp.sum(du, axis=0, keepdims=True)
        ds_ref[5:6, :] += jnp.sum(dpre_a, axis=0, keepdims=True)
        ds_ref[6:7, :] += jnp.sum(dpre_i, axis=0, keepdims=True)
        ds_ref[7:8, :] += dlam
        duext_ref[0:tt, :] = du
        du0 = jnp.zeros((tt, D), F32)
        for tap in range(CONV_WIDTH):
            sh = CONV_WIDTH - 1 - tap
            du0 = du0 + duext_ref[sh:sh + tt, :] * small_ref[tap:tap + 1, :]
        dp_ref[:, :D] = du0.astype(BF16)
        duext_ref[tt:, :] = du[0:SUBLANES, :]

    rev = lambda i: nt - 1 - i
    wspec = pl.BlockSpec((RNN_BLOCKS, W, W), lambda i: (0, 0, 0))
    return pl.pallas_call(
        body, name=name, grid=(nt,),
        in_specs=[pl.BlockSpec((tt, D), lambda i: (rev(i), 0)),
                  pl.BlockSpec((SUBLANES, D), lambda i: (jnp.maximum(rev(i) * hb - 1, 0), 0)),
                  pl.BlockSpec((tt, D), lambda i: (rev(i), 1)),
                  pl.BlockSpec((tt, D), lambda i: (rev(i), 0)),
                  pl.BlockSpec((SUBLANES, D), lambda i: (jnp.maximum(rev(i) * hb - 1, 0), 0)),
                  pl.BlockSpec((tt, D), lambda i: (rev(i), 0)),
                  pl.BlockSpec((SUBLANES, D), lambda i: (0, 0)),
                  wspec, wspec],
        out_specs=[pl.BlockSpec((tt, 2 * D), lambda i: (rev(i), 0)),
                   wspec, wspec, pl.BlockSpec((SUBLANES, D), lambda i: (0, 0))],
        out_shape=[jax.ShapeDtypeStruct((T, 2 * D), BF16),
                   jax.ShapeDtypeStruct((RNN_BLOCKS, W, W), F32),
                   jax.ShapeDtypeStruct((RNN_BLOCKS, W, W), F32),
                   jax.ShapeDtypeStruct((SUBLANES, D), F32)],
        scratch_shapes=[pltpu.VMEM((tt + SUBLANES, D), F32), pltpu.VMEM((tt + SUBLANES, D), F32),
                        pltpu.VMEM((tt, D), F32), pltpu.VMEM((tt, D), F32),
                        pltpu.VMEM((tt + SUBLANES, D), F32), pltpu.VMEM((1, D), F32)],
        compiler_params=_cparams("arbitrary"),
    )(proj, proj, proj, hs, hs, dy, small, wa, wi)


def _ln_fwd(x, h, g, b, *, tt, name):
    T, D = x.shape

    def body(x_ref, h_ref, g_ref, b_ref, y_ref, yb_ref, zh_ref, rs_ref):
        z = ALPHA * x_ref[...] + h_ref[...]
        mu = jnp.mean(z, axis=-1, keepdims=True)
        zc = z - mu
        rstd = lax.rsqrt(jnp.mean(zc * zc, axis=-1, keepdims=True) + LN_EPS)
        zh = zc * rstd
        zh_ref[...] = zh
        rs_ref[...] = rstd
        y = zh * g_ref[...] + b_ref[...]
        y_ref[...] = y
        yb_ref[...] = y.astype(BF16)

    blk = pl.BlockSpec((tt, D), lambda i: (i, 0))
    row = pl.BlockSpec((1, D), lambda i: (0, 0))
    return pl.pallas_call(
        body, name=name, grid=(T // tt,),
        in_specs=[blk, blk, row, row],
        out_specs=[blk, blk, blk, pl.BlockSpec((tt, 1), lambda i: (i, 0))],
        out_shape=[jax.ShapeDtypeStruct((T, D), F32), jax.ShapeDtypeStruct((T, D), BF16),
                   jax.ShapeDtypeStruct((T, D), F32), jax.ShapeDtypeStruct((T, 1), F32)],
        compiler_params=_cparams("parallel"),
    )(x, h, g, b)


def _ln_bwd(dy, zh, rstd, g, *, tt, name):
    T, D = dy.shape

    def body(dy_ref, zh_ref, rs_ref, g_ref, dz_ref, dzb_ref, dg_ref, db_ref):
        @pl.when(pl.program_id(0) == 0)
        def _():
            dg_ref[...] = jnp.zeros_like(dg_ref)
            db_ref[...] = jnp.zeros_like(db_ref)

        dy = dy_ref[...]
        zh = zh_ref[...]
        dg_ref[...] += jnp.sum(dy * zh, axis=0, keepdims=True)
        db_ref[...] += jnp.sum(dy, axis=0, keepdims=True)
        dzh = dy * g_ref[...]
        m1 = jnp.mean(dzh, axis=-1, keepdims=True)
        m2 = jnp.mean(dzh * zh, axis=-1, keepdims=True)
        dz = rs_ref[...] * (dzh - m1 - zh * m2)
        dz_ref[...] = dz
        dzb_ref[...] = dz.astype(BF16)

    blk = pl.BlockSpec((tt, D), lambda i: (i, 0))
    row = pl.BlockSpec((1, D), lambda i: (0, 0))
    return pl.pallas_call(
        body, name=name, grid=(T // tt,),
        in_specs=[blk, blk, pl.BlockSpec((tt, 1), lambda i: (i, 0)), row],
        out_specs=[blk, blk, row, row],
        out_shape=[jax.ShapeDtypeStruct((T, D), F32), jax.ShapeDtypeStruct((T, D), BF16),
                   jax.ShapeDtypeStruct((1, D), F32), jax.ShapeDtypeStruct((1, D), F32)],
        compiler_params=_cparams("arbitrary"),
    )(dy, zh, rstd, g)


def _loss(y, tgt, *, tt, name):
    T, D = y.shape

    def body(y_ref, t_ref, l_ref, dy_ref):
        @pl.when(pl.program_id(0) == 0)
        def _():
            l_ref[...] = jnp.zeros_like(l_ref)

        e = y_ref[...] - t_ref[...]
        dy_ref[...] = e * (1.0 / D)
        l_ref[...] += jnp.sum(e * e, axis=0, keepdims=True) * (0.5 / D)

    blk = pl.BlockSpec((tt, D), lambda i: (i, 0))
    return pl.pallas_call(
        body, name=name, grid=(T // tt,),
        in_specs=[blk, blk], out_specs=[pl.BlockSpec((1, D), lambda i: (0, 0)), blk],
        out_shape=[jax.ShapeDtypeStruct((1, D), F32), jax.ShapeDtypeStruct((T, D), F32)],
        compiler_params=_cparams("arbitrary"),
    )(y, tgt)


def _row_tile(rows, target):
    best = SUBLANES
    for t in range(SUBLANES, target + 1, SUBLANES):
        if rows % t == 0:
            best = t
    return best


def _add_own(g, recv, c_idx, *, tr, name, narrow=False):
    _, M, R, C = g.shape

    def body(c_ref, g_ref, r_ref, *o_refs):
        s = g_ref[0] + r_ref[...]
        for o_ref in o_refs:
            o_ref[...] = s.astype(o_ref.dtype)

    blk = pl.BlockSpec((1, tr, C), lambda k, i, c: (k, i, 0))
    dtypes = [F32, BF16] if narrow else [F32]
    outs = pl.pallas_call(
        body, name=name,
        grid_spec=pltpu.PrefetchScalarGridSpec(
            num_scalar_prefetch=1, grid=(M, R // tr),
            in_specs=[pl.BlockSpec((1, 1, tr, C), lambda k, i, c: (c[0], k, i, 0)), blk],
            out_specs=[blk] * len(dtypes)),
        out_shape=[jax.ShapeDtypeStruct((M, R, C), d) for d in dtypes],
        compiler_params=_cparams("parallel", "parallel"),
    )(c_idx, g, recv)
    return outs if narrow else outs[0]


def _adamw_math(g, w_ref, m_ref, v_ref, g_ref, d_ref, nm_ref, nv_ref):
    nm = ADAM_B1 * m_ref[...] + (1.0 - ADAM_B1) * g
    nv = ADAM_B2 * v_ref[...] + (1.0 - ADAM_B2) * (g * g)
    m_hat = nm / (1.0 - ADAM_B1 ** ADAM_STEP)
    v_hat = nv / (1.0 - ADAM_B2 ** ADAM_STEP)
    g_ref[...] = g
    nm_ref[...] = nm
    nv_ref[...] = nv
    d_ref[...] = (-ADAM_LR) * (m_hat / (jnp.sqrt(v_hat) + ADAM_EPS) + ADAM_WD * w_ref[...])


def _adamw(parts, w, m, v, *, tr, name):
    n, R, C = parts.shape
    tr = min(tr, R)

    def body(p_ref, w_ref, m_ref, v_ref, *out_refs):
        g = p_ref[0]
        for k in range(1, n):
            g = g + p_ref[k]
        _adamw_math(g, w_ref, m_ref, v_ref, *out_refs)

    blk = pl.BlockSpec((tr, C), lambda i: (i, 0))
    out = jax.ShapeDtypeStruct((R, C), F32)
    return pl.pallas_call(
        body, name=name, grid=(R // tr,),
        in_specs=[pl.BlockSpec((n, tr, C), lambda i: (0, i, 0)), blk, blk, blk],
        out_specs=[blk, blk, blk, blk], out_shape=[out, out, out, out],
        compiler_params=_cparams("parallel"),
    )(parts, w, m, v)


def _adamw_shard(h, recv, me_idx, w, m, v, *, idx, prev, tr, name):
    _, _, R, C = h.shape
    n_prev = 0 if prev is None else 4

    def body(me_ref, h_ref, r1_ref, r2_ref, r3_ref, w_ref, m_ref, v_ref, *rest):
        g = ((h_ref[0] + r1_ref[0].astype(F32)) + r2_ref[0].astype(F32)) + r3_ref[0].astype(F32)
        _adamw_math(g, w_ref, m_ref, v_ref, *rest[n_prev:])

    blk = pl.BlockSpec((1, tr, C), lambda i, me: (idx, i, 0))

    def slot(d):
        return pl.BlockSpec((1, 1, tr, C), lambda i, me: (me[0] ^ d, 0, i, 0))

    out = jax.ShapeDtypeStruct(w.shape, F32)
    return pl.pallas_call(
        body, name=name,
        grid_spec=pltpu.PrefetchScalarGridSpec(
            num_scalar_prefetch=1, grid=(R // tr,),
            in_specs=[slot(0), slot(1), slot(2), slot(3), blk, blk, blk]
            + [pl.BlockSpec(memory_space=pl.ANY)] * n_prev,
            out_specs=[blk, blk, blk, blk]),
        out_shape=[out, out, out, out],
        input_output_aliases={8 + j: j for j in range(n_prev)},
        compiler_params=_cparams("parallel"),
    )(me_idx, h, recv, recv, recv, w, m, v, *(prev or ()))


SHARD_AXIS = dict(attn_w_in=1, attn_w_out=0, rnn_w_in=1, rnn_w_out=0, rnn_w_a=1, rnn_w_i=1,
                  rnn_conv_w=1, rnn_conv_b=0, rnn_b_a=0, rnn_b_i=0, rnn_lambda=0)
RNN_ROWED = ("rnn_w_out", "rnn_w_a", "rnn_w_i")
SMALL = ("rnn_conv_w", "rnn_conv_b", "rnn_b_a", "rnn_b_i", "rnn_lambda")
PACK_C = 1024


def _elems(shape):
    n = 1
    for s in shape:
        n *= s
    return n


def _pack_rows(p, idx, dtype):
    parts = [p[k][idx].astype(dtype).reshape(-1, PACK_C) for k in RNN_ROWED]
    small = jnp.concatenate([p[k][idx].reshape(-1) for k in SMALL])
    tile_rows = SUBLANES * (4 // jnp.dtype(dtype).itemsize)
    if dtype == BF16:
        small = lax.bitcast_convert_type(small, BF16)
    small = small.reshape(-1, PACK_C)
    parts.append(jnp.pad(small, ((0, tile_rows - small.shape[0]), (0, 0))))
    return jnp.concatenate(parts, axis=0)


def _unpack_rows(flat, shapes):
    out, r = {}, 0
    for k in RNN_ROWED:
        n = _elems(shapes[k]) // PACK_C
        out[k] = flat[r:r + n].reshape(shapes[k])
        r += n
    n_small = sum(_elems(shapes[k]) for k in SMALL)
    small = flat[r:r + n_small // PACK_C].reshape(-1)
    o = 0
    for k in SMALL:
        n = _elems(shapes[k])
        out[k] = small[o:o + n].reshape(shapes[k])
        o += n
    return out


def _join_columns(g, width, *, tr, name):
    _, _, R, S = g.shape

    def body(*refs):
        o_ref = refs[8]
        parts = [refs[r][0, 0].astype(F32) for r in range(8)]
        parts.append(jnp.zeros((tr, width - 8 * S), F32))
        o_ref[...] = jnp.concatenate(parts, axis=-1).astype(o_ref.dtype)

    def shard(r):
        return pl.BlockSpec((1, 1, tr, S), lambda i: (r % 2, r // 2, i, 0))

    return pl.pallas_call(
        body, name=name, grid=(R // tr,),
        in_specs=[shard(r) for r in range(8)],
        out_specs=pl.BlockSpec((tr, width), lambda i: (i, 0)),
        out_shape=jax.ShapeDtypeStruct((R, width), g.dtype),
        compiler_params=_cparams("parallel"),
    )(*([g] * 8))


def _split_columns(parts, S, *, tr, name):
    R = parts[0].shape[0]
    n = len(parts)

    def body(*refs):
        o_ref = refs[n]
        x = jnp.concatenate([r[...] for r in refs[:n]], axis=1)
        for r in range(8):
            o_ref[r % 2, r // 2] = x[:, r * S:(r + 1) * S]

    return pl.pallas_call(
        body, name=name, grid=(R // tr,),
        in_specs=[pl.BlockSpec((tr, p.shape[1]), lambda i: (i, 0)) for p in parts],
        out_specs=pl.BlockSpec((2, 4, tr, S), lambda i: (0, 0, i, 0)),
        out_shape=jax.ShapeDtypeStruct((2, 4, R, S), parts[0].dtype),
        compiler_params=_cparams("parallel"),
    )(*parts)


def _to_full(g, k, sh):
    ax, nd = SHARD_AXIS[k], len(sh)
    perm = tuple(range(2, 2 + ax)) + (1, 0) + tuple(range(2 + ax, 2 + nd))
    return g.transpose(perm).reshape(sh[:ax] + (8 * sh[ax],) + sh[ax + 1:])


def _from_full(full, k, sh):
    ax, nd = SHARD_AXIS[k], len(sh)
    t = full.reshape(sh[:ax] + (4, 2, sh[ax]) + sh[ax + 1:])
    return t.transpose((ax + 1, ax) + tuple(range(ax)) + tuple(range(ax + 2, nd + 2)))


def _unpack_gathered_rows(g, shapes):
    out, r = {}, 0
    for k in RNN_ROWED:
        n = _elems(shapes[k]) // PACK_C
        out[k] = _to_full(g[:, :, r:r + n].reshape((2, 4) + shapes[k]), k, shapes[k])
        r += n
    n_small = sum(_elems(shapes[k]) for k in SMALL)
    nr = 2 * n_small // PACK_C
    small = lax.bitcast_convert_type(g[:, :, r:r + nr].reshape(2, 4, n_small, 2), F32)
    o = 0
    for k in SMALL:
        n = _elems(shapes[k])
        out[k] = _to_full(small[:, :, o:o + n].reshape((2, 4) + shapes[k]), k, shapes[k])
        o += n
    return out


def _pack_grad_rows(full, shapes):
    parts = [_from_full(full[k], k, shapes[k]).reshape(2, 4, -1, PACK_C) for k in RNN_ROWED]
    small = jnp.concatenate(
        [_from_full(full[k], k, shapes[k]).reshape(2, 4, -1) for k in SMALL], axis=-1)
    small = small.reshape(2, 4, -1, PACK_C)
    parts.append(jnp.pad(small, ((0, 0), (0, 0), (0, SUBLANES - small.shape[2]), (0, 0))))
    return jnp.concatenate(parts, axis=2)


def kernel(x, ln_g, ln_b, attn_w_in, attn_b_f, attn_w_out, rnn_w_in, rnn_conv_w, rnn_conv_b, rnn_w_a, rnn_b_a, rnn_w_i, rnn_b_i, rnn_lambda, rnn_w_out, loss_target, m_ln_g, m_ln_b, m_attn_w_in, m_attn_b_f, m_attn_w_out, m_rnn_w_in, m_rnn_conv_w, m_rnn_conv_b, m_rnn_w_a, m_rnn_b_a, m_rnn_w_i, m_rnn_b_i, m_rnn_lambda, m_rnn_w_out, v_ln_g, v_ln_b, v_attn_w_in, v_attn_b_f, v_attn_w_out, v_rnn_w_in, v_rnn_conv_w, v_rnn_conv_b, v_rnn_w_a, v_rnn_b_a, v_rnn_w_i, v_rnn_b_i, v_rnn_lambda, v_rnn_w_out):
    w_loc = dict(attn_w_in=attn_w_in, attn_w_out=attn_w_out, rnn_w_in=rnn_w_in, rnn_w_a=rnn_w_a,
                 rnn_w_i=rnn_w_i, rnn_w_out=rnn_w_out, rnn_conv_w=rnn_conv_w, rnn_conv_b=rnn_conv_b,
                 rnn_b_a=rnn_b_a, rnn_b_i=rnn_b_i, rnn_lambda=rnn_lambda)
    m_loc = dict(attn_w_in=m_attn_w_in, attn_w_out=m_attn_w_out, rnn_w_in=m_rnn_w_in,
                 rnn_w_a=m_rnn_w_a, rnn_w_i=m_rnn_w_i, rnn_w_out=m_rnn_w_out,
                 rnn_conv_w=m_rnn_conv_w, rnn_conv_b=m_rnn_conv_b, rnn_b_a=m_rnn_b_a,
                 rnn_b_i=m_rnn_b_i, rnn_lambda=m_rnn_lambda)
    v_loc = dict(attn_w_in=v_attn_w_in, attn_w_out=v_attn_w_out, rnn_w_in=v_rnn_w_in,
                 rnn_w_a=v_rnn_w_a, rnn_w_i=v_rnn_w_i, rnn_w_out=v_rnn_w_out,
                 rnn_conv_w=v_rnn_conv_w, rnn_conv_b=v_rnn_conv_b, rnn_b_a=v_rnn_b_a,
                 rnn_b_i=v_rnn_b_i, rnn_lambda=v_rnn_lambda)
    shapes = {k: tuple(a.shape[1:]) for k, a in w_loc.items()}
    T, D = x.shape[1], x.shape[2]
    n_f = attn_b_f.shape[1]
    tb = min(1024, T)
    tb_bwd = min(512, T)
    tt_rg = min(128, T)
    tt_ln = min(256, T)
    c_idx = lax.axis_index("c").astype(jnp.int32).reshape(1)
    me_idx = (2 * lax.axis_index("x") + lax.axis_index("y")).astype(jnp.int32).reshape(1)

    def attn_w_in_full(g_in, idx):
        return _join_columns(g_in, 4 * D + LANES, tr=256, name=f"a_join{idx}")

    def attn_w_out_full(g_out):
        return _to_full(g_out, "attn_w_out", shapes["attn_w_out"])

    def rnn_weights(g_in, g_rows):
        w = _unpack_gathered_rows(g_rows, shapes)
        w["rnn_w_in"] = _to_full(g_in, "rnn_w_in", shapes["rnn_w_in"])
        w["small"] = jnp.concatenate([w["rnn_conv_w"], w["rnn_conv_b"][None], w["rnn_b_a"][None],
                                      w["rnn_b_i"][None], w["rnn_lambda"][None]])
        return w

    g0 = _ag_c(_run_exchange(_Exchange("gather", [attn_w_in[0].astype(BF16)]), "ag_w0_xy"),
               "ag_w0_c")
    later = _Exchange("gather", [
        attn_w_out.astype(BF16), attn_w_in[1].astype(BF16), rnn_w_in.astype(BF16),
        jnp.stack([_pack_rows(w_loc, i, BF16) for i in range(2)])])
    w_attn_in, w_attn_out, w_rnn = [attn_w_in_full(g0[0], 0), None], [None, None], [None, None]
    bf_rows = jnp.pad(attn_b_f, ((0, 0), (0, LANES - n_f)))[:, None, :]

    xs, xb, saved = [x[0]], [x[0]], []
    for layer in range(DEPTH):
        idx, xl, xm = layer // 2, xs[-1], xb[-1]
        if layer % 2 == 0:
            proj = _matmul(xm, w_attn_in[idx], trans_b=False, tm=512, tn=1408,
                           name=f"a_proj{layer}")
            cum_t = _cumsum_fwd(proj, bf_rows[idx], tt=min(512, T), name=f"a_cum{layer}")
            cum2 = cum_t[:N_HEADS].reshape(N_PAIRS, 2, T)
            o, og, lp, *got = _flash_fwd(proj, cum2.reshape(N_PAIRS, 2, T // tb, tb), tb=tb,
                                         name=f"a_fwd{layer}", host=later if layer == 0 else None)
            cum4 = cum2.reshape(N_PAIRS, 2, T // tb_bwd, tb_bwd)
            if layer == 0:
                g1 = _ag_c(got, "ag_w1_c")
                w_attn_out = [attn_w_out_full(g1[0][:, :, i]) for i in range(2)]
                w_attn_in[1] = attn_w_in_full(g1[1], 1)
                w_rnn = [rnn_weights(g1[2][:, :, i], g1[3][:, :, i]) for i in range(2)]
            hbr = _matmul(og, w_attn_out[idx], trans_b=False, tm=512, tn=1024,
                          name=f"a_out{layer}")
            saved.append((proj, cum4, o, og, lp))
        else:
            w = w_rnn[idx]
            proj = _matmul(xm, w["rnn_w_in"], trans_b=False, tm=512, tn=1024,
                           name=f"r_proj{layer}")
            hs, yr = _rg_fwd(proj, w["small"], w["rnn_w_a"], w["rnn_w_i"], tt=tt_rg,
                             name=f"r_fwd{layer}")
            hbr = _matmul(yr, w["rnn_w_out"], trans_b=False, tm=512, tn=1024,
                          name=f"r_out{layer}")
            saved.append((proj, hs, yr))
        y, yb, zh, rstd = _ln_fwd(xl, hbr, ln_g[layer][None], ln_b[layer][None], tt=tt_ln,
                                  name=f"ln_fwd{layer}")
        saved[-1] = saved[-1] + (zh, rstd)
        xs.append(y)
        xb.append(yb)

    loss_lanes, dy = _loss(xs[-1], loss_target[0], tt=tt_ln, name="loss")
    loss = lax.psum(jnp.sum(loss_lanes), ("x", "y", "c"))

    def reduce_pair(gs, layer, narrow=False):
        recv = _rs_c(gs, f"rs_c{layer}")
        outs = [_add_own(g, r, c_idx, tr=_row_tile(g.shape[2], 512), name=f"rs_add{layer}_{n}",
                         narrow=narrow) for n, (g, r) in enumerate(zip(gs, recv))]
        if narrow:
            return [o[0][:, None] for o in outs], [o[1][:, None] for o in outs]
        return [o[:, None] for o in outs]

    half, quad = [None] * DEPTH, [None] * DEPTH
    d_ln_g, d_ln_b, d_bf = [None] * DEPTH, [None] * DEPTH, [None, None]
    for layer in reversed(range(DEPTH)):
        idx, xm = layer // 2, xb[layer]
        zh, rstd = saved[layer][-2:]
        dz, dzb, dg, db = _ln_bwd(dy, zh, rstd, ln_g[layer][None], tt=tt_ln,
                                  name=f"ln_bwd{layer}")
        d_ln_g[layer], d_ln_b[layer] = dg[0], db[0]
        if layer % 2 == 0:
            w_in, w_out = w_attn_in[idx], w_attn_out[idx]
            proj, cum4, o, og, lp = saved[layer][:5]
            dog = _matmul(dzb, w_out, trans_b=True, tm=512, tn=1024, name=f"a_dog{layer}")
            dwo = _matmul_tn(og, dzb, tm=512, tn=1024, tk=512, name=f"a_dwo{layer}")
            riders = [l for l in range(layer + 1, DEPTH) if quad[l] is None]
            host = _Exchange("scatter", [h for l in riders for h in half[l]]) if riders else None
            dq, dgate, dk, dv, dcum_q, dcum_k, *got = _flash_bwd(proj, cum4, o, dog, lp, tb=tb_bwd,
                                                                 name=f"a_bwd{layer}", host=host)
            for l in riders:
                quad[l], got = got[:len(half[l])], got[len(half[l]):]
            dcum_t = (dcum_q.transpose(0, 2, 1, 3) + dcum_k).reshape(N_HEADS, T)
            dcum_t = jnp.pad(dcum_t, ((0, LANES - N_HEADS), (0, 0)))
            df, dbf = _cumsum_bwd(dcum_t, proj, bf_rows[idx], tt=min(512, T), name=f"a_dcum{layer}")
            d_bf[idx] = dbf[0, :n_f]
            dproj = [dq, dk, dv, dgate, df]
            dwi = _matmul_tn_parts(xm, dproj, tm=512, tk=512, name=f"a_dwi{layer}")
            gs = [_split_columns(dwi, shapes["attn_w_in"][1], tr=256, name=f"a_split{layer}"),
                  _from_full(dwo, "attn_w_out", shapes["attn_w_out"])]
            if layer > 0:
                half[layer] = reduce_pair(gs, layer)
                dy = _matmul(dproj, w_in, trans_b=True, tm=512, tn=1024, name=f"a_dx{layer}",
                             add=dz, add_scale=ALPHA)
            else:
                half[layer], narrow = reduce_pair(gs, layer, narrow=True)
                dy, *quad[layer] = _matmul(dproj, w_in, trans_b=True, tm=512, tn=1024,
                                           name=f"a_dx{layer}", add=dz, add_scale=ALPHA,
                                           host=_Exchange("scatter", narrow))
        else:
            w = w_rnn[idx]
            proj, hs, yr = saved[layer][:3]
            dyr = _matmul(dzb, w["rnn_w_out"], trans_b=True, tm=512, tn=1024, name=f"r_dy{layer}")
            dwo = _matmul_tn(yr, dzb, tm=512, tn=1024, tk=512, name=f"r_dwo{layer}")
            dproj, dwa, dwi_, dsm = _rg_bwd(proj, hs, dyr, w["small"], w["rnn_w_a"], w["rnn_w_i"],
                                            tt=tt_rg, name=f"r_bwd{layer}")
            dwin = _matmul_tn(xm, dproj, tm=512, tn=1024, tk=512, name=f"r_dwi{layer}")
            dy = _matmul(dproj, w["rnn_w_in"], trans_b=True, tm=512, tn=1024, name=f"r_dx{layer}",
                         add=dz, add_scale=ALPHA)
            full = dict(rnn_w_out=dwo, rnn_w_a=dwa, rnn_w_i=dwi_, rnn_conv_w=dsm[0:4],
                        rnn_conv_b=dsm[4], rnn_b_a=dsm[5], rnn_b_i=dsm[6], rnn_lambda=dsm[7])
            half[layer] = reduce_pair([_from_full(dwin, "rnn_w_in", shapes["rnn_w_in"]),
                                       _pack_grad_rows(full, shapes)], layer)
    grad_x = dy[None]

    def update(k, n):
        res = None
        for idx in (1, 0):
            layer = 2 * idx + (0 if k.startswith("attn") else 1)
            res = _adamw_shard(half[layer][n], quad[layer][n], me_idx, w_loc[k], m_loc[k], v_loc[k],
                               idx=idx, prev=res, tr=_row_tile(shapes[k][0], 256),
                               name=f"adamw_{k}{idx}")
        return res

    shard_outs = [dict() for _ in range(4)]
    for k, n in (("attn_w_in", 0), ("attn_w_out", 1), ("rnn_w_in", 0)):
        for j, a in enumerate(update(k, n)):
            shard_outs[j][k] = a
    rows = []
    for idx in range(2):
        layer = 2 * idx + 1
        wmv = [_pack_rows(d, idx, F32)[None] for d in (w_loc, m_loc, v_loc)]
        res = _adamw_shard(half[layer][1], quad[layer][1], me_idx, *wmv, idx=0, prev=None,
                           tr=_row_tile(wmv[0].shape[1], 256), name=f"adamw_rows{idx}")
        rows.append([_unpack_rows(a[0], shapes) for a in res])
    for j in range(4):
        for k in RNN_ROWED + SMALL:
            shard_outs[j][k] = jnp.stack([rows[0][j][k], rows[1][j][k]])
    g_sh, d_sh, nm_sh, nv_sh = shard_outs

    def rep_pack(lg, lb, bf):
        rows = jnp.concatenate([lg, lb, jnp.pad(bf.reshape(1, -1), ((0, 0), (0, D - 2 * n_f)))])
        return jnp.pad(rows, ((0, 16 - rows.shape[0]), (0, 0)))

    rep = _all_gather(rep_pack(jnp.stack(d_ln_g), jnp.stack(d_ln_b), jnp.stack(d_bf)), "ag_rep")
    rg, rd, rm, rv = _adamw(rep.reshape(8, 16, D), rep_pack(ln_g, ln_b, attn_b_f),
                            rep_pack(m_ln_g, m_ln_b, m_attn_b_f),
                            rep_pack(v_ln_g, v_ln_b, v_attn_b_f), tr=16, name="adamw_rep")

    def rep_unpack(a):
        return dict(ln_g=a[0:DEPTH], ln_b=a[DEPTH:2 * DEPTH],
                    attn_b_f=a[2 * DEPTH, :2 * n_f].reshape(2, n_f))

    order = ("ln_g", "ln_b", "attn_w_in", "attn_b_f", "attn_w_out", "rnn_w_in", "rnn_conv_w",
             "rnn_conv_b", "rnn_w_a", "rnn_b_a", "rnn_w_i", "rnn_b_i", "rnn_lambda", "rnn_w_out")
    outs = [loss, grad_x]
    for sh, rp in ((g_sh, rg), (d_sh, rd), (nm_sh, rm), (nv_sh, rv)):
        allp = {**sh, **rep_unpack(rp)}
        outs.extend(allp[k] for k in order)
    return tuple(outs)
```

```python
import functools

import jax
import jax.numpy as jnp
from jax import lax
from jax.experimental import pallas as pl
from jax.experimental.pallas import tpu as pltpu

F32 = jnp.float32
BF16 = jnp.bfloat16

DEPTH = 4
N_HEADS = 16
HEAD_DIM = 64
N_PAIRS = N_HEADS // 2
RNN_BLOCKS = 4
RNN_BLOCK_WIDTH = 256
CONV_WIDTH = 4
LRU_C = 8.0
ALPHA = (2.0 * DEPTH) ** 0.25
LN_EPS = 1e-5
ADAM_LR, ADAM_B1, ADAM_B2, ADAM_EPS, ADAM_WD, ADAM_STEP = 0.001, 0.9, 0.999, 1e-8, 0.01, 10

LANES = 128
SUBLANES = 8
VMEM_LIMIT = 48 * 1024 * 1024

MESH = pl.DeviceIdType.MESH
HBM_SPEC = pl.BlockSpec(memory_space=pltpu.HBM)


def _cparams(*sem):
    return pltpu.CompilerParams(dimension_semantics=sem, vmem_limit_bytes=VMEM_LIMIT)


def _sigmoid(x):
    return 1.0 / (1.0 + jnp.exp(-x))


def _softplus(x):
    return jnp.maximum(x, 0.0) + jnp.log(1.0 + jnp.exp(-jnp.abs(x)))


def _a2a(src, *, group, bcast, name):
    n = 2 if group == "c" else 4
    blk = tuple(src.shape) if bcast else tuple(src.shape[1:])

    def body(src_ref, out_ref, send_sems, recv_sems, local_sem):
        x, y, c = lax.axis_index("x"), lax.axis_index("y"), lax.axis_index("c")
        if group == "c":
            me = c

            def peer(d):
                return (x, y, 1 - c), 1 - c
        else:
            me = 2 * x + y

            def peer(d):
                px, py = x ^ (d >> 1), y ^ (d & 1)
                return (px, py, c), 2 * px + py

        def block_for(k):
            return src_ref if bcast else src_ref.at[k]

        local = pltpu.make_async_copy(block_for(me), out_ref.at[me], local_sem)
        local.start()
        sends = []
        for d in range(1, n):
            dev, idx = peer(d)
            cp = pltpu.make_async_remote_copy(
                src_ref=block_for(idx), dst_ref=out_ref.at[me],
                send_sem=send_sems.at[d], recv_sem=recv_sems.at[d],
                device_id=dev, device_id_type=MESH)
            cp.start()
            sends.append(cp)
        for d in range(1, n):
            dev, idx = peer(d)
            pltpu.make_async_remote_copy(
                src_ref=block_for(idx), dst_ref=out_ref.at[idx],
                send_sem=send_sems.at[d], recv_sem=recv_sems.at[d],
                device_id=dev, device_id_type=MESH).wait_recv()
        for cp in sends:
            cp.wait_send()
        local.wait()

    return pl.pallas_call(
        body, name=name,
        out_shape=jax.ShapeDtypeStruct((n,) + blk, src.dtype),
        in_specs=[HBM_SPEC], out_specs=HBM_SPEC,
        scratch_shapes=[pltpu.SemaphoreType.DMA((n,)), pltpu.SemaphoreType.DMA((n,)),
                        pltpu.SemaphoreType.DMA],
    )(src)


def _all_gather(piece, name):
    return _a2a(_a2a(piece, group="xy", bcast=True, name=name + "_xy"),
                group="c", bcast=True, name=name + "_c")


D2D_CHUNKS = 16
ICI_CHUNKS = 8


def _row_chunks(rows, dtype, k):
    unit = SUBLANES * (4 // jnp.dtype(dtype).itemsize)
    assert rows % unit == 0
    units = rows // unit
    k = max(1, min(k, units))
    base, rem = divmod(units, k)
    out, r = [], 0
    for i in range(k):
        n = (base + (1 if i < rem else 0)) * unit
        out.append((r, n))
        r += n
    return out


def _chunks(shape, dtype, k):
    if len(shape) == 2:
        return [(pl.ds(r0, n),) for r0, n in _row_chunks(shape[0], dtype, k)]
    per = max(1, k // shape[0])
    return [(l, pl.ds(r0, n)) for l in range(shape[0]) for r0, n in _row_chunks(shape[1], dtype, per)]


def _mesh_place():
    x, y, c = lax.axis_index("x"), lax.axis_index("y"), lax.axis_index("c")
    return x, y, c, 2 * x + y


def _chip_peer(x, y, c, d):
    px, py = x ^ (d >> 1), y ^ (d & 1)
    return (px, py, c), 2 * px + py


def _remote(src, dst, send_sem, recv_sem, dev):
    return pltpu.make_async_remote_copy(src_ref=src, dst_ref=dst, send_sem=send_sem,
                                        recv_sem=recv_sem, device_id=dev, device_id_type=MESH)


def _comm_call(body, name, ins, out_shapes, n_sems, aliases=None):
    n = len(ins)
    return pl.pallas_call(
        body, name=name,
        out_shape=out_shapes, in_specs=[HBM_SPEC] * n, out_specs=[HBM_SPEC] * n,
        input_output_aliases=aliases or {},
        scratch_shapes=[pltpu.SemaphoreType.DMA((n_sems, n)), pltpu.SemaphoreType.DMA((n_sems, n))],
    )(*ins)


class _Exchange:
    def __init__(self, kind, arrays):
        self.kind, self.arrays, self.n = kind, list(arrays), len(arrays)
        if kind == "gather":
            self.chunks = [_chunks(a.shape, a.dtype, ICI_CHUNKS) for a in arrays]
            self.out_shapes = [jax.ShapeDtypeStruct((2, 4) + tuple(a.shape), a.dtype) for a in arrays]
        else:
            self.chunks = [_chunks(a.shape[1:], a.dtype, ICI_CHUNKS) for a in arrays]
            self.out_shapes = [jax.ShapeDtypeStruct(a.shape, a.dtype) for a in arrays]
        self.sem_shapes = [pltpu.SemaphoreType.DMA((4, self.n)), pltpu.SemaphoreType.DMA((4, self.n))]

    def _blocks(self, srcs, outs, o, c, me, pidx):
        if self.kind == "gather":
            return srcs[o], outs[o].at[c, me], outs[o].at[c, pidx]
        return srcs[o].at[pidx], outs[o].at[me], outs[o].at[pidx]

    def start(self, srcs, outs, send_sems, recv_sems):
        x, y, c, me = _mesh_place()
        if self.kind == "gather":
            for o in range(self.n):
                for idx in self.chunks[o]:
                    pltpu.make_async_copy(srcs[o].at[idx], outs[o].at[(c, me) + idx],
                                          send_sems.at[0, o]).start()
        for d in range(1, 4):
            dev, pidx = _chip_peer(x, y, c, d)
            for o in range(self.n):
                src, dst, _ = self._blocks(srcs, outs, o, c, me, pidx)
                for idx in self.chunks[o]:
                    _remote(src.at[idx], dst.at[idx], send_sems.at[d, o], recv_sems.at[d, o],
                            dev).start()

    def wait(self, srcs, outs, send_sems, recv_sems):
        x, y, c, me = _mesh_place()
        for d in range(1, 4):
            dev, pidx = _chip_peer(x, y, c, d)
            for o in range(self.n):
                src, _, land = self._blocks(srcs, outs, o, c, me, pidx)
                _remote(src, land, send_sems.at[d, o], recv_sems.at[d, o], dev).wait_recv()
        for d in range(1, 4):
            dev, pidx = _chip_peer(x, y, c, d)
            for o in range(self.n):
                src, _, land = self._blocks(srcs, outs, o, c, me, pidx)
                _remote(src, land, send_sems.at[d, o], recv_sems.at[d, o], dev).wait_send()
        if self.kind == "gather":
            for o in range(self.n):
                pltpu.make_async_copy(srcs[o], outs[o].at[c, me], send_sems.at[0, o]).wait()


def _run_exchange(ex, name):
    n = ex.n

    def body(*refs):
        srcs, outs, send_sems, recv_sems = refs[:n], refs[n:2 * n], refs[2 * n], refs[2 * n + 1]
        ex.start(srcs, outs, send_sems, recv_sems)
        ex.wait(srcs, outs, send_sems, recv_sems)

    return _comm_call(body, name, ex.arrays, ex.out_shapes, 4)


def _ag_c(bufs, name):
    n = len(bufs)
    chunks = [_chunks(b.shape[2:], b.dtype, D2D_CHUNKS // 4) for b in bufs]

    def body(*refs):
        srcs, outs, send_sems, recv_sems = refs[:n], refs[n:2 * n], refs[2 * n], refs[2 * n + 1]
        x, y, c, _ = _mesh_place()
        sib = (x, y, 1 - c)
        for o in range(n):
            for k in range(4):
                for idx in chunks[o]:
                    _remote(srcs[o].at[(c, k) + idx], outs[o].at[(c, k) + idx],
                            send_sems.at[0, o], recv_sems.at[0, o], sib).start()
        for o in range(n):
            _remote(srcs[o].at[c], outs[o].at[1 - c], send_sems.at[0, o], recv_sems.at[0, o],
                    sib).wait_recv()
        for o in range(n):
            _remote(srcs[o].at[c], outs[o].at[1 - c], send_sems.at[0, o], recv_sems.at[0, o],
                    sib).wait_send()

    shapes = [jax.ShapeDtypeStruct(b.shape, b.dtype) for b in bufs]
    return _comm_call(body, name, bufs, shapes, 1, aliases={i: i for i in range(n)})


def _rs_c(gs, name):
    n = len(gs)
    chunks = [_chunks(g.shape[2:], g.dtype, max(1, D2D_CHUNKS // g.shape[1])) for g in gs]

    def body(*refs):
        srcs, outs, send_sems, recv_sems = refs[:n], refs[n:2 * n], refs[2 * n], refs[2 * n + 1]
        x, y, c, _ = _mesh_place()
        sib = (x, y, 1 - c)
        for o in range(n):
            for k in range(gs[o].shape[1]):
                for idx in chunks[o]:
                    _remote(srcs[o].at[(1 - c, k) + idx], outs[o].at[(k,) + idx],
                            send_sems.at[0, o], recv_sems.at[0, o], sib).start()
        for o in range(n):
            _remote(srcs[o].at[1 - c], outs[o], send_sems.at[0, o], recv_sems.at[0, o],
                    sib).wait_recv()
        for o in range(n):
            _remote(srcs[o].at[1 - c], outs[o], send_sems.at[0, o], recv_sems.at[0, o],
                    sib).wait_send()

    shapes = [jax.ShapeDtypeStruct(g.shape[1:], g.dtype) for g in gs]
    return _comm_call(body, name, gs, shapes, 1)


def _matmul(a, b, *, trans_b, tm, tn, name, add=None, add_scale=1.0, host=None):
    a_parts = list(a) if isinstance(a, (list, tuple)) else [a]
    M, K = a_parts[0].shape[0], sum(p.shape[1] for p in a_parts)
    N = b.shape[0] if trans_b else b.shape[1]
    tm, tn = min(tm, M), min(tn, N)
    assert M % tm == 0 and N % tn == 0
    dn = (((1,), (1,)), ((), ())) if trans_b else (((1,), (0,)), ((), ()))
    na = len(a_parts)

    def body(*refs):
        a_refs, b_ref, o_ref = refs[:na], refs[na], refs[-1]
        av = [r[...].astype(BF16) for r in a_refs]
        av = av[0] if na == 1 else jnp.concatenate(av, axis=1)
        r = lax.dot_general(av, b_ref[...].astype(BF16), dn, preferred_element_type=F32)
        if add is not None:
            r = r + add_scale * refs[na + 1][...]
        o_ref[...] = r

    b_spec = (pl.BlockSpec((tn, K), lambda j, i: (j, 0)) if trans_b
              else pl.BlockSpec((K, tn), lambda j, i: (0, j)))
    in_specs = [pl.BlockSpec((tm, p.shape[1]), lambda j, i: (i, 0)) for p in a_parts] + [b_spec]
    args = a_parts + [b]
    if add is not None:
        in_specs.append(pl.BlockSpec((tm, tn), lambda j, i: (i, j)))
        args.append(add)
    grid = (N // tn, M // tm)
    x_in, x_out, x_shapes, x_scratch, x_args = _host_specs(host)
    body = _hosted(body, len(args), 1, 0, host, grid)
    outs = pl.pallas_call(
        body, name=name, grid=grid,
        in_specs=in_specs + x_in,
        out_specs=[pl.BlockSpec((tm, tn), lambda j, i: (i, j))] + x_out,
        out_shape=[jax.ShapeDtypeStruct((M, N), F32)] + x_shapes,
        scratch_shapes=x_scratch,
        compiler_params=_cparams(*(("arbitrary",) * 2 if host else ("parallel",) * 2)),
    )(*args, *x_args)
    return outs if host else outs[0]


def _matmul_tn(a, b, *, tm, tn, tk, name):
    T, M = a.shape
    N = b.shape[1]
    tm, tn, tk = min(tm, M), min(tn, N), min(tk, T)
    assert M % tm == 0 and N % tn == 0 and T % tk == 0

    def body(a_ref, b_ref, o_ref):
        @pl.when(pl.program_id(2) == 0)
        def _():
            o_ref[...] = jnp.zeros_like(o_ref)

        o_ref[...] += lax.dot_general(a_ref[...].astype(BF16), b_ref[...].astype(BF16),
                                      (((0,), (0,)), ((), ())), preferred_element_type=F32)

    return pl.pallas_call(
        body, name=name, grid=(M // tm, N // tn, T // tk),
        in_specs=[pl.BlockSpec((tk, tm), lambda i, j, k: (k, i)),
                  pl.BlockSpec((tk, tn), lambda i, j, k: (k, j))],
        out_specs=pl.BlockSpec((tm, tn), lambda i, j, k: (i, j)),
        out_shape=jax.ShapeDtypeStruct((M, N), F32),
        compiler_params=_cparams("parallel", "parallel", "arbitrary"),
    )(a, b)


def _matmul_tn_parts(a, parts, *, tm, tk, name):
    T, M = a.shape
    tm, tk = min(tm, M), min(tk, T)
    assert M % tm == 0 and T % tk == 0
    n = len(parts)

    def body(*refs):
        a_ref, b_refs, o_refs = refs[0], refs[1:1 + n], refs[1 + n:]
        av = a_ref[...].astype(BF16)
        for b_ref, o_ref in zip(b_refs, o_refs):
            @pl.when(pl.program_id(1) == 0)
            def _(o_ref=o_ref):
                o_ref[...] = jnp.zeros_like(o_ref)

            o_ref[...] += lax.dot_general(av, b_ref[...].astype(BF16), (((0,), (0,)), ((), ())),
                                          preferred_element_type=F32)

    return pl.pallas_call(
        body, name=name, grid=(M // tm, T // tk),
        in_specs=[pl.BlockSpec((tk, tm), lambda i, k: (k, i))]
        + [pl.BlockSpec((tk, p.shape[1]), lambda i, k: (k, 0)) for p in parts],
        out_specs=[pl.BlockSpec((tm, p.shape[1]), lambda i, k: (i, 0)) for p in parts],
        out_shape=[jax.ShapeDtypeStruct((M, p.shape[1]), F32) for p in parts],
        compiler_params=_cparams("parallel", "arbitrary"),
    )(a, *parts)


def _head_masks(rows):
    lane = lax.broadcasted_iota(jnp.int32, (rows, LANES), 1)
    return lane < HEAD_DIM, lane >= HEAD_DIM


def _causal(i_q, i_k, tq, tk):
    row = i_q * tq + lax.broadcasted_iota(jnp.int32, (tq, tk), 0)
    col = i_k * tk + lax.broadcasted_iota(jnp.int32, (tq, tk), 1)
    return row >= col


def _hosted(body, n_in, n_out, n_scratch, host, grid):
    if host is None:
        return body
    nx = host.n

    def wrapped(*refs):
        ins, xsrcs = refs[:n_in], refs[n_in:n_in + nx]
        outs = refs[n_in + nx:n_in + nx + n_out]
        xouts = refs[n_in + nx + n_out:n_in + 2 * nx + n_out]
        scratch = refs[n_in + 2 * nx + n_out:n_in + 2 * nx + n_out + n_scratch]
        xsems = refs[n_in + 2 * nx + n_out + n_scratch:]
        step = pl.program_id(0) * grid[1] + pl.program_id(1)

        @pl.when(step == 0)
        def _():
            host.start(xsrcs, xouts, *xsems)

        body(*ins, *outs, *scratch)

        @pl.when(step == grid[0] * grid[1] - 1)
        def _():
            host.wait(xsrcs, xouts, *xsems)

    return wrapped


def _host_specs(host):
    if host is None:
        return [], [], [], [], []
    return ([HBM_SPEC] * host.n, [HBM_SPEC] * host.n, host.out_shapes, host.sem_shapes, host.arrays)


def _flash_fwd(proj, cum4, *, tb, name, host=None):
    T = proj.shape[0]
    D = N_HEADS * HEAD_DIM
    nb = T // tb
    cb = D // LANES
    x_in, x_out, x_shapes, x_scratch, x_args = _host_specs(host)

    def body(q_ref, k_ref, v_ref, g_ref, cum_ref, o_ref, og_ref, lp_ref, kb_ref, vb_ref):
        i = pl.program_id(1)

        @pl.when(i == 0)
        def _():
            kb_ref[...] = k_ref[...].astype(BF16)
            vb_ref[...] = v_ref[...].astype(BF16)

        q = q_ref[...] * (HEAD_DIM ** -0.5)
        masks = _head_masks(tb)
        qh = [jnp.where(masks[h], q, 0.0).astype(BF16) for h in range(2)]
        cref = [cum_ref[0, h, pl.ds(i, 1), :][:, 0:1] for h in range(2)]

        def step(kbi, carry, masked):
            k0 = pl.multiple_of(kbi * tb, tb)
            kblk = kb_ref[pl.ds(k0, tb), :]
            vblk = vb_ref[pl.ds(k0, tb), :]
            new = []
            for h in range(2):
                m, l, acc = carry[h]
                s = lax.dot_general(qh[h], kblk, (((1,), (1,)), ((), ())),
                                    preferred_element_type=F32)
                s = s + (cref[h] - cum_ref[0, h, pl.ds(kbi, 1), :])
                if masked:
                    s = jnp.where(_causal(i, kbi, tb, tb), s, -jnp.inf)
                m_new = jnp.maximum(m, jnp.max(s, axis=-1, keepdims=True))
                alpha = jnp.exp(m - m_new)
                p = jnp.exp(s - m_new)
                l = alpha * l + jnp.sum(p, axis=-1, keepdims=True)
                acc = alpha * acc + jnp.dot(p.astype(BF16), vblk, preferred_element_type=F32)
                new.append((m_new, l, acc))
            return tuple(new)

        init1 = (jnp.full((tb, 1), -jnp.inf, F32), jnp.zeros((tb, 1), F32),
                 jnp.zeros((tb, LANES), F32))
        carry = lax.fori_loop(0, i, lambda kbi, c: step(kbi, c, False), (init1, init1))
        outs = []
        for h, (m, l, acc) in enumerate(step(i, carry, True)):
            outs.append(acc / l)
            lp_ref[h] = jnp.broadcast_to(m + jnp.log(l) - cref[h], (tb, LANES))
        o = jnp.where(masks[0], outs[0], outs[1])
        o_ref[...] = o
        gate = g_ref[...]
        og_ref[...] = (o * (gate * _sigmoid(gate))).astype(BF16)

    body = _hosted(body, 5, 3, 2, host, (N_PAIRS, nb))
    return pl.pallas_call(
        body, name=name, grid=(N_PAIRS, nb),
        in_specs=[pl.BlockSpec((tb, LANES), lambda j, i: (i, j)),
                  pl.BlockSpec((T, LANES), lambda j, i: (0, cb + j)),
                  pl.BlockSpec((T, LANES), lambda j, i: (0, 2 * cb + j)),
                  pl.BlockSpec((tb, LANES), lambda j, i: (i, 3 * cb + j)),
                  pl.BlockSpec((1, 2, nb, tb), lambda j, i: (j, 0, 0, 0))] + x_in,
        out_specs=[pl.BlockSpec((tb, LANES), lambda j, i: (i, j)),
                   pl.BlockSpec((tb, LANES), lambda j, i: (i, j)),
                   pl.BlockSpec((2, tb, LANES), lambda j, i: (j, i, 0))] + x_out,
        out_shape=[jax.ShapeDtypeStruct((T, D), F32), jax.ShapeDtypeStruct((T, D), BF16),
                   jax.ShapeDtypeStruct((N_HEADS, T, LANES), F32)] + x_shapes,
        scratch_shapes=[pltpu.VMEM((T, LANES), BF16), pltpu.VMEM((T, LANES), BF16)] + x_scratch,
        compiler_params=_cparams("arbitrary", "arbitrary"),
    )(proj, proj, proj, proj, cum4, *x_args)


def _flash_bwd_dq(proj, cum4, o, dog, lp, *, tb, name, host=None):
    T = proj.shape[0]
    D = N_HEADS * HEAD_DIM
    nb = T // tb
    cb = D // LANES
    x_in, x_out, x_shapes, x_scratch, x_args = _host_specs(host)

    def body(q_ref, k_ref, v_ref, g_ref, cum_ref, o_ref, dog_ref, lp_ref,
             dq_ref, dg_ref, do_ref, dl_ref, dc_ref, kb_ref, vb_ref):
        i = pl.program_id(1)

        @pl.when(i == 0)
        def _():
            kb_ref[...] = k_ref[...].astype(BF16)
            vb_ref[...] = v_ref[...].astype(BF16)

        gate = g_ref[...]
        sg = _sigmoid(gate)
        o = o_ref[...]
        dog = dog_ref[...]
        do = dog * (gate * sg)
        dg_ref[...] = (dog * o * (sg * (1.0 + gate * (1.0 - sg)))).astype(BF16)
        do_ref[...] = do.astype(BF16)
        q = q_ref[...] * (HEAD_DIM ** -0.5)
        masks = _head_masks(tb)
        qh = [jnp.where(masks[h], q, 0.0).astype(BF16) for h in range(2)]
        doh = [jnp.where(masks[h], do, 0.0).astype(BF16) for h in range(2)]
        delta = [jnp.sum(jnp.where(masks[h], do * o, 0.0), axis=-1, keepdims=True) for h in range(2)]
        lph = [lp_ref[h][:, 0:1] for h in range(2)]
        for h in range(2):
            dl_ref[h] = jnp.broadcast_to(delta[h], (tb, LANES))

        def step(kbi, carry, masked):
            k0 = pl.multiple_of(kbi * tb, tb)
            kblk = kb_ref[pl.ds(k0, tb), :]
            vblk = vb_ref[pl.ds(k0, tb), :]
            new = []
            for h in range(2):
                acc, rs = carry[h]
                s = lax.dot_general(qh[h], kblk, (((1,), (1,)), ((), ())), preferred_element_type=F32)
                p = jnp.exp(s - cum_ref[0, h, pl.ds(kbi, 1), :] - lph[h])
                if masked:
                    p = jnp.where(_causal(i, kbi, tb, tb), p, 0.0)
                dp = lax.dot_general(doh[h], vblk, (((1,), (1,)), ((), ())),
                                     preferred_element_type=F32)
                ds = p * (dp - delta[h])
                new.append((acc + jnp.dot(ds.astype(BF16), kblk, preferred_element_type=F32),
                            rs + jnp.sum(ds, axis=-1, keepdims=True)))
            return tuple(new)

        init1 = (jnp.zeros((tb, LANES), F32), jnp.zeros((tb, 1), F32))
        carry = lax.fori_loop(0, i, lambda kbi, c: step(kbi, c, False), (init1, init1))
        dqs = []
        for h, (acc, rs) in enumerate(step(i, carry, True)):
            dqs.append(acc)
            dc_ref[0, 0, pl.ds(h, 1), :] = jnp.broadcast_to(rs, (tb, LANES)).T[0:1, :]
        dq_ref[...] = (jnp.where(masks[0], dqs[0], dqs[1]) * (HEAD_DIM ** -0.5)).astype(BF16)

    blk = pl.BlockSpec((tb, LANES), lambda j, i: (i, j))
    stat = pl.BlockSpec((2, tb, LANES), lambda j, i: (j, i, 0))
    body = _hosted(body, 8, 5, 2, host, (N_PAIRS, nb))
    return pl.pallas_call(
        body, name=name, grid=(N_PAIRS, nb),
        in_specs=[blk,
                  pl.BlockSpec((T, LANES), lambda j, i: (0, cb + j)),
                  pl.BlockSpec((T, LANES), lambda j, i: (0, 2 * cb + j)),
                  pl.BlockSpec((tb, LANES), lambda j, i: (i, 3 * cb + j)),
                  pl.BlockSpec((1, 2, nb, tb), lambda j, i: (j, 0, 0, 0)),
                  blk, blk, stat] + x_in,
        out_specs=[blk, blk, blk, stat,
                   pl.BlockSpec((1, 1, 2, tb), lambda j, i: (j, i, 0, 0))] + x_out,
        out_shape=[jax.ShapeDtypeStruct((T, D), BF16), jax.ShapeDtypeStruct((T, D), BF16),
                   jax.ShapeDtypeStruct((T, D), BF16),
                   jax.ShapeDtypeStruct((N_HEADS, T, LANES), F32),
                   jax.ShapeDtypeStruct((N_PAIRS, nb, 2, tb), F32)] + x_shapes,
        scratch_shapes=[pltpu.VMEM((T, LANES), BF16), pltpu.VMEM((T, LANES), BF16)] + x_scratch,
        compiler_params=_cparams("arbitrary", "arbitrary"),
    )(proj, proj, proj, proj, cum4, o, dog, lp, *x_args)


def _flash_bwd_dkv(proj, cum4, do, lp, delta, *, tb, name):
    T = proj.shape[0]
    D = N_HEADS * HEAD_DIM
    nb = T // tb
    cb = D // LANES

    def body(q_ref, k_ref, v_ref, cum_ref, do_ref, lp_ref, dl_ref, dk_ref, dv_ref, dc_ref):
        kbi = pl.program_id(1)
        k = k_ref[...] * (HEAD_DIM ** -0.5)
        v = v_ref[...]
        masks = _head_masks(tb)
        kh = [jnp.where(masks[h], k, 0.0).astype(BF16) for h in range(2)]
        vh = [jnp.where(masks[h], v, 0.0).astype(BF16) for h in range(2)]
        ck = [cum_ref[0, h, pl.ds(kbi, 1), :] for h in range(2)]

        def step(i, carry, masked):
            q0 = pl.multiple_of(i * tb, tb)
            qb = q_ref[pl.ds(q0, tb), :].astype(BF16)
            dob = do_ref[pl.ds(q0, tb), :]
            new = []
            for h in range(2):
                dk, dv, dc = carry[h]
                s = lax.dot_general(qb, kh[h], (((1,), (1,)), ((), ())), preferred_element_type=F32)
                p = jnp.exp(s - ck[h] - lp_ref[h, pl.ds(q0, tb), :][:, 0:1])
                if masked:
                    p = jnp.where(_causal(i, kbi, tb, tb), p, 0.0)
                dp = lax.dot_general(dob, vh[h], (((1,), (1,)), ((), ())), preferred_element_type=F32)
                ds = p * (dp - dl_ref[h, pl.ds(q0, tb), :][:, 0:1])
                dv = dv + lax.dot_general(p.astype(BF16), dob, (((0,), (0,)), ((), ())),
                                          preferred_element_type=F32)
                dk = dk + lax.dot_general(ds.astype(BF16), qb, (((0,), (0,)), ((), ())),
                                          preferred_element_type=F32)
                new.append((dk, dv, dc - jnp.sum(ds, axis=0, keepdims=True)))
            return tuple(new)

        init1 = (jnp.zeros((tb, LANES), F32), jnp.zeros((tb, LANES), F32), jnp.zeros((1, tb), F32))
        carry = step(kbi, (init1, init1), True)
        carry = lax.fori_loop(kbi + 1, nb, lambda i, c: step(i, c, False), carry)
        dks, dvs = [], []
        for h, (dk, dv, dc) in enumerate(carry):
            dks.append(dk)
            dvs.append(dv)
            dc_ref[0, 0, pl.ds(h, 1), :] = dc
        dk_ref[...] = (jnp.where(masks[0], dks[0], dks[1]) * (HEAD_DIM ** -0.5)).astype(BF16)
        dv_ref[...] = jnp.where(masks[0], dvs[0], dvs[1]).astype(BF16)

    full = pl.BlockSpec((T, LANES), lambda j, i: (0, j))
    stat = pl.BlockSpec((2, T, LANES), lambda j, i: (j, 0, 0))
    blk = pl.BlockSpec((tb, LANES), lambda j, i: (i, j))
    return pl.pallas_call(
        body, name=name, grid=(N_PAIRS, nb),
        in_specs=[full,
                  pl.BlockSpec((tb, LANES), lambda j, i: (i, cb + j)),
                  pl.BlockSpec((tb, LANES), lambda j, i: (i, 2 * cb + j)),
                  pl.BlockSpec((1, 2, nb, tb), lambda j, i: (j, 0, 0, 0)),
                  full, stat, stat],
        out_specs=[blk, blk, pl.BlockSpec((1, 1, 2, tb), lambda j, i: (j, i, 0, 0))],
        out_shape=[jax.ShapeDtypeStruct((T, D), BF16), jax.ShapeDtypeStruct((T, D), BF16),
                   jax.ShapeDtypeStruct((N_PAIRS, nb, 2, tb), F32)],
        compiler_params=_cparams("parallel", "arbitrary"),
    )(proj, proj, proj, cum4, do, lp, delta)


def _flash_bwd(proj, cum4, o, dog, lp, *, tb, name, host=None):
    T = proj.shape[0]
    D = N_HEADS * HEAD_DIM
    nb = T // tb
    cb = D // LANES
    x_in, x_out, x_shapes, x_scratch, x_args = _host_specs(host)

    def body(q_ref, k_ref, v_ref, g_ref, cum_ref, o_ref, dog_ref, lp_ref,
             dq_ref, dg_ref, dk_ref, dv_ref, dcq_ref, dck_ref,
             kb_ref, vb_ref, dka_ref, dva_ref, dca_ref):
        i = pl.program_id(1)

        @pl.when(i == 0)
        def _():
            kb_ref[...] = k_ref[...].astype(BF16)
            vb_ref[...] = v_ref[...].astype(BF16)
            dka_ref[...] = jnp.zeros_like(dka_ref)
            dva_ref[...] = jnp.zeros_like(dva_ref)
            dca_ref[...] = jnp.zeros_like(dca_ref)

        gate = g_ref[...]
        sg = _sigmoid(gate)
        o = o_ref[...]
        dog = dog_ref[...]
        do = dog * (gate * sg)
        dg_ref[...] = (dog * o * (sg * (1.0 + gate * (1.0 - sg)))).astype(BF16)
        q = q_ref[...] * (HEAD_DIM ** -0.5)
        masks = _head_masks(tb)
        qh = [jnp.where(masks[h], q, 0.0).astype(BF16) for h in range(2)]
        doh = [jnp.where(masks[h], do, 0.0).astype(BF16) for h in range(2)]
        delta = [jnp.sum(jnp.where(masks[h], do * o, 0.0), axis=-1, keepdims=True) for h in range(2)]
        lph = [lp_ref[h][:, 0:1] for h in range(2)]

        def step(kbi, carry, masked):
            k0 = pl.multiple_of(kbi * tb, tb)
            kblk = kb_ref[pl.ds(k0, tb), :]
            vblk = vb_ref[pl.ds(k0, tb), :]
            new, dk, dv = [], None, None
            for h in range(2):
                acc, rs = carry[h]
                s = lax.dot_general(qh[h], kblk, (((1,), (1,)), ((), ())), preferred_element_type=F32)
                p = jnp.exp(s - cum_ref[0, h, pl.ds(kbi, 1), :] - lph[h])
                if masked:
                    p = jnp.where(_causal(i, kbi, tb, tb), p, 0.0)
                dp = lax.dot_general(doh[h], vblk, (((1,), (1,)), ((), ())),
                                     preferred_element_type=F32)
                ds = p * (dp - delta[h])
                pb, dsb = p.astype(BF16), ds.astype(BF16)
                dv_h = lax.dot_general(pb, doh[h], (((0,), (0,)), ((), ())),
                                       preferred_element_type=F32)
                dk_h = lax.dot_general(dsb, qh[h], (((0,), (0,)), ((), ())),
                                       preferred_element_type=F32)
                dv = dv_h if dv is None else dv + dv_h
                dk = dk_h if dk is None else dk + dk_h
                dca_ref[h, pl.ds(kbi, 1), :] -= jnp.sum(ds, axis=0, keepdims=True)
                new.append((acc + jnp.dot(dsb, kblk, preferred_element_type=F32),
                            rs + jnp.sum(ds, axis=-1, keepdims=True)))
            dka_ref[pl.ds(k0, tb), :] += dk
            dva_ref[pl.ds(k0, tb), :] += dv
            return tuple(new)

        init1 = (jnp.zeros((tb, LANES), F32), jnp.zeros((tb, 1), F32))
        carry = lax.fori_loop(0, i, lambda kbi, c: step(kbi, c, False), (init1, init1))
        dqs = []
        for h, (acc, rs) in enumerate(step(i, carry, True)):
            dqs.append(acc)
            dcq_ref[0, 0, pl.ds(h, 1), :] = jnp.broadcast_to(rs, (tb, LANES)).T[0:1, :]
        dq_ref[...] = (jnp.where(masks[0], dqs[0], dqs[1]) * (HEAD_DIM ** -0.5)).astype(BF16)

        @pl.when(i == nb - 1)
        def _():
            dk_ref[...] = dka_ref[...].astype(BF16)
            dv_ref[...] = dva_ref[...].astype(BF16)
            dck_ref[0] = dca_ref[...]

    blk = pl.BlockSpec((tb, LANES), lambda j, i: (i, j))
    full = pl.BlockSpec((T, LANES), lambda j, i: (0, j))
    body = _hosted(body, 8, 6, 5, host, (N_PAIRS, nb))
    return pl.pallas_call(
        body, name=name, grid=(N_PAIRS, nb),
        in_specs=[blk,
                  pl.BlockSpec((T, LANES), lambda j, i: (0, cb + j)),
                  pl.BlockSpec((T, LANES), lambda j, i: (0, 2 * cb + j)),
                  pl.BlockSpec((tb, LANES), lambda j, i: (i, 3 * cb + j)),
                  pl.BlockSpec((1, 2, nb, tb), lambda j, i: (j, 0, 0, 0)),
                  blk, blk, pl.BlockSpec((2, tb, LANES), lambda j, i: (j, i, 0))] + x_in,
        out_specs=[blk, blk, full, full,
                   pl.BlockSpec((1, 1, 2, tb), lambda j, i: (j, i, 0, 0)),
                   pl.BlockSpec((1, 2, nb, tb), lambda j, i: (j, 0, 0, 0))] + x_out,
        out_shape=[jax.ShapeDtypeStruct((T, D), BF16)] * 4
        + [jax.ShapeDtypeStruct((N_PAIRS, nb, 2, tb), F32),
           jax.ShapeDtypeStruct((N_PAIRS, 2, nb, tb), F32)] + x_shapes,
        scratch_shapes=[pltpu.VMEM((T, LANES), BF16), pltpu.VMEM((T, LANES), BF16),
                        pltpu.VMEM((T, LANES), F32), pltpu.VMEM((T, LANES), F32),
                        pltpu.VMEM((2, nb, tb), F32)] + x_scratch,
        compiler_params=_cparams("arbitrary", "arbitrary"),
    )(proj, proj, proj, proj, cum4, o, dog, lp, *x_args)


def _cumsum_fwd(proj, bf_row, *, tt, name):
    T = proj.shape[0]
    cb = (proj.shape[1] - LANES) // LANES

    def body(f_ref, b_ref, out_ref, carry_ref):
        i = pl.program_id(0)

        @pl.when(i == 0)
        def _():
            carry_ref[...] = jnp.zeros_like(carry_ref)

        ls = -_softplus(-(f_ref[...] + b_ref[...]))
        tri = (lax.broadcasted_iota(jnp.int32, (tt, tt), 0)
               >= lax.broadcasted_iota(jnp.int32, (tt, tt), 1)).astype(F32)
        cum = jnp.dot(tri, ls, preferred_element_type=F32,
                      precision=lax.Precision.HIGHEST) + carry_ref[...]
        carry_ref[...] = cum[tt - 1:tt, :]
        out_ref[...] = cum.T

    return pl.pallas_call(
        body, name=name, grid=(T // tt,),
        in_specs=[pl.BlockSpec((tt, LANES), lambda i: (i, cb)),
                  pl.BlockSpec((1, LANES), lambda i: (0, 0))],
        out_specs=pl.BlockSpec((LANES, tt), lambda i: (0, i)),
        out_shape=jax.ShapeDtypeStruct((LANES, T), F32),
        scratch_shapes=[pltpu.VMEM((1, LANES), F32)],
        compiler_params=_cparams("arbitrary"),
    )(proj, bf_row)


def _cumsum_bwd(dcum_t, proj, bf_row, *, tt, name):
    T = proj.shape[0]
    cb = (proj.shape[1] - LANES) // LANES
    nt = T // tt

    def body(dc_ref, f_ref, b_ref, df_ref, db_ref, carry_ref):
        i = pl.program_id(0)

        @pl.when(i == 0)
        def _():
            carry_ref[...] = jnp.zeros_like(carry_ref)
            db_ref[...] = jnp.zeros_like(db_ref)

        dc = dc_ref[...].T
        tri = (lax.broadcasted_iota(jnp.int32, (tt, tt), 0)
               <= lax.broadcasted_iota(jnp.int32, (tt, tt), 1)).astype(F32)
        rev = jnp.dot(tri, dc, preferred_element_type=F32,
                      precision=lax.Precision.HIGHEST) + carry_ref[...]
        carry_ref[...] = rev[0:1, :]
        df = rev * _sigmoid(-(f_ref[...] + b_ref[...]))
        df_ref[...] = df.astype(BF16)
        db_ref[...] += jnp.sum(df, axis=0, keepdims=True)

    return pl.pallas_call(
        body, name=name, grid=(nt,),
        in_specs=[pl.BlockSpec((LANES, tt), lambda i: (0, nt - 1 - i)),
                  pl.BlockSpec((tt, LANES), lambda i: (nt - 1 - i, cb)),
                  pl.BlockSpec((1, LANES), lambda i: (0, 0))],
        out_specs=[pl.BlockSpec((tt, LANES), lambda i: (nt - 1 - i, 0)),
                   pl.BlockSpec((1, LANES), lambda i: (0, 0))],
        out_shape=[jax.ShapeDtypeStruct((T, LANES), BF16), jax.ShapeDtypeStruct((1, LANES), F32)],
        scratch_shapes=[pltpu.VMEM((1, LANES), F32)],
        compiler_params=_cparams("arbitrary"),
    )(dcum_t, proj, bf_row)


def _rg_gates(upad_ref, small_ref, wa_ref, wi_ref, tt):
    off = SUBLANES - (CONV_WIDTH - 1)
    u = small_ref[4:5, :]
    for tap in range(CONV_WIDTH):
        u = u + upad_ref[off + tap:off + tap + tt, :] * small_ref[tap:tap + 1, :]
    pa, pi = [], []
    for n in range(RNN_BLOCKS):
        ub = u[:, n * RNN_BLOCK_WIDTH:(n + 1) * RNN_BLOCK_WIDTH].astype(BF16)
        pa.append(jnp.dot(ub, wa_ref[n], preferred_element_type=F32))
        pi.append(jnp.dot(ub, wi_ref[n], preferred_element_type=F32))
    r = _sigmoid(jnp.concatenate(pa, axis=-1) + small_ref[5:6, :])
    ig = _sigmoid(jnp.concatenate(pi, axis=-1) + small_ref[6:7, :])
    spl = _softplus(-small_ref[7:8, :])
    log_a = (-LRU_C) * r * spl
    a = jnp.exp(log_a)
    s = jnp.sqrt(jnp.tanh(-log_a) * (a * a + 1.0))
    return u, r, ig, spl, a, s


def _rg_fwd(proj, small, wa, wi, *, tt, name):
    T = proj.shape[0]
    D = RNN_BLOCKS * RNN_BLOCK_WIDTH
    hb = tt // SUBLANES

    def body(u0_ref, halo_ref, g_ref, small_ref, wa_ref, wi_ref, h_ref, y_ref,
             upad_ref, a_ref, b_ref, carry_ref):
        i = pl.program_id(0)

        @pl.when(i == 0)
        def _():
            carry_ref[...] = jnp.zeros_like(carry_ref)

        upad_ref[0:SUBLANES, :] = jnp.where(i == 0, 0.0, halo_ref[...])
        upad_ref[SUBLANES:, :] = u0_ref[...]
        u, r, ig, spl, a, s = _rg_gates(upad_ref, small_ref, wa_ref, wi_ref, tt)
        a_ref[...] = a
        b_ref[...] = s * (ig * u)

        def row(t, h):
            h = a_ref[pl.ds(t, 1), :] * h + b_ref[pl.ds(t, 1), :]
            h_ref[pl.ds(t, 1), :] = h
            return h

        carry_ref[...] = lax.fori_loop(0, tt, row, carry_ref[...], unroll=8)
        gate = g_ref[...]
        y_ref[...] = (h_ref[...] * (gate * _sigmoid(gate))).astype(BF16)

    return pl.pallas_call(
        body, name=name, grid=(T // tt,),
        in_specs=[pl.BlockSpec((tt, D), lambda i: (i, 0)),
                  pl.BlockSpec((SUBLANES, D), lambda i: (jnp.maximum(i * hb - 1, 0), 0)),
                  pl.BlockSpec((tt, D), lambda i: (i, 1)),
                  pl.BlockSpec((SUBLANES, D), lambda i: (0, 0)),
                  pl.BlockSpec((RNN_BLOCKS, RNN_BLOCK_WIDTH, RNN_BLOCK_WIDTH), lambda i: (0, 0, 0)),
                  pl.BlockSpec((RNN_BLOCKS, RNN_BLOCK_WIDTH, RNN_BLOCK_WIDTH), lambda i: (0, 0, 0))],
        out_specs=[pl.BlockSpec((tt, D), lambda i: (i, 0)), pl.BlockSpec((tt, D), lambda i: (i, 0))],
        out_shape=[jax.ShapeDtypeStruct((T, D), F32), jax.ShapeDtypeStruct((T, D), BF16)],
        scratch_shapes=[pltpu.VMEM((tt + SUBLANES, D), F32), pltpu.VMEM((tt, D), F32),
                        pltpu.VMEM((tt, D), F32), pltpu.VMEM((1, D), F32)],
        compiler_params=_cparams("arbitrary"),
    )(proj, proj, proj, small, wa, wi)


def _rg_bwd(proj, hs, dy, small, wa, wi, *, tt, name):
    T = proj.shape[0]
    D = RNN_BLOCKS * RNN_BLOCK_WIDTH
    W = RNN_BLOCK_WIDTH
    hb = tt // SUBLANES
    nt = T // tt

    def body(u0_ref, uhalo_ref, g_ref, h_ref, hhalo_ref, dy_ref, small_ref, wa_ref, wi_ref,
             dp_ref, dwa_ref, dwi_ref, ds_ref,
             upad_ref, hpad_ref, a_ref, g_s_ref, duext_ref, carry_ref):
        i = pl.program_id(0)
        first_chunk = i == nt - 1

        @pl.when(i == 0)
        def _():
            carry_ref[...] = jnp.zeros_like(carry_ref)
            duext_ref[...] = jnp.zeros_like(duext_ref)
            dwa_ref[...] = jnp.zeros_like(dwa_ref)
            dwi_ref[...] = jnp.zeros_like(dwi_ref)
            ds_ref[...] = jnp.zeros_like(ds_ref)

        upad_ref[0:SUBLANES, :] = jnp.where(first_chunk, 0.0, uhalo_ref[...])
        upad_ref[SUBLANES:, :] = u0_ref[...]
        hpad_ref[0:SUBLANES, :] = jnp.where(first_chunk, 0.0, hhalo_ref[...])
        hpad_ref[SUBLANES:, :] = h_ref[...]
        u, r, ig, spl, a, s = _rg_gates(upad_ref, small_ref, wa_ref, wi_ref, tt)
        gate = g_ref[...]
        sg = _sigmoid(gate)
        dy = dy_ref[...]
        dp_ref[:, D:] = (dy * h_ref[...] * (sg * (1.0 + gate * (1.0 - sg)))).astype(BF16)
        a_ref[...] = a
        g_s_ref[...] = dy * (gate * sg)

        def row(k, c):
            t = tt - 1 - k
            g = g_s_ref[pl.ds(t, 1), :] + c
            g_s_ref[pl.ds(t, 1), :] = g
            return a_ref[pl.ds(t, 1), :] * g

        carry_ref[...] = lax.fori_loop(0, tt, row, carry_ref[...], unroll=8)
        g = g_s_ref[...]
        h_prev = hpad_ref[SUBLANES - 1:SUBLANES - 1 + tt, :]
        iu = ig * u
        d_iu = g * s
        dlog_a = (g * h_prev) * a - (g * iu) * (a * a) / s
        dpre_a = (dlog_a * ((-LRU_C) * spl)) * r * (1.0 - r)
        dpre_i = (d_iu * u) * ig * (1.0 - ig)
        dlam = jnp.sum(dlog_a * r, axis=0, keepdims=True) * (LRU_C * _sigmoid(-small_ref[7:8, :]))
        du_parts = []
        for n in range(RNN_BLOCKS):
            sl = slice(n * W, (n + 1) * W)
            ub = u[:, sl].astype(BF16)
            da_n = dpre_a[:, sl].astype(BF16)
            di_n = dpre_i[:, sl].astype(BF16)
            dwa_ref[n] += lax.dot_general(ub, da_n, (((0,), (0,)), ((), ())),
                                          preferred_element_type=F32)
            dwi_ref[n] += lax.dot_general(ub, di_n, (((0,), (0,)), ((), ())),
                                          preferred_element_type=F32)
            du_parts.append(
                lax.dot_general(da_n, wa_ref[n], (((1,), (1,)), ((), ())), preferred_element_type=F32)
                + lax.dot_general(di_n, wi_ref[n], (((1,), (1,)), ((), ())), preferred_element_type=F32))
        du = d_iu * ig + jnp.concatenate(du_parts, axis=-1)
        off = SUBLANES - (CONV_WIDTH - 1)
        for tap in range(CONV_WIDTH):
            ds_ref[tap:tap + 1, :] += jnp.sum(du * upad_ref[off + tap:off + tap + tt, :],
                                              axis=0, keepdims=True)
        ds_ref[4:5, :] += jnp.sum(du, axis=0, keepdims=True)
        ds_ref[5:6, :] += jnp.sum(dpre_a, axis=0, keepdims=True)
        ds_ref[6:7, :] += jnp.sum(dpre_i, axis=0, keepdims=True)
        ds_ref[7:8, :] += dlam
        duext_ref[0:tt, :] = du
        du0 = jnp.zeros((tt, D), F32)
        for tap in range(CONV_WIDTH):
            sh = CONV_WIDTH - 1 - tap
            du0 = du0 + duext_ref[sh:sh + tt, :] * small_ref[tap:tap + 1, :]
        dp_ref[:, :D] = du0.astype(BF16)
        duext_ref[tt:, :] = du[0:SUBLANES, :]

    rev = lambda i: nt - 1 - i
    wspec = pl.BlockSpec((RNN_BLOCKS, W, W), lambda i: (0, 0, 0))
    return pl.pallas_call(
        body, name=name, grid=(nt,),
        in_specs=[pl.BlockSpec((tt, D), lambda i: (rev(i), 0)),
                  pl.BlockSpec((SUBLANES, D), lambda i: (jnp.maximum(rev(i) * hb - 1, 0), 0)),
                  pl.BlockSpec((tt, D), lambda i: (rev(i), 1)),
                  pl.BlockSpec((tt, D), lambda i: (rev(i), 0)),
                  pl.BlockSpec((SUBLANES, D), lambda i: (jnp.maximum(rev(i) * hb - 1, 0), 0)),
                  pl.BlockSpec((tt, D), lambda i: (rev(i), 0)),
                  pl.BlockSpec((SUBLANES, D), lambda i: (0, 0)),
                  wspec, wspec],
        out_specs=[pl.BlockSpec((tt, 2 * D), lambda i: (rev(i), 0)),
                   wspec, wspec, pl.BlockSpec((SUBLANES, D), lambda i: (0, 0))],
        out_shape=[jax.ShapeDtypeStruct((T, 2 * D), BF16),
                   jax.ShapeDtypeStruct((RNN_BLOCKS, W, W), F32),
                   jax.ShapeDtypeStruct((RNN_BLOCKS, W, W), F32),
                   jax.ShapeDtypeStruct((SUBLANES, D), F32)],
        scratch_shapes=[pltpu.VMEM((tt + SUBLANES, D), F32), pltpu.VMEM((tt + SUBLANES, D), F32),
                        pltpu.VMEM((tt, D), F32), pltpu.VMEM((tt, D), F32),
                        pltpu.VMEM((tt + SUBLANES, D), F32), pltpu.VMEM((1, D), F32)],
        compiler_params=_cparams("arbitrary"),
    )(proj, proj, proj, hs, hs, dy, small, wa, wi)


def _out_ln(a, w, x, g, b, *, tt, name):
    T, D = x.shape
    K = a.shape[1]

    def body(a_ref, w_ref, x_ref, g_ref, b_ref, y_ref, yb_ref, zh_ref, rs_ref):
        h = jnp.dot(a_ref[...].astype(BF16), w_ref[...].astype(BF16), preferred_element_type=F32)
        z = ALPHA * x_ref[...] + h
        mu = jnp.mean(z, axis=-1, keepdims=True)
        zc = z - mu
        rstd = lax.rsqrt(jnp.mean(zc * zc, axis=-1, keepdims=True) + LN_EPS)
        zh = zc * rstd
        zh_ref[...] = zh
        rs_ref[...] = rstd
        y = zh * g_ref[...] + b_ref[...]
        y_ref[...] = y
        yb_ref[...] = y.astype(BF16)

    blk = pl.BlockSpec((tt, D), lambda i: (i, 0))
    row = pl.BlockSpec((1, D), lambda i: (0, 0))
    return pl.pallas_call(
        body, name=name, grid=(T // tt,),
        in_specs=[pl.BlockSpec((tt, K), lambda i: (i, 0)), pl.BlockSpec((K, D), lambda i: (0, 0)),
                  blk, row, row],
        out_specs=[blk, blk, blk, pl.BlockSpec((tt, 1), lambda i: (i, 0))],
        out_shape=[jax.ShapeDtypeStruct((T, D), F32), jax.ShapeDtypeStruct((T, D), BF16),
                   jax.ShapeDtypeStruct((T, D), F32), jax.ShapeDtypeStruct((T, 1), F32)],
        compiler_params=_cparams("parallel"),
    )(a, w, x, g, b)


def _ln_bwd(dy, zh, rstd, g, *, tt, name):
    T, D = dy.shape

    def body(dy_ref, zh_ref, rs_ref, g_ref, dz_ref, dzb_ref, dg_ref, db_ref):
        @pl.when(pl.program_id(0) == 0)
        def _():
            dg_ref[...] = jnp.zeros_like(dg_ref)
            db_ref[...] = jnp.zeros_like(db_ref)

        dy = dy_ref[...]
        zh = zh_ref[...]
        dg_ref[...] += jnp.sum(dy * zh, axis=0, keepdims=True)
        db_ref[...] += jnp.sum(dy, axis=0, keepdims=True)
        dzh = dy * g_ref[...]
        m1 = jnp.mean(dzh, axis=-1, keepdims=True)
        m2 = jnp.mean(dzh * zh, axis=-1, keepdims=True)
        dz = rs_ref[...] * (dzh - m1 - zh * m2)
        dz_ref[...] = dz
        dzb_ref[...] = dz.astype(BF16)

    blk = pl.BlockSpec((tt, D), lambda i: (i, 0))
    row = pl.BlockSpec((1, D), lambda i: (0, 0))
    return pl.pallas_call(
        body, name=name, grid=(T // tt,),
        in_specs=[blk, blk, pl.BlockSpec((tt, 1), lambda i: (i, 0)), row],
        out_specs=[blk, blk, row, row],
        out_shape=[jax.ShapeDtypeStruct((T, D), F32), jax.ShapeDtypeStruct((T, D), BF16),
                   jax.ShapeDtypeStruct((1, D), F32), jax.ShapeDtypeStruct((1, D), F32)],
        compiler_params=_cparams("arbitrary"),
    )(dy, zh, rstd, g)


def _loss(y, tgt, *, tt, name):
    T, D = y.shape

    def body(y_ref, t_ref, l_ref, dy_ref):
        @pl.when(pl.program_id(0) == 0)
        def _():
            l_ref[...] = jnp.zeros_like(l_ref)

        e = y_ref[...] - t_ref[...]
        dy_ref[...] = e * (1.0 / D)
        l_ref[...] += jnp.sum(e * e, axis=0, keepdims=True) * (0.5 / D)

    blk = pl.BlockSpec((tt, D), lambda i: (i, 0))
    return pl.pallas_call(
        body, name=name, grid=(T // tt,),
        in_specs=[blk, blk], out_specs=[pl.BlockSpec((1, D), lambda i: (0, 0)), blk],
        out_shape=[jax.ShapeDtypeStruct((1, D), F32), jax.ShapeDtypeStruct((T, D), F32)],
        compiler_params=_cparams("arbitrary"),
    )(y, tgt)


def _row_tile(rows, target):
    best = SUBLANES
    for t in range(SUBLANES, target + 1, SUBLANES):
        if rows % t == 0:
            best = t
    return best


def _add_own(g, recv, c_idx, *, tr, name, narrow=False):
    _, M, R, C = g.shape

    def body(c_ref, g_ref, r_ref, *o_refs):
        s = g_ref[0] + r_ref[...]
        for o_ref in o_refs:
            o_ref[...] = s.astype(o_ref.dtype)

    blk = pl.BlockSpec((1, tr, C), lambda k, i, c: (k, i, 0))
    dtypes = [F32, BF16] if narrow else [F32]
    outs = pl.pallas_call(
        body, name=name,
        grid_spec=pltpu.PrefetchScalarGridSpec(
            num_scalar_prefetch=1, grid=(M, R // tr),
            in_specs=[pl.BlockSpec((1, 1, tr, C), lambda k, i, c: (c[0], k, i, 0)), blk],
            out_specs=[blk] * len(dtypes)),
        out_shape=[jax.ShapeDtypeStruct((M, R, C), d) for d in dtypes],
        compiler_params=_cparams("parallel", "parallel"),
    )(c_idx, g, recv)
    return outs if narrow else outs[0]


def _adamw_math(g, w_ref, m_ref, v_ref, g_ref, d_ref, nm_ref, nv_ref):
    nm = ADAM_B1 * m_ref[...] + (1.0 - ADAM_B1) * g
    nv = ADAM_B2 * v_ref[...] + (1.0 - ADAM_B2) * (g * g)
    m_hat = nm / (1.0 - ADAM_B1 ** ADAM_STEP)
    v_hat = nv / (1.0 - ADAM_B2 ** ADAM_STEP)
    g_ref[...] = g
    nm_ref[...] = nm
    nv_ref[...] = nv
    d_ref[...] = (-ADAM_LR) * (m_hat / (jnp.sqrt(v_hat) + ADAM_EPS) + ADAM_WD * w_ref[...])


def _adamw(parts, w, m, v, *, tr, name):
    n, R, C = parts.shape
    tr = min(tr, R)

    def body(p_ref, w_ref, m_ref, v_ref, *out_refs):
        g = p_ref[0]
        for k in range(1, n):
            g = g + p_ref[k]
        _adamw_math(g, w_ref, m_ref, v_ref, *out_refs)

    blk = pl.BlockSpec((tr, C), lambda i: (i, 0))
    out = jax.ShapeDtypeStruct((R, C), F32)
    return pl.pallas_call(
        body, name=name, grid=(R // tr,),
        in_specs=[pl.BlockSpec((n, tr, C), lambda i: (0, i, 0)), blk, blk, blk],
        out_specs=[blk, blk, blk, blk], out_shape=[out, out, out, out],
        compiler_params=_cparams("parallel"),
    )(parts, w, m, v)


def _adamw_shard(h, recv, me_idx, w, m, v, *, idx, prev, tr, name):
    _, _, R, C = h.shape
    n_prev = 0 if prev is None else 4

    def body(me_ref, h_ref, r1_ref, r2_ref, r3_ref, w_ref, m_ref, v_ref, *rest):
        g = ((h_ref[0] + r1_ref[0].astype(F32)) + r2_ref[0].astype(F32)) + r3_ref[0].astype(F32)
        _adamw_math(g, w_ref, m_ref, v_ref, *rest[n_prev:])

    blk = pl.BlockSpec((1, tr, C), lambda i, me: (idx, i, 0))

    def slot(d):
        return pl.BlockSpec((1, 1, tr, C), lambda i, me: (me[0] ^ d, 0, i, 0))

    out = jax.ShapeDtypeStruct(w.shape, F32)
    return pl.pallas_call(
        body, name=name,
        grid_spec=pltpu.PrefetchScalarGridSpec(
            num_scalar_prefetch=1, grid=(R // tr,),
            in_specs=[slot(0), slot(1), slot(2), slot(3), blk, blk, blk]
            + [pl.BlockSpec(memory_space=pl.ANY)] * n_prev,
            out_specs=[blk, blk, blk, blk]),
        out_shape=[out, out, out, out],
        input_output_aliases={8 + j: j for j in range(n_prev)},
        compiler_params=_cparams("parallel"),
    )(me_idx, h, recv, recv, recv, w, m, v, *(prev or ()))


SHARD_AXIS = dict(attn_w_in=1, attn_w_out=0, rnn_w_in=1, rnn_w_out=0, rnn_w_a=1, rnn_w_i=1,
                  rnn_conv_w=1, rnn_conv_b=0, rnn_b_a=0, rnn_b_i=0, rnn_lambda=0)
RNN_ROWED = ("rnn_w_out", "rnn_w_a", "rnn_w_i")
SMALL = ("rnn_conv_w", "rnn_conv_b", "rnn_b_a", "rnn_b_i", "rnn_lambda")
PACK_C = 1024


def _elems(shape):
    n = 1
    for s in shape:
        n *= s
    return n


def _pack_rows(p, idx, dtype):
    parts = [p[k][idx].astype(dtype).reshape(-1, PACK_C) for k in RNN_ROWED]
    small = jnp.concatenate([p[k][idx].reshape(-1) for k in SMALL])
    tile_rows = SUBLANES * (4 // jnp.dtype(dtype).itemsize)
    if dtype == BF16:
        small = lax.bitcast_convert_type(small, BF16)
    small = small.reshape(-1, PACK_C)
    parts.append(jnp.pad(small, ((0, tile_rows - small.shape[0]), (0, 0))))
    return jnp.concatenate(parts, axis=0)


def _unpack_rows(flat, shapes):
    out, r = {}, 0
    for k in RNN_ROWED:
        n = _elems(shapes[k]) // PACK_C
        out[k] = flat[r:r + n].reshape(shapes[k])
        r += n
    n_small = sum(_elems(shapes[k]) for k in SMALL)
    small = flat[r:r + n_small // PACK_C].reshape(-1)
    o = 0
    for k in SMALL:
        n = _elems(shapes[k])
        out[k] = small[o:o + n].reshape(shapes[k])
        o += n
    return out


def _join_columns(g, width, *, tr, name):
    _, _, R, S = g.shape

    def body(*refs):
        o_ref = refs[8]
        parts = [refs[r][0, 0].astype(F32) for r in range(8)]
        parts.append(jnp.zeros((tr, width - 8 * S), F32))
        o_ref[...] = jnp.concatenate(parts, axis=-1).astype(o_ref.dtype)

    def shard(r):
        return pl.BlockSpec((1, 1, tr, S), lambda i: (r % 2, r // 2, i, 0))

    return pl.pallas_call(
        body, name=name, grid=(R // tr,),
        in_specs=[shard(r) for r in range(8)],
        out_specs=pl.BlockSpec((tr, width), lambda i: (i, 0)),
        out_shape=jax.ShapeDtypeStruct((R, width), g.dtype),
        compiler_params=_cparams("parallel"),
    )(*([g] * 8))


def _split_columns(parts, S, *, tr, name):
    R = parts[0].shape[0]
    n = len(parts)

    def body(*refs):
        o_ref = refs[n]
        x = jnp.concatenate([r[...] for r in refs[:n]], axis=1)
        for r in range(8):
            o_ref[r % 2, r // 2] = x[:, r * S:(r + 1) * S]

    return pl.pallas_call(
        body, name=name, grid=(R // tr,),
        in_specs=[pl.BlockSpec((tr, p.shape[1]), lambda i: (i, 0)) for p in parts],
        out_specs=pl.BlockSpec((2, 4, tr, S), lambda i: (0, 0, i, 0)),
        out_shape=jax.ShapeDtypeStruct((2, 4, R, S), parts[0].dtype),
        compiler_params=_cparams("parallel"),
    )(*parts)


def _to_full(g, k, sh):
    ax, nd = SHARD_AXIS[k], len(sh)
    perm = tuple(range(2, 2 + ax)) + (1, 0) + tuple(range(2 + ax, 2 + nd))
    return g.transpose(perm).reshape(sh[:ax] + (8 * sh[ax],) + sh[ax + 1:])


def _from_full(full, k, sh):
    ax, nd = SHARD_AXIS[k], len(sh)
    t = full.reshape(sh[:ax] + (4, 2, sh[ax]) + sh[ax + 1:])
    return t.transpose((ax + 1, ax) + tuple(range(ax)) + tuple(range(ax + 2, nd + 2)))


def _unpack_gathered_rows(g, shapes):
    out, r = {}, 0
    for k in RNN_ROWED:
        n = _elems(shapes[k]) // PACK_C
        out[k] = _to_full(g[:, :, r:r + n].reshape((2, 4) + shapes[k]), k, shapes[k])
        r += n
    n_small = sum(_elems(shapes[k]) for k in SMALL)
    nr = 2 * n_small // PACK_C
    small = lax.bitcast_convert_type(g[:, :, r:r + nr].reshape(2, 4, n_small, 2), F32)
    o = 0
    for k in SMALL:
        n = _elems(shapes[k])
        out[k] = _to_full(small[:, :, o:o + n].reshape((2, 4) + shapes[k]), k, shapes[k])
        o += n
    return out


def _pack_grad_rows(full, shapes):
    parts = [_from_full(full[k], k, shapes[k]).reshape(2, 4, -1, PACK_C) for k in RNN_ROWED]
    small = jnp.concatenate(
        [_from_full(full[k], k, shapes[k]).reshape(2, 4, -1) for k in SMALL], axis=-1)
    small = small.reshape(2, 4, -1, PACK_C)
    parts.append(jnp.pad(small, ((0, 0), (0, 0), (0, SUBLANES - small.shape[2]), (0, 0))))
    return jnp.concatenate(parts, axis=2)


def kernel(x, ln_g, ln_b, attn_w_in, attn_b_f, attn_w_out, rnn_w_in, rnn_conv_w, rnn_conv_b, rnn_w_a, rnn_b_a, rnn_w_i, rnn_b_i, rnn_lambda, rnn_w_out, loss_target, m_ln_g, m_ln_b, m_attn_w_in, m_attn_b_f, m_attn_w_out, m_rnn_w_in, m_rnn_conv_w, m_rnn_conv_b, m_rnn_w_a, m_rnn_b_a, m_rnn_w_i, m_rnn_b_i, m_rnn_lambda, m_rnn_w_out, v_ln_g, v_ln_b, v_attn_w_in, v_attn_b_f, v_attn_w_out, v_rnn_w_in, v_rnn_conv_w, v_rnn_conv_b, v_rnn_w_a, v_rnn_b_a, v_rnn_w_i, v_rnn_b_i, v_rnn_lambda, v_rnn_w_out):
    w_loc = dict(attn_w_in=attn_w_in, attn_w_out=attn_w_out, rnn_w_in=rnn_w_in, rnn_w_a=rnn_w_a,
                 rnn_w_i=rnn_w_i, rnn_w_out=rnn_w_out, rnn_conv_w=rnn_conv_w, rnn_conv_b=rnn_conv_b,
                 rnn_b_a=rnn_b_a, rnn_b_i=rnn_b_i, rnn_lambda=rnn_lambda)
    m_loc = dict(attn_w_in=m_attn_w_in, attn_w_out=m_attn_w_out, rnn_w_in=m_rnn_w_in,
                 rnn_w_a=m_rnn_w_a, rnn_w_i=m_rnn_w_i, rnn_w_out=m_rnn_w_out,
                 rnn_conv_w=m_rnn_conv_w, rnn_conv_b=m_rnn_conv_b, rnn_b_a=m_rnn_b_a,
                 rnn_b_i=m_rnn_b_i, rnn_lambda=m_rnn_lambda)
    v_loc = dict(attn_w_in=v_attn_w_in, attn_w_out=v_attn_w_out, rnn_w_in=v_rnn_w_in,
                 rnn_w_a=v_rnn_w_a, rnn_w_i=v_rnn_w_i, rnn_w_out=v_rnn_w_out,
                 rnn_conv_w=v_rnn_conv_w, rnn_conv_b=v_rnn_conv_b, rnn_b_a=v_rnn_b_a,
                 rnn_b_i=v_rnn_b_i, rnn_lambda=v_rnn_lambda)
    shapes = {k: tuple(a.shape[1:]) for k, a in w_loc.items()}
    T, D = x.shape[1], x.shape[2]
    n_f = attn_b_f.shape[1]
    tb = min(1024, T)
    tb_bwd = min(512, T)
    tt_rg = min(128, T)
    tt_ln = min(256, T)
    c_idx = lax.axis_index("c").astype(jnp.int32).reshape(1)
    me_idx = (2 * lax.axis_index("x") + lax.axis_index("y")).astype(jnp.int32).reshape(1)

    def attn_w_in_full(g_in, idx):
        return _join_columns(g_in, 4 * D + LANES, tr=256, name=f"a_join{idx}")

    def attn_w_out_full(g_out):
        return _to_full(g_out, "attn_w_out", shapes["attn_w_out"])

    def rnn_weights(g_in, g_rows):
        w = _unpack_gathered_rows(g_rows, shapes)
        w["rnn_w_in"] = _to_full(g_in, "rnn_w_in", shapes["rnn_w_in"])
        w["small"] = jnp.concatenate([w["rnn_conv_w"], w["rnn_conv_b"][None], w["rnn_b_a"][None],
                                      w["rnn_b_i"][None], w["rnn_lambda"][None]])
        return w

    g0 = _ag_c(_run_exchange(_Exchange("gather", [attn_w_in[0].astype(BF16)]), "ag_w0_xy"),
               "ag_w0_c")
    later = _Exchange("gather", [
        attn_w_out.astype(BF16), attn_w_in[1].astype(BF16), rnn_w_in.astype(BF16),
        jnp.stack([_pack_rows(w_loc, i, BF16) for i in range(2)])])
    w_attn_in, w_attn_out, w_rnn = [attn_w_in_full(g0[0], 0), None], [None, None], [None, None]
    bf_rows = jnp.pad(attn_b_f, ((0, 0), (0, LANES - n_f)))[:, None, :]

    xs, xb, saved = [x[0]], [x[0]], []
    for layer in range(DEPTH):
        idx, xl, xm = layer // 2, xs[-1], xb[-1]
        if layer % 2 == 0:
            proj = _matmul(xm, w_attn_in[idx], trans_b=False, tm=512, tn=1408,
                           name=f"a_proj{layer}")
            cum_t = _cumsum_fwd(proj, bf_rows[idx], tt=min(512, T), name=f"a_cum{layer}")
            cum2 = cum_t[:N_HEADS].reshape(N_PAIRS, 2, T)
            o, og, lp, *got = _flash_fwd(proj, cum2.reshape(N_PAIRS, 2, T // tb, tb), tb=tb,
                                         name=f"a_fwd{layer}", host=later if layer == 0 else None)
            cum4 = cum2.reshape(N_PAIRS, 2, T // tb_bwd, tb_bwd)
            if layer == 0:
                g1 = _ag_c(got, "ag_w1_c")
                w_attn_out = [attn_w_out_full(g1[0][:, :, i]) for i in range(2)]
                w_attn_in[1] = attn_w_in_full(g1[1], 1)
                w_rnn = [rnn_weights(g1[2][:, :, i], g1[3][:, :, i]) for i in range(2)]
            branch, w_out = og, w_attn_out[idx]
            saved.append((proj, cum4, o, og, lp))
        else:
            w = w_rnn[idx]
            proj = _matmul(xm, w["rnn_w_in"], trans_b=False, tm=512, tn=1024,
                           name=f"r_proj{layer}")
            hs, yr = _rg_fwd(proj, w["small"], w["rnn_w_a"], w["rnn_w_i"], tt=tt_rg,
                             name=f"r_fwd{layer}")
            branch, w_out = yr, w["rnn_w_out"]
            saved.append((proj, hs, yr))
        y, yb, zh, rstd = _out_ln(branch, w_out, xl, ln_g[layer][None], ln_b[layer][None],
                                  tt=512, name=f"out_ln{layer}")
        saved[-1] = saved[-1] + (zh, rstd)
        xs.append(y)
        xb.append(yb)

    loss_lanes, dy = _loss(xs[-1], loss_target[0], tt=tt_ln, name="loss")
    loss = lax.psum(jnp.sum(loss_lanes), ("x", "y", "c"))

    def reduce_pair(gs, layer, narrow=False):
        recv = _rs_c(gs, f"rs_c{layer}")
        outs = [_add_own(g, r, c_idx, tr=_row_tile(g.shape[2], 512), name=f"rs_add{layer}_{n}",
                         narrow=narrow) for n, (g, r) in enumerate(zip(gs, recv))]
        if narrow:
            return [o[0][:, None] for o in outs], [o[1][:, None] for o in outs]
        return [o[:, None] for o in outs]

    half, quad = [None] * DEPTH, [None] * DEPTH
    d_ln_g, d_ln_b, d_bf = [None] * DEPTH, [None] * DEPTH, [None, None]
    for layer in reversed(range(DEPTH)):
        idx, xm = layer // 2, xb[layer]
        zh, rstd = saved[layer][-2:]
        dz, dzb, dg, db = _ln_bwd(dy, zh, rstd, ln_g[layer][None], tt=tt_ln,
                                  name=f"ln_bwd{layer}")
        d_ln_g[layer], d_ln_b[layer] = dg[0], db[0]
        if layer % 2 == 0:
            w_in, w_out = w_attn_in[idx], w_attn_out[idx]
            proj, cum4, o, og, lp = saved[layer][:5]
            dog = _matmul(dzb, w_out, trans_b=True, tm=512, tn=1024, name=f"a_dog{layer}")
            dwo = _matmul_tn(og, dzb, tm=512, tn=1024, tk=1024, name=f"a_dwo{layer}")
            riders = [l for l in range(layer + 1, DEPTH) if quad[l] is None]
            host = _Exchange("scatter", [h for l in riders for h in half[l]]) if riders else None
            dq, dgate, dk, dv, dcum_q, dcum_k, *got = _flash_bwd(proj, cum4, o, dog, lp, tb=tb_bwd,
                                                                 name=f"a_bwd{layer}", host=host)
            for l in riders:
                quad[l], got = got[:len(half[l])], got[len(half[l]):]
            dcum_t = (dcum_q.transpose(0, 2, 1, 3) + dcum_k).reshape(N_HEADS, T)
            dcum_t = jnp.pad(dcum_t, ((0, LANES - N_HEADS), (0, 0)))
            df, dbf = _cumsum_bwd(dcum_t, proj, bf_rows[idx], tt=min(512, T), name=f"a_dcum{layer}")
            d_bf[idx] = dbf[0, :n_f]
            dproj = [dq, dk, dv, dgate, df]
            dwi = _matmul_tn_parts(xm, dproj, tm=512, tk=1024, name=f"a_dwi{layer}")
            gs = [_split_columns(dwi, shapes["attn_w_in"][1], tr=256, name=f"a_split{layer}"),
                  _from_full(dwo, "attn_w_out", shapes["attn_w_out"])]
            if layer > 0:
                half[layer] = reduce_pair(gs, layer)
                dy = _matmul(dproj, w_in, trans_b=True, tm=512, tn=1024, name=f"a_dx{layer}",
                             add=dz, add_scale=ALPHA)
            else:
                half[layer], narrow = reduce_pair(gs, layer, narrow=True)
                dy, *quad[layer] = _matmul(dproj, w_in, trans_b=True, tm=512, tn=1024,
                                           name=f"a_dx{layer}", add=dz, add_scale=ALPHA,
                                           host=_Exchange("scatter", narrow))
        else:
            w = w_rnn[idx]
            proj, hs, yr = saved[layer][:3]
            dyr = _matmul(dzb, w["rnn_w_out"], trans_b=True, tm=512, tn=1024, name=f"r_dy{layer}")
            dwo = _matmul_tn(yr, dzb, tm=512, tn=1024, tk=1024, name=f"r_dwo{layer}")
            dproj, dwa, dwi_, dsm = _rg_bwd(proj, hs, dyr, w["small"], w["rnn_w_a"], w["rnn_w_i"],
                                            tt=tt_rg, name=f"r_bwd{layer}")
            dwin = _matmul_tn(xm, dproj, tm=512, tn=2048, tk=1024, name=f"r_dwi{layer}")
            dy = _matmul(dproj, w["rnn_w_in"], trans_b=True, tm=512, tn=1024, name=f"r_dx{layer}",
                         add=dz, add_scale=ALPHA)
            full = dict(rnn_w_out=dwo, rnn_w_a=dwa, rnn_w_i=dwi_, rnn_conv_w=dsm[0:4],
                        rnn_conv_b=dsm[4], rnn_b_a=dsm[5], rnn_b_i=dsm[6], rnn_lambda=dsm[7])
            half[layer] = reduce_pair([_from_full(dwin, "rnn_w_in", shapes["rnn_w_in"]),
                                       _pack_grad_rows(full, shapes)], layer)
    grad_x = dy[None]

    def update(k, n):
        res = None
        for idx in (1, 0):
            layer = 2 * idx + (0 if k.startswith("attn") else 1)
            res = _adamw_shard(half[layer][n], quad[layer][n], me_idx, w_loc[k], m_loc[k], v_loc[k],
                               idx=idx, prev=res, tr=_row_tile(shapes[k][0], 256),
                               name=f"adamw_{k}{idx}")
        return res

    shard_outs = [dict() for _ in range(4)]
    for k, n in (("attn_w_in", 0), ("attn_w_out", 1), ("rnn_w_in", 0)):
        for j, a in enumerate(update(k, n)):
            shard_outs[j][k] = a
    rows = []
    for idx in range(2):
        layer = 2 * idx + 1
        wmv = [_pack_rows(d, idx, F32)[None] for d in (w_loc, m_loc, v_loc)]
        res = _adamw_shard(half[layer][1], quad[layer][1], me_idx, *wmv, idx=0, prev=None,
                           tr=_row_tile(wmv[0].shape[1], 256), name=f"adamw_rows{idx}")
        rows.append([_unpack_rows(a[0], shapes) for a in res])
    for j in range(4):
        for k in RNN_ROWED + SMALL:
            shard_outs[j][k] = jnp.stack([rows[0][j][k], rows[1][j][k]])
    g_sh, d_sh, nm_sh, nv_sh = shard_outs

    def rep_pack(lg, lb, bf):
        rows = jnp.concatenate([lg, lb, jnp.pad(bf.reshape(1, -1), ((0, 0), (0, D - 2 * n_f)))])
        return jnp.pad(rows, ((0, 16 - rows.shape[0]), (0, 0)))

    rep = _all_gather(rep_pack(jnp.stack(d_ln_g), jnp.stack(d_ln_b), jnp.stack(d_bf)), "ag_rep")
    rg, rd, rm, rv = _adamw(rep.reshape(8, 16, D), rep_pack(ln_g, ln_b, attn_b_f),
                            rep_pack(m_ln_g, m_ln_b, m_attn_b_f),
                            rep_pack(v_ln_g, v_ln_b, v_attn_b_f), tr=16, name="adamw_rep")

    def rep_unpack(a):
        return dict(ln_g=a[0:DEPTH], ln_b=a[DEPTH:2 * DEPTH],
                    attn_b_f=a[2 * DEPTH, :2 * n_f].reshape(2, n_f))

    order = ("ln_g", "ln_b", "attn_w_in", "attn_b_f", "attn_w_out", "rnn_w_in", "rnn_conv_w",
             "rnn_conv_b", "rnn_w_a", "rnn_b_a", "rnn_w_i", "rnn_b_i", "rnn_lambda", "rnn_w_out")
    outs = [loss, grad_x]
    for sh, rp in ((g_sh, rg), (d_sh, rd), (nm_sh, rm), (nv_sh, rv)):
        allp = {**sh, **rep_unpack(rp)}
        outs.extend(allp[k] for k in order)
    return tuple(outs)
```

```python
import functools

import jax
import jax.numpy as jnp
from jax import lax
from jax.experimental import pallas as pl
from jax.experimental.pallas import tpu as pltpu

F32 = jnp.float32
BF16 = jnp.bfloat16

DEPTH = 4
N_HEADS = 16
HEAD_DIM = 64
N_PAIRS = N_HEADS // 2
RNN_BLOCKS = 4
RNN_BLOCK_WIDTH = 256
CONV_WIDTH = 4
LRU_C = 8.0
ALPHA = (2.0 * DEPTH) ** 0.25
LN_EPS = 1e-5
ADAM_LR, ADAM_B1, ADAM_B2, ADAM_EPS, ADAM_WD, ADAM_STEP = 0.001, 0.9, 0.999, 1e-8, 0.01, 10

LANES = 128
SUBLANES = 8
VMEM_LIMIT = 48 * 1024 * 1024

MESH = pl.DeviceIdType.MESH
HBM_SPEC = pl.BlockSpec(memory_space=pltpu.HBM)


def _cparams(*sem):
    return pltpu.CompilerParams(dimension_semantics=sem, vmem_limit_bytes=VMEM_LIMIT)


def _sigmoid(x):
    return 1.0 / (1.0 + jnp.exp(-x))


def _softplus(x):
    return jnp.maximum(x, 0.0) + jnp.log(1.0 + jnp.exp(-jnp.abs(x)))


def _a2a(src, *, group, bcast, name):
    n = 2 if group == "c" else 4
    blk = tuple(src.shape) if bcast else tuple(src.shape[1:])

    def body(src_ref, out_ref, send_sems, recv_sems, local_sem):
        x, y, c = lax.axis_index("x"), lax.axis_index("y"), lax.axis_index("c")
        if group == "c":
            me = c

            def peer(d):
                return (x, y, 1 - c), 1 - c
        else:
            me = 2 * x + y

            def peer(d):
                px, py = x ^ (d >> 1), y ^ (d & 1)
                return (px, py, c), 2 * px + py

        def block_for(k):
            return src_ref if bcast else src_ref.at[k]

        local = pltpu.make_async_copy(block_for(me), out_ref.at[me], local_sem)
        local.start()
        sends = []
        for d in range(1, n):
            dev, idx = peer(d)
            cp = pltpu.make_async_remote_copy(
                src_ref=block_for(idx), dst_ref=out_ref.at[me],
                send_sem=send_sems.at[d], recv_sem=recv_sems.at[d],
                device_id=dev, device_id_type=MESH)
            cp.start()
            sends.append(cp)
        for d in range(1, n):
            dev, idx = peer(d)
            pltpu.make_async_remote_copy(
                src_ref=block_for(idx), dst_ref=out_ref.at[idx],
                send_sem=send_sems.at[d], recv_sem=recv_sems.at[d],
                device_id=dev, device_id_type=MESH).wait_recv()
        for cp in sends:
            cp.wait_send()
        local.wait()

    return pl.pallas_call(
        body, name=name,
        out_shape=jax.ShapeDtypeStruct((n,) + blk, src.dtype),
        in_specs=[HBM_SPEC], out_specs=HBM_SPEC,
        scratch_shapes=[pltpu.SemaphoreType.DMA((n,)), pltpu.SemaphoreType.DMA((n,)),
                        pltpu.SemaphoreType.DMA],
    )(src)


def _all_gather(piece, name):
    return _a2a(_a2a(piece, group="xy", bcast=True, name=name + "_xy"),
                group="c", bcast=True, name=name + "_c")


D2D_CHUNKS = 16
ICI_CHUNKS = 8


def _row_chunks(rows, dtype, k):
    unit = SUBLANES * (4 // jnp.dtype(dtype).itemsize)
    assert rows % unit == 0
    units = rows // unit
    k = max(1, min(k, units))
    base, rem = divmod(units, k)
    out, r = [], 0
    for i in range(k):
        n = (base + (1 if i < rem else 0)) * unit
        out.append((r, n))
        r += n
    return out


def _chunks(shape, dtype, k):
    if len(shape) == 2:
        return [(pl.ds(r0, n),) for r0, n in _row_chunks(shape[0], dtype, k)]
    per = max(1, k // shape[0])
    return [(l, pl.ds(r0, n)) for l in range(shape[0]) for r0, n in _row_chunks(shape[1], dtype, per)]


def _mesh_place():
    x, y, c = lax.axis_index("x"), lax.axis_index("y"), lax.axis_index("c")
    return x, y, c, 2 * x + y


def _chip_peer(x, y, c, d):
    px, py = x ^ (d >> 1), y ^ (d & 1)
    return (px, py, c), 2 * px + py


def _remote(src, dst, send_sem, recv_sem, dev):
    return pltpu.make_async_remote_copy(src_ref=src, dst_ref=dst, send_sem=send_sem,
                                        recv_sem=recv_sem, device_id=dev, device_id_type=MESH)


def _comm_call(body, name, ins, out_shapes, n_sems, aliases=None):
    n = len(ins)
    return pl.pallas_call(
        body, name=name,
        out_shape=out_shapes, in_specs=[HBM_SPEC] * n, out_specs=[HBM_SPEC] * n,
        input_output_aliases=aliases or {},
        scratch_shapes=[pltpu.SemaphoreType.DMA((n_sems, n)), pltpu.SemaphoreType.DMA((n_sems, n))],
    )(*ins)


class _Exchange:
    def __init__(self, kind, arrays):
        self.kind, self.arrays, self.n = kind, list(arrays), len(arrays)
        if kind == "gather":
            self.chunks = [_chunks(a.shape, a.dtype, ICI_CHUNKS) for a in arrays]
            self.out_shapes = [jax.ShapeDtypeStruct((2, 4) + tuple(a.shape), a.dtype) for a in arrays]
        else:
            lead, k = (1, ICI_CHUNKS) if kind == "scatter" else (2, ICI_CHUNKS // 4)
            self.chunks = [_chunks(a.shape[lead:], a.dtype, k) for a in arrays]
            self.out_shapes = [jax.ShapeDtypeStruct(a.shape, a.dtype) for a in arrays]
        self.peers = [p for p in range(1, 8 if kind == "scatter8" else 4)]
        n_sems = 8 if kind == "scatter8" else 4
        self.sem_shapes = [pltpu.SemaphoreType.DMA((n_sems, self.n)),
                           pltpu.SemaphoreType.DMA((n_sems, self.n))]

    def _peer(self, x, y, c, me, p):
        a, d = p // 4, p % 4
        px, py = x ^ (d >> 1), y ^ (d & 1)
        pc = 1 - c if a else c
        if self.kind == "scatter8":
            return (px, py, pc), (pc, 2 * px + py), (c, me)
        return (px, py, pc), (2 * px + py,), (me,)

    def _blocks(self, srcs, outs, o, c, me, theirs, mine):
        if self.kind == "gather":
            return srcs[o], outs[o].at[(c,) + mine], outs[o].at[(c,) + theirs]
        return srcs[o].at[theirs], outs[o].at[mine], outs[o].at[theirs]

    def start(self, srcs, outs, send_sems, recv_sems):
        x, y, c, me = _mesh_place()
        if self.kind == "gather":
            for o in range(self.n):
                for idx in self.chunks[o]:
                    pltpu.make_async_copy(srcs[o].at[idx], outs[o].at[(c, me) + idx],
                                          send_sems.at[0, o]).start()
        for p in self.peers:
            dev, theirs, mine = self._peer(x, y, c, me, p)
            for o in range(self.n):
                src, dst, _ = self._blocks(srcs, outs, o, c, me, theirs, mine)
                for idx in self.chunks[o]:
                    _remote(src.at[idx], dst.at[idx], send_sems.at[p, o], recv_sems.at[p, o],
                            dev).start()

    def wait(self, srcs, outs, send_sems, recv_sems):
        x, y, c, me = _mesh_place()
        for wait_recv in (True, False):
            for p in self.peers:
                dev, theirs, mine = self._peer(x, y, c, me, p)
                for o in range(self.n):
                    src, _, land = self._blocks(srcs, outs, o, c, me, theirs, mine)
                    cp = _remote(src, land, send_sems.at[p, o], recv_sems.at[p, o], dev)
                    cp.wait_recv() if wait_recv else cp.wait_send()
        if self.kind == "gather":
            for o in range(self.n):
                pltpu.make_async_copy(srcs[o], outs[o].at[c, me], send_sems.at[0, o]).wait()


def _run_exchange(ex, name):
    n = ex.n

    def body(*refs):
        srcs, outs, send_sems, recv_sems = refs[:n], refs[n:2 * n], refs[2 * n], refs[2 * n + 1]
        ex.start(srcs, outs, send_sems, recv_sems)
        ex.wait(srcs, outs, send_sems, recv_sems)

    return _comm_call(body, name, ex.arrays, ex.out_shapes, len(ex.peers) + 1)


def _ag_c(bufs, name):
    n = len(bufs)
    chunks = [_chunks(b.shape[2:], b.dtype, D2D_CHUNKS // 4) for b in bufs]

    def body(*refs):
        srcs, outs, send_sems, recv_sems = refs[:n], refs[n:2 * n], refs[2 * n], refs[2 * n + 1]
        x, y, c, _ = _mesh_place()
        sib = (x, y, 1 - c)
        for o in range(n):
            for k in range(4):
                for idx in chunks[o]:
                    _remote(srcs[o].at[(c, k) + idx], outs[o].at[(c, k) + idx],
                            send_sems.at[0, o], recv_sems.at[0, o], sib).start()
        for o in range(n):
            _remote(srcs[o].at[c], outs[o].at[1 - c], send_sems.at[0, o], recv_sems.at[0, o],
                    sib).wait_recv()
        for o in range(n):
            _remote(srcs[o].at[c], outs[o].at[1 - c], send_sems.at[0, o], recv_sems.at[0, o],
                    sib).wait_send()

    shapes = [jax.ShapeDtypeStruct(b.shape, b.dtype) for b in bufs]
    return _comm_call(body, name, bufs, shapes, 1, aliases={i: i for i in range(n)})


def _rs_c(gs, name):
    n = len(gs)
    chunks = [_chunks(g.shape[2:], g.dtype, max(1, D2D_CHUNKS // g.shape[1])) for g in gs]

    def body(*refs):
        srcs, outs, send_sems, recv_sems = refs[:n], refs[n:2 * n], refs[2 * n], refs[2 * n + 1]
        x, y, c, _ = _mesh_place()
        sib = (x, y, 1 - c)
        for o in range(n):
            for k in range(gs[o].shape[1]):
                for idx in chunks[o]:
                    _remote(srcs[o].at[(1 - c, k) + idx], outs[o].at[(k,) + idx],
                            send_sems.at[0, o], recv_sems.at[0, o], sib).start()
        for o in range(n):
            _remote(srcs[o].at[1 - c], outs[o], send_sems.at[0, o], recv_sems.at[0, o],
                    sib).wait_recv()
        for o in range(n):
            _remote(srcs[o].at[1 - c], outs[o], send_sems.at[0, o], recv_sems.at[0, o],
                    sib).wait_send()

    shapes = [jax.ShapeDtypeStruct(g.shape[1:], g.dtype) for g in gs]
    return _comm_call(body, name, gs, shapes, 1)


def _matmul(a, b, *, trans_b, tm, tn, name, add=None, add_scale=1.0, host=None):
    a_parts = list(a) if isinstance(a, (list, tuple)) else [a]
    M, K = a_parts[0].shape[0], sum(p.shape[1] for p in a_parts)
    N = b.shape[0] if trans_b else b.shape[1]
    tm, tn = min(tm, M), min(tn, N)
    assert M % tm == 0 and N % tn == 0
    dn = (((1,), (1,)), ((), ())) if trans_b else (((1,), (0,)), ((), ()))
    na = len(a_parts)

    def body(*refs):
        a_refs, b_ref, o_ref = refs[:na], refs[na], refs[-1]
        av = [r[...].astype(BF16) for r in a_refs]
        av = av[0] if na == 1 else jnp.concatenate(av, axis=1)
        r = lax.dot_general(av, b_ref[...].astype(BF16), dn, preferred_element_type=F32)
        if add is not None:
            r = r + add_scale * refs[na + 1][...]
        o_ref[...] = r

    b_spec = (pl.BlockSpec((tn, K), lambda j, i: (j, 0)) if trans_b
              else pl.BlockSpec((K, tn), lambda j, i: (0, j)))
    in_specs = [pl.BlockSpec((tm, p.shape[1]), lambda j, i: (i, 0)) for p in a_parts] + [b_spec]
    args = a_parts + [b]
    if add is not None:
        in_specs.append(pl.BlockSpec((tm, tn), lambda j, i: (i, j)))
        args.append(add)
    grid = (N // tn, M // tm)
    x_in, x_out, x_shapes, x_scratch, x_args = _host_specs(host)
    body = _hosted(body, len(args), 1, 0, host, grid)
    outs = pl.pallas_call(
        body, name=name, grid=grid,
        in_specs=in_specs + x_in,
        out_specs=[pl.BlockSpec((tm, tn), lambda j, i: (i, j))] + x_out,
        out_shape=[jax.ShapeDtypeStruct((M, N), F32)] + x_shapes,
        scratch_shapes=x_scratch,
        compiler_params=_cparams(*(("arbitrary",) * 2 if host else ("parallel",) * 2)),
    )(*args, *x_args)
    return outs if host else outs[0]


def _matmul_tn(a, b, *, tm, tn, tk, name):
    T, M = a.shape
    N = b.shape[1]
    tm, tn, tk = min(tm, M), min(tn, N), min(tk, T)
    assert M % tm == 0 and N % tn == 0 and T % tk == 0

    def body(a_ref, b_ref, o_ref):
        @pl.when(pl.program_id(2) == 0)
        def _():
            o_ref[...] = jnp.zeros_like(o_ref)

        o_ref[...] += lax.dot_general(a_ref[...].astype(BF16), b_ref[...].astype(BF16),
                                      (((0,), (0,)), ((), ())), preferred_element_type=F32)

    return pl.pallas_call(
        body, name=name, grid=(M // tm, N // tn, T // tk),
        in_specs=[pl.BlockSpec((tk, tm), lambda i, j, k: (k, i)),
                  pl.BlockSpec((tk, tn), lambda i, j, k: (k, j))],
        out_specs=pl.BlockSpec((tm, tn), lambda i, j, k: (i, j)),
        out_shape=jax.ShapeDtypeStruct((M, N), F32),
        compiler_params=_cparams("parallel", "parallel", "arbitrary"),
    )(a, b)


def _matmul_tn_parts(a, parts, *, tm, tk, name):
    T, M = a.shape
    tm, tk = min(tm, M), min(tk, T)
    assert M % tm == 0 and T % tk == 0
    n = len(parts)

    def body(*refs):
        a_ref, b_refs, o_refs = refs[0], refs[1:1 + n], refs[1 + n:]
        av = a_ref[...].astype(BF16)
        for b_ref, o_ref in zip(b_refs, o_refs):
            @pl.when(pl.program_id(1) == 0)
            def _(o_ref=o_ref):
                o_ref[...] = jnp.zeros_like(o_ref)

            o_ref[...] += lax.dot_general(av, b_ref[...].astype(BF16), (((0,), (0,)), ((), ())),
                                          preferred_element_type=F32)

    return pl.pallas_call(
        body, name=name, grid=(M // tm, T // tk),
        in_specs=[pl.BlockSpec((tk, tm), lambda i, k: (k, i))]
        + [pl.BlockSpec((tk, p.shape[1]), lambda i, k: (k, 0)) for p in parts],
        out_specs=[pl.BlockSpec((tm, p.shape[1]), lambda i, k: (i, 0)) for p in parts],
        out_shape=[jax.ShapeDtypeStruct((M, p.shape[1]), F32) for p in parts],
        compiler_params=_cparams("parallel", "arbitrary"),
    )(a, *parts)


def _head_masks(rows):
    lane = lax.broadcasted_iota(jnp.int32, (rows, LANES), 1)
    return lane < HEAD_DIM, lane >= HEAD_DIM


def _causal(i_q, i_k, tq, tk):
    row = i_q * tq + lax.broadcasted_iota(jnp.int32, (tq, tk), 0)
    col = i_k * tk + lax.broadcasted_iota(jnp.int32, (tq, tk), 1)
    return row >= col


def _hosted(body, n_in, n_out, n_scratch, host, grid):
    if host is None:
        return body
    nx = host.n

    def wrapped(*refs):
        ins, xsrcs = refs[:n_in], refs[n_in:n_in + nx]
        outs = refs[n_in + nx:n_in + nx + n_out]
        xouts = refs[n_in + nx + n_out:n_in + 2 * nx + n_out]
        scratch = refs[n_in + 2 * nx + n_out:n_in + 2 * nx + n_out + n_scratch]
        xsems = refs[n_in + 2 * nx + n_out + n_scratch:]
        step = pl.program_id(0) * grid[1] + pl.program_id(1)

        @pl.when(step == 0)
        def _():
            host.start(xsrcs, xouts, *xsems)

        body(*ins, *outs, *scratch)

        @pl.when(step == grid[0] * grid[1] - 1)
        def _():
            host.wait(xsrcs, xouts, *xsems)

    return wrapped


def _host_specs(host):
    if host is None:
        return [], [], [], [], []
    return ([HBM_SPEC] * host.n, [HBM_SPEC] * host.n, host.out_shapes, host.sem_shapes, host.arrays)


def _flash_fwd(proj, cum4, *, tb, name, host=None):
    T = proj.shape[0]
    D = N_HEADS * HEAD_DIM
    nb = T // tb
    cb = D // LANES
    x_in, x_out, x_shapes, x_scratch, x_args = _host_specs(host)

    def body(q_ref, k_ref, v_ref, g_ref, cum_ref, o_ref, og_ref, lp_ref, kb_ref, vb_ref):
        i = pl.program_id(1)

        @pl.when(i == 0)
        def _():
            kb_ref[...] = k_ref[...].astype(BF16)
            vb_ref[...] = v_ref[...].astype(BF16)

        q = q_ref[...] * (HEAD_DIM ** -0.5)
        masks = _head_masks(tb)
        qh = [jnp.where(masks[h], q, 0.0).astype(BF16) for h in range(2)]
        cref = [cum_ref[0, h, pl.ds(i, 1), :][:, 0:1] for h in range(2)]

        def step(kbi, carry, masked):
            k0 = pl.multiple_of(kbi * tb, tb)
            kblk = kb_ref[pl.ds(k0, tb), :]
            vblk = vb_ref[pl.ds(k0, tb), :]
            new = []
            for h in range(2):
                m, l, acc = carry[h]
                s = lax.dot_general(qh[h], kblk, (((1,), (1,)), ((), ())),
                                    preferred_element_type=F32)
                s = s + (cref[h] - cum_ref[0, h, pl.ds(kbi, 1), :])
                if masked:
                    s = jnp.where(_causal(i, kbi, tb, tb), s, -jnp.inf)
                m_new = jnp.maximum(m, jnp.max(s, axis=-1, keepdims=True))
                alpha = jnp.exp(m - m_new)
                p = jnp.exp(s - m_new)
                l = alpha * l + jnp.sum(p, axis=-1, keepdims=True)
                acc = alpha * acc + jnp.dot(p.astype(BF16), vblk, preferred_element_type=F32)
                new.append((m_new, l, acc))
            return tuple(new)

        init1 = (jnp.full((tb, 1), -jnp.inf, F32), jnp.zeros((tb, 1), F32),
                 jnp.zeros((tb, LANES), F32))
        carry = lax.fori_loop(0, i, lambda kbi, c: step(kbi, c, False), (init1, init1))
        outs = []
        for h, (m, l, acc) in enumerate(step(i, carry, True)):
            outs.append(acc / l)
            lp_ref[h] = jnp.broadcast_to(m + jnp.log(l) - cref[h], (tb, LANES))
        o = jnp.where(masks[0], outs[0], outs[1])
        o_ref[...] = o
        gate = g_ref[...]
        og_ref[...] = (o * (gate * _sigmoid(gate))).astype(BF16)

    body = _hosted(body, 5, 3, 2, host, (N_PAIRS, nb))
    return pl.pallas_call(
        body, name=name, grid=(N_PAIRS, nb),
        in_specs=[pl.BlockSpec((tb, LANES), lambda j, i: (i, j)),
                  pl.BlockSpec((T, LANES), lambda j, i: (0, cb + j)),
                  pl.BlockSpec((T, LANES), lambda j, i: (0, 2 * cb + j)),
                  pl.BlockSpec((tb, LANES), lambda j, i: (i, 3 * cb + j)),
                  pl.BlockSpec((1, 2, nb, tb), lambda j, i: (j, 0, 0, 0))] + x_in,
        out_specs=[pl.BlockSpec((tb, LANES), lambda j, i: (i, j)),
                   pl.BlockSpec((tb, LANES), lambda j, i: (i, j)),
                   pl.BlockSpec((2, tb, LANES), lambda j, i: (j, i, 0))] + x_out,
        out_shape=[jax.ShapeDtypeStruct((T, D), F32), jax.ShapeDtypeStruct((T, D), BF16),
                   jax.ShapeDtypeStruct((N_HEADS, T, LANES), F32)] + x_shapes,
        scratch_shapes=[pltpu.VMEM((T, LANES), BF16), pltpu.VMEM((T, LANES), BF16)] + x_scratch,
        compiler_params=_cparams("arbitrary", "arbitrary"),
    )(proj, proj, proj, proj, cum4, *x_args)


def _flash_bwd_dq(proj, cum4, o, dog, lp, *, tb, name, host=None):
    T = proj.shape[0]
    D = N_HEADS * HEAD_DIM
    nb = T // tb
    cb = D // LANES
    x_in, x_out, x_shapes, x_scratch, x_args = _host_specs(host)

    def body(q_ref, k_ref, v_ref, g_ref, cum_ref, o_ref, dog_ref, lp_ref,
             dq_ref, dg_ref, do_ref, dl_ref, dc_ref, kb_ref, vb_ref):
        i = pl.program_id(1)

        @pl.when(i == 0)
        def _():
            kb_ref[...] = k_ref[...].astype(BF16)
            vb_ref[...] = v_ref[...].astype(BF16)

        gate = g_ref[...]
        sg = _sigmoid(gate)
        o = o_ref[...]
        dog = dog_ref[...]
        do = dog * (gate * sg)
        dg_ref[...] = (dog * o * (sg * (1.0 + gate * (1.0 - sg)))).astype(BF16)
        do_ref[...] = do.astype(BF16)
        q = q_ref[...] * (HEAD_DIM ** -0.5)
        masks = _head_masks(tb)
        qh = [jnp.where(masks[h], q, 0.0).astype(BF16) for h in range(2)]
        doh = [jnp.where(masks[h], do, 0.0).astype(BF16) for h in range(2)]
        delta = [jnp.sum(jnp.where(masks[h], do * o, 0.0), axis=-1, keepdims=True) for h in range(2)]
        lph = [lp_ref[h][:, 0:1] for h in range(2)]
        for h in range(2):
            dl_ref[h] = jnp.broadcast_to(delta[h], (tb, LANES))

        def step(kbi, carry, masked):
            k0 = pl.multiple_of(kbi * tb, tb)
            kblk = kb_ref[pl.ds(k0, tb), :]
            vblk = vb_ref[pl.ds(k0, tb), :]
            new = []
            for h in range(2):
                acc, rs = carry[h]
                s = lax.dot_general(qh[h], kblk, (((1,), (1,)), ((), ())), preferred_element_type=F32)
                p = jnp.exp(s - cum_ref[0, h, pl.ds(kbi, 1), :] - lph[h])
                if masked:
                    p = jnp.where(_causal(i, kbi, tb, tb), p, 0.0)
                dp = lax.dot_general(doh[h], vblk, (((1,), (1,)), ((), ())),
                                     preferred_element_type=F32)
                ds = p * (dp - delta[h])
                new.append((acc + jnp.dot(ds.astype(BF16), kblk, preferred_element_type=F32),
                            rs + jnp.sum(ds, axis=-1, keepdims=True)))
            return tuple(new)

        init1 = (jnp.zeros((tb, LANES), F32), jnp.zeros((tb, 1), F32))
        carry = lax.fori_loop(0, i, lambda kbi, c: step(kbi, c, False), (init1, init1))
        dqs = []
        for h, (acc, rs) in enumerate(step(i, carry, True)):
            dqs.append(acc)
            dc_ref[0, 0, pl.ds(h, 1), :] = jnp.broadcast_to(rs, (tb, LANES)).T[0:1, :]
        dq_ref[...] = (jnp.where(masks[0], dqs[0], dqs[1]) * (HEAD_DIM ** -0.5)).astype(BF16)

    blk = pl.BlockSpec((tb, LANES), lambda j, i: (i, j))
    stat = pl.BlockSpec((2, tb, LANES), lambda j, i: (j, i, 0))
    body = _hosted(body, 8, 5, 2, host, (N_PAIRS, nb))
    return pl.pallas_call(
        body, name=name, grid=(N_PAIRS, nb),
        in_specs=[blk,
                  pl.BlockSpec((T, LANES), lambda j, i: (0, cb + j)),
                  pl.BlockSpec((T, LANES), lambda j, i: (0, 2 * cb + j)),
                  pl.BlockSpec((tb, LANES), lambda j, i: (i, 3 * cb + j)),
                  pl.BlockSpec((1, 2, nb, tb), lambda j, i: (j, 0, 0, 0)),
                  blk, blk, stat] + x_in,
        out_specs=[blk, blk, blk, stat,
                   pl.BlockSpec((1, 1, 2, tb), lambda j, i: (j, i, 0, 0))] + x_out,
        out_shape=[jax.ShapeDtypeStruct((T, D), BF16), jax.ShapeDtypeStruct((T, D), BF16),
                   jax.ShapeDtypeStruct((T, D), BF16),
                   jax.ShapeDtypeStruct((N_HEADS, T, LANES), F32),
                   jax.ShapeDtypeStruct((N_PAIRS, nb, 2, tb), F32)] + x_shapes,
        scratch_shapes=[pltpu.VMEM((T, LANES), BF16), pltpu.VMEM((T, LANES), BF16)] + x_scratch,
        compiler_params=_cparams("arbitrary", "arbitrary"),
    )(proj, proj, proj, proj, cum4, o, dog, lp, *x_args)


def _flash_bwd_dkv(proj, cum4, do, lp, delta, *, tb, name):
    T = proj.shape[0]
    D = N_HEADS * HEAD_DIM
    nb = T // tb
    cb = D // LANES

    def body(q_ref, k_ref, v_ref, cum_ref, do_ref, lp_ref, dl_ref, dk_ref, dv_ref, dc_ref):
        kbi = pl.program_id(1)
        k = k_ref[...] * (HEAD_DIM ** -0.5)
        v = v_ref[...]
        masks = _head_masks(tb)
        kh = [jnp.where(masks[h], k, 0.0).astype(BF16) for h in range(2)]
        vh = [jnp.where(masks[h], v, 0.0).astype(BF16) for h in range(2)]
        ck = [cum_ref[0, h, pl.ds(kbi, 1), :] for h in range(2)]

        def step(i, carry, masked):
            q0 = pl.multiple_of(i * tb, tb)
            qb = q_ref[pl.ds(q0, tb), :].astype(BF16)
            dob = do_ref[pl.ds(q0, tb), :]
            new = []
            for h in range(2):
                dk, dv, dc = carry[h]
                s = lax.dot_general(qb, kh[h], (((1,), (1,)), ((), ())), preferred_element_type=F32)
                p = jnp.exp(s - ck[h] - lp_ref[h, pl.ds(q0, tb), :][:, 0:1])
                if masked:
                    p = jnp.where(_causal(i, kbi, tb, tb), p, 0.0)
                dp = lax.dot_general(dob, vh[h], (((1,), (1,)), ((), ())), preferred_element_type=F32)
                ds = p * (dp - dl_ref[h, pl.ds(q0, tb), :][:, 0:1])
                dv = dv + lax.dot_general(p.astype(BF16), dob, (((0,), (0,)), ((), ())),
                                          preferred_element_type=F32)
                dk = dk + lax.dot_general(ds.astype(BF16), qb, (((0,), (0,)), ((), ())),
                                          preferred_element_type=F32)
                new.append((dk, dv, dc - jnp.sum(ds, axis=0, keepdims=True)))
            return tuple(new)

        init1 = (jnp.zeros((tb, LANES), F32), jnp.zeros((tb, LANES), F32), jnp.zeros((1, tb), F32))
        carry = step(kbi, (init1, init1), True)
        carry = lax.fori_loop(kbi + 1, nb, lambda i, c: step(i, c, False), carry)
        dks, dvs = [], []
        for h, (dk, dv, dc) in enumerate(carry):
            dks.append(dk)
            dvs.append(dv)
            dc_ref[0, 0, pl.ds(h, 1), :] = dc
        dk_ref[...] = (jnp.where(masks[0], dks[0], dks[1]) * (HEAD_DIM ** -0.5)).astype(BF16)
        dv_ref[...] = jnp.where(masks[0], dvs[0], dvs[1]).astype(BF16)

    full = pl.BlockSpec((T, LANES), lambda j, i: (0, j))
    stat = pl.BlockSpec((2, T, LANES), lambda j, i: (j, 0, 0))
    blk = pl.BlockSpec((tb, LANES), lambda j, i: (i, j))
    return pl.pallas_call(
        body, name=name, grid=(N_PAIRS, nb),
        in_specs=[full,
                  pl.BlockSpec((tb, LANES), lambda j, i: (i, cb + j)),
                  pl.BlockSpec((tb, LANES), lambda j, i: (i, 2 * cb + j)),
                  pl.BlockSpec((1, 2, nb, tb), lambda j, i: (j, 0, 0, 0)),
                  full, stat, stat],
        out_specs=[blk, blk, pl.BlockSpec((1, 1, 2, tb), lambda j, i: (j, i, 0, 0))],
        out_shape=[jax.ShapeDtypeStruct((T, D), BF16), jax.ShapeDtypeStruct((T, D), BF16),
                   jax.ShapeDtypeStruct((N_PAIRS, nb, 2, tb), F32)],
        compiler_params=_cparams("parallel", "arbitrary"),
    )(proj, proj, proj, cum4, do, lp, delta)


def _flash_bwd(proj, cum4, o, dog, lp, *, tb, name, host=None):
    T = proj.shape[0]
    D = N_HEADS * HEAD_DIM
    nb = T // tb
    cb = D // LANES
    x_in, x_out, x_shapes, x_scratch, x_args = _host_specs(host)

    def body(q_ref, k_ref, v_ref, g_ref, cum_ref, o_ref, dog_ref, lp_ref,
             dq_ref, dg_ref, dk_ref, dv_ref, dcq_ref, dck_ref,
             kb_ref, vb_ref, dka_ref, dva_ref, dca_ref):
        i = pl.program_id(1)

        @pl.when(i == 0)
        def _():
            kb_ref[...] = k_ref[...].astype(BF16)
            vb_ref[...] = v_ref[...].astype(BF16)
            dka_ref[...] = jnp.zeros_like(dka_ref)
            dva_ref[...] = jnp.zeros_like(dva_ref)
            dca_ref[...] = jnp.zeros_like(dca_ref)

        gate = g_ref[...]
        sg = _sigmoid(gate)
        o = o_ref[...]
        dog = dog_ref[...]
        do = dog * (gate * sg)
        dg_ref[...] = (dog * o * (sg * (1.0 + gate * (1.0 - sg)))).astype(BF16)
        q = q_ref[...] * (HEAD_DIM ** -0.5)
        masks = _head_masks(tb)
        qh = [jnp.where(masks[h], q, 0.0).astype(BF16) for h in range(2)]
        doh = [jnp.where(masks[h], do, 0.0).astype(BF16) for h in range(2)]
        delta = [jnp.sum(jnp.where(masks[h], do * o, 0.0), axis=-1, keepdims=True) for h in range(2)]
        lph = [lp_ref[h][:, 0:1] for h in range(2)]

        def step(kbi, carry, masked):
            k0 = pl.multiple_of(kbi * tb, tb)
            kblk = kb_ref[pl.ds(k0, tb), :]
            vblk = vb_ref[pl.ds(k0, tb), :]
            new, dk, dv = [], None, None
            for h in range(2):
                acc, rs = carry[h]
                s = lax.dot_general(qh[h], kblk, (((1,), (1,)), ((), ())), preferred_element_type=F32)
                p = jnp.exp(s - cum_ref[0, h, pl.ds(kbi, 1), :] - lph[h])
                if masked:
                    p = jnp.where(_causal(i, kbi, tb, tb), p, 0.0)
                dp = lax.dot_general(doh[h], vblk, (((1,), (1,)), ((), ())),
                                     preferred_element_type=F32)
                ds = p * (dp - delta[h])
                pb, dsb = p.astype(BF16), ds.astype(BF16)
                dv_h = lax.dot_general(pb, doh[h], (((0,), (0,)), ((), ())),
                                       preferred_element_type=F32)
                dk_h = lax.dot_general(dsb, qh[h], (((0,), (0,)), ((), ())),
                                       preferred_element_type=F32)
                dv = dv_h if dv is None else dv + dv_h
                dk = dk_h if dk is None else dk + dk_h
                dca_ref[h, pl.ds(kbi, 1), :] -= jnp.sum(ds, axis=0, keepdims=True)
                new.append((acc + jnp.dot(dsb, kblk, preferred_element_type=F32),
                            rs + jnp.sum(ds, axis=-1, keepdims=True)))
            dka_ref[pl.ds(k0, tb), :] += dk
            dva_ref[pl.ds(k0, tb), :] += dv
            return tuple(new)

        init1 = (jnp.zeros((tb, LANES), F32), jnp.zeros((tb, 1), F32))
        carry = lax.fori_loop(0, i, lambda kbi, c: step(kbi, c, False), (init1, init1))
        dqs = []
        for h, (acc, rs) in enumerate(step(i, carry, True)):
            dqs.append(acc)
            dcq_ref[0, 0, pl.ds(h, 1), :] = jnp.broadcast_to(rs, (tb, LANES)).T[0:1, :]
        dq_ref[...] = (jnp.where(masks[0], dqs[0], dqs[1]) * (HEAD_DIM ** -0.5)).astype(BF16)

        @pl.when(i == nb - 1)
        def _():
            dk_ref[...] = dka_ref[...].astype(BF16)
            dv_ref[...] = dva_ref[...].astype(BF16)
            dck_ref[0] = dca_ref[...]

    blk = pl.BlockSpec((tb, LANES), lambda j, i: (i, j))
    full = pl.BlockSpec((T, LANES), lambda j, i: (0, j))
    body = _hosted(body, 8, 6, 5, host, (N_PAIRS, nb))
    return pl.pallas_call(
        body, name=name, grid=(N_PAIRS, nb),
        in_specs=[blk,
                  pl.BlockSpec((T, LANES), lambda j, i: (0, cb + j)),
                  pl.BlockSpec((T, LANES), lambda j, i: (0, 2 * cb + j)),
                  pl.BlockSpec((tb, LANES), lambda j, i: (i, 3 * cb + j)),
                  pl.BlockSpec((1, 2, nb, tb), lambda j, i: (j, 0, 0, 0)),
                  blk, blk, pl.BlockSpec((2, tb, LANES), lambda j, i: (j, i, 0))] + x_in,
        out_specs=[blk, blk, full, full,
                   pl.BlockSpec((1, 1, 2, tb), lambda j, i: (j, i, 0, 0)),
                   pl.BlockSpec((1, 2, nb, tb), lambda j, i: (j, 0, 0, 0))] + x_out,
        out_shape=[jax.ShapeDtypeStruct((T, D), BF16)] * 4
        + [jax.ShapeDtypeStruct((N_PAIRS, nb, 2, tb), F32),
           jax.ShapeDtypeStruct((N_PAIRS, 2, nb, tb), F32)] + x_shapes,
        scratch_shapes=[pltpu.VMEM((T, LANES), BF16), pltpu.VMEM((T, LANES), BF16),
                        pltpu.VMEM((T, LANES), F32), pltpu.VMEM((T, LANES), F32),
                        pltpu.VMEM((2, nb, tb), F32)] + x_scratch,
        compiler_params=_cparams("arbitrary", "arbitrary"),
    )(proj, proj, proj, proj, cum4, o, dog, lp, *x_args)


def _cumsum_fwd(proj, bf_row, *, tt, name):
    T = proj.shape[0]
    cb = (proj.shape[1] - LANES) // LANES

    def body(f_ref, b_ref, out_ref, carry_ref):
        i = pl.program_id(0)

        @pl.when(i == 0)
        def _():
            carry_ref[...] = jnp.zeros_like(carry_ref)

        ls = -_softplus(-(f_ref[...] + b_ref[...]))
        tri = (lax.broadcasted_iota(jnp.int32, (tt, tt), 0)
               >= lax.broadcasted_iota(jnp.int32, (tt, tt), 1)).astype(F32)
        cum = jnp.dot(tri, ls, preferred_element_type=F32,
                      precision=lax.Precision.HIGHEST) + carry_ref[...]
        carry_ref[...] = cum[tt - 1:tt, :]
        out_ref[...] = cum.T

    return pl.pallas_call(
        body, name=name, grid=(T // tt,),
        in_specs=[pl.BlockSpec((tt, LANES), lambda i: (i, cb)),
                  pl.BlockSpec((1, LANES), lambda i: (0, 0))],
        out_specs=pl.BlockSpec((LANES, tt), lambda i: (0, i)),
        out_shape=jax.ShapeDtypeStruct((LANES, T), F32),
        scratch_shapes=[pltpu.VMEM((1, LANES), F32)],
        compiler_params=_cparams("arbitrary"),
    )(proj, bf_row)


def _cumsum_bwd(dcum_t, proj, bf_row, *, tt, name):
    T = proj.shape[0]
    cb = (proj.shape[1] - LANES) // LANES
    nt = T // tt

    def body(dc_ref, f_ref, b_ref, df_ref, db_ref, carry_ref):
        i = pl.program_id(0)

        @pl.when(i == 0)
        def _():
            carry_ref[...] = jnp.zeros_like(carry_ref)
            db_ref[...] = jnp.zeros_like(db_ref)

        dc = dc_ref[...].T
        tri = (lax.broadcasted_iota(jnp.int32, (tt, tt), 0)
               <= lax.broadcasted_iota(jnp.int32, (tt, tt), 1)).astype(F32)
        rev = jnp.dot(tri, dc, preferred_element_type=F32,
                      precision=lax.Precision.HIGHEST) + carry_ref[...]
        carry_ref[...] = rev[0:1, :]
        df = rev * _sigmoid(-(f_ref[...] + b_ref[...]))
        df_ref[...] = df.astype(BF16)
        db_ref[...] += jnp.sum(df, axis=0, keepdims=True)

    return pl.pallas_call(
        body, name=name, grid=(nt,),
        in_specs=[pl.BlockSpec((LANES, tt), lambda i: (0, nt - 1 - i)),
                  pl.BlockSpec((tt, LANES), lambda i: (nt - 1 - i, cb)),
                  pl.BlockSpec((1, LANES), lambda i: (0, 0))],
        out_specs=[pl.BlockSpec((tt, LANES), lambda i: (nt - 1 - i, 0)),
                   pl.BlockSpec((1, LANES), lambda i: (0, 0))],
        out_shape=[jax.ShapeDtypeStruct((T, LANES), BF16), jax.ShapeDtypeStruct((1, LANES), F32)],
        scratch_shapes=[pltpu.VMEM((1, LANES), F32)],
        compiler_params=_cparams("arbitrary"),
    )(dcum_t, proj, bf_row)


def _rg_gates(upad_ref, small_ref, wa_ref, wi_ref, tt):
    off = SUBLANES - (CONV_WIDTH - 1)
    u = small_ref[4:5, :]
    for tap in range(CONV_WIDTH):
        u = u + upad_ref[off + tap:off + tap + tt, :] * small_ref[tap:tap + 1, :]
    pa, pi = [], []
    for n in range(RNN_BLOCKS):
        ub = u[:, n * RNN_BLOCK_WIDTH:(n + 1) * RNN_BLOCK_WIDTH].astype(BF16)
        pa.append(jnp.dot(ub, wa_ref[n], preferred_element_type=F32))
        pi.append(jnp.dot(ub, wi_ref[n], preferred_element_type=F32))
    r = _sigmoid(jnp.concatenate(pa, axis=-1) + small_ref[5:6, :])
    ig = _sigmoid(jnp.concatenate(pi, axis=-1) + small_ref[6:7, :])
    spl = _softplus(-small_ref[7:8, :])
    log_a = (-LRU_C) * r * spl
    a = jnp.exp(log_a)
    s = jnp.sqrt(jnp.tanh(-log_a) * (a * a + 1.0))
    return u, r, ig, spl, a, s


def _rg_fwd(proj, small, wa, wi, *, tt, name):
    T = proj.shape[0]
    D = RNN_BLOCKS * RNN_BLOCK_WIDTH
    hb = tt // SUBLANES

    def body(u0_ref, halo_ref, g_ref, small_ref, wa_ref, wi_ref, h_ref, y_ref,
             upad_ref, a_ref, b_ref, carry_ref):
        i = pl.program_id(0)

        @pl.when(i == 0)
        def _():
            carry_ref[...] = jnp.zeros_like(carry_ref)

        upad_ref[0:SUBLANES, :] = jnp.where(i == 0, 0.0, halo_ref[...])
        upad_ref[SUBLANES:, :] = u0_ref[...]
        u, r, ig, spl, a, s = _rg_gates(upad_ref, small_ref, wa_ref, wi_ref, tt)
        a_ref[...] = a
        b_ref[...] = s * (ig * u)

        def row(t, h):
            h = a_ref[pl.ds(t, 1), :] * h + b_ref[pl.ds(t, 1), :]
            h_ref[pl.ds(t, 1), :] = h
            return h

        carry_ref[...] = lax.fori_loop(0, tt, row, carry_ref[...], unroll=8)
        gate = g_ref[...]
        y_ref[...] = (h_ref[...] * (gate * _sigmoid(gate))).astype(BF16)

    return pl.pallas_call(
        body, name=name, grid=(T // tt,),
        in_specs=[pl.BlockSpec((tt, D), lambda i: (i, 0)),
                  pl.BlockSpec((SUBLANES, D), lambda i: (jnp.maximum(i * hb - 1, 0), 0)),
                  pl.BlockSpec((tt, D), lambda i: (i, 1)),
                  pl.BlockSpec((SUBLANES, D), lambda i: (0, 0)),
                  pl.BlockSpec((RNN_BLOCKS, RNN_BLOCK_WIDTH, RNN_BLOCK_WIDTH), lambda i: (0, 0, 0)),
                  pl.BlockSpec((RNN_BLOCKS, RNN_BLOCK_WIDTH, RNN_BLOCK_WIDTH), lambda i: (0, 0, 0))],
        out_specs=[pl.BlockSpec((tt, D), lambda i: (i, 0)), pl.BlockSpec((tt, D), lambda i: (i, 0))],
        out_shape=[jax.ShapeDtypeStruct((T, D), F32), jax.ShapeDtypeStruct((T, D), BF16)],
        scratch_shapes=[pltpu.VMEM((tt + SUBLANES, D), F32), pltpu.VMEM((tt, D), F32),
                        pltpu.VMEM((tt, D), F32), pltpu.VMEM((1, D), F32)],
        compiler_params=_cparams("arbitrary"),
    )(proj, proj, proj, small, wa, wi)


def _rg_bwd(proj, hs, dy, small, wa, wi, *, tt, name):
    T = proj.shape[0]
    D = RNN_BLOCKS * RNN_BLOCK_WIDTH
    W = RNN_BLOCK_WIDTH
    hb = tt // SUBLANES
    nt = T // tt

    def body(u0_ref, uhalo_ref, g_ref, h_ref, hhalo_ref, dy_ref, small_ref, wa_ref, wi_ref,
             dp_ref, dwa_ref, dwi_ref, ds_ref,
             upad_ref, hpad_ref, a_ref, g_s_ref, duext_ref, carry_ref):
        i = pl.program_id(0)
        first_chunk = i == nt - 1

        @pl.when(i == 0)
        def _():
            carry_ref[...] = jnp.zeros_like(carry_ref)
            duext_ref[...] = jnp.zeros_like(duext_ref)
            dwa_ref[...] = jnp.zeros_like(dwa_ref)
            dwi_ref[...] = jnp.zeros_like(dwi_ref)
            ds_ref[...] = jnp.zeros_like(ds_ref)

        upad_ref[0:SUBLANES, :] = jnp.where(first_chunk, 0.0, uhalo_ref[...])
        upad_ref[SUBLANES:, :] = u0_ref[...]
        hpad_ref[0:SUBLANES, :] = jnp.where(first_chunk, 0.0, hhalo_ref[...])
        hpad_ref[SUBLANES:, :] = h_ref[...]
        u, r, ig, spl, a, s = _rg_gates(upad_ref, small_ref, wa_ref, wi_ref, tt)
        gate = g_ref[...]
        sg = _sigmoid(gate)
        dy = dy_ref[...]
        dp_ref[:, D:] = (dy * h_ref[...] * (sg * (1.0 + gate * (1.0 - sg)))).astype(BF16)
        a_ref[...] = a
        g_s_ref[...] = dy * (gate * sg)

        def row(k, c):
            t = tt - 1 - k
            g = g_s_ref[pl.ds(t, 1), :] + c
            g_s_ref[pl.ds(t, 1), :] = g
            return a_ref[pl.ds(t, 1), :] * g

        carry_ref[...] = lax.fori_loop(0, tt, row, carry_ref[...], unroll=8)
        g = g_s_ref[...]
        h_prev = hpad_ref[SUBLANES - 1:SUBLANES - 1 + tt, :]
        iu = ig * u
        d_iu = g * s
        dlog_a = (g * h_prev) * a - (g * iu) * (a * a) / s
        dpre_a = (dlog_a * ((-LRU_C) * spl)) * r * (1.0 - r)
        dpre_i = (d_iu * u) * ig * (1.0 - ig)
        dlam = jnp.sum(dlog_a * r, axis=0, keepdims=True) * (LRU_C * _sigmoid(-small_ref[7:8, :]))
        du_parts = []
        for n in range(RNN_BLOCKS):
            sl = slice(n * W, (n + 1) * W)
            ub = u[:, sl].astype(BF16)
            da_n = dpre_a[:, sl].astype(BF16)
            di_n = dpre_i[:, sl].astype(BF16)
            dwa_ref[n] += lax.dot_general(ub, da_n, (((0,), (0,)), ((), ())),
                                          preferred_element_type=F32)
            dwi_ref[n] += lax.dot_general(ub, di_n, (((0,), (0,)), ((), ())),
                                          preferred_element_type=F32)
            du_parts.append(
                lax.dot_general(da_n, wa_ref[n], (((1,), (1,)), ((), ())), preferred_element_type=F32)
                + lax.dot_general(di_n, wi_ref[n], (((1,), (1,)), ((), ())), preferred_element_type=F32))
        du = d_iu * ig + jnp.concatenate(du_parts, axis=-1)
        off = SUBLANES - (CONV_WIDTH - 1)
        for tap in range(CONV_WIDTH):
            ds_ref[tap:tap + 1, :] += jnp.sum(du * upad_ref[off + tap:off + tap + tt, :],
                                              axis=0, keepdims=True)
        ds_ref[4:5, :] += jnp.sum(du, axis=0, keepdims=True)
        ds_ref[5:6, :] += jnp.sum(dpre_a, axis=0, keepdims=True)
        ds_ref[6:7, :] += jnp.sum(dpre_i, axis=0, keepdims=True)
        ds_ref[7:8, :] += dlam
        duext_ref[0:tt, :] = du
        du0 = jnp.zeros((tt, D), F32)
        for tap in range(CONV_WIDTH):
            sh = CONV_WIDTH - 1 - tap
            du0 = du0 + duext_ref[sh:sh + tt, :] * small_ref[tap:tap + 1, :]
        dp_ref[:, :D] = du0.astype(BF16)
        duext_ref[tt:, :] = du[0:SUBLANES, :]

    rev = lambda i: nt - 1 - i
    wspec = pl.BlockSpec((RNN_BLOCKS, W, W), lambda i: (0, 0, 0))
    return pl.pallas_call(
        body, name=name, grid=(nt,),
        in_specs=[pl.BlockSpec((tt, D), lambda i: (rev(i), 0)),
                  pl.BlockSpec((SUBLANES, D), lambda i: (jnp.maximum(rev(i) * hb - 1, 0), 0)),
                  pl.BlockSpec((tt, D), lambda i: (rev(i), 1)),
                  pl.BlockSpec((tt, D), lambda i: (rev(i), 0)),
                  pl.BlockSpec((SUBLANES, D), lambda i: (jnp.maximum(rev(i) * hb - 1, 0), 0)),
                  pl.BlockSpec((tt, D), lambda i: (rev(i), 0)),
                  pl.BlockSpec((SUBLANES, D), lambda i: (0, 0)),
                  wspec, wspec],
        out_specs=[pl.BlockSpec((tt, 2 * D), lambda i: (rev(i), 0)),
                   wspec, wspec, pl.BlockSpec((SUBLANES, D), lambda i: (0, 0))],
        out_shape=[jax.ShapeDtypeStruct((T, 2 * D), BF16),
                   jax.ShapeDtypeStruct((RNN_BLOCKS, W, W), F32),
                   jax.ShapeDtypeStruct((RNN_BLOCKS, W, W), F32),
                   jax.ShapeDtypeStruct((SUBLANES, D), F32)],
        scratch_shapes=[pltpu.VMEM((tt + SUBLANES, D), F32), pltpu.VMEM((tt + SUBLANES, D), F32),
                        pltpu.VMEM((tt, D), F32), pltpu.VMEM((tt, D), F32),
                        pltpu.VMEM((tt + SUBLANES, D), F32), pltpu.VMEM((1, D), F32)],
        compiler_params=_cparams("arbitrary"),
    )(proj, proj, proj, hs, hs, dy, small, wa, wi)


def _out_ln(a, w, x, g, b, *, tt, name):
    T, D = x.shape
    K = a.shape[1]

    def body(a_ref, w_ref, x_ref, g_ref, b_ref, y_ref, yb_ref, zh_ref, rs_ref):
        h = jnp.dot(a_ref[...].astype(BF16), w_ref[...].astype(BF16), preferred_element_type=F32)
        z = ALPHA * x_ref[...] + h
        mu = jnp.mean(z, axis=-1, keepdims=True)
        zc = z - mu
        rstd = lax.rsqrt(jnp.mean(zc * zc, axis=-1, keepdims=True) + LN_EPS)
        zh = zc * rstd
        zh_ref[...] = zh
        rs_ref[...] = rstd
        y = zh * g_ref[...] + b_ref[...]
        y_ref[...] = y
        yb_ref[...] = y.astype(BF16)

    blk = pl.BlockSpec((tt, D), lambda i: (i, 0))
    row = pl.BlockSpec((1, D), lambda i: (0, 0))
    return pl.pallas_call(
        body, name=name, grid=(T // tt,),
        in_specs=[pl.BlockSpec((tt, K), lambda i: (i, 0)), pl.BlockSpec((K, D), lambda i: (0, 0)),
                  blk, row, row],
        out_specs=[blk, blk, blk, pl.BlockSpec((tt, 1), lambda i: (i, 0))],
        out_shape=[jax.ShapeDtypeStruct((T, D), F32), jax.ShapeDtypeStruct((T, D), BF16),
                   jax.ShapeDtypeStruct((T, D), F32), jax.ShapeDtypeStruct((T, 1), F32)],
        compiler_params=_cparams("parallel"),
    )(a, w, x, g, b)


def _ln_bwd(dy, zh, rstd, g, *, tt, name):
    T, D = dy.shape

    def body(dy_ref, zh_ref, rs_ref, g_ref, dz_ref, dzb_ref, dg_ref, db_ref):
        @pl.when(pl.program_id(0) == 0)
        def _():
            dg_ref[...] = jnp.zeros_like(dg_ref)
            db_ref[...] = jnp.zeros_like(db_ref)

        dy = dy_ref[...]
        zh = zh_ref[...]
        dg_ref[...] += jnp.sum(dy * zh, axis=0, keepdims=True)
        db_ref[...] += jnp.sum(dy, axis=0, keepdims=True)
        dzh = dy * g_ref[...]
        m1 = jnp.mean(dzh, axis=-1, keepdims=True)
        m2 = jnp.mean(dzh * zh, axis=-1, keepdims=True)
        dz = rs_ref[...] * (dzh - m1 - zh * m2)
        dz_ref[...] = dz
        dzb_ref[...] = dz.astype(BF16)

    blk = pl.BlockSpec((tt, D), lambda i: (i, 0))
    row = pl.BlockSpec((1, D), lambda i: (0, 0))
    return pl.pallas_call(
        body, name=name, grid=(T // tt,),
        in_specs=[blk, blk, pl.BlockSpec((tt, 1), lambda i: (i, 0)), row],
        out_specs=[blk, blk, row, row],
        out_shape=[jax.ShapeDtypeStruct((T, D), F32), jax.ShapeDtypeStruct((T, D), BF16),
                   jax.ShapeDtypeStruct((1, D), F32), jax.ShapeDtypeStruct((1, D), F32)],
        compiler_params=_cparams("arbitrary"),
    )(dy, zh, rstd, g)


def _loss(y, tgt, *, tt, name):
    T, D = y.shape

    def body(y_ref, t_ref, l_ref, dy_ref):
        @pl.when(pl.program_id(0) == 0)
        def _():
            l_ref[...] = jnp.zeros_like(l_ref)

        e = y_ref[...] - t_ref[...]
        dy_ref[...] = e * (1.0 / D)
        l_ref[...] += jnp.sum(e * e, axis=0, keepdims=True) * (0.5 / D)

    blk = pl.BlockSpec((tt, D), lambda i: (i, 0))
    return pl.pallas_call(
        body, name=name, grid=(T // tt,),
        in_specs=[blk, blk], out_specs=[pl.BlockSpec((1, D), lambda i: (0, 0)), blk],
        out_shape=[jax.ShapeDtypeStruct((1, D), F32), jax.ShapeDtypeStruct((T, D), F32)],
        compiler_params=_cparams("arbitrary"),
    )(y, tgt)


def _row_tile(rows, target):
    best = SUBLANES
    for t in range(SUBLANES, target + 1, SUBLANES):
        if rows % t == 0:
            best = t
    return best


def _add_own(g, recv, c_idx, *, tr, name, narrow=False):
    _, M, R, C = g.shape

    def body(c_ref, g_ref, r_ref, *o_refs):
        s = g_ref[0] + r_ref[...]
        for o_ref in o_refs:
            o_ref[...] = s.astype(o_ref.dtype)

    blk = pl.BlockSpec((1, tr, C), lambda k, i, c: (k, i, 0))
    dtypes = [F32, BF16] if narrow else [F32]
    outs = pl.pallas_call(
        body, name=name,
        grid_spec=pltpu.PrefetchScalarGridSpec(
            num_scalar_prefetch=1, grid=(M, R // tr),
            in_specs=[pl.BlockSpec((1, 1, tr, C), lambda k, i, c: (c[0], k, i, 0)), blk],
            out_specs=[blk] * len(dtypes)),
        out_shape=[jax.ShapeDtypeStruct((M, R, C), d) for d in dtypes],
        compiler_params=_cparams("parallel", "parallel"),
    )(c_idx, g, recv)
    return outs if narrow else outs[0]


def _adamw_math(g, w_ref, m_ref, v_ref, g_ref, d_ref, nm_ref, nv_ref):
    nm = ADAM_B1 * m_ref[...] + (1.0 - ADAM_B1) * g
    nv = ADAM_B2 * v_ref[...] + (1.0 - ADAM_B2) * (g * g)
    m_hat = nm / (1.0 - ADAM_B1 ** ADAM_STEP)
    v_hat = nv / (1.0 - ADAM_B2 ** ADAM_STEP)
    g_ref[...] = g
    nm_ref[...] = nm
    nv_ref[...] = nv
    d_ref[...] = (-ADAM_LR) * (m_hat / (jnp.sqrt(v_hat) + ADAM_EPS) + ADAM_WD * w_ref[...])


def _adamw(parts, w, m, v, *, tr, name):
    n, R, C = parts.shape
    tr = min(tr, R)

    def body(p_ref, w_ref, m_ref, v_ref, *out_refs):
        g = p_ref[0]
        for k in range(1, n):
            g = g + p_ref[k]
        _adamw_math(g, w_ref, m_ref, v_ref, *out_refs)

    blk = pl.BlockSpec((tr, C), lambda i: (i, 0))
    out = jax.ShapeDtypeStruct((R, C), F32)
    return pl.pallas_call(
        body, name=name, grid=(R // tr,),
        in_specs=[pl.BlockSpec((n, tr, C), lambda i: (0, i, 0)), blk, blk, blk],
        out_specs=[blk, blk, blk, blk], out_shape=[out, out, out, out],
        compiler_params=_cparams("parallel"),
    )(parts, w, m, v)


def _adamw_shard(parts, place, w, m, v, *, idx, prev, tr, name):
    R, C = parts[0][0].shape[-2:]
    n_parts, n_prev = len(parts), 0 if prev is None else 4

    def body(place_ref, *refs):
        p_refs, (w_ref, m_ref, v_ref) = refs[:n_parts], refs[n_parts:n_parts + 3]
        g = None
        for r in p_refs:
            blk = r[(0,) * (len(r.shape) - 3)].astype(F32)
            g = blk if g is None else g + blk
        _adamw_math(g, w_ref, m_ref, v_ref, *refs[n_parts + 3 + n_prev:])

    blk = pl.BlockSpec((1, tr, C), lambda i, s: (idx, i, 0))

    def part_spec(a, pick):
        return pl.BlockSpec((1,) * (a.ndim - 2) + (tr, C), lambda i, s: (*pick(s), i, 0))

    out = jax.ShapeDtypeStruct(w.shape, F32)
    return pl.pallas_call(
        body, name=name,
        grid_spec=pltpu.PrefetchScalarGridSpec(
            num_scalar_prefetch=1, grid=(R // tr,),
            in_specs=[part_spec(a, pick) for a, pick in parts] + [blk, blk, blk]
            + [pl.BlockSpec(memory_space=pl.ANY)] * n_prev,
            out_specs=[blk, blk, blk, blk]),
        out_shape=[out, out, out, out],
        input_output_aliases={1 + n_parts + 3 + j: j for j in range(n_prev)},
        compiler_params=_cparams("parallel"),
    )(place, *[a for a, _ in parts], w, m, v, *(prev or ()))


def _two_stage_parts(h, recv):
    return [(h, lambda s: (s[0], 0))] + [(recv, lambda s, d=d: (s[0] ^ d, 0)) for d in (1, 2, 3)]


def _direct_parts(g, recv):
    return [(g, lambda s: (s[1], s[0]))] + [
        (recv, lambda s, a=p // 4, d=p % 4: (s[1] ^ a, s[0] ^ d)) for p in range(1, 8)]


SHARD_AXIS = dict(attn_w_in=1, attn_w_out=0, rnn_w_in=1, rnn_w_out=0, rnn_w_a=1, rnn_w_i=1,
                  rnn_conv_w=1, rnn_conv_b=0, rnn_b_a=0, rnn_b_i=0, rnn_lambda=0)
RNN_ROWED = ("rnn_w_out", "rnn_w_a", "rnn_w_i")
SMALL = ("rnn_conv_w", "rnn_conv_b", "rnn_b_a", "rnn_b_i", "rnn_lambda")
PACK_C = 1024


def _elems(shape):
    n = 1
    for s in shape:
        n *= s
    return n


def _pack_rows(p, idx, dtype):
    parts = [p[k][idx].astype(dtype).reshape(-1, PACK_C) for k in RNN_ROWED]
    small = jnp.concatenate([p[k][idx].reshape(-1) for k in SMALL])
    tile_rows = SUBLANES * (4 // jnp.dtype(dtype).itemsize)
    if dtype == BF16:
        small = lax.bitcast_convert_type(small, BF16)
    small = small.reshape(-1, PACK_C)
    parts.append(jnp.pad(small, ((0, tile_rows - small.shape[0]), (0, 0))))
    return jnp.concatenate(parts, axis=0)


def _unpack_rows(flat, shapes):
    out, r = {}, 0
    for k in RNN_ROWED:
        n = _elems(shapes[k]) // PACK_C
        out[k] = flat[r:r + n].reshape(shapes[k])
        r += n
    n_small = sum(_elems(shapes[k]) for k in SMALL)
    small = flat[r:r + n_small // PACK_C].reshape(-1)
    o = 0
    for k in SMALL:
        n = _elems(shapes[k])
        out[k] = small[o:o + n].reshape(shapes[k])
        o += n
    return out


def _join_columns(g, width, *, tr, name):
    _, _, R, S = g.shape

    def body(*refs):
        o_ref = refs[8]
        parts = [refs[r][0, 0].astype(F32) for r in range(8)]
        parts.append(jnp.zeros((tr, width - 8 * S), F32))
        o_ref[...] = jnp.concatenate(parts, axis=-1).astype(o_ref.dtype)

    def shard(r):
        return pl.BlockSpec((1, 1, tr, S), lambda i: (r % 2, r // 2, i, 0))

    return pl.pallas_call(
        body, name=name, grid=(R // tr,),
        in_specs=[shard(r) for r in range(8)],
        out_specs=pl.BlockSpec((tr, width), lambda i: (i, 0)),
        out_shape=jax.ShapeDtypeStruct((R, width), g.dtype),
        compiler_params=_cparams("parallel"),
    )(*([g] * 8))


def _split_columns(parts, S, *, tr, name):
    R = parts[0].shape[0]
    n = len(parts)

    def body(*refs):
        o_ref = refs[n]
        x = jnp.concatenate([r[...] for r in refs[:n]], axis=1)
        for r in range(8):
            o_ref[r % 2, r // 2] = x[:, r * S:(r + 1) * S]

    return pl.pallas_call(
        body, name=name, grid=(R // tr,),
        in_specs=[pl.BlockSpec((tr, p.shape[1]), lambda i: (i, 0)) for p in parts],
        out_specs=pl.BlockSpec((2, 4, tr, S), lambda i: (0, 0, i, 0)),
        out_shape=jax.ShapeDtypeStruct((2, 4, R, S), parts[0].dtype),
        compiler_params=_cparams("parallel"),
    )(*parts)


def _to_full(g, k, sh):
    ax, nd = SHARD_AXIS[k], len(sh)
    perm = tuple(range(2, 2 + ax)) + (1, 0) + tuple(range(2 + ax, 2 + nd))
    return g.transpose(perm).reshape(sh[:ax] + (8 * sh[ax],) + sh[ax + 1:])


def _from_full(full, k, sh):
    ax, nd = SHARD_AXIS[k], len(sh)
    t = full.reshape(sh[:ax] + (4, 2, sh[ax]) + sh[ax + 1:])
    return t.transpose((ax + 1, ax) + tuple(range(ax)) + tuple(range(ax + 2, nd + 2)))


def _unpack_gathered_rows(g, shapes):
    out, r = {}, 0
    for k in RNN_ROWED:
        n = _elems(shapes[k]) // PACK_C
        out[k] = _to_full(g[:, :, r:r + n].reshape((2, 4) + shapes[k]), k, shapes[k])
        r += n
    n_small = sum(_elems(shapes[k]) for k in SMALL)
    nr = 2 * n_small // PACK_C
    small = lax.bitcast_convert_type(g[:, :, r:r + nr].reshape(2, 4, n_small, 2), F32)
    o = 0
    for k in SMALL:
        n = _elems(shapes[k])
        out[k] = _to_full(small[:, :, o:o + n].reshape((2, 4) + shapes[k]), k, shapes[k])
        o += n
    return out


def _pack_grad_rows(full, shapes):
    parts = [_from_full(full[k], k, shapes[k]).reshape(2, 4, -1, PACK_C) for k in RNN_ROWED]
    small = jnp.concatenate(
        [_from_full(full[k], k, shapes[k]).reshape(2, 4, -1) for k in SMALL], axis=-1)
    small = small.reshape(2, 4, -1, PACK_C)
    parts.append(jnp.pad(small, ((0, 0), (0, 0), (0, SUBLANES - small.shape[2]), (0, 0))))
    return jnp.concatenate(parts, axis=2)


def kernel(x, ln_g, ln_b, attn_w_in, attn_b_f, attn_w_out, rnn_w_in, rnn_conv_w, rnn_conv_b, rnn_w_a, rnn_b_a, rnn_w_i, rnn_b_i, rnn_lambda, rnn_w_out, loss_target, m_ln_g, m_ln_b, m_attn_w_in, m_attn_b_f, m_attn_w_out, m_rnn_w_in, m_rnn_conv_w, m_rnn_conv_b, m_rnn_w_a, m_rnn_b_a, m_rnn_w_i, m_rnn_b_i, m_rnn_lambda, m_rnn_w_out, v_ln_g, v_ln_b, v_attn_w_in, v_attn_b_f, v_attn_w_out, v_rnn_w_in, v_rnn_conv_w, v_rnn_conv_b, v_rnn_w_a, v_rnn_b_a, v_rnn_w_i, v_rnn_b_i, v_rnn_lambda, v_rnn_w_out):
    w_loc = dict(attn_w_in=attn_w_in, attn_w_out=attn_w_out, rnn_w_in=rnn_w_in, rnn_w_a=rnn_w_a,
                 rnn_w_i=rnn_w_i, rnn_w_out=rnn_w_out, rnn_conv_w=rnn_conv_w, rnn_conv_b=rnn_conv_b,
                 rnn_b_a=rnn_b_a, rnn_b_i=rnn_b_i, rnn_lambda=rnn_lambda)
    m_loc = dict(attn_w_in=m_attn_w_in, attn_w_out=m_attn_w_out, rnn_w_in=m_rnn_w_in,
                 rnn_w_a=m_rnn_w_a, rnn_w_i=m_rnn_w_i, rnn_w_out=m_rnn_w_out,
                 rnn_conv_w=m_rnn_conv_w, rnn_conv_b=m_rnn_conv_b, rnn_b_a=m_rnn_b_a,
                 rnn_b_i=m_rnn_b_i, rnn_lambda=m_rnn_lambda)
    v_loc = dict(attn_w_in=v_attn_w_in, attn_w_out=v_attn_w_out, rnn_w_in=v_rnn_w_in,
                 rnn_w_a=v_rnn_w_a, rnn_w_i=v_rnn_w_i, rnn_w_out=v_rnn_w_out,
                 rnn_conv_w=v_rnn_conv_w, rnn_conv_b=v_rnn_conv_b, rnn_b_a=v_rnn_b_a,
                 rnn_b_i=v_rnn_b_i, rnn_lambda=v_rnn_lambda)
    shapes = {k: tuple(a.shape[1:]) for k, a in w_loc.items()}
    T, D = x.shape[1], x.shape[2]
    n_f = attn_b_f.shape[1]
    tb = min(1024, T)
    tb_bwd = min(512, T)
    tt_rg = min(128, T)
    tt_ln = min(256, T)
    c_idx = lax.axis_index("c").astype(jnp.int32).reshape(1)
    me_idx = (2 * lax.axis_index("x") + lax.axis_index("y")).astype(jnp.int32).reshape(1)
    place = jnp.concatenate([me_idx, c_idx])

    def attn_w_in_full(g_in, idx):
        return _join_columns(g_in, 4 * D + LANES, tr=256, name=f"a_join{idx}")

    def attn_w_out_full(g_out):
        return _to_full(g_out, "attn_w_out", shapes["attn_w_out"])

    def rnn_weights(g_in, g_rows):
        w = _unpack_gathered_rows(g_rows, shapes)
        w["rnn_w_in"] = _to_full(g_in, "rnn_w_in", shapes["rnn_w_in"])
        w["small"] = jnp.concatenate([w["rnn_conv_w"], w["rnn_conv_b"][None], w["rnn_b_a"][None],
                                      w["rnn_b_i"][None], w["rnn_lambda"][None]])
        return w

    g0 = _ag_c(_run_exchange(_Exchange("gather", [attn_w_in[0].astype(BF16)]), "ag_w0_xy"),
               "ag_w0_c")
    later = _Exchange("gather", [
        attn_w_out.astype(BF16), attn_w_in[1].astype(BF16), rnn_w_in.astype(BF16),
        jnp.stack([_pack_rows(w_loc, i, BF16) for i in range(2)])])
    w_attn_in, w_attn_out, w_rnn = [attn_w_in_full(g0[0], 0), None], [None, None], [None, None]
    bf_rows = jnp.pad(attn_b_f, ((0, 0), (0, LANES - n_f)))[:, None, :]

    xs, xb, saved = [x[0]], [x[0]], []
    for layer in range(DEPTH):
        idx, xl, xm = layer // 2, xs[-1], xb[-1]
        if layer % 2 == 0:
            proj = _matmul(xm, w_attn_in[idx], trans_b=False, tm=512, tn=1408,
                           name=f"a_proj{layer}")
            cum_t = _cumsum_fwd(proj, bf_rows[idx], tt=min(512, T), name=f"a_cum{layer}")
            cum2 = cum_t[:N_HEADS].reshape(N_PAIRS, 2, T)
            o, og, lp, *got = _flash_fwd(proj, cum2.reshape(N_PAIRS, 2, T // tb, tb), tb=tb,
                                         name=f"a_fwd{layer}", host=later if layer == 0 else None)
            cum4 = cum2.reshape(N_PAIRS, 2, T // tb_bwd, tb_bwd)
            if layer == 0:
                g1 = _ag_c(got, "ag_w1_c")
                w_attn_out = [attn_w_out_full(g1[0][:, :, i]) for i in range(2)]
                w_attn_in[1] = attn_w_in_full(g1[1], 1)
                w_rnn = [rnn_weights(g1[2][:, :, i], g1[3][:, :, i]) for i in range(2)]
            branch, w_out = og, w_attn_out[idx]
            saved.append((proj, cum4, o, og, lp))
        else:
            w = w_rnn[idx]
            proj = _matmul(xm, w["rnn_w_in"], trans_b=False, tm=512, tn=1024,
                           name=f"r_proj{layer}")
            hs, yr = _rg_fwd(proj, w["small"], w["rnn_w_a"], w["rnn_w_i"], tt=tt_rg,
                             name=f"r_fwd{layer}")
            branch, w_out = yr, w["rnn_w_out"]
            saved.append((proj, hs, yr))
        y, yb, zh, rstd = _out_ln(branch, w_out, xl, ln_g[layer][None], ln_b[layer][None],
                                  tt=512, name=f"out_ln{layer}")
        saved[-1] = saved[-1] + (zh, rstd)
        xs.append(y)
        xb.append(yb)

    loss_lanes, dy = _loss(xs[-1], loss_target[0], tt=tt_ln, name="loss")
    loss = lax.psum(jnp.sum(loss_lanes), ("x", "y", "c"))

    def reduce_pair(gs, layer):
        recv = _rs_c(gs, f"rs_c{layer}")
        outs = [_add_own(g, r, c_idx, tr=_row_tile(g.shape[2], 512), name=f"rs_add{layer}_{n}",
                         narrow=True) for n, (g, r) in enumerate(zip(gs, recv))]
        return [o[0][:, None] for o in outs], [o[1][:, None] for o in outs]

    part, got_parts = [None] * DEPTH, [None] * DEPTH
    d_ln_g, d_ln_b, d_bf = [None] * DEPTH, [None] * DEPTH, [None, None]
    for layer in reversed(range(DEPTH)):
        idx, xm = layer // 2, xb[layer]
        zh, rstd = saved[layer][-2:]
        dz, dzb, dg, db = _ln_bwd(dy, zh, rstd, ln_g[layer][None], tt=tt_ln,
                                  name=f"ln_bwd{layer}")
        d_ln_g[layer], d_ln_b[layer] = dg[0], db[0]
        if layer % 2 == 0:
            w_in, w_out = w_attn_in[idx], w_attn_out[idx]
            proj, cum4, o, og, lp = saved[layer][:5]
            dog = _matmul(dzb, w_out, trans_b=True, tm=512, tn=1024, name=f"a_dog{layer}")
            dwo = _matmul_tn(og, dzb, tm=512, tn=1024, tk=1024, name=f"a_dwo{layer}")
            riders = [l for l in range(layer + 1, DEPTH) if got_parts[l] is None]
            host = _Exchange("scatter8", [g for l in riders for g in part[l]]) if riders else None
            dq, dgate, dk, dv, dcum_q, dcum_k, *got = _flash_bwd(proj, cum4, o, dog, lp, tb=tb_bwd,
                                                                 name=f"a_bwd{layer}", host=host)
            for l in riders:
                got_parts[l], got = got[:len(part[l])], got[len(part[l]):]
            dcum_t = (dcum_q.transpose(0, 2, 1, 3) + dcum_k).reshape(N_HEADS, T)
            dcum_t = jnp.pad(dcum_t, ((0, LANES - N_HEADS), (0, 0)))
            df, dbf = _cumsum_bwd(dcum_t, proj, bf_rows[idx], tt=min(512, T), name=f"a_dcum{layer}")
            d_bf[idx] = dbf[0, :n_f]
            dproj = [dq, dk, dv, dgate, df]
            dwi = _matmul_tn_parts(xm, dproj, tm=512, tk=1024, name=f"a_dwi{layer}")
            gs = [_split_columns(dwi, shapes["attn_w_in"][1], tr=256, name=f"a_split{layer}"),
                  _from_full(dwo, "attn_w_out", shapes["attn_w_out"])]
            if layer > 0:
                part[layer] = gs
                dy = _matmul(dproj, w_in, trans_b=True, tm=512, tn=1024, name=f"a_dx{layer}",
                             add=dz, add_scale=ALPHA)
            else:
                part[layer], narrow = reduce_pair(gs, layer)
                dy, *got_parts[layer] = _matmul(dproj, w_in, trans_b=True, tm=512, tn=1024,
                                                name=f"a_dx{layer}", add=dz, add_scale=ALPHA,
                                                host=_Exchange("scatter", narrow))
        else:
            w = w_rnn[idx]
            proj, hs, yr = saved[layer][:3]
            dyr = _matmul(dzb, w["rnn_w_out"], trans_b=True, tm=512, tn=1024, name=f"r_dy{layer}")
            dwo = _matmul_tn(yr, dzb, tm=512, tn=1024, tk=1024, name=f"r_dwo{layer}")
            dproj, dwa, dwi_, dsm = _rg_bwd(proj, hs, dyr, w["small"], w["rnn_w_a"], w["rnn_w_i"],
                                            tt=tt_rg, name=f"r_bwd{layer}")
            dwin = _matmul_tn(xm, dproj, tm=512, tn=2048, tk=1024, name=f"r_dwi{layer}")
            dy = _matmul(dproj, w["rnn_w_in"], trans_b=True, tm=512, tn=1024, name=f"r_dx{layer}",
                         add=dz, add_scale=ALPHA)
            full = dict(rnn_w_out=dwo, rnn_w_a=dwa, rnn_w_i=dwi_, rnn_conv_w=dsm[0:4],
                        rnn_conv_b=dsm[4], rnn_b_a=dsm[5], rnn_b_i=dsm[6], rnn_lambda=dsm[7])
            part[layer] = [_from_full(dwin, "rnn_w_in", shapes["rnn_w_in"]),
                           _pack_grad_rows(full, shapes)]
    grad_x = dy[None]

    def grad_parts(layer, n):
        make = _two_stage_parts if layer == 0 else _direct_parts
        return make(part[layer][n], got_parts[layer][n])

    def update(k, n):
        res = None
        for idx in (1, 0):
            layer = 2 * idx + (0 if k.startswith("attn") else 1)
            res = _adamw_shard(grad_parts(layer, n), place, w_loc[k], m_loc[k], v_loc[k],
                               idx=idx, prev=res, tr=_row_tile(shapes[k][0], 256),
                               name=f"adamw_{k}{idx}")
        return res

    shard_outs = [dict() for _ in range(4)]
    for k, n in (("attn_w_in", 0), ("attn_w_out", 1), ("rnn_w_in", 0)):
        for j, a in enumerate(update(k, n)):
            shard_outs[j][k] = a
    rows = []
    for idx in range(2):
        layer = 2 * idx + 1
        wmv = [_pack_rows(d, idx, F32)[None] for d in (w_loc, m_loc, v_loc)]
        res = _adamw_shard(grad_parts(layer, 1), place, *wmv, idx=0, prev=None,
                           tr=_row_tile(wmv[0].shape[1], 256), name=f"adamw_rows{idx}")
        rows.append([_unpack_rows(a[0], shapes) for a in res])
    for j in range(4):
        for k in RNN_ROWED + SMALL:
            shard_outs[j][k] = jnp.stack([rows[0][j][k], rows[1][j][k]])
    g_sh, d_sh, nm_sh, nv_sh = shard_outs

    def rep_pack(lg, lb, bf):
        rows = jnp.concatenate([lg, lb, jnp.pad(bf.reshape(1, -1), ((0, 0), (0, D - 2 * n_f)))])
        return jnp.pad(rows, ((0, 16 - rows.shape[0]), (0, 0)))

    rep = _all_gather(rep_pack(jnp.stack(d_ln_g), jnp.stack(d_ln_b), jnp.stack(d_bf)), "ag_rep")
    rg, rd, rm, rv = _adamw(rep.reshape(8, 16, D), rep_pack(ln_g, ln_b, attn_b_f),
                            rep_pack(m_ln_g, m_ln_b, m_attn_b_f),
                            rep_pack(v_ln_g, v_ln_b, v_attn_b_f), tr=16, name="adamw_rep")

    def rep_unpack(a):
        return dict(ln_g=a[0:DEPTH], ln_b=a[DEPTH:2 * DEPTH],
                    attn_b_f=a[2 * DEPTH, :2 * n_f].reshape(2, n_f))

    order = ("ln_g", "ln_b", "attn_w_in", "attn_b_f", "attn_w_out", "rnn_w_in", "rnn_conv_w",
             "rnn_conv_b", "rnn_w_a", "rnn_b_a", "rnn_w_i", "rnn_b_i", "rnn_lambda", "rnn_w_out")
    outs = [loss, grad_x]
    for sh, rp in ((g_sh, rg), (d_sh, rd), (nm_sh, rm), (nv_sh, rv)):
        allp = {**sh, **rep_unpack(rp)}
        outs.extend(allp[k] for k in order)
    return tuple(outs)
```

```python
import functools

import jax
import jax.numpy as jnp
from jax import lax
from jax.experimental import pallas as pl
from jax.experimental.pallas import tpu as pltpu

F32 = jnp.float32
BF16 = jnp.bfloat16

DEPTH = 4
N_HEADS = 16
HEAD_DIM = 64
N_PAIRS = N_HEADS // 2
RNN_BLOCKS = 4
RNN_BLOCK_WIDTH = 256
CONV_WIDTH = 4
LRU_C = 8.0
ALPHA = (2.0 * DEPTH) ** 0.25
LN_EPS = 1e-5
ADAM_LR, ADAM_B1, ADAM_B2, ADAM_EPS, ADAM_WD, ADAM_STEP = 0.001, 0.9, 0.999, 1e-8, 0.01, 10

LANES = 128
SUBLANES = 8
VMEM_LIMIT = 48 * 1024 * 1024

MESH = pl.DeviceIdType.MESH
HBM_SPEC = pl.BlockSpec(memory_space=pltpu.HBM)


def _cparams(*sem):
    return pltpu.CompilerParams(dimension_semantics=sem, vmem_limit_bytes=VMEM_LIMIT)


def _sigmoid(x):
    return 1.0 / (1.0 + jnp.exp(-x))


def _softplus(x):
    return jnp.maximum(x, 0.0) + jnp.log(1.0 + jnp.exp(-jnp.abs(x)))


def _a2a(src, *, group, bcast, name):
    n = 2 if group == "c" else 4
    blk = tuple(src.shape) if bcast else tuple(src.shape[1:])

    def body(src_ref, out_ref, send_sems, recv_sems, local_sem):
        x, y, c = lax.axis_index("x"), lax.axis_index("y"), lax.axis_index("c")
        if group == "c":
            me = c

            def peer(d):
                return (x, y, 1 - c), 1 - c
        else:
            me = 2 * x + y

            def peer(d):
                px, py = x ^ (d >> 1), y ^ (d & 1)
                return (px, py, c), 2 * px + py

        def block_for(k):
            return src_ref if bcast else src_ref.at[k]

        local = pltpu.make_async_copy(block_for(me), out_ref.at[me], local_sem)
        local.start()
        sends = []
        for d in range(1, n):
            dev, idx = peer(d)
            cp = pltpu.make_async_remote_copy(
                src_ref=block_for(idx), dst_ref=out_ref.at[me],
                send_sem=send_sems.at[d], recv_sem=recv_sems.at[d],
                device_id=dev, device_id_type=MESH)
            cp.start()
            sends.append(cp)
        for d in range(1, n):
            dev, idx = peer(d)
            pltpu.make_async_remote_copy(
                src_ref=block_for(idx), dst_ref=out_ref.at[idx],
                send_sem=send_sems.at[d], recv_sem=recv_sems.at[d],
                device_id=dev, device_id_type=MESH).wait_recv()
        for cp in sends:
            cp.wait_send()
        local.wait()

    return pl.pallas_call(
        body, name=name,
        out_shape=jax.ShapeDtypeStruct((n,) + blk, src.dtype),
        in_specs=[HBM_SPEC], out_specs=HBM_SPEC,
        scratch_shapes=[pltpu.SemaphoreType.DMA((n,)), pltpu.SemaphoreType.DMA((n,)),
                        pltpu.SemaphoreType.DMA],
    )(src)


def _all_gather(piece, name):
    return _a2a(_a2a(piece, group="xy", bcast=True, name=name + "_xy"),
                group="c", bcast=True, name=name + "_c")


D2D_CHUNKS = 16
ICI_CHUNKS = 8


def _row_chunks(rows, dtype, k):
    unit = SUBLANES * (4 // jnp.dtype(dtype).itemsize)
    assert rows % unit == 0
    units = rows // unit
    k = max(1, min(k, units))
    base, rem = divmod(units, k)
    out, r = [], 0
    for i in range(k):
        n = (base + (1 if i < rem else 0)) * unit
        out.append((r, n))
        r += n
    return out


def _chunks(shape, dtype, k):
    if len(shape) == 2:
        return [(pl.ds(r0, n),) for r0, n in _row_chunks(shape[0], dtype, k)]
    per = max(1, k // shape[0])
    return [(l, pl.ds(r0, n)) for l in range(shape[0]) for r0, n in _row_chunks(shape[1], dtype, per)]


def _mesh_place():
    x, y, c = lax.axis_index("x"), lax.axis_index("y"), lax.axis_index("c")
    return x, y, c, 2 * x + y


def _chip_peer(x, y, c, d):
    px, py = x ^ (d >> 1), y ^ (d & 1)
    return (px, py, c), 2 * px + py


def _remote(src, dst, send_sem, recv_sem, dev):
    return pltpu.make_async_remote_copy(src_ref=src, dst_ref=dst, send_sem=send_sem,
                                        recv_sem=recv_sem, device_id=dev, device_id_type=MESH)


def _comm_call(body, name, ins, out_shapes, n_sems, aliases=None):
    n = len(ins)
    return pl.pallas_call(
        body, name=name,
        out_shape=out_shapes, in_specs=[HBM_SPEC] * n, out_specs=[HBM_SPEC] * n,
        input_output_aliases=aliases or {},
        scratch_shapes=[pltpu.SemaphoreType.DMA((n_sems, n)), pltpu.SemaphoreType.DMA((n_sems, n))],
    )(*ins)


class _Exchange:
    def __init__(self, kind, arrays):
        self.kind, self.arrays, self.n = kind, list(arrays), len(arrays)
        if kind == "gather":
            self.chunks = [_chunks(a.shape, a.dtype, ICI_CHUNKS) for a in arrays]
            self.out_shapes = [jax.ShapeDtypeStruct((2, 4) + tuple(a.shape), a.dtype) for a in arrays]
        else:
            lead, k = (1, ICI_CHUNKS) if kind == "scatter" else (2, ICI_CHUNKS // 4)
            self.chunks = [_chunks(a.shape[lead:], a.dtype, k) for a in arrays]
            self.out_shapes = [jax.ShapeDtypeStruct(a.shape, a.dtype) for a in arrays]
        self.peers = [p for p in range(1, 8 if kind == "scatter8" else 4)]
        n_sems = 8 if kind == "scatter8" else 4
        self.sem_shapes = [pltpu.SemaphoreType.DMA((n_sems, self.n)),
                           pltpu.SemaphoreType.DMA((n_sems, self.n))]

    def _peer(self, x, y, c, me, p):
        a, d = p // 4, p % 4
        px, py = x ^ (d >> 1), y ^ (d & 1)
        pc = 1 - c if a else c
        if self.kind == "scatter8":
            return (px, py, pc), (pc, 2 * px + py), (c, me)
        return (px, py, pc), (2 * px + py,), (me,)

    def _blocks(self, srcs, outs, o, c, me, theirs, mine):
        if self.kind == "gather":
            return srcs[o], outs[o].at[(c,) + mine], outs[o].at[(c,) + theirs]
        return srcs[o].at[theirs], outs[o].at[mine], outs[o].at[theirs]

    def start(self, srcs, outs, send_sems, recv_sems):
        x, y, c, me = _mesh_place()
        if self.kind == "gather":
            for o in range(self.n):
                for idx in self.chunks[o]:
                    pltpu.make_async_copy(srcs[o].at[idx], outs[o].at[(c, me) + idx],
                                          send_sems.at[0, o]).start()
        for p in self.peers:
            dev, theirs, mine = self._peer(x, y, c, me, p)
            for o in range(self.n):
                src, dst, _ = self._blocks(srcs, outs, o, c, me, theirs, mine)
                for idx in self.chunks[o]:
                    _remote(src.at[idx], dst.at[idx], send_sems.at[p, o], recv_sems.at[p, o],
                            dev).start()

    def wait(self, srcs, outs, send_sems, recv_sems):
        x, y, c, me = _mesh_place()
        for wait_recv in (True, False):
            for p in self.peers:
                dev, theirs, mine = self._peer(x, y, c, me, p)
                for o in range(self.n):
                    src, _, land = self._blocks(srcs, outs, o, c, me, theirs, mine)
                    cp = _remote(src, land, send_sems.at[p, o], recv_sems.at[p, o], dev)
                    cp.wait_recv() if wait_recv else cp.wait_send()
        if self.kind == "gather":
            for o in range(self.n):
                pltpu.make_async_copy(srcs[o], outs[o].at[c, me], send_sems.at[0, o]).wait()


def _run_exchange(ex, name):
    n = ex.n

    def body(*refs):
        srcs, outs, send_sems, recv_sems = refs[:n], refs[n:2 * n], refs[2 * n], refs[2 * n + 1]
        ex.start(srcs, outs, send_sems, recv_sems)
        ex.wait(srcs, outs, send_sems, recv_sems)

    return _comm_call(body, name, ex.arrays, ex.out_shapes, len(ex.peers) + 1)


def _ag_c(bufs, name):
    n = len(bufs)
    chunks = [_chunks(b.shape[2:], b.dtype, D2D_CHUNKS // 4) for b in bufs]

    def body(*refs):
        srcs, outs, send_sems, recv_sems = refs[:n], refs[n:2 * n], refs[2 * n], refs[2 * n + 1]
        x, y, c, _ = _mesh_place()
        sib = (x, y, 1 - c)
        for o in range(n):
            for k in range(4):
                for idx in chunks[o]:
                    _remote(srcs[o].at[(c, k) + idx], outs[o].at[(c, k) + idx],
                            send_sems.at[0, o], recv_sems.at[0, o], sib).start()
        for o in range(n):
            _remote(srcs[o].at[c], outs[o].at[1 - c], send_sems.at[0, o], recv_sems.at[0, o],
                    sib).wait_recv()
        for o in range(n):
            _remote(srcs[o].at[c], outs[o].at[1 - c], send_sems.at[0, o], recv_sems.at[0, o],
                    sib).wait_send()

    shapes = [jax.ShapeDtypeStruct(b.shape, b.dtype) for b in bufs]
    return _comm_call(body, name, bufs, shapes, 1, aliases={i: i for i in range(n)})


def _rs_c(gs, name):
    n = len(gs)
    chunks = [_chunks(g.shape[2:], g.dtype, max(1, D2D_CHUNKS // g.shape[1])) for g in gs]

    def body(*refs):
        srcs, outs, send_sems, recv_sems = refs[:n], refs[n:2 * n], refs[2 * n], refs[2 * n + 1]
        x, y, c, _ = _mesh_place()
        sib = (x, y, 1 - c)
        for o in range(n):
            for k in range(gs[o].shape[1]):
                for idx in chunks[o]:
                    _remote(srcs[o].at[(1 - c, k) + idx], outs[o].at[(k,) + idx],
                            send_sems.at[0, o], recv_sems.at[0, o], sib).start()
        for o in range(n):
            _remote(srcs[o].at[1 - c], outs[o], send_sems.at[0, o], recv_sems.at[0, o],
                    sib).wait_recv()
        for o in range(n):
            _remote(srcs[o].at[1 - c], outs[o], send_sems.at[0, o], recv_sems.at[0, o],
                    sib).wait_send()

    shapes = [jax.ShapeDtypeStruct(g.shape[1:], g.dtype) for g in gs]
    return _comm_call(body, name, gs, shapes, 1)


def _matmul(a, b, *, trans_b, tm, tn, name, add=None, add_scale=1.0, host=None):
    a_parts = list(a) if isinstance(a, (list, tuple)) else [a]
    M, K = a_parts[0].shape[0], sum(p.shape[1] for p in a_parts)
    N = b.shape[0] if trans_b else b.shape[1]
    tm, tn = min(tm, M), min(tn, N)
    assert M % tm == 0 and N % tn == 0
    dn = (((1,), (1,)), ((), ())) if trans_b else (((1,), (0,)), ((), ()))
    na = len(a_parts)

    def body(*refs):
        a_refs, b_ref, o_ref = refs[:na], refs[na], refs[-1]
        av = [r[...].astype(BF16) for r in a_refs]
        av = av[0] if na == 1 else jnp.concatenate(av, axis=1)
        r = lax.dot_general(av, b_ref[...].astype(BF16), dn, preferred_element_type=F32)
        if add is not None:
            r = r + add_scale * refs[na + 1][...]
        o_ref[...] = r

    b_spec = (pl.BlockSpec((tn, K), lambda j, i: (j, 0)) if trans_b
              else pl.BlockSpec((K, tn), lambda j, i: (0, j)))
    in_specs = [pl.BlockSpec((tm, p.shape[1]), lambda j, i: (i, 0)) for p in a_parts] + [b_spec]
    args = a_parts + [b]
    if add is not None:
        in_specs.append(pl.BlockSpec((tm, tn), lambda j, i: (i, j)))
        args.append(add)
    grid = (N // tn, M // tm)
    x_in, x_out, x_shapes, x_scratch, x_args = _host_specs(host)
    body = _hosted(body, len(args), 1, 0, host, grid)
    outs = pl.pallas_call(
        body, name=name, grid=grid,
        in_specs=in_specs + x_in,
        out_specs=[pl.BlockSpec((tm, tn), lambda j, i: (i, j))] + x_out,
        out_shape=[jax.ShapeDtypeStruct((M, N), F32)] + x_shapes,
        scratch_shapes=x_scratch,
        compiler_params=_cparams(*(("arbitrary",) * 2 if host else ("parallel",) * 2)),
    )(*args, *x_args)
    return outs if host else outs[0]


def _matmul_tn(a, b, *, tm, tn, tk, name):
    T, M = a.shape
    N = b.shape[1]
    tm, tn, tk = min(tm, M), min(tn, N), min(tk, T)
    assert M % tm == 0 and N % tn == 0 and T % tk == 0

    def body(a_ref, b_ref, o_ref):
        @pl.when(pl.program_id(2) == 0)
        def _():
            o_ref[...] = jnp.zeros_like(o_ref)

        o_ref[...] += lax.dot_general(a_ref[...].astype(BF16), b_ref[...].astype(BF16),
                                      (((0,), (0,)), ((), ())), preferred_element_type=F32)

    return pl.pallas_call(
        body, name=name, grid=(M // tm, N // tn, T // tk),
        in_specs=[pl.BlockSpec((tk, tm), lambda i, j, k: (k, i)),
                  pl.BlockSpec((tk, tn), lambda i, j, k: (k, j))],
        out_specs=pl.BlockSpec((tm, tn), lambda i, j, k: (i, j)),
        out_shape=jax.ShapeDtypeStruct((M, N), F32),
        compiler_params=_cparams("parallel", "parallel", "arbitrary"),
    )(a, b)


def _matmul_tn_parts(a, parts, *, tm, tk, name):
    T, M = a.shape
    tm, tk = min(tm, M), min(tk, T)
    assert M % tm == 0 and T % tk == 0
    n = len(parts)

    def body(*refs):
        a_ref, b_refs, o_refs = refs[0], refs[1:1 + n], refs[1 + n:]
        av = a_ref[...].astype(BF16)
        for b_ref, o_ref in zip(b_refs, o_refs):
            @pl.when(pl.program_id(1) == 0)
            def _(o_ref=o_ref):
                o_ref[...] = jnp.zeros_like(o_ref)

            o_ref[...] += lax.dot_general(av, b_ref[...].astype(BF16), (((0,), (0,)), ((), ())),
                                          preferred_element_type=F32)

    return pl.pallas_call(
        body, name=name, grid=(M // tm, T // tk),
        in_specs=[pl.BlockSpec((tk, tm), lambda i, k: (k, i))]
        + [pl.BlockSpec((tk, p.shape[1]), lambda i, k: (k, 0)) for p in parts],
        out_specs=[pl.BlockSpec((tm, p.shape[1]), lambda i, k: (i, 0)) for p in parts],
        out_shape=[jax.ShapeDtypeStruct((M, p.shape[1]), F32) for p in parts],
        compiler_params=_cparams("parallel", "arbitrary"),
    )(a, *parts)


def _head_masks(rows):
    lane = lax.broadcasted_iota(jnp.int32, (rows, LANES), 1)
    return lane < HEAD_DIM, lane >= HEAD_DIM


def _causal(i_q, i_k, tq, tk):
    row = i_q * tq + lax.broadcasted_iota(jnp.int32, (tq, tk), 0)
    col = i_k * tk + lax.broadcasted_iota(jnp.int32, (tq, tk), 1)
    return row >= col


def _hosted(body, n_in, n_out, n_scratch, host, grid):
    if host is None:
        return body
    nx = host.n

    def wrapped(*refs):
        ins, xsrcs = refs[:n_in], refs[n_in:n_in + nx]
        outs = refs[n_in + nx:n_in + nx + n_out]
        xouts = refs[n_in + nx + n_out:n_in + 2 * nx + n_out]
        scratch = refs[n_in + 2 * nx + n_out:n_in + 2 * nx + n_out + n_scratch]
        xsems = refs[n_in + 2 * nx + n_out + n_scratch:]
        step = pl.program_id(0) * grid[1] + pl.program_id(1)

        @pl.when(step == 0)
        def _():
            host.start(xsrcs, xouts, *xsems)

        body(*ins, *outs, *scratch)

        @pl.when(step == grid[0] * grid[1] - 1)
        def _():
            host.wait(xsrcs, xouts, *xsems)

    return wrapped


def _host_specs(host):
    if host is None:
        return [], [], [], [], []
    return ([HBM_SPEC] * host.n, [HBM_SPEC] * host.n, host.out_shapes, host.sem_shapes, host.arrays)


def _flash_fwd(proj, cum4, *, tb, name, host=None):
    T = proj.shape[0]
    D = N_HEADS * HEAD_DIM
    nb = T // tb
    cb = D // LANES
    x_in, x_out, x_shapes, x_scratch, x_args = _host_specs(host)

    def body(q_ref, k_ref, v_ref, g_ref, cum_ref, o_ref, og_ref, lp_ref, kb_ref, vb_ref):
        i = pl.program_id(1)

        @pl.when(i == 0)
        def _():
            kb_ref[...] = k_ref[...].astype(BF16)
            vb_ref[...] = v_ref[...].astype(BF16)

        q = q_ref[...] * (HEAD_DIM ** -0.5)
        masks = _head_masks(tb)
        qh = [jnp.where(masks[h], q, 0.0).astype(BF16) for h in range(2)]
        cref = [cum_ref[0, h, pl.ds(i, 1), :][:, 0:1] for h in range(2)]

        def step(kbi, carry, masked):
            k0 = pl.multiple_of(kbi * tb, tb)
            kblk = kb_ref[pl.ds(k0, tb), :]
            vblk = vb_ref[pl.ds(k0, tb), :]
            new = []
            for h in range(2):
                m, l, acc = carry[h]
                s = lax.dot_general(qh[h], kblk, (((1,), (1,)), ((), ())),
                                    preferred_element_type=F32)
                s = s + (cref[h] - cum_ref[0, h, pl.ds(kbi, 1), :])
                if masked:
                    s = jnp.where(_causal(i, kbi, tb, tb), s, -jnp.inf)
                m_new = jnp.maximum(m, jnp.max(s, axis=-1, keepdims=True))
                alpha = jnp.exp(m - m_new)
                p = jnp.exp(s - m_new)
                l = alpha * l + jnp.sum(p, axis=-1, keepdims=True)
                acc = alpha * acc + jnp.dot(p.astype(BF16), vblk, preferred_element_type=F32)
                new.append((m_new, l, acc))
            return tuple(new)

        init1 = (jnp.full((tb, 1), -jnp.inf, F32), jnp.zeros((tb, 1), F32),
                 jnp.zeros((tb, LANES), F32))
        carry = lax.fori_loop(0, i, lambda kbi, c: step(kbi, c, False), (init1, init1))
        outs = []
        for h, (m, l, acc) in enumerate(step(i, carry, True)):
            outs.append(acc / l)
            lp_ref[h] = jnp.broadcast_to(m + jnp.log(l) - cref[h], (tb, LANES))
        o = jnp.where(masks[0], outs[0], outs[1])
        o_ref[...] = o
        gate = g_ref[...]
        og_ref[...] = (o * (gate * _sigmoid(gate))).astype(BF16)

    body = _hosted(body, 5, 3, 2, host, (N_PAIRS, nb))
    return pl.pallas_call(
        body, name=name, grid=(N_PAIRS, nb),
        in_specs=[pl.BlockSpec((tb, LANES), lambda j, i: (i, j)),
                  pl.BlockSpec((T, LANES), lambda j, i: (0, cb + j)),
                  pl.BlockSpec((T, LANES), lambda j, i: (0, 2 * cb + j)),
                  pl.BlockSpec((tb, LANES), lambda j, i: (i, 3 * cb + j)),
                  pl.BlockSpec((1, 2, nb, tb), lambda j, i: (j, 0, 0, 0))] + x_in,
        out_specs=[pl.BlockSpec((tb, LANES), lambda j, i: (i, j)),
                   pl.BlockSpec((tb, LANES), lambda j, i: (i, j)),
                   pl.BlockSpec((2, tb, LANES), lambda j, i: (j, i, 0))] + x_out,
        out_shape=[jax.ShapeDtypeStruct((T, D), F32), jax.ShapeDtypeStruct((T, D), BF16),
                   jax.ShapeDtypeStruct((N_HEADS, T, LANES), F32)] + x_shapes,
        scratch_shapes=[pltpu.VMEM((T, LANES), BF16), pltpu.VMEM((T, LANES), BF16)] + x_scratch,
        compiler_params=_cparams("arbitrary", "arbitrary"),
    )(proj, proj, proj, proj, cum4, *x_args)


def _flash_bwd_dq(proj, cum4, o, dog, lp, *, tb, name, host=None):
    T = proj.shape[0]
    D = N_HEADS * HEAD_DIM
    nb = T // tb
    cb = D // LANES
    x_in, x_out, x_shapes, x_scratch, x_args = _host_specs(host)

    def body(q_ref, k_ref, v_ref, g_ref, cum_ref, o_ref, dog_ref, lp_ref,
             dq_ref, dg_ref, do_ref, dl_ref, dc_ref, kb_ref, vb_ref):
        i = pl.program_id(1)

        @pl.when(i == 0)
        def _():
            kb_ref[...] = k_ref[...].astype(BF16)
            vb_ref[...] = v_ref[...].astype(BF16)

        gate = g_ref[...]
        sg = _sigmoid(gate)
        o = o_ref[...]
        dog = dog_ref[...]
        do = dog * (gate * sg)
        dg_ref[...] = (dog * o * (sg * (1.0 + gate * (1.0 - sg)))).astype(BF16)
        do_ref[...] = do.astype(BF16)
        q = q_ref[...] * (HEAD_DIM ** -0.5)
        masks = _head_masks(tb)
        qh = [jnp.where(masks[h], q, 0.0).astype(BF16) for h in range(2)]
        doh = [jnp.where(masks[h], do, 0.0).astype(BF16) for h in range(2)]
        delta = [jnp.sum(jnp.where(masks[h], do * o, 0.0), axis=-1, keepdims=True) for h in range(2)]
        lph = [lp_ref[h][:, 0:1] for h in range(2)]
        for h in range(2):
            dl_ref[h] = jnp.broadcast_to(delta[h], (tb, LANES))

        def step(kbi, carry, masked):
            k0 = pl.multiple_of(kbi * tb, tb)
            kblk = kb_ref[pl.ds(k0, tb), :]
            vblk = vb_ref[pl.ds(k0, tb), :]
            new = []
            for h in range(2):
                acc, rs = carry[h]
                s = lax.dot_general(qh[h], kblk, (((1,), (1,)), ((), ())), preferred_element_type=F32)
                p = jnp.exp(s - cum_ref[0, h, pl.ds(kbi, 1), :] - lph[h])
                if masked:
                    p = jnp.where(_causal(i, kbi, tb, tb), p, 0.0)
                dp = lax.dot_general(doh[h], vblk, (((1,), (1,)), ((), ())),
                                     preferred_element_type=F32)
                ds = p * (dp - delta[h])
                new.append((acc + jnp.dot(ds.astype(BF16), kblk, preferred_element_type=F32),
                            rs + jnp.sum(ds, axis=-1, keepdims=True)))
            return tuple(new)

        init1 = (jnp.zeros((tb, LANES), F32), jnp.zeros((tb, 1), F32))
        carry = lax.fori_loop(0, i, lambda kbi, c: step(kbi, c, False), (init1, init1))
        dqs = []
        for h, (acc, rs) in enumerate(step(i, carry, True)):
            dqs.append(acc)
            dc_ref[0, 0, pl.ds(h, 1), :] = jnp.broadcast_to(rs, (tb, LANES)).T[0:1, :]
        dq_ref[...] = (jnp.where(masks[0], dqs[0], dqs[1]) * (HEAD_DIM ** -0.5)).astype(BF16)

    blk = pl.BlockSpec((tb, LANES), lambda j, i: (i, j))
    stat = pl.BlockSpec((2, tb, LANES), lambda j, i: (j, i, 0))
    body = _hosted(body, 8, 5, 2, host, (N_PAIRS, nb))
    return pl.pallas_call(
        body, name=name, grid=(N_PAIRS, nb),
        in_specs=[blk,
                  pl.BlockSpec((T, LANES), lambda j, i: (0, cb + j)),
                  pl.BlockSpec((T, LANES), lambda j, i: (0, 2 * cb + j)),
                  pl.BlockSpec((tb, LANES), lambda j, i: (i, 3 * cb + j)),
                  pl.BlockSpec((1, 2, nb, tb), lambda j, i: (j, 0, 0, 0)),
                  blk, blk, stat] + x_in,
        out_specs=[blk, blk, blk, stat,
                   pl.BlockSpec((1, 1, 2, tb), lambda j, i: (j, i, 0, 0))] + x_out,
        out_shape=[jax.ShapeDtypeStruct((T, D), BF16), jax.ShapeDtypeStruct((T, D), BF16),
                   jax.ShapeDtypeStruct((T, D), BF16),
                   jax.ShapeDtypeStruct((N_HEADS, T, LANES), F32),
                   jax.ShapeDtypeStruct((N_PAIRS, nb, 2, tb), F32)] + x_shapes,
        scratch_shapes=[pltpu.VMEM((T, LANES), BF16), pltpu.VMEM((T, LANES), BF16)] + x_scratch,
        compiler_params=_cparams("arbitrary", "arbitrary"),
    )(proj, proj, proj, proj, cum4, o, dog, lp, *x_args)


def _flash_bwd_dkv(proj, cum4, do, lp, delta, *, tb, name):
    T = proj.shape[0]
    D = N_HEADS * HEAD_DIM
    nb = T // tb
    cb = D // LANES

    def body(q_ref, k_ref, v_ref, cum_ref, do_ref, lp_ref, dl_ref, dk_ref, dv_ref, dc_ref):
        kbi = pl.program_id(1)
        k = k_ref[...] * (HEAD_DIM ** -0.5)
        v = v_ref[...]
        masks = _head_masks(tb)
        kh = [jnp.where(masks[h], k, 0.0).astype(BF16) for h in range(2)]
        vh = [jnp.where(masks[h], v, 0.0).astype(BF16) for h in range(2)]
        ck = [cum_ref[0, h, pl.ds(kbi, 1), :] for h in range(2)]

        def step(i, carry, masked):
            q0 = pl.multiple_of(i * tb, tb)
            qb = q_ref[pl.ds(q0, tb), :].astype(BF16)
            dob = do_ref[pl.ds(q0, tb), :]
            new = []
            for h in range(2):
                dk, dv, dc = carry[h]
                s = lax.dot_general(qb, kh[h], (((1,), (1,)), ((), ())), preferred_element_type=F32)
                p = jnp.exp(s - ck[h] - lp_ref[h, pl.ds(q0, tb), :][:, 0:1])
                if masked:
                    p = jnp.where(_causal(i, kbi, tb, tb), p, 0.0)
                dp = lax.dot_general(dob, vh[h], (((1,), (1,)), ((), ())), preferred_element_type=F32)
                ds = p * (dp - dl_ref[h, pl.ds(q0, tb), :][:, 0:1])
                dv = dv + lax.dot_general(p.astype(BF16), dob, (((0,), (0,)), ((), ())),
                                          preferred_element_type=F32)
                dk = dk + lax.dot_general(ds.astype(BF16), qb, (((0,), (0,)), ((), ())),
                                          preferred_element_type=F32)
                new.append((dk, dv, dc - jnp.sum(ds, axis=0, keepdims=True)))
            return tuple(new)

        init1 = (jnp.zeros((tb, LANES), F32), jnp.zeros((tb, LANES), F32), jnp.zeros((1, tb), F32))
        carry = step(kbi, (init1, init1), True)
        carry = lax.fori_loop(kbi + 1, nb, lambda i, c: step(i, c, False), carry)
        dks, dvs = [], []
        for h, (dk, dv, dc) in enumerate(carry):
            dks.append(dk)
            dvs.append(dv)
            dc_ref[0, 0, pl.ds(h, 1), :] = dc
        dk_ref[...] = (jnp.where(masks[0], dks[0], dks[1]) * (HEAD_DIM ** -0.5)).astype(BF16)
        dv_ref[...] = jnp.where(masks[0], dvs[0], dvs[1]).astype(BF16)

    full = pl.BlockSpec((T, LANES), lambda j, i: (0, j))
    stat = pl.BlockSpec((2, T, LANES), lambda j, i: (j, 0, 0))
    blk = pl.BlockSpec((tb, LANES), lambda j, i: (i, j))
    return pl.pallas_call(
        body, name=name, grid=(N_PAIRS, nb),
        in_specs=[full,
                  pl.BlockSpec((tb, LANES), lambda j, i: (i, cb + j)),
                  pl.BlockSpec((tb, LANES), lambda j, i: (i, 2 * cb + j)),
                  pl.BlockSpec((1, 2, nb, tb), lambda j, i: (j, 0, 0, 0)),
                  full, stat, stat],
        out_specs=[blk, blk, pl.BlockSpec((1, 1, 2, tb), lambda j, i: (j, i, 0, 0))],
        out_shape=[jax.ShapeDtypeStruct((T, D), BF16), jax.ShapeDtypeStruct((T, D), BF16),
                   jax.ShapeDtypeStruct((N_PAIRS, nb, 2, tb), F32)],
        compiler_params=_cparams("parallel", "arbitrary"),
    )(proj, proj, proj, cum4, do, lp, delta)


def _flash_bwd(proj, cum4, o, dog, lp, *, tb, name, host=None):
    T = proj.shape[0]
    D = N_HEADS * HEAD_DIM
    nb = T // tb
    cb = D // LANES
    x_in, x_out, x_shapes, x_scratch, x_args = _host_specs(host)

    def body(q_ref, k_ref, v_ref, g_ref, cum_ref, o_ref, dog_ref, lp_ref,
             dq_ref, dg_ref, dk_ref, dv_ref, dcq_ref, dck_ref,
             kb_ref, vb_ref, dka_ref, dva_ref, dca_ref):
        i = pl.program_id(1)

        @pl.when(i == 0)
        def _():
            kb_ref[...] = k_ref[...].astype(BF16)
            vb_ref[...] = v_ref[...].astype(BF16)
            dka_ref[...] = jnp.zeros_like(dka_ref)
            dva_ref[...] = jnp.zeros_like(dva_ref)
            dca_ref[...] = jnp.zeros_like(dca_ref)

        gate = g_ref[...]
        sg = _sigmoid(gate)
        o = o_ref[...]
        dog = dog_ref[...]
        do = dog * (gate * sg)
        dg_ref[...] = (dog * o * (sg * (1.0 + gate * (1.0 - sg)))).astype(BF16)
        q = q_ref[...] * (HEAD_DIM ** -0.5)
        masks = _head_masks(tb)
        qh = [jnp.where(masks[h], q, 0.0).astype(BF16) for h in range(2)]
        doh = [jnp.where(masks[h], do, 0.0).astype(BF16) for h in range(2)]
        delta = [jnp.sum(jnp.where(masks[h], do * o, 0.0), axis=-1, keepdims=True) for h in range(2)]
        lph = [lp_ref[h][:, 0:1] for h in range(2)]

        def step(kbi, carry, masked):
            k0 = pl.multiple_of(kbi * tb, tb)
            kblk = kb_ref[pl.ds(k0, tb), :]
            vblk = vb_ref[pl.ds(k0, tb), :]
            new, dk, dv = [], None, None
            for h in range(2):
                acc, rs = carry[h]
                s = lax.dot_general(qh[h], kblk, (((1,), (1,)), ((), ())), preferred_element_type=F32)
                p = jnp.exp(s - cum_ref[0, h, pl.ds(kbi, 1), :] - lph[h])
                if masked:
                    p = jnp.where(_causal(i, kbi, tb, tb), p, 0.0)
                dp = lax.dot_general(doh[h], vblk, (((1,), (1,)), ((), ())),
                                     preferred_element_type=F32)
                ds = p * (dp - delta[h])
                pb, dsb = p.astype(BF16), ds.astype(BF16)
                dv_h = lax.dot_general(pb, doh[h], (((0,), (0,)), ((), ())),
                                       preferred_element_type=F32)
                dk_h = lax.dot_general(dsb, qh[h], (((0,), (0,)), ((), ())),
                                       preferred_element_type=F32)
                dv = dv_h if dv is None else dv + dv_h
                dk = dk_h if dk is None else dk + dk_h
                dca_ref[h, pl.ds(kbi, 1), :] -= jnp.sum(ds, axis=0, keepdims=True)
                new.append((acc + jnp.dot(dsb, kblk, preferred_element_type=F32),
                            rs + jnp.sum(ds, axis=-1, keepdims=True)))
            dka_ref[pl.ds(k0, tb), :] += dk
            dva_ref[pl.ds(k0, tb), :] += dv
            return tuple(new)

        init1 = (jnp.zeros((tb, LANES), F32), jnp.zeros((tb, 1), F32))
        carry = lax.fori_loop(0, i, lambda kbi, c: step(kbi, c, False), (init1, init1))
        dqs = []
        for h, (acc, rs) in enumerate(step(i, carry, True)):
            dqs.append(acc)
            dcq_ref[0, 0, pl.ds(h, 1), :] = jnp.broadcast_to(rs, (tb, LANES)).T[0:1, :]
        dq_ref[...] = (jnp.where(masks[0], dqs[0], dqs[1]) * (HEAD_DIM ** -0.5)).astype(BF16)

        @pl.when(i == nb - 1)
        def _():
            dk_ref[...] = dka_ref[...].astype(BF16)
            dv_ref[...] = dva_ref[...].astype(BF16)
            dck_ref[0] = dca_ref[...]

    blk = pl.BlockSpec((tb, LANES), lambda j, i: (i, j))
    full = pl.BlockSpec((T, LANES), lambda j, i: (0, j))
    body = _hosted(body, 8, 6, 5, host, (N_PAIRS, nb))
    return pl.pallas_call(
        body, name=name, grid=(N_PAIRS, nb),
        in_specs=[blk,
                  pl.BlockSpec((T, LANES), lambda j, i: (0, cb + j)),
                  pl.BlockSpec((T, LANES), lambda j, i: (0, 2 * cb + j)),
                  pl.BlockSpec((tb, LANES), lambda j, i: (i, 3 * cb + j)),
                  pl.BlockSpec((1, 2, nb, tb), lambda j, i: (j, 0, 0, 0)),
                  blk, blk, pl.BlockSpec((2, tb, LANES), lambda j, i: (j, i, 0))] + x_in,
        out_specs=[blk, blk, full, full,
                   pl.BlockSpec((1, 1, 2, tb), lambda j, i: (j, i, 0, 0)),
                   pl.BlockSpec((1, 2, nb, tb), lambda j, i: (j, 0, 0, 0))] + x_out,
        out_shape=[jax.ShapeDtypeStruct((T, D), BF16)] * 4
        + [jax.ShapeDtypeStruct((N_PAIRS, nb, 2, tb), F32),
           jax.ShapeDtypeStruct((N_PAIRS, 2, nb, tb), F32)] + x_shapes,
        scratch_shapes=[pltpu.VMEM((T, LANES), BF16), pltpu.VMEM((T, LANES), BF16),
                        pltpu.VMEM((T, LANES), F32), pltpu.VMEM((T, LANES), F32),
                        pltpu.VMEM((2, nb, tb), F32)] + x_scratch,
        compiler_params=_cparams("arbitrary", "arbitrary"),
    )(proj, proj, proj, proj, cum4, o, dog, lp, *x_args)


def _cumsum_fwd(proj, bf_row, *, tt, name):
    T = proj.shape[0]
    cb = (proj.shape[1] - LANES) // LANES

    def body(f_ref, b_ref, out_ref, carry_ref):
        i = pl.program_id(0)

        @pl.when(i == 0)
        def _():
            carry_ref[...] = jnp.zeros_like(carry_ref)

        ls = -_softplus(-(f_ref[...] + b_ref[...]))
        tri = (lax.broadcasted_iota(jnp.int32, (tt, tt), 0)
               >= lax.broadcasted_iota(jnp.int32, (tt, tt), 1)).astype(F32)
        cum = jnp.dot(tri, ls, preferred_element_type=F32,
                      precision=lax.Precision.HIGHEST) + carry_ref[...]
        carry_ref[...] = cum[tt - 1:tt, :]
        out_ref[...] = cum.T

    return pl.pallas_call(
        body, name=name, grid=(T // tt,),
        in_specs=[pl.BlockSpec((tt, LANES), lambda i: (i, cb)),
                  pl.BlockSpec((1, LANES), lambda i: (0, 0))],
        out_specs=pl.BlockSpec((LANES, tt), lambda i: (0, i)),
        out_shape=jax.ShapeDtypeStruct((LANES, T), F32),
        scratch_shapes=[pltpu.VMEM((1, LANES), F32)],
        compiler_params=_cparams("arbitrary"),
    )(proj, bf_row)


def _cumsum_bwd(dcum_t, proj, bf_row, *, tt, name):
    T = proj.shape[0]
    cb = (proj.shape[1] - LANES) // LANES
    nt = T // tt

    def body(dc_ref, f_ref, b_ref, df_ref, db_ref, carry_ref):
        i = pl.program_id(0)

        @pl.when(i == 0)
        def _():
            carry_ref[...] = jnp.zeros_like(carry_ref)
            db_ref[...] = jnp.zeros_like(db_ref)

        dc = dc_ref[...].T
        tri = (lax.broadcasted_iota(jnp.int32, (tt, tt), 0)
               <= lax.broadcasted_iota(jnp.int32, (tt, tt), 1)).astype(F32)
        rev = jnp.dot(tri, dc, preferred_element_type=F32,
                      precision=lax.Precision.HIGHEST) + carry_ref[...]
        carry_ref[...] = rev[0:1, :]
        df = rev * _sigmoid(-(f_ref[...] + b_ref[...]))
        df_ref[...] = df.astype(BF16)
        db_ref[...] += jnp.sum(df, axis=0, keepdims=True)

    return pl.pallas_call(
        body, name=name, grid=(nt,),
        in_specs=[pl.BlockSpec((LANES, tt), lambda i: (0, nt - 1 - i)),
                  pl.BlockSpec((tt, LANES), lambda i: (nt - 1 - i, cb)),
                  pl.BlockSpec((1, LANES), lambda i: (0, 0))],
        out_specs=[pl.BlockSpec((tt, LANES), lambda i: (nt - 1 - i, 0)),
                   pl.BlockSpec((1, LANES), lambda i: (0, 0))],
        out_shape=[jax.ShapeDtypeStruct((T, LANES), BF16), jax.ShapeDtypeStruct((1, LANES), F32)],
        scratch_shapes=[pltpu.VMEM((1, LANES), F32)],
        compiler_params=_cparams("arbitrary"),
    )(dcum_t, proj, bf_row)


def _rg_gates(upad_ref, small_ref, wa_ref, wi_ref, tt):
    off = SUBLANES - (CONV_WIDTH - 1)
    u = small_ref[4:5, :]
    for tap in range(CONV_WIDTH):
        u = u + upad_ref[off + tap:off + tap + tt, :] * small_ref[tap:tap + 1, :]
    pa, pi = [], []
    for n in range(RNN_BLOCKS):
        ub = u[:, n * RNN_BLOCK_WIDTH:(n + 1) * RNN_BLOCK_WIDTH].astype(BF16)
        pa.append(jnp.dot(ub, wa_ref[n], preferred_element_type=F32))
        pi.append(jnp.dot(ub, wi_ref[n], preferred_element_type=F32))
    r = _sigmoid(jnp.concatenate(pa, axis=-1) + small_ref[5:6, :])
    ig = _sigmoid(jnp.concatenate(pi, axis=-1) + small_ref[6:7, :])
    spl = _softplus(-small_ref[7:8, :])
    log_a = (-LRU_C) * r * spl
    a = jnp.exp(log_a)
    s = jnp.sqrt(jnp.tanh(-log_a) * (a * a + 1.0))
    return u, r, ig, spl, a, s


def _rg_fwd(proj, small, wa, wi, *, tt, name):
    T = proj.shape[0]
    D = RNN_BLOCKS * RNN_BLOCK_WIDTH
    hb = tt // SUBLANES

    def body(u0_ref, halo_ref, g_ref, small_ref, wa_ref, wi_ref, h_ref, y_ref,
             upad_ref, a_ref, b_ref, carry_ref):
        i = pl.program_id(0)

        @pl.when(i == 0)
        def _():
            carry_ref[...] = jnp.zeros_like(carry_ref)

        upad_ref[0:SUBLANES, :] = jnp.where(i == 0, 0.0, halo_ref[...])
        upad_ref[SUBLANES:, :] = u0_ref[...]
        u, r, ig, spl, a, s = _rg_gates(upad_ref, small_ref, wa_ref, wi_ref, tt)
        a_ref[...] = a
        b_ref[...] = s * (ig * u)

        def row(t, h):
            h = a_ref[pl.ds(t, 1), :] * h + b_ref[pl.ds(t, 1), :]
            h_ref[pl.ds(t, 1), :] = h
            return h

        carry_ref[...] = lax.fori_loop(0, tt, row, carry_ref[...], unroll=8)
        gate = g_ref[...]
        y_ref[...] = (h_ref[...] * (gate * _sigmoid(gate))).astype(BF16)

    return pl.pallas_call(
        body, name=name, grid=(T // tt,),
        in_specs=[pl.BlockSpec((tt, D), lambda i: (i, 0)),
                  pl.BlockSpec((SUBLANES, D), lambda i: (jnp.maximum(i * hb - 1, 0), 0)),
                  pl.BlockSpec((tt, D), lambda i: (i, 1)),
                  pl.BlockSpec((SUBLANES, D), lambda i: (0, 0)),
                  pl.BlockSpec((RNN_BLOCKS, RNN_BLOCK_WIDTH, RNN_BLOCK_WIDTH), lambda i: (0, 0, 0)),
                  pl.BlockSpec((RNN_BLOCKS, RNN_BLOCK_WIDTH, RNN_BLOCK_WIDTH), lambda i: (0, 0, 0))],
        out_specs=[pl.BlockSpec((tt, D), lambda i: (i, 0)), pl.BlockSpec((tt, D), lambda i: (i, 0))],
        out_shape=[jax.ShapeDtypeStruct((T, D), F32), jax.ShapeDtypeStruct((T, D), BF16)],
        scratch_shapes=[pltpu.VMEM((tt + SUBLANES, D), F32), pltpu.VMEM((tt, D), F32),
                        pltpu.VMEM((tt, D), F32), pltpu.VMEM((1, D), F32)],
        compiler_params=_cparams("arbitrary"),
    )(proj, proj, proj, small, wa, wi)


def _rg_bwd(proj, hs, dy, small, wa, wi, *, tt, name):
    T = proj.shape[0]
    D = RNN_BLOCKS * RNN_BLOCK_WIDTH
    W = RNN_BLOCK_WIDTH
    hb = tt // SUBLANES
    nt = T // tt

    def body(u0_ref, uhalo_ref, g_ref, h_ref, hhalo_ref, dy_ref, small_ref, wa_ref, wi_ref,
             dp_ref, dwa_ref, dwi_ref, ds_ref,
             upad_ref, hpad_ref, a_ref, g_s_ref, duext_ref, carry_ref):
        i = pl.program_id(0)
        first_chunk = i == nt - 1

        @pl.when(i == 0)
        def _():
            carry_ref[...] = jnp.zeros_like(carry_ref)
            duext_ref[...] = jnp.zeros_like(duext_ref)
            dwa_ref[...] = jnp.zeros_like(dwa_ref)
            dwi_ref[...] = jnp.zeros_like(dwi_ref)
            ds_ref[...] = jnp.zeros_like(ds_ref)

        upad_ref[0:SUBLANES, :] = jnp.where(first_chunk, 0.0, uhalo_ref[...])
        upad_ref[SUBLANES:, :] = u0_ref[...]
        hpad_ref[0:SUBLANES, :] = jnp.where(first_chunk, 0.0, hhalo_ref[...])
        hpad_ref[SUBLANES:, :] = h_ref[...]
        u, r, ig, spl, a, s = _rg_gates(upad_ref, small_ref, wa_ref, wi_ref, tt)
        gate = g_ref[...]
        sg = _sigmoid(gate)
        dy = dy_ref[...]
        dp_ref[:, D:] = (dy * h_ref[...] * (sg * (1.0 + gate * (1.0 - sg)))).astype(BF16)
        a_ref[...] = a
        g_s_ref[...] = dy * (gate * sg)

        def row(k, c):
            t = tt - 1 - k
            g = g_s_ref[pl.ds(t, 1), :] + c
            g_s_ref[pl.ds(t, 1), :] = g
            return a_ref[pl.ds(t, 1), :] * g

        carry_ref[...] = lax.fori_loop(0, tt, row, carry_ref[...], unroll=8)
        g = g_s_ref[...]
        h_prev = hpad_ref[SUBLANES - 1:SUBLANES - 1 + tt, :]
        iu = ig * u
        d_iu = g * s
        dlog_a = (g * h_prev) * a - (g * iu) * (a * a) / s
        dpre_a = (dlog_a * ((-LRU_C) * spl)) * r * (1.0 - r)
        dpre_i = (d_iu * u) * ig * (1.0 - ig)
        dlam = jnp.sum(dlog_a * r, axis=0, keepdims=True) * (LRU_C * _sigmoid(-small_ref[7:8, :]))
        du_parts = []
        for n in range(RNN_BLOCKS):
            sl = slice(n * W, (n + 1) * W)
            ub = u[:, sl].astype(BF16)
            da_n = dpre_a[:, sl].astype(BF16)
            di_n = dpre_i[:, sl].astype(BF16)
            dwa_ref[n] += lax.dot_general(ub, da_n, (((0,), (0,)), ((), ())),
                                          preferred_element_type=F32)
            dwi_ref[n] += lax.dot_general(ub, di_n, (((0,), (0,)), ((), ())),
                                          preferred_element_type=F32)
            du_parts.append(
                lax.dot_general(da_n, wa_ref[n], (((1,), (1,)), ((), ())), preferred_element_type=F32)
                + lax.dot_general(di_n, wi_ref[n], (((1,), (1,)), ((), ())), preferred_element_type=F32))
        du = d_iu * ig + jnp.concatenate(du_parts, axis=-1)
        off = SUBLANES - (CONV_WIDTH - 1)
        for tap in range(CONV_WIDTH):
            ds_ref[tap:tap + 1, :] += jnp.sum(du * upad_ref[off + tap:off + tap + tt, :],
                                              axis=0, keepdims=True)
        ds_ref[4:5, :] += jnp.sum(du, axis=0, keepdims=True)
        ds_ref[5:6, :] += jnp.sum(dpre_a, axis=0, keepdims=True)
        ds_ref[6:7, :] += jnp.sum(dpre_i, axis=0, keepdims=True)
        ds_ref[7:8, :] += dlam
        duext_ref[0:tt, :] = du
        du0 = jnp.zeros((tt, D), F32)
        for tap in range(CONV_WIDTH):
            sh = CONV_WIDTH - 1 - tap
            du0 = du0 + duext_ref[sh:sh + tt, :] * small_ref[tap:tap + 1, :]
        dp_ref[:, :D] = du0.astype(BF16)
        duext_ref[tt:, :] = du[0:SUBLANES, :]

    rev = lambda i: nt - 1 - i
    wspec = pl.BlockSpec((RNN_BLOCKS, W, W), lambda i: (0, 0, 0))
    return pl.pallas_call(
        body, name=name, grid=(nt,),
        in_specs=[pl.BlockSpec((tt, D), lambda i: (rev(i), 0)),
                  pl.BlockSpec((SUBLANES, D), lambda i: (jnp.maximum(rev(i) * hb - 1, 0), 0)),
                  pl.BlockSpec((tt, D), lambda i: (rev(i), 1)),
                  pl.BlockSpec((tt, D), lambda i: (rev(i), 0)),
                  pl.BlockSpec((SUBLANES, D), lambda i: (jnp.maximum(rev(i) * hb - 1, 0), 0)),
                  pl.BlockSpec((tt, D), lambda i: (rev(i), 0)),
                  pl.BlockSpec((SUBLANES, D), lambda i: (0, 0)),
                  wspec, wspec],
        out_specs=[pl.BlockSpec((tt, 2 * D), lambda i: (rev(i), 0)),
                   wspec, wspec, pl.BlockSpec((SUBLANES, D), lambda i: (0, 0))],
        out_shape=[jax.ShapeDtypeStruct((T, 2 * D), BF16),
                   jax.ShapeDtypeStruct((RNN_BLOCKS, W, W), F32),
                   jax.ShapeDtypeStruct((RNN_BLOCKS, W, W), F32),
                   jax.ShapeDtypeStruct((SUBLANES, D), F32)],
        scratch_shapes=[pltpu.VMEM((tt + SUBLANES, D), F32), pltpu.VMEM((tt + SUBLANES, D), F32),
                        pltpu.VMEM((tt, D), F32), pltpu.VMEM((tt, D), F32),
                        pltpu.VMEM((tt + SUBLANES, D), F32), pltpu.VMEM((1, D), F32)],
        compiler_params=_cparams("arbitrary"),
    )(proj, proj, proj, hs, hs, dy, small, wa, wi)


def _out_ln(a, w, x, g, b, *, tt, name):
    T, D = x.shape
    K = a.shape[1]

    def body(a_ref, w_ref, x_ref, g_ref, b_ref, y_ref, yb_ref, zh_ref, rs_ref):
        h = jnp.dot(a_ref[...].astype(BF16), w_ref[...].astype(BF16), preferred_element_type=F32)
        z = ALPHA * x_ref[...] + h
        mu = jnp.mean(z, axis=-1, keepdims=True)
        zc = z - mu
        rstd = lax.rsqrt(jnp.mean(zc * zc, axis=-1, keepdims=True) + LN_EPS)
        zh = zc * rstd
        zh_ref[...] = zh
        rs_ref[...] = rstd
        y = zh * g_ref[...] + b_ref[...]
        y_ref[...] = y
        yb_ref[...] = y.astype(BF16)

    blk = pl.BlockSpec((tt, D), lambda i: (i, 0))
    row = pl.BlockSpec((1, D), lambda i: (0, 0))
    return pl.pallas_call(
        body, name=name, grid=(T // tt,),
        in_specs=[pl.BlockSpec((tt, K), lambda i: (i, 0)), pl.BlockSpec((K, D), lambda i: (0, 0)),
                  blk, row, row],
        out_specs=[blk, blk, blk, pl.BlockSpec((tt, 1), lambda i: (i, 0))],
        out_shape=[jax.ShapeDtypeStruct((T, D), F32), jax.ShapeDtypeStruct((T, D), BF16),
                   jax.ShapeDtypeStruct((T, D), F32), jax.ShapeDtypeStruct((T, 1), F32)],
        compiler_params=_cparams("parallel"),
    )(a, w, x, g, b)


def _ln_bwd_tile(dy, zh_ref, rs_ref, g_ref, dz_ref, dzb_ref, dg_ref, db_ref, first):
    @pl.when(first)
    def _():
        dg_ref[...] = jnp.zeros_like(dg_ref)
        db_ref[...] = jnp.zeros_like(db_ref)

    zh = zh_ref[...]
    dg_ref[...] += jnp.sum(dy * zh, axis=0, keepdims=True)
    db_ref[...] += jnp.sum(dy, axis=0, keepdims=True)
    dzh = dy * g_ref[...]
    m1 = jnp.mean(dzh, axis=-1, keepdims=True)
    m2 = jnp.mean(dzh * zh, axis=-1, keepdims=True)
    dz = rs_ref[...] * (dzh - m1 - zh * m2)
    dz_ref[...] = dz
    dzb_ref[...] = dz.astype(BF16)


def _ln_bwd_specs(T, D, tt, index):
    blk = pl.BlockSpec((tt, D), index(lambda i: (i, 0)))
    row = pl.BlockSpec((1, D), index(lambda i: (0, 0)))
    return ([blk, pl.BlockSpec((tt, 1), index(lambda i: (i, 0))), row], [blk, blk, row, row],
            [jax.ShapeDtypeStruct((T, D), F32), jax.ShapeDtypeStruct((T, D), BF16),
             jax.ShapeDtypeStruct((1, D), F32), jax.ShapeDtypeStruct((1, D), F32)])


def _loss_ln_bwd(y, tgt, zh, rstd, g, *, tt, name):
    T, D = y.shape
    ln_in, ln_out, ln_shapes = _ln_bwd_specs(T, D, tt, lambda f: f)

    def body(y_ref, t_ref, zh_ref, rs_ref, g_ref, l_ref, dz_ref, dzb_ref, dg_ref, db_ref):
        first = pl.program_id(0) == 0

        @pl.when(first)
        def _():
            l_ref[...] = jnp.zeros_like(l_ref)

        e = y_ref[...] - t_ref[...]
        l_ref[...] += jnp.sum(e * e, axis=0, keepdims=True) * (0.5 / D)
        _ln_bwd_tile(e * (1.0 / D), zh_ref, rs_ref, g_ref, dz_ref, dzb_ref, dg_ref, db_ref, first)

    blk = pl.BlockSpec((tt, D), lambda i: (i, 0))
    return pl.pallas_call(
        body, name=name, grid=(T // tt,),
        in_specs=[blk, blk] + ln_in,
        out_specs=[pl.BlockSpec((1, D), lambda i: (0, 0))] + ln_out,
        out_shape=[jax.ShapeDtypeStruct((1, D), F32)] + ln_shapes,
        compiler_params=_cparams("arbitrary"),
    )(y, tgt, zh, rstd, g)


def _dx_ln_bwd(a, b, add, zh, rstd, g, *, tm, name):
    T, D = add.shape
    na = len(a)
    K = sum(p.shape[1] for p in a)
    ln_in, ln_out, ln_shapes = _ln_bwd_specs(T, D, tm, lambda f: f)

    def body(*refs):
        a_refs, b_ref, add_ref = refs[:na], refs[na], refs[na + 1]
        av = [r[...].astype(BF16) for r in a_refs]
        av = av[0] if na == 1 else jnp.concatenate(av, axis=1)
        dy = lax.dot_general(av, b_ref[...].astype(BF16), (((1,), (1,)), ((), ())),
                             preferred_element_type=F32) + ALPHA * add_ref[...]
        _ln_bwd_tile(dy, *refs[na + 2:], pl.program_id(0) == 0)

    return pl.pallas_call(
        body, name=name, grid=(T // tm,),
        in_specs=[pl.BlockSpec((tm, p.shape[1]), lambda i: (i, 0)) for p in a]
        + [pl.BlockSpec((D, K), lambda i: (0, 0)), pl.BlockSpec((tm, D), lambda i: (i, 0))] + ln_in,
        out_specs=ln_out, out_shape=ln_shapes,
        compiler_params=_cparams("arbitrary"),
    )(*a, b, add, zh, rstd, g)


def _row_tile(rows, target):
    best = SUBLANES
    for t in range(SUBLANES, target + 1, SUBLANES):
        if rows % t == 0:
            best = t
    return best


def _add_own(g, recv, c_idx, *, tr, name, narrow=False):
    _, M, R, C = g.shape

    def body(c_ref, g_ref, r_ref, *o_refs):
        s = g_ref[0] + r_ref[...]
        for o_ref in o_refs:
            o_ref[...] = s.astype(o_ref.dtype)

    blk = pl.BlockSpec((1, tr, C), lambda k, i, c: (k, i, 0))
    dtypes = [F32, BF16] if narrow else [F32]
    outs = pl.pallas_call(
        body, name=name,
        grid_spec=pltpu.PrefetchScalarGridSpec(
            num_scalar_prefetch=1, grid=(M, R // tr),
            in_specs=[pl.BlockSpec((1, 1, tr, C), lambda k, i, c: (c[0], k, i, 0)), blk],
            out_specs=[blk] * len(dtypes)),
        out_shape=[jax.ShapeDtypeStruct((M, R, C), d) for d in dtypes],
        compiler_params=_cparams("parallel", "parallel"),
    )(c_idx, g, recv)
    return outs if narrow else outs[0]


def _adamw_math(g, w_ref, m_ref, v_ref, g_ref, d_ref, nm_ref, nv_ref):
    nm = ADAM_B1 * m_ref[...] + (1.0 - ADAM_B1) * g
    nv = ADAM_B2 * v_ref[...] + (1.0 - ADAM_B2) * (g * g)
    m_hat = nm / (1.0 - ADAM_B1 ** ADAM_STEP)
    v_hat = nv / (1.0 - ADAM_B2 ** ADAM_STEP)
    g_ref[...] = g
    nm_ref[...] = nm
    nv_ref[...] = nv
    d_ref[...] = (-ADAM_LR) * (m_hat / (jnp.sqrt(v_hat) + ADAM_EPS) + ADAM_WD * w_ref[...])


def _adamw(parts, w, m, v, *, tr, name):
    n, R, C = parts.shape
    tr = min(tr, R)

    def body(p_ref, w_ref, m_ref, v_ref, *out_refs):
        g = p_ref[0]
        for k in range(1, n):
            g = g + p_ref[k]
        _adamw_math(g, w_ref, m_ref, v_ref, *out_refs)

    blk = pl.BlockSpec((tr, C), lambda i: (i, 0))
    out = jax.ShapeDtypeStruct((R, C), F32)
    return pl.pallas_call(
        body, name=name, grid=(R // tr,),
        in_specs=[pl.BlockSpec((n, tr, C), lambda i: (0, i, 0)), blk, blk, blk],
        out_specs=[blk, blk, blk, blk], out_shape=[out, out, out, out],
        compiler_params=_cparams("parallel"),
    )(parts, w, m, v)


def _adamw_shard(parts, place, w, m, v, *, idx, prev, tr, name):
    R, C = parts[0][0].shape[-2:]
    n_parts, n_prev = len(parts), 0 if prev is None else 4

    def body(place_ref, *refs):
        p_refs, (w_ref, m_ref, v_ref) = refs[:n_parts], refs[n_parts:n_parts + 3]
        g = None
        for r in p_refs:
            blk = r[(0,) * (len(r.shape) - 3)].astype(F32)
            g = blk if g is None else g + blk
        _adamw_math(g, w_ref, m_ref, v_ref, *refs[n_parts + 3 + n_prev:])

    blk = pl.BlockSpec((1, tr, C), lambda i, s: (idx, i, 0))

    def part_spec(a, pick):
        return pl.BlockSpec((1,) * (a.ndim - 2) + (tr, C), lambda i, s: (*pick(s), i, 0))

    out = jax.ShapeDtypeStruct(w.shape, F32)
    return pl.pallas_call(
        body, name=name,
        grid_spec=pltpu.PrefetchScalarGridSpec(
            num_scalar_prefetch=1, grid=(R // tr,),
            in_specs=[part_spec(a, pick) for a, pick in parts] + [blk, blk, blk]
            + [pl.BlockSpec(memory_space=pl.ANY)] * n_prev,
            out_specs=[blk, blk, blk, blk]),
        out_shape=[out, out, out, out],
        input_output_aliases={1 + n_parts + 3 + j: j for j in range(n_prev)},
        compiler_params=_cparams("parallel"),
    )(place, *[a for a, _ in parts], w, m, v, *(prev or ()))


def _two_stage_parts(h, recv):
    return [(h, lambda s: (s[0], 0))] + [(recv, lambda s, d=d: (s[0] ^ d, 0)) for d in (1, 2, 3)]


def _direct_parts(g, recv):
    return [(g, lambda s: (s[1], s[0]))] + [
        (recv, lambda s, a=p // 4, d=p % 4: (s[1] ^ a, s[0] ^ d)) for p in range(1, 8)]


SHARD_AXIS = dict(attn_w_in=1, attn_w_out=0, rnn_w_in=1, rnn_w_out=0, rnn_w_a=1, rnn_w_i=1,
                  rnn_conv_w=1, rnn_conv_b=0, rnn_b_a=0, rnn_b_i=0, rnn_lambda=0)
RNN_ROWED = ("rnn_w_out", "rnn_w_a", "rnn_w_i")
SMALL = ("rnn_conv_w", "rnn_conv_b", "rnn_b_a", "rnn_b_i", "rnn_lambda")
PACK_C = 1024


def _elems(shape):
    n = 1
    for s in shape:
        n *= s
    return n


def _pack_rows(p, idx, dtype):
    parts = [p[k][idx].astype(dtype).reshape(-1, PACK_C) for k in RNN_ROWED]
    small = jnp.concatenate([p[k][idx].reshape(-1) for k in SMALL])
    tile_rows = SUBLANES * (4 // jnp.dtype(dtype).itemsize)
    if dtype == BF16:
        small = lax.bitcast_convert_type(small, BF16)
    small = small.reshape(-1, PACK_C)
    parts.append(jnp.pad(small, ((0, tile_rows - small.shape[0]), (0, 0))))
    return jnp.concatenate(parts, axis=0)


def _unpack_rows(flat, shapes):
    out, r = {}, 0
    for k in RNN_ROWED:
        n = _elems(shapes[k]) // PACK_C
        out[k] = flat[r:r + n].reshape(shapes[k])
        r += n
    n_small = sum(_elems(shapes[k]) for k in SMALL)
    small = flat[r:r + n_small // PACK_C].reshape(-1)
    o = 0
    for k in SMALL:
        n = _elems(shapes[k])
        out[k] = small[o:o + n].reshape(shapes[k])
        o += n
    return out


def _join_columns(g, width, *, tr, name):
    _, _, R, S = g.shape

    def body(*refs):
        o_ref = refs[8]
        parts = [refs[r][0, 0].astype(F32) for r in range(8)]
        parts.append(jnp.zeros((tr, width - 8 * S), F32))
        o_ref[...] = jnp.concatenate(parts, axis=-1).astype(o_ref.dtype)

    def shard(r):
        return pl.BlockSpec((1, 1, tr, S), lambda i: (r % 2, r // 2, i, 0))

    return pl.pallas_call(
        body, name=name, grid=(R // tr,),
        in_specs=[shard(r) for r in range(8)],
        out_specs=pl.BlockSpec((tr, width), lambda i: (i, 0)),
        out_shape=jax.ShapeDtypeStruct((R, width), g.dtype),
        compiler_params=_cparams("parallel"),
    )(*([g] * 8))


def _split_columns(parts, S, *, tr, name):
    R = parts[0].shape[0]
    n = len(parts)

    def body(*refs):
        o_ref = refs[n]
        x = jnp.concatenate([r[...] for r in refs[:n]], axis=1)
        for r in range(8):
            o_ref[r % 2, r // 2] = x[:, r * S:(r + 1) * S]

    return pl.pallas_call(
        body, name=name, grid=(R // tr,),
        in_specs=[pl.BlockSpec((tr, p.shape[1]), lambda i: (i, 0)) for p in parts],
        out_specs=pl.BlockSpec((2, 4, tr, S), lambda i: (0, 0, i, 0)),
        out_shape=jax.ShapeDtypeStruct((2, 4, R, S), parts[0].dtype),
        compiler_params=_cparams("parallel"),
    )(*parts)


def _to_full(g, k, sh):
    ax, nd = SHARD_AXIS[k], len(sh)
    perm = tuple(range(2, 2 + ax)) + (1, 0) + tuple(range(2 + ax, 2 + nd))
    return g.transpose(perm).reshape(sh[:ax] + (8 * sh[ax],) + sh[ax + 1:])


def _from_full(full, k, sh):
    ax, nd = SHARD_AXIS[k], len(sh)
    t = full.reshape(sh[:ax] + (4, 2, sh[ax]) + sh[ax + 1:])
    return t.transpose((ax + 1, ax) + tuple(range(ax)) + tuple(range(ax + 2, nd + 2)))


def _unpack_gathered_rows(g, shapes):
    out, r = {}, 0
    for k in RNN_ROWED:
        n = _elems(shapes[k]) // PACK_C
        out[k] = _to_full(g[:, :, r:r + n].reshape((2, 4) + shapes[k]), k, shapes[k])
        r += n
    n_small = sum(_elems(shapes[k]) for k in SMALL)
    nr = 2 * n_small // PACK_C
    small = lax.bitcast_convert_type(g[:, :, r:r + nr].reshape(2, 4, n_small, 2), F32)
    o = 0
    for k in SMALL:
        n = _elems(shapes[k])
        out[k] = _to_full(small[:, :, o:o + n].reshape((2, 4) + shapes[k]), k, shapes[k])
        o += n
    return out


def _pack_grad_rows(full, shapes):
    parts = [_from_full(full[k], k, shapes[k]).reshape(2, 4, -1, PACK_C) for k in RNN_ROWED]
    small = jnp.concatenate(
        [_from_full(full[k], k, shapes[k]).reshape(2, 4, -1) for k in SMALL], axis=-1)
    small = small.reshape(2, 4, -1, PACK_C)
    parts.append(jnp.pad(small, ((0, 0), (0, 0), (0, SUBLANES - small.shape[2]), (0, 0))))
    return jnp.concatenate(parts, axis=2)


def kernel(x, ln_g, ln_b, attn_w_in, attn_b_f, attn_w_out, rnn_w_in, rnn_conv_w, rnn_conv_b, rnn_w_a, rnn_b_a, rnn_w_i, rnn_b_i, rnn_lambda, rnn_w_out, loss_target, m_ln_g, m_ln_b, m_attn_w_in, m_attn_b_f, m_attn_w_out, m_rnn_w_in, m_rnn_conv_w, m_rnn_conv_b, m_rnn_w_a, m_rnn_b_a, m_rnn_w_i, m_rnn_b_i, m_rnn_lambda, m_rnn_w_out, v_ln_g, v_ln_b, v_attn_w_in, v_attn_b_f, v_attn_w_out, v_rnn_w_in, v_rnn_conv_w, v_rnn_conv_b, v_rnn_w_a, v_rnn_b_a, v_rnn_w_i, v_rnn_b_i, v_rnn_lambda, v_rnn_w_out):
    w_loc = dict(attn_w_in=attn_w_in, attn_w_out=attn_w_out, rnn_w_in=rnn_w_in, rnn_w_a=rnn_w_a,
                 rnn_w_i=rnn_w_i, rnn_w_out=rnn_w_out, rnn_conv_w=rnn_conv_w, rnn_conv_b=rnn_conv_b,
                 rnn_b_a=rnn_b_a, rnn_b_i=rnn_b_i, rnn_lambda=rnn_lambda)
    m_loc = dict(attn_w_in=m_attn_w_in, attn_w_out=m_attn_w_out, rnn_w_in=m_rnn_w_in,
                 rnn_w_a=m_rnn_w_a, rnn_w_i=m_rnn_w_i, rnn_w_out=m_rnn_w_out,
                 rnn_conv_w=m_rnn_conv_w, rnn_conv_b=m_rnn_conv_b, rnn_b_a=m_rnn_b_a,
                 rnn_b_i=m_rnn_b_i, rnn_lambda=m_rnn_lambda)
    v_loc = dict(attn_w_in=v_attn_w_in, attn_w_out=v_attn_w_out, rnn_w_in=v_rnn_w_in,
                 rnn_w_a=v_rnn_w_a, rnn_w_i=v_rnn_w_i, rnn_w_out=v_rnn_w_out,
                 rnn_conv_w=v_rnn_conv_w, rnn_conv_b=v_rnn_conv_b, rnn_b_a=v_rnn_b_a,
                 rnn_b_i=v_rnn_b_i, rnn_lambda=v_rnn_lambda)
    shapes = {k: tuple(a.shape[1:]) for k, a in w_loc.items()}
    T, D = x.shape[1], x.shape[2]
    n_f = attn_b_f.shape[1]
    tb = min(1024, T)
    tb_bwd = min(512, T)
    tt_rg = min(128, T)
    tt_ln = min(256, T)
    c_idx = lax.axis_index("c").astype(jnp.int32).reshape(1)
    me_idx = (2 * lax.axis_index("x") + lax.axis_index("y")).astype(jnp.int32).reshape(1)
    place = jnp.concatenate([me_idx, c_idx])

    def attn_w_in_full(g_in, idx):
        return _join_columns(g_in, 4 * D + LANES, tr=256, name=f"a_join{idx}")

    def attn_w_out_full(g_out):
        return _to_full(g_out, "attn_w_out", shapes["attn_w_out"])

    def rnn_weights(g_in, g_rows):
        w = _unpack_gathered_rows(g_rows, shapes)
        w["rnn_w_in"] = _to_full(g_in, "rnn_w_in", shapes["rnn_w_in"])
        w["small"] = jnp.concatenate([w["rnn_conv_w"], w["rnn_conv_b"][None], w["rnn_b_a"][None],
                                      w["rnn_b_i"][None], w["rnn_lambda"][None]])
        return w

    g0 = _ag_c(_run_exchange(_Exchange("gather", [attn_w_in[0].astype(BF16)]), "ag_w0_xy"),
               "ag_w0_c")
    later = _Exchange("gather", [
        attn_w_out.astype(BF16), attn_w_in[1].astype(BF16), rnn_w_in.astype(BF16),
        jnp.stack([_pack_rows(w_loc, i, BF16) for i in range(2)])])
    w_attn_in, w_attn_out, w_rnn = [attn_w_in_full(g0[0], 0), None], [None, None], [None, None]
    bf_rows = jnp.pad(attn_b_f, ((0, 0), (0, LANES - n_f)))[:, None, :]

    xs, xb, saved = [x[0]], [x[0]], []
    for layer in range(DEPTH):
        idx, xl, xm = layer // 2, xs[-1], xb[-1]
        if layer % 2 == 0:
            proj = _matmul(xm, w_attn_in[idx], trans_b=False, tm=512, tn=1408,
                           name=f"a_proj{layer}")
            cum_t = _cumsum_fwd(proj, bf_rows[idx], tt=min(512, T), name=f"a_cum{layer}")
            cum2 = cum_t[:N_HEADS].reshape(N_PAIRS, 2, T)
            o, og, lp, *got = _flash_fwd(proj, cum2.reshape(N_PAIRS, 2, T // tb, tb), tb=tb,
                                         name=f"a_fwd{layer}", host=later if layer == 0 else None)
            cum4 = cum2.reshape(N_PAIRS, 2, T // tb_bwd, tb_bwd)
            if layer == 0:
                g1 = _ag_c(got, "ag_w1_c")
                w_attn_out = [attn_w_out_full(g1[0][:, :, i]) for i in range(2)]
                w_attn_in[1] = attn_w_in_full(g1[1], 1)
                w_rnn = [rnn_weights(g1[2][:, :, i], g1[3][:, :, i]) for i in range(2)]
            branch, w_out = og, w_attn_out[idx]
            saved.append((proj, cum4, o, og, lp))
        else:
            w = w_rnn[idx]
            proj = _matmul(xm, w["rnn_w_in"], trans_b=False, tm=512, tn=1024,
                           name=f"r_proj{layer}")
            hs, yr = _rg_fwd(proj, w["small"], w["rnn_w_a"], w["rnn_w_i"], tt=tt_rg,
                             name=f"r_fwd{layer}")
            branch, w_out = yr, w["rnn_w_out"]
            saved.append((proj, hs, yr))
        y, yb, zh, rstd = _out_ln(branch, w_out, xl, ln_g[layer][None], ln_b[layer][None],
                                  tt=512, name=f"out_ln{layer}")
        saved[-1] = saved[-1] + (zh, rstd)
        xs.append(y)
        xb.append(yb)

    def ln_below(layer):
        return saved[layer][-2:] + (ln_g[layer][None],)

    loss_lanes, *ln_grads = _loss_ln_bwd(xs[-1], loss_target[0], *ln_below(DEPTH - 1), tt=tt_ln,
                                         name="loss_ln_bwd")
    loss = lax.psum(jnp.sum(loss_lanes), ("x", "y", "c"))

    def reduce_pair(gs, layer):
        recv = _rs_c(gs, f"rs_c{layer}")
        outs = [_add_own(g, r, c_idx, tr=_row_tile(g.shape[2], 512), name=f"rs_add{layer}_{n}",
                         narrow=True) for n, (g, r) in enumerate(zip(gs, recv))]
        return [o[0][:, None] for o in outs], [o[1][:, None] for o in outs]

    part, got_parts = [None] * DEPTH, [None] * DEPTH
    d_ln_g, d_ln_b, d_bf = [None] * DEPTH, [None] * DEPTH, [None, None]
    for layer in reversed(range(DEPTH)):
        idx, xm = layer // 2, xb[layer]
        dz, dzb, dg, db = ln_grads
        d_ln_g[layer], d_ln_b[layer] = dg[0], db[0]
        if layer % 2 == 0:
            w_in, w_out = w_attn_in[idx], w_attn_out[idx]
            proj, cum4, o, og, lp = saved[layer][:5]
            dog = _matmul(dzb, w_out, trans_b=True, tm=512, tn=1024, name=f"a_dog{layer}")
            dwo = _matmul_tn(og, dzb, tm=512, tn=1024, tk=1024, name=f"a_dwo{layer}")
            riders = [l for l in range(layer + 1, DEPTH) if got_parts[l] is None]
            host = _Exchange("scatter8", [g for l in riders for g in part[l]]) if riders else None
            dq, dgate, dk, dv, dcum_q, dcum_k, *got = _flash_bwd(proj, cum4, o, dog, lp, tb=tb_bwd,
                                                                 name=f"a_bwd{layer}", host=host)
            for l in riders:
                got_parts[l], got = got[:len(part[l])], got[len(part[l]):]
            dcum_t = (dcum_q.transpose(0, 2, 1, 3) + dcum_k).reshape(N_HEADS, T)
            dcum_t = jnp.pad(dcum_t, ((0, LANES - N_HEADS), (0, 0)))
            df, dbf = _cumsum_bwd(dcum_t, proj, bf_rows[idx], tt=min(512, T), name=f"a_dcum{layer}")
            d_bf[idx] = dbf[0, :n_f]
            dproj = [dq, dk, dv, dgate, df]
            dwi = _matmul_tn_parts(xm, dproj, tm=512, tk=1024, name=f"a_dwi{layer}")
            gs = [_split_columns(dwi, shapes["attn_w_in"][1], tr=256, name=f"a_split{layer}"),
                  _from_full(dwo, "attn_w_out", shapes["attn_w_out"])]
            if layer > 0:
                part[layer] = gs
                ln_grads = _dx_ln_bwd(dproj, w_in, dz, *ln_below(layer - 1), tm=256,
                                      name=f"a_dx{layer}")
            else:
                part[layer], narrow = reduce_pair(gs, layer)
                dy, *got_parts[layer] = _matmul(dproj, w_in, trans_b=True, tm=512, tn=1024,
                                                name=f"a_dx{layer}", add=dz, add_scale=ALPHA,
                                                host=_Exchange("scatter", narrow))
        else:
            w = w_rnn[idx]
            proj, hs, yr = saved[layer][:3]
            dyr = _matmul(dzb, w["rnn_w_out"], trans_b=True, tm=512, tn=1024, name=f"r_dy{layer}")
            dwo = _matmul_tn(yr, dzb, tm=512, tn=1024, tk=1024, name=f"r_dwo{layer}")
            dproj, dwa, dwi_, dsm = _rg_bwd(proj, hs, dyr, w["small"], w["rnn_w_a"], w["rnn_w_i"],
                                            tt=tt_rg, name=f"r_bwd{layer}")
            dwin = _matmul_tn(xm, dproj, tm=512, tn=2048, tk=1024, name=f"r_dwi{layer}")
            ln_grads = _dx_ln_bwd([dproj], w["rnn_w_in"], dz, *ln_below(layer - 1), tm=512,
                                  name=f"r_dx{layer}")
            full = dict(rnn_w_out=dwo, rnn_w_a=dwa, rnn_w_i=dwi_, rnn_conv_w=dsm[0:4],
                        rnn_conv_b=dsm[4], rnn_b_a=dsm[5], rnn_b_i=dsm[6], rnn_lambda=dsm[7])
            part[layer] = [_from_full(dwin, "rnn_w_in", shapes["rnn_w_in"]),
                           _pack_grad_rows(full, shapes)]
    grad_x = dy[None]

    def grad_parts(layer, n):
        make = _two_stage_parts if layer == 0 else _direct_parts
        return make(part[layer][n], got_parts[layer][n])

    def update(k, n):
        res = None
        for idx in (1, 0):
            layer = 2 * idx + (0 if k.startswith("attn") else 1)
            res = _adamw_shard(grad_parts(layer, n), place, w_loc[k], m_loc[k], v_loc[k],
                               idx=idx, prev=res, tr=_row_tile(shapes[k][0], 256),
                               name=f"adamw_{k}{idx}")
        return res

    shard_outs = [dict() for _ in range(4)]
    for k, n in (("attn_w_in", 0), ("attn_w_out", 1), ("rnn_w_in", 0)):
        for j, a in enumerate(update(k, n)):
            shard_outs[j][k] = a
    rows = []
    for idx in range(2):
        layer = 2 * idx + 1
        wmv = [_pack_rows(d, idx, F32)[None] for d in (w_loc, m_loc, v_loc)]
        res = _adamw_shard(grad_parts(layer, 1), place, *wmv, idx=0, prev=None,
                           tr=_row_tile(wmv[0].shape[1], 256), name=f"adamw_rows{idx}")
        rows.append([_unpack_rows(a[0], shapes) for a in res])
    for j in range(4):
        for k in RNN_ROWED + SMALL:
            shard_outs[j][k] = jnp.stack([rows[0][j][k], rows[1][j][k]])
    g_sh, d_sh, nm_sh, nv_sh = shard_outs

    def rep_pack(lg, lb, bf):
        rows = jnp.concatenate([lg, lb, jnp.pad(bf.reshape(1, -1), ((0, 0), (0, D - 2 * n_f)))])
        return jnp.pad(rows, ((0, 16 - rows.shape[0]), (0, 0)))

    rep = _all_gather(rep_pack(jnp.stack(d_ln_g), jnp.stack(d_ln_b), jnp.stack(d_bf)), "ag_rep")
    rg, rd, rm, rv = _adamw(rep.reshape(8, 16, D), rep_pack(ln_g, ln_b, attn_b_f),
                            rep_pack(m_ln_g, m_ln_b, m_attn_b_f),
                            rep_pack(v_ln_g, v_ln_b, v_attn_b_f), tr=16, name="adamw_rep")

    def rep_unpack(a):
        return dict(ln_g=a[0:DEPTH], ln_b=a[DEPTH:2 * DEPTH],
                    attn_b_f=a[2 * DEPTH, :2 * n_f].reshape(2, n_f))

    order = ("ln_g", "ln_b", "attn_w_in", "attn_b_f", "attn_w_out", "rnn_w_in", "rnn_conv_w",
             "rnn_conv_b", "rnn_w_a", "rnn_b_a", "rnn_w_i", "rnn_b_i", "rnn_lambda", "rnn_w_out")
    outs = [loss, grad_x]
    for sh, rp in ((g_sh, rg), (d_sh, rd), (nm_sh, rm), (nv_sh, rv)):
        allp = {**sh, **rep_unpack(rp)}
        outs.extend(allp[k] for k in order)
    return tuple(outs)
```

```python
import jax
import jax.numpy as jnp
from jax import lax
from jax.experimental import pallas as pl
from jax.experimental.pallas import tpu as pltpu

F32 = jnp.float32
BF16 = jnp.bfloat16

DEPTH = 4
N_HEADS = 16
HEAD_DIM = 64
N_PAIRS = N_HEADS // 2
RNN_BLOCKS = 4
RNN_BLOCK_WIDTH = 256
CONV_WIDTH = 4
LRU_C = 8.0
ALPHA = (2.0 * DEPTH) ** 0.25
LN_EPS = 1e-5
ADAM_LR, ADAM_B1, ADAM_B2, ADAM_EPS, ADAM_WD, ADAM_STEP = 0.001, 0.9, 0.999, 1e-8, 0.01, 10

LANES = 128
SUBLANES = 8
VMEM_LIMIT = 48 * 1024 * 1024

MESH = pl.DeviceIdType.MESH
HBM_SPEC = pl.BlockSpec(memory_space=pltpu.HBM)


def _cparams(*sem):
    return pltpu.CompilerParams(dimension_semantics=sem, vmem_limit_bytes=VMEM_LIMIT)


def _sigmoid(x):
    return 1.0 / (1.0 + jnp.exp(-x))


def _softplus(x):
    return jnp.maximum(x, 0.0) + jnp.log(1.0 + jnp.exp(-jnp.abs(x)))


def _a2a(src, *, group, bcast, name):
    n = 2 if group == "c" else 4
    blk = tuple(src.shape) if bcast else tuple(src.shape[1:])

    def body(src_ref, out_ref, send_sems, recv_sems, local_sem):
        x, y, c = lax.axis_index("x"), lax.axis_index("y"), lax.axis_index("c")
        if group == "c":
            me = c

            def peer(d):
                return (x, y, 1 - c), 1 - c
        else:
            me = 2 * x + y

            def peer(d):
                px, py = x ^ (d >> 1), y ^ (d & 1)
                return (px, py, c), 2 * px + py

        def block_for(k):
            return src_ref if bcast else src_ref.at[k]

        local = pltpu.make_async_copy(block_for(me), out_ref.at[me], local_sem)
        local.start()
        sends = []
        for d in range(1, n):
            dev, idx = peer(d)
            cp = pltpu.make_async_remote_copy(
                src_ref=block_for(idx), dst_ref=out_ref.at[me],
                send_sem=send_sems.at[d], recv_sem=recv_sems.at[d],
                device_id=dev, device_id_type=MESH)
            cp.start()
            sends.append(cp)
        for d in range(1, n):
            dev, idx = peer(d)
            pltpu.make_async_remote_copy(
                src_ref=block_for(idx), dst_ref=out_ref.at[idx],
                send_sem=send_sems.at[d], recv_sem=recv_sems.at[d],
                device_id=dev, device_id_type=MESH).wait_recv()
        for cp in sends:
            cp.wait_send()
        local.wait()

    return pl.pallas_call(
        body, name=name,
        out_shape=jax.ShapeDtypeStruct((n,) + blk, src.dtype),
        in_specs=[HBM_SPEC], out_specs=HBM_SPEC,
        scratch_shapes=[pltpu.SemaphoreType.DMA((n,)), pltpu.SemaphoreType.DMA((n,)),
                        pltpu.SemaphoreType.DMA],
    )(src)


def _all_gather(piece, name):
    return _a2a(_a2a(piece, group="xy", bcast=True, name=name + "_xy"),
                group="c", bcast=True, name=name + "_c")


D2D_CHUNKS = 16
ICI_CHUNKS = 8


def _row_chunks(rows, dtype, k):
    unit = SUBLANES * (4 // jnp.dtype(dtype).itemsize)
    assert rows % unit == 0
    units = rows // unit
    k = max(1, min(k, units))
    base, rem = divmod(units, k)
    out, r = [], 0
    for i in range(k):
        n = (base + (1 if i < rem else 0)) * unit
        out.append((r, n))
        r += n
    return out


def _chunks(shape, dtype, k):
    if len(shape) == 2:
        return [(pl.ds(r0, n),) for r0, n in _row_chunks(shape[0], dtype, k)]
    per = max(1, k // shape[0])
    return [(l, pl.ds(r0, n)) for l in range(shape[0]) for r0, n in _row_chunks(shape[1], dtype, per)]


def _mesh_place():
    x, y, c = lax.axis_index("x"), lax.axis_index("y"), lax.axis_index("c")
    return x, y, c, 2 * x + y


def _chip_peer(x, y, c, d):
    px, py = x ^ (d >> 1), y ^ (d & 1)
    return (px, py, c), 2 * px + py


def _remote(src, dst, send_sem, recv_sem, dev):
    return pltpu.make_async_remote_copy(src_ref=src, dst_ref=dst, send_sem=send_sem,
                                        recv_sem=recv_sem, device_id=dev, device_id_type=MESH)


def _comm_call(body, name, ins, out_shapes, n_sems, aliases=None):
    n = len(ins)
    return pl.pallas_call(
        body, name=name,
        out_shape=out_shapes, in_specs=[HBM_SPEC] * n, out_specs=[HBM_SPEC] * n,
        input_output_aliases=aliases or {},
        scratch_shapes=[pltpu.SemaphoreType.DMA((n_sems, n)), pltpu.SemaphoreType.DMA((n_sems, n))],
    )(*ins)


class _Exchange:
    def __init__(self, kind, arrays):
        self.kind, self.arrays, self.n = kind, list(arrays), len(arrays)
        if kind == "gather":
            self.chunks = [_chunks(a.shape, a.dtype, ICI_CHUNKS) for a in arrays]
            self.out_shapes = [jax.ShapeDtypeStruct((2, 4) + tuple(a.shape), a.dtype) for a in arrays]
        else:
            lead, k = (1, ICI_CHUNKS) if kind == "scatter" else (2, ICI_CHUNKS // 4)
            self.chunks = [_chunks(a.shape[lead:], a.dtype, k) for a in arrays]
            self.out_shapes = [jax.ShapeDtypeStruct(a.shape, a.dtype) for a in arrays]
        self.peers = [p for p in range(1, 8 if kind == "scatter8" else 4)]
        n_sems = 8 if kind == "scatter8" else 4
        self.sem_shapes = [pltpu.SemaphoreType.DMA((n_sems, self.n)),
                           pltpu.SemaphoreType.DMA((n_sems, self.n))]

    def _peer(self, x, y, c, me, p):
        a, d = p // 4, p % 4
        px, py = x ^ (d >> 1), y ^ (d & 1)
        pc = 1 - c if a else c
        if self.kind == "scatter8":
            return (px, py, pc), (pc, 2 * px + py), (c, me)
        return (px, py, pc), (2 * px + py,), (me,)

    def _blocks(self, srcs, outs, o, c, me, theirs, mine):
        if self.kind == "gather":
            return srcs[o], outs[o].at[(c,) + mine], outs[o].at[(c,) + theirs]
        return srcs[o].at[theirs], outs[o].at[mine], outs[o].at[theirs]

    def start(self, srcs, outs, send_sems, recv_sems):
        x, y, c, me = _mesh_place()
        if self.kind == "gather":
            for o in range(self.n):
                for idx in self.chunks[o]:
                    pltpu.make_async_copy(srcs[o].at[idx], outs[o].at[(c, me) + idx],
                                          send_sems.at[0, o]).start()
        for p in self.peers:
            dev, theirs, mine = self._peer(x, y, c, me, p)
            for o in range(self.n):
                src, dst, _ = self._blocks(srcs, outs, o, c, me, theirs, mine)
                for idx in self.chunks[o]:
                    _remote(src.at[idx], dst.at[idx], send_sems.at[p, o], recv_sems.at[p, o],
                            dev).start()

    def wait(self, srcs, outs, send_sems, recv_sems):
        x, y, c, me = _mesh_place()
        for wait_recv in (True, False):
            for p in self.peers:
                dev, theirs, mine = self._peer(x, y, c, me, p)
                for o in range(self.n):
                    src, _, land = self._blocks(srcs, outs, o, c, me, theirs, mine)
                    cp = _remote(src, land, send_sems.at[p, o], recv_sems.at[p, o], dev)
                    cp.wait_recv() if wait_recv else cp.wait_send()
        if self.kind == "gather":
            for o in range(self.n):
                pltpu.make_async_copy(srcs[o], outs[o].at[c, me], send_sems.at[0, o]).wait()


def _run_exchange(ex, name):
    n = ex.n

    def body(*refs):
        srcs, outs, send_sems, recv_sems = refs[:n], refs[n:2 * n], refs[2 * n], refs[2 * n + 1]
        ex.start(srcs, outs, send_sems, recv_sems)
        ex.wait(srcs, outs, send_sems, recv_sems)

    return _comm_call(body, name, ex.arrays, ex.out_shapes, len(ex.peers) + 1)


def _ag_c(bufs, name):
    n = len(bufs)
    chunks = [_chunks(b.shape[2:], b.dtype, D2D_CHUNKS // 4) for b in bufs]

    def body(*refs):
        srcs, outs, send_sems, recv_sems = refs[:n], refs[n:2 * n], refs[2 * n], refs[2 * n + 1]
        x, y, c, _ = _mesh_place()
        sib = (x, y, 1 - c)
        for o in range(n):
            for k in range(4):
                for idx in chunks[o]:
                    _remote(srcs[o].at[(c, k) + idx], outs[o].at[(c, k) + idx],
                            send_sems.at[0, o], recv_sems.at[0, o], sib).start()
        for o in range(n):
            _remote(srcs[o].at[c], outs[o].at[1 - c], send_sems.at[0, o], recv_sems.at[0, o],
                    sib).wait_recv()
        for o in range(n):
            _remote(srcs[o].at[c], outs[o].at[1 - c], send_sems.at[0, o], recv_sems.at[0, o],
                    sib).wait_send()

    shapes = [jax.ShapeDtypeStruct(b.shape, b.dtype) for b in bufs]
    return _comm_call(body, name, bufs, shapes, 1, aliases={i: i for i in range(n)})


def _rs_c(gs, name):
    n = len(gs)
    chunks = [_chunks(g.shape[2:], g.dtype, max(1, D2D_CHUNKS // g.shape[1])) for g in gs]

    def body(*refs):
        srcs, outs, send_sems, recv_sems = refs[:n], refs[n:2 * n], refs[2 * n], refs[2 * n + 1]
        x, y, c, _ = _mesh_place()
        sib = (x, y, 1 - c)
        for o in range(n):
            for k in range(gs[o].shape[1]):
                for idx in chunks[o]:
                    _remote(srcs[o].at[(1 - c, k) + idx], outs[o].at[(k,) + idx],
                            send_sems.at[0, o], recv_sems.at[0, o], sib).start()
        for o in range(n):
            _remote(srcs[o].at[1 - c], outs[o], send_sems.at[0, o], recv_sems.at[0, o],
                    sib).wait_recv()
        for o in range(n):
            _remote(srcs[o].at[1 - c], outs[o], send_sems.at[0, o], recv_sems.at[0, o],
                    sib).wait_send()

    shapes = [jax.ShapeDtypeStruct(g.shape[1:], g.dtype) for g in gs]
    return _comm_call(body, name, gs, shapes, 1)


def _matmul(a, b, *, trans_b, tm, tn, name, add=None, add_scale=1.0, host=None):
    a_parts = list(a) if isinstance(a, (list, tuple)) else [a]
    M, K = a_parts[0].shape[0], sum(p.shape[1] for p in a_parts)
    N = b.shape[0] if trans_b else b.shape[1]
    tm, tn = min(tm, M), min(tn, N)
    assert M % tm == 0 and N % tn == 0
    dn = (((1,), (1,)), ((), ())) if trans_b else (((1,), (0,)), ((), ()))
    na = len(a_parts)

    def body(*refs):
        a_refs, b_ref, o_ref = refs[:na], refs[na], refs[-1]
        av = [r[...].astype(BF16) for r in a_refs]
        av = av[0] if na == 1 else jnp.concatenate(av, axis=1)
        r = lax.dot_general(av, b_ref[...].astype(BF16), dn, preferred_element_type=F32)
        if add is not None:
            r = r + add_scale * refs[na + 1][...]
        o_ref[...] = r

    b_spec = (pl.BlockSpec((tn, K), lambda j, i: (j, 0)) if trans_b
              else pl.BlockSpec((K, tn), lambda j, i: (0, j)))
    in_specs = [pl.BlockSpec((tm, p.shape[1]), lambda j, i: (i, 0)) for p in a_parts] + [b_spec]
    args = a_parts + [b]
    if add is not None:
        in_specs.append(pl.BlockSpec((tm, tn), lambda j, i: (i, j)))
        args.append(add)
    grid = (N // tn, M // tm)
    x_in, x_out, x_shapes, x_scratch, x_args = _host_specs(host)
    body = _hosted(body, len(args), 1, 0, host, grid)
    outs = pl.pallas_call(
        body, name=name, grid=grid,
        in_specs=in_specs + x_in,
        out_specs=[pl.BlockSpec((tm, tn), lambda j, i: (i, j))] + x_out,
        out_shape=[jax.ShapeDtypeStruct((M, N), F32)] + x_shapes,
        scratch_shapes=x_scratch,
        compiler_params=_cparams(*(("arbitrary",) * 2 if host else ("parallel",) * 2)),
    )(*args, *x_args)
    return outs if host else outs[0]


def _matmul_tn(a, b, *, tm, tn, tk, name):
    T, M = a.shape
    N = b.shape[1]
    tm, tn, tk = min(tm, M), min(tn, N), min(tk, T)
    assert M % tm == 0 and N % tn == 0 and T % tk == 0

    def body(a_ref, b_ref, o_ref):
        @pl.when(pl.program_id(2) == 0)
        def _():
            o_ref[...] = jnp.zeros_like(o_ref)

        o_ref[...] += lax.dot_general(a_ref[...].astype(BF16), b_ref[...].astype(BF16),
                                      (((0,), (0,)), ((), ())), preferred_element_type=F32)

    return pl.pallas_call(
        body, name=name, grid=(M // tm, N // tn, T // tk),
        in_specs=[pl.BlockSpec((tk, tm), lambda i, j, k: (k, i)),
                  pl.BlockSpec((tk, tn), lambda i, j, k: (k, j))],
        out_specs=pl.BlockSpec((tm, tn), lambda i, j, k: (i, j)),
        out_shape=jax.ShapeDtypeStruct((M, N), F32),
        compiler_params=_cparams("parallel", "parallel", "arbitrary"),
    )(a, b)


def _matmul_tn_parts(a, parts, *, tm, tk, name):
    T, M = a.shape
    tm, tk = min(tm, M), min(tk, T)
    assert M % tm == 0 and T % tk == 0
    n = len(parts)

    def body(*refs):
        a_ref, b_refs, o_refs = refs[0], refs[1:1 + n], refs[1 + n:]
        av = a_ref[...].astype(BF16)
        for b_ref, o_ref in zip(b_refs, o_refs):
            @pl.when(pl.program_id(1) == 0)
            def _(o_ref=o_ref):
                o_ref[...] = jnp.zeros_like(o_ref)

            o_ref[...] += lax.dot_general(av, b_ref[...].astype(BF16), (((0,), (0,)), ((), ())),
                                          preferred_element_type=F32)

    return pl.pallas_call(
        body, name=name, grid=(M // tm, T // tk),
        in_specs=[pl.BlockSpec((tk, tm), lambda i, k: (k, i))]
        + [pl.BlockSpec((tk, p.shape[1]), lambda i, k: (k, 0)) for p in parts],
        out_specs=[pl.BlockSpec((tm, p.shape[1]), lambda i, k: (i, 0)) for p in parts],
        out_shape=[jax.ShapeDtypeStruct((M, p.shape[1]), F32) for p in parts],
        compiler_params=_cparams("parallel", "arbitrary"),
    )(a, *parts)


def _head_masks(rows):
    lane = lax.broadcasted_iota(jnp.int32, (rows, LANES), 1)
    return lane < HEAD_DIM, lane >= HEAD_DIM


def _causal(i_q, i_k, tq, tk):
    row = i_q * tq + lax.broadcasted_iota(jnp.int32, (tq, tk), 0)
    col = i_k * tk + lax.broadcasted_iota(jnp.int32, (tq, tk), 1)
    return row >= col


def _hosted(body, n_in, n_out, n_scratch, host, grid):
    if host is None:
        return body
    nx = host.n

    def wrapped(*refs):
        ins, xsrcs = refs[:n_in], refs[n_in:n_in + nx]
        outs = refs[n_in + nx:n_in + nx + n_out]
        xouts = refs[n_in + nx + n_out:n_in + 2 * nx + n_out]
        scratch = refs[n_in + 2 * nx + n_out:n_in + 2 * nx + n_out + n_scratch]
        xsems = refs[n_in + 2 * nx + n_out + n_scratch:]
        step = pl.program_id(0) * grid[1] + pl.program_id(1)

        @pl.when(step == 0)
        def _():
            host.start(xsrcs, xouts, *xsems)

        body(*ins, *outs, *scratch)

        @pl.when(step == grid[0] * grid[1] - 1)
        def _():
            host.wait(xsrcs, xouts, *xsems)

    return wrapped


def _host_specs(host):
    if host is None:
        return [], [], [], [], []
    return ([HBM_SPEC] * host.n, [HBM_SPEC] * host.n, host.out_shapes, host.sem_shapes, host.arrays)


def _flash_fwd(proj, cum4, *, tb, name, host=None):
    T = proj.shape[0]
    D = N_HEADS * HEAD_DIM
    nb = T // tb
    cb = D // LANES
    x_in, x_out, x_shapes, x_scratch, x_args = _host_specs(host)

    def body(q_ref, k_ref, v_ref, g_ref, cum_ref, o_ref, og_ref, lp_ref, kb_ref, vb_ref):
        i = pl.program_id(1)

        @pl.when(i == 0)
        def _():
            kb_ref[...] = k_ref[...].astype(BF16)
            vb_ref[...] = v_ref[...].astype(BF16)

        q = q_ref[...] * (HEAD_DIM ** -0.5)
        masks = _head_masks(tb)
        qh = [jnp.where(masks[h], q, 0.0).astype(BF16) for h in range(2)]
        cref = [cum_ref[0, h, pl.ds(i, 1), :][:, 0:1] for h in range(2)]

        def step(kbi, carry, masked):
            k0 = pl.multiple_of(kbi * tb, tb)
            kblk = kb_ref[pl.ds(k0, tb), :]
            vblk = vb_ref[pl.ds(k0, tb), :]
            new = []
            for h in range(2):
                m, l, acc = carry[h]
                s = lax.dot_general(qh[h], kblk, (((1,), (1,)), ((), ())),
                                    preferred_element_type=F32)
                s = s + (cref[h] - cum_ref[0, h, pl.ds(kbi, 1), :])
                if masked:
                    s = jnp.where(_causal(i, kbi, tb, tb), s, -jnp.inf)
                m_new = jnp.maximum(m, jnp.max(s, axis=-1, keepdims=True))
                alpha = jnp.exp(m - m_new)
                p = jnp.exp(s - m_new)
                l = alpha * l + jnp.sum(p, axis=-1, keepdims=True)
                acc = alpha * acc + jnp.dot(p.astype(BF16), vblk, preferred_element_type=F32)
                new.append((m_new, l, acc))
            return tuple(new)

        init1 = (jnp.full((tb, 1), -jnp.inf, F32), jnp.zeros((tb, 1), F32),
                 jnp.zeros((tb, LANES), F32))
        carry = lax.fori_loop(0, i, lambda kbi, c: step(kbi, c, False), (init1, init1))
        outs = []
        for h, (m, l, acc) in enumerate(step(i, carry, True)):
            outs.append(acc / l)
            lp_ref[h] = jnp.broadcast_to(m + jnp.log(l) - cref[h], (tb, LANES))
        o = jnp.where(masks[0], outs[0], outs[1])
        o_ref[...] = o
        gate = g_ref[...]
        og_ref[...] = (o * (gate * _sigmoid(gate))).astype(BF16)

    body = _hosted(body, 5, 3, 2, host, (N_PAIRS, nb))
    return pl.pallas_call(
        body, name=name, grid=(N_PAIRS, nb),
        in_specs=[pl.BlockSpec((tb, LANES), lambda j, i: (i, j)),
                  pl.BlockSpec((T, LANES), lambda j, i: (0, cb + j)),
                  pl.BlockSpec((T, LANES), lambda j, i: (0, 2 * cb + j)),
                  pl.BlockSpec((tb, LANES), lambda j, i: (i, 3 * cb + j)),
                  pl.BlockSpec((1, 2, nb, tb), lambda j, i: (j, 0, 0, 0))] + x_in,
        out_specs=[pl.BlockSpec((tb, LANES), lambda j, i: (i, j)),
                   pl.BlockSpec((tb, LANES), lambda j, i: (i, j)),
                   pl.BlockSpec((2, tb, LANES), lambda j, i: (j, i, 0))] + x_out,
        out_shape=[jax.ShapeDtypeStruct((T, D), F32), jax.ShapeDtypeStruct((T, D), BF16),
                   jax.ShapeDtypeStruct((N_HEADS, T, LANES), F32)] + x_shapes,
        scratch_shapes=[pltpu.VMEM((T, LANES), BF16), pltpu.VMEM((T, LANES), BF16)] + x_scratch,
        compiler_params=_cparams("arbitrary", "arbitrary"),
    )(proj, proj, proj, proj, cum4, *x_args)


def _flash_bwd(proj, cum4, o, dog, lp, *, tb, name, host=None):
    T = proj.shape[0]
    D = N_HEADS * HEAD_DIM
    nb = T // tb
    cb = D // LANES
    x_in, x_out, x_shapes, x_scratch, x_args = _host_specs(host)

    def body(q_ref, k_ref, v_ref, g_ref, cum_ref, o_ref, dog_ref, lp_ref,
             dq_ref, dg_ref, dk_ref, dv_ref, dcq_ref, dck_ref,
             kb_ref, vb_ref, dka_ref, dva_ref, dca_ref):
        i = pl.program_id(1)

        @pl.when(i == 0)
        def _():
            kb_ref[...] = k_ref[...].astype(BF16)
            vb_ref[...] = v_ref[...].astype(BF16)
            dka_ref[...] = jnp.zeros_like(dka_ref)
            dva_ref[...] = jnp.zeros_like(dva_ref)
            dca_ref[...] = jnp.zeros_like(dca_ref)

        gate = g_ref[...]
        sg = _sigmoid(gate)
        o = o_ref[...]
        dog = dog_ref[...]
        do = dog * (gate * sg)
        dg_ref[...] = (dog * o * (sg * (1.0 + gate * (1.0 - sg)))).astype(BF16)
        q = q_ref[...] * (HEAD_DIM ** -0.5)
        masks = _head_masks(tb)
        qh = [jnp.where(masks[h], q, 0.0).astype(BF16) for h in range(2)]
        doh = [jnp.where(masks[h], do, 0.0).astype(BF16) for h in range(2)]
        delta = [jnp.sum(jnp.where(masks[h], do * o, 0.0), axis=-1, keepdims=True) for h in range(2)]
        lph = [lp_ref[h][:, 0:1] for h in range(2)]

        def step(kbi, carry, masked):
            k0 = pl.multiple_of(kbi * tb, tb)
            kblk = kb_ref[pl.ds(k0, tb), :]
            vblk = vb_ref[pl.ds(k0, tb), :]
            new, dk, dv = [], None, None
            for h in range(2):
                acc, rs = carry[h]
                s = lax.dot_general(qh[h], kblk, (((1,), (1,)), ((), ())), preferred_element_type=F32)
                p = jnp.exp(s - cum_ref[0, h, pl.ds(kbi, 1), :] - lph[h])
                if masked:
                    p = jnp.where(_causal(i, kbi, tb, tb), p, 0.0)
                dp = lax.dot_general(doh[h], vblk, (((1,), (1,)), ((), ())),
                                     preferred_element_type=F32)
                ds = p * (dp - delta[h])
                pb, dsb = p.astype(BF16), ds.astype(BF16)
                dv_h = lax.dot_general(pb, doh[h], (((0,), (0,)), ((), ())),
                                       preferred_element_type=F32)
                dk_h = lax.dot_general(dsb, qh[h], (((0,), (0,)), ((), ())),
                                       preferred_element_type=F32)
                dv = dv_h if dv is None else dv + dv_h
                dk = dk_h if dk is None else dk + dk_h
                dca_ref[h, pl.ds(kbi, 1), :] -= jnp.sum(ds, axis=0, keepdims=True)
                new.append((acc + jnp.dot(dsb, kblk, preferred_element_type=F32),
                            rs + jnp.sum(ds, axis=-1, keepdims=True)))
            dka_ref[pl.ds(k0, tb), :] += dk
            dva_ref[pl.ds(k0, tb), :] += dv
            return tuple(new)

        init1 = (jnp.zeros((tb, LANES), F32), jnp.zeros((tb, 1), F32))
        carry = lax.fori_loop(0, i, lambda kbi, c: step(kbi, c, False), (init1, init1))
        dqs = []
        for h, (acc, rs) in enumerate(step(i, carry, True)):
            dqs.append(acc)
            dcq_ref[0, 0, pl.ds(h, 1), :] = jnp.broadcast_to(rs, (tb, LANES)).T[0:1, :]
        dq_ref[...] = (jnp.where(masks[0], dqs[0], dqs[1]) * (HEAD_DIM ** -0.5)).astype(BF16)

        @pl.when(i == nb - 1)
        def _():
            dk_ref[...] = dka_ref[...].astype(BF16)
            dv_ref[...] = dva_ref[...].astype(BF16)
            dck_ref[0] = dca_ref[...]

    blk = pl.BlockSpec((tb, LANES), lambda j, i: (i, j))
    full = pl.BlockSpec((T, LANES), lambda j, i: (0, j))
    body = _hosted(body, 8, 6, 5, host, (N_PAIRS, nb))
    return pl.pallas_call(
        body, name=name, grid=(N_PAIRS, nb),
        in_specs=[blk,
                  pl.BlockSpec((T, LANES), lambda j, i: (0, cb + j)),
                  pl.BlockSpec((T, LANES), lambda j, i: (0, 2 * cb + j)),
                  pl.BlockSpec((tb, LANES), lambda j, i: (i, 3 * cb + j)),
                  pl.BlockSpec((1, 2, nb, tb), lambda j, i: (j, 0, 0, 0)),
                  blk, blk, pl.BlockSpec((2, tb, LANES), lambda j, i: (j, i, 0))] + x_in,
        out_specs=[blk, blk, full, full,
                   pl.BlockSpec((1, 1, 2, tb), lambda j, i: (j, i, 0, 0)),
                   pl.BlockSpec((1, 2, nb, tb), lambda j, i: (j, 0, 0, 0))] + x_out,
        out_shape=[jax.ShapeDtypeStruct((T, D), BF16)] * 4
        + [jax.ShapeDtypeStruct((N_PAIRS, nb, 2, tb), F32),
           jax.ShapeDtypeStruct((N_PAIRS, 2, nb, tb), F32)] + x_shapes,
        scratch_shapes=[pltpu.VMEM((T, LANES), BF16), pltpu.VMEM((T, LANES), BF16),
                        pltpu.VMEM((T, LANES), F32), pltpu.VMEM((T, LANES), F32),
                        pltpu.VMEM((2, nb, tb), F32)] + x_scratch,
        compiler_params=_cparams("arbitrary", "arbitrary"),
    )(proj, proj, proj, proj, cum4, o, dog, lp, *x_args)


def _cumsum_fwd(proj, bf_row, *, tt, name):
    T = proj.shape[0]
    cb = (proj.shape[1] - LANES) // LANES

    def body(f_ref, b_ref, out_ref, carry_ref):
        i = pl.program_id(0)

        @pl.when(i == 0)
        def _():
            carry_ref[...] = jnp.zeros_like(carry_ref)

        ls = -_softplus(-(f_ref[...] + b_ref[...]))
        tri = (lax.broadcasted_iota(jnp.int32, (tt, tt), 0)
               >= lax.broadcasted_iota(jnp.int32, (tt, tt), 1)).astype(F32)
        cum = jnp.dot(tri, ls, preferred_element_type=F32,
                      precision=lax.Precision.HIGHEST) + carry_ref[...]
        carry_ref[...] = cum[tt - 1:tt, :]
        out_ref[...] = cum.T

    return pl.pallas_call(
        body, name=name, grid=(T // tt,),
        in_specs=[pl.BlockSpec((tt, LANES), lambda i: (i, cb)),
                  pl.BlockSpec((1, LANES), lambda i: (0, 0))],
        out_specs=pl.BlockSpec((LANES, tt), lambda i: (0, i)),
        out_shape=jax.ShapeDtypeStruct((LANES, T), F32),
        scratch_shapes=[pltpu.VMEM((1, LANES), F32)],
        compiler_params=_cparams("arbitrary"),
    )(proj, bf_row)


def _cumsum_bwd(dcum_t, proj, bf_row, *, tt, name):
    T = proj.shape[0]
    cb = (proj.shape[1] - LANES) // LANES
    nt = T // tt

    def body(dc_ref, f_ref, b_ref, df_ref, db_ref, carry_ref):
        i = pl.program_id(0)

        @pl.when(i == 0)
        def _():
            carry_ref[...] = jnp.zeros_like(carry_ref)
            db_ref[...] = jnp.zeros_like(db_ref)

        dc = dc_ref[...].T
        tri = (lax.broadcasted_iota(jnp.int32, (tt, tt), 0)
               <= lax.broadcasted_iota(jnp.int32, (tt, tt), 1)).astype(F32)
        rev = jnp.dot(tri, dc, preferred_element_type=F32,
                      precision=lax.Precision.HIGHEST) + carry_ref[...]
        carry_ref[...] = rev[0:1, :]
        df = rev * _sigmoid(-(f_ref[...] + b_ref[...]))
        df_ref[...] = df.astype(BF16)
        db_ref[...] += jnp.sum(df, axis=0, keepdims=True)

    return pl.pallas_call(
        body, name=name, grid=(nt,),
        in_specs=[pl.BlockSpec((LANES, tt), lambda i: (0, nt - 1 - i)),
                  pl.BlockSpec((tt, LANES), lambda i: (nt - 1 - i, cb)),
                  pl.BlockSpec((1, LANES), lambda i: (0, 0))],
        out_specs=[pl.BlockSpec((tt, LANES), lambda i: (nt - 1 - i, 0)),
                   pl.BlockSpec((1, LANES), lambda i: (0, 0))],
        out_shape=[jax.ShapeDtypeStruct((T, LANES), BF16), jax.ShapeDtypeStruct((1, LANES), F32)],
        scratch_shapes=[pltpu.VMEM((1, LANES), F32)],
        compiler_params=_cparams("arbitrary"),
    )(dcum_t, proj, bf_row)


def _rows_down(x, before, sh):
    if sh == 0:
        return x
    rolled = pltpu.roll(x, sh, axis=0)
    row = lax.broadcasted_iota(jnp.int32, (SUBLANES, x.shape[1]), 0)
    head = jnp.where(row < sh, pltpu.roll(before, sh, axis=0), rolled[:SUBLANES])
    return jnp.concatenate([head, rolled[SUBLANES:]], axis=0)


def _rows_up(x, after, sh):
    if sh == 0:
        return x
    tt = x.shape[0]
    rolled = pltpu.roll(x, tt - sh, axis=0)
    row = lax.broadcasted_iota(jnp.int32, (SUBLANES, x.shape[1]), 0)
    tail = jnp.where(row >= SUBLANES - sh, pltpu.roll(after, SUBLANES - sh, axis=0),
                     rolled[tt - SUBLANES:])
    return jnp.concatenate([rolled[:tt - SUBLANES], tail], axis=0)


def _rg_gates(u0, before, small_ref, wa_ref, wi_ref):
    taps = [_rows_down(u0, before, CONV_WIDTH - 1 - tap) for tap in range(CONV_WIDTH)]
    u = small_ref[4:5, :]
    for tap in range(CONV_WIDTH):
        u = u + taps[tap] * small_ref[tap:tap + 1, :]
    pa, pi = [], []
    for n in range(RNN_BLOCKS):
        ub = u[:, n * RNN_BLOCK_WIDTH:(n + 1) * RNN_BLOCK_WIDTH].astype(BF16)
        pa.append(jnp.dot(ub, wa_ref[n], preferred_element_type=F32))
        pi.append(jnp.dot(ub, wi_ref[n], preferred_element_type=F32))
    r = _sigmoid(jnp.concatenate(pa, axis=-1) + small_ref[5:6, :])
    ig = _sigmoid(jnp.concatenate(pi, axis=-1) + small_ref[6:7, :])
    spl = _softplus(-small_ref[7:8, :])
    log_a = (-LRU_C) * r * spl
    a = jnp.exp(log_a)
    s2 = jnp.tanh(-log_a) * (a * a + 1.0)
    inv_s = lax.rsqrt(s2)
    s = jnp.where(s2 > 0.0, s2 * inv_s, 0.0)
    return u, taps, r, ig, spl, a, s, inv_s


def _rg_fwd(proj, small, wa, wi, *, tt, name):
    T = proj.shape[0]
    D = RNN_BLOCKS * RNN_BLOCK_WIDTH
    hb = tt // SUBLANES

    def body(u0_ref, halo_ref, g_ref, small_ref, wa_ref, wi_ref, h_ref, y_ref,
             a_ref, b_ref, carry_ref):
        i = pl.program_id(0)

        @pl.when(i == 0)
        def _():
            carry_ref[...] = jnp.zeros_like(carry_ref)

        before = jnp.where(i == 0, 0.0, halo_ref[...])
        u, _, r, ig, spl, a, s, _ = _rg_gates(u0_ref[...], before, small_ref, wa_ref, wi_ref)
        a_ref[...] = a
        b_ref[...] = s * (ig * u)

        def row(t, h):
            h = a_ref[pl.ds(t, 1), :] * h + b_ref[pl.ds(t, 1), :]
            h_ref[pl.ds(t, 1), :] = h
            return h

        carry_ref[...] = lax.fori_loop(0, tt, row, carry_ref[...])
        gate = g_ref[...]
        y_ref[...] = (h_ref[...] * (gate * _sigmoid(gate))).astype(BF16)

    return pl.pallas_call(
        body, name=name, grid=(T // tt,),
        in_specs=[pl.BlockSpec((tt, D), lambda i: (i, 0)),
                  pl.BlockSpec((SUBLANES, D), lambda i: (jnp.maximum(i * hb - 1, 0), 0)),
                  pl.BlockSpec((tt, D), lambda i: (i, 1)),
                  pl.BlockSpec((SUBLANES, D), lambda i: (0, 0)),
                  pl.BlockSpec((RNN_BLOCKS, RNN_BLOCK_WIDTH, RNN_BLOCK_WIDTH), lambda i: (0, 0, 0)),
                  pl.BlockSpec((RNN_BLOCKS, RNN_BLOCK_WIDTH, RNN_BLOCK_WIDTH), lambda i: (0, 0, 0))],
        out_specs=[pl.BlockSpec((tt, D), lambda i: (i, 0)), pl.BlockSpec((tt, D), lambda i: (i, 0))],
        out_shape=[jax.ShapeDtypeStruct((T, D), F32), jax.ShapeDtypeStruct((T, D), BF16)],
        scratch_shapes=[pltpu.VMEM((tt, D), F32), pltpu.VMEM((tt, D), F32),
                        pltpu.VMEM((1, D), F32)],
        compiler_params=_cparams("arbitrary"),
    )(proj, proj, proj, small, wa, wi)


def _rg_bwd(proj, hs, dy, small, wa, wi, *, tt, name):
    T = proj.shape[0]
    D = RNN_BLOCKS * RNN_BLOCK_WIDTH
    W = RNN_BLOCK_WIDTH
    hb = tt // SUBLANES
    nt = T // tt

    def body(u0_ref, uhalo_ref, g_ref, h_ref, hhalo_ref, dy_ref, small_ref, wa_ref, wi_ref,
             dp_ref, dwa_ref, dwi_ref, ds_ref,
             a_ref, g_s_ref, dunext_ref, carry_ref):
        i = pl.program_id(0)
        first_chunk = i == nt - 1

        @pl.when(i == 0)
        def _():
            carry_ref[...] = jnp.zeros_like(carry_ref)
            dunext_ref[...] = jnp.zeros_like(dunext_ref)
            dwa_ref[...] = jnp.zeros_like(dwa_ref)
            dwi_ref[...] = jnp.zeros_like(dwi_ref)
            ds_ref[...] = jnp.zeros_like(ds_ref)

        u_before = jnp.where(first_chunk, 0.0, uhalo_ref[...])
        h_before = jnp.where(first_chunk, 0.0, hhalo_ref[...])
        u, taps, r, ig, spl, a, s, inv_s = _rg_gates(u0_ref[...], u_before, small_ref, wa_ref,
                                                     wi_ref)
        gate = g_ref[...]
        sg = _sigmoid(gate)
        dy = dy_ref[...]
        dp_ref[:, D:] = (dy * h_ref[...] * (sg * (1.0 + gate * (1.0 - sg)))).astype(BF16)
        a_ref[...] = a
        g_s_ref[...] = dy * (gate * sg)

        def row(k, c):
            t = tt - 1 - k
            g = g_s_ref[pl.ds(t, 1), :] + c
            g_s_ref[pl.ds(t, 1), :] = g
            return a_ref[pl.ds(t, 1), :] * g

        carry_ref[...] = lax.fori_loop(0, tt, row, carry_ref[...])
        g = g_s_ref[...]
        h_prev = _rows_down(h_ref[...], h_before, 1)
        iu = ig * u
        d_iu = g * s
        dlog_a = (g * h_prev) * a - (g * iu) * (a * a) * inv_s
        dpre_a = (dlog_a * ((-LRU_C) * spl)) * r * (1.0 - r)
        dpre_i = (d_iu * u) * ig * (1.0 - ig)
        dlam = jnp.sum(dlog_a * r, axis=0, keepdims=True) * (LRU_C * _sigmoid(-small_ref[7:8, :]))
        du_parts = []
        for n in range(RNN_BLOCKS):
            sl = slice(n * W, (n + 1) * W)
            ub = u[:, sl].astype(BF16)
            da_n = dpre_a[:, sl].astype(BF16)
            di_n = dpre_i[:, sl].astype(BF16)
            dwa_ref[n] += lax.dot_general(ub, da_n, (((0,), (0,)), ((), ())),
                                          preferred_element_type=F32)
            dwi_ref[n] += lax.dot_general(ub, di_n, (((0,), (0,)), ((), ())),
                                          preferred_element_type=F32)
            du_parts.append(
                lax.dot_general(da_n, wa_ref[n], (((1,), (1,)), ((), ())), preferred_element_type=F32)
                + lax.dot_general(di_n, wi_ref[n], (((1,), (1,)), ((), ())), preferred_element_type=F32))
        du = d_iu * ig + jnp.concatenate(du_parts, axis=-1)
        for tap in range(CONV_WIDTH):
            ds_ref[tap:tap + 1, :] += jnp.sum(du * taps[tap], axis=0, keepdims=True)
        ds_ref[4:5, :] += jnp.sum(du, axis=0, keepdims=True)
        ds_ref[5:6, :] += jnp.sum(dpre_a, axis=0, keepdims=True)
        ds_ref[6:7, :] += jnp.sum(dpre_i, axis=0, keepdims=True)
        ds_ref[7:8, :] += dlam
        du_after = dunext_ref[...]
        du0 = jnp.zeros((tt, D), F32)
        for tap in range(CONV_WIDTH):
            du0 = du0 + _rows_up(du, du_after, CONV_WIDTH - 1 - tap) * small_ref[tap:tap + 1, :]
        dp_ref[:, :D] = du0.astype(BF16)
        dunext_ref[...] = du[0:SUBLANES, :]

    rev = lambda i: nt - 1 - i
    wspec = pl.BlockSpec((RNN_BLOCKS, W, W), lambda i: (0, 0, 0))
    return pl.pallas_call(
        body, name=name, grid=(nt,),
        in_specs=[pl.BlockSpec((tt, D), lambda i: (rev(i), 0)),
                  pl.BlockSpec((SUBLANES, D), lambda i: (jnp.maximum(rev(i) * hb - 1, 0), 0)),
                  pl.BlockSpec((tt, D), lambda i: (rev(i), 1)),
                  pl.BlockSpec((tt, D), lambda i: (rev(i), 0)),
                  pl.BlockSpec((SUBLANES, D), lambda i: (jnp.maximum(rev(i) * hb - 1, 0), 0)),
                  pl.BlockSpec((tt, D), lambda i: (rev(i), 0)),
                  pl.BlockSpec((SUBLANES, D), lambda i: (0, 0)),
                  wspec, wspec],
        out_specs=[pl.BlockSpec((tt, 2 * D), lambda i: (rev(i), 0)),
                   wspec, wspec, pl.BlockSpec((SUBLANES, D), lambda i: (0, 0))],
        out_shape=[jax.ShapeDtypeStruct((T, 2 * D), BF16),
                   jax.ShapeDtypeStruct((RNN_BLOCKS, W, W), F32),
                   jax.ShapeDtypeStruct((RNN_BLOCKS, W, W), F32),
                   jax.ShapeDtypeStruct((SUBLANES, D), F32)],
        scratch_shapes=[pltpu.VMEM((tt, D), F32), pltpu.VMEM((tt, D), F32),
                        pltpu.VMEM((SUBLANES, D), F32), pltpu.VMEM((1, D), F32)],
        compiler_params=_cparams("arbitrary"),
    )(proj, proj, proj, hs, hs, dy, small, wa, wi)


def _out_ln(a, w, x, g, b, *, tt, name):
    T, D = x.shape
    K = a.shape[1]

    def body(a_ref, w_ref, x_ref, g_ref, b_ref, y_ref, yb_ref, zh_ref, rs_ref):
        h = jnp.dot(a_ref[...].astype(BF16), w_ref[...].astype(BF16), preferred_element_type=F32)
        z = ALPHA * x_ref[...] + h
        mu = jnp.mean(z, axis=-1, keepdims=True)
        zc = z - mu
        rstd = lax.rsqrt(jnp.mean(zc * zc, axis=-1, keepdims=True) + LN_EPS)
        zh = zc * rstd
        zh_ref[...] = zh
        rs_ref[...] = rstd
        y = zh * g_ref[...] + b_ref[...]
        y_ref[...] = y
        yb_ref[...] = y.astype(BF16)

    blk = pl.BlockSpec((tt, D), lambda i: (i, 0))
    row = pl.BlockSpec((1, D), lambda i: (0, 0))
    return pl.pallas_call(
        body, name=name, grid=(T // tt,),
        in_specs=[pl.BlockSpec((tt, K), lambda i: (i, 0)), pl.BlockSpec((K, D), lambda i: (0, 0)),
                  blk, row, row],
        out_specs=[blk, blk, blk, pl.BlockSpec((tt, 1), lambda i: (i, 0))],
        out_shape=[jax.ShapeDtypeStruct((T, D), F32), jax.ShapeDtypeStruct((T, D), BF16),
                   jax.ShapeDtypeStruct((T, D), F32), jax.ShapeDtypeStruct((T, 1), F32)],
        compiler_params=_cparams("parallel"),
    )(a, w, x, g, b)


def _ln_bwd_tile(dy, zh_ref, rs_ref, g_ref, dz_ref, dzb_ref, dg_ref, db_ref, first):
    @pl.when(first)
    def _():
        dg_ref[...] = jnp.zeros_like(dg_ref)
        db_ref[...] = jnp.zeros_like(db_ref)

    zh = zh_ref[...]
    dg_ref[...] += jnp.sum(dy * zh, axis=0, keepdims=True)
    db_ref[...] += jnp.sum(dy, axis=0, keepdims=True)
    dzh = dy * g_ref[...]
    m1 = jnp.mean(dzh, axis=-1, keepdims=True)
    m2 = jnp.mean(dzh * zh, axis=-1, keepdims=True)
    dz = rs_ref[...] * (dzh - m1 - zh * m2)
    dz_ref[...] = dz
    dzb_ref[...] = dz.astype(BF16)


def _ln_bwd_specs(T, D, tt):
    blk = pl.BlockSpec((tt, D), lambda i: (i, 0))
    row = pl.BlockSpec((1, D), lambda i: (0, 0))
    return ([blk, pl.BlockSpec((tt, 1), lambda i: (i, 0)), row], [blk, blk, row, row],
            [jax.ShapeDtypeStruct((T, D), F32), jax.ShapeDtypeStruct((T, D), BF16),
             jax.ShapeDtypeStruct((1, D), F32), jax.ShapeDtypeStruct((1, D), F32)])


def _loss_ln_bwd(y, tgt, zh, rstd, g, *, tt, name):
    T, D = y.shape
    ln_in, ln_out, ln_shapes = _ln_bwd_specs(T, D, tt)

    def body(y_ref, t_ref, zh_ref, rs_ref, g_ref, l_ref, dz_ref, dzb_ref, dg_ref, db_ref):
        first = pl.program_id(0) == 0

        @pl.when(first)
        def _():
            l_ref[...] = jnp.zeros_like(l_ref)

        e = y_ref[...] - t_ref[...]
        l_ref[...] += jnp.sum(e * e, axis=0, keepdims=True) * (0.5 / D)
        _ln_bwd_tile(e * (1.0 / D), zh_ref, rs_ref, g_ref, dz_ref, dzb_ref, dg_ref, db_ref, first)

    blk = pl.BlockSpec((tt, D), lambda i: (i, 0))
    return pl.pallas_call(
        body, name=name, grid=(T // tt,),
        in_specs=[blk, blk] + ln_in,
        out_specs=[pl.BlockSpec((1, D), lambda i: (0, 0))] + ln_out,
        out_shape=[jax.ShapeDtypeStruct((1, D), F32)] + ln_shapes,
        compiler_params=_cparams("arbitrary"),
    )(y, tgt, zh, rstd, g)


def _dx_ln_bwd(a, b, add, zh, rstd, g, *, tm, name):
    T, D = add.shape
    na = len(a)
    K = sum(p.shape[1] for p in a)
    ln_in, ln_out, ln_shapes = _ln_bwd_specs(T, D, tm)

    def body(*refs):
        a_refs, b_ref, add_ref = refs[:na], refs[na], refs[na + 1]
        av = [r[...].astype(BF16) for r in a_refs]
        av = av[0] if na == 1 else jnp.concatenate(av, axis=1)
        dy = lax.dot_general(av, b_ref[...].astype(BF16), (((1,), (1,)), ((), ())),
                             preferred_element_type=F32) + ALPHA * add_ref[...]
        _ln_bwd_tile(dy, *refs[na + 2:], pl.program_id(0) == 0)

    return pl.pallas_call(
        body, name=name, grid=(T // tm,),
        in_specs=[pl.BlockSpec((tm, p.shape[1]), lambda i: (i, 0)) for p in a]
        + [pl.BlockSpec((D, K), lambda i: (0, 0)), pl.BlockSpec((tm, D), lambda i: (i, 0))] + ln_in,
        out_specs=ln_out, out_shape=ln_shapes,
        compiler_params=_cparams("arbitrary"),
    )(*a, b, add, zh, rstd, g)


def _row_tile(rows, target):
    best = SUBLANES
    for t in range(SUBLANES, target + 1, SUBLANES):
        if rows % t == 0:
            best = t
    return best


def _add_own(g, recv, c_idx, *, tr, name):
    _, M, R, C = g.shape

    def body(c_ref, g_ref, r_ref, o_ref, ob_ref):
        s = g_ref[0] + r_ref[...]
        o_ref[...] = s
        ob_ref[...] = s.astype(BF16)

    blk = pl.BlockSpec((1, tr, C), lambda k, i, c: (k, i, 0))
    return pl.pallas_call(
        body, name=name,
        grid_spec=pltpu.PrefetchScalarGridSpec(
            num_scalar_prefetch=1, grid=(M, R // tr),
            in_specs=[pl.BlockSpec((1, 1, tr, C), lambda k, i, c: (c[0], k, i, 0)), blk],
            out_specs=[blk, blk]),
        out_shape=[jax.ShapeDtypeStruct((M, R, C), F32), jax.ShapeDtypeStruct((M, R, C), BF16)],
        compiler_params=_cparams("parallel", "parallel"),
    )(c_idx, g, recv)


def _adamw_math(g, w_ref, m_ref, v_ref, g_ref, d_ref, nm_ref, nv_ref):
    nm = ADAM_B1 * m_ref[...] + (1.0 - ADAM_B1) * g
    nv = ADAM_B2 * v_ref[...] + (1.0 - ADAM_B2) * (g * g)
    m_hat = nm / (1.0 - ADAM_B1 ** ADAM_STEP)
    v_hat = nv / (1.0 - ADAM_B2 ** ADAM_STEP)
    g_ref[...] = g
    nm_ref[...] = nm
    nv_ref[...] = nv
    d_ref[...] = (-ADAM_LR) * (m_hat / (jnp.sqrt(v_hat) + ADAM_EPS) + ADAM_WD * w_ref[...])


def _adamw(parts, w, m, v, *, tr, name):
    n, R, C = parts.shape
    tr = min(tr, R)

    def body(p_ref, w_ref, m_ref, v_ref, *out_refs):
        g = p_ref[0]
        for k in range(1, n):
            g = g + p_ref[k]
        _adamw_math(g, w_ref, m_ref, v_ref, *out_refs)

    blk = pl.BlockSpec((tr, C), lambda i: (i, 0))
    out = jax.ShapeDtypeStruct((R, C), F32)
    return pl.pallas_call(
        body, name=name, grid=(R // tr,),
        in_specs=[pl.BlockSpec((n, tr, C), lambda i: (0, i, 0)), blk, blk, blk],
        out_specs=[blk, blk, blk, blk], out_shape=[out, out, out, out],
        compiler_params=_cparams("parallel"),
    )(parts, w, m, v)


def _adamw_shard(parts, place, w, m, v, *, idx, prev, tr, name):
    R, C = parts[0][0].shape[-2:]
    n_parts, n_prev = len(parts), 0 if prev is None else 4

    def body(place_ref, *refs):
        p_refs, (w_ref, m_ref, v_ref) = refs[:n_parts], refs[n_parts:n_parts + 3]
        g = None
        for r in p_refs:
            blk = r[(0,) * (len(r.shape) - 3)].astype(F32)
            g = blk if g is None else g + blk
        _adamw_math(g, w_ref, m_ref, v_ref, *refs[n_parts + 3 + n_prev:])

    blk = pl.BlockSpec((1, tr, C), lambda i, s: (idx, i, 0))

    def part_spec(a, pick):
        return pl.BlockSpec((1,) * (a.ndim - 2) + (tr, C), lambda i, s: (*pick(s), i, 0))

    out = jax.ShapeDtypeStruct(w.shape, F32)
    return pl.pallas_call(
        body, name=name,
        grid_spec=pltpu.PrefetchScalarGridSpec(
            num_scalar_prefetch=1, grid=(R // tr,),
            in_specs=[part_spec(a, pick) for a, pick in parts] + [blk, blk, blk]
            + [pl.BlockSpec(memory_space=pl.ANY)] * n_prev,
            out_specs=[blk, blk, blk, blk]),
        out_shape=[out, out, out, out],
        input_output_aliases={1 + n_parts + 3 + j: j for j in range(n_prev)},
        compiler_params=_cparams("parallel"),
    )(place, *[a for a, _ in parts], w, m, v, *(prev or ()))


def _two_stage_parts(h, recv):
    return [(h, lambda s: (s[0], 0))] + [(recv, lambda s, d=d: (s[0] ^ d, 0)) for d in (1, 2, 3)]


def _direct_parts(g, recv):
    return [(g, lambda s: (s[1], s[0]))] + [
        (recv, lambda s, a=p // 4, d=p % 4: (s[1] ^ a, s[0] ^ d)) for p in range(1, 8)]


SHARD_AXIS = dict(attn_w_in=1, attn_w_out=0, rnn_w_in=1, rnn_w_out=0, rnn_w_a=1, rnn_w_i=1,
                  rnn_conv_w=1, rnn_conv_b=0, rnn_b_a=0, rnn_b_i=0, rnn_lambda=0)
RNN_ROWED = ("rnn_w_out", "rnn_w_a", "rnn_w_i")
SMALL = ("rnn_conv_w", "rnn_conv_b", "rnn_b_a", "rnn_b_i", "rnn_lambda")
PACK_C = 1024


def _elems(shape):
    n = 1
    for s in shape:
        n *= s
    return n


def _pack_rows(p, idx, dtype):
    parts = [p[k][idx].astype(dtype).reshape(-1, PACK_C) for k in RNN_ROWED]
    small = jnp.concatenate([p[k][idx].reshape(-1) for k in SMALL])
    tile_rows = SUBLANES * (4 // jnp.dtype(dtype).itemsize)
    if dtype == BF16:
        small = lax.bitcast_convert_type(small, BF16)
    small = small.reshape(-1, PACK_C)
    parts.append(jnp.pad(small, ((0, tile_rows - small.shape[0]), (0, 0))))
    return jnp.concatenate(parts, axis=0)


def _unpack_rows(flat, shapes):
    out, r = {}, 0
    for k in RNN_ROWED:
        n = _elems(shapes[k]) // PACK_C
        out[k] = flat[r:r + n].reshape(shapes[k])
        r += n
    n_small = sum(_elems(shapes[k]) for k in SMALL)
    small = flat[r:r + n_small // PACK_C].reshape(-1)
    o = 0
    for k in SMALL:
        n = _elems(shapes[k])
        out[k] = small[o:o + n].reshape(shapes[k])
        o += n
    return out


def _join_columns(g, width, *, tr, name):
    _, _, R, S = g.shape

    def body(*refs):
        o_ref = refs[8]
        parts = [refs[r][0, 0].astype(F32) for r in range(8)]
        parts.append(jnp.zeros((tr, width - 8 * S), F32))
        o_ref[...] = jnp.concatenate(parts, axis=-1).astype(o_ref.dtype)

    def shard(r):
        return pl.BlockSpec((1, 1, tr, S), lambda i: (r % 2, r // 2, i, 0))

    return pl.pallas_call(
        body, name=name, grid=(R // tr,),
        in_specs=[shard(r) for r in range(8)],
        out_specs=pl.BlockSpec((tr, width), lambda i: (i, 0)),
        out_shape=jax.ShapeDtypeStruct((R, width), g.dtype),
        compiler_params=_cparams("parallel"),
    )(*([g] * 8))


def _split_columns(parts, S, *, tr, name):
    R = parts[0].shape[0]
    n = len(parts)

    def body(*refs):
        o_ref = refs[n]
        x = jnp.concatenate([r[...] for r in refs[:n]], axis=1)
        for r in range(8):
            o_ref[r % 2, r // 2] = x[:, r * S:(r + 1) * S]

    return pl.pallas_call(
        body, name=name, grid=(R // tr,),
        in_specs=[pl.BlockSpec((tr, p.shape[1]), lambda i: (i, 0)) for p in parts],
        out_specs=pl.BlockSpec((2, 4, tr, S), lambda i: (0, 0, i, 0)),
        out_shape=jax.ShapeDtypeStruct((2, 4, R, S), parts[0].dtype),
        compiler_params=_cparams("parallel"),
    )(*parts)


def _to_full(g, k, sh):
    ax, nd = SHARD_AXIS[k], len(sh)
    perm = tuple(range(2, 2 + ax)) + (1, 0) + tuple(range(2 + ax, 2 + nd))
    return g.transpose(perm).reshape(sh[:ax] + (8 * sh[ax],) + sh[ax + 1:])


def _from_full(full, k, sh):
    ax, nd = SHARD_AXIS[k], len(sh)
    t = full.reshape(sh[:ax] + (4, 2, sh[ax]) + sh[ax + 1:])
    return t.transpose((ax + 1, ax) + tuple(range(ax)) + tuple(range(ax + 2, nd + 2)))


def _unpack_gathered_rows(g, shapes):
    out, r = {}, 0
    for k in RNN_ROWED:
        n = _elems(shapes[k]) // PACK_C
        out[k] = _to_full(g[:, :, r:r + n].reshape((2, 4) + shapes[k]), k, shapes[k])
        r += n
    n_small = sum(_elems(shapes[k]) for k in SMALL)
    nr = 2 * n_small // PACK_C
    small = lax.bitcast_convert_type(g[:, :, r:r + nr].reshape(2, 4, n_small, 2), F32)
    o = 0
    for k in SMALL:
        n = _elems(shapes[k])
        out[k] = _to_full(small[:, :, o:o + n].reshape((2, 4) + shapes[k]), k, shapes[k])
        o += n
    return out


def _pack_grad_rows(full, shapes):
    parts = [_from_full(full[k], k, shapes[k]).reshape(2, 4, -1, PACK_C) for k in RNN_ROWED]
    small = jnp.concatenate(
        [_from_full(full[k], k, shapes[k]).reshape(2, 4, -1) for k in SMALL], axis=-1)
    small = small.reshape(2, 4, -1, PACK_C)
    parts.append(jnp.pad(small, ((0, 0), (0, 0), (0, SUBLANES - small.shape[2]), (0, 0))))
    return jnp.concatenate(parts, axis=2)


def kernel(x, ln_g, ln_b, attn_w_in, attn_b_f, attn_w_out, rnn_w_in, rnn_conv_w, rnn_conv_b, rnn_w_a, rnn_b_a, rnn_w_i, rnn_b_i, rnn_lambda, rnn_w_out, loss_target, m_ln_g, m_ln_b, m_attn_w_in, m_attn_b_f, m_attn_w_out, m_rnn_w_in, m_rnn_conv_w, m_rnn_conv_b, m_rnn_w_a, m_rnn_b_a, m_rnn_w_i, m_rnn_b_i, m_rnn_lambda, m_rnn_w_out, v_ln_g, v_ln_b, v_attn_w_in, v_attn_b_f, v_attn_w_out, v_rnn_w_in, v_rnn_conv_w, v_rnn_conv_b, v_rnn_w_a, v_rnn_b_a, v_rnn_w_i, v_rnn_b_i, v_rnn_lambda, v_rnn_w_out):
    w_loc = dict(attn_w_in=attn_w_in, attn_w_out=attn_w_out, rnn_w_in=rnn_w_in, rnn_w_a=rnn_w_a,
                 rnn_w_i=rnn_w_i, rnn_w_out=rnn_w_out, rnn_conv_w=rnn_conv_w, rnn_conv_b=rnn_conv_b,
                 rnn_b_a=rnn_b_a, rnn_b_i=rnn_b_i, rnn_lambda=rnn_lambda)
    m_loc = dict(attn_w_in=m_attn_w_in, attn_w_out=m_attn_w_out, rnn_w_in=m_rnn_w_in,
                 rnn_w_a=m_rnn_w_a, rnn_w_i=m_rnn_w_i, rnn_w_out=m_rnn_w_out,
                 rnn_conv_w=m_rnn_conv_w, rnn_conv_b=m_rnn_conv_b, rnn_b_a=m_rnn_b_a,
                 rnn_b_i=m_rnn_b_i, rnn_lambda=m_rnn_lambda)
    v_loc = dict(attn_w_in=v_attn_w_in, attn_w_out=v_attn_w_out, rnn_w_in=v_rnn_w_in,
                 rnn_w_a=v_rnn_w_a, rnn_w_i=v_rnn_w_i, rnn_w_out=v_rnn_w_out,
                 rnn_conv_w=v_rnn_conv_w, rnn_conv_b=v_rnn_conv_b, rnn_b_a=v_rnn_b_a,
                 rnn_b_i=v_rnn_b_i, rnn_lambda=v_rnn_lambda)
    shapes = {k: tuple(a.shape[1:]) for k, a in w_loc.items()}
    T, D = x.shape[1], x.shape[2]
    n_f = attn_b_f.shape[1]
    tb = min(1024, T)
    tb_bwd = min(512, T)
    tt_rg = min(128, T)
    tt_ln = min(256, T)
    c_idx = lax.axis_index("c").astype(jnp.int32).reshape(1)
    me_idx = (2 * lax.axis_index("x") + lax.axis_index("y")).astype(jnp.int32).reshape(1)
    place = jnp.concatenate([me_idx, c_idx])

    def attn_w_in_full(g_in, idx):
        return _join_columns(g_in, 4 * D + LANES, tr=256, name=f"a_join{idx}")

    def attn_w_out_full(g_out):
        return _to_full(g_out, "attn_w_out", shapes["attn_w_out"])

    def rnn_weights(g_in, g_rows):
        w = _unpack_gathered_rows(g_rows, shapes)
        w["rnn_w_in"] = _to_full(g_in, "rnn_w_in", shapes["rnn_w_in"])
        w["small"] = jnp.concatenate([w["rnn_conv_w"], w["rnn_conv_b"][None], w["rnn_b_a"][None],
                                      w["rnn_b_i"][None], w["rnn_lambda"][None]])
        return w

    g0 = _ag_c(_run_exchange(_Exchange("gather", [attn_w_in[0].astype(BF16)]), "ag_w0_xy"),
               "ag_w0_c")
    later = _Exchange("gather", [
        attn_w_out.astype(BF16), attn_w_in[1].astype(BF16), rnn_w_in.astype(BF16),
        jnp.stack([_pack_rows(w_loc, i, BF16) for i in range(2)])])
    w_attn_in, w_attn_out, w_rnn = [attn_w_in_full(g0[0], 0), None], [None, None], [None, None]
    bf_rows = jnp.pad(attn_b_f, ((0, 0), (0, LANES - n_f)))[:, None, :]

    xs, xb, saved = [x[0]], [x[0]], []
    for layer in range(DEPTH):
        idx, xl, xm = layer // 2, xs[-1], xb[-1]
        if layer % 2 == 0:
            proj = _matmul(xm, w_attn_in[idx], trans_b=False, tm=512, tn=1408,
                           name=f"a_proj{layer}")
            cum_t = _cumsum_fwd(proj, bf_rows[idx], tt=min(512, T), name=f"a_cum{layer}")
            cum2 = cum_t[:N_HEADS].reshape(N_PAIRS, 2, T)
            o, og, lp, *got = _flash_fwd(proj, cum2.reshape(N_PAIRS, 2, T // tb, tb), tb=tb,
                                         name=f"a_fwd{layer}", host=later if layer == 0 else None)
            cum4 = cum2.reshape(N_PAIRS, 2, T // tb_bwd, tb_bwd)
            if layer == 0:
                g1 = _ag_c(got, "ag_w1_c")
                w_attn_out = [attn_w_out_full(g1[0][:, :, i]) for i in range(2)]
                w_attn_in[1] = attn_w_in_full(g1[1], 1)
                w_rnn = [rnn_weights(g1[2][:, :, i], g1[3][:, :, i]) for i in range(2)]
            branch, w_out = og, w_attn_out[idx]
            saved.append((proj, cum4, o, og, lp))
        else:
            w = w_rnn[idx]
            proj = _matmul(xm, w["rnn_w_in"], trans_b=False, tm=512, tn=1024,
                           name=f"r_proj{layer}")
            hs, yr = _rg_fwd(proj, w["small"], w["rnn_w_a"], w["rnn_w_i"], tt=tt_rg,
                             name=f"r_fwd{layer}")
            branch, w_out = yr, w["rnn_w_out"]
            saved.append((proj, hs, yr))
        y, yb, zh, rstd = _out_ln(branch, w_out, xl, ln_g[layer][None], ln_b[layer][None],
                                  tt=512, name=f"out_ln{layer}")
        saved[-1] = saved[-1] + (zh, rstd)
        xs.append(y)
        xb.append(yb)

    def ln_below(layer):
        return saved[layer][-2:] + (ln_g[layer][None],)

    loss_lanes, *ln_grads = _loss_ln_bwd(xs[-1], loss_target[0], *ln_below(DEPTH - 1), tt=tt_ln,
                                         name="loss_ln_bwd")
    loss = lax.psum(jnp.sum(loss_lanes), ("x", "y", "c"))

    def reduce_pair(gs, layer):
        recv = _rs_c(gs, f"rs_c{layer}")
        outs = [_add_own(g, r, c_idx, tr=_row_tile(g.shape[2], 512), name=f"rs_add{layer}_{n}")
                for n, (g, r) in enumerate(zip(gs, recv))]
        return [o[0][:, None] for o in outs], [o[1][:, None] for o in outs]

    part, got_parts = [None] * DEPTH, [None] * DEPTH
    d_ln_g, d_ln_b, d_bf = [None] * DEPTH, [None] * DEPTH, [None, None]
    for layer in reversed(range(DEPTH)):
        idx, xm = layer // 2, xb[layer]
        dz, dzb, dg, db = ln_grads
        d_ln_g[layer], d_ln_b[layer] = dg[0], db[0]
        if layer % 2 == 0:
            w_in, w_out = w_attn_in[idx], w_attn_out[idx]
            proj, cum4, o, og, lp = saved[layer][:5]
            dog = _matmul(dzb, w_out, trans_b=True, tm=512, tn=1024, name=f"a_dog{layer}")
            dwo = _matmul_tn(og, dzb, tm=512, tn=1024, tk=1024, name=f"a_dwo{layer}")
            riders = [l for l in range(layer + 1, DEPTH) if got_parts[l] is None]
            host = _Exchange("scatter8", [g for l in riders for g in part[l]]) if riders else None
            dq, dgate, dk, dv, dcum_q, dcum_k, *got = _flash_bwd(proj, cum4, o, dog, lp, tb=tb_bwd,
                                                                 name=f"a_bwd{layer}", host=host)
            for l in riders:
                got_parts[l], got = got[:len(part[l])], got[len(part[l]):]
            dcum_t = (dcum_q.transpose(0, 2, 1, 3) + dcum_k).reshape(N_HEADS, T)
            dcum_t = jnp.pad(dcum_t, ((0, LANES - N_HEADS), (0, 0)))
            df, dbf = _cumsum_bwd(dcum_t, proj, bf_rows[idx], tt=min(512, T), name=f"a_dcum{layer}")
            d_bf[idx] = dbf[0, :n_f]
            dproj = [dq, dk, dv, dgate, df]
            dwi = _matmul_tn_parts(xm, dproj, tm=512, tk=1024, name=f"a_dwi{layer}")
            gs = [_split_columns(dwi, shapes["attn_w_in"][1], tr=256, name=f"a_split{layer}"),
                  _from_full(dwo, "attn_w_out", shapes["attn_w_out"])]
            if layer > 0:
                part[layer] = gs
                ln_grads = _dx_ln_bwd(dproj, w_in, dz, *ln_below(layer - 1), tm=256,
                                      name=f"a_dx{layer}")
            else:
                part[layer], narrow = reduce_pair(gs, layer)
                dy, *got_parts[layer] = _matmul(dproj, w_in, trans_b=True, tm=512, tn=1024,
                                                name=f"a_dx{layer}", add=dz, add_scale=ALPHA,
                                                host=_Exchange("scatter", narrow))
        else:
            w = w_rnn[idx]
            proj, hs, yr = saved[layer][:3]
            dyr = _matmul(dzb, w["rnn_w_out"], trans_b=True, tm=512, tn=1024, name=f"r_dy{layer}")
            dwo = _matmul_tn(yr, dzb, tm=512, tn=1024, tk=1024, name=f"r_dwo{layer}")
            dproj, dwa, dwi_, dsm = _rg_bwd(proj, hs, dyr, w["small"], w["rnn_w_a"], w["rnn_w_i"],
                                            tt=tt_rg, name=f"r_bwd{layer}")
            dwin = _matmul_tn(xm, dproj, tm=512, tn=2048, tk=1024, name=f"r_dwi{layer}")
            ln_grads = _dx_ln_bwd([dproj], w["rnn_w_in"], dz, *ln_below(layer - 1), tm=512,
                                  name=f"r_dx{layer}")
            full = dict(rnn_w_out=dwo, rnn_w_a=dwa, rnn_w_i=dwi_, rnn_conv_w=dsm[0:4],
                        rnn_conv_b=dsm[4], rnn_b_a=dsm[5], rnn_b_i=dsm[6], rnn_lambda=dsm[7])
            part[layer] = [_from_full(dwin, "rnn_w_in", shapes["rnn_w_in"]),
                           _pack_grad_rows(full, shapes)]
    grad_x = dy[None]

    def grad_parts(layer, n):
        make = _two_stage_parts if layer == 0 else _direct_parts
        return make(part[layer][n], got_parts[layer][n])

    def update(k, n):
        res = None
        for idx in (1, 0):
            layer = 2 * idx + (0 if k.startswith("attn") else 1)
            res = _adamw_shard(grad_parts(layer, n), place, w_loc[k], m_loc[k], v_loc[k],
                               idx=idx, prev=res, tr=_row_tile(shapes[k][0], 256),
                               name=f"adamw_{k}{idx}")
        return res

    shard_outs = [dict() for _ in range(4)]
    for k, n in (("attn_w_in", 0), ("attn_w_out", 1), ("rnn_w_in", 0)):
        for j, a in enumerate(update(k, n)):
            shard_outs[j][k] = a
    rows = []
    for idx in range(2):
        layer = 2 * idx + 1
        wmv = [_pack_rows(d, idx, F32)[None] for d in (w_loc, m_loc, v_loc)]
        res = _adamw_shard(grad_parts(layer, 1), place, *wmv, idx=0, prev=None,
                           tr=_row_tile(wmv[0].shape[1], 256), name=f"adamw_rows{idx}")
        rows.append([_unpack_rows(a[0], shapes) for a in res])
    for j in range(4):
        for k in RNN_ROWED + SMALL:
            shard_outs[j][k] = jnp.stack([rows[0][j][k], rows[1][j][k]])
    g_sh, d_sh, nm_sh, nv_sh = shard_outs

    def rep_pack(lg, lb, bf):
        rows = jnp.concatenate([lg, lb, jnp.pad(bf.reshape(1, -1), ((0, 0), (0, D - 2 * n_f)))])
        return jnp.pad(rows, ((0, 16 - rows.shape[0]), (0, 0)))

    rep = _all_gather(rep_pack(jnp.stack(d_ln_g), jnp.stack(d_ln_b), jnp.stack(d_bf)), "ag_rep")
    rg, rd, rm, rv = _adamw(rep.reshape(8, 16, D), rep_pack(ln_g, ln_b, attn_b_f),
                            rep_pack(m_ln_g, m_ln_b, m_attn_b_f),
                            rep_pack(v_ln_g, v_ln_b, v_attn_b_f), tr=16, name="adamw_rep")

    def rep_unpack(a):
        return dict(ln_g=a[0:DEPTH], ln_b=a[DEPTH:2 * DEPTH],
                    attn_b_f=a[2 * DEPTH, :2 * n_f].reshape(2, n_f))

    order = ("ln_g", "ln_b", "attn_w_in", "attn_b_f", "attn_w_out", "rnn_w_in", "rnn_conv_w",
             "rnn_conv_b", "rnn_w_a", "rnn_b_a", "rnn_w_i", "rnn_b_i", "rnn_lambda", "rnn_w_out")
    outs = [loss, grad_x]
    for sh, rp in ((g_sh, rg), (d_sh, rd), (nm_sh, rm), (nv_sh, rv)):
        allp = {**sh, **rep_unpack(rp)}
        outs.extend(allp[k] for k in order)
    return tuple(outs)
```

```python
import jax
import jax.numpy as jnp
from jax import lax
from jax.experimental import pallas as pl
from jax.experimental.pallas import tpu as pltpu

F32 = jnp.float32
BF16 = jnp.bfloat16

DEPTH = 4
N_HEADS = 16
HEAD_DIM = 64
N_PAIRS = N_HEADS // 2
RNN_BLOCKS = 4
RNN_BLOCK_WIDTH = 256
CONV_WIDTH = 4
LRU_C = 8.0
ALPHA = (2.0 * DEPTH) ** 0.25
LN_EPS = 1e-5
ADAM_LR, ADAM_B1, ADAM_B2, ADAM_EPS, ADAM_WD, ADAM_STEP = 0.001, 0.9, 0.999, 1e-8, 0.01, 10

LANES = 128
SUBLANES = 8
VMEM_LIMIT = 48 * 1024 * 1024

MESH = pl.DeviceIdType.MESH
HBM_SPEC = pl.BlockSpec(memory_space=pltpu.HBM)


def _cparams(*sem):
    return pltpu.CompilerParams(dimension_semantics=sem, vmem_limit_bytes=VMEM_LIMIT)


def _sigmoid(x):
    return 1.0 / (1.0 + jnp.exp(-x))


def _softplus(x):
    return jnp.maximum(x, 0.0) + jnp.log(1.0 + jnp.exp(-jnp.abs(x)))


def _a2a(src, *, group, bcast, name):
    n = 2 if group == "c" else 4
    blk = tuple(src.shape) if bcast else tuple(src.shape[1:])

    def body(src_ref, out_ref, send_sems, recv_sems, local_sem):
        x, y, c = lax.axis_index("x"), lax.axis_index("y"), lax.axis_index("c")
        if group == "c":
            me = c

            def peer(d):
                return (x, y, 1 - c), 1 - c
        else:
            me = 2 * x + y

            def peer(d):
                px, py = x ^ (d >> 1), y ^ (d & 1)
                return (px, py, c), 2 * px + py

        def block_for(k):
            return src_ref if bcast else src_ref.at[k]

        local = pltpu.make_async_copy(block_for(me), out_ref.at[me], local_sem)
        local.start()
        sends = []
        for d in range(1, n):
            dev, idx = peer(d)
            cp = pltpu.make_async_remote_copy(
                src_ref=block_for(idx), dst_ref=out_ref.at[me],
                send_sem=send_sems.at[d], recv_sem=recv_sems.at[d],
                device_id=dev, device_id_type=MESH)
            cp.start()
            sends.append(cp)
        for d in range(1, n):
            dev, idx = peer(d)
            pltpu.make_async_remote_copy(
                src_ref=block_for(idx), dst_ref=out_ref.at[idx],
                send_sem=send_sems.at[d], recv_sem=recv_sems.at[d],
                device_id=dev, device_id_type=MESH).wait_recv()
        for cp in sends:
            cp.wait_send()
        local.wait()

    return pl.pallas_call(
        body, name=name,
        out_shape=jax.ShapeDtypeStruct((n,) + blk, src.dtype),
        in_specs=[HBM_SPEC], out_specs=HBM_SPEC,
        scratch_shapes=[pltpu.SemaphoreType.DMA((n,)), pltpu.SemaphoreType.DMA((n,)),
                        pltpu.SemaphoreType.DMA],
    )(src)


def _all_gather(piece, name):
    return _a2a(_a2a(piece, group="xy", bcast=True, name=name + "_xy"),
                group="c", bcast=True, name=name + "_c")


D2D_CHUNKS = 16
ICI_CHUNKS = 8


def _row_chunks(rows, dtype, k):
    unit = SUBLANES * (4 // jnp.dtype(dtype).itemsize)
    assert rows % unit == 0
    units = rows // unit
    k = max(1, min(k, units))
    base, rem = divmod(units, k)
    out, r = [], 0
    for i in range(k):
        n = (base + (1 if i < rem else 0)) * unit
        out.append((r, n))
        r += n
    return out


def _chunks(shape, dtype, k):
    if len(shape) == 2:
        return [(pl.ds(r0, n),) for r0, n in _row_chunks(shape[0], dtype, k)]
    per = max(1, k // shape[0])
    return [(l, pl.ds(r0, n)) for l in range(shape[0]) for r0, n in _row_chunks(shape[1], dtype, per)]


def _mesh_place():
    x, y, c = lax.axis_index("x"), lax.axis_index("y"), lax.axis_index("c")
    return x, y, c, 2 * x + y


def _chip_peer(x, y, c, d):
    px, py = x ^ (d >> 1), y ^ (d & 1)
    return (px, py, c), 2 * px + py


def _remote(src, dst, send_sem, recv_sem, dev):
    return pltpu.make_async_remote_copy(src_ref=src, dst_ref=dst, send_sem=send_sem,
                                        recv_sem=recv_sem, device_id=dev, device_id_type=MESH)


def _comm_call(body, name, ins, out_shapes, n_sems, aliases=None):
    n = len(ins)
    return pl.pallas_call(
        body, name=name,
        out_shape=out_shapes, in_specs=[HBM_SPEC] * n, out_specs=[HBM_SPEC] * n,
        input_output_aliases=aliases or {},
        scratch_shapes=[pltpu.SemaphoreType.DMA((n_sems, n)), pltpu.SemaphoreType.DMA((n_sems, n))],
    )(*ins)


class _Exchange:
    def __init__(self, kind, arrays):
        self.kind, self.arrays, self.n = kind, list(arrays), len(arrays)
        self.is_gather, self.all8 = kind.startswith("gather"), kind.endswith("8")
        k = ICI_CHUNKS // 4 if self.all8 else ICI_CHUNKS
        if self.is_gather:
            self.chunks = [_chunks(a.shape, a.dtype, k) for a in arrays]
            self.out_shapes = [jax.ShapeDtypeStruct((2, 4) + tuple(a.shape), a.dtype) for a in arrays]
        else:
            lead = 2 if self.all8 else 1
            self.chunks = [_chunks(a.shape[lead:], a.dtype, k) for a in arrays]
            self.out_shapes = [jax.ShapeDtypeStruct(a.shape, a.dtype) for a in arrays]
        self.peers = list(range(1, 8 if self.all8 else 4))
        n_sems = len(self.peers) + 1
        self.sem_shapes = [pltpu.SemaphoreType.DMA((n_sems, self.n)),
                           pltpu.SemaphoreType.DMA((n_sems, self.n))]

    def _peer(self, x, y, c, me, p):
        a, d = p // 4, p % 4
        px, py = x ^ (d >> 1), y ^ (d & 1)
        pc = 1 - c if a else c
        if self.all8:
            return (px, py, pc), (pc, 2 * px + py), (c, me)
        return (px, py, pc), (2 * px + py,), (me,)

    def _blocks(self, srcs, outs, o, c, me, theirs, mine):
        if self.kind == "gather":
            return srcs[o], outs[o].at[(c,) + mine], outs[o].at[(c,) + theirs]
        if self.kind == "gather8":
            return srcs[o], outs[o].at[mine], outs[o].at[theirs]
        return srcs[o].at[theirs], outs[o].at[mine], outs[o].at[theirs]

    def start(self, srcs, outs, send_sems, recv_sems):
        x, y, c, me = _mesh_place()
        if self.is_gather:
            for o in range(self.n):
                for idx in self.chunks[o]:
                    pltpu.make_async_copy(srcs[o].at[idx], outs[o].at[(c, me) + idx],
                                          send_sems.at[0, o]).start()
        for p in self.peers:
            dev, theirs, mine = self._peer(x, y, c, me, p)
            for o in range(self.n):
                src, dst, _ = self._blocks(srcs, outs, o, c, me, theirs, mine)
                for idx in self.chunks[o]:
                    _remote(src.at[idx], dst.at[idx], send_sems.at[p, o], recv_sems.at[p, o],
                            dev).start()

    def wait(self, srcs, outs, send_sems, recv_sems):
        x, y, c, me = _mesh_place()
        for wait_recv in (True, False):
            for p in self.peers:
                dev, theirs, mine = self._peer(x, y, c, me, p)
                for o in range(self.n):
                    src, _, land = self._blocks(srcs, outs, o, c, me, theirs, mine)
                    cp = _remote(src, land, send_sems.at[p, o], recv_sems.at[p, o], dev)
                    cp.wait_recv() if wait_recv else cp.wait_send()
        if self.is_gather:
            for o in range(self.n):
                pltpu.make_async_copy(srcs[o], outs[o].at[c, me], send_sems.at[0, o]).wait()


def _run_exchange(ex, name):
    n = ex.n

    def body(*refs):
        srcs, outs, send_sems, recv_sems = refs[:n], refs[n:2 * n], refs[2 * n], refs[2 * n + 1]
        ex.start(srcs, outs, send_sems, recv_sems)
        ex.wait(srcs, outs, send_sems, recv_sems)

    return _comm_call(body, name, ex.arrays, ex.out_shapes, len(ex.peers) + 1)


def _ag_c(bufs, name):
    n = len(bufs)
    chunks = [_chunks(b.shape[2:], b.dtype, D2D_CHUNKS // 4) for b in bufs]

    def body(*refs):
        srcs, outs, send_sems, recv_sems = refs[:n], refs[n:2 * n], refs[2 * n], refs[2 * n + 1]
        x, y, c, _ = _mesh_place()
        sib = (x, y, 1 - c)
        for o in range(n):
            for k in range(4):
                for idx in chunks[o]:
                    _remote(srcs[o].at[(c, k) + idx], outs[o].at[(c, k) + idx],
                            send_sems.at[0, o], recv_sems.at[0, o], sib).start()
        for o in range(n):
            _remote(srcs[o].at[c], outs[o].at[1 - c], send_sems.at[0, o], recv_sems.at[0, o],
                    sib).wait_recv()
        for o in range(n):
            _remote(srcs[o].at[c], outs[o].at[1 - c], send_sems.at[0, o], recv_sems.at[0, o],
                    sib).wait_send()

    shapes = [jax.ShapeDtypeStruct(b.shape, b.dtype) for b in bufs]
    return _comm_call(body, name, bufs, shapes, 1, aliases={i: i for i in range(n)})


def _rs_c(gs, name):
    n = len(gs)
    chunks = [_chunks(g.shape[2:], g.dtype, max(1, D2D_CHUNKS // g.shape[1])) for g in gs]

    def body(*refs):
        srcs, outs, send_sems, recv_sems = refs[:n], refs[n:2 * n], refs[2 * n], refs[2 * n + 1]
        x, y, c, _ = _mesh_place()
        sib = (x, y, 1 - c)
        for o in range(n):
            for k in range(gs[o].shape[1]):
                for idx in chunks[o]:
                    _remote(srcs[o].at[(1 - c, k) + idx], outs[o].at[(k,) + idx],
                            send_sems.at[0, o], recv_sems.at[0, o], sib).start()
        for o in range(n):
            _remote(srcs[o].at[1 - c], outs[o], send_sems.at[0, o], recv_sems.at[0, o],
                    sib).wait_recv()
        for o in range(n):
            _remote(srcs[o].at[1 - c], outs[o], send_sems.at[0, o], recv_sems.at[0, o],
                    sib).wait_send()

    shapes = [jax.ShapeDtypeStruct(g.shape[1:], g.dtype) for g in gs]
    return _comm_call(body, name, gs, shapes, 1)


def _matmul(a, b, *, trans_b, tm, tn, name, add=None, add_scale=1.0, host=None):
    a_parts = list(a) if isinstance(a, (list, tuple)) else [a]
    M, K = a_parts[0].shape[0], sum(p.shape[1] for p in a_parts)
    N = b.shape[0] if trans_b else b.shape[1]
    tm, tn = min(tm, M), min(tn, N)
    assert M % tm == 0 and N % tn == 0
    dn = (((1,), (1,)), ((), ())) if trans_b else (((1,), (0,)), ((), ()))
    na = len(a_parts)

    def body(*refs):
        a_refs, b_ref, o_ref = refs[:na], refs[na], refs[-1]
        av = [r[...].astype(BF16) for r in a_refs]
        av = av[0] if na == 1 else jnp.concatenate(av, axis=1)
        r = lax.dot_general(av, b_ref[...].astype(BF16), dn, preferred_element_type=F32)
        if add is not None:
            r = r + add_scale * refs[na + 1][...]
        o_ref[...] = r

    b_spec = (pl.BlockSpec((tn, K), lambda j, i: (j, 0)) if trans_b
              else pl.BlockSpec((K, tn), lambda j, i: (0, j)))
    in_specs = [pl.BlockSpec((tm, p.shape[1]), lambda j, i: (i, 0)) for p in a_parts] + [b_spec]
    args = a_parts + [b]
    if add is not None:
        in_specs.append(pl.BlockSpec((tm, tn), lambda j, i: (i, j)))
        args.append(add)
    grid = (N // tn, M // tm)
    x_in, x_out, x_shapes, x_scratch, x_args = _host_specs(host)
    body = _hosted(body, len(args), 1, 0, host, grid)
    outs = pl.pallas_call(
        body, name=name, grid=grid,
        in_specs=in_specs + x_in,
        out_specs=[pl.BlockSpec((tm, tn), lambda j, i: (i, j))] + x_out,
        out_shape=[jax.ShapeDtypeStruct((M, N), F32)] + x_shapes,
        scratch_shapes=x_scratch,
        compiler_params=_cparams(*(("arbitrary",) * 2 if host else ("parallel",) * 2)),
    )(*args, *x_args)
    return outs if host else outs[0]


def _matmul_tn(a, b, *, tm, tn, tk, name):
    T, M = a.shape
    N = b.shape[1]
    tm, tn, tk = min(tm, M), min(tn, N), min(tk, T)
    assert M % tm == 0 and N % tn == 0 and T % tk == 0

    def body(a_ref, b_ref, o_ref):
        @pl.when(pl.program_id(2) == 0)
        def _():
            o_ref[...] = jnp.zeros_like(o_ref)

        o_ref[...] += lax.dot_general(a_ref[...].astype(BF16), b_ref[...].astype(BF16),
                                      (((0,), (0,)), ((), ())), preferred_element_type=F32)

    return pl.pallas_call(
        body, name=name, grid=(M // tm, N // tn, T // tk),
        in_specs=[pl.BlockSpec((tk, tm), lambda i, j, k: (k, i)),
                  pl.BlockSpec((tk, tn), lambda i, j, k: (k, j))],
        out_specs=pl.BlockSpec((tm, tn), lambda i, j, k: (i, j)),
        out_shape=jax.ShapeDtypeStruct((M, N), F32),
        compiler_params=_cparams("parallel", "parallel", "arbitrary"),
    )(a, b)


def _matmul_tn_parts(a, parts, *, tm, tk, name):
    T, M = a.shape
    tm, tk = min(tm, M), min(tk, T)
    assert M % tm == 0 and T % tk == 0
    n = len(parts)

    def body(*refs):
        a_ref, b_refs, o_refs = refs[0], refs[1:1 + n], refs[1 + n:]
        av = a_ref[...].astype(BF16)
        for b_ref, o_ref in zip(b_refs, o_refs):
            @pl.when(pl.program_id(1) == 0)
            def _(o_ref=o_ref):
                o_ref[...] = jnp.zeros_like(o_ref)

            o_ref[...] += lax.dot_general(av, b_ref[...].astype(BF16), (((0,), (0,)), ((), ())),
                                          preferred_element_type=F32)

    return pl.pallas_call(
        body, name=name, grid=(M // tm, T // tk),
        in_specs=[pl.BlockSpec((tk, tm), lambda i, k: (k, i))]
        + [pl.BlockSpec((tk, p.shape[1]), lambda i, k: (k, 0)) for p in parts],
        out_specs=[pl.BlockSpec((tm, p.shape[1]), lambda i, k: (i, 0)) for p in parts],
        out_shape=[jax.ShapeDtypeStruct((M, p.shape[1]), F32) for p in parts],
        compiler_params=_cparams("parallel", "arbitrary"),
    )(a, *parts)


def _head_masks(rows):
    lane = lax.broadcasted_iota(jnp.int32, (rows, LANES), 1)
    return lane < HEAD_DIM, lane >= HEAD_DIM


def _causal(i_q, i_k, tq, tk):
    row = i_q * tq + lax.broadcasted_iota(jnp.int32, (tq, tk), 0)
    col = i_k * tk + lax.broadcasted_iota(jnp.int32, (tq, tk), 1)
    return row >= col


def _hosted(body, n_in, n_out, n_scratch, host, grid):
    if host is None:
        return body
    nx = host.n

    def wrapped(*refs):
        ins, xsrcs = refs[:n_in], refs[n_in:n_in + nx]
        outs = refs[n_in + nx:n_in + nx + n_out]
        xouts = refs[n_in + nx + n_out:n_in + 2 * nx + n_out]
        scratch = refs[n_in + 2 * nx + n_out:n_in + 2 * nx + n_out + n_scratch]
        xsems = refs[n_in + 2 * nx + n_out + n_scratch:]
        step = pl.program_id(0) * grid[1] + pl.program_id(1)

        @pl.when(step == 0)
        def _():
            host.start(xsrcs, xouts, *xsems)

        body(*ins, *outs, *scratch)

        @pl.when(step == grid[0] * grid[1] - 1)
        def _():
            host.wait(xsrcs, xouts, *xsems)

    return wrapped


def _host_specs(host):
    if host is None:
        return [], [], [], [], []
    return ([HBM_SPEC] * host.n, [HBM_SPEC] * host.n, host.out_shapes, host.sem_shapes, host.arrays)


def _flash_fwd(proj, cum4, *, tb, name, host=None):
    T = proj.shape[0]
    D = N_HEADS * HEAD_DIM
    nb = T // tb
    cb = D // LANES
    x_in, x_out, x_shapes, x_scratch, x_args = _host_specs(host)

    def body(q_ref, k_ref, v_ref, g_ref, cum_ref, o_ref, og_ref, lp_ref, kb_ref, vb_ref):
        i = pl.program_id(1)

        @pl.when(i == 0)
        def _():
            kb_ref[...] = k_ref[...].astype(BF16)
            vb_ref[...] = v_ref[...].astype(BF16)

        q = q_ref[...] * (HEAD_DIM ** -0.5)
        masks = _head_masks(tb)
        qh = [jnp.where(masks[h], q, 0.0).astype(BF16) for h in range(2)]
        cref = [cum_ref[0, h, pl.ds(i, 1), :][:, 0:1] for h in range(2)]

        def step(kbi, carry, masked):
            k0 = pl.multiple_of(kbi * tb, tb)
            kblk = kb_ref[pl.ds(k0, tb), :]
            vblk = vb_ref[pl.ds(k0, tb), :]
            new = []
            for h in range(2):
                m, l, acc = carry[h]
                s = lax.dot_general(qh[h], kblk, (((1,), (1,)), ((), ())),
                                    preferred_element_type=F32)
                s = s + (cref[h] - cum_ref[0, h, pl.ds(kbi, 1), :])
                if masked:
                    s = jnp.where(_causal(i, kbi, tb, tb), s, -jnp.inf)
                m_new = jnp.maximum(m, jnp.max(s, axis=-1, keepdims=True))
                alpha = jnp.exp(m - m_new)
                p = jnp.exp(s - m_new)
                l = alpha * l + jnp.sum(p, axis=-1, keepdims=True)
                acc = alpha * acc + jnp.dot(p.astype(BF16), vblk, preferred_element_type=F32)
                new.append((m_new, l, acc))
            return tuple(new)

        init1 = (jnp.full((tb, 1), -jnp.inf, F32), jnp.zeros((tb, 1), F32),
                 jnp.zeros((tb, LANES), F32))
        carry = lax.fori_loop(0, i, lambda kbi, c: step(kbi, c, False), (init1, init1))
        outs = []
        for h, (m, l, acc) in enumerate(step(i, carry, True)):
            outs.append(acc / l)
            lp_ref[h] = jnp.broadcast_to(m + jnp.log(l) - cref[h], (tb, LANES))
        o = jnp.where(masks[0], outs[0], outs[1])
        o_ref[...] = o
        gate = g_ref[...]
        og_ref[...] = (o * (gate * _sigmoid(gate))).astype(BF16)

    body = _hosted(body, 5, 3, 2, host, (N_PAIRS, nb))
    return pl.pallas_call(
        body, name=name, grid=(N_PAIRS, nb),
        in_specs=[pl.BlockSpec((tb, LANES), lambda j, i: (i, j)),
                  pl.BlockSpec((T, LANES), lambda j, i: (0, cb + j)),
                  pl.BlockSpec((T, LANES), lambda j, i: (0, 2 * cb + j)),
                  pl.BlockSpec((tb, LANES), lambda j, i: (i, 3 * cb + j)),
                  pl.BlockSpec((1, 2, nb, tb), lambda j, i: (j, 0, 0, 0))] + x_in,
        out_specs=[pl.BlockSpec((tb, LANES), lambda j, i: (i, j)),
                   pl.BlockSpec((tb, LANES), lambda j, i: (i, j)),
                   pl.BlockSpec((2, tb, LANES), lambda j, i: (j, i, 0))] + x_out,
        out_shape=[jax.ShapeDtypeStruct((T, D), F32), jax.ShapeDtypeStruct((T, D), BF16),
                   jax.ShapeDtypeStruct((N_HEADS, T, LANES), F32)] + x_shapes,
        scratch_shapes=[pltpu.VMEM((T, LANES), BF16), pltpu.VMEM((T, LANES), BF16)] + x_scratch,
        compiler_params=_cparams("arbitrary", "arbitrary"),
    )(proj, proj, proj, proj, cum4, *x_args)


def _flash_bwd(proj, cum4, o, dog, lp, *, tb, name, host=None):
    T = proj.shape[0]
    D = N_HEADS * HEAD_DIM
    nb = T // tb
    cb = D // LANES
    x_in, x_out, x_shapes, x_scratch, x_args = _host_specs(host)

    def body(q_ref, k_ref, v_ref, g_ref, cum_ref, o_ref, dog_ref, lp_ref,
             dq_ref, dg_ref, dk_ref, dv_ref, dcq_ref, dck_ref,
             kb_ref, vb_ref, dka_ref, dva_ref, dca_ref):
        i = pl.program_id(1)

        @pl.when(i == 0)
        def _():
            kb_ref[...] = k_ref[...].astype(BF16)
            vb_ref[...] = v_ref[...].astype(BF16)
            dka_ref[...] = jnp.zeros_like(dka_ref)
            dva_ref[...] = jnp.zeros_like(dva_ref)
            dca_ref[...] = jnp.zeros_like(dca_ref)

        gate = g_ref[...]
        sg = _sigmoid(gate)
        o = o_ref[...]
        dog = dog_ref[...]
        do = dog * (gate * sg)
        dg_ref[...] = (dog * o * (sg * (1.0 + gate * (1.0 - sg)))).astype(BF16)
        q = q_ref[...] * (HEAD_DIM ** -0.5)
        masks = _head_masks(tb)
        qh = [jnp.where(masks[h], q, 0.0).astype(BF16) for h in range(2)]
        doh = [jnp.where(masks[h], do, 0.0).astype(BF16) for h in range(2)]
        delta = [jnp.sum(jnp.where(masks[h], do * o, 0.0), axis=-1, keepdims=True) for h in range(2)]
        lph = [lp_ref[h][:, 0:1] for h in range(2)]

        def step(kbi, carry, masked):
            k0 = pl.multiple_of(kbi * tb, tb)
            kblk = kb_ref[pl.ds(k0, tb), :]
            vblk = vb_ref[pl.ds(k0, tb), :]
            new, dk, dv = [], None, None
            for h in range(2):
                acc, rs = carry[h]
                s = lax.dot_general(qh[h], kblk, (((1,), (1,)), ((), ())), preferred_element_type=F32)
                p = jnp.exp(s - cum_ref[0, h, pl.ds(kbi, 1), :] - lph[h])
                if masked:
                    p = jnp.where(_causal(i, kbi, tb, tb), p, 0.0)
                dp = lax.dot_general(doh[h], vblk, (((1,), (1,)), ((), ())),
                                     preferred_element_type=F32)
                ds = p * (dp - delta[h])
                pb, dsb = p.astype(BF16), ds.astype(BF16)
                dv_h = lax.dot_general(pb, doh[h], (((0,), (0,)), ((), ())),
                                       preferred_element_type=F32)
                dk_h = lax.dot_general(dsb, qh[h], (((0,), (0,)), ((), ())),
                                       preferred_element_type=F32)
                dv = dv_h if dv is None else dv + dv_h
                dk = dk_h if dk is None else dk + dk_h
                dca_ref[h, pl.ds(kbi, 1), :] -= jnp.sum(ds, axis=0, keepdims=True)
                new.append((acc + jnp.dot(dsb, kblk, preferred_element_type=F32),
                            rs + jnp.sum(ds, axis=-1, keepdims=True)))
            dka_ref[pl.ds(k0, tb), :] += dk
            dva_ref[pl.ds(k0, tb), :] += dv
            return tuple(new)

        init1 = (jnp.zeros((tb, LANES), F32), jnp.zeros((tb, 1), F32))
        carry = lax.fori_loop(0, i, lambda kbi, c: step(kbi, c, False), (init1, init1))
        dqs = []
        for h, (acc, rs) in enumerate(step(i, carry, True)):
            dqs.append(acc)
            dcq_ref[0, 0, pl.ds(h, 1), :] = jnp.broadcast_to(rs, (tb, LANES)).T[0:1, :]
        dq_ref[...] = (jnp.where(masks[0], dqs[0], dqs[1]) * (HEAD_DIM ** -0.5)).astype(BF16)

        @pl.when(i == nb - 1)
        def _():
            dk_ref[...] = dka_ref[...].astype(BF16)
            dv_ref[...] = dva_ref[...].astype(BF16)
            dck_ref[0] = dca_ref[...]

    blk = pl.BlockSpec((tb, LANES), lambda j, i: (i, j))
    full = pl.BlockSpec((T, LANES), lambda j, i: (0, j))
    body = _hosted(body, 8, 6, 5, host, (N_PAIRS, nb))
    return pl.pallas_call(
        body, name=name, grid=(N_PAIRS, nb),
        in_specs=[blk,
                  pl.BlockSpec((T, LANES), lambda j, i: (0, cb + j)),
                  pl.BlockSpec((T, LANES), lambda j, i: (0, 2 * cb + j)),
                  pl.BlockSpec((tb, LANES), lambda j, i: (i, 3 * cb + j)),
                  pl.BlockSpec((1, 2, nb, tb), lambda j, i: (j, 0, 0, 0)),
                  blk, blk, pl.BlockSpec((2, tb, LANES), lambda j, i: (j, i, 0))] + x_in,
        out_specs=[blk, blk, full, full,
                   pl.BlockSpec((1, 1, 2, tb), lambda j, i: (j, i, 0, 0)),
                   pl.BlockSpec((1, 2, nb, tb), lambda j, i: (j, 0, 0, 0))] + x_out,
        out_shape=[jax.ShapeDtypeStruct((T, D), BF16)] * 4
        + [jax.ShapeDtypeStruct((N_PAIRS, nb, 2, tb), F32),
           jax.ShapeDtypeStruct((N_PAIRS, 2, nb, tb), F32)] + x_shapes,
        scratch_shapes=[pltpu.VMEM((T, LANES), BF16), pltpu.VMEM((T, LANES), BF16),
                        pltpu.VMEM((T, LANES), F32), pltpu.VMEM((T, LANES), F32),
                        pltpu.VMEM((2, nb, tb), F32)] + x_scratch,
        compiler_params=_cparams("arbitrary", "arbitrary"),
    )(proj, proj, proj, proj, cum4, o, dog, lp, *x_args)


def _cumsum_fwd(proj, bf_row, *, tt, name):
    T = proj.shape[0]
    cb = (proj.shape[1] - LANES) // LANES

    def body(f_ref, b_ref, out_ref, carry_ref):
        i = pl.program_id(0)

        @pl.when(i == 0)
        def _():
            carry_ref[...] = jnp.zeros_like(carry_ref)

        ls = -_softplus(-(f_ref[...] + b_ref[...]))
        tri = (lax.broadcasted_iota(jnp.int32, (tt, tt), 0)
               >= lax.broadcasted_iota(jnp.int32, (tt, tt), 1)).astype(F32)
        cum = jnp.dot(tri, ls, preferred_element_type=F32,
                      precision=lax.Precision.HIGHEST) + carry_ref[...]
        carry_ref[...] = cum[tt - 1:tt, :]
        out_ref[...] = cum.T

    return pl.pallas_call(
        body, name=name, grid=(T // tt,),
        in_specs=[pl.BlockSpec((tt, LANES), lambda i: (i, cb)),
                  pl.BlockSpec((1, LANES), lambda i: (0, 0))],
        out_specs=pl.BlockSpec((LANES, tt), lambda i: (0, i)),
        out_shape=jax.ShapeDtypeStruct((LANES, T), F32),
        scratch_shapes=[pltpu.VMEM((1, LANES), F32)],
        compiler_params=_cparams("arbitrary"),
    )(proj, bf_row)


def _cumsum_bwd(dcum_t, proj, bf_row, *, tt, name):
    T = proj.shape[0]
    cb = (proj.shape[1] - LANES) // LANES
    nt = T // tt

    def body(dc_ref, f_ref, b_ref, df_ref, db_ref, carry_ref):
        i = pl.program_id(0)

        @pl.when(i == 0)
        def _():
            carry_ref[...] = jnp.zeros_like(carry_ref)
            db_ref[...] = jnp.zeros_like(db_ref)

        dc = dc_ref[...].T
        tri = (lax.broadcasted_iota(jnp.int32, (tt, tt), 0)
               <= lax.broadcasted_iota(jnp.int32, (tt, tt), 1)).astype(F32)
        rev = jnp.dot(tri, dc, preferred_element_type=F32,
                      precision=lax.Precision.HIGHEST) + carry_ref[...]
        carry_ref[...] = rev[0:1, :]
        df = rev * _sigmoid(-(f_ref[...] + b_ref[...]))
        df_ref[...] = df.astype(BF16)
        db_ref[...] += jnp.sum(df, axis=0, keepdims=True)

    return pl.pallas_call(
        body, name=name, grid=(nt,),
        in_specs=[pl.BlockSpec((LANES, tt), lambda i: (0, nt - 1 - i)),
                  pl.BlockSpec((tt, LANES), lambda i: (nt - 1 - i, cb)),
                  pl.BlockSpec((1, LANES), lambda i: (0, 0))],
        out_specs=[pl.BlockSpec((tt, LANES), lambda i: (nt - 1 - i, 0)),
                   pl.BlockSpec((1, LANES), lambda i: (0, 0))],
        out_shape=[jax.ShapeDtypeStruct((T, LANES), BF16), jax.ShapeDtypeStruct((1, LANES), F32)],
        scratch_shapes=[pltpu.VMEM((1, LANES), F32)],
        compiler_params=_cparams("arbitrary"),
    )(dcum_t, proj, bf_row)


def _rows_down(x, before, sh):
    if sh == 0:
        return x
    rolled = pltpu.roll(x, sh, axis=0)
    row = lax.broadcasted_iota(jnp.int32, (SUBLANES, x.shape[1]), 0)
    head = jnp.where(row < sh, pltpu.roll(before, sh, axis=0), rolled[:SUBLANES])
    return jnp.concatenate([head, rolled[SUBLANES:]], axis=0)


def _rows_up(x, after, sh):
    if sh == 0:
        return x
    tt = x.shape[0]
    rolled = pltpu.roll(x, tt - sh, axis=0)
    row = lax.broadcasted_iota(jnp.int32, (SUBLANES, x.shape[1]), 0)
    tail = jnp.where(row >= SUBLANES - sh, pltpu.roll(after, SUBLANES - sh, axis=0),
                     rolled[tt - SUBLANES:])
    return jnp.concatenate([rolled[:tt - SUBLANES], tail], axis=0)


def _rg_gates(u0, before, small_ref, wa_ref, wi_ref):
    taps = [_rows_down(u0, before, CONV_WIDTH - 1 - tap) for tap in range(CONV_WIDTH)]
    u = small_ref[4:5, :]
    for tap in range(CONV_WIDTH):
        u = u + taps[tap] * small_ref[tap:tap + 1, :]
    pa, pi = [], []
    for n in range(RNN_BLOCKS):
        ub = u[:, n * RNN_BLOCK_WIDTH:(n + 1) * RNN_BLOCK_WIDTH].astype(BF16)
        pa.append(jnp.dot(ub, wa_ref[n], preferred_element_type=F32))
        pi.append(jnp.dot(ub, wi_ref[n], preferred_element_type=F32))
    r = _sigmoid(jnp.concatenate(pa, axis=-1) + small_ref[5:6, :])
    ig = _sigmoid(jnp.concatenate(pi, axis=-1) + small_ref[6:7, :])
    spl = _softplus(-small_ref[7:8, :])
    log_a = (-LRU_C) * r * spl
    a = jnp.exp(log_a)
    s2 = jnp.tanh(-log_a) * (a * a + 1.0)
    inv_s = lax.rsqrt(s2)
    s = jnp.where(s2 > 0.0, s2 * inv_s, 0.0)
    return u, taps, r, ig, spl, a, s, inv_s


def _rg_fwd(proj, small, wa, wi, *, tt, name):
    T = proj.shape[0]
    D = RNN_BLOCKS * RNN_BLOCK_WIDTH
    hb = tt // SUBLANES

    def body(u0_ref, halo_ref, g_ref, small_ref, wa_ref, wi_ref, h_ref, y_ref,
             a_ref, b_ref, carry_ref):
        i = pl.program_id(0)

        @pl.when(i == 0)
        def _():
            carry_ref[...] = jnp.zeros_like(carry_ref)

        before = jnp.where(i == 0, 0.0, halo_ref[...])
        u, _, r, ig, spl, a, s, _ = _rg_gates(u0_ref[...], before, small_ref, wa_ref, wi_ref)
        a_ref[...] = a
        b_ref[...] = s * (ig * u)

        def row(t, h):
            h = a_ref[pl.ds(t, 1), :] * h + b_ref[pl.ds(t, 1), :]
            h_ref[pl.ds(t, 1), :] = h
            return h

        carry_ref[...] = lax.fori_loop(0, tt, row, carry_ref[...])
        gate = g_ref[...]
        y_ref[...] = (h_ref[...] * (gate * _sigmoid(gate))).astype(BF16)

    return pl.pallas_call(
        body, name=name, grid=(T // tt,),
        in_specs=[pl.BlockSpec((tt, D), lambda i: (i, 0)),
                  pl.BlockSpec((SUBLANES, D), lambda i: (jnp.maximum(i * hb - 1, 0), 0)),
                  pl.BlockSpec((tt, D), lambda i: (i, 1)),
                  pl.BlockSpec((SUBLANES, D), lambda i: (0, 0)),
                  pl.BlockSpec((RNN_BLOCKS, RNN_BLOCK_WIDTH, RNN_BLOCK_WIDTH), lambda i: (0, 0, 0)),
                  pl.BlockSpec((RNN_BLOCKS, RNN_BLOCK_WIDTH, RNN_BLOCK_WIDTH), lambda i: (0, 0, 0))],
        out_specs=[pl.BlockSpec((tt, D), lambda i: (i, 0)), pl.BlockSpec((tt, D), lambda i: (i, 0))],
        out_shape=[jax.ShapeDtypeStruct((T, D), F32), jax.ShapeDtypeStruct((T, D), BF16)],
        scratch_shapes=[pltpu.VMEM((tt, D), F32), pltpu.VMEM((tt, D), F32),
                        pltpu.VMEM((1, D), F32)],
        compiler_params=_cparams("arbitrary"),
    )(proj, proj, proj, small, wa, wi)


def _rg_bwd(proj, hs, dy, small, wa, wi, *, tt, name):
    T = proj.shape[0]
    D = RNN_BLOCKS * RNN_BLOCK_WIDTH
    W = RNN_BLOCK_WIDTH
    hb = tt // SUBLANES
    nt = T // tt

    def body(u0_ref, uhalo_ref, g_ref, h_ref, hhalo_ref, dy_ref, small_ref, wa_ref, wi_ref,
             dp_ref, dwa_ref, dwi_ref, ds_ref,
             a_ref, g_s_ref, dunext_ref, carry_ref):
        i = pl.program_id(0)
        first_chunk = i == nt - 1

        @pl.when(i == 0)
        def _():
            carry_ref[...] = jnp.zeros_like(carry_ref)
            dunext_ref[...] = jnp.zeros_like(dunext_ref)
            dwa_ref[...] = jnp.zeros_like(dwa_ref)
            dwi_ref[...] = jnp.zeros_like(dwi_ref)
            ds_ref[...] = jnp.zeros_like(ds_ref)

        u_before = jnp.where(first_chunk, 0.0, uhalo_ref[...])
        h_before = jnp.where(first_chunk, 0.0, hhalo_ref[...])
        u, taps, r, ig, spl, a, s, inv_s = _rg_gates(u0_ref[...], u_before, small_ref, wa_ref,
                                                     wi_ref)
        gate = g_ref[...]
        sg = _sigmoid(gate)
        dy = dy_ref[...]
        dp_ref[:, D:] = (dy * h_ref[...] * (sg * (1.0 + gate * (1.0 - sg)))).astype(BF16)
        a_ref[...] = a
        g_s_ref[...] = dy * (gate * sg)

        def row(k, c):
            t = tt - 1 - k
            g = g_s_ref[pl.ds(t, 1), :] + c
            g_s_ref[pl.ds(t, 1), :] = g
            return a_ref[pl.ds(t, 1), :] * g

        carry_ref[...] = lax.fori_loop(0, tt, row, carry_ref[...])
        g = g_s_ref[...]
        h_prev = _rows_down(h_ref[...], h_before, 1)
        iu = ig * u
        d_iu = g * s
        dlog_a = (g * h_prev) * a - (g * iu) * (a * a) * inv_s
        dpre_a = (dlog_a * ((-LRU_C) * spl)) * r * (1.0 - r)
        dpre_i = (d_iu * u) * ig * (1.0 - ig)
        dlam = jnp.sum(dlog_a * r, axis=0, keepdims=True) * (LRU_C * _sigmoid(-small_ref[7:8, :]))
        du_parts = []
        for n in range(RNN_BLOCKS):
            sl = slice(n * W, (n + 1) * W)
            ub = u[:, sl].astype(BF16)
            da_n = dpre_a[:, sl].astype(BF16)
            di_n = dpre_i[:, sl].astype(BF16)
            dwa_ref[n] += lax.dot_general(ub, da_n, (((0,), (0,)), ((), ())),
                                          preferred_element_type=F32)
            dwi_ref[n] += lax.dot_general(ub, di_n, (((0,), (0,)), ((), ())),
                                          preferred_element_type=F32)
            du_parts.append(
                lax.dot_general(da_n, wa_ref[n], (((1,), (1,)), ((), ())), preferred_element_type=F32)
                + lax.dot_general(di_n, wi_ref[n], (((1,), (1,)), ((), ())), preferred_element_type=F32))
        du = d_iu * ig + jnp.concatenate(du_parts, axis=-1)
        for tap in range(CONV_WIDTH):
            ds_ref[tap:tap + 1, :] += jnp.sum(du * taps[tap], axis=0, keepdims=True)
        ds_ref[4:5, :] += jnp.sum(du, axis=0, keepdims=True)
        ds_ref[5:6, :] += jnp.sum(dpre_a, axis=0, keepdims=True)
        ds_ref[6:7, :] += jnp.sum(dpre_i, axis=0, keepdims=True)
        ds_ref[7:8, :] += dlam
        du_after = dunext_ref[...]
        du0 = jnp.zeros((tt, D), F32)
        for tap in range(CONV_WIDTH):
            du0 = du0 + _rows_up(du, du_after, CONV_WIDTH - 1 - tap) * small_ref[tap:tap + 1, :]
        dp_ref[:, :D] = du0.astype(BF16)
        dunext_ref[...] = du[0:SUBLANES, :]

    rev = lambda i: nt - 1 - i
    wspec = pl.BlockSpec((RNN_BLOCKS, W, W), lambda i: (0, 0, 0))
    return pl.pallas_call(
        body, name=name, grid=(nt,),
        in_specs=[pl.BlockSpec((tt, D), lambda i: (rev(i), 0)),
                  pl.BlockSpec((SUBLANES, D), lambda i: (jnp.maximum(rev(i) * hb - 1, 0), 0)),
                  pl.BlockSpec((tt, D), lambda i: (rev(i), 1)),
                  pl.BlockSpec((tt, D), lambda i: (rev(i), 0)),
                  pl.BlockSpec((SUBLANES, D), lambda i: (jnp.maximum(rev(i) * hb - 1, 0), 0)),
                  pl.BlockSpec((tt, D), lambda i: (rev(i), 0)),
                  pl.BlockSpec((SUBLANES, D), lambda i: (0, 0)),
                  wspec, wspec],
        out_specs=[pl.BlockSpec((tt, 2 * D), lambda i: (rev(i), 0)),
                   wspec, wspec, pl.BlockSpec((SUBLANES, D), lambda i: (0, 0))],
        out_shape=[jax.ShapeDtypeStruct((T, 2 * D), BF16),
                   jax.ShapeDtypeStruct((RNN_BLOCKS, W, W), F32),
                   jax.ShapeDtypeStruct((RNN_BLOCKS, W, W), F32),
                   jax.ShapeDtypeStruct((SUBLANES, D), F32)],
        scratch_shapes=[pltpu.VMEM((tt, D), F32), pltpu.VMEM((tt, D), F32),
                        pltpu.VMEM((SUBLANES, D), F32), pltpu.VMEM((1, D), F32)],
        compiler_params=_cparams("arbitrary"),
    )(proj, proj, proj, hs, hs, dy, small, wa, wi)


def _out_ln(a, w, x, g, b, *, tt, name):
    T, D = x.shape
    K = a.shape[1]

    def body(a_ref, w_ref, x_ref, g_ref, b_ref, y_ref, yb_ref, zh_ref, rs_ref):
        h = jnp.dot(a_ref[...].astype(BF16), w_ref[...].astype(BF16), preferred_element_type=F32)
        z = ALPHA * x_ref[...] + h
        mu = jnp.mean(z, axis=-1, keepdims=True)
        zc = z - mu
        rstd = lax.rsqrt(jnp.mean(zc * zc, axis=-1, keepdims=True) + LN_EPS)
        zh = zc * rstd
        zh_ref[...] = zh
        rs_ref[...] = rstd
        y = zh * g_ref[...] + b_ref[...]
        y_ref[...] = y
        yb_ref[...] = y.astype(BF16)

    blk = pl.BlockSpec((tt, D), lambda i: (i, 0))
    row = pl.BlockSpec((1, D), lambda i: (0, 0))
    return pl.pallas_call(
        body, name=name, grid=(T // tt,),
        in_specs=[pl.BlockSpec((tt, K), lambda i: (i, 0)), pl.BlockSpec((K, D), lambda i: (0, 0)),
                  blk, row, row],
        out_specs=[blk, blk, blk, pl.BlockSpec((tt, 1), lambda i: (i, 0))],
        out_shape=[jax.ShapeDtypeStruct((T, D), F32), jax.ShapeDtypeStruct((T, D), BF16),
                   jax.ShapeDtypeStruct((T, D), F32), jax.ShapeDtypeStruct((T, 1), F32)],
        compiler_params=_cparams("parallel"),
    )(a, w, x, g, b)


def _ln_bwd_tile(dy, zh_ref, rs_ref, g_ref, dz_ref, dzb_ref, dg_ref, db_ref, first):
    @pl.when(first)
    def _():
        dg_ref[...] = jnp.zeros_like(dg_ref)
        db_ref[...] = jnp.zeros_like(db_ref)

    zh = zh_ref[...]
    dg_ref[...] += jnp.sum(dy * zh, axis=0, keepdims=True)
    db_ref[...] += jnp.sum(dy, axis=0, keepdims=True)
    dzh = dy * g_ref[...]
    m1 = jnp.mean(dzh, axis=-1, keepdims=True)
    m2 = jnp.mean(dzh * zh, axis=-1, keepdims=True)
    dz = rs_ref[...] * (dzh - m1 - zh * m2)
    dz_ref[...] = dz
    dzb_ref[...] = dz.astype(BF16)


def _ln_bwd_specs(T, D, tt):
    blk = pl.BlockSpec((tt, D), lambda i: (i, 0))
    row = pl.BlockSpec((1, D), lambda i: (0, 0))
    return ([blk, pl.BlockSpec((tt, 1), lambda i: (i, 0)), row], [blk, blk, row, row],
            [jax.ShapeDtypeStruct((T, D), F32), jax.ShapeDtypeStruct((T, D), BF16),
             jax.ShapeDtypeStruct((1, D), F32), jax.ShapeDtypeStruct((1, D), F32)])


def _loss_ln_bwd(y, tgt, zh, rstd, g, *, tt, name):
    T, D = y.shape
    ln_in, ln_out, ln_shapes = _ln_bwd_specs(T, D, tt)

    def body(y_ref, t_ref, zh_ref, rs_ref, g_ref, l_ref, dz_ref, dzb_ref, dg_ref, db_ref):
        first = pl.program_id(0) == 0

        @pl.when(first)
        def _():
            l_ref[...] = jnp.zeros_like(l_ref)

        e = y_ref[...] - t_ref[...]
        l_ref[...] += jnp.sum(e * e, axis=0, keepdims=True) * (0.5 / D)
        _ln_bwd_tile(e * (1.0 / D), zh_ref, rs_ref, g_ref, dz_ref, dzb_ref, dg_ref, db_ref, first)

    blk = pl.BlockSpec((tt, D), lambda i: (i, 0))
    return pl.pallas_call(
        body, name=name, grid=(T // tt,),
        in_specs=[blk, blk] + ln_in,
        out_specs=[pl.BlockSpec((1, D), lambda i: (0, 0))] + ln_out,
        out_shape=[jax.ShapeDtypeStruct((1, D), F32)] + ln_shapes,
        compiler_params=_cparams("arbitrary"),
    )(y, tgt, zh, rstd, g)


def _dx_ln_bwd(a, b, add, zh, rstd, g, *, tm, name):
    T, D = add.shape
    na = len(a)
    K = sum(p.shape[1] for p in a)
    ln_in, ln_out, ln_shapes = _ln_bwd_specs(T, D, tm)

    def body(*refs):
        a_refs, b_ref, add_ref = refs[:na], refs[na], refs[na + 1]
        av = [r[...].astype(BF16) for r in a_refs]
        av = av[0] if na == 1 else jnp.concatenate(av, axis=1)
        dy = lax.dot_general(av, b_ref[...].astype(BF16), (((1,), (1,)), ((), ())),
                             preferred_element_type=F32) + ALPHA * add_ref[...]
        _ln_bwd_tile(dy, *refs[na + 2:], pl.program_id(0) == 0)

    return pl.pallas_call(
        body, name=name, grid=(T // tm,),
        in_specs=[pl.BlockSpec((tm, p.shape[1]), lambda i: (i, 0)) for p in a]
        + [pl.BlockSpec((D, K), lambda i: (0, 0)), pl.BlockSpec((tm, D), lambda i: (i, 0))] + ln_in,
        out_specs=ln_out, out_shape=ln_shapes,
        compiler_params=_cparams("arbitrary"),
    )(*a, b, add, zh, rstd, g)


def _row_tile(rows, target):
    best = SUBLANES
    for t in range(SUBLANES, target + 1, SUBLANES):
        if rows % t == 0:
            best = t
    return best


def _add_own(g, recv, c_idx, *, tr, name):
    _, M, R, C = g.shape

    def body(c_ref, g_ref, r_ref, o_ref, ob_ref):
        s = g_ref[0] + r_ref[...]
        o_ref[...] = s
        ob_ref[...] = s.astype(BF16)

    blk = pl.BlockSpec((1, tr, C), lambda k, i, c: (k, i, 0))
    return pl.pallas_call(
        body, name=name,
        grid_spec=pltpu.PrefetchScalarGridSpec(
            num_scalar_prefetch=1, grid=(M, R // tr),
            in_specs=[pl.BlockSpec((1, 1, tr, C), lambda k, i, c: (c[0], k, i, 0)), blk],
            out_specs=[blk, blk]),
        out_shape=[jax.ShapeDtypeStruct((M, R, C), F32), jax.ShapeDtypeStruct((M, R, C), BF16)],
        compiler_params=_cparams("parallel", "parallel"),
    )(c_idx, g, recv)


def _adamw_math(g, w_ref, m_ref, v_ref, g_ref, d_ref, nm_ref, nv_ref):
    nm = ADAM_B1 * m_ref[...] + (1.0 - ADAM_B1) * g
    nv = ADAM_B2 * v_ref[...] + (1.0 - ADAM_B2) * (g * g)
    m_hat = nm / (1.0 - ADAM_B1 ** ADAM_STEP)
    v_hat = nv / (1.0 - ADAM_B2 ** ADAM_STEP)
    g_ref[...] = g
    nm_ref[...] = nm
    nv_ref[...] = nv
    d_ref[...] = (-ADAM_LR) * (m_hat / (jnp.sqrt(v_hat) + ADAM_EPS) + ADAM_WD * w_ref[...])


def _adamw(parts, w, m, v, *, tr, name):
    n, R, C = parts.shape
    tr = min(tr, R)

    def body(p_ref, w_ref, m_ref, v_ref, *out_refs):
        g = p_ref[0]
        for k in range(1, n):
            g = g + p_ref[k]
        _adamw_math(g, w_ref, m_ref, v_ref, *out_refs)

    blk = pl.BlockSpec((tr, C), lambda i: (i, 0))
    out = jax.ShapeDtypeStruct((R, C), F32)
    return pl.pallas_call(
        body, name=name, grid=(R // tr,),
        in_specs=[pl.BlockSpec((n, tr, C), lambda i: (0, i, 0)), blk, blk, blk],
        out_specs=[blk, blk, blk, blk], out_shape=[out, out, out, out],
        compiler_params=_cparams("parallel"),
    )(parts, w, m, v)


def _adamw_shard(parts, place, w, m, v, *, idx, prev, tr, name):
    R, C = parts[0][0].shape[-2:]
    n_parts, n_prev = len(parts), 0 if prev is None else 4

    def body(place_ref, *refs):
        p_refs, (w_ref, m_ref, v_ref) = refs[:n_parts], refs[n_parts:n_parts + 3]
        g = None
        for r in p_refs:
            blk = r[(0,) * (len(r.shape) - 3)].astype(F32)
            g = blk if g is None else g + blk
        _adamw_math(g, w_ref, m_ref, v_ref, *refs[n_parts + 3 + n_prev:])

    blk = pl.BlockSpec((1, tr, C), lambda i, s: (idx, i, 0))

    def part_spec(a, pick):
        return pl.BlockSpec((1,) * (a.ndim - 2) + (tr, C), lambda i, s: (*pick(s), i, 0))

    out = jax.ShapeDtypeStruct(w.shape, F32)
    return pl.pallas_call(
        body, name=name,
        grid_spec=pltpu.PrefetchScalarGridSpec(
            num_scalar_prefetch=1, grid=(R // tr,),
            in_specs=[part_spec(a, pick) for a, pick in parts] + [blk, blk, blk]
            + [pl.BlockSpec(memory_space=pl.ANY)] * n_prev,
            out_specs=[blk, blk, blk, blk]),
        out_shape=[out, out, out, out],
        input_output_aliases={1 + n_parts + 3 + j: j for j in range(n_prev)},
        compiler_params=_cparams("parallel"),
    )(place, *[a for a, _ in parts], w, m, v, *(prev or ()))


def _two_stage_parts(h, recv):
    return [(h, lambda s: (s[0], 0))] + [(recv, lambda s, d=d: (s[0] ^ d, 0)) for d in (1, 2, 3)]


def _direct_parts(g, recv):
    return [(g, lambda s: (s[1], s[0]))] + [
        (recv, lambda s, a=p // 4, d=p % 4: (s[1] ^ a, s[0] ^ d)) for p in range(1, 8)]


SHARD_AXIS = dict(attn_w_in=1, attn_w_out=0, rnn_w_in=1, rnn_w_out=0, rnn_w_a=1, rnn_w_i=1,
                  rnn_conv_w=1, rnn_conv_b=0, rnn_b_a=0, rnn_b_i=0, rnn_lambda=0)
RNN_ROWED = ("rnn_w_out", "rnn_w_a", "rnn_w_i")
SMALL = ("rnn_conv_w", "rnn_conv_b", "rnn_b_a", "rnn_b_i", "rnn_lambda")
PACK_C = 1024


def _elems(shape):
    n = 1
    for s in shape:
        n *= s
    return n


def _pack_rows(p, idx, dtype):
    parts = [p[k][idx].astype(dtype).reshape(-1, PACK_C) for k in RNN_ROWED]
    small = jnp.concatenate([p[k][idx].reshape(-1) for k in SMALL])
    tile_rows = SUBLANES * (4 // jnp.dtype(dtype).itemsize)
    if dtype == BF16:
        small = lax.bitcast_convert_type(small, BF16)
    small = small.reshape(-1, PACK_C)
    parts.append(jnp.pad(small, ((0, tile_rows - small.shape[0]), (0, 0))))
    return jnp.concatenate(parts, axis=0)


def _unpack_rows(flat, shapes):
    out, r = {}, 0
    for k in RNN_ROWED:
        n = _elems(shapes[k]) // PACK_C
        out[k] = flat[r:r + n].reshape(shapes[k])
        r += n
    n_small = sum(_elems(shapes[k]) for k in SMALL)
    small = flat[r:r + n_small // PACK_C].reshape(-1)
    o = 0
    for k in SMALL:
        n = _elems(shapes[k])
        out[k] = small[o:o + n].reshape(shapes[k])
        o += n
    return out


def _join_columns(g, width, *, tr, name):
    _, _, R, S = g.shape

    def body(*refs):
        o_ref = refs[8]
        parts = [refs[r][0, 0].astype(F32) for r in range(8)]
        parts.append(jnp.zeros((tr, width - 8 * S), F32))
        o_ref[...] = jnp.concatenate(parts, axis=-1).astype(o_ref.dtype)

    def shard(r):
        return pl.BlockSpec((1, 1, tr, S), lambda i: (r % 2, r // 2, i, 0))

    return pl.pallas_call(
        body, name=name, grid=(R // tr,),
        in_specs=[shard(r) for r in range(8)],
        out_specs=pl.BlockSpec((tr, width), lambda i: (i, 0)),
        out_shape=jax.ShapeDtypeStruct((R, width), g.dtype),
        compiler_params=_cparams("parallel"),
    )(*([g] * 8))


def _split_columns(parts, S, *, tr, name):
    R = parts[0].shape[0]
    n = len(parts)

    def body(*refs):
        o_ref = refs[n]
        x = jnp.concatenate([r[...] for r in refs[:n]], axis=1)
        for r in range(8):
            o_ref[r % 2, r // 2] = x[:, r * S:(r + 1) * S]

    return pl.pallas_call(
        body, name=name, grid=(R // tr,),
        in_specs=[pl.BlockSpec((tr, p.shape[1]), lambda i: (i, 0)) for p in parts],
        out_specs=pl.BlockSpec((2, 4, tr, S), lambda i: (0, 0, i, 0)),
        out_shape=jax.ShapeDtypeStruct((2, 4, R, S), parts[0].dtype),
        compiler_params=_cparams("parallel"),
    )(*parts)


def _to_full(g, k, sh):
    ax, nd = SHARD_AXIS[k], len(sh)
    perm = tuple(range(2, 2 + ax)) + (1, 0) + tuple(range(2 + ax, 2 + nd))
    return g.transpose(perm).reshape(sh[:ax] + (8 * sh[ax],) + sh[ax + 1:])


def _from_full(full, k, sh):
    ax, nd = SHARD_AXIS[k], len(sh)
    t = full.reshape(sh[:ax] + (4, 2, sh[ax]) + sh[ax + 1:])
    return t.transpose((ax + 1, ax) + tuple(range(ax)) + tuple(range(ax + 2, nd + 2)))


def _unpack_gathered_rows(g, shapes):
    out, r = {}, 0
    for k in RNN_ROWED:
        n = _elems(shapes[k]) // PACK_C
        out[k] = _to_full(g[:, :, r:r + n].reshape((2, 4) + shapes[k]), k, shapes[k])
        r += n
    n_small = sum(_elems(shapes[k]) for k in SMALL)
    nr = 2 * n_small // PACK_C
    small = lax.bitcast_convert_type(g[:, :, r:r + nr].reshape(2, 4, n_small, 2), F32)
    o = 0
    for k in SMALL:
        n = _elems(shapes[k])
        out[k] = _to_full(small[:, :, o:o + n].reshape((2, 4) + shapes[k]), k, shapes[k])
        o += n
    return out


def _pack_grad_rows(full, shapes):
    parts = [_from_full(full[k], k, shapes[k]).reshape(2, 4, -1, PACK_C) for k in RNN_ROWED]
    small = jnp.concatenate(
        [_from_full(full[k], k, shapes[k]).reshape(2, 4, -1) for k in SMALL], axis=-1)
    small = small.reshape(2, 4, -1, PACK_C)
    parts.append(jnp.pad(small, ((0, 0), (0, 0), (0, SUBLANES - small.shape[2]), (0, 0))))
    return jnp.concatenate(parts, axis=2)


def kernel(x, ln_g, ln_b, attn_w_in, attn_b_f, attn_w_out, rnn_w_in, rnn_conv_w, rnn_conv_b, rnn_w_a, rnn_b_a, rnn_w_i, rnn_b_i, rnn_lambda, rnn_w_out, loss_target, m_ln_g, m_ln_b, m_attn_w_in, m_attn_b_f, m_attn_w_out, m_rnn_w_in, m_rnn_conv_w, m_rnn_conv_b, m_rnn_w_a, m_rnn_b_a, m_rnn_w_i, m_rnn_b_i, m_rnn_lambda, m_rnn_w_out, v_ln_g, v_ln_b, v_attn_w_in, v_attn_b_f, v_attn_w_out, v_rnn_w_in, v_rnn_conv_w, v_rnn_conv_b, v_rnn_w_a, v_rnn_b_a, v_rnn_w_i, v_rnn_b_i, v_rnn_lambda, v_rnn_w_out):
    w_loc = dict(attn_w_in=attn_w_in, attn_w_out=attn_w_out, rnn_w_in=rnn_w_in, rnn_w_a=rnn_w_a,
                 rnn_w_i=rnn_w_i, rnn_w_out=rnn_w_out, rnn_conv_w=rnn_conv_w, rnn_conv_b=rnn_conv_b,
                 rnn_b_a=rnn_b_a, rnn_b_i=rnn_b_i, rnn_lambda=rnn_lambda)
    m_loc = dict(attn_w_in=m_attn_w_in, attn_w_out=m_attn_w_out, rnn_w_in=m_rnn_w_in,
                 rnn_w_a=m_rnn_w_a, rnn_w_i=m_rnn_w_i, rnn_w_out=m_rnn_w_out,
                 rnn_conv_w=m_rnn_conv_w, rnn_conv_b=m_rnn_conv_b, rnn_b_a=m_rnn_b_a,
                 rnn_b_i=m_rnn_b_i, rnn_lambda=m_rnn_lambda)
    v_loc = dict(attn_w_in=v_attn_w_in, attn_w_out=v_attn_w_out, rnn_w_in=v_rnn_w_in,
                 rnn_w_a=v_rnn_w_a, rnn_w_i=v_rnn_w_i, rnn_w_out=v_rnn_w_out,
                 rnn_conv_w=v_rnn_conv_w, rnn_conv_b=v_rnn_conv_b, rnn_b_a=v_rnn_b_a,
                 rnn_b_i=v_rnn_b_i, rnn_lambda=v_rnn_lambda)
    shapes = {k: tuple(a.shape[1:]) for k, a in w_loc.items()}
    T, D = x.shape[1], x.shape[2]
    n_f = attn_b_f.shape[1]
    tb = min(1024, T)
    tb_bwd = min(512, T)
    tt_rg = min(128, T)
    tt_ln = min(256, T)
    c_idx = lax.axis_index("c").astype(jnp.int32).reshape(1)
    me_idx = (2 * lax.axis_index("x") + lax.axis_index("y")).astype(jnp.int32).reshape(1)
    place = jnp.concatenate([me_idx, c_idx])

    def attn_w_in_full(g_in, idx):
        return _join_columns(g_in, 4 * D + LANES, tr=256, name=f"a_join{idx}")

    def attn_w_out_full(g_out):
        return _to_full(g_out, "attn_w_out", shapes["attn_w_out"])

    def rnn_weights(g_in, g_rows):
        w = _unpack_gathered_rows(g_rows, shapes)
        w["rnn_w_in"] = _to_full(g_in, "rnn_w_in", shapes["rnn_w_in"])
        w["small"] = jnp.concatenate([w["rnn_conv_w"], w["rnn_conv_b"][None], w["rnn_b_a"][None],
                                      w["rnn_b_i"][None], w["rnn_lambda"][None]])
        return w

    g0 = _ag_c(_run_exchange(_Exchange("gather", [attn_w_in[0].astype(BF16)]), "ag_w0_xy"),
               "ag_w0_c")
    later = _Exchange("gather8", [
        attn_w_out.astype(BF16), attn_w_in[1].astype(BF16), rnn_w_in.astype(BF16),
        jnp.stack([_pack_rows(w_loc, i, BF16) for i in range(2)])])
    w_attn_in, w_attn_out, w_rnn = [attn_w_in_full(g0[0], 0), None], [None, None], [None, None]
    bf_rows = jnp.pad(attn_b_f, ((0, 0), (0, LANES - n_f)))[:, None, :]

    xs, xb, saved = [x[0]], [x[0]], []
    for layer in range(DEPTH):
        idx, xl, xm = layer // 2, xs[-1], xb[-1]
        if layer % 2 == 0:
            proj = _matmul(xm, w_attn_in[idx], trans_b=False, tm=512, tn=1408,
                           name=f"a_proj{layer}")
            cum_t = _cumsum_fwd(proj, bf_rows[idx], tt=min(512, T), name=f"a_cum{layer}")
            cum2 = cum_t[:N_HEADS].reshape(N_PAIRS, 2, T)
            o, og, lp, *got = _flash_fwd(proj, cum2.reshape(N_PAIRS, 2, T // tb, tb), tb=tb,
                                         name=f"a_fwd{layer}", host=later if layer == 0 else None)
            cum4 = cum2.reshape(N_PAIRS, 2, T // tb_bwd, tb_bwd)
            if layer == 0:
                g1 = got
                w_attn_out = [attn_w_out_full(g1[0][:, :, i]) for i in range(2)]
                w_attn_in[1] = attn_w_in_full(g1[1], 1)
                w_rnn = [rnn_weights(g1[2][:, :, i], g1[3][:, :, i]) for i in range(2)]
            branch, w_out = og, w_attn_out[idx]
            saved.append((proj, cum4, o, og, lp))
        else:
            w = w_rnn[idx]
            proj = _matmul(xm, w["rnn_w_in"], trans_b=False, tm=512, tn=1024,
                           name=f"r_proj{layer}")
            hs, yr = _rg_fwd(proj, w["small"], w["rnn_w_a"], w["rnn_w_i"], tt=tt_rg,
                             name=f"r_fwd{layer}")
            branch, w_out = yr, w["rnn_w_out"]
            saved.append((proj, hs, yr))
        y, yb, zh, rstd = _out_ln(branch, w_out, xl, ln_g[layer][None], ln_b[layer][None],
                                  tt=512, name=f"out_ln{layer}")
        saved[-1] = saved[-1] + (zh, rstd)
        xs.append(y)
        xb.append(yb)

    def ln_below(layer):
        return saved[layer][-2:] + (ln_g[layer][None],)

    loss_lanes, *ln_grads = _loss_ln_bwd(xs[-1], loss_target[0], *ln_below(DEPTH - 1), tt=tt_ln,
                                         name="loss_ln_bwd")
    loss = lax.psum(jnp.sum(loss_lanes), ("x", "y", "c"))

    def reduce_pair(gs, layer):
        recv = _rs_c(gs, f"rs_c{layer}")
        outs = [_add_own(g, r, c_idx, tr=_row_tile(g.shape[2], 512), name=f"rs_add{layer}_{n}")
                for n, (g, r) in enumerate(zip(gs, recv))]
        return [o[0][:, None] for o in outs], [o[1][:, None] for o in outs]

    part, got_parts = [None] * DEPTH, [None] * DEPTH
    d_ln_g, d_ln_b, d_bf = [None] * DEPTH, [None] * DEPTH, [None, None]
    for layer in reversed(range(DEPTH)):
        idx, xm = layer // 2, xb[layer]
        dz, dzb, dg, db = ln_grads
        d_ln_g[layer], d_ln_b[layer] = dg[0], db[0]
        if layer % 2 == 0:
            w_in, w_out = w_attn_in[idx], w_attn_out[idx]
            proj, cum4, o, og, lp = saved[layer][:5]
            dog = _matmul(dzb, w_out, trans_b=True, tm=512, tn=1024, name=f"a_dog{layer}")
            dwo = _matmul_tn(og, dzb, tm=512, tn=1024, tk=1024, name=f"a_dwo{layer}")
            riders = [l for l in range(layer + 1, DEPTH) if got_parts[l] is None]
            host = _Exchange("scatter8", [g for l in riders for g in part[l]]) if riders else None
            dq, dgate, dk, dv, dcum_q, dcum_k, *got = _flash_bwd(proj, cum4, o, dog, lp, tb=tb_bwd,
                                                                 name=f"a_bwd{layer}", host=host)
            for l in riders:
                got_parts[l], got = got[:len(part[l])], got[len(part[l]):]
            dcum_t = (dcum_q.transpose(0, 2, 1, 3) + dcum_k).reshape(N_HEADS, T)
            dcum_t = jnp.pad(dcum_t, ((0, LANES - N_HEADS), (0, 0)))
            df, dbf = _cumsum_bwd(dcum_t, proj, bf_rows[idx], tt=min(512, T), name=f"a_dcum{layer}")
            d_bf[idx] = dbf[0, :n_f]
            dproj = [dq, dk, dv, dgate, df]
            dwi = _matmul_tn_parts(xm, dproj, tm=512, tk=1024, name=f"a_dwi{layer}")
            gs = [_split_columns(dwi, shapes["attn_w_in"][1], tr=256, name=f"a_split{layer}"),
                  _from_full(dwo, "attn_w_out", shapes["attn_w_out"])]
            if layer > 0:
                part[layer] = gs
                ln_grads = _dx_ln_bwd(dproj, w_in, dz, *ln_below(layer - 1), tm=256,
                                      name=f"a_dx{layer}")
            else:
                part[layer], narrow = reduce_pair(gs, layer)
                dy, *got_parts[layer] = _matmul(dproj, w_in, trans_b=True, tm=512, tn=1024,
                                                name=f"a_dx{layer}", add=dz, add_scale=ALPHA,
                                                host=_Exchange("scatter", narrow))
        else:
            w = w_rnn[idx]
            proj, hs, yr = saved[layer][:3]
            dyr = _matmul(dzb, w["rnn_w_out"], trans_b=True, tm=512, tn=1024, name=f"r_dy{layer}")
            dwo = _matmul_tn(yr, dzb, tm=512, tn=1024, tk=1024, name=f"r_dwo{layer}")
            dproj, dwa, dwi_, dsm = _rg_bwd(proj, hs, dyr, w["small"], w["rnn_w_a"], w["rnn_w_i"],
                                            tt=tt_rg, name=f"r_bwd{layer}")
            dwin = _matmul_tn(xm, dproj, tm=512, tn=2048, tk=1024, name=f"r_dwi{layer}")
            ln_grads = _dx_ln_bwd([dproj], w["rnn_w_in"], dz, *ln_below(layer - 1), tm=512,
                                  name=f"r_dx{layer}")
            full = dict(rnn_w_out=dwo, rnn_w_a=dwa, rnn_w_i=dwi_, rnn_conv_w=dsm[0:4],
                        rnn_conv_b=dsm[4], rnn_b_a=dsm[5], rnn_b_i=dsm[6], rnn_lambda=dsm[7])
            part[layer] = [_from_full(dwin, "rnn_w_in", shapes["rnn_w_in"]),
                           _pack_grad_rows(full, shapes)]
    grad_x = dy[None]

    def grad_parts(layer, n):
        make = _two_stage_parts if layer == 0 else _direct_parts
        return make(part[layer][n], got_parts[layer][n])

    def update(k, n):
        res = None
        for idx in (1, 0):
            layer = 2 * idx + (0 if k.startswith("attn") else 1)
            res = _adamw_shard(grad_parts(layer, n), place, w_loc[k], m_loc[k], v_loc[k],
                               idx=idx, prev=res, tr=_row_tile(shapes[k][0], 256),
                               name=f"adamw_{k}{idx}")
        return res

    shard_outs = [dict() for _ in range(4)]
    for k, n in (("attn_w_in", 0), ("attn_w_out", 1), ("rnn_w_in", 0)):
        for j, a in enumerate(update(k, n)):
            shard_outs[j][k] = a
    rows = []
    for idx in range(2):
        layer = 2 * idx + 1
        wmv = [_pack_rows(d, idx, F32)[None] for d in (w_loc, m_loc, v_loc)]
        res = _adamw_shard(grad_parts(layer, 1), place, *wmv, idx=0, prev=None,
                           tr=_row_tile(wmv[0].shape[1], 256), name=f"adamw_rows{idx}")
        rows.append([_unpack_rows(a[0], shapes) for a in res])
    for j in range(4):
        for k in RNN_ROWED + SMALL:
            shard_outs[j][k] = jnp.stack([rows[0][j][k], rows[1][j][k]])
    g_sh, d_sh, nm_sh, nv_sh = shard_outs

    def rep_pack(lg, lb, bf):
        rows = jnp.concatenate([lg, lb, jnp.pad(bf.reshape(1, -1), ((0, 0), (0, D - 2 * n_f)))])
        return jnp.pad(rows, ((0, 16 - rows.shape[0]), (0, 0)))

    rep = _all_gather(rep_pack(jnp.stack(d_ln_g), jnp.stack(d_ln_b), jnp.stack(d_bf)), "ag_rep")
    rg, rd, rm, rv = _adamw(rep.reshape(8, 16, D), rep_pack(ln_g, ln_b, attn_b_f),
                            rep_pack(m_ln_g, m_ln_b, m_attn_b_f),
                            rep_pack(v_ln_g, v_ln_b, v_attn_b_f), tr=16, name="adamw_rep")

    def rep_unpack(a):
        return dict(ln_g=a[0:DEPTH], ln_b=a[DEPTH:2 * DEPTH],
                    attn_b_f=a[2 * DEPTH, :2 * n_f].reshape(2, n_f))

    order = ("ln_g", "ln_b", "attn_w_in", "attn_b_f", "attn_w_out", "rnn_w_in", "rnn_conv_w",
             "rnn_conv_b", "rnn_w_a", "rnn_b_a", "rnn_w_i", "rnn_b_i", "rnn_lambda", "rnn_w_out")
    outs = [loss, grad_x]
    for sh, rp in ((g_sh, rg), (d_sh, rd), (nm_sh, rm), (nv_sh, rv)):
        allp = {**sh, **rep_unpack(rp)}
        outs.extend(allp[k] for k in order)
    return tuple(outs)
```

```python
import jax
import jax.numpy as jnp
from jax import lax
from jax.experimental import pallas as pl
from jax.experimental.pallas import tpu as pltpu

F32 = jnp.float32
BF16 = jnp.bfloat16

DEPTH = 4
N_HEADS = 16
HEAD_DIM = 64
N_PAIRS = N_HEADS // 2
RNN_BLOCKS = 4
RNN_BLOCK_WIDTH = 256
CONV_WIDTH = 4
LRU_C = 8.0
ALPHA = (2.0 * DEPTH) ** 0.25
LN_EPS = 1e-5
ADAM_LR, ADAM_B1, ADAM_B2, ADAM_EPS, ADAM_WD, ADAM_STEP = 0.001, 0.9, 0.999, 1e-8, 0.01, 10

LANES = 128
SUBLANES = 8
VMEM_LIMIT = 48 * 1024 * 1024

MESH = pl.DeviceIdType.MESH
HBM_SPEC = pl.BlockSpec(memory_space=pltpu.HBM)


def _cparams(*sem):
    return pltpu.CompilerParams(dimension_semantics=sem, vmem_limit_bytes=VMEM_LIMIT)


def _sigmoid(x):
    return 1.0 / (1.0 + jnp.exp(-x))


def _softplus(x):
    return jnp.maximum(x, 0.0) + jnp.log(1.0 + jnp.exp(-jnp.abs(x)))


def _a2a(src, *, group, bcast, name):
    n = 2 if group == "c" else 4
    blk = tuple(src.shape) if bcast else tuple(src.shape[1:])

    def body(src_ref, out_ref, send_sems, recv_sems, local_sem):
        x, y, c = lax.axis_index("x"), lax.axis_index("y"), lax.axis_index("c")
        if group == "c":
            me = c

            def peer(d):
                return (x, y, 1 - c), 1 - c
        else:
            me = 2 * x + y

            def peer(d):
                px, py = x ^ (d >> 1), y ^ (d & 1)
                return (px, py, c), 2 * px + py

        def block_for(k):
            return src_ref if bcast else src_ref.at[k]

        local = pltpu.make_async_copy(block_for(me), out_ref.at[me], local_sem)
        local.start()
        sends = []
        for d in range(1, n):
            dev, idx = peer(d)
            cp = pltpu.make_async_remote_copy(
                src_ref=block_for(idx), dst_ref=out_ref.at[me],
                send_sem=send_sems.at[d], recv_sem=recv_sems.at[d],
                device_id=dev, device_id_type=MESH)
            cp.start()
            sends.append(cp)
        for d in range(1, n):
            dev, idx = peer(d)
            pltpu.make_async_remote_copy(
                src_ref=block_for(idx), dst_ref=out_ref.at[idx],
                send_sem=send_sems.at[d], recv_sem=recv_sems.at[d],
                device_id=dev, device_id_type=MESH).wait_recv()
        for cp in sends:
            cp.wait_send()
        local.wait()

    return pl.pallas_call(
        body, name=name,
        out_shape=jax.ShapeDtypeStruct((n,) + blk, src.dtype),
        in_specs=[HBM_SPEC], out_specs=HBM_SPEC,
        scratch_shapes=[pltpu.SemaphoreType.DMA((n,)), pltpu.SemaphoreType.DMA((n,)),
                        pltpu.SemaphoreType.DMA],
    )(src)


def _all_gather(piece, name):
    return _a2a(_a2a(piece, group="xy", bcast=True, name=name + "_xy"),
                group="c", bcast=True, name=name + "_c")


D2D_CHUNKS = 16
ICI_CHUNKS = 8


def _row_chunks(rows, dtype, k):
    unit = SUBLANES * (4 // jnp.dtype(dtype).itemsize)
    assert rows % unit == 0
    units = rows // unit
    k = max(1, min(k, units))
    base, rem = divmod(units, k)
    out, r = [], 0
    for i in range(k):
        n = (base + (1 if i < rem else 0)) * unit
        out.append((r, n))
        r += n
    return out


def _chunks(shape, dtype, k):
    if len(shape) == 2:
        return [(pl.ds(r0, n),) for r0, n in _row_chunks(shape[0], dtype, k)]
    per = max(1, k // shape[0])
    return [(l, pl.ds(r0, n)) for l in range(shape[0]) for r0, n in _row_chunks(shape[1], dtype, per)]


def _mesh_place():
    x, y, c = lax.axis_index("x"), lax.axis_index("y"), lax.axis_index("c")
    return x, y, c, 2 * x + y


def _chip_peer(x, y, c, d):
    px, py = x ^ (d >> 1), y ^ (d & 1)
    return (px, py, c), 2 * px + py


def _remote(src, dst, send_sem, recv_sem, dev):
    return pltpu.make_async_remote_copy(src_ref=src, dst_ref=dst, send_sem=send_sem,
                                        recv_sem=recv_sem, device_id=dev, device_id_type=MESH)


def _comm_call(body, name, ins, out_shapes, n_sems, aliases=None):
    n = len(ins)
    return pl.pallas_call(
        body, name=name,
        out_shape=out_shapes, in_specs=[HBM_SPEC] * n, out_specs=[HBM_SPEC] * n,
        input_output_aliases=aliases or {},
        scratch_shapes=[pltpu.SemaphoreType.DMA((n_sems, n)), pltpu.SemaphoreType.DMA((n_sems, n))],
    )(*ins)


class _Exchange:
    def __init__(self, kind, arrays):
        self.kind, self.arrays, self.n = kind, list(arrays), len(arrays)
        self.is_gather, self.all8 = kind.startswith("gather"), kind.endswith("8")
        k = ICI_CHUNKS // 4 if self.all8 else ICI_CHUNKS
        if self.is_gather:
            self.chunks = [_chunks(a.shape, a.dtype, k) for a in arrays]
            self.out_shapes = [jax.ShapeDtypeStruct((2, 4) + tuple(a.shape), a.dtype) for a in arrays]
        else:
            lead = 2 if self.all8 else 1
            self.chunks = [_chunks(a.shape[lead:], a.dtype, k) for a in arrays]
            self.out_shapes = [jax.ShapeDtypeStruct(a.shape, a.dtype) for a in arrays]
        self.peers = list(range(1, 8 if self.all8 else 4))
        n_sems = len(self.peers) + 1
        self.sem_shapes = [pltpu.SemaphoreType.DMA((n_sems, self.n)),
                           pltpu.SemaphoreType.DMA((n_sems, self.n))]

    def _peer(self, x, y, c, me, p):
        a, d = p // 4, p % 4
        px, py = x ^ (d >> 1), y ^ (d & 1)
        pc = 1 - c if a else c
        if self.all8:
            return (px, py, pc), (pc, 2 * px + py), (c, me)
        return (px, py, pc), (2 * px + py,), (me,)

    def _blocks(self, srcs, outs, o, c, me, theirs, mine):
        if self.kind == "gather":
            return srcs[o], outs[o].at[(c,) + mine], outs[o].at[(c,) + theirs]
        if self.kind == "gather8":
            return srcs[o], outs[o].at[mine], outs[o].at[theirs]
        return srcs[o].at[theirs], outs[o].at[mine], outs[o].at[theirs]

    def start(self, srcs, outs, send_sems, recv_sems):
        x, y, c, me = _mesh_place()
        if self.is_gather:
            for o in range(self.n):
                for idx in self.chunks[o]:
                    pltpu.make_async_copy(srcs[o].at[idx], outs[o].at[(c, me) + idx],
                                          send_sems.at[0, o]).start()
        for p in self.peers:
            dev, theirs, mine = self._peer(x, y, c, me, p)
            for o in range(self.n):
                src, dst, _ = self._blocks(srcs, outs, o, c, me, theirs, mine)
                for idx in self.chunks[o]:
                    _remote(src.at[idx], dst.at[idx], send_sems.at[p, o], recv_sems.at[p, o],
                            dev).start()

    def wait(self, srcs, outs, send_sems, recv_sems):
        x, y, c, me = _mesh_place()
        for wait_recv in (True, False):
            for p in self.peers:
                dev, theirs, mine = self._peer(x, y, c, me, p)
                for o in range(self.n):
                    src, _, land = self._blocks(srcs, outs, o, c, me, theirs, mine)
                    cp = _remote(src, land, send_sems.at[p, o], recv_sems.at[p, o], dev)
                    cp.wait_recv() if wait_recv else cp.wait_send()
        if self.is_gather:
            for o in range(self.n):
                pltpu.make_async_copy(srcs[o], outs[o].at[c, me], send_sems.at[0, o]).wait()


def _run_exchange(ex, name):
    n = ex.n

    def body(*refs):
        srcs, outs, send_sems, recv_sems = refs[:n], refs[n:2 * n], refs[2 * n], refs[2 * n + 1]
        ex.start(srcs, outs, send_sems, recv_sems)
        ex.wait(srcs, outs, send_sems, recv_sems)

    return _comm_call(body, name, ex.arrays, ex.out_shapes, len(ex.peers) + 1)


def _ag_c(bufs, name):
    n = len(bufs)
    chunks = [_chunks(b.shape[2:], b.dtype, D2D_CHUNKS // 4) for b in bufs]

    def body(*refs):
        srcs, outs, send_sems, recv_sems = refs[:n], refs[n:2 * n], refs[2 * n], refs[2 * n + 1]
        x, y, c, _ = _mesh_place()
        sib = (x, y, 1 - c)
        for o in range(n):
            for k in range(4):
                for idx in chunks[o]:
                    _remote(srcs[o].at[(c, k) + idx], outs[o].at[(c, k) + idx],
                            send_sems.at[0, o], recv_sems.at[0, o], sib).start()
        for o in range(n):
            _remote(srcs[o].at[c], outs[o].at[1 - c], send_sems.at[0, o], recv_sems.at[0, o],
                    sib).wait_recv()
        for o in range(n):
            _remote(srcs[o].at[c], outs[o].at[1 - c], send_sems.at[0, o], recv_sems.at[0, o],
                    sib).wait_send()

    shapes = [jax.ShapeDtypeStruct(b.shape, b.dtype) for b in bufs]
    return _comm_call(body, name, bufs, shapes, 1, aliases={i: i for i in range(n)})


def _rs_c(gs, name):
    n = len(gs)
    chunks = [_chunks(g.shape[2:], g.dtype, max(1, D2D_CHUNKS // g.shape[1])) for g in gs]

    def body(*refs):
        srcs, outs, send_sems, recv_sems = refs[:n], refs[n:2 * n], refs[2 * n], refs[2 * n + 1]
        x, y, c, _ = _mesh_place()
        sib = (x, y, 1 - c)
        for o in range(n):
            for k in range(gs[o].shape[1]):
                for idx in chunks[o]:
                    _remote(srcs[o].at[(1 - c, k) + idx], outs[o].at[(k,) + idx],
                            send_sems.at[0, o], recv_sems.at[0, o], sib).start()
        for o in range(n):
            _remote(srcs[o].at[1 - c], outs[o], send_sems.at[0, o], recv_sems.at[0, o],
                    sib).wait_recv()
        for o in range(n):
            _remote(srcs[o].at[1 - c], outs[o], send_sems.at[0, o], recv_sems.at[0, o],
                    sib).wait_send()

    shapes = [jax.ShapeDtypeStruct(g.shape[1:], g.dtype) for g in gs]
    return _comm_call(body, name, gs, shapes, 1)


def _matmul(a, b, *, trans_b, tm, tn, name, add=None, add_scale=1.0, host=None):
    a_parts = list(a) if isinstance(a, (list, tuple)) else [a]
    M, K = a_parts[0].shape[0], sum(p.shape[1] for p in a_parts)
    N = b.shape[0] if trans_b else b.shape[1]
    tm, tn = min(tm, M), min(tn, N)
    assert M % tm == 0 and N % tn == 0
    dn = (((1,), (1,)), ((), ())) if trans_b else (((1,), (0,)), ((), ()))
    na = len(a_parts)

    def body(*refs):
        a_refs, b_ref, o_ref = refs[:na], refs[na], refs[-1]
        av = [r[...].astype(BF16) for r in a_refs]
        av = av[0] if na == 1 else jnp.concatenate(av, axis=1)
        r = lax.dot_general(av, b_ref[...].astype(BF16), dn, preferred_element_type=F32)
        if add is not None:
            r = r + add_scale * refs[na + 1][...]
        o_ref[...] = r

    b_spec = (pl.BlockSpec((tn, K), lambda j, i: (j, 0)) if trans_b
              else pl.BlockSpec((K, tn), lambda j, i: (0, j)))
    in_specs = [pl.BlockSpec((tm, p.shape[1]), lambda j, i: (i, 0)) for p in a_parts] + [b_spec]
    args = a_parts + [b]
    if add is not None:
        in_specs.append(pl.BlockSpec((tm, tn), lambda j, i: (i, j)))
        args.append(add)
    grid = (N // tn, M // tm)
    x_in, x_out, x_shapes, x_scratch, x_args = _host_specs(host)
    body = _hosted(body, len(args), 1, 0, host, grid)
    outs = pl.pallas_call(
        body, name=name, grid=grid,
        in_specs=in_specs + x_in,
        out_specs=[pl.BlockSpec((tm, tn), lambda j, i: (i, j))] + x_out,
        out_shape=[jax.ShapeDtypeStruct((M, N), F32)] + x_shapes,
        scratch_shapes=x_scratch,
        compiler_params=_cparams(*(("arbitrary",) * 2 if host else ("parallel",) * 2)),
    )(*args, *x_args)
    return outs if host else outs[0]


def _matmul_tn(a, b, *, tm, tn, tk, name):
    T, M = a.shape
    N = b.shape[1]
    tm, tn, tk = min(tm, M), min(tn, N), min(tk, T)
    assert M % tm == 0 and N % tn == 0 and T % tk == 0

    def body(a_ref, b_ref, o_ref):
        @pl.when(pl.program_id(2) == 0)
        def _():
            o_ref[...] = jnp.zeros_like(o_ref)

        o_ref[...] += lax.dot_general(a_ref[...].astype(BF16), b_ref[...].astype(BF16),
                                      (((0,), (0,)), ((), ())), preferred_element_type=F32)

    return pl.pallas_call(
        body, name=name, grid=(M // tm, N // tn, T // tk),
        in_specs=[pl.BlockSpec((tk, tm), lambda i, j, k: (k, i)),
                  pl.BlockSpec((tk, tn), lambda i, j, k: (k, j))],
        out_specs=pl.BlockSpec((tm, tn), lambda i, j, k: (i, j)),
        out_shape=jax.ShapeDtypeStruct((M, N), F32),
        compiler_params=_cparams("parallel", "parallel", "arbitrary"),
    )(a, b)


def _matmul_tn_parts(a, parts, *, tm, tk, name):
    T, M = a.shape
    tm, tk = min(tm, M), min(tk, T)
    assert M % tm == 0 and T % tk == 0
    n = len(parts)

    def body(*refs):
        a_ref, b_refs, o_refs = refs[0], refs[1:1 + n], refs[1 + n:]
        av = a_ref[...].astype(BF16)
        for b_ref, o_ref in zip(b_refs, o_refs):
            @pl.when(pl.program_id(1) == 0)
            def _(o_ref=o_ref):
                o_ref[...] = jnp.zeros_like(o_ref)

            o_ref[...] += lax.dot_general(av, b_ref[...].astype(BF16), (((0,), (0,)), ((), ())),
                                          preferred_element_type=F32)

    return pl.pallas_call(
        body, name=name, grid=(M // tm, T // tk),
        in_specs=[pl.BlockSpec((tk, tm), lambda i, k: (k, i))]
        + [pl.BlockSpec((tk, p.shape[1]), lambda i, k: (k, 0)) for p in parts],
        out_specs=[pl.BlockSpec((tm, p.shape[1]), lambda i, k: (i, 0)) for p in parts],
        out_shape=[jax.ShapeDtypeStruct((M, p.shape[1]), F32) for p in parts],
        compiler_params=_cparams("parallel", "arbitrary"),
    )(a, *parts)


def _head_masks(rows):
    lane = lax.broadcasted_iota(jnp.int32, (rows, LANES), 1)
    return lane < HEAD_DIM, lane >= HEAD_DIM


def _causal(i_q, i_k, tq, tk):
    row = i_q * tq + lax.broadcasted_iota(jnp.int32, (tq, tk), 0)
    col = i_k * tk + lax.broadcasted_iota(jnp.int32, (tq, tk), 1)
    return row >= col


def _hosted(body, n_in, n_out, n_scratch, host, grid):
    if host is None:
        return body
    nx = host.n

    def wrapped(*refs):
        ins, xsrcs = refs[:n_in], refs[n_in:n_in + nx]
        outs = refs[n_in + nx:n_in + nx + n_out]
        xouts = refs[n_in + nx + n_out:n_in + 2 * nx + n_out]
        scratch = refs[n_in + 2 * nx + n_out:n_in + 2 * nx + n_out + n_scratch]
        xsems = refs[n_in + 2 * nx + n_out + n_scratch:]
        step = pl.program_id(0) * grid[1] + pl.program_id(1)

        @pl.when(step == 0)
        def _():
            host.start(xsrcs, xouts, *xsems)

        body(*ins, *outs, *scratch)

        @pl.when(step == grid[0] * grid[1] - 1)
        def _():
            host.wait(xsrcs, xouts, *xsems)

    return wrapped


def _host_specs(host):
    if host is None:
        return [], [], [], [], []
    return ([HBM_SPEC] * host.n, [HBM_SPEC] * host.n, host.out_shapes, host.sem_shapes, host.arrays)


def _flash_fwd(proj, cum4, *, tb, name, host=None):
    T = proj.shape[0]
    D = N_HEADS * HEAD_DIM
    nb = T // tb
    cb = D // LANES
    x_in, x_out, x_shapes, x_scratch, x_args = _host_specs(host)

    def body(q_ref, k_ref, v_ref, g_ref, cum_ref, o_ref, og_ref, lp_ref, kb_ref, vb_ref):
        i = pl.program_id(1)

        @pl.when(i == 0)
        def _():
            kb_ref[...] = k_ref[...].astype(BF16)
            vb_ref[...] = v_ref[...].astype(BF16)

        q = q_ref[...] * (HEAD_DIM ** -0.5)
        masks = _head_masks(tb)
        qh = [jnp.where(masks[h], q, 0.0).astype(BF16) for h in range(2)]
        cref = [cum_ref[0, h, pl.ds(i, 1), :][:, 0:1] for h in range(2)]

        def step(kbi, carry, masked):
            k0 = pl.multiple_of(kbi * tb, tb)
            kblk = kb_ref[pl.ds(k0, tb), :]
            vblk = vb_ref[pl.ds(k0, tb), :]
            new = []
            for h in range(2):
                m, l, acc = carry[h]
                s = lax.dot_general(qh[h], kblk, (((1,), (1,)), ((), ())),
                                    preferred_element_type=F32)
                s = s + (cref[h] - cum_ref[0, h, pl.ds(kbi, 1), :])
                if masked:
                    s = jnp.where(_causal(i, kbi, tb, tb), s, -jnp.inf)
                m_new = jnp.maximum(m, jnp.max(s, axis=-1, keepdims=True))
                alpha = jnp.exp(m - m_new)
                p = jnp.exp(s - m_new)
                l = alpha * l + jnp.sum(p, axis=-1, keepdims=True)
                acc = alpha * acc + jnp.dot(p.astype(BF16), vblk, preferred_element_type=F32)
                new.append((m_new, l, acc))
            return tuple(new)

        init1 = (jnp.full((tb, 1), -jnp.inf, F32), jnp.zeros((tb, 1), F32),
                 jnp.zeros((tb, LANES), F32))
        carry = lax.fori_loop(0, i, lambda kbi, c: step(kbi, c, False), (init1, init1))
        outs = []
        for h, (m, l, acc) in enumerate(step(i, carry, True)):
            outs.append(acc / l)
            lp_ref[h] = jnp.broadcast_to(m + jnp.log(l) - cref[h], (tb, LANES))
        o = jnp.where(masks[0], outs[0], outs[1])
        o_ref[...] = o
        gate = g_ref[...]
        og_ref[...] = (o * (gate * _sigmoid(gate))).astype(BF16)

    body = _hosted(body, 5, 3, 2, host, (N_PAIRS, nb))
    return pl.pallas_call(
        body, name=name, grid=(N_PAIRS, nb),
        in_specs=[pl.BlockSpec((tb, LANES), lambda j, i: (i, j)),
                  pl.BlockSpec((T, LANES), lambda j, i: (0, cb + j)),
                  pl.BlockSpec((T, LANES), lambda j, i: (0, 2 * cb + j)),
                  pl.BlockSpec((tb, LANES), lambda j, i: (i, 3 * cb + j)),
                  pl.BlockSpec((1, 2, nb, tb), lambda j, i: (j, 0, 0, 0))] + x_in,
        out_specs=[pl.BlockSpec((tb, LANES), lambda j, i: (i, j)),
                   pl.BlockSpec((tb, LANES), lambda j, i: (i, j)),
                   pl.BlockSpec((2, tb, LANES), lambda j, i: (j, i, 0))] + x_out,
        out_shape=[jax.ShapeDtypeStruct((T, D), F32), jax.ShapeDtypeStruct((T, D), BF16),
                   jax.ShapeDtypeStruct((N_HEADS, T, LANES), F32)] + x_shapes,
        scratch_shapes=[pltpu.VMEM((T, LANES), BF16), pltpu.VMEM((T, LANES), BF16)] + x_scratch,
        compiler_params=_cparams("arbitrary", "arbitrary"),
    )(proj, proj, proj, proj, cum4, *x_args)


def _flash_bwd(proj, cum4, o, dog, lp, *, tb, name, host=None):
    T = proj.shape[0]
    D = N_HEADS * HEAD_DIM
    nb = T // tb
    cb = D // LANES
    x_in, x_out, x_shapes, x_scratch, x_args = _host_specs(host)

    def body(q_ref, k_ref, v_ref, g_ref, cum_ref, o_ref, dog_ref, lp_ref,
             dq_ref, dg_ref, dk_ref, dv_ref, dcq_ref, dck_ref,
             kb_ref, vb_ref, dka_ref, dva_ref, dca_ref):
        i = pl.program_id(1)

        @pl.when(i == 0)
        def _():
            kb_ref[...] = k_ref[...].astype(BF16)
            vb_ref[...] = v_ref[...].astype(BF16)
            dka_ref[...] = jnp.zeros_like(dka_ref)
            dva_ref[...] = jnp.zeros_like(dva_ref)
            dca_ref[...] = jnp.zeros_like(dca_ref)

        gate = g_ref[...]
        sg = _sigmoid(gate)
        o = o_ref[...]
        dog = dog_ref[...]
        do = dog * (gate * sg)
        dg_ref[...] = (dog * o * (sg * (1.0 + gate * (1.0 - sg)))).astype(BF16)
        q = q_ref[...] * (HEAD_DIM ** -0.5)
        masks = _head_masks(tb)
        qh = [jnp.where(masks[h], q, 0.0).astype(BF16) for h in range(2)]
        doh = [jnp.where(masks[h], do, 0.0).astype(BF16) for h in range(2)]
        delta = [jnp.sum(jnp.where(masks[h], do * o, 0.0), axis=-1, keepdims=True) for h in range(2)]
        lph = [lp_ref[h][:, 0:1] for h in range(2)]

        def step(kbi, carry, masked):
            k0 = pl.multiple_of(kbi * tb, tb)
            kblk = kb_ref[pl.ds(k0, tb), :]
            vblk = vb_ref[pl.ds(k0, tb), :]
            new, dk, dv = [], None, None
            for h in range(2):
                acc, rs = carry[h]
                s = lax.dot_general(qh[h], kblk, (((1,), (1,)), ((), ())), preferred_element_type=F32)
                p = jnp.exp(s - cum_ref[0, h, pl.ds(kbi, 1), :] - lph[h])
                if masked:
                    p = jnp.where(_causal(i, kbi, tb, tb), p, 0.0)
                dp = lax.dot_general(doh[h], vblk, (((1,), (1,)), ((), ())),
                                     preferred_element_type=F32)
                ds = p * (dp - delta[h])
                pb, dsb = p.astype(BF16), ds.astype(BF16)
                dv_h = lax.dot_general(pb, doh[h], (((0,), (0,)), ((), ())),
                                       preferred_element_type=F32)
                dk_h = lax.dot_general(dsb, qh[h], (((0,), (0,)), ((), ())),
                                       preferred_element_type=F32)
                dv = dv_h if dv is None else dv + dv_h
                dk = dk_h if dk is None else dk + dk_h
                dca_ref[h, pl.ds(kbi, 1), :] -= jnp.sum(ds, axis=0, keepdims=True)
                new.append((acc + jnp.dot(dsb, kblk, preferred_element_type=F32),
                            rs + jnp.sum(ds, axis=-1, keepdims=True)))
            dka_ref[pl.ds(k0, tb), :] += dk
            dva_ref[pl.ds(k0, tb), :] += dv
            return tuple(new)

        init1 = (jnp.zeros((tb, LANES), F32), jnp.zeros((tb, 1), F32))
        carry = lax.fori_loop(0, i, lambda kbi, c: step(kbi, c, False), (init1, init1))
        dqs = []
        for h, (acc, rs) in enumerate(step(i, carry, True)):
            dqs.append(acc)
            dcq_ref[0, 0, pl.ds(h, 1), :] = jnp.broadcast_to(rs, (tb, LANES)).T[0:1, :]
        dq_ref[...] = (jnp.where(masks[0], dqs[0], dqs[1]) * (HEAD_DIM ** -0.5)).astype(BF16)

        @pl.when(i == nb - 1)
        def _():
            dk_ref[...] = dka_ref[...].astype(BF16)
            dv_ref[...] = dva_ref[...].astype(BF16)
            dck_ref[0] = dca_ref[...]

    blk = pl.BlockSpec((tb, LANES), lambda j, i: (i, j))
    full = pl.BlockSpec((T, LANES), lambda j, i: (0, j))
    body = _hosted(body, 8, 6, 5, host, (N_PAIRS, nb))
    return pl.pallas_call(
        body, name=name, grid=(N_PAIRS, nb),
        in_specs=[blk,
                  pl.BlockSpec((T, LANES), lambda j, i: (0, cb + j)),
                  pl.BlockSpec((T, LANES), lambda j, i: (0, 2 * cb + j)),
                  pl.BlockSpec((tb, LANES), lambda j, i: (i, 3 * cb + j)),
                  pl.BlockSpec((1, 2, nb, tb), lambda j, i: (j, 0, 0, 0)),
                  blk, blk, pl.BlockSpec((2, tb, LANES), lambda j, i: (j, i, 0))] + x_in,
        out_specs=[blk, blk, full, full,
                   pl.BlockSpec((1, 1, 2, tb), lambda j, i: (j, i, 0, 0)),
                   pl.BlockSpec((1, 2, nb, tb), lambda j, i: (j, 0, 0, 0))] + x_out,
        out_shape=[jax.ShapeDtypeStruct((T, D), BF16)] * 4
        + [jax.ShapeDtypeStruct((N_PAIRS, nb, 2, tb), F32),
           jax.ShapeDtypeStruct((N_PAIRS, 2, nb, tb), F32)] + x_shapes,
        scratch_shapes=[pltpu.VMEM((T, LANES), BF16), pltpu.VMEM((T, LANES), BF16),
                        pltpu.VMEM((T, LANES), F32), pltpu.VMEM((T, LANES), F32),
                        pltpu.VMEM((2, nb, tb), F32)] + x_scratch,
        compiler_params=_cparams("arbitrary", "arbitrary"),
    )(proj, proj, proj, proj, cum4, o, dog, lp, *x_args)


def _cumsum_fwd(proj, bf_row, *, tt, name):
    T = proj.shape[0]
    cb = (proj.shape[1] - LANES) // LANES

    def body(f_ref, b_ref, out_ref, carry_ref):
        i = pl.program_id(0)

        @pl.when(i == 0)
        def _():
            carry_ref[...] = jnp.zeros_like(carry_ref)

        ls = -_softplus(-(f_ref[...] + b_ref[...]))
        tri = (lax.broadcasted_iota(jnp.int32, (tt, tt), 0)
               >= lax.broadcasted_iota(jnp.int32, (tt, tt), 1)).astype(F32)
        cum = jnp.dot(tri, ls, preferred_element_type=F32,
                      precision=lax.Precision.HIGHEST) + carry_ref[...]
        carry_ref[...] = cum[tt - 1:tt, :]
        out_ref[...] = cum.T

    return pl.pallas_call(
        body, name=name, grid=(T // tt,),
        in_specs=[pl.BlockSpec((tt, LANES), lambda i: (i, cb)),
                  pl.BlockSpec((1, LANES), lambda i: (0, 0))],
        out_specs=pl.BlockSpec((LANES, tt), lambda i: (0, i)),
        out_shape=jax.ShapeDtypeStruct((LANES, T), F32),
        scratch_shapes=[pltpu.VMEM((1, LANES), F32)],
        compiler_params=_cparams("arbitrary"),
    )(proj, bf_row)


def _cumsum_bwd(dcum_t, proj, bf_row, *, tt, name):
    T = proj.shape[0]
    cb = (proj.shape[1] - LANES) // LANES
    nt = T // tt

    def body(dc_ref, f_ref, b_ref, df_ref, db_ref, carry_ref):
        i = pl.program_id(0)

        @pl.when(i == 0)
        def _():
            carry_ref[...] = jnp.zeros_like(carry_ref)
            db_ref[...] = jnp.zeros_like(db_ref)

        dc = dc_ref[...].T
        tri = (lax.broadcasted_iota(jnp.int32, (tt, tt), 0)
               <= lax.broadcasted_iota(jnp.int32, (tt, tt), 1)).astype(F32)
        rev = jnp.dot(tri, dc, preferred_element_type=F32,
                      precision=lax.Precision.HIGHEST) + carry_ref[...]
        carry_ref[...] = rev[0:1, :]
        df = rev * _sigmoid(-(f_ref[...] + b_ref[...]))
        df_ref[...] = df.astype(BF16)
        db_ref[...] += jnp.sum(df, axis=0, keepdims=True)

    return pl.pallas_call(
        body, name=name, grid=(nt,),
        in_specs=[pl.BlockSpec((LANES, tt), lambda i: (0, nt - 1 - i)),
                  pl.BlockSpec((tt, LANES), lambda i: (nt - 1 - i, cb)),
                  pl.BlockSpec((1, LANES), lambda i: (0, 0))],
        out_specs=[pl.BlockSpec((tt, LANES), lambda i: (nt - 1 - i, 0)),
                   pl.BlockSpec((1, LANES), lambda i: (0, 0))],
        out_shape=[jax.ShapeDtypeStruct((T, LANES), BF16), jax.ShapeDtypeStruct((1, LANES), F32)],
        scratch_shapes=[pltpu.VMEM((1, LANES), F32)],
        compiler_params=_cparams("arbitrary"),
    )(dcum_t, proj, bf_row)


def _rows_down(x, before, sh):
    if sh == 0:
        return x
    rolled = pltpu.roll(x, sh, axis=0)
    row = lax.broadcasted_iota(jnp.int32, (SUBLANES, x.shape[1]), 0)
    head = jnp.where(row < sh, pltpu.roll(before, sh, axis=0), rolled[:SUBLANES])
    return jnp.concatenate([head, rolled[SUBLANES:]], axis=0)


def _rows_up(x, after, sh):
    if sh == 0:
        return x
    tt = x.shape[0]
    rolled = pltpu.roll(x, tt - sh, axis=0)
    row = lax.broadcasted_iota(jnp.int32, (SUBLANES, x.shape[1]), 0)
    tail = jnp.where(row >= SUBLANES - sh, pltpu.roll(after, SUBLANES - sh, axis=0),
                     rolled[tt - SUBLANES:])
    return jnp.concatenate([rolled[:tt - SUBLANES], tail], axis=0)


def _rg_gates(u0, before, small_ref, wa_ref, wi_ref):
    taps = [_rows_down(u0, before, CONV_WIDTH - 1 - tap) for tap in range(CONV_WIDTH)]
    u = small_ref[4:5, :]
    for tap in range(CONV_WIDTH):
        u = u + taps[tap] * small_ref[tap:tap + 1, :]
    pa, pi = [], []
    for n in range(RNN_BLOCKS):
        ub = u[:, n * RNN_BLOCK_WIDTH:(n + 1) * RNN_BLOCK_WIDTH].astype(BF16)
        pa.append(jnp.dot(ub, wa_ref[n], preferred_element_type=F32))
        pi.append(jnp.dot(ub, wi_ref[n], preferred_element_type=F32))
    r = _sigmoid(jnp.concatenate(pa, axis=-1) + small_ref[5:6, :])
    ig = _sigmoid(jnp.concatenate(pi, axis=-1) + small_ref[6:7, :])
    spl = _softplus(-small_ref[7:8, :])
    log_a = (-LRU_C) * r * spl
    a = jnp.exp(log_a)
    s2 = jnp.tanh(-log_a) * (a * a + 1.0)
    inv_s = lax.rsqrt(s2)
    s = jnp.where(s2 > 0.0, s2 * inv_s, 0.0)
    return u, taps, r, ig, spl, a, s, inv_s


def _rg_fwd(proj, small, wa, wi, *, tt, name):
    T = proj.shape[0]
    D = RNN_BLOCKS * RNN_BLOCK_WIDTH
    hb = tt // SUBLANES

    def body(u0_ref, halo_ref, g_ref, small_ref, wa_ref, wi_ref, h_ref, y_ref,
             a_ref, b_ref, carry_ref):
        i = pl.program_id(0)

        @pl.when(i == 0)
        def _():
            carry_ref[...] = jnp.zeros_like(carry_ref)

        before = jnp.where(i == 0, 0.0, halo_ref[...])
        u, _, r, ig, spl, a, s, _ = _rg_gates(u0_ref[...], before, small_ref, wa_ref, wi_ref)
        a_ref[...] = a
        b_ref[...] = s * (ig * u)

        def row(t, h):
            h = a_ref[pl.ds(t, 1), :] * h + b_ref[pl.ds(t, 1), :]
            h_ref[pl.ds(t, 1), :] = h
            return h

        carry_ref[...] = lax.fori_loop(0, tt, row, carry_ref[...])
        gate = g_ref[...]
        y_ref[...] = (h_ref[...] * (gate * _sigmoid(gate))).astype(BF16)

    return pl.pallas_call(
        body, name=name, grid=(T // tt,),
        in_specs=[pl.BlockSpec((tt, D), lambda i: (i, 0)),
                  pl.BlockSpec((SUBLANES, D), lambda i: (jnp.maximum(i * hb - 1, 0), 0)),
                  pl.BlockSpec((tt, D), lambda i: (i, 1)),
                  pl.BlockSpec((SUBLANES, D), lambda i: (0, 0)),
                  pl.BlockSpec((RNN_BLOCKS, RNN_BLOCK_WIDTH, RNN_BLOCK_WIDTH), lambda i: (0, 0, 0)),
                  pl.BlockSpec((RNN_BLOCKS, RNN_BLOCK_WIDTH, RNN_BLOCK_WIDTH), lambda i: (0, 0, 0))],
        out_specs=[pl.BlockSpec((tt, D), lambda i: (i, 0)), pl.BlockSpec((tt, D), lambda i: (i, 0))],
        out_shape=[jax.ShapeDtypeStruct((T, D), F32), jax.ShapeDtypeStruct((T, D), BF16)],
        scratch_shapes=[pltpu.VMEM((tt, D), F32), pltpu.VMEM((tt, D), F32),
                        pltpu.VMEM((1, D), F32)],
        compiler_params=_cparams("arbitrary"),
    )(proj, proj, proj, small, wa, wi)


def _rg_bwd(proj, hs, dy, small, wa, wi, *, tt, name):
    T = proj.shape[0]
    D = RNN_BLOCKS * RNN_BLOCK_WIDTH
    W = RNN_BLOCK_WIDTH
    hb = tt // SUBLANES
    nt = T // tt

    def body(u0_ref, uhalo_ref, g_ref, h_ref, hhalo_ref, dy_ref, small_ref, wa_ref, wi_ref,
             dp_ref, dwa_ref, dwi_ref, ds_ref,
             a_ref, g_s_ref, dunext_ref, carry_ref):
        i = pl.program_id(0)
        first_chunk = i == nt - 1

        @pl.when(i == 0)
        def _():
            carry_ref[...] = jnp.zeros_like(carry_ref)
            dunext_ref[...] = jnp.zeros_like(dunext_ref)
            dwa_ref[...] = jnp.zeros_like(dwa_ref)
            dwi_ref[...] = jnp.zeros_like(dwi_ref)
            ds_ref[...] = jnp.zeros_like(ds_ref)

        u_before = jnp.where(first_chunk, 0.0, uhalo_ref[...])
        h_before = jnp.where(first_chunk, 0.0, hhalo_ref[...])
        u, taps, r, ig, spl, a, s, inv_s = _rg_gates(u0_ref[...], u_before, small_ref, wa_ref,
                                                     wi_ref)
        gate = g_ref[...]
        sg = _sigmoid(gate)
        dy = dy_ref[...]
        dp_ref[:, D:] = (dy * h_ref[...] * (sg * (1.0 + gate * (1.0 - sg)))).astype(BF16)
        a_ref[...] = a
        g_s_ref[...] = dy * (gate * sg)

        def row(k, c):
            t = tt - 1 - k
            g = g_s_ref[pl.ds(t, 1), :] + c
            g_s_ref[pl.ds(t, 1), :] = g
            return a_ref[pl.ds(t, 1), :] * g

        carry_ref[...] = lax.fori_loop(0, tt, row, carry_ref[...])
        g = g_s_ref[...]
        h_prev = _rows_down(h_ref[...], h_before, 1)
        iu = ig * u
        d_iu = g * s
        dlog_a = (g * h_prev) * a - (g * iu) * (a * a) * inv_s
        dpre_a = (dlog_a * ((-LRU_C) * spl)) * r * (1.0 - r)
        dpre_i = (d_iu * u) * ig * (1.0 - ig)
        dlam = jnp.sum(dlog_a * r, axis=0, keepdims=True) * (LRU_C * _sigmoid(-small_ref[7:8, :]))
        du_parts = []
        for n in range(RNN_BLOCKS):
            sl = slice(n * W, (n + 1) * W)
            ub = u[:, sl].astype(BF16)
            da_n = dpre_a[:, sl].astype(BF16)
            di_n = dpre_i[:, sl].astype(BF16)
            dwa_ref[n] += lax.dot_general(ub, da_n, (((0,), (0,)), ((), ())),
                                          preferred_element_type=F32)
            dwi_ref[n] += lax.dot_general(ub, di_n, (((0,), (0,)), ((), ())),
                                          preferred_element_type=F32)
            du_parts.append(
                lax.dot_general(da_n, wa_ref[n], (((1,), (1,)), ((), ())), preferred_element_type=F32)
                + lax.dot_general(di_n, wi_ref[n], (((1,), (1,)), ((), ())), preferred_element_type=F32))
        du = d_iu * ig + jnp.concatenate(du_parts, axis=-1)
        for tap in range(CONV_WIDTH):
            ds_ref[tap:tap + 1, :] += jnp.sum(du * taps[tap], axis=0, keepdims=True)
        ds_ref[4:5, :] += jnp.sum(du, axis=0, keepdims=True)
        ds_ref[5:6, :] += jnp.sum(dpre_a, axis=0, keepdims=True)
        ds_ref[6:7, :] += jnp.sum(dpre_i, axis=0, keepdims=True)
        ds_ref[7:8, :] += dlam
        du_after = dunext_ref[...]
        du0 = jnp.zeros((tt, D), F32)
        for tap in range(CONV_WIDTH):
            du0 = du0 + _rows_up(du, du_after, CONV_WIDTH - 1 - tap) * small_ref[tap:tap + 1, :]
        dp_ref[:, :D] = du0.astype(BF16)
        dunext_ref[...] = du[0:SUBLANES, :]

    rev = lambda i: nt - 1 - i
    wspec = pl.BlockSpec((RNN_BLOCKS, W, W), lambda i: (0, 0, 0))
    return pl.pallas_call(
        body, name=name, grid=(nt,),
        in_specs=[pl.BlockSpec((tt, D), lambda i: (rev(i), 0)),
                  pl.BlockSpec((SUBLANES, D), lambda i: (jnp.maximum(rev(i) * hb - 1, 0), 0)),
                  pl.BlockSpec((tt, D), lambda i: (rev(i), 1)),
                  pl.BlockSpec((tt, D), lambda i: (rev(i), 0)),
                  pl.BlockSpec((SUBLANES, D), lambda i: (jnp.maximum(rev(i) * hb - 1, 0), 0)),
                  pl.BlockSpec((tt, D), lambda i: (rev(i), 0)),
                  pl.BlockSpec((SUBLANES, D), lambda i: (0, 0)),
                  wspec, wspec],
        out_specs=[pl.BlockSpec((tt, 2 * D), lambda i: (rev(i), 0)),
                   wspec, wspec, pl.BlockSpec((SUBLANES, D), lambda i: (0, 0))],
        out_shape=[jax.ShapeDtypeStruct((T, 2 * D), BF16),
                   jax.ShapeDtypeStruct((RNN_BLOCKS, W, W), F32),
                   jax.ShapeDtypeStruct((RNN_BLOCKS, W, W), F32),
                   jax.ShapeDtypeStruct((SUBLANES, D), F32)],
        scratch_shapes=[pltpu.VMEM((tt, D), F32), pltpu.VMEM((tt, D), F32),
                        pltpu.VMEM((SUBLANES, D), F32), pltpu.VMEM((1, D), F32)],
        compiler_params=_cparams("arbitrary"),
    )(proj, proj, proj, hs, hs, dy, small, wa, wi)


def _out_ln(a, w, x, g, b, *, tt, name):
    T, D = x.shape
    K = a.shape[1]

    def body(a_ref, w_ref, x_ref, g_ref, b_ref, y_ref, yb_ref, zh_ref, rs_ref):
        h = jnp.dot(a_ref[...].astype(BF16), w_ref[...].astype(BF16), preferred_element_type=F32)
        z = ALPHA * x_ref[...] + h
        mu = jnp.mean(z, axis=-1, keepdims=True)
        zc = z - mu
        rstd = lax.rsqrt(jnp.mean(zc * zc, axis=-1, keepdims=True) + LN_EPS)
        zh = zc * rstd
        zh_ref[...] = zh
        rs_ref[...] = rstd
        y = zh * g_ref[...] + b_ref[...]
        y_ref[...] = y
        yb_ref[...] = y.astype(BF16)

    blk = pl.BlockSpec((tt, D), lambda i: (i, 0))
    row = pl.BlockSpec((1, D), lambda i: (0, 0))
    return pl.pallas_call(
        body, name=name, grid=(T // tt,),
        in_specs=[pl.BlockSpec((tt, K), lambda i: (i, 0)), pl.BlockSpec((K, D), lambda i: (0, 0)),
                  blk, row, row],
        out_specs=[blk, blk, blk, pl.BlockSpec((tt, 1), lambda i: (i, 0))],
        out_shape=[jax.ShapeDtypeStruct((T, D), F32), jax.ShapeDtypeStruct((T, D), BF16),
                   jax.ShapeDtypeStruct((T, D), F32), jax.ShapeDtypeStruct((T, 1), F32)],
        compiler_params=_cparams("parallel"),
    )(a, w, x, g, b)


def _ln_bwd_tile(dy, zh_ref, rs_ref, g_ref, dz_ref, dzb_ref, dg_ref, db_ref, first):
    @pl.when(first)
    def _():
        dg_ref[...] = jnp.zeros_like(dg_ref)
        db_ref[...] = jnp.zeros_like(db_ref)

    zh = zh_ref[...]
    dg_ref[...] += jnp.sum(dy * zh, axis=0, keepdims=True)
    db_ref[...] += jnp.sum(dy, axis=0, keepdims=True)
    dzh = dy * g_ref[...]
    m1 = jnp.mean(dzh, axis=-1, keepdims=True)
    m2 = jnp.mean(dzh * zh, axis=-1, keepdims=True)
    dz = rs_ref[...] * (dzh - m1 - zh * m2)
    dz_ref[...] = dz
    dzb_ref[...] = dz.astype(BF16)


def _ln_bwd_specs(T, D, tt):
    blk = pl.BlockSpec((tt, D), lambda i: (i, 0))
    row = pl.BlockSpec((1, D), lambda i: (0, 0))
    return ([blk, pl.BlockSpec((tt, 1), lambda i: (i, 0)), row], [blk, blk, row, row],
            [jax.ShapeDtypeStruct((T, D), F32), jax.ShapeDtypeStruct((T, D), BF16),
             jax.ShapeDtypeStruct((1, D), F32), jax.ShapeDtypeStruct((1, D), F32)])


def _loss_ln_bwd(y, tgt, zh, rstd, g, *, tt, name):
    T, D = y.shape
    ln_in, ln_out, ln_shapes = _ln_bwd_specs(T, D, tt)

    def body(y_ref, t_ref, zh_ref, rs_ref, g_ref, l_ref, dz_ref, dzb_ref, dg_ref, db_ref):
        first = pl.program_id(0) == 0

        @pl.when(first)
        def _():
            l_ref[...] = jnp.zeros_like(l_ref)

        e = y_ref[...] - t_ref[...]
        l_ref[...] += jnp.sum(e * e, axis=0, keepdims=True) * (0.5 / D)
        _ln_bwd_tile(e * (1.0 / D), zh_ref, rs_ref, g_ref, dz_ref, dzb_ref, dg_ref, db_ref, first)

    blk = pl.BlockSpec((tt, D), lambda i: (i, 0))
    return pl.pallas_call(
        body, name=name, grid=(T // tt,),
        in_specs=[blk, blk] + ln_in,
        out_specs=[pl.BlockSpec((1, D), lambda i: (0, 0))] + ln_out,
        out_shape=[jax.ShapeDtypeStruct((1, D), F32)] + ln_shapes,
        compiler_params=_cparams("arbitrary"),
    )(y, tgt, zh, rstd, g)


def _dx_ln_bwd(a, b, add, zh, rstd, g, *, tm, name):
    T, D = add.shape
    na = len(a)
    K = sum(p.shape[1] for p in a)
    ln_in, ln_out, ln_shapes = _ln_bwd_specs(T, D, tm)

    def body(*refs):
        a_refs, b_ref, add_ref = refs[:na], refs[na], refs[na + 1]
        av = [r[...].astype(BF16) for r in a_refs]
        av = av[0] if na == 1 else jnp.concatenate(av, axis=1)
        dy = lax.dot_general(av, b_ref[...].astype(BF16), (((1,), (1,)), ((), ())),
                             preferred_element_type=F32) + ALPHA * add_ref[...]
        _ln_bwd_tile(dy, *refs[na + 2:], pl.program_id(0) == 0)

    return pl.pallas_call(
        body, name=name, grid=(T // tm,),
        in_specs=[pl.BlockSpec((tm, p.shape[1]), lambda i: (i, 0)) for p in a]
        + [pl.BlockSpec((D, K), lambda i: (0, 0)), pl.BlockSpec((tm, D), lambda i: (i, 0))] + ln_in,
        out_specs=ln_out, out_shape=ln_shapes,
        compiler_params=_cparams("arbitrary"),
    )(*a, b, add, zh, rstd, g)


def _row_tile(rows, target):
    best = SUBLANES
    for t in range(SUBLANES, target + 1, SUBLANES):
        if rows % t == 0:
            best = t
    return best


def _add_own(g, recv, c_idx, *, tr, name):
    _, M, R, C = g.shape

    def body(c_ref, g_ref, r_ref, o_ref, ob_ref):
        s = g_ref[0] + r_ref[...]
        o_ref[...] = s
        ob_ref[...] = s.astype(BF16)

    blk = pl.BlockSpec((1, tr, C), lambda k, i, c: (k, i, 0))
    return pl.pallas_call(
        body, name=name,
        grid_spec=pltpu.PrefetchScalarGridSpec(
            num_scalar_prefetch=1, grid=(M, R // tr),
            in_specs=[pl.BlockSpec((1, 1, tr, C), lambda k, i, c: (c[0], k, i, 0)), blk],
            out_specs=[blk, blk]),
        out_shape=[jax.ShapeDtypeStruct((M, R, C), F32), jax.ShapeDtypeStruct((M, R, C), BF16)],
        compiler_params=_cparams("parallel", "parallel"),
    )(c_idx, g, recv)


def _adamw_math(g, w_ref, m_ref, v_ref, g_ref, d_ref, nm_ref, nv_ref):
    nm = ADAM_B1 * m_ref[...] + (1.0 - ADAM_B1) * g
    nv = ADAM_B2 * v_ref[...] + (1.0 - ADAM_B2) * (g * g)
    m_hat = nm / (1.0 - ADAM_B1 ** ADAM_STEP)
    v_hat = nv / (1.0 - ADAM_B2 ** ADAM_STEP)
    g_ref[...] = g
    nm_ref[...] = nm
    nv_ref[...] = nv
    d_ref[...] = (-ADAM_LR) * (m_hat / (jnp.sqrt(v_hat) + ADAM_EPS) + ADAM_WD * w_ref[...])


def _adamw(parts, w, m, v, *, tr, name):
    n, R, C = parts.shape
    tr = min(tr, R)

    def body(p_ref, w_ref, m_ref, v_ref, *out_refs):
        g = p_ref[0]
        for k in range(1, n):
            g = g + p_ref[k]
        _adamw_math(g, w_ref, m_ref, v_ref, *out_refs)

    blk = pl.BlockSpec((tr, C), lambda i: (i, 0))
    out = jax.ShapeDtypeStruct((R, C), F32)
    return pl.pallas_call(
        body, name=name, grid=(R // tr,),
        in_specs=[pl.BlockSpec((n, tr, C), lambda i: (0, i, 0)), blk, blk, blk],
        out_specs=[blk, blk, blk, blk], out_shape=[out, out, out, out],
        compiler_params=_cparams("parallel"),
    )(parts, w, m, v)


def _adamw_shard(parts_by_layer, place, w, m, v, *, tr, name):
    L, R, C = w.shape
    flat = [(l, a, pick) for l, parts in enumerate(parts_by_layer) for a, pick in parts]
    n = len(flat)

    def body(place_ref, *refs):
        w_ref, m_ref, v_ref = refs[n:n + 3]
        for layer in range(L):
            @pl.when(pl.program_id(0) == layer)
            def _(layer=layer):
                g = None
                for (l, _, _), r in zip(flat, refs[:n]):
                    if l == layer:
                        blk = r[(0,) * (len(r.shape) - 3)].astype(F32)
                        g = blk if g is None else g + blk
                _adamw_math(g, w_ref, m_ref, v_ref, *refs[n + 3:])

    blk = pl.BlockSpec((1, tr, C), lambda ly, i, s: (ly, i, 0))

    def part_spec(l, a, pick):
        return pl.BlockSpec((1,) * (a.ndim - 2) + (tr, C),
                            lambda ly, i, s: (*pick(s), jnp.where(ly == l, i, 0), 0))

    out = jax.ShapeDtypeStruct(w.shape, F32)
    return pl.pallas_call(
        body, name=name,
        grid_spec=pltpu.PrefetchScalarGridSpec(
            num_scalar_prefetch=1, grid=(L, R // tr),
            in_specs=[part_spec(*f) for f in flat] + [blk, blk, blk],
            out_specs=[blk, blk, blk, blk]),
        out_shape=[out, out, out, out],
        compiler_params=_cparams("arbitrary", "arbitrary"),
    )(place, *[a for _, a, _ in flat], w, m, v)


def _two_stage_parts(h, recv):
    return [(h, lambda s: (s[0], 0))] + [(recv, lambda s, d=d: (s[0] ^ d, 0)) for d in (1, 2, 3)]


def _direct_parts(g, recv):
    return [(g, lambda s: (s[1], s[0]))] + [
        (recv, lambda s, a=p // 4, d=p % 4: (s[1] ^ a, s[0] ^ d)) for p in range(1, 8)]


SHARD_AXIS = dict(attn_w_in=1, attn_w_out=0, rnn_w_in=1, rnn_w_out=0, rnn_w_a=1, rnn_w_i=1,
                  rnn_conv_w=1, rnn_conv_b=0, rnn_b_a=0, rnn_b_i=0, rnn_lambda=0)
RNN_ROWED = ("rnn_w_out", "rnn_w_a", "rnn_w_i")
SMALL = ("rnn_conv_w", "rnn_conv_b", "rnn_b_a", "rnn_b_i", "rnn_lambda")
PACK_C = 1024


def _elems(shape):
    n = 1
    for s in shape:
        n *= s
    return n


def _pack_rows(p, idx, dtype):
    parts = [p[k][idx].astype(dtype).reshape(-1, PACK_C) for k in RNN_ROWED]
    small = jnp.concatenate([p[k][idx].reshape(-1) for k in SMALL])
    tile_rows = SUBLANES * (4 // jnp.dtype(dtype).itemsize)
    if dtype == BF16:
        small = lax.bitcast_convert_type(small, BF16)
    small = small.reshape(-1, PACK_C)
    parts.append(jnp.pad(small, ((0, tile_rows - small.shape[0]), (0, 0))))
    return jnp.concatenate(parts, axis=0)


def _unpack_rows(flat, shapes):
    lead = flat.shape[:-2]
    out, r = {}, 0
    for k in RNN_ROWED:
        n = _elems(shapes[k]) // PACK_C
        out[k] = flat[..., r:r + n, :].reshape(lead + shapes[k])
        r += n
    n_small = sum(_elems(shapes[k]) for k in SMALL)
    small = flat[..., r:r + n_small // PACK_C, :].reshape(lead + (-1,))
    o = 0
    for k in SMALL:
        n = _elems(shapes[k])
        out[k] = small[..., o:o + n].reshape(lead + shapes[k])
        o += n
    return out


def _join_columns(g, width, *, tr, name):
    _, _, R, S = g.shape

    def body(*refs):
        o_ref = refs[8]
        parts = [refs[r][0, 0].astype(F32) for r in range(8)]
        parts.append(jnp.zeros((tr, width - 8 * S), F32))
        o_ref[...] = jnp.concatenate(parts, axis=-1).astype(o_ref.dtype)

    def shard(r):
        return pl.BlockSpec((1, 1, tr, S), lambda i: (r % 2, r // 2, i, 0))

    return pl.pallas_call(
        body, name=name, grid=(R // tr,),
        in_specs=[shard(r) for r in range(8)],
        out_specs=pl.BlockSpec((tr, width), lambda i: (i, 0)),
        out_shape=jax.ShapeDtypeStruct((R, width), g.dtype),
        compiler_params=_cparams("parallel"),
    )(*([g] * 8))


def _split_columns(parts, S, *, tr, name):
    R = parts[0].shape[0]
    n = len(parts)

    def body(*refs):
        o_ref = refs[n]
        x = jnp.concatenate([r[...] for r in refs[:n]], axis=1)
        for r in range(8):
            o_ref[r % 2, r // 2] = x[:, r * S:(r + 1) * S]

    return pl.pallas_call(
        body, name=name, grid=(R // tr,),
        in_specs=[pl.BlockSpec((tr, p.shape[1]), lambda i: (i, 0)) for p in parts],
        out_specs=pl.BlockSpec((2, 4, tr, S), lambda i: (0, 0, i, 0)),
        out_shape=jax.ShapeDtypeStruct((2, 4, R, S), parts[0].dtype),
        compiler_params=_cparams("parallel"),
    )(*parts)


def _to_full(g, k, sh):
    ax, nd = SHARD_AXIS[k], len(sh)
    perm = tuple(range(2, 2 + ax)) + (1, 0) + tuple(range(2 + ax, 2 + nd))
    return g.transpose(perm).reshape(sh[:ax] + (8 * sh[ax],) + sh[ax + 1:])


def _from_full(full, k, sh):
    ax, nd = SHARD_AXIS[k], len(sh)
    t = full.reshape(sh[:ax] + (4, 2, sh[ax]) + sh[ax + 1:])
    return t.transpose((ax + 1, ax) + tuple(range(ax)) + tuple(range(ax + 2, nd + 2)))


def _unpack_gathered_rows(g, shapes):
    out, r = {}, 0
    for k in RNN_ROWED:
        n = _elems(shapes[k]) // PACK_C
        out[k] = _to_full(g[:, :, r:r + n].reshape((2, 4) + shapes[k]), k, shapes[k])
        r += n
    n_small = sum(_elems(shapes[k]) for k in SMALL)
    nr = 2 * n_small // PACK_C
    small = lax.bitcast_convert_type(g[:, :, r:r + nr].reshape(2, 4, n_small, 2), F32)
    o = 0
    for k in SMALL:
        n = _elems(shapes[k])
        out[k] = _to_full(small[:, :, o:o + n].reshape((2, 4) + shapes[k]), k, shapes[k])
        o += n
    return out


def _pack_grad_rows(full, shapes):
    parts = [_from_full(full[k], k, shapes[k]).reshape(2, 4, -1, PACK_C) for k in RNN_ROWED]
    small = jnp.concatenate(
        [_from_full(full[k], k, shapes[k]).reshape(2, 4, -1) for k in SMALL], axis=-1)
    small = small.reshape(2, 4, -1, PACK_C)
    parts.append(jnp.pad(small, ((0, 0), (0, 0), (0, SUBLANES - small.shape[2]), (0, 0))))
    return jnp.concatenate(parts, axis=2)


def kernel(x, ln_g, ln_b, attn_w_in, attn_b_f, attn_w_out, rnn_w_in, rnn_conv_w, rnn_conv_b, rnn_w_a, rnn_b_a, rnn_w_i, rnn_b_i, rnn_lambda, rnn_w_out, loss_target, m_ln_g, m_ln_b, m_attn_w_in, m_attn_b_f, m_attn_w_out, m_rnn_w_in, m_rnn_conv_w, m_rnn_conv_b, m_rnn_w_a, m_rnn_b_a, m_rnn_w_i, m_rnn_b_i, m_rnn_lambda, m_rnn_w_out, v_ln_g, v_ln_b, v_attn_w_in, v_attn_b_f, v_attn_w_out, v_rnn_w_in, v_rnn_conv_w, v_rnn_conv_b, v_rnn_w_a, v_rnn_b_a, v_rnn_w_i, v_rnn_b_i, v_rnn_lambda, v_rnn_w_out):
    w_loc = dict(attn_w_in=attn_w_in, attn_w_out=attn_w_out, rnn_w_in=rnn_w_in, rnn_w_a=rnn_w_a,
                 rnn_w_i=rnn_w_i, rnn_w_out=rnn_w_out, rnn_conv_w=rnn_conv_w, rnn_conv_b=rnn_conv_b,
                 rnn_b_a=rnn_b_a, rnn_b_i=rnn_b_i, rnn_lambda=rnn_lambda)
    m_loc = dict(attn_w_in=m_attn_w_in, attn_w_out=m_attn_w_out, rnn_w_in=m_rnn_w_in,
                 rnn_w_a=m_rnn_w_a, rnn_w_i=m_rnn_w_i, rnn_w_out=m_rnn_w_out,
                 rnn_conv_w=m_rnn_conv_w, rnn_conv_b=m_rnn_conv_b, rnn_b_a=m_rnn_b_a,
                 rnn_b_i=m_rnn_b_i, rnn_lambda=m_rnn_lambda)
    v_loc = dict(attn_w_in=v_attn_w_in, attn_w_out=v_attn_w_out, rnn_w_in=v_rnn_w_in,
                 rnn_w_a=v_rnn_w_a, rnn_w_i=v_rnn_w_i, rnn_w_out=v_rnn_w_out,
                 rnn_conv_w=v_rnn_conv_w, rnn_conv_b=v_rnn_conv_b, rnn_b_a=v_rnn_b_a,
                 rnn_b_i=v_rnn_b_i, rnn_lambda=v_rnn_lambda)
    shapes = {k: tuple(a.shape[1:]) for k, a in w_loc.items()}
    T, D = x.shape[1], x.shape[2]
    n_f = attn_b_f.shape[1]
    tb = min(1024, T)
    tb_bwd = min(512, T)
    tt_rg = min(128, T)
    tt_ln = min(256, T)
    c_idx = lax.axis_index("c").astype(jnp.int32).reshape(1)
    me_idx = (2 * lax.axis_index("x") + lax.axis_index("y")).astype(jnp.int32).reshape(1)
    place = jnp.concatenate([me_idx, c_idx])

    def attn_w_in_full(g_in, idx):
        return _join_columns(g_in, 4 * D + LANES, tr=256, name=f"a_join{idx}")

    def attn_w_out_full(g_out):
        return _to_full(g_out, "attn_w_out", shapes["attn_w_out"])

    def rnn_weights(g_in, g_rows):
        w = _unpack_gathered_rows(g_rows, shapes)
        w["rnn_w_in"] = _to_full(g_in, "rnn_w_in", shapes["rnn_w_in"])
        w["small"] = jnp.concatenate([w["rnn_conv_w"], w["rnn_conv_b"][None], w["rnn_b_a"][None],
                                      w["rnn_b_i"][None], w["rnn_lambda"][None]])
        return w

    g0 = _ag_c(_run_exchange(_Exchange("gather", [attn_w_in[0].astype(BF16)]), "ag_w0_xy"),
               "ag_w0_c")
    later = _Exchange("gather8", [
        attn_w_out.astype(BF16), attn_w_in[1].astype(BF16), rnn_w_in.astype(BF16),
        jnp.stack([_pack_rows(w_loc, i, BF16) for i in range(2)])])
    w_attn_in, w_attn_out, w_rnn = [attn_w_in_full(g0[0], 0), None], [None, None], [None, None]
    bf_rows = jnp.pad(attn_b_f, ((0, 0), (0, LANES - n_f)))[:, None, :]

    xs, xb, saved = [x[0]], [x[0]], []
    for layer in range(DEPTH):
        idx, xl, xm = layer // 2, xs[-1], xb[-1]
        if layer % 2 == 0:
            proj = _matmul(xm, w_attn_in[idx], trans_b=False, tm=512, tn=1408,
                           name=f"a_proj{layer}")
            cum_t = _cumsum_fwd(proj, bf_rows[idx], tt=min(512, T), name=f"a_cum{layer}")
            cum2 = cum_t[:N_HEADS].reshape(N_PAIRS, 2, T)
            o, og, lp, *got = _flash_fwd(proj, cum2.reshape(N_PAIRS, 2, T // tb, tb), tb=tb,
                                         name=f"a_fwd{layer}", host=later if layer == 0 else None)
            cum4 = cum2.reshape(N_PAIRS, 2, T // tb_bwd, tb_bwd)
            if layer == 0:
                g1 = got
                w_attn_out = [attn_w_out_full(g1[0][:, :, i]) for i in range(2)]
                w_attn_in[1] = attn_w_in_full(g1[1], 1)
                w_rnn = [rnn_weights(g1[2][:, :, i], g1[3][:, :, i]) for i in range(2)]
            branch, w_out = og, w_attn_out[idx]
            saved.append((proj, cum4, o, og, lp))
        else:
            w = w_rnn[idx]
            proj = _matmul(xm, w["rnn_w_in"], trans_b=False, tm=512, tn=1024,
                           name=f"r_proj{layer}")
            hs, yr = _rg_fwd(proj, w["small"], w["rnn_w_a"], w["rnn_w_i"], tt=tt_rg,
                             name=f"r_fwd{layer}")
            branch, w_out = yr, w["rnn_w_out"]
            saved.append((proj, hs, yr))
        y, yb, zh, rstd = _out_ln(branch, w_out, xl, ln_g[layer][None], ln_b[layer][None],
                                  tt=512, name=f"out_ln{layer}")
        saved[-1] = saved[-1] + (zh, rstd)
        xs.append(y)
        xb.append(yb)

    def ln_below(layer):
        return saved[layer][-2:] + (ln_g[layer][None],)

    loss_lanes, *ln_grads = _loss_ln_bwd(xs[-1], loss_target[0], *ln_below(DEPTH - 1), tt=tt_ln,
                                         name="loss_ln_bwd")
    loss = lax.psum(jnp.sum(loss_lanes), ("x", "y", "c"))

    def reduce_pair(gs, layer):
        recv = _rs_c(gs, f"rs_c{layer}")
        outs = [_add_own(g, r, c_idx, tr=_row_tile(g.shape[2], 512), name=f"rs_add{layer}_{n}")
                for n, (g, r) in enumerate(zip(gs, recv))]
        return [o[0][:, None] for o in outs], [o[1][:, None] for o in outs]

    part, got_parts = [None] * DEPTH, [None] * DEPTH
    d_ln_g, d_ln_b, d_bf = [None] * DEPTH, [None] * DEPTH, [None, None]
    for layer in reversed(range(DEPTH)):
        idx, xm = layer // 2, xb[layer]
        dz, dzb, dg, db = ln_grads
        d_ln_g[layer], d_ln_b[layer] = dg[0], db[0]
        if layer % 2 == 0:
            w_in, w_out = w_attn_in[idx], w_attn_out[idx]
            proj, cum4, o, og, lp = saved[layer][:5]
            dog = _matmul(dzb, w_out, trans_b=True, tm=512, tn=1024, name=f"a_dog{layer}")
            dwo = _matmul_tn(og, dzb, tm=512, tn=1024, tk=1024, name=f"a_dwo{layer}")
            riders = [l for l in range(layer + 1, DEPTH) if got_parts[l] is None]
            host = _Exchange("scatter8", [g for l in riders for g in part[l]]) if riders else None
            dq, dgate, dk, dv, dcum_q, dcum_k, *got = _flash_bwd(proj, cum4, o, dog, lp, tb=tb_bwd,
                                                                 name=f"a_bwd{layer}", host=host)
            for l in riders:
                got_parts[l], got = got[:len(part[l])], got[len(part[l]):]
            dcum_t = (dcum_q.transpose(0, 2, 1, 3) + dcum_k).reshape(N_HEADS, T)
            dcum_t = jnp.pad(dcum_t, ((0, LANES - N_HEADS), (0, 0)))
            df, dbf = _cumsum_bwd(dcum_t, proj, bf_rows[idx], tt=min(512, T), name=f"a_dcum{layer}")
            d_bf[idx] = dbf[0, :n_f]
            dproj = [dq, dk, dv, dgate, df]
            dwi = _matmul_tn_parts(xm, dproj, tm=512, tk=1024, name=f"a_dwi{layer}")
            gs = [_split_columns(dwi, shapes["attn_w_in"][1], tr=256, name=f"a_split{layer}"),
                  _from_full(dwo, "attn_w_out", shapes["attn_w_out"])]
            if layer > 0:
                part[layer] = gs
                ln_grads = _dx_ln_bwd(dproj, w_in, dz, *ln_below(layer - 1), tm=256,
                                      name=f"a_dx{layer}")
            else:
                part[layer], narrow = reduce_pair(gs, layer)
                dy, *got_parts[layer] = _matmul(dproj, w_in, trans_b=True, tm=512, tn=1024,
                                                name=f"a_dx{layer}", add=dz, add_scale=ALPHA,
                                                host=_Exchange("scatter", narrow))
        else:
            w = w_rnn[idx]
            proj, hs, yr = saved[layer][:3]
            dyr = _matmul(dzb, w["rnn_w_out"], trans_b=True, tm=512, tn=1024, name=f"r_dy{layer}")
            dwo = _matmul_tn(yr, dzb, tm=512, tn=1024, tk=1024, name=f"r_dwo{layer}")
            dproj, dwa, dwi_, dsm = _rg_bwd(proj, hs, dyr, w["small"], w["rnn_w_a"], w["rnn_w_i"],
                                            tt=tt_rg, name=f"r_bwd{layer}")
            dwin = _matmul_tn(xm, dproj, tm=512, tn=2048, tk=1024, name=f"r_dwi{layer}")
            ln_grads = _dx_ln_bwd([dproj], w["rnn_w_in"], dz, *ln_below(layer - 1), tm=512,
                                  name=f"r_dx{layer}")
            full = dict(rnn_w_out=dwo, rnn_w_a=dwa, rnn_w_i=dwi_, rnn_conv_w=dsm[0:4],
                        rnn_conv_b=dsm[4], rnn_b_a=dsm[5], rnn_b_i=dsm[6], rnn_lambda=dsm[7])
            part[layer] = [_from_full(dwin, "rnn_w_in", shapes["rnn_w_in"]),
                           _pack_grad_rows(full, shapes)]
    grad_x = dy[None]

    def grad_parts(layer, n):
        make = _two_stage_parts if layer == 0 else _direct_parts
        return make(part[layer][n], got_parts[layer][n])

    def update(k, n, wmv):
        layers = [2 * idx + (0 if k.startswith("attn") else 1) for idx in range(2)]
        return _adamw_shard([grad_parts(layer, n) for layer in layers], place, *wmv,
                            tr=_row_tile(wmv[0].shape[1], 256), name=f"adamw_{k}")

    shard_outs = [dict() for _ in range(4)]
    for k, n in (("attn_w_in", 0), ("attn_w_out", 1), ("rnn_w_in", 0)):
        for j, a in enumerate(update(k, n, (w_loc[k], m_loc[k], v_loc[k]))):
            shard_outs[j][k] = a
    rows_wmv = [jnp.stack([_pack_rows(d, idx, F32) for idx in range(2)]) for d in (w_loc, m_loc, v_loc)]
    for j, a in enumerate(update("rnn_rows", 1, rows_wmv)):
        shard_outs[j].update(_unpack_rows(a, shapes))
    g_sh, d_sh, nm_sh, nv_sh = shard_outs

    def rep_pack(lg, lb, bf):
        rows = jnp.concatenate([lg, lb, jnp.pad(bf.reshape(1, -1), ((0, 0), (0, D - 2 * n_f)))])
        return jnp.pad(rows, ((0, 16 - rows.shape[0]), (0, 0)))

    rep = _all_gather(rep_pack(jnp.stack(d_ln_g), jnp.stack(d_ln_b), jnp.stack(d_bf)), "ag_rep")
    rg, rd, rm, rv = _adamw(rep.reshape(8, 16, D), rep_pack(ln_g, ln_b, attn_b_f),
                            rep_pack(m_ln_g, m_ln_b, m_attn_b_f),
                            rep_pack(v_ln_g, v_ln_b, v_attn_b_f), tr=16, name="adamw_rep")

    def rep_unpack(a):
        return dict(ln_g=a[0:DEPTH], ln_b=a[DEPTH:2 * DEPTH],
                    attn_b_f=a[2 * DEPTH, :2 * n_f].reshape(2, n_f))

    order = ("ln_g", "ln_b", "attn_w_in", "attn_b_f", "attn_w_out", "rnn_w_in", "rnn_conv_w",
             "rnn_conv_b", "rnn_w_a", "rnn_b_a", "rnn_w_i", "rnn_b_i", "rnn_lambda", "rnn_w_out")
    outs = [loss, grad_x]
    for sh, rp in ((g_sh, rg), (d_sh, rd), (nm_sh, rm), (nv_sh, rv)):
        allp = {**sh, **rep_unpack(rp)}
        outs.extend(allp[k] for k in order)
    return tuple(outs)
```

```python
import jax
import jax.numpy as jnp
from jax import lax
from jax.experimental import pallas as pl
from jax.experimental.pallas import tpu as pltpu

F32 = jnp.float32
BF16 = jnp.bfloat16

DEPTH = 4
N_HEADS = 16
HEAD_DIM = 64
N_PAIRS = N_HEADS // 2
RNN_BLOCKS = 4
RNN_BLOCK_WIDTH = 256
CONV_WIDTH = 4
LRU_C = 8.0
ALPHA = (2.0 * DEPTH) ** 0.25
LN_EPS = 1e-5
ADAM_LR, ADAM_B1, ADAM_B2, ADAM_EPS, ADAM_WD, ADAM_STEP = 0.001, 0.9, 0.999, 1e-8, 0.01, 10

LANES = 128
SUBLANES = 8
VMEM_LIMIT = 48 * 1024 * 1024

MESH = pl.DeviceIdType.MESH
HBM_SPEC = pl.BlockSpec(memory_space=pltpu.HBM)


def _cparams(*sem):
    return pltpu.CompilerParams(dimension_semantics=sem, vmem_limit_bytes=VMEM_LIMIT)


def _sigmoid(x):
    return 1.0 / (1.0 + jnp.exp(-x))


def _softplus(x):
    return jnp.maximum(x, 0.0) + jnp.log(1.0 + jnp.exp(-jnp.abs(x)))


def _a2a(src, *, group, bcast, name):
    n = 2 if group == "c" else 4
    blk = tuple(src.shape) if bcast else tuple(src.shape[1:])

    def body(src_ref, out_ref, send_sems, recv_sems, local_sem):
        x, y, c = lax.axis_index("x"), lax.axis_index("y"), lax.axis_index("c")
        if group == "c":
            me = c

            def peer(d):
                return (x, y, 1 - c), 1 - c
        else:
            me = 2 * x + y

            def peer(d):
                px, py = x ^ (d >> 1), y ^ (d & 1)
                return (px, py, c), 2 * px + py

        def block_for(k):
            return src_ref if bcast else src_ref.at[k]

        local = pltpu.make_async_copy(block_for(me), out_ref.at[me], local_sem)
        local.start()
        sends = []
        for d in range(1, n):
            dev, idx = peer(d)
            cp = pltpu.make_async_remote_copy(
                src_ref=block_for(idx), dst_ref=out_ref.at[me],
                send_sem=send_sems.at[d], recv_sem=recv_sems.at[d],
                device_id=dev, device_id_type=MESH)
            cp.start()
            sends.append(cp)
        for d in range(1, n):
            dev, idx = peer(d)
            pltpu.make_async_remote_copy(
                src_ref=block_for(idx), dst_ref=out_ref.at[idx],
                send_sem=send_sems.at[d], recv_sem=recv_sems.at[d],
                device_id=dev, device_id_type=MESH).wait_recv()
        for cp in sends:
            cp.wait_send()
        local.wait()

    return pl.pallas_call(
        body, name=name,
        out_shape=jax.ShapeDtypeStruct((n,) + blk, src.dtype),
        in_specs=[HBM_SPEC], out_specs=HBM_SPEC,
        scratch_shapes=[pltpu.SemaphoreType.DMA((n,)), pltpu.SemaphoreType.DMA((n,)),
                        pltpu.SemaphoreType.DMA],
    )(src)


def _all_gather(piece, name):
    return _a2a(_a2a(piece, group="xy", bcast=True, name=name + "_xy"),
                group="c", bcast=True, name=name + "_c")


D2D_CHUNKS = 16
ICI_CHUNKS = 8


def _row_chunks(rows, dtype, k):
    unit = SUBLANES * (4 // jnp.dtype(dtype).itemsize)
    assert rows % unit == 0
    units = rows // unit
    k = max(1, min(k, units))
    base, rem = divmod(units, k)
    out, r = [], 0
    for i in range(k):
        n = (base + (1 if i < rem else 0)) * unit
        out.append((r, n))
        r += n
    return out


def _chunks(shape, dtype, k):
    if len(shape) == 2:
        return [(pl.ds(r0, n),) for r0, n in _row_chunks(shape[0], dtype, k)]
    per = max(1, k // shape[0])
    return [(l, pl.ds(r0, n)) for l in range(shape[0]) for r0, n in _row_chunks(shape[1], dtype, per)]


def _mesh_place():
    x, y, c = lax.axis_index("x"), lax.axis_index("y"), lax.axis_index("c")
    return x, y, c, 2 * x + y


def _chip_peer(x, y, c, d):
    px, py = x ^ (d >> 1), y ^ (d & 1)
    return (px, py, c), 2 * px + py


def _remote(src, dst, send_sem, recv_sem, dev):
    return pltpu.make_async_remote_copy(src_ref=src, dst_ref=dst, send_sem=send_sem,
                                        recv_sem=recv_sem, device_id=dev, device_id_type=MESH)


def _comm_call(body, name, ins, out_shapes, n_sems, aliases=None):
    n = len(ins)
    return pl.pallas_call(
        body, name=name,
        out_shape=out_shapes, in_specs=[HBM_SPEC] * n, out_specs=[HBM_SPEC] * n,
        input_output_aliases=aliases or {},
        scratch_shapes=[pltpu.SemaphoreType.DMA((n_sems, n)), pltpu.SemaphoreType.DMA((n_sems, n))],
    )(*ins)


class _Exchange:
    def __init__(self, kind, arrays):
        self.kind, self.arrays, self.n = kind, list(arrays), len(arrays)
        self.is_gather, self.all8 = kind.startswith("gather"), kind.endswith("8")
        k = ICI_CHUNKS // 4 if self.all8 else ICI_CHUNKS
        if self.is_gather:
            self.chunks = [_chunks(a.shape, a.dtype, k) for a in arrays]
            self.out_shapes = [jax.ShapeDtypeStruct((2, 4) + tuple(a.shape), a.dtype) for a in arrays]
        else:
            lead = 2 if self.all8 else 1
            self.chunks = [_chunks(a.shape[lead:], a.dtype, k) for a in arrays]
            self.out_shapes = [jax.ShapeDtypeStruct(a.shape, a.dtype) for a in arrays]
        self.peers = list(range(1, 8 if self.all8 else 4))
        n_sems = len(self.peers) + 1
        self.sem_shapes = [pltpu.SemaphoreType.DMA((n_sems, self.n)),
                           pltpu.SemaphoreType.DMA((n_sems, self.n))]

    def _peer(self, x, y, c, me, p):
        a, d = p // 4, p % 4
        px, py = x ^ (d >> 1), y ^ (d & 1)
        pc = 1 - c if a else c
        if self.all8:
            return (px, py, pc), (pc, 2 * px + py), (c, me)
        return (px, py, pc), (2 * px + py,), (me,)

    def _blocks(self, srcs, outs, o, c, me, theirs, mine):
        if self.kind == "gather":
            return srcs[o], outs[o].at[(c,) + mine], outs[o].at[(c,) + theirs]
        if self.kind == "gather8":
            return srcs[o], outs[o].at[mine], outs[o].at[theirs]
        return srcs[o].at[theirs], outs[o].at[mine], outs[o].at[theirs]

    def start(self, srcs, outs, send_sems, recv_sems):
        x, y, c, me = _mesh_place()
        if self.is_gather:
            for o in range(self.n):
                for idx in self.chunks[o]:
                    pltpu.make_async_copy(srcs[o].at[idx], outs[o].at[(c, me) + idx],
                                          send_sems.at[0, o]).start()
        for p in self.peers:
            dev, theirs, mine = self._peer(x, y, c, me, p)
            for o in range(self.n):
                src, dst, _ = self._blocks(srcs, outs, o, c, me, theirs, mine)
                for idx in self.chunks[o]:
                    _remote(src.at[idx], dst.at[idx], send_sems.at[p, o], recv_sems.at[p, o],
                            dev).start()

    def wait(self, srcs, outs, send_sems, recv_sems):
        x, y, c, me = _mesh_place()
        for wait_recv in (True, False):
            for p in self.peers:
                dev, theirs, mine = self._peer(x, y, c, me, p)
                for o in range(self.n):
                    src, _, land = self._blocks(srcs, outs, o, c, me, theirs, mine)
                    cp = _remote(src, land, send_sems.at[p, o], recv_sems.at[p, o], dev)
                    cp.wait_recv() if wait_recv else cp.wait_send()
        if self.is_gather:
            for o in range(self.n):
                pltpu.make_async_copy(srcs[o], outs[o].at[c, me], send_sems.at[0, o]).wait()


def _run_exchange(ex, name):
    n = ex.n

    def body(*refs):
        srcs, outs, send_sems, recv_sems = refs[:n], refs[n:2 * n], refs[2 * n], refs[2 * n + 1]
        ex.start(srcs, outs, send_sems, recv_sems)
        ex.wait(srcs, outs, send_sems, recv_sems)

    return _comm_call(body, name, ex.arrays, ex.out_shapes, len(ex.peers) + 1)


def _ag_c(bufs, name):
    n = len(bufs)
    chunks = [_chunks(b.shape[2:], b.dtype, D2D_CHUNKS // 4) for b in bufs]

    def body(*refs):
        srcs, outs, send_sems, recv_sems = refs[:n], refs[n:2 * n], refs[2 * n], refs[2 * n + 1]
        x, y, c, _ = _mesh_place()
        sib = (x, y, 1 - c)
        for o in range(n):
            for k in range(4):
                for idx in chunks[o]:
                    _remote(srcs[o].at[(c, k) + idx], outs[o].at[(c, k) + idx],
                            send_sems.at[0, o], recv_sems.at[0, o], sib).start()
        for o in range(n):
            _remote(srcs[o].at[c], outs[o].at[1 - c], send_sems.at[0, o], recv_sems.at[0, o],
                    sib).wait_recv()
        for o in range(n):
            _remote(srcs[o].at[c], outs[o].at[1 - c], send_sems.at[0, o], recv_sems.at[0, o],
                    sib).wait_send()

    shapes = [jax.ShapeDtypeStruct(b.shape, b.dtype) for b in bufs]
    return _comm_call(body, name, bufs, shapes, 1, aliases={i: i for i in range(n)})


def _rs_c(gs, name):
    n = len(gs)
    chunks = [_chunks(g.shape[2:], g.dtype, max(1, D2D_CHUNKS // g.shape[1])) for g in gs]

    def body(*refs):
        srcs, outs, send_sems, recv_sems = refs[:n], refs[n:2 * n], refs[2 * n], refs[2 * n + 1]
        x, y, c, _ = _mesh_place()
        sib = (x, y, 1 - c)
        for o in range(n):
            for k in range(gs[o].shape[1]):
                for idx in chunks[o]:
                    _remote(srcs[o].at[(1 - c, k) + idx], outs[o].at[(k,) + idx],
                            send_sems.at[0, o], recv_sems.at[0, o], sib).start()
        for o in range(n):
            _remote(srcs[o].at[1 - c], outs[o], send_sems.at[0, o], recv_sems.at[0, o],
                    sib).wait_recv()
        for o in range(n):
            _remote(srcs[o].at[1 - c], outs[o], send_sems.at[0, o], recv_sems.at[0, o],
                    sib).wait_send()

    shapes = [jax.ShapeDtypeStruct(g.shape[1:], g.dtype) for g in gs]
    return _comm_call(body, name, gs, shapes, 1)


def _matmul(a, b, *, trans_b, tm, tn, name, add=None, add_scale=1.0, host=None):
    a_parts = list(a) if isinstance(a, (list, tuple)) else [a]
    M, K = a_parts[0].shape[0], sum(p.shape[1] for p in a_parts)
    N = b.shape[0] if trans_b else b.shape[1]
    tm, tn = min(tm, M), min(tn, N)
    assert M % tm == 0 and N % tn == 0
    dn = (((1,), (1,)), ((), ())) if trans_b else (((1,), (0,)), ((), ()))
    na = len(a_parts)

    def body(*refs):
        a_refs, b_ref, o_ref = refs[:na], refs[na], refs[-1]
        av = [r[...].astype(BF16) for r in a_refs]
        av = av[0] if na == 1 else jnp.concatenate(av, axis=1)
        r = lax.dot_general(av, b_ref[...].astype(BF16), dn, preferred_element_type=F32)
        if add is not None:
            r = r + add_scale * refs[na + 1][...]
        o_ref[...] = r

    b_spec = (pl.BlockSpec((tn, K), lambda j, i: (j, 0)) if trans_b
              else pl.BlockSpec((K, tn), lambda j, i: (0, j)))
    in_specs = [pl.BlockSpec((tm, p.shape[1]), lambda j, i: (i, 0)) for p in a_parts] + [b_spec]
    args = a_parts + [b]
    if add is not None:
        in_specs.append(pl.BlockSpec((tm, tn), lambda j, i: (i, j)))
        args.append(add)
    grid = (N // tn, M // tm)
    x_in, x_out, x_shapes, x_scratch, x_args = _host_specs(host)
    body = _hosted(body, len(args), 1, 0, host, grid)
    outs = pl.pallas_call(
        body, name=name, grid=grid,
        in_specs=in_specs + x_in,
        out_specs=[pl.BlockSpec((tm, tn), lambda j, i: (i, j))] + x_out,
        out_shape=[jax.ShapeDtypeStruct((M, N), F32)] + x_shapes,
        scratch_shapes=x_scratch,
        compiler_params=_cparams(*(("arbitrary",) * 2 if host else ("parallel",) * 2)),
    )(*args, *x_args)
    return outs if host else outs[0]


def _matmul_tn(a, b, *, tm, tn, tk, name):
    T, M = a.shape
    N = b.shape[1]
    tm, tn, tk = min(tm, M), min(tn, N), min(tk, T)
    assert M % tm == 0 and N % tn == 0 and T % tk == 0

    def body(a_ref, b_ref, o_ref):
        @pl.when(pl.program_id(2) == 0)
        def _():
            o_ref[...] = jnp.zeros_like(o_ref)

        o_ref[...] += lax.dot_general(a_ref[...].astype(BF16), b_ref[...].astype(BF16),
                                      (((0,), (0,)), ((), ())), preferred_element_type=F32)

    return pl.pallas_call(
        body, name=name, grid=(M // tm, N // tn, T // tk),
        in_specs=[pl.BlockSpec((tk, tm), lambda i, j, k: (k, i)),
                  pl.BlockSpec((tk, tn), lambda i, j, k: (k, j))],
        out_specs=pl.BlockSpec((tm, tn), lambda i, j, k: (i, j)),
        out_shape=jax.ShapeDtypeStruct((M, N), F32),
        compiler_params=_cparams("parallel", "parallel", "arbitrary"),
    )(a, b)


def _matmul_tn_parts(a, parts, *, tm, tk, name):
    T, M = a.shape
    tm, tk = min(tm, M), min(tk, T)
    assert M % tm == 0 and T % tk == 0
    n = len(parts)

    def body(*refs):
        a_ref, b_refs, o_refs = refs[0], refs[1:1 + n], refs[1 + n:]
        av = a_ref[...].astype(BF16)
        for b_ref, o_ref in zip(b_refs, o_refs):
            @pl.when(pl.program_id(1) == 0)
            def _(o_ref=o_ref):
                o_ref[...] = jnp.zeros_like(o_ref)

            o_ref[...] += lax.dot_general(av, b_ref[...].astype(BF16), (((0,), (0,)), ((), ())),
                                          preferred_element_type=F32)

    return pl.pallas_call(
        body, name=name, grid=(M // tm, T // tk),
        in_specs=[pl.BlockSpec((tk, tm), lambda i, k: (k, i))]
        + [pl.BlockSpec((tk, p.shape[1]), lambda i, k: (k, 0)) for p in parts],
        out_specs=[pl.BlockSpec((tm, p.shape[1]), lambda i, k: (i, 0)) for p in parts],
        out_shape=[jax.ShapeDtypeStruct((M, p.shape[1]), F32) for p in parts],
        compiler_params=_cparams("parallel", "arbitrary"),
    )(a, *parts)


def _head_masks(rows):
    lane = lax.broadcasted_iota(jnp.int32, (rows, LANES), 1)
    return lane < HEAD_DIM, lane >= HEAD_DIM


def _causal(i_q, i_k, tq, tk):
    row = i_q * tq + lax.broadcasted_iota(jnp.int32, (tq, tk), 0)
    col = i_k * tk + lax.broadcasted_iota(jnp.int32, (tq, tk), 1)
    return row >= col


def _hosted(body, n_in, n_out, n_scratch, host, grid):
    if host is None:
        return body
    nx = host.n

    def wrapped(*refs):
        ins, xsrcs = refs[:n_in], refs[n_in:n_in + nx]
        outs = refs[n_in + nx:n_in + nx + n_out]
        xouts = refs[n_in + nx + n_out:n_in + 2 * nx + n_out]
        scratch = refs[n_in + 2 * nx + n_out:n_in + 2 * nx + n_out + n_scratch]
        xsems = refs[n_in + 2 * nx + n_out + n_scratch:]
        step = pl.program_id(0) * grid[1] + pl.program_id(1)

        @pl.when(step == 0)
        def _():
            host.start(xsrcs, xouts, *xsems)

        body(*ins, *outs, *scratch)

        @pl.when(step == grid[0] * grid[1] - 1)
        def _():
            host.wait(xsrcs, xouts, *xsems)

    return wrapped


def _host_specs(host):
    if host is None:
        return [], [], [], [], []
    return ([HBM_SPEC] * host.n, [HBM_SPEC] * host.n, host.out_shapes, host.sem_shapes, host.arrays)


def _flash_fwd(proj, cum4, *, tb, name, host=None):
    T = proj.shape[0]
    D = N_HEADS * HEAD_DIM
    nb = T // tb
    cb = D // LANES
    x_in, x_out, x_shapes, x_scratch, x_args = _host_specs(host)

    def body(q_ref, k_ref, v_ref, g_ref, cum_ref, o_ref, og_ref, lp_ref, kb_ref, vb_ref):
        i = pl.program_id(1)

        @pl.when(i == 0)
        def _():
            kb_ref[...] = k_ref[...].astype(BF16)
            vb_ref[...] = v_ref[...].astype(BF16)

        q = q_ref[...] * (HEAD_DIM ** -0.5)
        masks = _head_masks(tb)
        qh = [jnp.where(masks[h], q, 0.0).astype(BF16) for h in range(2)]
        cref = [cum_ref[0, h, pl.ds(i, 1), :][:, 0:1] for h in range(2)]

        def step(kbi, carry, masked):
            k0 = pl.multiple_of(kbi * tb, tb)
            kblk = kb_ref[pl.ds(k0, tb), :]
            vblk = vb_ref[pl.ds(k0, tb), :]
            new = []
            for h in range(2):
                m, l, acc = carry[h]
                s = lax.dot_general(qh[h], kblk, (((1,), (1,)), ((), ())),
                                    preferred_element_type=F32)
                s = s + (cref[h] - cum_ref[0, h, pl.ds(kbi, 1), :])
                if masked:
                    s = jnp.where(_causal(i, kbi, tb, tb), s, -jnp.inf)
                m_new = jnp.maximum(m, jnp.max(s, axis=-1, keepdims=True))
                alpha = jnp.exp(m - m_new)
                p = jnp.exp(s - m_new)
                l = alpha * l + jnp.sum(p, axis=-1, keepdims=True)
                acc = alpha * acc + jnp.dot(p.astype(BF16), vblk, preferred_element_type=F32)
                new.append((m_new, l, acc))
            return tuple(new)

        init1 = (jnp.full((tb, 1), -jnp.inf, F32), jnp.zeros((tb, 1), F32),
                 jnp.zeros((tb, LANES), F32))
        carry = lax.fori_loop(0, i, lambda kbi, c: step(kbi, c, False), (init1, init1))
        outs = []
        for h, (m, l, acc) in enumerate(step(i, carry, True)):
            outs.append(acc / l)
            lp_ref[h] = jnp.broadcast_to(m + jnp.log(l) - cref[h], (tb, LANES))
        o = jnp.where(masks[0], outs[0], outs[1])
        o_ref[...] = o
        gate = g_ref[...]
        og_ref[...] = (o * (gate * _sigmoid(gate))).astype(BF16)

    body = _hosted(body, 5, 3, 2, host, (N_PAIRS, nb))
    return pl.pallas_call(
        body, name=name, grid=(N_PAIRS, nb),
        in_specs=[pl.BlockSpec((tb, LANES), lambda j, i: (i, j)),
                  pl.BlockSpec((T, LANES), lambda j, i: (0, cb + j)),
                  pl.BlockSpec((T, LANES), lambda j, i: (0, 2 * cb + j)),
                  pl.BlockSpec((tb, LANES), lambda j, i: (i, 3 * cb + j)),
                  pl.BlockSpec((1, 2, nb, tb), lambda j, i: (j, 0, 0, 0))] + x_in,
        out_specs=[pl.BlockSpec((tb, LANES), lambda j, i: (i, j)),
                   pl.BlockSpec((tb, LANES), lambda j, i: (i, j)),
                   pl.BlockSpec((2, tb, LANES), lambda j, i: (j, i, 0))] + x_out,
        out_shape=[jax.ShapeDtypeStruct((T, D), F32), jax.ShapeDtypeStruct((T, D), BF16),
                   jax.ShapeDtypeStruct((N_HEADS, T, LANES), F32)] + x_shapes,
        scratch_shapes=[pltpu.VMEM((T, LANES), BF16), pltpu.VMEM((T, LANES), BF16)] + x_scratch,
        compiler_params=_cparams("arbitrary", "arbitrary"),
    )(proj, proj, proj, proj, cum4, *x_args)


def _flash_bwd(proj, cum4, o, dog, lp, *, tb, name, host=None):
    T = proj.shape[0]
    D = N_HEADS * HEAD_DIM
    nb = T // tb
    cb = D // LANES
    x_in, x_out, x_shapes, x_scratch, x_args = _host_specs(host)

    def body(q_ref, k_ref, v_ref, g_ref, cum_ref, o_ref, dog_ref, lp_ref,
             dq_ref, dg_ref, dk_ref, dv_ref, dcq_ref, dck_ref,
             kb_ref, vb_ref, dka_ref, dva_ref, dca_ref):
        i = pl.program_id(1)

        @pl.when(i == 0)
        def _():
            kb_ref[...] = k_ref[...].astype(BF16)
            vb_ref[...] = v_ref[...].astype(BF16)
            dka_ref[...] = jnp.zeros_like(dka_ref)
            dva_ref[...] = jnp.zeros_like(dva_ref)
            dca_ref[...] = jnp.zeros_like(dca_ref)

        gate = g_ref[...]
        sg = _sigmoid(gate)
        o = o_ref[...]
        dog = dog_ref[...]
        do = dog * (gate * sg)
        dg_ref[...] = (dog * o * (sg * (1.0 + gate * (1.0 - sg)))).astype(BF16)
        q = q_ref[...] * (HEAD_DIM ** -0.5)
        masks = _head_masks(tb)
        qh = [jnp.where(masks[h], q, 0.0).astype(BF16) for h in range(2)]
        doh = [jnp.where(masks[h], do, 0.0).astype(BF16) for h in range(2)]
        delta = [jnp.sum(jnp.where(masks[h], do * o, 0.0), axis=-1, keepdims=True) for h in range(2)]
        lph = [lp_ref[h][:, 0:1] for h in range(2)]

        def step(kbi, carry, masked):
            k0 = pl.multiple_of(kbi * tb, tb)
            kblk = kb_ref[pl.ds(k0, tb), :]
            vblk = vb_ref[pl.ds(k0, tb), :]
            new, dk, dv = [], None, None
            for h in range(2):
                acc, rs = carry[h]
                s = lax.dot_general(qh[h], kblk, (((1,), (1,)), ((), ())), preferred_element_type=F32)
                p = jnp.exp(s - cum_ref[0, h, pl.ds(kbi, 1), :] - lph[h])
                if masked:
                    p = jnp.where(_causal(i, kbi, tb, tb), p, 0.0)
                dp = lax.dot_general(doh[h], vblk, (((1,), (1,)), ((), ())),
                                     preferred_element_type=F32)
                ds = p * (dp - delta[h])
                pb, dsb = p.astype(BF16), ds.astype(BF16)
                dv_h = lax.dot_general(pb, doh[h], (((0,), (0,)), ((), ())),
                                       preferred_element_type=F32)
                dk_h = lax.dot_general(dsb, qh[h], (((0,), (0,)), ((), ())),
                                       preferred_element_type=F32)
                dv = dv_h if dv is None else dv + dv_h
                dk = dk_h if dk is None else dk + dk_h
                dca_ref[h, pl.ds(kbi, 1), :] -= jnp.sum(ds, axis=0, keepdims=True)
                new.append((acc + jnp.dot(dsb, kblk, preferred_element_type=F32),
                            rs + jnp.sum(ds, axis=-1, keepdims=True)))
            dka_ref[pl.ds(k0, tb), :] += dk
            dva_ref[pl.ds(k0, tb), :] += dv
            return tuple(new)

        init1 = (jnp.zeros((tb, LANES), F32), jnp.zeros((tb, 1), F32))
        carry = lax.fori_loop(0, i, lambda kbi, c: step(kbi, c, False), (init1, init1))
        dqs = []
        for h, (acc, rs) in enumerate(step(i, carry, True)):
            dqs.append(acc)
            dcq_ref[0, 0, pl.ds(h, 1), :] = jnp.broadcast_to(rs, (tb, LANES)).T[0:1, :]
        dq_ref[...] = (jnp.where(masks[0], dqs[0], dqs[1]) * (HEAD_DIM ** -0.5)).astype(BF16)

        @pl.when(i == nb - 1)
        def _():
            dk_ref[...] = dka_ref[...].astype(BF16)
            dv_ref[...] = dva_ref[...].astype(BF16)
            dck_ref[0] = dca_ref[...]

    blk = pl.BlockSpec((tb, LANES), lambda j, i: (i, j))
    full = pl.BlockSpec((T, LANES), lambda j, i: (0, j))
    body = _hosted(body, 8, 6, 5, host, (N_PAIRS, nb))
    return pl.pallas_call(
        body, name=name, grid=(N_PAIRS, nb),
        in_specs=[blk,
                  pl.BlockSpec((T, LANES), lambda j, i: (0, cb + j)),
                  pl.BlockSpec((T, LANES), lambda j, i: (0, 2 * cb + j)),
                  pl.BlockSpec((tb, LANES), lambda j, i: (i, 3 * cb + j)),
                  pl.BlockSpec((1, 2, nb, tb), lambda j, i: (j, 0, 0, 0)),
                  blk, blk, pl.BlockSpec((2, tb, LANES), lambda j, i: (j, i, 0))] + x_in,
        out_specs=[blk, blk, full, full,
                   pl.BlockSpec((1, 1, 2, tb), lambda j, i: (j, i, 0, 0)),
                   pl.BlockSpec((1, 2, nb, tb), lambda j, i: (j, 0, 0, 0))] + x_out,
        out_shape=[jax.ShapeDtypeStruct((T, D), BF16)] * 4
        + [jax.ShapeDtypeStruct((N_PAIRS, nb, 2, tb), F32),
           jax.ShapeDtypeStruct((N_PAIRS, 2, nb, tb), F32)] + x_shapes,
        scratch_shapes=[pltpu.VMEM((T, LANES), BF16), pltpu.VMEM((T, LANES), BF16),
                        pltpu.VMEM((T, LANES), F32), pltpu.VMEM((T, LANES), F32),
                        pltpu.VMEM((2, nb, tb), F32)] + x_scratch,
        compiler_params=_cparams("arbitrary", "arbitrary"),
    )(proj, proj, proj, proj, cum4, o, dog, lp, *x_args)


def _cumsum_fwd(proj, bf_row, *, tt, name):
    T = proj.shape[0]
    cb = (proj.shape[1] - LANES) // LANES

    def body(f_ref, b_ref, out_ref, carry_ref):
        i = pl.program_id(0)

        @pl.when(i == 0)
        def _():
            carry_ref[...] = jnp.zeros_like(carry_ref)

        ls = -_softplus(-(f_ref[...] + b_ref[...]))
        tri = (lax.broadcasted_iota(jnp.int32, (tt, tt), 0)
               >= lax.broadcasted_iota(jnp.int32, (tt, tt), 1)).astype(F32)
        cum = jnp.dot(tri, ls, preferred_element_type=F32,
                      precision=lax.Precision.HIGHEST) + carry_ref[...]
        carry_ref[...] = cum[tt - 1:tt, :]
        out_ref[...] = cum.T

    return pl.pallas_call(
        body, name=name, grid=(T // tt,),
        in_specs=[pl.BlockSpec((tt, LANES), lambda i: (i, cb)),
                  pl.BlockSpec((1, LANES), lambda i: (0, 0))],
        out_specs=pl.BlockSpec((LANES, tt), lambda i: (0, i)),
        out_shape=jax.ShapeDtypeStruct((LANES, T), F32),
        scratch_shapes=[pltpu.VMEM((1, LANES), F32)],
        compiler_params=_cparams("arbitrary"),
    )(proj, bf_row)


def _cumsum_bwd(dcum_t, proj, bf_row, *, tt, name):
    T = proj.shape[0]
    cb = (proj.shape[1] - LANES) // LANES
    nt = T // tt

    def body(dc_ref, f_ref, b_ref, df_ref, db_ref, carry_ref):
        i = pl.program_id(0)

        @pl.when(i == 0)
        def _():
            carry_ref[...] = jnp.zeros_like(carry_ref)
            db_ref[...] = jnp.zeros_like(db_ref)

        dc = dc_ref[...].T
        tri = (lax.broadcasted_iota(jnp.int32, (tt, tt), 0)
               <= lax.broadcasted_iota(jnp.int32, (tt, tt), 1)).astype(F32)
        rev = jnp.dot(tri, dc, preferred_element_type=F32,
                      precision=lax.Precision.HIGHEST) + carry_ref[...]
        carry_ref[...] = rev[0:1, :]
        df = rev * _sigmoid(-(f_ref[...] + b_ref[...]))
        df_ref[...] = df.astype(BF16)
        db_ref[...] += jnp.sum(df, axis=0, keepdims=True)

    return pl.pallas_call(
        body, name=name, grid=(nt,),
        in_specs=[pl.BlockSpec((LANES, tt), lambda i: (0, nt - 1 - i)),
                  pl.BlockSpec((tt, LANES), lambda i: (nt - 1 - i, cb)),
                  pl.BlockSpec((1, LANES), lambda i: (0, 0))],
        out_specs=[pl.BlockSpec((tt, LANES), lambda i: (nt - 1 - i, 0)),
                   pl.BlockSpec((1, LANES), lambda i: (0, 0))],
        out_shape=[jax.ShapeDtypeStruct((T, LANES), BF16), jax.ShapeDtypeStruct((1, LANES), F32)],
        scratch_shapes=[pltpu.VMEM((1, LANES), F32)],
        compiler_params=_cparams("arbitrary"),
    )(dcum_t, proj, bf_row)


def _rows_down(x, before, sh):
    if sh == 0:
        return x
    rolled = pltpu.roll(x, sh, axis=0)
    row = lax.broadcasted_iota(jnp.int32, (SUBLANES, x.shape[1]), 0)
    head = jnp.where(row < sh, pltpu.roll(before, sh, axis=0), rolled[:SUBLANES])
    return jnp.concatenate([head, rolled[SUBLANES:]], axis=0)


def _rows_up(x, after, sh):
    if sh == 0:
        return x
    tt = x.shape[0]
    rolled = pltpu.roll(x, tt - sh, axis=0)
    row = lax.broadcasted_iota(jnp.int32, (SUBLANES, x.shape[1]), 0)
    tail = jnp.where(row >= SUBLANES - sh, pltpu.roll(after, SUBLANES - sh, axis=0),
                     rolled[tt - SUBLANES:])
    return jnp.concatenate([rolled[:tt - SUBLANES], tail], axis=0)


def _rg_gates(u0, before, small_ref, wa_ref, wi_ref):
    taps = [_rows_down(u0, before, CONV_WIDTH - 1 - tap) for tap in range(CONV_WIDTH)]
    u = small_ref[4:5, :]
    for tap in range(CONV_WIDTH):
        u = u + taps[tap] * small_ref[tap:tap + 1, :]
    pa, pi = [], []
    for n in range(RNN_BLOCKS):
        ub = u[:, n * RNN_BLOCK_WIDTH:(n + 1) * RNN_BLOCK_WIDTH].astype(BF16)
        pa.append(jnp.dot(ub, wa_ref[n], preferred_element_type=F32))
        pi.append(jnp.dot(ub, wi_ref[n], preferred_element_type=F32))
    r = _sigmoid(jnp.concatenate(pa, axis=-1) + small_ref[5:6, :])
    ig = _sigmoid(jnp.concatenate(pi, axis=-1) + small_ref[6:7, :])
    spl = _softplus(-small_ref[7:8, :])
    log_a = (-LRU_C) * r * spl
    a = jnp.exp(log_a)
    s2 = jnp.tanh(-log_a) * (a * a + 1.0)
    inv_s = lax.rsqrt(s2)
    s = jnp.where(s2 > 0.0, s2 * inv_s, 0.0)
    return u, taps, r, ig, spl, a, s, inv_s


def _rg_fwd(proj, small, wa, wi, *, tt, name):
    T = proj.shape[0]
    D = RNN_BLOCKS * RNN_BLOCK_WIDTH
    hb = tt // SUBLANES

    def body(u0_ref, halo_ref, g_ref, small_ref, wa_ref, wi_ref, h_ref, y_ref,
             a_ref, b_ref, carry_ref):
        i = pl.program_id(0)

        @pl.when(i == 0)
        def _():
            carry_ref[...] = jnp.zeros_like(carry_ref)

        before = jnp.where(i == 0, 0.0, halo_ref[...])
        u, _, r, ig, spl, a, s, _ = _rg_gates(u0_ref[...], before, small_ref, wa_ref, wi_ref)
        a_ref[...] = a
        b_ref[...] = s * (ig * u)

        def row(t, h):
            h = a_ref[pl.ds(t, 1), :] * h + b_ref[pl.ds(t, 1), :]
            h_ref[pl.ds(t, 1), :] = h
            return h

        carry_ref[...] = lax.fori_loop(0, tt, row, carry_ref[...])
        gate = g_ref[...]
        y_ref[...] = (h_ref[...] * (gate * _sigmoid(gate))).astype(BF16)

    return pl.pallas_call(
        body, name=name, grid=(T // tt,),
        in_specs=[pl.BlockSpec((tt, D), lambda i: (i, 0)),
                  pl.BlockSpec((SUBLANES, D), lambda i: (jnp.maximum(i * hb - 1, 0), 0)),
                  pl.BlockSpec((tt, D), lambda i: (i, 1)),
                  pl.BlockSpec((SUBLANES, D), lambda i: (0, 0)),
                  pl.BlockSpec((RNN_BLOCKS, RNN_BLOCK_WIDTH, RNN_BLOCK_WIDTH), lambda i: (0, 0, 0)),
                  pl.BlockSpec((RNN_BLOCKS, RNN_BLOCK_WIDTH, RNN_BLOCK_WIDTH), lambda i: (0, 0, 0))],
        out_specs=[pl.BlockSpec((tt, D), lambda i: (i, 0)), pl.BlockSpec((tt, D), lambda i: (i, 0))],
        out_shape=[jax.ShapeDtypeStruct((T, D), F32), jax.ShapeDtypeStruct((T, D), BF16)],
        scratch_shapes=[pltpu.VMEM((tt, D), F32), pltpu.VMEM((tt, D), F32),
                        pltpu.VMEM((1, D), F32)],
        compiler_params=_cparams("arbitrary"),
    )(proj, proj, proj, small, wa, wi)


def _rg_bwd(proj, hs, dy, small, wa, wi, *, tt, name):
    T = proj.shape[0]
    D = RNN_BLOCKS * RNN_BLOCK_WIDTH
    W = RNN_BLOCK_WIDTH
    hb = tt // SUBLANES
    nt = T // tt

    def body(u0_ref, uhalo_ref, g_ref, h_ref, hhalo_ref, dy_ref, small_ref, wa_ref, wi_ref,
             dp_ref, dwa_ref, dwi_ref, ds_ref,
             a_ref, g_s_ref, dunext_ref, carry_ref):
        i = pl.program_id(0)
        first_chunk = i == nt - 1

        @pl.when(i == 0)
        def _():
            carry_ref[...] = jnp.zeros_like(carry_ref)
            dunext_ref[...] = jnp.zeros_like(dunext_ref)
            dwa_ref[...] = jnp.zeros_like(dwa_ref)
            dwi_ref[...] = jnp.zeros_like(dwi_ref)
            ds_ref[...] = jnp.zeros_like(ds_ref)

        u_before = jnp.where(first_chunk, 0.0, uhalo_ref[...])
        h_before = jnp.where(first_chunk, 0.0, hhalo_ref[...])
        u, taps, r, ig, spl, a, s, inv_s = _rg_gates(u0_ref[...], u_before, small_ref, wa_ref,
                                                     wi_ref)
        gate = g_ref[...]
        sg = _sigmoid(gate)
        dy = dy_ref[...]
        dp_ref[:, D:] = (dy * h_ref[...] * (sg * (1.0 + gate * (1.0 - sg)))).astype(BF16)
        a_ref[...] = a
        g_s_ref[...] = dy * (gate * sg)

        def row(k, c):
            t = tt - 1 - k
            g = g_s_ref[pl.ds(t, 1), :] + c
            g_s_ref[pl.ds(t, 1), :] = g
            return a_ref[pl.ds(t, 1), :] * g

        carry_ref[...] = lax.fori_loop(0, tt, row, carry_ref[...])
        g = g_s_ref[...]
        h_prev = _rows_down(h_ref[...], h_before, 1)
        iu = ig * u
        d_iu = g * s
        dlog_a = (g * h_prev) * a - (g * iu) * (a * a) * inv_s
        dpre_a = (dlog_a * ((-LRU_C) * spl)) * r * (1.0 - r)
        dpre_i = (d_iu * u) * ig * (1.0 - ig)
        dlam = jnp.sum(dlog_a * r, axis=0, keepdims=True) * (LRU_C * _sigmoid(-small_ref[7:8, :]))
        du_parts = []
        for n in range(RNN_BLOCKS):
            sl = slice(n * W, (n + 1) * W)
            ub = u[:, sl].astype(BF16)
            da_n = dpre_a[:, sl].astype(BF16)
            di_n = dpre_i[:, sl].astype(BF16)
            dwa_ref[n] += lax.dot_general(ub, da_n, (((0,), (0,)), ((), ())),
                                          preferred_element_type=F32)
            dwi_ref[n] += lax.dot_general(ub, di_n, (((0,), (0,)), ((), ())),
                                          preferred_element_type=F32)
            du_parts.append(
                lax.dot_general(da_n, wa_ref[n], (((1,), (1,)), ((), ())), preferred_element_type=F32)
                + lax.dot_general(di_n, wi_ref[n], (((1,), (1,)), ((), ())), preferred_element_type=F32))
        du = d_iu * ig + jnp.concatenate(du_parts, axis=-1)
        for tap in range(CONV_WIDTH):
            ds_ref[tap:tap + 1, :] += jnp.sum(du * taps[tap], axis=0, keepdims=True)
        ds_ref[4:5, :] += jnp.sum(du, axis=0, keepdims=True)
        ds_ref[5:6, :] += jnp.sum(dpre_a, axis=0, keepdims=True)
        ds_ref[6:7, :] += jnp.sum(dpre_i, axis=0, keepdims=True)
        ds_ref[7:8, :] += dlam
        du_after = dunext_ref[...]
        du0 = jnp.zeros((tt, D), F32)
        for tap in range(CONV_WIDTH):
            du0 = du0 + _rows_up(du, du_after, CONV_WIDTH - 1 - tap) * small_ref[tap:tap + 1, :]
        dp_ref[:, :D] = du0.astype(BF16)
        dunext_ref[...] = du[0:SUBLANES, :]

    rev = lambda i: nt - 1 - i
    wspec = pl.BlockSpec((RNN_BLOCKS, W, W), lambda i: (0, 0, 0))
    return pl.pallas_call(
        body, name=name, grid=(nt,),
        in_specs=[pl.BlockSpec((tt, D), lambda i: (rev(i), 0)),
                  pl.BlockSpec((SUBLANES, D), lambda i: (jnp.maximum(rev(i) * hb - 1, 0), 0)),
                  pl.BlockSpec((tt, D), lambda i: (rev(i), 1)),
                  pl.BlockSpec((tt, D), lambda i: (rev(i), 0)),
                  pl.BlockSpec((SUBLANES, D), lambda i: (jnp.maximum(rev(i) * hb - 1, 0), 0)),
                  pl.BlockSpec((tt, D), lambda i: (rev(i), 0)),
                  pl.BlockSpec((SUBLANES, D), lambda i: (0, 0)),
                  wspec, wspec],
        out_specs=[pl.BlockSpec((tt, 2 * D), lambda i: (rev(i), 0)),
                   wspec, wspec, pl.BlockSpec((SUBLANES, D), lambda i: (0, 0))],
        out_shape=[jax.ShapeDtypeStruct((T, 2 * D), BF16),
                   jax.ShapeDtypeStruct((RNN_BLOCKS, W, W), F32),
                   jax.ShapeDtypeStruct((RNN_BLOCKS, W, W), F32),
                   jax.ShapeDtypeStruct((SUBLANES, D), F32)],
        scratch_shapes=[pltpu.VMEM((tt, D), F32), pltpu.VMEM((tt, D), F32),
                        pltpu.VMEM((SUBLANES, D), F32), pltpu.VMEM((1, D), F32)],
        compiler_params=_cparams("arbitrary"),
    )(proj, proj, proj, hs, hs, dy, small, wa, wi)


def _out_ln(a, w, x, g, b, *, tt, name):
    T, D = x.shape
    K = a.shape[1]

    def body(a_ref, w_ref, x_ref, g_ref, b_ref, y_ref, yb_ref, zh_ref, rs_ref):
        h = jnp.dot(a_ref[...].astype(BF16), w_ref[...].astype(BF16), preferred_element_type=F32)
        z = ALPHA * x_ref[...] + h
        mu = jnp.mean(z, axis=-1, keepdims=True)
        zc = z - mu
        rstd = lax.rsqrt(jnp.mean(zc * zc, axis=-1, keepdims=True) + LN_EPS)
        zh = zc * rstd
        zh_ref[...] = zh
        rs_ref[...] = rstd
        y = zh * g_ref[...] + b_ref[...]
        y_ref[...] = y
        yb_ref[...] = y.astype(BF16)

    blk = pl.BlockSpec((tt, D), lambda i: (i, 0))
    row = pl.BlockSpec((1, D), lambda i: (0, 0))
    return pl.pallas_call(
        body, name=name, grid=(T // tt,),
        in_specs=[pl.BlockSpec((tt, K), lambda i: (i, 0)), pl.BlockSpec((K, D), lambda i: (0, 0)),
                  blk, row, row],
        out_specs=[blk, blk, blk, pl.BlockSpec((tt, 1), lambda i: (i, 0))],
        out_shape=[jax.ShapeDtypeStruct((T, D), F32), jax.ShapeDtypeStruct((T, D), BF16),
                   jax.ShapeDtypeStruct((T, D), F32), jax.ShapeDtypeStruct((T, 1), F32)],
        compiler_params=_cparams("parallel"),
    )(a, w, x, g, b)


def _ln_bwd_tile(dy, zh_ref, rs_ref, g_ref, dz_ref, dzb_ref, dg_ref, db_ref, first):
    @pl.when(first)
    def _():
        dg_ref[...] = jnp.zeros_like(dg_ref)
        db_ref[...] = jnp.zeros_like(db_ref)

    zh = zh_ref[...]
    dg_ref[...] += jnp.sum(dy * zh, axis=0, keepdims=True)
    db_ref[...] += jnp.sum(dy, axis=0, keepdims=True)
    dzh = dy * g_ref[...]
    m1 = jnp.mean(dzh, axis=-1, keepdims=True)
    m2 = jnp.mean(dzh * zh, axis=-1, keepdims=True)
    dz = rs_ref[...] * (dzh - m1 - zh * m2)
    dz_ref[...] = dz
    dzb_ref[...] = dz.astype(BF16)


def _ln_bwd_specs(T, D, tt):
    blk = pl.BlockSpec((tt, D), lambda i: (i, 0))
    row = pl.BlockSpec((1, D), lambda i: (0, 0))
    return ([blk, pl.BlockSpec((tt, 1), lambda i: (i, 0)), row], [blk, blk, row, row],
            [jax.ShapeDtypeStruct((T, D), F32), jax.ShapeDtypeStruct((T, D), BF16),
             jax.ShapeDtypeStruct((1, D), F32), jax.ShapeDtypeStruct((1, D), F32)])


def _loss_ln_bwd(y, tgt, zh, rstd, g, *, tt, name):
    T, D = y.shape
    ln_in, ln_out, ln_shapes = _ln_bwd_specs(T, D, tt)

    def body(y_ref, t_ref, zh_ref, rs_ref, g_ref, l_ref, dz_ref, dzb_ref, dg_ref, db_ref):
        first = pl.program_id(0) == 0

        @pl.when(first)
        def _():
            l_ref[...] = jnp.zeros_like(l_ref)

        e = y_ref[...] - t_ref[...]
        l_ref[...] += jnp.sum(e * e, axis=0, keepdims=True) * (0.5 / D)
        _ln_bwd_tile(e * (1.0 / D), zh_ref, rs_ref, g_ref, dz_ref, dzb_ref, dg_ref, db_ref, first)

    blk = pl.BlockSpec((tt, D), lambda i: (i, 0))
    return pl.pallas_call(
        body, name=name, grid=(T // tt,),
        in_specs=[blk, blk] + ln_in,
        out_specs=[pl.BlockSpec((1, D), lambda i: (0, 0))] + ln_out,
        out_shape=[jax.ShapeDtypeStruct((1, D), F32)] + ln_shapes,
        compiler_params=_cparams("arbitrary"),
    )(y, tgt, zh, rstd, g)


def _dx_ln_bwd(a, b, add, zh, rstd, g, *, tm, name):
    T, D = add.shape
    na = len(a)
    K = sum(p.shape[1] for p in a)
    ln_in, ln_out, ln_shapes = _ln_bwd_specs(T, D, tm)

    def body(*refs):
        a_refs, b_ref, add_ref = refs[:na], refs[na], refs[na + 1]
        av = [r[...].astype(BF16) for r in a_refs]
        av = av[0] if na == 1 else jnp.concatenate(av, axis=1)
        dy = lax.dot_general(av, b_ref[...].astype(BF16), (((1,), (1,)), ((), ())),
                             preferred_element_type=F32) + ALPHA * add_ref[...]
        _ln_bwd_tile(dy, *refs[na + 2:], pl.program_id(0) == 0)

    return pl.pallas_call(
        body, name=name, grid=(T // tm,),
        in_specs=[pl.BlockSpec((tm, p.shape[1]), lambda i: (i, 0)) for p in a]
        + [pl.BlockSpec((D, K), lambda i: (0, 0)), pl.BlockSpec((tm, D), lambda i: (i, 0))] + ln_in,
        out_specs=ln_out, out_shape=ln_shapes,
        compiler_params=_cparams("arbitrary"),
    )(*a, b, add, zh, rstd, g)


def _row_tile(rows, target):
    best = SUBLANES
    for t in range(SUBLANES, target + 1, SUBLANES):
        if rows % t == 0:
            best = t
    return best


def _add_own(g, recv, c_idx, *, tr, name):
    _, M, R, C = g.shape

    def body(c_ref, g_ref, r_ref, o_ref, ob_ref):
        s = g_ref[0] + r_ref[...]
        o_ref[...] = s
        ob_ref[...] = s.astype(BF16)

    blk = pl.BlockSpec((1, tr, C), lambda k, i, c: (k, i, 0))
    return pl.pallas_call(
        body, name=name,
        grid_spec=pltpu.PrefetchScalarGridSpec(
            num_scalar_prefetch=1, grid=(M, R // tr),
            in_specs=[pl.BlockSpec((1, 1, tr, C), lambda k, i, c: (c[0], k, i, 0)), blk],
            out_specs=[blk, blk]),
        out_shape=[jax.ShapeDtypeStruct((M, R, C), F32), jax.ShapeDtypeStruct((M, R, C), BF16)],
        compiler_params=_cparams("parallel", "parallel"),
    )(c_idx, g, recv)


def _adamw_math(g, w_ref, m_ref, v_ref, g_ref, d_ref, nm_ref, nv_ref):
    nm = ADAM_B1 * m_ref[...] + (1.0 - ADAM_B1) * g
    nv = ADAM_B2 * v_ref[...] + (1.0 - ADAM_B2) * (g * g)
    m_hat = nm / (1.0 - ADAM_B1 ** ADAM_STEP)
    v_hat = nv / (1.0 - ADAM_B2 ** ADAM_STEP)
    g_ref[...] = g
    nm_ref[...] = nm
    nv_ref[...] = nv
    d_ref[...] = (-ADAM_LR) * (m_hat / (jnp.sqrt(v_hat) + ADAM_EPS) + ADAM_WD * w_ref[...])


def _adamw(parts, w, m, v, *, tr, name):
    n, R, C = parts.shape
    tr = min(tr, R)

    def body(p_ref, w_ref, m_ref, v_ref, *out_refs):
        g = p_ref[0]
        for k in range(1, n):
            g = g + p_ref[k]
        _adamw_math(g, w_ref, m_ref, v_ref, *out_refs)

    blk = pl.BlockSpec((tr, C), lambda i: (i, 0))
    out = jax.ShapeDtypeStruct((R, C), F32)
    return pl.pallas_call(
        body, name=name, grid=(R // tr,),
        in_specs=[pl.BlockSpec((n, tr, C), lambda i: (0, i, 0)), blk, blk, blk],
        out_specs=[blk, blk, blk, blk], out_shape=[out, out, out, out],
        compiler_params=_cparams("parallel"),
    )(parts, w, m, v)


def _adamw_shard(parts_by_layer, place, w, m, v, *, tr, name):
    L, R, C = w.shape
    flat = [(l, a, pick) for l, parts in enumerate(parts_by_layer) for a, pick in parts]
    n = len(flat)

    def body(place_ref, *refs):
        w_ref, m_ref, v_ref = refs[n:n + 3]
        for layer in range(L):
            @pl.when(pl.program_id(0) == layer)
            def _(layer=layer):
                g = None
                for (l, _, _), r in zip(flat, refs[:n]):
                    if l == layer:
                        blk = r[(0,) * (len(r.shape) - 3)].astype(F32)
                        g = blk if g is None else g + blk
                _adamw_math(g, w_ref, m_ref, v_ref, *refs[n + 3:])

    blk = pl.BlockSpec((1, tr, C), lambda ly, i, s: (ly, i, 0))

    def part_spec(l, a, pick):
        return pl.BlockSpec((1,) * (a.ndim - 2) + (tr, C),
                            lambda ly, i, s: (*pick(s), jnp.where(ly == l, i, 0), 0))

    out = jax.ShapeDtypeStruct(w.shape, F32)
    return pl.pallas_call(
        body, name=name,
        grid_spec=pltpu.PrefetchScalarGridSpec(
            num_scalar_prefetch=1, grid=(L, R // tr),
            in_specs=[part_spec(*f) for f in flat] + [blk, blk, blk],
            out_specs=[blk, blk, blk, blk]),
        out_shape=[out, out, out, out],
        compiler_params=_cparams("arbitrary", "arbitrary"),
    )(place, *[a for _, a, _ in flat], w, m, v)


def _two_stage_parts(h, recv):
    return [(h, lambda s: (s[0], 0))] + [(recv, lambda s, d=d: (s[0] ^ d, 0)) for d in (1, 2, 3)]


def _direct_parts(g, recv):
    return [(g, lambda s: (s[1], s[0]))] + [
        (recv, lambda s, a=p // 4, d=p % 4: (s[1] ^ a, s[0] ^ d)) for p in range(1, 8)]


SHARD_AXIS = dict(attn_w_in=1, attn_w_out=0, rnn_w_in=1, rnn_w_out=0, rnn_w_a=1, rnn_w_i=1,
                  rnn_conv_w=1, rnn_conv_b=0, rnn_b_a=0, rnn_b_i=0, rnn_lambda=0)
RNN_ROWED = ("rnn_w_out", "rnn_w_a", "rnn_w_i")
SMALL = ("rnn_conv_w", "rnn_conv_b", "rnn_b_a", "rnn_b_i", "rnn_lambda")
PACK_C = 1024


def _elems(shape):
    n = 1
    for s in shape:
        n *= s
    return n


def _pack_rows(p, idx, dtype):
    parts = [p[k][idx].astype(dtype).reshape(-1, PACK_C) for k in RNN_ROWED]
    small = jnp.concatenate([p[k][idx].reshape(-1) for k in SMALL])
    tile_rows = SUBLANES * (4 // jnp.dtype(dtype).itemsize)
    if dtype == BF16:
        small = lax.bitcast_convert_type(small, BF16)
    small = small.reshape(-1, PACK_C)
    parts.append(jnp.pad(small, ((0, tile_rows - small.shape[0]), (0, 0))))
    return jnp.concatenate(parts, axis=0)


def _unpack_rows(flat, shapes):
    lead = flat.shape[:-2]
    out, r = {}, 0
    for k in RNN_ROWED:
        n = _elems(shapes[k]) // PACK_C
        out[k] = flat[..., r:r + n, :].reshape(lead + shapes[k])
        r += n
    n_small = sum(_elems(shapes[k]) for k in SMALL)
    small = flat[..., r:r + n_small // PACK_C, :].reshape(lead + (-1,))
    o = 0
    for k in SMALL:
        n = _elems(shapes[k])
        out[k] = small[..., o:o + n].reshape(lead + shapes[k])
        o += n
    return out


def _join_columns(g, width, *, tr, name):
    _, _, R, S = g.shape

    def body(*refs):
        o_ref = refs[8]
        parts = [refs[r][0, 0].astype(F32) for r in range(8)]
        if width > 8 * S:
            parts.append(jnp.zeros((tr, width - 8 * S), F32))
        o_ref[...] = jnp.concatenate(parts, axis=-1).astype(o_ref.dtype)

    def shard(r):
        return pl.BlockSpec((1, 1, tr, S), lambda i: (r % 2, r // 2, i, 0))

    return pl.pallas_call(
        body, name=name, grid=(R // tr,),
        in_specs=[shard(r) for r in range(8)],
        out_specs=pl.BlockSpec((tr, width), lambda i: (i, 0)),
        out_shape=jax.ShapeDtypeStruct((R, width), g.dtype),
        compiler_params=_cparams("parallel"),
    )(*([g] * 8))


def _split_columns(parts, S, *, tr, name):
    R = parts[0].shape[0]
    n = len(parts)

    def body(*refs):
        o_ref = refs[n]
        x = jnp.concatenate([r[...] for r in refs[:n]], axis=1)
        for r in range(8):
            o_ref[r % 2, r // 2] = x[:, r * S:(r + 1) * S]

    return pl.pallas_call(
        body, name=name, grid=(R // tr,),
        in_specs=[pl.BlockSpec((tr, p.shape[1]), lambda i: (i, 0)) for p in parts],
        out_specs=pl.BlockSpec((2, 4, tr, S), lambda i: (0, 0, i, 0)),
        out_shape=jax.ShapeDtypeStruct((2, 4, R, S), parts[0].dtype),
        compiler_params=_cparams("parallel"),
    )(*parts)


def _to_full(g, k, sh):
    ax, nd = SHARD_AXIS[k], len(sh)
    perm = tuple(range(2, 2 + ax)) + (1, 0) + tuple(range(2 + ax, 2 + nd))
    return g.transpose(perm).reshape(sh[:ax] + (8 * sh[ax],) + sh[ax + 1:])


def _from_full(full, k, sh):
    ax, nd = SHARD_AXIS[k], len(sh)
    t = full.reshape(sh[:ax] + (4, 2, sh[ax]) + sh[ax + 1:])
    return t.transpose((ax + 1, ax) + tuple(range(ax)) + tuple(range(ax + 2, nd + 2)))


def _unpack_gathered_rows(g, shapes):
    out, r = {}, 0
    for k in RNN_ROWED:
        n = _elems(shapes[k]) // PACK_C
        out[k] = _to_full(g[:, :, r:r + n].reshape((2, 4) + shapes[k]), k, shapes[k])
        r += n
    n_small = sum(_elems(shapes[k]) for k in SMALL)
    nr = 2 * n_small // PACK_C
    small = lax.bitcast_convert_type(g[:, :, r:r + nr].reshape(2, 4, n_small, 2), F32)
    o = 0
    for k in SMALL:
        n = _elems(shapes[k])
        out[k] = _to_full(small[:, :, o:o + n].reshape((2, 4) + shapes[k]), k, shapes[k])
        o += n
    return out


def _pack_grad_rows(full, shapes):
    parts = [_from_full(full[k], k, shapes[k]).reshape(2, 4, -1, PACK_C) for k in RNN_ROWED]
    small = jnp.concatenate(
        [_from_full(full[k], k, shapes[k]).reshape(2, 4, -1) for k in SMALL], axis=-1)
    small = small.reshape(2, 4, -1, PACK_C)
    parts.append(jnp.pad(small, ((0, 0), (0, 0), (0, SUBLANES - small.shape[2]), (0, 0))))
    return jnp.concatenate(parts, axis=2)


def kernel(x, ln_g, ln_b, attn_w_in, attn_b_f, attn_w_out, rnn_w_in, rnn_conv_w, rnn_conv_b, rnn_w_a, rnn_b_a, rnn_w_i, rnn_b_i, rnn_lambda, rnn_w_out, loss_target, m_ln_g, m_ln_b, m_attn_w_in, m_attn_b_f, m_attn_w_out, m_rnn_w_in, m_rnn_conv_w, m_rnn_conv_b, m_rnn_w_a, m_rnn_b_a, m_rnn_w_i, m_rnn_b_i, m_rnn_lambda, m_rnn_w_out, v_ln_g, v_ln_b, v_attn_w_in, v_attn_b_f, v_attn_w_out, v_rnn_w_in, v_rnn_conv_w, v_rnn_conv_b, v_rnn_w_a, v_rnn_b_a, v_rnn_w_i, v_rnn_b_i, v_rnn_lambda, v_rnn_w_out):
    w_loc = dict(attn_w_in=attn_w_in, attn_w_out=attn_w_out, rnn_w_in=rnn_w_in, rnn_w_a=rnn_w_a,
                 rnn_w_i=rnn_w_i, rnn_w_out=rnn_w_out, rnn_conv_w=rnn_conv_w, rnn_conv_b=rnn_conv_b,
                 rnn_b_a=rnn_b_a, rnn_b_i=rnn_b_i, rnn_lambda=rnn_lambda)
    m_loc = dict(attn_w_in=m_attn_w_in, attn_w_out=m_attn_w_out, rnn_w_in=m_rnn_w_in,
                 rnn_w_a=m_rnn_w_a, rnn_w_i=m_rnn_w_i, rnn_w_out=m_rnn_w_out,
                 rnn_conv_w=m_rnn_conv_w, rnn_conv_b=m_rnn_conv_b, rnn_b_a=m_rnn_b_a,
                 rnn_b_i=m_rnn_b_i, rnn_lambda=m_rnn_lambda)
    v_loc = dict(attn_w_in=v_attn_w_in, attn_w_out=v_attn_w_out, rnn_w_in=v_rnn_w_in,
                 rnn_w_a=v_rnn_w_a, rnn_w_i=v_rnn_w_i, rnn_w_out=v_rnn_w_out,
                 rnn_conv_w=v_rnn_conv_w, rnn_conv_b=v_rnn_conv_b, rnn_b_a=v_rnn_b_a,
                 rnn_b_i=v_rnn_b_i, rnn_lambda=v_rnn_lambda)
    shapes = {k: tuple(a.shape[1:]) for k, a in w_loc.items()}
    T, D = x.shape[1], x.shape[2]
    n_f = attn_b_f.shape[1]
    tb = min(1024, T)
    tb_bwd = min(512, T)
    tt_rg = min(128, T)
    tt_ln = min(256, T)
    c_idx = lax.axis_index("c").astype(jnp.int32).reshape(1)
    me_idx = (2 * lax.axis_index("x") + lax.axis_index("y")).astype(jnp.int32).reshape(1)
    place = jnp.concatenate([me_idx, c_idx])

    def attn_w_in_full(g_in, idx):
        return _join_columns(g_in, 4 * D + LANES, tr=256, name=f"a_join{idx}")

    def attn_w_out_full(g_out):
        return _to_full(g_out, "attn_w_out", shapes["attn_w_out"])

    def rnn_weights(g_in, g_rows, idx):
        w = _unpack_gathered_rows(g_rows, shapes)
        w["rnn_w_in"] = _join_columns(g_in, 2 * D, tr=256, name=f"r_join{idx}")
        w["small"] = jnp.concatenate([w["rnn_conv_w"], w["rnn_conv_b"][None], w["rnn_b_a"][None],
                                      w["rnn_b_i"][None], w["rnn_lambda"][None]])
        return w

    g0 = _ag_c(_run_exchange(_Exchange("gather", [attn_w_in[0].astype(BF16)]), "ag_w0_xy"),
               "ag_w0_c")
    later = _Exchange("gather8", [attn_w_in[1].astype(BF16)] + [
        a for i in range(2) for a in (attn_w_out[i].astype(BF16), rnn_w_in[i].astype(BF16),
                                      _pack_rows(w_loc, i, BF16))])
    w_attn_in, w_attn_out, w_rnn = [attn_w_in_full(g0[0], 0), None], [None, None], [None, None]
    bf_rows = jnp.pad(attn_b_f, ((0, 0), (0, LANES - n_f)))[:, None, :]

    xs, xb, saved = [x[0]], [x[0]], []
    for layer in range(DEPTH):
        idx, xl, xm = layer // 2, xs[-1], xb[-1]
        if layer % 2 == 0:
            proj = _matmul(xm, w_attn_in[idx], trans_b=False, tm=512, tn=1408,
                           name=f"a_proj{layer}")
            cum_t = _cumsum_fwd(proj, bf_rows[idx], tt=min(512, T), name=f"a_cum{layer}")
            cum2 = cum_t[:N_HEADS].reshape(N_PAIRS, 2, T)
            o, og, lp, *got = _flash_fwd(proj, cum2.reshape(N_PAIRS, 2, T // tb, tb), tb=tb,
                                         name=f"a_fwd{layer}", host=later if layer == 0 else None)
            cum4 = cum2.reshape(N_PAIRS, 2, T // tb_bwd, tb_bwd)
            if layer == 0:
                w_attn_in[1] = attn_w_in_full(got[0], 1)
                w_attn_out = [attn_w_out_full(got[1 + 3 * i]) for i in range(2)]
                w_rnn = [rnn_weights(got[2 + 3 * i], got[3 + 3 * i], i) for i in range(2)]
            branch, w_out = og, w_attn_out[idx]
            saved.append((proj, cum4, o, og, lp))
        else:
            w = w_rnn[idx]
            proj = _matmul(xm, w["rnn_w_in"], trans_b=False, tm=512, tn=1024,
                           name=f"r_proj{layer}")
            hs, yr = _rg_fwd(proj, w["small"], w["rnn_w_a"], w["rnn_w_i"], tt=tt_rg,
                             name=f"r_fwd{layer}")
            branch, w_out = yr, w["rnn_w_out"]
            saved.append((proj, hs, yr))
        y, yb, zh, rstd = _out_ln(branch, w_out, xl, ln_g[layer][None], ln_b[layer][None],
                                  tt=512, name=f"out_ln{layer}")
        saved[-1] = saved[-1] + (zh, rstd)
        xs.append(y)
        xb.append(yb)

    def ln_below(layer):
        return saved[layer][-2:] + (ln_g[layer][None],)

    loss_lanes, *ln_grads = _loss_ln_bwd(xs[-1], loss_target[0], *ln_below(DEPTH - 1), tt=tt_ln,
                                         name="loss_ln_bwd")
    loss = lax.psum(jnp.sum(loss_lanes), ("x", "y", "c"))

    def reduce_pair(gs, layer):
        recv = _rs_c(gs, f"rs_c{layer}")
        outs = [_add_own(g, r, c_idx, tr=_row_tile(g.shape[2], 512), name=f"rs_add{layer}_{n}")
                for n, (g, r) in enumerate(zip(gs, recv))]
        return [o[0][:, None] for o in outs], [o[1][:, None] for o in outs]

    part, got_parts = [None] * DEPTH, [None] * DEPTH
    d_ln_g, d_ln_b, d_bf = [None] * DEPTH, [None] * DEPTH, [None, None]
    for layer in reversed(range(DEPTH)):
        idx, xm = layer // 2, xb[layer]
        dz, dzb, dg, db = ln_grads
        d_ln_g[layer], d_ln_b[layer] = dg[0], db[0]
        if layer % 2 == 0:
            w_in, w_out = w_attn_in[idx], w_attn_out[idx]
            proj, cum4, o, og, lp = saved[layer][:5]
            dog = _matmul(dzb, w_out, trans_b=True, tm=512, tn=1024, name=f"a_dog{layer}")
            dwo = _matmul_tn(og, dzb, tm=512, tn=1024, tk=1024, name=f"a_dwo{layer}")
            riders = [l for l in range(layer + 1, DEPTH) if got_parts[l] is None]
            host = _Exchange("scatter8", [g for l in riders for g in part[l]]) if riders else None
            dq, dgate, dk, dv, dcum_q, dcum_k, *got = _flash_bwd(proj, cum4, o, dog, lp, tb=tb_bwd,
                                                                 name=f"a_bwd{layer}", host=host)
            for l in riders:
                got_parts[l], got = got[:len(part[l])], got[len(part[l]):]
            dcum_t = (dcum_q.transpose(0, 2, 1, 3) + dcum_k).reshape(N_HEADS, T)
            dcum_t = jnp.pad(dcum_t, ((0, LANES - N_HEADS), (0, 0)))
            df, dbf = _cumsum_bwd(dcum_t, proj, bf_rows[idx], tt=min(512, T), name=f"a_dcum{layer}")
            d_bf[idx] = dbf[0, :n_f]
            dproj = [dq, dk, dv, dgate, df]
            dwi = _matmul_tn_parts(xm, dproj, tm=512, tk=1024, name=f"a_dwi{layer}")
            gs = [_split_columns(dwi, shapes["attn_w_in"][1], tr=256, name=f"a_split{layer}"),
                  _from_full(dwo, "attn_w_out", shapes["attn_w_out"])]
            if layer > 0:
                part[layer] = gs
                ln_grads = _dx_ln_bwd(dproj, w_in, dz, *ln_below(layer - 1), tm=256,
                                      name=f"a_dx{layer}")
            else:
                part[layer], narrow = reduce_pair(gs, layer)
                dy, *got_parts[layer] = _matmul(dproj, w_in, trans_b=True, tm=512, tn=1024,
                                                name=f"a_dx{layer}", add=dz, add_scale=ALPHA,
                                                host=_Exchange("scatter", narrow))
        else:
            w = w_rnn[idx]
            proj, hs, yr = saved[layer][:3]
            dyr = _matmul(dzb, w["rnn_w_out"], trans_b=True, tm=512, tn=1024, name=f"r_dy{layer}")
            dwo = _matmul_tn(yr, dzb, tm=512, tn=1024, tk=1024, name=f"r_dwo{layer}")
            dproj, dwa, dwi_, dsm = _rg_bwd(proj, hs, dyr, w["small"], w["rnn_w_a"], w["rnn_w_i"],
                                            tt=tt_rg, name=f"r_bwd{layer}")
            dwin = _matmul_tn(xm, dproj, tm=512, tn=2048, tk=1024, name=f"r_dwi{layer}")
            ln_grads = _dx_ln_bwd([dproj], w["rnn_w_in"], dz, *ln_below(layer - 1), tm=512,
                                  name=f"r_dx{layer}")
            full = dict(rnn_w_out=dwo, rnn_w_a=dwa, rnn_w_i=dwi_, rnn_conv_w=dsm[0:4],
                        rnn_conv_b=dsm[4], rnn_b_a=dsm[5], rnn_b_i=dsm[6], rnn_lambda=dsm[7])
            part[layer] = [_split_columns([dwin], shapes["rnn_w_in"][1], tr=256,
                                          name=f"r_split{layer}"),
                           _pack_grad_rows(full, shapes)]
    grad_x = dy[None]

    def grad_parts(layer, n):
        make = _two_stage_parts if layer == 0 else _direct_parts
        return make(part[layer][n], got_parts[layer][n])

    def update(k, n, wmv):
        layers = [2 * idx + (0 if k.startswith("attn") else 1) for idx in range(2)]
        return _adamw_shard([grad_parts(layer, n) for layer in layers], place, *wmv,
                            tr=_row_tile(wmv[0].shape[1], 256), name=f"adamw_{k}")

    shard_outs = [dict() for _ in range(4)]
    for k, n in (("attn_w_in", 0), ("attn_w_out", 1), ("rnn_w_in", 0)):
        for j, a in enumerate(update(k, n, (w_loc[k], m_loc[k], v_loc[k]))):
            shard_outs[j][k] = a
    rows_wmv = [jnp.stack([_pack_rows(d, idx, F32) for idx in range(2)]) for d in (w_loc, m_loc, v_loc)]
    for j, a in enumerate(update("rnn_rows", 1, rows_wmv)):
        shard_outs[j].update(_unpack_rows(a, shapes))
    g_sh, d_sh, nm_sh, nv_sh = shard_outs

    def rep_pack(lg, lb, bf):
        rows = jnp.concatenate([lg, lb, jnp.pad(bf.reshape(1, -1), ((0, 0), (0, D - 2 * n_f)))])
        return jnp.pad(rows, ((0, 16 - rows.shape[0]), (0, 0)))

    rep = _all_gather(rep_pack(jnp.stack(d_ln_g), jnp.stack(d_ln_b), jnp.stack(d_bf)), "ag_rep")
    rg, rd, rm, rv = _adamw(rep.reshape(8, 16, D), rep_pack(ln_g, ln_b, attn_b_f),
                            rep_pack(m_ln_g, m_ln_b, m_attn_b_f),
                            rep_pack(v_ln_g, v_ln_b, v_attn_b_f), tr=16, name="adamw_rep")

    def rep_unpack(a):
        return dict(ln_g=a[0:DEPTH], ln_b=a[DEPTH:2 * DEPTH],
                    attn_b_f=a[2 * DEPTH, :2 * n_f].reshape(2, n_f))

    order = ("ln_g", "ln_b", "attn_w_in", "attn_b_f", "attn_w_out", "rnn_w_in", "rnn_conv_w",
             "rnn_conv_b", "rnn_w_a", "rnn_b_a", "rnn_w_i", "rnn_b_i", "rnn_lambda", "rnn_w_out")
    outs = [loss, grad_x]
    for sh, rp in ((g_sh, rg), (d_sh, rd), (nm_sh, rm), (nv_sh, rv)):
        allp = {**sh, **rep_unpack(rp)}
        outs.extend(allp[k] for k in order)
    return tuple(outs)
```

```python
import jax
import jax.numpy as jnp
from jax import lax
from jax.experimental import pallas as pl
from jax.experimental.pallas import tpu as pltpu

F32 = jnp.float32
BF16 = jnp.bfloat16

DEPTH = 4
N_HEADS = 16
HEAD_DIM = 64
N_PAIRS = N_HEADS // 2
RNN_BLOCKS = 4
RNN_BLOCK_WIDTH = 256
CONV_WIDTH = 4
LRU_C = 8.0
ALPHA = (2.0 * DEPTH) ** 0.25
LN_EPS = 1e-5
ADAM_LR, ADAM_B1, ADAM_B2, ADAM_EPS, ADAM_WD, ADAM_STEP = 0.001, 0.9, 0.999, 1e-8, 0.01, 10

LANES = 128
SUBLANES = 8
VMEM_LIMIT = 48 * 1024 * 1024

MESH = pl.DeviceIdType.MESH
HBM_SPEC = pl.BlockSpec(memory_space=pltpu.HBM)


def _cparams(*sem):
    return pltpu.CompilerParams(dimension_semantics=sem, vmem_limit_bytes=VMEM_LIMIT)


def _sigmoid(x):
    return 1.0 / (1.0 + jnp.exp(-x))


def _softplus(x):
    return jnp.maximum(x, 0.0) + jnp.log(1.0 + jnp.exp(-jnp.abs(x)))


def _a2a(src, *, group, bcast, name):
    n = 2 if group == "c" else 4
    blk = tuple(src.shape) if bcast else tuple(src.shape[1:])

    def body(src_ref, out_ref, send_sems, recv_sems, local_sem):
        x, y, c = lax.axis_index("x"), lax.axis_index("y"), lax.axis_index("c")
        if group == "c":
            me = c

            def peer(d):
                return (x, y, 1 - c), 1 - c
        else:
            me = 2 * x + y

            def peer(d):
                px, py = x ^ (d >> 1), y ^ (d & 1)
                return (px, py, c), 2 * px + py

        def block_for(k):
            return src_ref if bcast else src_ref.at[k]

        local = pltpu.make_async_copy(block_for(me), out_ref.at[me], local_sem)
        local.start()
        sends = []
        for d in range(1, n):
            dev, idx = peer(d)
            cp = pltpu.make_async_remote_copy(
                src_ref=block_for(idx), dst_ref=out_ref.at[me],
                send_sem=send_sems.at[d], recv_sem=recv_sems.at[d],
                device_id=dev, device_id_type=MESH)
            cp.start()
            sends.append(cp)
        for d in range(1, n):
            dev, idx = peer(d)
            pltpu.make_async_remote_copy(
                src_ref=block_for(idx), dst_ref=out_ref.at[idx],
                send_sem=send_sems.at[d], recv_sem=recv_sems.at[d],
                device_id=dev, device_id_type=MESH).wait_recv()
        for cp in sends:
            cp.wait_send()
        local.wait()

    return pl.pallas_call(
        body, name=name,
        out_shape=jax.ShapeDtypeStruct((n,) + blk, src.dtype),
        in_specs=[HBM_SPEC], out_specs=HBM_SPEC,
        scratch_shapes=[pltpu.SemaphoreType.DMA((n,)), pltpu.SemaphoreType.DMA((n,)),
                        pltpu.SemaphoreType.DMA],
    )(src)


def _all_gather(piece, name):
    return _a2a(_a2a(piece, group="xy", bcast=True, name=name + "_xy"),
                group="c", bcast=True, name=name + "_c")


D2D_CHUNKS = 16
ICI_CHUNKS = 8


def _row_chunks(rows, dtype, k):
    unit = SUBLANES * (4 // jnp.dtype(dtype).itemsize)
    assert rows % unit == 0
    units = rows // unit
    k = max(1, min(k, units))
    base, rem = divmod(units, k)
    out, r = [], 0
    for i in range(k):
        n = (base + (1 if i < rem else 0)) * unit
        out.append((r, n))
        r += n
    return out


def _chunks(shape, dtype, k):
    if len(shape) == 2:
        return [(pl.ds(r0, n),) for r0, n in _row_chunks(shape[0], dtype, k)]
    per = max(1, k // shape[0])
    return [(l, pl.ds(r0, n)) for l in range(shape[0]) for r0, n in _row_chunks(shape[1], dtype, per)]


def _mesh_place():
    x, y, c = lax.axis_index("x"), lax.axis_index("y"), lax.axis_index("c")
    return x, y, c, 2 * x + y


def _chip_peer(x, y, c, d):
    px, py = x ^ (d >> 1), y ^ (d & 1)
    return (px, py, c), 2 * px + py


def _remote(src, dst, send_sem, recv_sem, dev):
    return pltpu.make_async_remote_copy(src_ref=src, dst_ref=dst, send_sem=send_sem,
                                        recv_sem=recv_sem, device_id=dev, device_id_type=MESH)


def _comm_call(body, name, ins, out_shapes, n_sems, aliases=None):
    n = len(ins)
    return pl.pallas_call(
        body, name=name,
        out_shape=out_shapes, in_specs=[HBM_SPEC] * n, out_specs=[HBM_SPEC] * n,
        input_output_aliases=aliases or {},
        scratch_shapes=[pltpu.SemaphoreType.DMA((n_sems, n)), pltpu.SemaphoreType.DMA((n_sems, n))],
    )(*ins)


class _Exchange:
    def __init__(self, kind, arrays):
        self.kind, self.arrays, self.n = kind, list(arrays), len(arrays)
        self.is_gather, self.all8 = kind.startswith("gather"), kind.endswith("8")
        k = ICI_CHUNKS // 4 if self.all8 else ICI_CHUNKS
        if self.is_gather:
            self.chunks = [_chunks(a.shape, a.dtype, k) for a in arrays]
            self.out_shapes = [jax.ShapeDtypeStruct((2, 4) + tuple(a.shape), a.dtype) for a in arrays]
        else:
            lead = 2 if self.all8 else 1
            self.chunks = [_chunks(a.shape[lead:], a.dtype, k) for a in arrays]
            self.out_shapes = [jax.ShapeDtypeStruct(a.shape, a.dtype) for a in arrays]
        self.peers = list(range(1, 8 if self.all8 else 4))
        n_sems = len(self.peers) + 1
        self.sem_shapes = [pltpu.SemaphoreType.DMA((n_sems, self.n)),
                           pltpu.SemaphoreType.DMA((n_sems, self.n))]

    def _peer(self, x, y, c, me, p):
        a, d = p // 4, p % 4
        px, py = x ^ (d >> 1), y ^ (d & 1)
        pc = 1 - c if a else c
        if self.all8:
            return (px, py, pc), (pc, 2 * px + py), (c, me)
        return (px, py, pc), (2 * px + py,), (me,)

    def _blocks(self, srcs, outs, o, c, me, theirs, mine):
        if self.kind == "gather":
            return srcs[o], outs[o].at[(c,) + mine], outs[o].at[(c,) + theirs]
        if self.kind == "gather8":
            return srcs[o], outs[o].at[mine], outs[o].at[theirs]
        return srcs[o].at[theirs], outs[o].at[mine], outs[o].at[theirs]

    def start(self, srcs, outs, send_sems, recv_sems):
        x, y, c, me = _mesh_place()
        if self.is_gather:
            for o in range(self.n):
                for idx in self.chunks[o]:
                    pltpu.make_async_copy(srcs[o].at[idx], outs[o].at[(c, me) + idx],
                                          send_sems.at[0, o]).start()
        for p in self.peers:
            dev, theirs, mine = self._peer(x, y, c, me, p)
            for o in range(self.n):
                src, dst, _ = self._blocks(srcs, outs, o, c, me, theirs, mine)
                for idx in self.chunks[o]:
                    _remote(src.at[idx], dst.at[idx], send_sems.at[p, o], recv_sems.at[p, o],
                            dev).start()

    def wait(self, srcs, outs, send_sems, recv_sems):
        x, y, c, me = _mesh_place()
        for wait_recv in (True, False):
            for p in self.peers:
                dev, theirs, mine = self._peer(x, y, c, me, p)
                for o in range(self.n):
                    src, _, land = self._blocks(srcs, outs, o, c, me, theirs, mine)
                    cp = _remote(src, land, send_sems.at[p, o], recv_sems.at[p, o], dev)
                    cp.wait_recv() if wait_recv else cp.wait_send()
        if self.is_gather:
            for o in range(self.n):
                pltpu.make_async_copy(srcs[o], outs[o].at[c, me], send_sems.at[0, o]).wait()


def _run_exchange(ex, name):
    n = ex.n

    def body(*refs):
        srcs, outs, send_sems, recv_sems = refs[:n], refs[n:2 * n], refs[2 * n], refs[2 * n + 1]
        ex.start(srcs, outs, send_sems, recv_sems)
        ex.wait(srcs, outs, send_sems, recv_sems)

    return _comm_call(body, name, ex.arrays, ex.out_shapes, len(ex.peers) + 1)


def _ag_c(bufs, name):
    n = len(bufs)
    chunks = [_chunks(b.shape[2:], b.dtype, D2D_CHUNKS // 4) for b in bufs]

    def body(*refs):
        srcs, outs, send_sems, recv_sems = refs[:n], refs[n:2 * n], refs[2 * n], refs[2 * n + 1]
        x, y, c, _ = _mesh_place()
        sib = (x, y, 1 - c)
        for o in range(n):
            for k in range(4):
                for idx in chunks[o]:
                    _remote(srcs[o].at[(c, k) + idx], outs[o].at[(c, k) + idx],
                            send_sems.at[0, o], recv_sems.at[0, o], sib).start()
        for o in range(n):
            _remote(srcs[o].at[c], outs[o].at[1 - c], send_sems.at[0, o], recv_sems.at[0, o],
                    sib).wait_recv()
        for o in range(n):
            _remote(srcs[o].at[c], outs[o].at[1 - c], send_sems.at[0, o], recv_sems.at[0, o],
                    sib).wait_send()

    shapes = [jax.ShapeDtypeStruct(b.shape, b.dtype) for b in bufs]
    return _comm_call(body, name, bufs, shapes, 1, aliases={i: i for i in range(n)})


def _rs_c(gs, name):
    n = len(gs)
    chunks = [_chunks(g.shape[2:], g.dtype, max(1, D2D_CHUNKS // g.shape[1])) for g in gs]

    def body(*refs):
        srcs, outs, send_sems, recv_sems = refs[:n], refs[n:2 * n], refs[2 * n], refs[2 * n + 1]
        x, y, c, _ = _mesh_place()
        sib = (x, y, 1 - c)
        for o in range(n):
            for k in range(gs[o].shape[1]):
                for idx in chunks[o]:
                    _remote(srcs[o].at[(1 - c, k) + idx], outs[o].at[(k,) + idx],
                            send_sems.at[0, o], recv_sems.at[0, o], sib).start()
        for o in range(n):
            _remote(srcs[o].at[1 - c], outs[o], send_sems.at[0, o], recv_sems.at[0, o],
                    sib).wait_recv()
        for o in range(n):
            _remote(srcs[o].at[1 - c], outs[o], send_sems.at[0, o], recv_sems.at[0, o],
                    sib).wait_send()

    shapes = [jax.ShapeDtypeStruct(g.shape[1:], g.dtype) for g in gs]
    return _comm_call(body, name, gs, shapes, 1)


def _matmul(a, b, *, trans_b, tm, tn, name, add=None, add_scale=1.0, host=None):
    a_parts = list(a) if isinstance(a, (list, tuple)) else [a]
    M, K = a_parts[0].shape[0], sum(p.shape[1] for p in a_parts)
    N = b.shape[0] if trans_b else b.shape[1]
    tm, tn = min(tm, M), min(tn, N)
    assert M % tm == 0 and N % tn == 0
    dn = (((1,), (1,)), ((), ())) if trans_b else (((1,), (0,)), ((), ()))
    na = len(a_parts)

    def body(*refs):
        a_refs, b_ref, o_ref = refs[:na], refs[na], refs[-1]
        av = [r[...].astype(BF16) for r in a_refs]
        av = av[0] if na == 1 else jnp.concatenate(av, axis=1)
        r = lax.dot_general(av, b_ref[...].astype(BF16), dn, preferred_element_type=F32)
        if add is not None:
            r = r + add_scale * refs[na + 1][...]
        o_ref[...] = r

    b_spec = (pl.BlockSpec((tn, K), lambda j, i: (j, 0)) if trans_b
              else pl.BlockSpec((K, tn), lambda j, i: (0, j)))
    in_specs = [pl.BlockSpec((tm, p.shape[1]), lambda j, i: (i, 0)) for p in a_parts] + [b_spec]
    args = a_parts + [b]
    if add is not None:
        in_specs.append(pl.BlockSpec((tm, tn), lambda j, i: (i, j)))
        args.append(add)
    grid = (N // tn, M // tm)
    x_in, x_out, x_shapes, x_scratch, x_args = _host_specs(host)
    body = _hosted(body, len(args), 1, 0, host, grid)
    outs = pl.pallas_call(
        body, name=name, grid=grid,
        in_specs=in_specs + x_in,
        out_specs=[pl.BlockSpec((tm, tn), lambda j, i: (i, j))] + x_out,
        out_shape=[jax.ShapeDtypeStruct((M, N), F32)] + x_shapes,
        scratch_shapes=x_scratch,
        compiler_params=_cparams(*(("arbitrary",) * 2 if host else ("parallel",) * 2)),
    )(*args, *x_args)
    return outs if host else outs[0]


def _matmul_tn(a, b, *, tm, tn, tk, name):
    T, M = a.shape
    N = b.shape[1]
    tm, tn, tk = min(tm, M), min(tn, N), min(tk, T)
    assert M % tm == 0 and N % tn == 0 and T % tk == 0

    def body(a_ref, b_ref, o_ref):
        @pl.when(pl.program_id(2) == 0)
        def _():
            o_ref[...] = jnp.zeros_like(o_ref)

        o_ref[...] += lax.dot_general(a_ref[...].astype(BF16), b_ref[...].astype(BF16),
                                      (((0,), (0,)), ((), ())), preferred_element_type=F32)

    return pl.pallas_call(
        body, name=name, grid=(M // tm, N // tn, T // tk),
        in_specs=[pl.BlockSpec((tk, tm), lambda i, j, k: (k, i)),
                  pl.BlockSpec((tk, tn), lambda i, j, k: (k, j))],
        out_specs=pl.BlockSpec((tm, tn), lambda i, j, k: (i, j)),
        out_shape=jax.ShapeDtypeStruct((M, N), F32),
        compiler_params=_cparams("parallel", "parallel", "arbitrary"),
    )(a, b)


def _matmul_tn_parts(a, parts, *, tm, tk, name):
    T, M = a.shape
    tm, tk = min(tm, M), min(tk, T)
    assert M % tm == 0 and T % tk == 0
    n = len(parts)

    def body(*refs):
        a_ref, b_refs, o_refs = refs[0], refs[1:1 + n], refs[1 + n:]
        av = a_ref[...].astype(BF16)
        for b_ref, o_ref in zip(b_refs, o_refs):
            @pl.when(pl.program_id(1) == 0)
            def _(o_ref=o_ref):
                o_ref[...] = jnp.zeros_like(o_ref)

            o_ref[...] += lax.dot_general(av, b_ref[...].astype(BF16), (((0,), (0,)), ((), ())),
                                          preferred_element_type=F32)

    return pl.pallas_call(
        body, name=name, grid=(M // tm, T // tk),
        in_specs=[pl.BlockSpec((tk, tm), lambda i, k: (k, i))]
        + [pl.BlockSpec((tk, p.shape[1]), lambda i, k: (k, 0)) for p in parts],
        out_specs=[pl.BlockSpec((tm, p.shape[1]), lambda i, k: (i, 0)) for p in parts],
        out_shape=[jax.ShapeDtypeStruct((M, p.shape[1]), F32) for p in parts],
        compiler_params=_cparams("parallel", "arbitrary"),
    )(a, *parts)


def _head_masks(rows):
    lane = lax.broadcasted_iota(jnp.int32, (rows, LANES), 1)
    return lane < HEAD_DIM, lane >= HEAD_DIM


def _causal(i_q, i_k, tq, tk):
    row = i_q * tq + lax.broadcasted_iota(jnp.int32, (tq, tk), 0)
    col = i_k * tk + lax.broadcasted_iota(jnp.int32, (tq, tk), 1)
    return row >= col


def _hosted(body, n_in, n_out, n_scratch, host, grid):
    if host is None:
        return body
    nx = host.n

    def wrapped(*refs):
        ins, xsrcs = refs[:n_in], refs[n_in:n_in + nx]
        outs = refs[n_in + nx:n_in + nx + n_out]
        xouts = refs[n_in + nx + n_out:n_in + 2 * nx + n_out]
        scratch = refs[n_in + 2 * nx + n_out:n_in + 2 * nx + n_out + n_scratch]
        xsems = refs[n_in + 2 * nx + n_out + n_scratch:]
        step = pl.program_id(0) * grid[1] + pl.program_id(1)

        @pl.when(step == 0)
        def _():
            host.start(xsrcs, xouts, *xsems)

        body(*ins, *outs, *scratch)

        @pl.when(step == grid[0] * grid[1] - 1)
        def _():
            host.wait(xsrcs, xouts, *xsems)

    return wrapped


def _host_specs(host):
    if host is None:
        return [], [], [], [], []
    return ([HBM_SPEC] * host.n, [HBM_SPEC] * host.n, host.out_shapes, host.sem_shapes, host.arrays)


def _flash_fwd(proj, cum4, *, tb, name, host=None):
    T = proj.shape[0]
    D = N_HEADS * HEAD_DIM
    nb = T // tb
    cb = D // LANES
    x_in, x_out, x_shapes, x_scratch, x_args = _host_specs(host)

    def body(q_ref, k_ref, v_ref, g_ref, cum_ref, o_ref, og_ref, lp_ref, kb_ref, vb_ref):
        i = pl.program_id(1)

        @pl.when(i == 0)
        def _():
            kb_ref[...] = k_ref[...].astype(BF16)
            vb_ref[...] = v_ref[...].astype(BF16)

        q = q_ref[...] * (HEAD_DIM ** -0.5)
        masks = _head_masks(tb)
        qh = [jnp.where(masks[h], q, 0.0).astype(BF16) for h in range(2)]
        cref = [cum_ref[0, h, pl.ds(i, 1), :][:, 0:1] for h in range(2)]

        def tile(kbi, r0, nr, nk, carry, first_row):
            k0 = pl.multiple_of(kbi * tb, tb)
            kblk = kb_ref[pl.ds(k0, nk), :]
            vblk = vb_ref[pl.ds(k0, nk), :]
            new = []
            for h in range(2):
                m, l, acc = carry[h]
                s = lax.dot_general(qh[h][r0:r0 + nr], kblk, (((1,), (1,)), ((), ())),
                                    preferred_element_type=F32)
                s = s + (cref[h] - cum_ref[0, h, pl.ds(kbi, 1), :][:, 0:nk])
                if first_row is not None:
                    row = first_row + lax.broadcasted_iota(jnp.int32, (nr, nk), 0)
                    s = jnp.where(row >= lax.broadcasted_iota(jnp.int32, (nr, nk), 1), s, -jnp.inf)
                m_new = jnp.maximum(m, jnp.max(s, axis=-1, keepdims=True))
                alpha = jnp.exp(m - m_new)
                p = jnp.exp(s - m_new)
                l = alpha * l + jnp.sum(p, axis=-1, keepdims=True)
                acc = alpha * acc + jnp.dot(p.astype(BF16), vblk, preferred_element_type=F32)
                new.append((m_new, l, acc))
            return tuple(new)

        init1 = (jnp.full((tb, 1), -jnp.inf, F32), jnp.zeros((tb, 1), F32),
                 jnp.zeros((tb, LANES), F32))
        carry = lax.fori_loop(0, i, lambda kbi, c: tile(kbi, 0, tb, tb, c, None), (init1, init1))
        hb = tb // 2
        upper = tile(i, 0, hb, hb, tuple(tuple(a[:hb] for a in c) for c in carry), 0)
        lower = tile(i, hb, hb, tb, tuple(tuple(a[hb:] for a in c) for c in carry), hb)
        outs = []
        for h, (m, l, acc) in enumerate(
                tuple(jnp.concatenate([u, w], axis=0) for u, w in zip(upper[h], lower[h]))
                for h in range(2)):
            outs.append(acc / l)
            lp_ref[h] = jnp.broadcast_to(m + jnp.log(l) - cref[h], (tb, LANES))
        o = jnp.where(masks[0], outs[0], outs[1])
        o_ref[...] = o
        gate = g_ref[...]
        og_ref[...] = (o * (gate * _sigmoid(gate))).astype(BF16)

    body = _hosted(body, 5, 3, 2, host, (N_PAIRS, nb))
    return pl.pallas_call(
        body, name=name, grid=(N_PAIRS, nb),
        in_specs=[pl.BlockSpec((tb, LANES), lambda j, i: (i, j)),
                  pl.BlockSpec((T, LANES), lambda j, i: (0, cb + j)),
                  pl.BlockSpec((T, LANES), lambda j, i: (0, 2 * cb + j)),
                  pl.BlockSpec((tb, LANES), lambda j, i: (i, 3 * cb + j)),
                  pl.BlockSpec((1, 2, nb, tb), lambda j, i: (j, 0, 0, 0))] + x_in,
        out_specs=[pl.BlockSpec((tb, LANES), lambda j, i: (i, j)),
                   pl.BlockSpec((tb, LANES), lambda j, i: (i, j)),
                   pl.BlockSpec((2, tb, LANES), lambda j, i: (j, i, 0))] + x_out,
        out_shape=[jax.ShapeDtypeStruct((T, D), F32), jax.ShapeDtypeStruct((T, D), BF16),
                   jax.ShapeDtypeStruct((N_HEADS, T, LANES), F32)] + x_shapes,
        scratch_shapes=[pltpu.VMEM((T, LANES), BF16), pltpu.VMEM((T, LANES), BF16)] + x_scratch,
        compiler_params=_cparams("arbitrary", "arbitrary"),
    )(proj, proj, proj, proj, cum4, *x_args)


def _flash_bwd(proj, cum4, o, dog, lp, *, tb, name, host=None):
    T = proj.shape[0]
    D = N_HEADS * HEAD_DIM
    nb = T // tb
    cb = D // LANES
    x_in, x_out, x_shapes, x_scratch, x_args = _host_specs(host)

    def body(q_ref, k_ref, v_ref, g_ref, cum_ref, o_ref, dog_ref, lp_ref,
             dq_ref, dg_ref, dk_ref, dv_ref, dcq_ref, dck_ref,
             kb_ref, vb_ref, dka_ref, dva_ref, dca_ref):
        i = pl.program_id(1)

        @pl.when(i == 0)
        def _():
            kb_ref[...] = k_ref[...].astype(BF16)
            vb_ref[...] = v_ref[...].astype(BF16)
            dka_ref[...] = jnp.zeros_like(dka_ref)
            dva_ref[...] = jnp.zeros_like(dva_ref)
            dca_ref[...] = jnp.zeros_like(dca_ref)

        gate = g_ref[...]
        sg = _sigmoid(gate)
        o = o_ref[...]
        dog = dog_ref[...]
        do = dog * (gate * sg)
        dg_ref[...] = (dog * o * (sg * (1.0 + gate * (1.0 - sg)))).astype(BF16)
        q = q_ref[...] * (HEAD_DIM ** -0.5)
        masks = _head_masks(tb)
        qh = [jnp.where(masks[h], q, 0.0).astype(BF16) for h in range(2)]
        doh = [jnp.where(masks[h], do, 0.0).astype(BF16) for h in range(2)]
        delta = [jnp.sum(jnp.where(masks[h], do * o, 0.0), axis=-1, keepdims=True) for h in range(2)]
        lph = [lp_ref[h][:, 0:1] for h in range(2)]

        def step(kbi, carry, masked):
            k0 = pl.multiple_of(kbi * tb, tb)
            kblk = kb_ref[pl.ds(k0, tb), :]
            vblk = vb_ref[pl.ds(k0, tb), :]
            new, dk, dv = [], None, None
            for h in range(2):
                acc, rs = carry[h]
                s = lax.dot_general(qh[h], kblk, (((1,), (1,)), ((), ())), preferred_element_type=F32)
                p = jnp.exp(s - cum_ref[0, h, pl.ds(kbi, 1), :] - lph[h])
                if masked:
                    p = jnp.where(_causal(i, kbi, tb, tb), p, 0.0)
                dp = lax.dot_general(doh[h], vblk, (((1,), (1,)), ((), ())),
                                     preferred_element_type=F32)
                ds = p * (dp - delta[h])
                pb, dsb = p.astype(BF16), ds.astype(BF16)
                dv_h = lax.dot_general(pb, doh[h], (((0,), (0,)), ((), ())),
                                       preferred_element_type=F32)
                dk_h = lax.dot_general(dsb, qh[h], (((0,), (0,)), ((), ())),
                                       preferred_element_type=F32)
                dv = dv_h if dv is None else dv + dv_h
                dk = dk_h if dk is None else dk + dk_h
                dca_ref[h, pl.ds(kbi, 1), :] -= jnp.sum(ds, axis=0, keepdims=True)
                new.append((acc + jnp.dot(dsb, kblk, preferred_element_type=F32),
                            rs + jnp.sum(ds, axis=-1, keepdims=True)))
            dka_ref[pl.ds(k0, tb), :] += dk
            dva_ref[pl.ds(k0, tb), :] += dv
            return tuple(new)

        init1 = (jnp.zeros((tb, LANES), F32), jnp.zeros((tb, 1), F32))
        carry = lax.fori_loop(0, i, lambda kbi, c: step(kbi, c, False), (init1, init1))
        dqs = []
        for h, (acc, rs) in enumerate(step(i, carry, True)):
            dqs.append(acc)
            dcq_ref[0, 0, pl.ds(h, 1), :] = jnp.broadcast_to(rs, (tb, LANES)).T[0:1, :]
        dq_ref[...] = (jnp.where(masks[0], dqs[0], dqs[1]) * (HEAD_DIM ** -0.5)).astype(BF16)

        @pl.when(i == nb - 1)
        def _():
            dk_ref[...] = dka_ref[...].astype(BF16)
            dv_ref[...] = dva_ref[...].astype(BF16)
            dck_ref[0] = dca_ref[...]

    blk = pl.BlockSpec((tb, LANES), lambda j, i: (i, j))
    full = pl.BlockSpec((T, LANES), lambda j, i: (0, j))
    body = _hosted(body, 8, 6, 5, host, (N_PAIRS, nb))
    return pl.pallas_call(
        body, name=name, grid=(N_PAIRS, nb),
        in_specs=[blk,
                  pl.BlockSpec((T, LANES), lambda j, i: (0, cb + j)),
                  pl.BlockSpec((T, LANES), lambda j, i: (0, 2 * cb + j)),
                  pl.BlockSpec((tb, LANES), lambda j, i: (i, 3 * cb + j)),
                  pl.BlockSpec((1, 2, nb, tb), lambda j, i: (j, 0, 0, 0)),
                  blk, blk, pl.BlockSpec((2, tb, LANES), lambda j, i: (j, i, 0))] + x_in,
        out_specs=[blk, blk, full, full,
                   pl.BlockSpec((1, 1, 2, tb), lambda j, i: (j, i, 0, 0)),
                   pl.BlockSpec((1, 2, nb, tb), lambda j, i: (j, 0, 0, 0))] + x_out,
        out_shape=[jax.ShapeDtypeStruct((T, D), BF16)] * 4
        + [jax.ShapeDtypeStruct((N_PAIRS, nb, 2, tb), F32),
           jax.ShapeDtypeStruct((N_PAIRS, 2, nb, tb), F32)] + x_shapes,
        scratch_shapes=[pltpu.VMEM((T, LANES), BF16), pltpu.VMEM((T, LANES), BF16),
                        pltpu.VMEM((T, LANES), F32), pltpu.VMEM((T, LANES), F32),
                        pltpu.VMEM((2, nb, tb), F32)] + x_scratch,
        compiler_params=_cparams("arbitrary", "arbitrary"),
    )(proj, proj, proj, proj, cum4, o, dog, lp, *x_args)


def _cumsum_fwd(proj, bf_row, *, tt, name):
    T = proj.shape[0]
    cb = (proj.shape[1] - LANES) // LANES

    def body(f_ref, b_ref, out_ref, carry_ref):
        i = pl.program_id(0)

        @pl.when(i == 0)
        def _():
            carry_ref[...] = jnp.zeros_like(carry_ref)

        ls = -_softplus(-(f_ref[...] + b_ref[...]))
        tri = (lax.broadcasted_iota(jnp.int32, (tt, tt), 0)
               >= lax.broadcasted_iota(jnp.int32, (tt, tt), 1)).astype(F32)
        cum = jnp.dot(tri, ls, preferred_element_type=F32,
                      precision=lax.Precision.HIGHEST) + carry_ref[...]
        carry_ref[...] = cum[tt - 1:tt, :]
        out_ref[...] = cum.T

    return pl.pallas_call(
        body, name=name, grid=(T // tt,),
        in_specs=[pl.BlockSpec((tt, LANES), lambda i: (i, cb)),
                  pl.BlockSpec((1, LANES), lambda i: (0, 0))],
        out_specs=pl.BlockSpec((LANES, tt), lambda i: (0, i)),
        out_shape=jax.ShapeDtypeStruct((LANES, T), F32),
        scratch_shapes=[pltpu.VMEM((1, LANES), F32)],
        compiler_params=_cparams("arbitrary"),
    )(proj, bf_row)


def _cumsum_bwd(dcum_t, proj, bf_row, *, tt, name):
    T = proj.shape[0]
    cb = (proj.shape[1] - LANES) // LANES
    nt = T // tt

    def body(dc_ref, f_ref, b_ref, df_ref, db_ref, carry_ref):
        i = pl.program_id(0)

        @pl.when(i == 0)
        def _():
            carry_ref[...] = jnp.zeros_like(carry_ref)
            db_ref[...] = jnp.zeros_like(db_ref)

        dc = dc_ref[...].T
        tri = (lax.broadcasted_iota(jnp.int32, (tt, tt), 0)
               <= lax.broadcasted_iota(jnp.int32, (tt, tt), 1)).astype(F32)
        rev = jnp.dot(tri, dc, preferred_element_type=F32,
                      precision=lax.Precision.HIGHEST) + carry_ref[...]
        carry_ref[...] = rev[0:1, :]
        df = rev * _sigmoid(-(f_ref[...] + b_ref[...]))
        df_ref[...] = df.astype(BF16)
        db_ref[...] += jnp.sum(df, axis=0, keepdims=True)

    return pl.pallas_call(
        body, name=name, grid=(nt,),
        in_specs=[pl.BlockSpec((LANES, tt), lambda i: (0, nt - 1 - i)),
                  pl.BlockSpec((tt, LANES), lambda i: (nt - 1 - i, cb)),
                  pl.BlockSpec((1, LANES), lambda i: (0, 0))],
        out_specs=[pl.BlockSpec((tt, LANES), lambda i: (nt - 1 - i, 0)),
                   pl.BlockSpec((1, LANES), lambda i: (0, 0))],
        out_shape=[jax.ShapeDtypeStruct((T, LANES), BF16), jax.ShapeDtypeStruct((1, LANES), F32)],
        scratch_shapes=[pltpu.VMEM((1, LANES), F32)],
        compiler_params=_cparams("arbitrary"),
    )(dcum_t, proj, bf_row)


def _rows_down(x, before, sh):
    if sh == 0:
        return x
    rolled = pltpu.roll(x, sh, axis=0)
    row = lax.broadcasted_iota(jnp.int32, (SUBLANES, x.shape[1]), 0)
    head = jnp.where(row < sh, pltpu.roll(before, sh, axis=0), rolled[:SUBLANES])
    return jnp.concatenate([head, rolled[SUBLANES:]], axis=0)


def _rows_up(x, after, sh):
    if sh == 0:
        return x
    tt = x.shape[0]
    rolled = pltpu.roll(x, tt - sh, axis=0)
    row = lax.broadcasted_iota(jnp.int32, (SUBLANES, x.shape[1]), 0)
    tail = jnp.where(row >= SUBLANES - sh, pltpu.roll(after, SUBLANES - sh, axis=0),
                     rolled[tt - SUBLANES:])
    return jnp.concatenate([rolled[:tt - SUBLANES], tail], axis=0)


def _rg_gates(u0, before, small_ref, wa_ref, wi_ref):
    taps = [_rows_down(u0, before, CONV_WIDTH - 1 - tap) for tap in range(CONV_WIDTH)]
    u = small_ref[4:5, :]
    for tap in range(CONV_WIDTH):
        u = u + taps[tap] * small_ref[tap:tap + 1, :]
    pa, pi = [], []
    for n in range(RNN_BLOCKS):
        ub = u[:, n * RNN_BLOCK_WIDTH:(n + 1) * RNN_BLOCK_WIDTH].astype(BF16)
        pa.append(jnp.dot(ub, wa_ref[n], preferred_element_type=F32))
        pi.append(jnp.dot(ub, wi_ref[n], preferred_element_type=F32))
    r = _sigmoid(jnp.concatenate(pa, axis=-1) + small_ref[5:6, :])
    ig = _sigmoid(jnp.concatenate(pi, axis=-1) + small_ref[6:7, :])
    spl = _softplus(-small_ref[7:8, :])
    log_a = (-LRU_C) * r * spl
    a = jnp.exp(log_a)
    s2 = jnp.tanh(-log_a) * (a * a + 1.0)
    inv_s = lax.rsqrt(s2)
    s = jnp.where(s2 > 0.0, s2 * inv_s, 0.0)
    return u, taps, r, ig, spl, a, s, inv_s


def _rg_fwd(proj, small, wa, wi, *, tt, name):
    T = proj.shape[0]
    D = RNN_BLOCKS * RNN_BLOCK_WIDTH
    hb = tt // SUBLANES

    def body(u0_ref, halo_ref, g_ref, small_ref, wa_ref, wi_ref, h_ref, y_ref,
             a_ref, b_ref, carry_ref):
        i = pl.program_id(0)

        @pl.when(i == 0)
        def _():
            carry_ref[...] = jnp.zeros_like(carry_ref)

        before = jnp.where(i == 0, 0.0, halo_ref[...])
        u, _, r, ig, spl, a, s, _ = _rg_gates(u0_ref[...], before, small_ref, wa_ref, wi_ref)
        a_ref[...] = a
        b_ref[...] = s * (ig * u)

        def row(t, h):
            h = a_ref[pl.ds(t, 1), :] * h + b_ref[pl.ds(t, 1), :]
            h_ref[pl.ds(t, 1), :] = h
            return h

        carry_ref[...] = lax.fori_loop(0, tt, row, carry_ref[...])
        gate = g_ref[...]
        y_ref[...] = (h_ref[...] * (gate * _sigmoid(gate))).astype(BF16)

    return pl.pallas_call(
        body, name=name, grid=(T // tt,),
        in_specs=[pl.BlockSpec((tt, D), lambda i: (i, 0)),
                  pl.BlockSpec((SUBLANES, D), lambda i: (jnp.maximum(i * hb - 1, 0), 0)),
                  pl.BlockSpec((tt, D), lambda i: (i, 1)),
                  pl.BlockSpec((SUBLANES, D), lambda i: (0, 0)),
                  pl.BlockSpec((RNN_BLOCKS, RNN_BLOCK_WIDTH, RNN_BLOCK_WIDTH), lambda i: (0, 0, 0)),
                  pl.BlockSpec((RNN_BLOCKS, RNN_BLOCK_WIDTH, RNN_BLOCK_WIDTH), lambda i: (0, 0, 0))],
        out_specs=[pl.BlockSpec((tt, D), lambda i: (i, 0)), pl.BlockSpec((tt, D), lambda i: (i, 0))],
        out_shape=[jax.ShapeDtypeStruct((T, D), F32), jax.ShapeDtypeStruct((T, D), BF16)],
        scratch_shapes=[pltpu.VMEM((tt, D), F32), pltpu.VMEM((tt, D), F32),
                        pltpu.VMEM((1, D), F32)],
        compiler_params=_cparams("arbitrary"),
    )(proj, proj, proj, small, wa, wi)


def _rg_bwd(proj, hs, dy, small, wa, wi, *, tt, name):
    T = proj.shape[0]
    D = RNN_BLOCKS * RNN_BLOCK_WIDTH
    W = RNN_BLOCK_WIDTH
    hb = tt // SUBLANES
    nt = T // tt

    def body(u0_ref, uhalo_ref, g_ref, h_ref, hhalo_ref, dy_ref, small_ref, wa_ref, wi_ref,
             dp_ref, dwa_ref, dwi_ref, ds_ref,
             a_ref, g_s_ref, dunext_ref, carry_ref):
        i = pl.program_id(0)
        first_chunk = i == nt - 1

        @pl.when(i == 0)
        def _():
            carry_ref[...] = jnp.zeros_like(carry_ref)
            dunext_ref[...] = jnp.zeros_like(dunext_ref)
            dwa_ref[...] = jnp.zeros_like(dwa_ref)
            dwi_ref[...] = jnp.zeros_like(dwi_ref)
            ds_ref[...] = jnp.zeros_like(ds_ref)

        u_before = jnp.where(first_chunk, 0.0, uhalo_ref[...])
        h_before = jnp.where(first_chunk, 0.0, hhalo_ref[...])
        u, taps, r, ig, spl, a, s, inv_s = _rg_gates(u0_ref[...], u_before, small_ref, wa_ref,
                                                     wi_ref)
        gate = g_ref[...]
        sg = _sigmoid(gate)
        dy = dy_ref[...]
        dp_ref[:, D:] = (dy * h_ref[...] * (sg * (1.0 + gate * (1.0 - sg)))).astype(BF16)
        a_ref[...] = a
        g_s_ref[...] = dy * (gate * sg)

        def row(k, c):
            t = tt - 1 - k
            g = g_s_ref[pl.ds(t, 1), :] + c
            g_s_ref[pl.ds(t, 1), :] = g
            return a_ref[pl.ds(t, 1), :] * g

        carry_ref[...] = lax.fori_loop(0, tt, row, carry_ref[...])
        g = g_s_ref[...]
        h_prev = _rows_down(h_ref[...], h_before, 1)
        iu = ig * u
        d_iu = g * s
        dlog_a = (g * h_prev) * a - (g * iu) * (a * a) * inv_s
        dpre_a = (dlog_a * ((-LRU_C) * spl)) * r * (1.0 - r)
        dpre_i = (d_iu * u) * ig * (1.0 - ig)
        dlam = jnp.sum(dlog_a * r, axis=0, keepdims=True) * (LRU_C * _sigmoid(-small_ref[7:8, :]))
        du_parts = []
        for n in range(RNN_BLOCKS):
            sl = slice(n * W, (n + 1) * W)
            ub = u[:, sl].astype(BF16)
            da_n = dpre_a[:, sl].astype(BF16)
            di_n = dpre_i[:, sl].astype(BF16)
            dwa_ref[n] += lax.dot_general(ub, da_n, (((0,), (0,)), ((), ())),
                                          preferred_element_type=F32)
            dwi_ref[n] += lax.dot_general(ub, di_n, (((0,), (0,)), ((), ())),
                                          preferred_element_type=F32)
            du_parts.append(
                lax.dot_general(da_n, wa_ref[n], (((1,), (1,)), ((), ())), preferred_element_type=F32)
                + lax.dot_general(di_n, wi_ref[n], (((1,), (1,)), ((), ())), preferred_element_type=F32))
        du = d_iu * ig + jnp.concatenate(du_parts, axis=-1)
        for tap in range(CONV_WIDTH):
            ds_ref[tap:tap + 1, :] += jnp.sum(du * taps[tap], axis=0, keepdims=True)
        ds_ref[4:5, :] += jnp.sum(du, axis=0, keepdims=True)
        ds_ref[5:6, :] += jnp.sum(dpre_a, axis=0, keepdims=True)
        ds_ref[6:7, :] += jnp.sum(dpre_i, axis=0, keepdims=True)
        ds_ref[7:8, :] += dlam
        du_after = dunext_ref[...]
        du0 = jnp.zeros((tt, D), F32)
        for tap in range(CONV_WIDTH):
            du0 = du0 + _rows_up(du, du_after, CONV_WIDTH - 1 - tap) * small_ref[tap:tap + 1, :]
        dp_ref[:, :D] = du0.astype(BF16)
        dunext_ref[...] = du[0:SUBLANES, :]

    rev = lambda i: nt - 1 - i
    wspec = pl.BlockSpec((RNN_BLOCKS, W, W), lambda i: (0, 0, 0))
    return pl.pallas_call(
        body, name=name, grid=(nt,),
        in_specs=[pl.BlockSpec((tt, D), lambda i: (rev(i), 0)),
                  pl.BlockSpec((SUBLANES, D), lambda i: (jnp.maximum(rev(i) * hb - 1, 0), 0)),
                  pl.BlockSpec((tt, D), lambda i: (rev(i), 1)),
                  pl.BlockSpec((tt, D), lambda i: (rev(i), 0)),
                  pl.BlockSpec((SUBLANES, D), lambda i: (jnp.maximum(rev(i) * hb - 1, 0), 0)),
                  pl.BlockSpec((tt, D), lambda i: (rev(i), 0)),
                  pl.BlockSpec((SUBLANES, D), lambda i: (0, 0)),
                  wspec, wspec],
        out_specs=[pl.BlockSpec((tt, 2 * D), lambda i: (rev(i), 0)),
                   wspec, wspec, pl.BlockSpec((SUBLANES, D), lambda i: (0, 0))],
        out_shape=[jax.ShapeDtypeStruct((T, 2 * D), BF16),
                   jax.ShapeDtypeStruct((RNN_BLOCKS, W, W), F32),
                   jax.ShapeDtypeStruct((RNN_BLOCKS, W, W), F32),
                   jax.ShapeDtypeStruct((SUBLANES, D), F32)],
        scratch_shapes=[pltpu.VMEM((tt, D), F32), pltpu.VMEM((tt, D), F32),
                        pltpu.VMEM((SUBLANES, D), F32), pltpu.VMEM((1, D), F32)],
        compiler_params=_cparams("arbitrary"),
    )(proj, proj, proj, hs, hs, dy, small, wa, wi)


def _out_ln(a, w, x, g, b, *, tt, name):
    T, D = x.shape
    K = a.shape[1]

    def body(a_ref, w_ref, x_ref, g_ref, b_ref, y_ref, yb_ref, zh_ref, rs_ref):
        h = jnp.dot(a_ref[...].astype(BF16), w_ref[...].astype(BF16), preferred_element_type=F32)
        z = ALPHA * x_ref[...] + h
        mu = jnp.mean(z, axis=-1, keepdims=True)
        zc = z - mu
        rstd = lax.rsqrt(jnp.mean(zc * zc, axis=-1, keepdims=True) + LN_EPS)
        zh = zc * rstd
        zh_ref[...] = zh
        rs_ref[...] = rstd
        y = zh * g_ref[...] + b_ref[...]
        y_ref[...] = y
        yb_ref[...] = y.astype(BF16)

    blk = pl.BlockSpec((tt, D), lambda i: (i, 0))
    row = pl.BlockSpec((1, D), lambda i: (0, 0))
    return pl.pallas_call(
        body, name=name, grid=(T // tt,),
        in_specs=[pl.BlockSpec((tt, K), lambda i: (i, 0)), pl.BlockSpec((K, D), lambda i: (0, 0)),
                  blk, row, row],
        out_specs=[blk, blk, blk, pl.BlockSpec((tt, 1), lambda i: (i, 0))],
        out_shape=[jax.ShapeDtypeStruct((T, D), F32), jax.ShapeDtypeStruct((T, D), BF16),
                   jax.ShapeDtypeStruct((T, D), F32), jax.ShapeDtypeStruct((T, 1), F32)],
        compiler_params=_cparams("parallel"),
    )(a, w, x, g, b)


def _ln_bwd_tile(dy, zh_ref, rs_ref, g_ref, dz_ref, dzb_ref, dg_ref, db_ref, first):
    @pl.when(first)
    def _():
        dg_ref[...] = jnp.zeros_like(dg_ref)
        db_ref[...] = jnp.zeros_like(db_ref)

    zh = zh_ref[...]
    dg_ref[...] += jnp.sum(dy * zh, axis=0, keepdims=True)
    db_ref[...] += jnp.sum(dy, axis=0, keepdims=True)
    dzh = dy * g_ref[...]
    m1 = jnp.mean(dzh, axis=-1, keepdims=True)
    m2 = jnp.mean(dzh * zh, axis=-1, keepdims=True)
    dz = rs_ref[...] * (dzh - m1 - zh * m2)
    dz_ref[...] = dz
    dzb_ref[...] = dz.astype(BF16)


def _ln_bwd_specs(T, D, tt):
    blk = pl.BlockSpec((tt, D), lambda i: (i, 0))
    row = pl.BlockSpec((1, D), lambda i: (0, 0))
    return ([blk, pl.BlockSpec((tt, 1), lambda i: (i, 0)), row], [blk, blk, row, row],
            [jax.ShapeDtypeStruct((T, D), F32), jax.ShapeDtypeStruct((T, D), BF16),
             jax.ShapeDtypeStruct((1, D), F32), jax.ShapeDtypeStruct((1, D), F32)])


def _loss_ln_bwd(y, tgt, zh, rstd, g, *, tt, name):
    T, D = y.shape
    ln_in, ln_out, ln_shapes = _ln_bwd_specs(T, D, tt)

    def body(y_ref, t_ref, zh_ref, rs_ref, g_ref, l_ref, dz_ref, dzb_ref, dg_ref, db_ref):
        first = pl.program_id(0) == 0

        @pl.when(first)
        def _():
            l_ref[...] = jnp.zeros_like(l_ref)

        e = y_ref[...] - t_ref[...]
        l_ref[...] += jnp.sum(e * e, axis=0, keepdims=True) * (0.5 / D)
        _ln_bwd_tile(e * (1.0 / D), zh_ref, rs_ref, g_ref, dz_ref, dzb_ref, dg_ref, db_ref, first)

    blk = pl.BlockSpec((tt, D), lambda i: (i, 0))
    return pl.pallas_call(
        body, name=name, grid=(T // tt,),
        in_specs=[blk, blk] + ln_in,
        out_specs=[pl.BlockSpec((1, D), lambda i: (0, 0))] + ln_out,
        out_shape=[jax.ShapeDtypeStruct((1, D), F32)] + ln_shapes,
        compiler_params=_cparams("arbitrary"),
    )(y, tgt, zh, rstd, g)


def _dx_ln_bwd(a, b, add, zh, rstd, g, *, tm, name):
    T, D = add.shape
    na = len(a)
    K = sum(p.shape[1] for p in a)
    ln_in, ln_out, ln_shapes = _ln_bwd_specs(T, D, tm)

    def body(*refs):
        a_refs, b_ref, add_ref = refs[:na], refs[na], refs[na + 1]
        av = [r[...].astype(BF16) for r in a_refs]
        av = av[0] if na == 1 else jnp.concatenate(av, axis=1)
        dy = lax.dot_general(av, b_ref[...].astype(BF16), (((1,), (1,)), ((), ())),
                             preferred_element_type=F32) + ALPHA * add_ref[...]
        _ln_bwd_tile(dy, *refs[na + 2:], pl.program_id(0) == 0)

    return pl.pallas_call(
        body, name=name, grid=(T // tm,),
        in_specs=[pl.BlockSpec((tm, p.shape[1]), lambda i: (i, 0)) for p in a]
        + [pl.BlockSpec((D, K), lambda i: (0, 0)), pl.BlockSpec((tm, D), lambda i: (i, 0))] + ln_in,
        out_specs=ln_out, out_shape=ln_shapes,
        compiler_params=_cparams("arbitrary"),
    )(*a, b, add, zh, rstd, g)


def _row_tile(rows, target):
    best = SUBLANES
    for t in range(SUBLANES, target + 1, SUBLANES):
        if rows % t == 0:
            best = t
    return best


def _add_own(g, recv, c_idx, *, tr, name):
    _, M, R, C = g.shape

    def body(c_ref, g_ref, r_ref, o_ref, ob_ref):
        s = g_ref[0] + r_ref[...]
        o_ref[...] = s
        ob_ref[...] = s.astype(BF16)

    blk = pl.BlockSpec((1, tr, C), lambda k, i, c: (k, i, 0))
    return pl.pallas_call(
        body, name=name,
        grid_spec=pltpu.PrefetchScalarGridSpec(
            num_scalar_prefetch=1, grid=(M, R // tr),
            in_specs=[pl.BlockSpec((1, 1, tr, C), lambda k, i, c: (c[0], k, i, 0)), blk],
            out_specs=[blk, blk]),
        out_shape=[jax.ShapeDtypeStruct((M, R, C), F32), jax.ShapeDtypeStruct((M, R, C), BF16)],
        compiler_params=_cparams("parallel", "parallel"),
    )(c_idx, g, recv)


def _adamw_math(g, w_ref, m_ref, v_ref, g_ref, d_ref, nm_ref, nv_ref):
    nm = ADAM_B1 * m_ref[...] + (1.0 - ADAM_B1) * g
    nv = ADAM_B2 * v_ref[...] + (1.0 - ADAM_B2) * (g * g)
    m_hat = nm / (1.0 - ADAM_B1 ** ADAM_STEP)
    v_hat = nv / (1.0 - ADAM_B2 ** ADAM_STEP)
    g_ref[...] = g
    nm_ref[...] = nm
    nv_ref[...] = nv
    d_ref[...] = (-ADAM_LR) * (m_hat / (jnp.sqrt(v_hat) + ADAM_EPS) + ADAM_WD * w_ref[...])


def _adamw(parts, w, m, v, *, tr, name):
    n, R, C = parts.shape
    tr = min(tr, R)

    def body(p_ref, w_ref, m_ref, v_ref, *out_refs):
        g = p_ref[0]
        for k in range(1, n):
            g = g + p_ref[k]
        _adamw_math(g, w_ref, m_ref, v_ref, *out_refs)

    blk = pl.BlockSpec((tr, C), lambda i: (i, 0))
    out = jax.ShapeDtypeStruct((R, C), F32)
    return pl.pallas_call(
        body, name=name, grid=(R // tr,),
        in_specs=[pl.BlockSpec((n, tr, C), lambda i: (0, i, 0)), blk, blk, blk],
        out_specs=[blk, blk, blk, blk], out_shape=[out, out, out, out],
        compiler_params=_cparams("parallel"),
    )(parts, w, m, v)


def _adamw_shard(parts_by_layer, place, w, m, v, *, tr, name):
    L, R, C = w.shape
    flat = [(l, a, pick) for l, parts in enumerate(parts_by_layer) for a, pick in parts]
    n = len(flat)

    def body(place_ref, *refs):
        w_ref, m_ref, v_ref = refs[n:n + 3]
        for layer in range(L):
            @pl.when(pl.program_id(0) == layer)
            def _(layer=layer):
                g = None
                for (l, _, _), r in zip(flat, refs[:n]):
                    if l == layer:
                        blk = r[(0,) * (len(r.shape) - 3)].astype(F32)
                        g = blk if g is None else g + blk
                _adamw_math(g, w_ref, m_ref, v_ref, *refs[n + 3:])

    blk = pl.BlockSpec((1, tr, C), lambda ly, i, s: (ly, i, 0))

    def part_spec(l, a, pick):
        return pl.BlockSpec((1,) * (a.ndim - 2) + (tr, C),
                            lambda ly, i, s: (*pick(s), jnp.where(ly == l, i, 0), 0))

    out = jax.ShapeDtypeStruct(w.shape, F32)
    return pl.pallas_call(
        body, name=name,
        grid_spec=pltpu.PrefetchScalarGridSpec(
            num_scalar_prefetch=1, grid=(L, R // tr),
            in_specs=[part_spec(*f) for f in flat] + [blk, blk, blk],
            out_specs=[blk, blk, blk, blk]),
        out_shape=[out, out, out, out],
        compiler_params=_cparams("arbitrary", "arbitrary"),
    )(place, *[a for _, a, _ in flat], w, m, v)


def _two_stage_parts(h, recv):
    return [(h, lambda s: (s[0], 0))] + [(recv, lambda s, d=d: (s[0] ^ d, 0)) for d in (1, 2, 3)]


def _direct_parts(g, recv):
    return [(g, lambda s: (s[1], s[0]))] + [
        (recv, lambda s, a=p // 4, d=p % 4: (s[1] ^ a, s[0] ^ d)) for p in range(1, 8)]


SHARD_AXIS = dict(attn_w_in=1, attn_w_out=0, rnn_w_in=1, rnn_w_out=0, rnn_w_a=1, rnn_w_i=1,
                  rnn_conv_w=1, rnn_conv_b=0, rnn_b_a=0, rnn_b_i=0, rnn_lambda=0)
RNN_ROWED = ("rnn_w_out", "rnn_w_a", "rnn_w_i")
SMALL = ("rnn_conv_w", "rnn_conv_b", "rnn_b_a", "rnn_b_i", "rnn_lambda")
PACK_C = 1024


def _elems(shape):
    n = 1
    for s in shape:
        n *= s
    return n


def _pack_rows(p, idx, dtype):
    parts = [p[k][idx].astype(dtype).reshape(-1, PACK_C) for k in RNN_ROWED]
    small = jnp.concatenate([p[k][idx].reshape(-1) for k in SMALL])
    tile_rows = SUBLANES * (4 // jnp.dtype(dtype).itemsize)
    if dtype == BF16:
        small = lax.bitcast_convert_type(small, BF16)
    small = small.reshape(-1, PACK_C)
    parts.append(jnp.pad(small, ((0, tile_rows - small.shape[0]), (0, 0))))
    return jnp.concatenate(parts, axis=0)


def _unpack_rows(flat, shapes):
    lead = flat.shape[:-2]
    out, r = {}, 0
    for k in RNN_ROWED:
        n = _elems(shapes[k]) // PACK_C
        out[k] = flat[..., r:r + n, :].reshape(lead + shapes[k])
        r += n
    n_small = sum(_elems(shapes[k]) for k in SMALL)
    small = flat[..., r:r + n_small // PACK_C, :].reshape(lead + (-1,))
    o = 0
    for k in SMALL:
        n = _elems(shapes[k])
        out[k] = small[..., o:o + n].reshape(lead + shapes[k])
        o += n
    return out


def _join_columns(g, width, *, tr, name):
    _, _, R, S = g.shape

    def body(*refs):
        o_ref = refs[8]
        parts = [refs[r][0, 0].astype(F32) for r in range(8)]
        if width > 8 * S:
            parts.append(jnp.zeros((tr, width - 8 * S), F32))
        o_ref[...] = jnp.concatenate(parts, axis=-1).astype(o_ref.dtype)

    def shard(r):
        return pl.BlockSpec((1, 1, tr, S), lambda i: (r % 2, r // 2, i, 0))

    return pl.pallas_call(
        body, name=name, grid=(R // tr,),
        in_specs=[shard(r) for r in range(8)],
        out_specs=pl.BlockSpec((tr, width), lambda i: (i, 0)),
        out_shape=jax.ShapeDtypeStruct((R, width), g.dtype),
        compiler_params=_cparams("parallel"),
    )(*([g] * 8))


def _split_columns(parts, S, *, tr, name):
    R = parts[0].shape[0]
    n = len(parts)

    def body(*refs):
        o_ref = refs[n]
        x = jnp.concatenate([r[...] for r in refs[:n]], axis=1)
        for r in range(8):
            o_ref[r % 2, r // 2] = x[:, r * S:(r + 1) * S]

    return pl.pallas_call(
        body, name=name, grid=(R // tr,),
        in_specs=[pl.BlockSpec((tr, p.shape[1]), lambda i: (i, 0)) for p in parts],
        out_specs=pl.BlockSpec((2, 4, tr, S), lambda i: (0, 0, i, 0)),
        out_shape=jax.ShapeDtypeStruct((2, 4, R, S), parts[0].dtype),
        compiler_params=_cparams("parallel"),
    )(*parts)


def _to_full(g, k, sh):
    ax, nd = SHARD_AXIS[k], len(sh)
    perm = tuple(range(2, 2 + ax)) + (1, 0) + tuple(range(2 + ax, 2 + nd))
    return g.transpose(perm).reshape(sh[:ax] + (8 * sh[ax],) + sh[ax + 1:])


def _from_full(full, k, sh):
    ax, nd = SHARD_AXIS[k], len(sh)
    t = full.reshape(sh[:ax] + (4, 2, sh[ax]) + sh[ax + 1:])
    return t.transpose((ax + 1, ax) + tuple(range(ax)) + tuple(range(ax + 2, nd + 2)))


def _unpack_gathered_rows(g, shapes):
    out, r = {}, 0
    for k in RNN_ROWED:
        n = _elems(shapes[k]) // PACK_C
        out[k] = _to_full(g[:, :, r:r + n].reshape((2, 4) + shapes[k]), k, shapes[k])
        r += n
    n_small = sum(_elems(shapes[k]) for k in SMALL)
    nr = 2 * n_small // PACK_C
    small = lax.bitcast_convert_type(g[:, :, r:r + nr].reshape(2, 4, n_small, 2), F32)
    o = 0
    for k in SMALL:
        n = _elems(shapes[k])
        out[k] = _to_full(small[:, :, o:o + n].reshape((2, 4) + shapes[k]), k, shapes[k])
        o += n
    return out


def _pack_grad_rows(full, shapes):
    parts = [_from_full(full[k], k, shapes[k]).reshape(2, 4, -1, PACK_C) for k in RNN_ROWED]
    small = jnp.concatenate(
        [_from_full(full[k], k, shapes[k]).reshape(2, 4, -1) for k in SMALL], axis=-1)
    small = small.reshape(2, 4, -1, PACK_C)
    parts.append(jnp.pad(small, ((0, 0), (0, 0), (0, SUBLANES - small.shape[2]), (0, 0))))
    return jnp.concatenate(parts, axis=2)


def kernel(x, ln_g, ln_b, attn_w_in, attn_b_f, attn_w_out, rnn_w_in, rnn_conv_w, rnn_conv_b, rnn_w_a, rnn_b_a, rnn_w_i, rnn_b_i, rnn_lambda, rnn_w_out, loss_target, m_ln_g, m_ln_b, m_attn_w_in, m_attn_b_f, m_attn_w_out, m_rnn_w_in, m_rnn_conv_w, m_rnn_conv_b, m_rnn_w_a, m_rnn_b_a, m_rnn_w_i, m_rnn_b_i, m_rnn_lambda, m_rnn_w_out, v_ln_g, v_ln_b, v_attn_w_in, v_attn_b_f, v_attn_w_out, v_rnn_w_in, v_rnn_conv_w, v_rnn_conv_b, v_rnn_w_a, v_rnn_b_a, v_rnn_w_i, v_rnn_b_i, v_rnn_lambda, v_rnn_w_out):
    w_loc = dict(attn_w_in=attn_w_in, attn_w_out=attn_w_out, rnn_w_in=rnn_w_in, rnn_w_a=rnn_w_a,
                 rnn_w_i=rnn_w_i, rnn_w_out=rnn_w_out, rnn_conv_w=rnn_conv_w, rnn_conv_b=rnn_conv_b,
                 rnn_b_a=rnn_b_a, rnn_b_i=rnn_b_i, rnn_lambda=rnn_lambda)
    m_loc = dict(attn_w_in=m_attn_w_in, attn_w_out=m_attn_w_out, rnn_w_in=m_rnn_w_in,
                 rnn_w_a=m_rnn_w_a, rnn_w_i=m_rnn_w_i, rnn_w_out=m_rnn_w_out,
                 rnn_conv_w=m_rnn_conv_w, rnn_conv_b=m_rnn_conv_b, rnn_b_a=m_rnn_b_a,
                 rnn_b_i=m_rnn_b_i, rnn_lambda=m_rnn_lambda)
    v_loc = dict(attn_w_in=v_attn_w_in, attn_w_out=v_attn_w_out, rnn_w_in=v_rnn_w_in,
                 rnn_w_a=v_rnn_w_a, rnn_w_i=v_rnn_w_i, rnn_w_out=v_rnn_w_out,
                 rnn_conv_w=v_rnn_conv_w, rnn_conv_b=v_rnn_conv_b, rnn_b_a=v_rnn_b_a,
                 rnn_b_i=v_rnn_b_i, rnn_lambda=v_rnn_lambda)
    shapes = {k: tuple(a.shape[1:]) for k, a in w_loc.items()}
    T, D = x.shape[1], x.shape[2]
    n_f = attn_b_f.shape[1]
    tb = min(1024, T)
    tb_bwd = min(512, T)
    tt_rg = min(128, T)
    tt_ln = min(256, T)
    c_idx = lax.axis_index("c").astype(jnp.int32).reshape(1)
    me_idx = (2 * lax.axis_index("x") + lax.axis_index("y")).astype(jnp.int32).reshape(1)
    place = jnp.concatenate([me_idx, c_idx])

    def attn_w_in_full(g_in, idx):
        return _join_columns(g_in, 4 * D + LANES, tr=256, name=f"a_join{idx}")

    def attn_w_out_full(g_out):
        return _to_full(g_out, "attn_w_out", shapes["attn_w_out"])

    def rnn_weights(g_in, g_rows, idx):
        w = _unpack_gathered_rows(g_rows, shapes)
        w["rnn_w_in"] = _join_columns(g_in, 2 * D, tr=256, name=f"r_join{idx}")
        w["small"] = jnp.concatenate([w["rnn_conv_w"], w["rnn_conv_b"][None], w["rnn_b_a"][None],
                                      w["rnn_b_i"][None], w["rnn_lambda"][None]])
        return w

    g0 = _ag_c(_run_exchange(_Exchange("gather", [attn_w_in[0].astype(BF16)]), "ag_w0_xy"),
               "ag_w0_c")
    later = _Exchange("gather8", [attn_w_in[1].astype(BF16)] + [
        a for i in range(2) for a in (attn_w_out[i].astype(BF16), rnn_w_in[i].astype(BF16),
                                      _pack_rows(w_loc, i, BF16))])
    w_attn_in, w_attn_out, w_rnn = [attn_w_in_full(g0[0], 0), None], [None, None], [None, None]
    bf_rows = jnp.pad(attn_b_f, ((0, 0), (0, LANES - n_f)))[:, None, :]

    xs, xb, saved = [x[0]], [x[0]], []
    for layer in range(DEPTH):
        idx, xl, xm = layer // 2, xs[-1], xb[-1]
        if layer % 2 == 0:
            proj = _matmul(xm, w_attn_in[idx], trans_b=False, tm=512, tn=1408,
                           name=f"a_proj{layer}")
            cum_t = _cumsum_fwd(proj, bf_rows[idx], tt=min(512, T), name=f"a_cum{layer}")
            cum2 = cum_t[:N_HEADS].reshape(N_PAIRS, 2, T)
            o, og, lp, *got = _flash_fwd(proj, cum2.reshape(N_PAIRS, 2, T // tb, tb), tb=tb,
                                         name=f"a_fwd{layer}", host=later if layer == 0 else None)
            cum4 = cum2.reshape(N_PAIRS, 2, T // tb_bwd, tb_bwd)
            if layer == 0:
                w_attn_in[1] = attn_w_in_full(got[0], 1)
                w_attn_out = [attn_w_out_full(got[1 + 3 * i]) for i in range(2)]
                w_rnn = [rnn_weights(got[2 + 3 * i], got[3 + 3 * i], i) for i in range(2)]
            branch, w_out = og, w_attn_out[idx]
            saved.append((proj, cum4, o, og, lp))
        else:
            w = w_rnn[idx]
            proj = _matmul(xm, w["rnn_w_in"], trans_b=False, tm=512, tn=1024,
                           name=f"r_proj{layer}")
            hs, yr = _rg_fwd(proj, w["small"], w["rnn_w_a"], w["rnn_w_i"], tt=tt_rg,
                             name=f"r_fwd{layer}")
            branch, w_out = yr, w["rnn_w_out"]
            saved.append((proj, hs, yr))
        y, yb, zh, rstd = _out_ln(branch, w_out, xl, ln_g[layer][None], ln_b[layer][None],
                                  tt=512, name=f"out_ln{layer}")
        saved[-1] = saved[-1] + (zh, rstd)
        xs.append(y)
        xb.append(yb)

    def ln_below(layer):
        return saved[layer][-2:] + (ln_g[layer][None],)

    loss_lanes, *ln_grads = _loss_ln_bwd(xs[-1], loss_target[0], *ln_below(DEPTH - 1), tt=tt_ln,
                                         name="loss_ln_bwd")
    loss = lax.psum(jnp.sum(loss_lanes), ("x", "y", "c"))

    def reduce_pair(gs, layer):
        recv = _rs_c(gs, f"rs_c{layer}")
        outs = [_add_own(g, r, c_idx, tr=_row_tile(g.shape[2], 512), name=f"rs_add{layer}_{n}")
                for n, (g, r) in enumerate(zip(gs, recv))]
        return [o[0][:, None] for o in outs], [o[1][:, None] for o in outs]

    part, got_parts = [None] * DEPTH, [None] * DEPTH
    d_ln_g, d_ln_b, d_bf = [None] * DEPTH, [None] * DEPTH, [None, None]
    for layer in reversed(range(DEPTH)):
        idx, xm = layer // 2, xb[layer]
        dz, dzb, dg, db = ln_grads
        d_ln_g[layer], d_ln_b[layer] = dg[0], db[0]
        if layer % 2 == 0:
            w_in, w_out = w_attn_in[idx], w_attn_out[idx]
            proj, cum4, o, og, lp = saved[layer][:5]
            dog = _matmul(dzb, w_out, trans_b=True, tm=512, tn=1024, name=f"a_dog{layer}")
            dwo = _matmul_tn(og, dzb, tm=512, tn=1024, tk=1024, name=f"a_dwo{layer}")
            g_out = _from_full(dwo, "attn_w_out", shapes["attn_w_out"])
            riders = [l for l in range(layer + 1, DEPTH) if got_parts[l] is None]
            early = [g_out] if layer == 0 else []
            host = _Exchange("scatter8", [g for l in riders for g in part[l]] + early)
            dq, dgate, dk, dv, dcum_q, dcum_k, *got = _flash_bwd(proj, cum4, o, dog, lp, tb=tb_bwd,
                                                                 name=f"a_bwd{layer}", host=host)
            for l in riders:
                got_parts[l], got = got[:len(part[l])], got[len(part[l]):]
            dcum_t = (dcum_q.transpose(0, 2, 1, 3) + dcum_k).reshape(N_HEADS, T)
            dcum_t = jnp.pad(dcum_t, ((0, LANES - N_HEADS), (0, 0)))
            df, dbf = _cumsum_bwd(dcum_t, proj, bf_rows[idx], tt=min(512, T), name=f"a_dcum{layer}")
            d_bf[idx] = dbf[0, :n_f]
            dproj = [dq, dk, dv, dgate, df]
            dwi = _matmul_tn_parts(xm, dproj, tm=512, tk=1024, name=f"a_dwi{layer}")
            g_in = _split_columns(dwi, shapes["attn_w_in"][1], tr=256, name=f"a_split{layer}")
            if layer > 0:
                part[layer] = [g_in, g_out]
                ln_grads = _dx_ln_bwd(dproj, w_in, dz, *ln_below(layer - 1), tm=256,
                                      name=f"a_dx{layer}")
            else:
                half, narrow = reduce_pair([g_in], layer)
                dy, recv = _matmul(dproj, w_in, trans_b=True, tm=512, tn=1024, name=f"a_dx{layer}",
                                   add=dz, add_scale=ALPHA, host=_Exchange("scatter", narrow))
                last_parts = [_two_stage_parts(half[0], recv), _direct_parts(g_out, got[0])]
        else:
            w = w_rnn[idx]
            proj, hs, yr = saved[layer][:3]
            dyr = _matmul(dzb, w["rnn_w_out"], trans_b=True, tm=512, tn=1024, name=f"r_dy{layer}")
            dwo = _matmul_tn(yr, dzb, tm=512, tn=1024, tk=1024, name=f"r_dwo{layer}")
            dproj, dwa, dwi_, dsm = _rg_bwd(proj, hs, dyr, w["small"], w["rnn_w_a"], w["rnn_w_i"],
                                            tt=tt_rg, name=f"r_bwd{layer}")
            dwin = _matmul_tn(xm, dproj, tm=512, tn=2048, tk=1024, name=f"r_dwi{layer}")
            ln_grads = _dx_ln_bwd([dproj], w["rnn_w_in"], dz, *ln_below(layer - 1), tm=512,
                                  name=f"r_dx{layer}")
            full = dict(rnn_w_out=dwo, rnn_w_a=dwa, rnn_w_i=dwi_, rnn_conv_w=dsm[0:4],
                        rnn_conv_b=dsm[4], rnn_b_a=dsm[5], rnn_b_i=dsm[6], rnn_lambda=dsm[7])
            part[layer] = [_split_columns([dwin], shapes["rnn_w_in"][1], tr=256,
                                          name=f"r_split{layer}"),
                           _pack_grad_rows(full, shapes)]
    grad_x = dy[None]

    def grad_parts(layer, n):
        return last_parts[n] if layer == 0 else _direct_parts(part[layer][n], got_parts[layer][n])

    def update(k, n, wmv):
        layers = [2 * idx + (0 if k.startswith("attn") else 1) for idx in range(2)]
        return _adamw_shard([grad_parts(layer, n) for layer in layers], place, *wmv,
                            tr=_row_tile(wmv[0].shape[1], 256), name=f"adamw_{k}")

    shard_outs = [dict() for _ in range(4)]
    for k, n in (("attn_w_in", 0), ("attn_w_out", 1), ("rnn_w_in", 0)):
        for j, a in enumerate(update(k, n, (w_loc[k], m_loc[k], v_loc[k]))):
            shard_outs[j][k] = a
    rows_wmv = [jnp.stack([_pack_rows(d, idx, F32) for idx in range(2)]) for d in (w_loc, m_loc, v_loc)]
    for j, a in enumerate(update("rnn_rows", 1, rows_wmv)):
        shard_outs[j].update(_unpack_rows(a, shapes))
    g_sh, d_sh, nm_sh, nv_sh = shard_outs

    def rep_pack(lg, lb, bf):
        rows = jnp.concatenate([lg, lb, jnp.pad(bf.reshape(1, -1), ((0, 0), (0, D - 2 * n_f)))])
        return jnp.pad(rows, ((0, 16 - rows.shape[0]), (0, 0)))

    rep = _all_gather(rep_pack(jnp.stack(d_ln_g), jnp.stack(d_ln_b), jnp.stack(d_bf)), "ag_rep")
    rg, rd, rm, rv = _adamw(rep.reshape(8, 16, D), rep_pack(ln_g, ln_b, attn_b_f),
                            rep_pack(m_ln_g, m_ln_b, m_attn_b_f),
                            rep_pack(v_ln_g, v_ln_b, v_attn_b_f), tr=16, name="adamw_rep")

    def rep_unpack(a):
        return dict(ln_g=a[0:DEPTH], ln_b=a[DEPTH:2 * DEPTH],
                    attn_b_f=a[2 * DEPTH, :2 * n_f].reshape(2, n_f))

    order = ("ln_g", "ln_b", "attn_w_in", "attn_b_f", "attn_w_out", "rnn_w_in", "rnn_conv_w",
             "rnn_conv_b", "rnn_w_a", "rnn_b_a", "rnn_w_i", "rnn_b_i", "rnn_lambda", "rnn_w_out")
    outs = [loss, grad_x]
    for sh, rp in ((g_sh, rg), (d_sh, rd), (nm_sh, rm), (nv_sh, rv)):
        allp = {**sh, **rep_unpack(rp)}
        outs.extend(allp[k] for k in order)
    return tuple(outs)
```

```python
import jax
import jax.numpy as jnp
from jax import lax
from jax.experimental import pallas as pl
from jax.experimental.pallas import tpu as pltpu

F32 = jnp.float32
BF16 = jnp.bfloat16

DEPTH = 4
N_HEADS = 16
HEAD_DIM = 64
N_PAIRS = N_HEADS // 2
RNN_BLOCKS = 4
RNN_BLOCK_WIDTH = 256
CONV_WIDTH = 4
LRU_C = 8.0
ALPHA = (2.0 * DEPTH) ** 0.25
LN_EPS = 1e-5
ADAM_LR, ADAM_B1, ADAM_B2, ADAM_EPS, ADAM_WD, ADAM_STEP = 0.001, 0.9, 0.999, 1e-8, 0.01, 10

LANES = 128
SUBLANES = 8
VMEM_LIMIT = 48 * 1024 * 1024

MESH = pl.DeviceIdType.MESH
HBM_SPEC = pl.BlockSpec(memory_space=pltpu.HBM)


def _cparams(*sem):
    return pltpu.CompilerParams(dimension_semantics=sem, vmem_limit_bytes=VMEM_LIMIT)


def _sigmoid(x):
    return 1.0 / (1.0 + jnp.exp(-x))


def _softplus(x):
    return jnp.maximum(x, 0.0) + jnp.log(1.0 + jnp.exp(-jnp.abs(x)))


def _a2a(src, *, group, bcast, name):
    n = 2 if group == "c" else 4
    blk = tuple(src.shape) if bcast else tuple(src.shape[1:])

    def body(src_ref, out_ref, send_sems, recv_sems, local_sem):
        x, y, c = lax.axis_index("x"), lax.axis_index("y"), lax.axis_index("c")
        if group == "c":
            me = c

            def peer(d):
                return (x, y, 1 - c), 1 - c
        else:
            me = 2 * x + y

            def peer(d):
                px, py = x ^ (d >> 1), y ^ (d & 1)
                return (px, py, c), 2 * px + py

        def block_for(k):
            return src_ref if bcast else src_ref.at[k]

        local = pltpu.make_async_copy(block_for(me), out_ref.at[me], local_sem)
        local.start()
        sends = []
        for d in range(1, n):
            dev, idx = peer(d)
            cp = pltpu.make_async_remote_copy(
                src_ref=block_for(idx), dst_ref=out_ref.at[me],
                send_sem=send_sems.at[d], recv_sem=recv_sems.at[d],
                device_id=dev, device_id_type=MESH)
            cp.start()
            sends.append(cp)
        for d in range(1, n):
            dev, idx = peer(d)
            pltpu.make_async_remote_copy(
                src_ref=block_for(idx), dst_ref=out_ref.at[idx],
                send_sem=send_sems.at[d], recv_sem=recv_sems.at[d],
                device_id=dev, device_id_type=MESH).wait_recv()
        for cp in sends:
            cp.wait_send()
        local.wait()

    return pl.pallas_call(
        body, name=name,
        out_shape=jax.ShapeDtypeStruct((n,) + blk, src.dtype),
        in_specs=[HBM_SPEC], out_specs=HBM_SPEC,
        scratch_shapes=[pltpu.SemaphoreType.DMA((n,)), pltpu.SemaphoreType.DMA((n,)),
                        pltpu.SemaphoreType.DMA],
    )(src)


def _all_gather(piece, name):
    return _a2a(_a2a(piece, group="xy", bcast=True, name=name + "_xy"),
                group="c", bcast=True, name=name + "_c")


D2D_CHUNKS = 16
ICI_CHUNKS = 8


def _row_chunks(rows, dtype, k):
    unit = SUBLANES * (4 // jnp.dtype(dtype).itemsize)
    assert rows % unit == 0
    units = rows // unit
    k = max(1, min(k, units))
    base, rem = divmod(units, k)
    out, r = [], 0
    for i in range(k):
        n = (base + (1 if i < rem else 0)) * unit
        out.append((r, n))
        r += n
    return out


def _chunks(shape, dtype, k):
    if len(shape) == 2:
        return [(pl.ds(r0, n),) for r0, n in _row_chunks(shape[0], dtype, k)]
    per = max(1, k // shape[0])
    return [(l, pl.ds(r0, n)) for l in range(shape[0]) for r0, n in _row_chunks(shape[1], dtype, per)]


def _mesh_place():
    x, y, c = lax.axis_index("x"), lax.axis_index("y"), lax.axis_index("c")
    return x, y, c, 2 * x + y


def _chip_peer(x, y, c, d):
    px, py = x ^ (d >> 1), y ^ (d & 1)
    return (px, py, c), 2 * px + py


def _remote(src, dst, send_sem, recv_sem, dev):
    return pltpu.make_async_remote_copy(src_ref=src, dst_ref=dst, send_sem=send_sem,
                                        recv_sem=recv_sem, device_id=dev, device_id_type=MESH)


def _comm_call(body, name, ins, out_shapes, n_sems, aliases=None):
    n = len(ins)
    return pl.pallas_call(
        body, name=name,
        out_shape=out_shapes, in_specs=[HBM_SPEC] * n, out_specs=[HBM_SPEC] * n,
        input_output_aliases=aliases or {},
        scratch_shapes=[pltpu.SemaphoreType.DMA((n_sems, n)), pltpu.SemaphoreType.DMA((n_sems, n))],
    )(*ins)


class _Exchange:
    def __init__(self, kind, arrays):
        self.kind, self.arrays, self.n = kind, list(arrays), len(arrays)
        self.is_gather, self.all8 = kind.startswith("gather"), kind.endswith("8")
        k = ICI_CHUNKS // 4 if self.all8 else ICI_CHUNKS
        if self.is_gather:
            self.chunks = [_chunks(a.shape, a.dtype, k) for a in arrays]
            self.out_shapes = [jax.ShapeDtypeStruct((2, 4) + tuple(a.shape), a.dtype) for a in arrays]
        else:
            lead = 2 if self.all8 else 1
            self.chunks = [_chunks(a.shape[lead:], a.dtype, k) for a in arrays]
            self.out_shapes = [jax.ShapeDtypeStruct(a.shape, a.dtype) for a in arrays]
        self.peers = list(range(1, 8 if self.all8 else 4))
        n_sems = len(self.peers) + 1
        self.sem_shapes = [pltpu.SemaphoreType.DMA((n_sems, self.n)),
                           pltpu.SemaphoreType.DMA((n_sems, self.n))]

    def _peer(self, x, y, c, me, p):
        a, d = p // 4, p % 4
        px, py = x ^ (d >> 1), y ^ (d & 1)
        pc = 1 - c if a else c
        if self.all8:
            return (px, py, pc), (pc, 2 * px + py), (c, me)
        return (px, py, pc), (2 * px + py,), (me,)

    def _blocks(self, srcs, outs, o, c, me, theirs, mine):
        if self.kind == "gather":
            return srcs[o], outs[o].at[(c,) + mine], outs[o].at[(c,) + theirs]
        if self.kind == "gather8":
            return srcs[o], outs[o].at[mine], outs[o].at[theirs]
        return srcs[o].at[theirs], outs[o].at[mine], outs[o].at[theirs]

    def start(self, srcs, outs, send_sems, recv_sems):
        x, y, c, me = _mesh_place()
        if self.is_gather:
            for o in range(self.n):
                for idx in self.chunks[o]:
                    pltpu.make_async_copy(srcs[o].at[idx], outs[o].at[(c, me) + idx],
                                          send_sems.at[0, o]).start()
        for p in self.peers:
            dev, theirs, mine = self._peer(x, y, c, me, p)
            for o in range(self.n):
                src, dst, _ = self._blocks(srcs, outs, o, c, me, theirs, mine)
                for idx in self.chunks[o]:
                    _remote(src.at[idx], dst.at[idx], send_sems.at[p, o], recv_sems.at[p, o],
                            dev).start()

    def wait(self, srcs, outs, send_sems, recv_sems):
        x, y, c, me = _mesh_place()
        for wait_recv in (True, False):
            for p in self.peers:
                dev, theirs, mine = self._peer(x, y, c, me, p)
                for o in range(self.n):
                    src, _, land = self._blocks(srcs, outs, o, c, me, theirs, mine)
                    cp = _remote(src, land, send_sems.at[p, o], recv_sems.at[p, o], dev)
                    cp.wait_recv() if wait_recv else cp.wait_send()
        if self.is_gather:
            for o in range(self.n):
                pltpu.make_async_copy(srcs[o], outs[o].at[c, me], send_sems.at[0, o]).wait()


def _run_exchange(ex, name):
    n = ex.n

    def body(*refs):
        srcs, outs, send_sems, recv_sems = refs[:n], refs[n:2 * n], refs[2 * n], refs[2 * n + 1]
        ex.start(srcs, outs, send_sems, recv_sems)
        ex.wait(srcs, outs, send_sems, recv_sems)

    return _comm_call(body, name, ex.arrays, ex.out_shapes, len(ex.peers) + 1)


def _ag_c(bufs, name):
    n = len(bufs)
    chunks = [_chunks(b.shape[2:], b.dtype, D2D_CHUNKS // 4) for b in bufs]

    def body(*refs):
        srcs, outs, send_sems, recv_sems = refs[:n], refs[n:2 * n], refs[2 * n], refs[2 * n + 1]
        x, y, c, _ = _mesh_place()
        sib = (x, y, 1 - c)
        for o in range(n):
            for k in range(4):
                for idx in chunks[o]:
                    _remote(srcs[o].at[(c, k) + idx], outs[o].at[(c, k) + idx],
                            send_sems.at[0, o], recv_sems.at[0, o], sib).start()
        for o in range(n):
            _remote(srcs[o].at[c], outs[o].at[1 - c], send_sems.at[0, o], recv_sems.at[0, o],
                    sib).wait_recv()
        for o in range(n):
            _remote(srcs[o].at[c], outs[o].at[1 - c], send_sems.at[0, o], recv_sems.at[0, o],
                    sib).wait_send()

    shapes = [jax.ShapeDtypeStruct(b.shape, b.dtype) for b in bufs]
    return _comm_call(body, name, bufs, shapes, 1, aliases={i: i for i in range(n)})


def _rs_c(gs, name):
    n = len(gs)
    chunks = [_chunks(g.shape[2:], g.dtype, max(1, D2D_CHUNKS // g.shape[1])) for g in gs]

    def body(*refs):
        srcs, outs, send_sems, recv_sems = refs[:n], refs[n:2 * n], refs[2 * n], refs[2 * n + 1]
        x, y, c, _ = _mesh_place()
        sib = (x, y, 1 - c)
        for o in range(n):
            for k in range(gs[o].shape[1]):
                for idx in chunks[o]:
                    _remote(srcs[o].at[(1 - c, k) + idx], outs[o].at[(k,) + idx],
                            send_sems.at[0, o], recv_sems.at[0, o], sib).start()
        for o in range(n):
            _remote(srcs[o].at[1 - c], outs[o], send_sems.at[0, o], recv_sems.at[0, o],
                    sib).wait_recv()
        for o in range(n):
            _remote(srcs[o].at[1 - c], outs[o], send_sems.at[0, o], recv_sems.at[0, o],
                    sib).wait_send()

    shapes = [jax.ShapeDtypeStruct(g.shape[1:], g.dtype) for g in gs]
    return _comm_call(body, name, gs, shapes, 1)


def _matmul(a, b, *, trans_b, tm, tn, name, add=None, add_scale=1.0, host=None):
    a_parts = list(a) if isinstance(a, (list, tuple)) else [a]
    M, K = a_parts[0].shape[0], sum(p.shape[1] for p in a_parts)
    N = b.shape[0] if trans_b else b.shape[1]
    tm, tn = min(tm, M), min(tn, N)
    assert M % tm == 0 and N % tn == 0
    dn = (((1,), (1,)), ((), ())) if trans_b else (((1,), (0,)), ((), ()))
    na = len(a_parts)

    def body(*refs):
        a_refs, b_ref, o_ref = refs[:na], refs[na], refs[-1]
        av = [r[...].astype(BF16) for r in a_refs]
        av = av[0] if na == 1 else jnp.concatenate(av, axis=1)
        r = lax.dot_general(av, b_ref[...].astype(BF16), dn, preferred_element_type=F32)
        if add is not None:
            r = r + add_scale * refs[na + 1][...]
        o_ref[...] = r

    b_spec = (pl.BlockSpec((tn, K), lambda j, i: (j, 0)) if trans_b
              else pl.BlockSpec((K, tn), lambda j, i: (0, j)))
    in_specs = [pl.BlockSpec((tm, p.shape[1]), lambda j, i: (i, 0)) for p in a_parts] + [b_spec]
    args = a_parts + [b]
    if add is not None:
        in_specs.append(pl.BlockSpec((tm, tn), lambda j, i: (i, j)))
        args.append(add)
    grid = (N // tn, M // tm)
    x_in, x_out, x_shapes, x_scratch, x_args = _host_specs(host)
    body = _hosted(body, len(args), 1, 0, host, grid)
    outs = pl.pallas_call(
        body, name=name, grid=grid,
        in_specs=in_specs + x_in,
        out_specs=[pl.BlockSpec((tm, tn), lambda j, i: (i, j))] + x_out,
        out_shape=[jax.ShapeDtypeStruct((M, N), F32)] + x_shapes,
        scratch_shapes=x_scratch,
        compiler_params=_cparams(*(("arbitrary",) * 2 if host else ("parallel",) * 2)),
    )(*args, *x_args)
    return outs if host else outs[0]


def _matmul_tn(a, b, *, tm, tn, tk, name):
    T, M = a.shape
    N = b.shape[1]
    tm, tn, tk = min(tm, M), min(tn, N), min(tk, T)
    assert M % tm == 0 and N % tn == 0 and T % tk == 0

    def body(a_ref, b_ref, o_ref):
        @pl.when(pl.program_id(2) == 0)
        def _():
            o_ref[...] = jnp.zeros_like(o_ref)

        o_ref[...] += lax.dot_general(a_ref[...].astype(BF16), b_ref[...].astype(BF16),
                                      (((0,), (0,)), ((), ())), preferred_element_type=F32)

    return pl.pallas_call(
        body, name=name, grid=(M // tm, N // tn, T // tk),
        in_specs=[pl.BlockSpec((tk, tm), lambda i, j, k: (k, i)),
                  pl.BlockSpec((tk, tn), lambda i, j, k: (k, j))],
        out_specs=pl.BlockSpec((tm, tn), lambda i, j, k: (i, j)),
        out_shape=jax.ShapeDtypeStruct((M, N), F32),
        compiler_params=_cparams("parallel", "parallel", "arbitrary"),
    )(a, b)


def _matmul_tn_parts(a, parts, *, tm, tk, name):
    T, M = a.shape
    tm, tk = min(tm, M), min(tk, T)
    assert M % tm == 0 and T % tk == 0
    n = len(parts)

    def body(*refs):
        a_ref, b_refs, o_refs = refs[0], refs[1:1 + n], refs[1 + n:]
        av = a_ref[...].astype(BF16)
        for b_ref, o_ref in zip(b_refs, o_refs):
            @pl.when(pl.program_id(1) == 0)
            def _(o_ref=o_ref):
                o_ref[...] = jnp.zeros_like(o_ref)

            o_ref[...] += lax.dot_general(av, b_ref[...].astype(BF16), (((0,), (0,)), ((), ())),
                                          preferred_element_type=F32)

    return pl.pallas_call(
        body, name=name, grid=(M // tm, T // tk),
        in_specs=[pl.BlockSpec((tk, tm), lambda i, k: (k, i))]
        + [pl.BlockSpec((tk, p.shape[1]), lambda i, k: (k, 0)) for p in parts],
        out_specs=[pl.BlockSpec((tm, p.shape[1]), lambda i, k: (i, 0)) for p in parts],
        out_shape=[jax.ShapeDtypeStruct((M, p.shape[1]), F32) for p in parts],
        compiler_params=_cparams("parallel", "arbitrary"),
    )(a, *parts)


def _head_masks(rows):
    lane = lax.broadcasted_iota(jnp.int32, (rows, LANES), 1)
    return lane < HEAD_DIM, lane >= HEAD_DIM


DIAG_STRIP = 256


def _visible(first_row, nr, nk):
    row = first_row + lax.broadcasted_iota(jnp.int32, (nr, nk), 0)
    return row >= lax.broadcasted_iota(jnp.int32, (nr, nk), 1)


def _diag_strips(tile, i, tb, carry):
    rb = min(DIAG_STRIP, tb)
    parts = [tile(i, r0, rb, r0 + rb, jax.tree.map(lambda a: a[r0:r0 + rb], carry), r0)
             for r0 in range(0, tb, rb)]
    return jax.tree.map(lambda *a: jnp.concatenate(a, axis=0), *parts)


def _hosted(body, n_in, n_out, n_scratch, host, grid):
    if host is None:
        return body
    nx = host.n

    def wrapped(*refs):
        ins, xsrcs = refs[:n_in], refs[n_in:n_in + nx]
        outs = refs[n_in + nx:n_in + nx + n_out]
        xouts = refs[n_in + nx + n_out:n_in + 2 * nx + n_out]
        scratch = refs[n_in + 2 * nx + n_out:n_in + 2 * nx + n_out + n_scratch]
        xsems = refs[n_in + 2 * nx + n_out + n_scratch:]
        step = pl.program_id(0) * grid[1] + pl.program_id(1)

        @pl.when(step == 0)
        def _():
            host.start(xsrcs, xouts, *xsems)

        body(*ins, *outs, *scratch)

        @pl.when(step == grid[0] * grid[1] - 1)
        def _():
            host.wait(xsrcs, xouts, *xsems)

    return wrapped


def _host_specs(host):
    if host is None:
        return [], [], [], [], []
    return ([HBM_SPEC] * host.n, [HBM_SPEC] * host.n, host.out_shapes, host.sem_shapes, host.arrays)


def _flash_fwd(proj, cum4, *, tb, name, host=None):
    T = proj.shape[0]
    D = N_HEADS * HEAD_DIM
    nb = T // tb
    cb = D // LANES
    x_in, x_out, x_shapes, x_scratch, x_args = _host_specs(host)

    def body(q_ref, k_ref, v_ref, g_ref, cum_ref, o_ref, og_ref, lp_ref, kb_ref, vb_ref):
        i = pl.program_id(1)

        @pl.when(i == 0)
        def _():
            kb_ref[...] = k_ref[...].astype(BF16)
            vb_ref[...] = v_ref[...].astype(BF16)

        q = q_ref[...] * (HEAD_DIM ** -0.5)
        masks = _head_masks(tb)
        qh = [jnp.where(masks[h], q, 0.0).astype(BF16) for h in range(2)]
        cref = [cum_ref[0, h, pl.ds(i, 1), :][:, 0:1] for h in range(2)]

        def tile(kbi, r0, nr, nk, carry, first_row):
            k0 = pl.multiple_of(kbi * tb, tb)
            kblk = kb_ref[pl.ds(k0, nk), :]
            vblk = vb_ref[pl.ds(k0, nk), :]
            new = []
            for h in range(2):
                m, l, acc = carry[h]
                s = lax.dot_general(qh[h][r0:r0 + nr], kblk, (((1,), (1,)), ((), ())),
                                    preferred_element_type=F32)
                s = s + (cref[h] - cum_ref[0, h, pl.ds(kbi, 1), :][:, 0:nk])
                if first_row is not None:
                    s = jnp.where(_visible(first_row, nr, nk), s, -jnp.inf)
                m_new = jnp.maximum(m, jnp.max(s, axis=-1, keepdims=True))
                alpha = jnp.exp(m - m_new)
                p = jnp.exp(s - m_new)
                l = alpha * l + jnp.sum(p, axis=-1, keepdims=True)
                acc = alpha * acc + jnp.dot(p.astype(BF16), vblk, preferred_element_type=F32)
                new.append((m_new, l, acc))
            return tuple(new)

        init1 = (jnp.full((tb, 1), -jnp.inf, F32), jnp.zeros((tb, 1), F32),
                 jnp.zeros((tb, LANES), F32))
        carry = lax.fori_loop(0, i, lambda kbi, c: tile(kbi, 0, tb, tb, c, None), (init1, init1))
        outs = []
        for h, (m, l, acc) in enumerate(_diag_strips(tile, i, tb, carry)):
            outs.append(acc / l)
            lp_ref[h] = jnp.broadcast_to(m + jnp.log(l) - cref[h], (tb, LANES))
        o = jnp.where(masks[0], outs[0], outs[1])
        o_ref[...] = o
        gate = g_ref[...]
        og_ref[...] = (o * (gate * _sigmoid(gate))).astype(BF16)

    body = _hosted(body, 5, 3, 2, host, (N_PAIRS, nb))
    return pl.pallas_call(
        body, name=name, grid=(N_PAIRS, nb),
        in_specs=[pl.BlockSpec((tb, LANES), lambda j, i: (i, j)),
                  pl.BlockSpec((T, LANES), lambda j, i: (0, cb + j)),
                  pl.BlockSpec((T, LANES), lambda j, i: (0, 2 * cb + j)),
                  pl.BlockSpec((tb, LANES), lambda j, i: (i, 3 * cb + j)),
                  pl.BlockSpec((1, 2, nb, tb), lambda j, i: (j, 0, 0, 0))] + x_in,
        out_specs=[pl.BlockSpec((tb, LANES), lambda j, i: (i, j)),
                   pl.BlockSpec((tb, LANES), lambda j, i: (i, j)),
                   pl.BlockSpec((2, tb, LANES), lambda j, i: (j, i, 0))] + x_out,
        out_shape=[jax.ShapeDtypeStruct((T, D), F32), jax.ShapeDtypeStruct((T, D), BF16),
                   jax.ShapeDtypeStruct((N_HEADS, T, LANES), F32)] + x_shapes,
        scratch_shapes=[pltpu.VMEM((T, LANES), BF16), pltpu.VMEM((T, LANES), BF16)] + x_scratch,
        compiler_params=_cparams("arbitrary", "arbitrary"),
    )(proj, proj, proj, proj, cum4, *x_args)


def _flash_bwd(proj, cum4, o, dog, lp, *, tb, name, host=None):
    T = proj.shape[0]
    D = N_HEADS * HEAD_DIM
    nb = T // tb
    cb = D // LANES
    x_in, x_out, x_shapes, x_scratch, x_args = _host_specs(host)

    def body(q_ref, k_ref, v_ref, g_ref, cum_ref, o_ref, dog_ref, lp_ref,
             dq_ref, dg_ref, dk_ref, dv_ref, dcq_ref, dck_ref,
             kb_ref, vb_ref, dka_ref, dva_ref, dca_ref):
        i = pl.program_id(1)

        @pl.when(i == 0)
        def _():
            kb_ref[...] = k_ref[...].astype(BF16)
            vb_ref[...] = v_ref[...].astype(BF16)
            dka_ref[...] = jnp.zeros_like(dka_ref)
            dva_ref[...] = jnp.zeros_like(dva_ref)
            dca_ref[...] = jnp.zeros_like(dca_ref)

        gate = g_ref[...]
        sg = _sigmoid(gate)
        o = o_ref[...]
        dog = dog_ref[...]
        do = dog * (gate * sg)
        dg_ref[...] = (dog * o * (sg * (1.0 + gate * (1.0 - sg)))).astype(BF16)
        q = q_ref[...] * (HEAD_DIM ** -0.5)
        masks = _head_masks(tb)
        qh = [jnp.where(masks[h], q, 0.0).astype(BF16) for h in range(2)]
        doh = [jnp.where(masks[h], do, 0.0).astype(BF16) for h in range(2)]
        delta = [jnp.sum(jnp.where(masks[h], do * o, 0.0), axis=-1, keepdims=True) for h in range(2)]
        lph = [lp_ref[h][:, 0:1] for h in range(2)]

        def tile(kbi, r0, nr, nk, carry, first_row):
            k0 = pl.multiple_of(kbi * tb, tb)
            kblk = kb_ref[pl.ds(k0, nk), :]
            vblk = vb_ref[pl.ds(k0, nk), :]
            new, dk, dv = [], None, None
            for h in range(2):
                acc, rs = carry[h]
                q_h, do_h = qh[h][r0:r0 + nr], doh[h][r0:r0 + nr]
                s = lax.dot_general(q_h, kblk, (((1,), (1,)), ((), ())), preferred_element_type=F32)
                p = jnp.exp(s - cum_ref[0, h, pl.ds(kbi, 1), :][:, 0:nk] - lph[h][r0:r0 + nr])
                if first_row is not None:
                    p = jnp.where(_visible(first_row, nr, nk), p, 0.0)
                dp = lax.dot_general(do_h, vblk, (((1,), (1,)), ((), ())),
                                     preferred_element_type=F32)
                ds = p * (dp - delta[h][r0:r0 + nr])
                pb, dsb = p.astype(BF16), ds.astype(BF16)
                dv_h = lax.dot_general(pb, do_h, (((0,), (0,)), ((), ())),
                                       preferred_element_type=F32)
                dk_h = lax.dot_general(dsb, q_h, (((0,), (0,)), ((), ())),
                                       preferred_element_type=F32)
                dv = dv_h if dv is None else dv + dv_h
                dk = dk_h if dk is None else dk + dk_h
                dca_ref[h, pl.ds(kbi, 1), pl.ds(0, nk)] -= jnp.sum(ds, axis=0, keepdims=True)
                new.append((acc + jnp.dot(dsb, kblk, preferred_element_type=F32),
                            rs + jnp.sum(ds, axis=-1, keepdims=True)))
            dka_ref[pl.ds(k0, nk), :] += dk
            dva_ref[pl.ds(k0, nk), :] += dv
            return tuple(new)

        init1 = (jnp.zeros((tb, LANES), F32), jnp.zeros((tb, 1), F32))
        carry = lax.fori_loop(0, i, lambda kbi, c: tile(kbi, 0, tb, tb, c, None), (init1, init1))
        dqs = []
        for h, (acc, rs) in enumerate(_diag_strips(tile, i, tb, carry)):
            dqs.append(acc)
            dcq_ref[0, 0, pl.ds(h, 1), :] = jnp.broadcast_to(rs, (tb, LANES)).T[0:1, :]
        dq_ref[...] = (jnp.where(masks[0], dqs[0], dqs[1]) * (HEAD_DIM ** -0.5)).astype(BF16)

        @pl.when(i == nb - 1)
        def _():
            dk_ref[...] = dka_ref[...].astype(BF16)
            dv_ref[...] = dva_ref[...].astype(BF16)
            dck_ref[0] = dca_ref[...]

    blk = pl.BlockSpec((tb, LANES), lambda j, i: (i, j))
    full = pl.BlockSpec((T, LANES), lambda j, i: (0, j))
    body = _hosted(body, 8, 6, 5, host, (N_PAIRS, nb))
    return pl.pallas_call(
        body, name=name, grid=(N_PAIRS, nb),
        in_specs=[blk,
                  pl.BlockSpec((T, LANES), lambda j, i: (0, cb + j)),
                  pl.BlockSpec((T, LANES), lambda j, i: (0, 2 * cb + j)),
                  pl.BlockSpec((tb, LANES), lambda j, i: (i, 3 * cb + j)),
                  pl.BlockSpec((1, 2, nb, tb), lambda j, i: (j, 0, 0, 0)),
                  blk, blk, pl.BlockSpec((2, tb, LANES), lambda j, i: (j, i, 0))] + x_in,
        out_specs=[blk, blk, full, full,
                   pl.BlockSpec((1, 1, 2, tb), lambda j, i: (j, i, 0, 0)),
                   pl.BlockSpec((1, 2, nb, tb), lambda j, i: (j, 0, 0, 0))] + x_out,
        out_shape=[jax.ShapeDtypeStruct((T, D), BF16)] * 4
        + [jax.ShapeDtypeStruct((N_PAIRS, nb, 2, tb), F32),
           jax.ShapeDtypeStruct((N_PAIRS, 2, nb, tb), F32)] + x_shapes,
        scratch_shapes=[pltpu.VMEM((T, LANES), BF16), pltpu.VMEM((T, LANES), BF16),
                        pltpu.VMEM((T, LANES), F32), pltpu.VMEM((T, LANES), F32),
                        pltpu.VMEM((2, nb, tb), F32)] + x_scratch,
        compiler_params=_cparams("arbitrary", "arbitrary"),
    )(proj, proj, proj, proj, cum4, o, dog, lp, *x_args)


def _cumsum_fwd(proj, bf_row, *, tt, name):
    T = proj.shape[0]
    cb = (proj.shape[1] - LANES) // LANES

    def body(f_ref, b_ref, out_ref, carry_ref):
        i = pl.program_id(0)

        @pl.when(i == 0)
        def _():
            carry_ref[...] = jnp.zeros_like(carry_ref)

        ls = -_softplus(-(f_ref[...] + b_ref[...]))
        tri = (lax.broadcasted_iota(jnp.int32, (tt, tt), 0)
               >= lax.broadcasted_iota(jnp.int32, (tt, tt), 1)).astype(F32)
        cum = jnp.dot(tri, ls, preferred_element_type=F32,
                      precision=lax.Precision.HIGHEST) + carry_ref[...]
        carry_ref[...] = cum[tt - 1:tt, :]
        out_ref[...] = cum.T

    return pl.pallas_call(
        body, name=name, grid=(T // tt,),
        in_specs=[pl.BlockSpec((tt, LANES), lambda i: (i, cb)),
                  pl.BlockSpec((1, LANES), lambda i: (0, 0))],
        out_specs=pl.BlockSpec((LANES, tt), lambda i: (0, i)),
        out_shape=jax.ShapeDtypeStruct((LANES, T), F32),
        scratch_shapes=[pltpu.VMEM((1, LANES), F32)],
        compiler_params=_cparams("arbitrary"),
    )(proj, bf_row)


def _cumsum_bwd(dcum_t, proj, bf_row, *, tt, name):
    T = proj.shape[0]
    cb = (proj.shape[1] - LANES) // LANES
    nt = T // tt

    def body(dc_ref, f_ref, b_ref, df_ref, db_ref, carry_ref):
        i = pl.program_id(0)

        @pl.when(i == 0)
        def _():
            carry_ref[...] = jnp.zeros_like(carry_ref)
            db_ref[...] = jnp.zeros_like(db_ref)

        dc = dc_ref[...].T
        tri = (lax.broadcasted_iota(jnp.int32, (tt, tt), 0)
               <= lax.broadcasted_iota(jnp.int32, (tt, tt), 1)).astype(F32)
        rev = jnp.dot(tri, dc, preferred_element_type=F32,
                      precision=lax.Precision.HIGHEST) + carry_ref[...]
        carry_ref[...] = rev[0:1, :]
        df = rev * _sigmoid(-(f_ref[...] + b_ref[...]))
        df_ref[...] = df.astype(BF16)
        db_ref[...] += jnp.sum(df, axis=0, keepdims=True)

    return pl.pallas_call(
        body, name=name, grid=(nt,),
        in_specs=[pl.BlockSpec((LANES, tt), lambda i: (0, nt - 1 - i)),
                  pl.BlockSpec((tt, LANES), lambda i: (nt - 1 - i, cb)),
                  pl.BlockSpec((1, LANES), lambda i: (0, 0))],
        out_specs=[pl.BlockSpec((tt, LANES), lambda i: (nt - 1 - i, 0)),
                   pl.BlockSpec((1, LANES), lambda i: (0, 0))],
        out_shape=[jax.ShapeDtypeStruct((T, LANES), BF16), jax.ShapeDtypeStruct((1, LANES), F32)],
        scratch_shapes=[pltpu.VMEM((1, LANES), F32)],
        compiler_params=_cparams("arbitrary"),
    )(dcum_t, proj, bf_row)


def _rows_down(x, before, sh):
    if sh == 0:
        return x
    rolled = pltpu.roll(x, sh, axis=0)
    row = lax.broadcasted_iota(jnp.int32, (SUBLANES, x.shape[1]), 0)
    head = jnp.where(row < sh, pltpu.roll(before, sh, axis=0), rolled[:SUBLANES])
    return jnp.concatenate([head, rolled[SUBLANES:]], axis=0)


def _rows_up(x, after, sh):
    if sh == 0:
        return x
    tt = x.shape[0]
    rolled = pltpu.roll(x, tt - sh, axis=0)
    row = lax.broadcasted_iota(jnp.int32, (SUBLANES, x.shape[1]), 0)
    tail = jnp.where(row >= SUBLANES - sh, pltpu.roll(after, SUBLANES - sh, axis=0),
                     rolled[tt - SUBLANES:])
    return jnp.concatenate([rolled[:tt - SUBLANES], tail], axis=0)


def _rg_gates(u0, before, small_ref, wa_ref, wi_ref):
    taps = [_rows_down(u0, before, CONV_WIDTH - 1 - tap) for tap in range(CONV_WIDTH)]
    u = small_ref[4:5, :]
    for tap in range(CONV_WIDTH):
        u = u + taps[tap] * small_ref[tap:tap + 1, :]
    pa, pi = [], []
    for n in range(RNN_BLOCKS):
        ub = u[:, n * RNN_BLOCK_WIDTH:(n + 1) * RNN_BLOCK_WIDTH].astype(BF16)
        pa.append(jnp.dot(ub, wa_ref[n], preferred_element_type=F32))
        pi.append(jnp.dot(ub, wi_ref[n], preferred_element_type=F32))
    r = _sigmoid(jnp.concatenate(pa, axis=-1) + small_ref[5:6, :])
    ig = _sigmoid(jnp.concatenate(pi, axis=-1) + small_ref[6:7, :])
    spl = _softplus(-small_ref[7:8, :])
    log_a = (-LRU_C) * r * spl
    a = jnp.exp(log_a)
    s2 = jnp.tanh(-log_a) * (a * a + 1.0)
    inv_s = lax.rsqrt(s2)
    s = jnp.where(s2 > 0.0, s2 * inv_s, 0.0)
    return u, taps, r, ig, spl, a, s, inv_s


def _rg_fwd(proj, small, wa, wi, *, tt, name):
    T = proj.shape[0]
    D = RNN_BLOCKS * RNN_BLOCK_WIDTH
    hb = tt // SUBLANES

    def body(u0_ref, halo_ref, g_ref, small_ref, wa_ref, wi_ref, h_ref, y_ref,
             a_ref, b_ref, carry_ref):
        i = pl.program_id(0)

        @pl.when(i == 0)
        def _():
            carry_ref[...] = jnp.zeros_like(carry_ref)

        before = jnp.where(i == 0, 0.0, halo_ref[...])
        u, _, r, ig, spl, a, s, _ = _rg_gates(u0_ref[...], before, small_ref, wa_ref, wi_ref)
        a_ref[...] = a
        b_ref[...] = s * (ig * u)

        def row(t, h):
            h = a_ref[pl.ds(t, 1), :] * h + b_ref[pl.ds(t, 1), :]
            h_ref[pl.ds(t, 1), :] = h
            return h

        carry_ref[...] = lax.fori_loop(0, tt, row, carry_ref[...])
        gate = g_ref[...]
        y_ref[...] = (h_ref[...] * (gate * _sigmoid(gate))).astype(BF16)

    return pl.pallas_call(
        body, name=name, grid=(T // tt,),
        in_specs=[pl.BlockSpec((tt, D), lambda i: (i, 0)),
                  pl.BlockSpec((SUBLANES, D), lambda i: (jnp.maximum(i * hb - 1, 0), 0)),
                  pl.BlockSpec((tt, D), lambda i: (i, 1)),
                  pl.BlockSpec((SUBLANES, D), lambda i: (0, 0)),
                  pl.BlockSpec((RNN_BLOCKS, RNN_BLOCK_WIDTH, RNN_BLOCK_WIDTH), lambda i: (0, 0, 0)),
                  pl.BlockSpec((RNN_BLOCKS, RNN_BLOCK_WIDTH, RNN_BLOCK_WIDTH), lambda i: (0, 0, 0))],
        out_specs=[pl.BlockSpec((tt, D), lambda i: (i, 0)), pl.BlockSpec((tt, D), lambda i: (i, 0))],
        out_shape=[jax.ShapeDtypeStruct((T, D), F32), jax.ShapeDtypeStruct((T, D), BF16)],
        scratch_shapes=[pltpu.VMEM((tt, D), F32), pltpu.VMEM((tt, D), F32),
                        pltpu.VMEM((1, D), F32)],
        compiler_params=_cparams("arbitrary"),
    )(proj, proj, proj, small, wa, wi)


def _rg_bwd(proj, hs, dy, small, wa, wi, *, tt, name):
    T = proj.shape[0]
    D = RNN_BLOCKS * RNN_BLOCK_WIDTH
    W = RNN_BLOCK_WIDTH
    hb = tt // SUBLANES
    nt = T // tt

    def body(u0_ref, uhalo_ref, g_ref, h_ref, hhalo_ref, dy_ref, small_ref, wa_ref, wi_ref,
             dp_ref, dwa_ref, dwi_ref, ds_ref,
             a_ref, g_s_ref, dunext_ref, carry_ref):
        i = pl.program_id(0)
        first_chunk = i == nt - 1

        @pl.when(i == 0)
        def _():
            carry_ref[...] = jnp.zeros_like(carry_ref)
            dunext_ref[...] = jnp.zeros_like(dunext_ref)
            dwa_ref[...] = jnp.zeros_like(dwa_ref)
            dwi_ref[...] = jnp.zeros_like(dwi_ref)
            ds_ref[...] = jnp.zeros_like(ds_ref)

        u_before = jnp.where(first_chunk, 0.0, uhalo_ref[...])
        h_before = jnp.where(first_chunk, 0.0, hhalo_ref[...])
        u, taps, r, ig, spl, a, s, inv_s = _rg_gates(u0_ref[...], u_before, small_ref, wa_ref,
                                                     wi_ref)
        gate = g_ref[...]
        sg = _sigmoid(gate)
        dy = dy_ref[...]
        dp_ref[:, D:] = (dy * h_ref[...] * (sg * (1.0 + gate * (1.0 - sg)))).astype(BF16)
        a_ref[...] = a
        g_s_ref[...] = dy * (gate * sg)

        def row(k, c):
            t = tt - 1 - k
            g = g_s_ref[pl.ds(t, 1), :] + c
            g_s_ref[pl.ds(t, 1), :] = g
            return a_ref[pl.ds(t, 1), :] * g

        carry_ref[...] = lax.fori_loop(0, tt, row, carry_ref[...])
        g = g_s_ref[...]
        h_prev = _rows_down(h_ref[...], h_before, 1)
        iu = ig * u
        d_iu = g * s
        dlog_a = (g * h_prev) * a - (g * iu) * (a * a) * inv_s
        dpre_a = (dlog_a * ((-LRU_C) * spl)) * r * (1.0 - r)
        dpre_i = (d_iu * u) * ig * (1.0 - ig)
        dlam = jnp.sum(dlog_a * r, axis=0, keepdims=True) * (LRU_C * _sigmoid(-small_ref[7:8, :]))
        du_parts = []
        for n in range(RNN_BLOCKS):
            sl = slice(n * W, (n + 1) * W)
            ub = u[:, sl].astype(BF16)
            da_n = dpre_a[:, sl].astype(BF16)
            di_n = dpre_i[:, sl].astype(BF16)
            dwa_ref[n] += lax.dot_general(ub, da_n, (((0,), (0,)), ((), ())),
                                          preferred_element_type=F32)
            dwi_ref[n] += lax.dot_general(ub, di_n, (((0,), (0,)), ((), ())),
                                          preferred_element_type=F32)
            du_parts.append(
                lax.dot_general(da_n, wa_ref[n], (((1,), (1,)), ((), ())), preferred_element_type=F32)
                + lax.dot_general(di_n, wi_ref[n], (((1,), (1,)), ((), ())), preferred_element_type=F32))
        du = d_iu * ig + jnp.concatenate(du_parts, axis=-1)
        for tap in range(CONV_WIDTH):
            ds_ref[tap:tap + 1, :] += jnp.sum(du * taps[tap], axis=0, keepdims=True)
        ds_ref[4:5, :] += jnp.sum(du, axis=0, keepdims=True)
        ds_ref[5:6, :] += jnp.sum(dpre_a, axis=0, keepdims=True)
        ds_ref[6:7, :] += jnp.sum(dpre_i, axis=0, keepdims=True)
        ds_ref[7:8, :] += dlam
        du_after = dunext_ref[...]
        du0 = jnp.zeros((tt, D), F32)
        for tap in range(CONV_WIDTH):
            du0 = du0 + _rows_up(du, du_after, CONV_WIDTH - 1 - tap) * small_ref[tap:tap + 1, :]
        dp_ref[:, :D] = du0.astype(BF16)
        dunext_ref[...] = du[0:SUBLANES, :]

    rev = lambda i: nt - 1 - i
    wspec = pl.BlockSpec((RNN_BLOCKS, W, W), lambda i: (0, 0, 0))
    return pl.pallas_call(
        body, name=name, grid=(nt,),
        in_specs=[pl.BlockSpec((tt, D), lambda i: (rev(i), 0)),
                  pl.BlockSpec((SUBLANES, D), lambda i: (jnp.maximum(rev(i) * hb - 1, 0), 0)),
                  pl.BlockSpec((tt, D), lambda i: (rev(i), 1)),
                  pl.BlockSpec((tt, D), lambda i: (rev(i), 0)),
                  pl.BlockSpec((SUBLANES, D), lambda i: (jnp.maximum(rev(i) * hb - 1, 0), 0)),
                  pl.BlockSpec((tt, D), lambda i: (rev(i), 0)),
                  pl.BlockSpec((SUBLANES, D), lambda i: (0, 0)),
                  wspec, wspec],
        out_specs=[pl.BlockSpec((tt, 2 * D), lambda i: (rev(i), 0)),
                   wspec, wspec, pl.BlockSpec((SUBLANES, D), lambda i: (0, 0))],
        out_shape=[jax.ShapeDtypeStruct((T, 2 * D), BF16),
                   jax.ShapeDtypeStruct((RNN_BLOCKS, W, W), F32),
                   jax.ShapeDtypeStruct((RNN_BLOCKS, W, W), F32),
                   jax.ShapeDtypeStruct((SUBLANES, D), F32)],
        scratch_shapes=[pltpu.VMEM((tt, D), F32), pltpu.VMEM((tt, D), F32),
                        pltpu.VMEM((SUBLANES, D), F32), pltpu.VMEM((1, D), F32)],
        compiler_params=_cparams("arbitrary"),
    )(proj, proj, proj, hs, hs, dy, small, wa, wi)


def _out_ln(a, w, x, g, b, *, tt, name):
    T, D = x.shape
    K = a.shape[1]

    def body(a_ref, w_ref, x_ref, g_ref, b_ref, y_ref, yb_ref, zh_ref, rs_ref):
        h = jnp.dot(a_ref[...].astype(BF16), w_ref[...].astype(BF16), preferred_element_type=F32)
        z = ALPHA * x_ref[...] + h
        mu = jnp.mean(z, axis=-1, keepdims=True)
        zc = z - mu
        rstd = lax.rsqrt(jnp.mean(zc * zc, axis=-1, keepdims=True) + LN_EPS)
        zh = zc * rstd
        zh_ref[...] = zh
        rs_ref[...] = rstd
        y = zh * g_ref[...] + b_ref[...]
        y_ref[...] = y
        yb_ref[...] = y.astype(BF16)

    blk = pl.BlockSpec((tt, D), lambda i: (i, 0))
    row = pl.BlockSpec((1, D), lambda i: (0, 0))
    return pl.pallas_call(
        body, name=name, grid=(T // tt,),
        in_specs=[pl.BlockSpec((tt, K), lambda i: (i, 0)), pl.BlockSpec((K, D), lambda i: (0, 0)),
                  blk, row, row],
        out_specs=[blk, blk, blk, pl.BlockSpec((tt, 1), lambda i: (i, 0))],
        out_shape=[jax.ShapeDtypeStruct((T, D), F32), jax.ShapeDtypeStruct((T, D), BF16),
                   jax.ShapeDtypeStruct((T, D), F32), jax.ShapeDtypeStruct((T, 1), F32)],
        compiler_params=_cparams("parallel"),
    )(a, w, x, g, b)


def _ln_bwd_tile(dy, zh_ref, rs_ref, g_ref, dz_ref, dzb_ref, dg_ref, db_ref, first):
    @pl.when(first)
    def _():
        dg_ref[...] = jnp.zeros_like(dg_ref)
        db_ref[...] = jnp.zeros_like(db_ref)

    zh = zh_ref[...]
    dg_ref[...] += jnp.sum(dy * zh, axis=0, keepdims=True)
    db_ref[...] += jnp.sum(dy, axis=0, keepdims=True)
    dzh = dy * g_ref[...]
    m1 = jnp.mean(dzh, axis=-1, keepdims=True)
    m2 = jnp.mean(dzh * zh, axis=-1, keepdims=True)
    dz = rs_ref[...] * (dzh - m1 - zh * m2)
    dz_ref[...] = dz
    dzb_ref[...] = dz.astype(BF16)


def _ln_bwd_specs(T, D, tt):
    blk = pl.BlockSpec((tt, D), lambda i: (i, 0))
    row = pl.BlockSpec((1, D), lambda i: (0, 0))
    return ([blk, pl.BlockSpec((tt, 1), lambda i: (i, 0)), row], [blk, blk, row, row],
            [jax.ShapeDtypeStruct((T, D), F32), jax.ShapeDtypeStruct((T, D), BF16),
             jax.ShapeDtypeStruct((1, D), F32), jax.ShapeDtypeStruct((1, D), F32)])


def _loss_ln_bwd(y, tgt, zh, rstd, g, *, tt, name):
    T, D = y.shape
    ln_in, ln_out, ln_shapes = _ln_bwd_specs(T, D, tt)

    def body(y_ref, t_ref, zh_ref, rs_ref, g_ref, l_ref, dz_ref, dzb_ref, dg_ref, db_ref):
        first = pl.program_id(0) == 0

        @pl.when(first)
        def _():
            l_ref[...] = jnp.zeros_like(l_ref)

        e = y_ref[...] - t_ref[...]
        l_ref[...] += jnp.sum(e * e, axis=0, keepdims=True) * (0.5 / D)
        _ln_bwd_tile(e * (1.0 / D), zh_ref, rs_ref, g_ref, dz_ref, dzb_ref, dg_ref, db_ref, first)

    blk = pl.BlockSpec((tt, D), lambda i: (i, 0))
    return pl.pallas_call(
        body, name=name, grid=(T // tt,),
        in_specs=[blk, blk] + ln_in,
        out_specs=[pl.BlockSpec((1, D), lambda i: (0, 0))] + ln_out,
        out_shape=[jax.ShapeDtypeStruct((1, D), F32)] + ln_shapes,
        compiler_params=_cparams("arbitrary"),
    )(y, tgt, zh, rstd, g)


def _dx_ln_bwd(a, b, add, zh, rstd, g, *, tm, name):
    T, D = add.shape
    na = len(a)
    K = sum(p.shape[1] for p in a)
    ln_in, ln_out, ln_shapes = _ln_bwd_specs(T, D, tm)

    def body(*refs):
        a_refs, b_ref, add_ref = refs[:na], refs[na], refs[na + 1]
        av = [r[...].astype(BF16) for r in a_refs]
        av = av[0] if na == 1 else jnp.concatenate(av, axis=1)
        dy = lax.dot_general(av, b_ref[...].astype(BF16), (((1,), (1,)), ((), ())),
                             preferred_element_type=F32) + ALPHA * add_ref[...]
        _ln_bwd_tile(dy, *refs[na + 2:], pl.program_id(0) == 0)

    return pl.pallas_call(
        body, name=name, grid=(T // tm,),
        in_specs=[pl.BlockSpec((tm, p.shape[1]), lambda i: (i, 0)) for p in a]
        + [pl.BlockSpec((D, K), lambda i: (0, 0)), pl.BlockSpec((tm, D), lambda i: (i, 0))] + ln_in,
        out_specs=ln_out, out_shape=ln_shapes,
        compiler_params=_cparams("arbitrary"),
    )(*a, b, add, zh, rstd, g)


def _row_tile(rows, target):
    best = SUBLANES
    for t in range(SUBLANES, target + 1, SUBLANES):
        if rows % t == 0:
            best = t
    return best


def _add_own(g, recv, c_idx, *, tr, name):
    _, M, R, C = g.shape

    def body(c_ref, g_ref, r_ref, o_ref, ob_ref):
        s = g_ref[0] + r_ref[...]
        o_ref[...] = s
        ob_ref[...] = s.astype(BF16)

    blk = pl.BlockSpec((1, tr, C), lambda k, i, c: (k, i, 0))
    return pl.pallas_call(
        body, name=name,
        grid_spec=pltpu.PrefetchScalarGridSpec(
            num_scalar_prefetch=1, grid=(M, R // tr),
            in_specs=[pl.BlockSpec((1, 1, tr, C), lambda k, i, c: (c[0], k, i, 0)), blk],
            out_specs=[blk, blk]),
        out_shape=[jax.ShapeDtypeStruct((M, R, C), F32), jax.ShapeDtypeStruct((M, R, C), BF16)],
        compiler_params=_cparams("parallel", "parallel"),
    )(c_idx, g, recv)


def _adamw_math(g, w_ref, m_ref, v_ref, g_ref, d_ref, nm_ref, nv_ref):
    nm = ADAM_B1 * m_ref[...] + (1.0 - ADAM_B1) * g
    nv = ADAM_B2 * v_ref[...] + (1.0 - ADAM_B2) * (g * g)
    m_hat = nm / (1.0 - ADAM_B1 ** ADAM_STEP)
    v_hat = nv / (1.0 - ADAM_B2 ** ADAM_STEP)
    g_ref[...] = g
    nm_ref[...] = nm
    nv_ref[...] = nv
    d_ref[...] = (-ADAM_LR) * (m_hat / (jnp.sqrt(v_hat) + ADAM_EPS) + ADAM_WD * w_ref[...])


def _adamw(parts, w, m, v, *, tr, name):
    n, R, C = parts.shape
    tr = min(tr, R)

    def body(p_ref, w_ref, m_ref, v_ref, *out_refs):
        g = p_ref[0]
        for k in range(1, n):
            g = g + p_ref[k]
        _adamw_math(g, w_ref, m_ref, v_ref, *out_refs)

    blk = pl.BlockSpec((tr, C), lambda i: (i, 0))
    out = jax.ShapeDtypeStruct((R, C), F32)
    return pl.pallas_call(
        body, name=name, grid=(R // tr,),
        in_specs=[pl.BlockSpec((n, tr, C), lambda i: (0, i, 0)), blk, blk, blk],
        out_specs=[blk, blk, blk, blk], out_shape=[out, out, out, out],
        compiler_params=_cparams("parallel"),
    )(parts, w, m, v)


def _adamw_shard(parts_by_layer, place, w, m, v, *, tr, name):
    L, R, C = w.shape
    flat = [(l, a, pick) for l, parts in enumerate(parts_by_layer) for a, pick in parts]
    n = len(flat)

    def body(place_ref, *refs):
        w_ref, m_ref, v_ref = refs[n:n + 3]
        for layer in range(L):
            @pl.when(pl.program_id(0) == layer)
            def _(layer=layer):
                g = None
                for (l, _, _), r in zip(flat, refs[:n]):
                    if l == layer:
                        blk = r[(0,) * (len(r.shape) - 3)].astype(F32)
                        g = blk if g is None else g + blk
                _adamw_math(g, w_ref, m_ref, v_ref, *refs[n + 3:])

    blk = pl.BlockSpec((1, tr, C), lambda ly, i, s: (ly, i, 0))

    def part_spec(l, a, pick):
        return pl.BlockSpec((1,) * (a.ndim - 2) + (tr, C),
                            lambda ly, i, s: (*pick(s), jnp.where(ly == l, i, 0), 0))

    out = jax.ShapeDtypeStruct(w.shape, F32)
    return pl.pallas_call(
        body, name=name,
        grid_spec=pltpu.PrefetchScalarGridSpec(
            num_scalar_prefetch=1, grid=(L, R // tr),
            in_specs=[part_spec(*f) for f in flat] + [blk, blk, blk],
            out_specs=[blk, blk, blk, blk]),
        out_shape=[out, out, out, out],
        compiler_params=_cparams("arbitrary", "arbitrary"),
    )(place, *[a for _, a, _ in flat], w, m, v)


def _two_stage_parts(h, recv):
    return [(h, lambda s: (s[0], 0))] + [(recv, lambda s, d=d: (s[0] ^ d, 0)) for d in (1, 2, 3)]


def _direct_parts(g, recv):
    return [(g, lambda s: (s[1], s[0]))] + [
        (recv, lambda s, a=p // 4, d=p % 4: (s[1] ^ a, s[0] ^ d)) for p in range(1, 8)]


SHARD_AXIS = dict(attn_w_in=1, attn_w_out=0, rnn_w_in=1, rnn_w_out=0, rnn_w_a=1, rnn_w_i=1,
                  rnn_conv_w=1, rnn_conv_b=0, rnn_b_a=0, rnn_b_i=0, rnn_lambda=0)
RNN_ROWED = ("rnn_w_out", "rnn_w_a", "rnn_w_i")
SMALL = ("rnn_conv_w", "rnn_conv_b", "rnn_b_a", "rnn_b_i", "rnn_lambda")
PACK_C = 1024


def _elems(shape):
    n = 1
    for s in shape:
        n *= s
    return n


def _pack_rows(p, idx, dtype):
    parts = [p[k][idx].astype(dtype).reshape(-1, PACK_C) for k in RNN_ROWED]
    small = jnp.concatenate([p[k][idx].reshape(-1) for k in SMALL])
    tile_rows = SUBLANES * (4 // jnp.dtype(dtype).itemsize)
    if dtype == BF16:
        small = lax.bitcast_convert_type(small, BF16)
    small = small.reshape(-1, PACK_C)
    parts.append(jnp.pad(small, ((0, tile_rows - small.shape[0]), (0, 0))))
    return jnp.concatenate(parts, axis=0)


def _unpack_rows(flat, shapes):
    lead = flat.shape[:-2]
    out, r = {}, 0
    for k in RNN_ROWED:
        n = _elems(shapes[k]) // PACK_C
        out[k] = flat[..., r:r + n, :].reshape(lead + shapes[k])
        r += n
    n_small = sum(_elems(shapes[k]) for k in SMALL)
    small = flat[..., r:r + n_small // PACK_C, :].reshape(lead + (-1,))
    o = 0
    for k in SMALL:
        n = _elems(shapes[k])
        out[k] = small[..., o:o + n].reshape(lead + shapes[k])
        o += n
    return out


def _join_columns(g, width, *, tr, name):
    _, _, R, S = g.shape

    def body(*refs):
        o_ref = refs[8]
        parts = [refs[r][0, 0].astype(F32) for r in range(8)]
        if width > 8 * S:
            parts.append(jnp.zeros((tr, width - 8 * S), F32))
        o_ref[...] = jnp.concatenate(parts, axis=-1).astype(o_ref.dtype)

    def shard(r):
        return pl.BlockSpec((1, 1, tr, S), lambda i: (r % 2, r // 2, i, 0))

    return pl.pallas_call(
        body, name=name, grid=(R // tr,),
        in_specs=[shard(r) for r in range(8)],
        out_specs=pl.BlockSpec((tr, width), lambda i: (i, 0)),
        out_shape=jax.ShapeDtypeStruct((R, width), g.dtype),
        compiler_params=_cparams("parallel"),
    )(*([g] * 8))


def _split_columns(parts, S, *, tr, name):
    R = parts[0].shape[0]
    n = len(parts)

    def body(*refs):
        o_ref = refs[n]
        x = jnp.concatenate([r[...] for r in refs[:n]], axis=1)
        for r in range(8):
            o_ref[r % 2, r // 2] = x[:, r * S:(r + 1) * S]

    return pl.pallas_call(
        body, name=name, grid=(R // tr,),
        in_specs=[pl.BlockSpec((tr, p.shape[1]), lambda i: (i, 0)) for p in parts],
        out_specs=pl.BlockSpec((2, 4, tr, S), lambda i: (0, 0, i, 0)),
        out_shape=jax.ShapeDtypeStruct((2, 4, R, S), parts[0].dtype),
        compiler_params=_cparams("parallel"),
    )(*parts)


def _to_full(g, k, sh):
    ax, nd = SHARD_AXIS[k], len(sh)
    perm = tuple(range(2, 2 + ax)) + (1, 0) + tuple(range(2 + ax, 2 + nd))
    return g.transpose(perm).reshape(sh[:ax] + (8 * sh[ax],) + sh[ax + 1:])


def _from_full(full, k, sh):
    ax, nd = SHARD_AXIS[k], len(sh)
    t = full.reshape(sh[:ax] + (4, 2, sh[ax]) + sh[ax + 1:])
    return t.transpose((ax + 1, ax) + tuple(range(ax)) + tuple(range(ax + 2, nd + 2)))


def _unpack_gathered_rows(g, shapes):
    out, r = {}, 0
    for k in RNN_ROWED:
        n = _elems(shapes[k]) // PACK_C
        out[k] = _to_full(g[:, :, r:r + n].reshape((2, 4) + shapes[k]), k, shapes[k])
        r += n
    n_small = sum(_elems(shapes[k]) for k in SMALL)
    nr = 2 * n_small // PACK_C
    small = lax.bitcast_convert_type(g[:, :, r:r + nr].reshape(2, 4, n_small, 2), F32)
    o = 0
    for k in SMALL:
        n = _elems(shapes[k])
        out[k] = _to_full(small[:, :, o:o + n].reshape((2, 4) + shapes[k]), k, shapes[k])
        o += n
    return out


def _pack_grad_rows(full, shapes):
    parts = [_from_full(full[k], k, shapes[k]).reshape(2, 4, -1, PACK_C) for k in RNN_ROWED]
    small = jnp.concatenate(
        [_from_full(full[k], k, shapes[k]).reshape(2, 4, -1) for k in SMALL], axis=-1)
    small = small.reshape(2, 4, -1, PACK_C)
    parts.append(jnp.pad(small, ((0, 0), (0, 0), (0, SUBLANES - small.shape[2]), (0, 0))))
    return jnp.concatenate(parts, axis=2)


def kernel(x, ln_g, ln_b, attn_w_in, attn_b_f, attn_w_out, rnn_w_in, rnn_conv_w, rnn_conv_b, rnn_w_a, rnn_b_a, rnn_w_i, rnn_b_i, rnn_lambda, rnn_w_out, loss_target, m_ln_g, m_ln_b, m_attn_w_in, m_attn_b_f, m_attn_w_out, m_rnn_w_in, m_rnn_conv_w, m_rnn_conv_b, m_rnn_w_a, m_rnn_b_a, m_rnn_w_i, m_rnn_b_i, m_rnn_lambda, m_rnn_w_out, v_ln_g, v_ln_b, v_attn_w_in, v_attn_b_f, v_attn_w_out, v_rnn_w_in, v_rnn_conv_w, v_rnn_conv_b, v_rnn_w_a, v_rnn_b_a, v_rnn_w_i, v_rnn_b_i, v_rnn_lambda, v_rnn_w_out):
    w_loc = dict(attn_w_in=attn_w_in, attn_w_out=attn_w_out, rnn_w_in=rnn_w_in, rnn_w_a=rnn_w_a,
                 rnn_w_i=rnn_w_i, rnn_w_out=rnn_w_out, rnn_conv_w=rnn_conv_w, rnn_conv_b=rnn_conv_b,
                 rnn_b_a=rnn_b_a, rnn_b_i=rnn_b_i, rnn_lambda=rnn_lambda)
    m_loc = dict(attn_w_in=m_attn_w_in, attn_w_out=m_attn_w_out, rnn_w_in=m_rnn_w_in,
                 rnn_w_a=m_rnn_w_a, rnn_w_i=m_rnn_w_i, rnn_w_out=m_rnn_w_out,
                 rnn_conv_w=m_rnn_conv_w, rnn_conv_b=m_rnn_conv_b, rnn_b_a=m_rnn_b_a,
                 rnn_b_i=m_rnn_b_i, rnn_lambda=m_rnn_lambda)
    v_loc = dict(attn_w_in=v_attn_w_in, attn_w_out=v_attn_w_out, rnn_w_in=v_rnn_w_in,
                 rnn_w_a=v_rnn_w_a, rnn_w_i=v_rnn_w_i, rnn_w_out=v_rnn_w_out,
                 rnn_conv_w=v_rnn_conv_w, rnn_conv_b=v_rnn_conv_b, rnn_b_a=v_rnn_b_a,
                 rnn_b_i=v_rnn_b_i, rnn_lambda=v_rnn_lambda)
    shapes = {k: tuple(a.shape[1:]) for k, a in w_loc.items()}
    T, D = x.shape[1], x.shape[2]
    n_f = attn_b_f.shape[1]
    tb = min(1024, T)
    tb_bwd = min(512, T)
    tt_rg = min(128, T)
    tt_ln = min(256, T)
    c_idx = lax.axis_index("c").astype(jnp.int32).reshape(1)
    me_idx = (2 * lax.axis_index("x") + lax.axis_index("y")).astype(jnp.int32).reshape(1)
    place = jnp.concatenate([me_idx, c_idx])

    def attn_w_in_full(g_in, idx):
        return _join_columns(g_in, 4 * D + LANES, tr=256, name=f"a_join{idx}")

    def attn_w_out_full(g_out):
        return _to_full(g_out, "attn_w_out", shapes["attn_w_out"])

    def rnn_weights(g_in, g_rows, idx):
        w = _unpack_gathered_rows(g_rows, shapes)
        w["rnn_w_in"] = _join_columns(g_in, 2 * D, tr=256, name=f"r_join{idx}")
        w["small"] = jnp.concatenate([w["rnn_conv_w"], w["rnn_conv_b"][None], w["rnn_b_a"][None],
                                      w["rnn_b_i"][None], w["rnn_lambda"][None]])
        return w

    g0 = _ag_c(_run_exchange(_Exchange("gather", [attn_w_in[0].astype(BF16)]), "ag_w0_xy"),
               "ag_w0_c")
    later = _Exchange("gather8", [attn_w_in[1].astype(BF16)] + [
        a for i in range(2) for a in (attn_w_out[i].astype(BF16), rnn_w_in[i].astype(BF16),
                                      _pack_rows(w_loc, i, BF16))])
    w_attn_in, w_attn_out, w_rnn = [attn_w_in_full(g0[0], 0), None], [None, None], [None, None]
    bf_rows = jnp.pad(attn_b_f, ((0, 0), (0, LANES - n_f)))[:, None, :]

    xs, xb, saved = [x[0]], [x[0]], []
    for layer in range(DEPTH):
        idx, xl, xm = layer // 2, xs[-1], xb[-1]
        if layer % 2 == 0:
            proj = _matmul(xm, w_attn_in[idx], trans_b=False, tm=512, tn=1408,
                           name=f"a_proj{layer}")
            cum_t = _cumsum_fwd(proj, bf_rows[idx], tt=min(512, T), name=f"a_cum{layer}")
            cum2 = cum_t[:N_HEADS].reshape(N_PAIRS, 2, T)
            o, og, lp, *got = _flash_fwd(proj, cum2.reshape(N_PAIRS, 2, T // tb, tb), tb=tb,
                                         name=f"a_fwd{layer}", host=later if layer == 0 else None)
            cum4 = cum2.reshape(N_PAIRS, 2, T // tb_bwd, tb_bwd)
            if layer == 0:
                w_attn_in[1] = attn_w_in_full(got[0], 1)
                w_attn_out = [attn_w_out_full(got[1 + 3 * i]) for i in range(2)]
                w_rnn = [rnn_weights(got[2 + 3 * i], got[3 + 3 * i], i) for i in range(2)]
            branch, w_out = og, w_attn_out[idx]
            saved.append((proj, cum4, o, og, lp))
        else:
            w = w_rnn[idx]
            proj = _matmul(xm, w["rnn_w_in"], trans_b=False, tm=512, tn=1024,
                           name=f"r_proj{layer}")
            hs, yr = _rg_fwd(proj, w["small"], w["rnn_w_a"], w["rnn_w_i"], tt=tt_rg,
                             name=f"r_fwd{layer}")
            branch, w_out = yr, w["rnn_w_out"]
            saved.append((proj, hs, yr))
        y, yb, zh, rstd = _out_ln(branch, w_out, xl, ln_g[layer][None], ln_b[layer][None],
                                  tt=512, name=f"out_ln{layer}")
        saved[-1] = saved[-1] + (zh, rstd)
        xs.append(y)
        xb.append(yb)

    def ln_below(layer):
        return saved[layer][-2:] + (ln_g[layer][None],)

    loss_lanes, *ln_grads = _loss_ln_bwd(xs[-1], loss_target[0], *ln_below(DEPTH - 1), tt=tt_ln,
                                         name="loss_ln_bwd")
    loss = lax.psum(jnp.sum(loss_lanes), ("x", "y", "c"))

    def reduce_pair(gs, layer):
        recv = _rs_c(gs, f"rs_c{layer}")
        outs = [_add_own(g, r, c_idx, tr=_row_tile(g.shape[2], 512), name=f"rs_add{layer}_{n}")
                for n, (g, r) in enumerate(zip(gs, recv))]
        return [o[0][:, None] for o in outs], [o[1][:, None] for o in outs]

    part, got_parts = [None] * DEPTH, [None] * DEPTH
    d_ln_g, d_ln_b, d_bf = [None] * DEPTH, [None] * DEPTH, [None, None]
    for layer in reversed(range(DEPTH)):
        idx, xm = layer // 2, xb[layer]
        dz, dzb, dg, db = ln_grads
        d_ln_g[layer], d_ln_b[layer] = dg[0], db[0]
        if layer % 2 == 0:
            w_in, w_out = w_attn_in[idx], w_attn_out[idx]
            proj, cum4, o, og, lp = saved[layer][:5]
            dog = _matmul(dzb, w_out, trans_b=True, tm=512, tn=1024, name=f"a_dog{layer}")
            dwo = _matmul_tn(og, dzb, tm=512, tn=1024, tk=1024, name=f"a_dwo{layer}")
            g_out = _from_full(dwo, "attn_w_out", shapes["attn_w_out"])
            riders = [l for l in range(layer + 1, DEPTH) if got_parts[l] is None]
            early = [g_out] if layer == 0 else []
            host = _Exchange("scatter8", [g for l in riders for g in part[l]] + early)
            dq, dgate, dk, dv, dcum_q, dcum_k, *got = _flash_bwd(proj, cum4, o, dog, lp, tb=tb_bwd,
                                                                 name=f"a_bwd{layer}", host=host)
            for l in riders:
                got_parts[l], got = got[:len(part[l])], got[len(part[l]):]
            dcum_t = (dcum_q.transpose(0, 2, 1, 3) + dcum_k).reshape(N_HEADS, T)
            dcum_t = jnp.pad(dcum_t, ((0, LANES - N_HEADS), (0, 0)))
            df, dbf = _cumsum_bwd(dcum_t, proj, bf_rows[idx], tt=min(512, T), name=f"a_dcum{layer}")
            d_bf[idx] = dbf[0, :n_f]
            dproj = [dq, dk, dv, dgate, df]
            dwi = _matmul_tn_parts(xm, dproj, tm=512, tk=1024, name=f"a_dwi{layer}")
            g_in = _split_columns(dwi, shapes["attn_w_in"][1], tr=256, name=f"a_split{layer}")
            if layer > 0:
                part[layer] = [g_in, g_out]
                ln_grads = _dx_ln_bwd(dproj, w_in, dz, *ln_below(layer - 1), tm=256,
                                      name=f"a_dx{layer}")
            else:
                half, narrow = reduce_pair([g_in], layer)
                dy, recv = _matmul(dproj, w_in, trans_b=True, tm=512, tn=1024, name=f"a_dx{layer}",
                                   add=dz, add_scale=ALPHA, host=_Exchange("scatter", narrow))
                last_parts = [_two_stage_parts(half[0], recv), _direct_parts(g_out, got[0])]
        else:
            w = w_rnn[idx]
            proj, hs, yr = saved[layer][:3]
            dyr = _matmul(dzb, w["rnn_w_out"], trans_b=True, tm=512, tn=1024, name=f"r_dy{layer}")
            dwo = _matmul_tn(yr, dzb, tm=512, tn=1024, tk=1024, name=f"r_dwo{layer}")
            dproj, dwa, dwi_, dsm = _rg_bwd(proj, hs, dyr, w["small"], w["rnn_w_a"], w["rnn_w_i"],
                                            tt=tt_rg, name=f"r_bwd{layer}")
            dwin = _matmul_tn(xm, dproj, tm=512, tn=2048, tk=1024, name=f"r_dwi{layer}")
            ln_grads = _dx_ln_bwd([dproj], w["rnn_w_in"], dz, *ln_below(layer - 1), tm=512,
                                  name=f"r_dx{layer}")
            full = dict(rnn_w_out=dwo, rnn_w_a=dwa, rnn_w_i=dwi_, rnn_conv_w=dsm[0:4],
                        rnn_conv_b=dsm[4], rnn_b_a=dsm[5], rnn_b_i=dsm[6], rnn_lambda=dsm[7])
            part[layer] = [_split_columns([dwin], shapes["rnn_w_in"][1], tr=256,
                                          name=f"r_split{layer}"),
                           _pack_grad_rows(full, shapes)]
    grad_x = dy[None]

    def grad_parts(layer, n):
        return last_parts[n] if layer == 0 else _direct_parts(part[layer][n], got_parts[layer][n])

    def update(k, n, wmv):
        layers = [2 * idx + (0 if k.startswith("attn") else 1) for idx in range(2)]
        return _adamw_shard([grad_parts(layer, n) for layer in layers], place, *wmv,
                            tr=_row_tile(wmv[0].shape[1], 256), name=f"adamw_{k}")

    shard_outs = [dict() for _ in range(4)]
    for k, n in (("attn_w_in", 0), ("attn_w_out", 1), ("rnn_w_in", 0)):
        for j, a in enumerate(update(k, n, (w_loc[k], m_loc[k], v_loc[k]))):
            shard_outs[j][k] = a
    rows_wmv = [jnp.stack([_pack_rows(d, idx, F32) for idx in range(2)]) for d in (w_loc, m_loc, v_loc)]
    for j, a in enumerate(update("rnn_rows", 1, rows_wmv)):
        shard_outs[j].update(_unpack_rows(a, shapes))
    g_sh, d_sh, nm_sh, nv_sh = shard_outs

    def rep_pack(lg, lb, bf):
        rows = jnp.concatenate([lg, lb, jnp.pad(bf.reshape(1, -1), ((0, 0), (0, D - 2 * n_f)))])
        return jnp.pad(rows, ((0, 16 - rows.shape[0]), (0, 0)))

    rep = _all_gather(rep_pack(jnp.stack(d_ln_g), jnp.stack(d_ln_b), jnp.stack(d_bf)), "ag_rep")
    rg, rd, rm, rv = _adamw(rep.reshape(8, 16, D), rep_pack(ln_g, ln_b, attn_b_f),
                            rep_pack(m_ln_g, m_ln_b, m_attn_b_f),
                            rep_pack(v_ln_g, v_ln_b, v_attn_b_f), tr=16, name="adamw_rep")

    def rep_unpack(a):
        return dict(ln_g=a[0:DEPTH], ln_b=a[DEPTH:2 * DEPTH],
                    attn_b_f=a[2 * DEPTH, :2 * n_f].reshape(2, n_f))

    order = ("ln_g", "ln_b", "attn_w_in", "attn_b_f", "attn_w_out", "rnn_w_in", "rnn_conv_w",
             "rnn_conv_b", "rnn_w_a", "rnn_b_a", "rnn_w_i", "rnn_b_i", "rnn_lambda", "rnn_w_out")
    outs = [loss, grad_x]
    for sh, rp in ((g_sh, rg), (d_sh, rd), (nm_sh, rm), (nv_sh, rv)):
        allp = {**sh, **rep_unpack(rp)}
        outs.extend(allp[k] for k in order)
    return tuple(outs)
```

```python
import jax
import jax.numpy as jnp
from jax import lax
from jax.experimental import pallas as pl
from jax.experimental.pallas import tpu as pltpu

F32 = jnp.float32
BF16 = jnp.bfloat16

DEPTH = 4
N_HEADS = 16
HEAD_DIM = 64
N_PAIRS = N_HEADS // 2
RNN_BLOCKS = 4
RNN_BLOCK_WIDTH = 256
CONV_WIDTH = 4
LRU_C = 8.0
ALPHA = (2.0 * DEPTH) ** 0.25
LN_EPS = 1e-5
ADAM_LR, ADAM_B1, ADAM_B2, ADAM_EPS, ADAM_WD, ADAM_STEP = 0.001, 0.9, 0.999, 1e-8, 0.01, 10

LANES = 128
SUBLANES = 8
VMEM_LIMIT = 48 * 1024 * 1024

MESH = pl.DeviceIdType.MESH
HBM_SPEC = pl.BlockSpec(memory_space=pltpu.HBM)


def _cparams(*sem):
    return pltpu.CompilerParams(dimension_semantics=sem, vmem_limit_bytes=VMEM_LIMIT)


def _sigmoid(x):
    return 1.0 / (1.0 + jnp.exp(-x))


def _softplus(x):
    return jnp.maximum(x, 0.0) + jnp.log(1.0 + jnp.exp(-jnp.abs(x)))


D2D_CHUNKS = 16
ICI_CHUNKS = 8


def _row_chunks(rows, dtype, k):
    unit = SUBLANES * (4 // jnp.dtype(dtype).itemsize)
    assert rows % unit == 0
    units = rows // unit
    k = max(1, min(k, units))
    base, rem = divmod(units, k)
    out, r = [], 0
    for i in range(k):
        n = (base + (1 if i < rem else 0)) * unit
        out.append((r, n))
        r += n
    return out


def _chunks(shape, dtype, k):
    if len(shape) == 2:
        return [(pl.ds(r0, n),) for r0, n in _row_chunks(shape[0], dtype, k)]
    per = max(1, k // shape[0])
    return [(l, pl.ds(r0, n)) for l in range(shape[0]) for r0, n in _row_chunks(shape[1], dtype, per)]


def _mesh_place():
    x, y, c = lax.axis_index("x"), lax.axis_index("y"), lax.axis_index("c")
    return x, y, c, 2 * x + y


def _chip_peer(x, y, c, d):
    px, py = x ^ (d >> 1), y ^ (d & 1)
    return (px, py, c), 2 * px + py


def _remote(src, dst, send_sem, recv_sem, dev):
    return pltpu.make_async_remote_copy(src_ref=src, dst_ref=dst, send_sem=send_sem,
                                        recv_sem=recv_sem, device_id=dev, device_id_type=MESH)


def _comm_call(body, name, ins, out_shapes, n_sems, aliases=None):
    n = len(ins)
    return pl.pallas_call(
        body, name=name,
        out_shape=out_shapes, in_specs=[HBM_SPEC] * n, out_specs=[HBM_SPEC] * n,
        input_output_aliases=aliases or {},
        scratch_shapes=[pltpu.SemaphoreType.DMA((n_sems, n)), pltpu.SemaphoreType.DMA((n_sems, n))],
    )(*ins)


class _Exchange:
    def __init__(self, kind, arrays):
        self.kind, self.arrays, self.n = kind, list(arrays), len(arrays)
        self.is_gather, self.all8 = kind.startswith("gather"), kind.endswith("8")
        k = ICI_CHUNKS // 4 if self.all8 else ICI_CHUNKS
        if self.is_gather:
            self.chunks = [_chunks(a.shape, a.dtype, k) for a in arrays]
            self.out_shapes = [jax.ShapeDtypeStruct((2, 4) + tuple(a.shape), a.dtype) for a in arrays]
        else:
            lead = 2 if self.all8 else 1
            self.chunks = [_chunks(a.shape[lead:], a.dtype, k) for a in arrays]
            self.out_shapes = [jax.ShapeDtypeStruct(a.shape, a.dtype) for a in arrays]
        self.peers = list(range(1, 8 if self.all8 else 4))
        n_sems = len(self.peers) + 1
        self.sem_shapes = [pltpu.SemaphoreType.DMA((n_sems, self.n)),
                           pltpu.SemaphoreType.DMA((n_sems, self.n))]

    def _peer(self, x, y, c, me, p):
        a, d = p // 4, p % 4
        px, py = x ^ (d >> 1), y ^ (d & 1)
        pc = 1 - c if a else c
        if self.all8:
            return (px, py, pc), (pc, 2 * px + py), (c, me)
        return (px, py, pc), (2 * px + py,), (me,)

    def _blocks(self, srcs, outs, o, c, me, theirs, mine):
        if self.kind == "gather":
            return srcs[o], outs[o].at[(c,) + mine], outs[o].at[(c,) + theirs]
        if self.kind == "gather8":
            return srcs[o], outs[o].at[mine], outs[o].at[theirs]
        return srcs[o].at[theirs], outs[o].at[mine], outs[o].at[theirs]

    def start(self, srcs, outs, send_sems, recv_sems):
        x, y, c, me = _mesh_place()
        if self.is_gather:
            for o in range(self.n):
                for idx in self.chunks[o]:
                    pltpu.make_async_copy(srcs[o].at[idx], outs[o].at[(c, me) + idx],
                                          send_sems.at[0, o]).start()
        for p in self.peers:
            dev, theirs, mine = self._peer(x, y, c, me, p)
            for o in range(self.n):
                src, dst, _ = self._blocks(srcs, outs, o, c, me, theirs, mine)
                for idx in self.chunks[o]:
                    _remote(src.at[idx], dst.at[idx], send_sems.at[p, o], recv_sems.at[p, o],
                            dev).start()

    def wait(self, srcs, outs, send_sems, recv_sems):
        x, y, c, me = _mesh_place()
        for wait_recv in (True, False):
            for p in self.peers:
                dev, theirs, mine = self._peer(x, y, c, me, p)
                for o in range(self.n):
                    src, _, land = self._blocks(srcs, outs, o, c, me, theirs, mine)
                    cp = _remote(src, land, send_sems.at[p, o], recv_sems.at[p, o], dev)
                    cp.wait_recv() if wait_recv else cp.wait_send()
        if self.is_gather:
            for o in range(self.n):
                pltpu.make_async_copy(srcs[o], outs[o].at[c, me], send_sems.at[0, o]).wait()


def _run_exchange(ex, name):
    n = ex.n

    def body(*refs):
        srcs, outs, send_sems, recv_sems = refs[:n], refs[n:2 * n], refs[2 * n], refs[2 * n + 1]
        ex.start(srcs, outs, send_sems, recv_sems)
        ex.wait(srcs, outs, send_sems, recv_sems)

    return _comm_call(body, name, ex.arrays, ex.out_shapes, len(ex.peers) + 1)


def _ag_c(bufs, name):
    n = len(bufs)
    chunks = [_chunks(b.shape[2:], b.dtype, D2D_CHUNKS // 4) for b in bufs]

    def body(*refs):
        srcs, outs, send_sems, recv_sems = refs[:n], refs[n:2 * n], refs[2 * n], refs[2 * n + 1]
        x, y, c, _ = _mesh_place()
        sib = (x, y, 1 - c)
        for o in range(n):
            for k in range(4):
                for idx in chunks[o]:
                    _remote(srcs[o].at[(c, k) + idx], outs[o].at[(c, k) + idx],
                            send_sems.at[0, o], recv_sems.at[0, o], sib).start()
        for o in range(n):
            _remote(srcs[o].at[c], outs[o].at[1 - c], send_sems.at[0, o], recv_sems.at[0, o],
                    sib).wait_recv()
        for o in range(n):
            _remote(srcs[o].at[c], outs[o].at[1 - c], send_sems.at[0, o], recv_sems.at[0, o],
                    sib).wait_send()

    shapes = [jax.ShapeDtypeStruct(b.shape, b.dtype) for b in bufs]
    return _comm_call(body, name, bufs, shapes, 1, aliases={i: i for i in range(n)})


def _rs_c(gs, name):
    n = len(gs)
    chunks = [_chunks(g.shape[2:], g.dtype, max(1, D2D_CHUNKS // g.shape[1])) for g in gs]

    def body(*refs):
        srcs, outs, send_sems, recv_sems = refs[:n], refs[n:2 * n], refs[2 * n], refs[2 * n + 1]
        x, y, c, _ = _mesh_place()
        sib = (x, y, 1 - c)
        for o in range(n):
            for k in range(gs[o].shape[1]):
                for idx in chunks[o]:
                    _remote(srcs[o].at[(1 - c, k) + idx], outs[o].at[(k,) + idx],
                            send_sems.at[0, o], recv_sems.at[0, o], sib).start()
        for o in range(n):
            _remote(srcs[o].at[1 - c], outs[o], send_sems.at[0, o], recv_sems.at[0, o],
                    sib).wait_recv()
        for o in range(n):
            _remote(srcs[o].at[1 - c], outs[o], send_sems.at[0, o], recv_sems.at[0, o],
                    sib).wait_send()

    shapes = [jax.ShapeDtypeStruct(g.shape[1:], g.dtype) for g in gs]
    return _comm_call(body, name, gs, shapes, 1)


def _matmul(a, b, *, trans_b, tm, tn, name, add=None, add_scale=1.0, host=None):
    a_parts = list(a) if isinstance(a, (list, tuple)) else [a]
    M, K = a_parts[0].shape[0], sum(p.shape[1] for p in a_parts)
    N = b.shape[0] if trans_b else b.shape[1]
    tm, tn = min(tm, M), min(tn, N)
    assert M % tm == 0 and N % tn == 0
    dn = (((1,), (1,)), ((), ())) if trans_b else (((1,), (0,)), ((), ()))
    na = len(a_parts)

    def body(*refs):
        a_refs, b_ref, o_ref = refs[:na], refs[na], refs[-1]
        av = [r[...].astype(BF16) for r in a_refs]
        av = av[0] if na == 1 else jnp.concatenate(av, axis=1)
        r = lax.dot_general(av, b_ref[...].astype(BF16), dn, preferred_element_type=F32)
        if add is not None:
            r = r + add_scale * refs[na + 1][...]
        o_ref[...] = r

    b_spec = (pl.BlockSpec((tn, K), lambda j, i: (j, 0)) if trans_b
              else pl.BlockSpec((K, tn), lambda j, i: (0, j)))
    in_specs = [pl.BlockSpec((tm, p.shape[1]), lambda j, i: (i, 0)) for p in a_parts] + [b_spec]
    args = a_parts + [b]
    if add is not None:
        in_specs.append(pl.BlockSpec((tm, tn), lambda j, i: (i, j)))
        args.append(add)
    grid = (N // tn, M // tm)
    x_in, x_out, x_shapes, x_scratch, x_args = _host_specs(host)
    body = _hosted(body, len(args), 1, 0, host, grid)
    outs = pl.pallas_call(
        body, name=name, grid=grid,
        in_specs=in_specs + x_in,
        out_specs=[pl.BlockSpec((tm, tn), lambda j, i: (i, j))] + x_out,
        out_shape=[jax.ShapeDtypeStruct((M, N), F32)] + x_shapes,
        scratch_shapes=x_scratch,
        compiler_params=_cparams(*(("arbitrary",) * 2 if host else ("parallel",) * 2)),
    )(*args, *x_args)
    return outs if host else outs[0]


def _matmul_tn(a, b, *, tm, tn, tk, name):
    T, M = a.shape
    N = b.shape[1]
    tm, tn, tk = min(tm, M), min(tn, N), min(tk, T)
    assert M % tm == 0 and N % tn == 0 and T % tk == 0

    def body(a_ref, b_ref, o_ref):
        @pl.when(pl.program_id(2) == 0)
        def _():
            o_ref[...] = jnp.zeros_like(o_ref)

        o_ref[...] += lax.dot_general(a_ref[...].astype(BF16), b_ref[...].astype(BF16),
                                      (((0,), (0,)), ((), ())), preferred_element_type=F32)

    return pl.pallas_call(
        body, name=name, grid=(M // tm, N // tn, T // tk),
        in_specs=[pl.BlockSpec((tk, tm), lambda i, j, k: (k, i)),
                  pl.BlockSpec((tk, tn), lambda i, j, k: (k, j))],
        out_specs=pl.BlockSpec((tm, tn), lambda i, j, k: (i, j)),
        out_shape=jax.ShapeDtypeStruct((M, N), F32),
        compiler_params=_cparams("parallel", "parallel", "arbitrary"),
    )(a, b)


def _matmul_tn_parts(a, parts, *, tm, tk, name, host=None):
    T, M = a.shape
    tm, tk = min(tm, M), min(tk, T)
    assert M % tm == 0 and T % tk == 0
    n = len(parts)
    grid = (M // tm, T // tk)
    x_in, x_out, x_shapes, x_scratch, x_args = _host_specs(host)

    def body(*refs):
        a_ref, b_refs, o_refs = refs[0], refs[1:1 + n], refs[1 + n:]
        av = a_ref[...].astype(BF16)
        for b_ref, o_ref in zip(b_refs, o_refs):
            @pl.when(pl.program_id(1) == 0)
            def _(o_ref=o_ref):
                o_ref[...] = jnp.zeros_like(o_ref)

            o_ref[...] += lax.dot_general(av, b_ref[...].astype(BF16), (((0,), (0,)), ((), ())),
                                          preferred_element_type=F32)

    body = _hosted(body, 1 + n, n, 0, host, grid)
    return pl.pallas_call(
        body, name=name, grid=grid,
        in_specs=[pl.BlockSpec((tk, tm), lambda i, k: (k, i))]
        + [pl.BlockSpec((tk, p.shape[1]), lambda i, k: (k, 0)) for p in parts] + x_in,
        out_specs=[pl.BlockSpec((tm, p.shape[1]), lambda i, k: (i, 0)) for p in parts] + x_out,
        out_shape=[jax.ShapeDtypeStruct((M, p.shape[1]), F32) for p in parts] + x_shapes,
        scratch_shapes=x_scratch,
        compiler_params=_cparams("arbitrary" if host else "parallel", "arbitrary"),
    )(a, *parts, *x_args)


def _head_masks(rows):
    lane = lax.broadcasted_iota(jnp.int32, (rows, LANES), 1)
    return lane < HEAD_DIM, lane >= HEAD_DIM


def _causal(i_q, i_k, tq, tk):
    row = i_q * tq + lax.broadcasted_iota(jnp.int32, (tq, tk), 0)
    col = i_k * tk + lax.broadcasted_iota(jnp.int32, (tq, tk), 1)
    return row >= col


def _hosted(body, n_in, n_out, n_scratch, host, grid):
    if host is None:
        return body
    nx = host.n

    def wrapped(*refs):
        ins, xsrcs = refs[:n_in], refs[n_in:n_in + nx]
        outs = refs[n_in + nx:n_in + nx + n_out]
        xouts = refs[n_in + nx + n_out:n_in + 2 * nx + n_out]
        scratch = refs[n_in + 2 * nx + n_out:n_in + 2 * nx + n_out + n_scratch]
        xsems = refs[n_in + 2 * nx + n_out + n_scratch:]
        step = pl.program_id(0) * grid[1] + pl.program_id(1)

        @pl.when(step == 0)
        def _():
            host.start(xsrcs, xouts, *xsems)

        body(*ins, *outs, *scratch)

        @pl.when(step == grid[0] * grid[1] - 1)
        def _():
            host.wait(xsrcs, xouts, *xsems)

    return wrapped


def _host_specs(host):
    if host is None:
        return [], [], [], [], []
    return ([HBM_SPEC] * host.n, [HBM_SPEC] * host.n, host.out_shapes, host.sem_shapes, host.arrays)


def _flash_fwd(proj, cum4, *, tb, name, host=None):
    T = proj.shape[0]
    D = N_HEADS * HEAD_DIM
    nb = T // tb
    cb = D // LANES
    x_in, x_out, x_shapes, x_scratch, x_args = _host_specs(host)

    def body(q_ref, k_ref, v_ref, g_ref, cum_ref, o_ref, og_ref, lp_ref, kb_ref, vb_ref):
        i = pl.program_id(1)

        @pl.when(i == 0)
        def _():
            kb_ref[...] = k_ref[...].astype(BF16)
            vb_ref[...] = v_ref[...].astype(BF16)

        q = q_ref[...] * (HEAD_DIM ** -0.5)
        masks = _head_masks(tb)
        qh = [jnp.where(masks[h], q, 0.0).astype(BF16) for h in range(2)]
        cref = [cum_ref[0, h, pl.ds(i, 1), :][:, 0:1] for h in range(2)]

        def tile(kbi, r0, nr, nk, carry, first_row):
            k0 = pl.multiple_of(kbi * tb, tb)
            kblk = kb_ref[pl.ds(k0, nk), :]
            vblk = vb_ref[pl.ds(k0, nk), :]
            new = []
            for h in range(2):
                m, l, acc = carry[h]
                s = lax.dot_general(qh[h][r0:r0 + nr], kblk, (((1,), (1,)), ((), ())),
                                    preferred_element_type=F32)
                s = s + (cref[h] - cum_ref[0, h, pl.ds(kbi, 1), :][:, 0:nk])
                if first_row is not None:
                    row = first_row + lax.broadcasted_iota(jnp.int32, (nr, nk), 0)
                    s = jnp.where(row >= lax.broadcasted_iota(jnp.int32, (nr, nk), 1), s, -jnp.inf)
                m_new = jnp.maximum(m, jnp.max(s, axis=-1, keepdims=True))
                alpha = jnp.exp(m - m_new)
                p = jnp.exp(s - m_new)
                l = alpha * l + jnp.sum(p, axis=-1, keepdims=True)
                acc = alpha * acc + jnp.dot(p.astype(BF16), vblk, preferred_element_type=F32)
                new.append((m_new, l, acc))
            return tuple(new)

        init1 = (jnp.full((tb, 1), -jnp.inf, F32), jnp.zeros((tb, 1), F32),
                 jnp.zeros((tb, LANES), F32))
        carry = lax.fori_loop(0, i, lambda kbi, c: tile(kbi, 0, tb, tb, c, None), (init1, init1))
        hb = tb // 2
        upper = tile(i, 0, hb, hb, tuple(tuple(a[:hb] for a in c) for c in carry), 0)
        lower = tile(i, hb, hb, tb, tuple(tuple(a[hb:] for a in c) for c in carry), hb)
        outs = []
        for h, (m, l, acc) in enumerate(
                tuple(jnp.concatenate([u, w], axis=0) for u, w in zip(upper[h], lower[h]))
                for h in range(2)):
            outs.append(acc / l)
            lp_ref[h] = jnp.broadcast_to(m + jnp.log(l) - cref[h], (tb, LANES))
        o = jnp.where(masks[0], outs[0], outs[1])
        o_ref[...] = o
        gate = g_ref[...]
        og_ref[...] = (o * (gate * _sigmoid(gate))).astype(BF16)

    body = _hosted(body, 5, 3, 2, host, (N_PAIRS, nb))
    return pl.pallas_call(
        body, name=name, grid=(N_PAIRS, nb),
        in_specs=[pl.BlockSpec((tb, LANES), lambda j, i: (i, j)),
                  pl.BlockSpec((T, LANES), lambda j, i: (0, cb + j)),
                  pl.BlockSpec((T, LANES), lambda j, i: (0, 2 * cb + j)),
                  pl.BlockSpec((tb, LANES), lambda j, i: (i, 3 * cb + j)),
                  pl.BlockSpec((1, 2, nb, tb), lambda j, i: (j, 0, 0, 0))] + x_in,
        out_specs=[pl.BlockSpec((tb, LANES), lambda j, i: (i, j)),
                   pl.BlockSpec((tb, LANES), lambda j, i: (i, j)),
                   pl.BlockSpec((2, tb, LANES), lambda j, i: (j, i, 0))] + x_out,
        out_shape=[jax.ShapeDtypeStruct((T, D), F32), jax.ShapeDtypeStruct((T, D), BF16),
                   jax.ShapeDtypeStruct((N_HEADS, T, LANES), F32)] + x_shapes,
        scratch_shapes=[pltpu.VMEM((T, LANES), BF16), pltpu.VMEM((T, LANES), BF16)] + x_scratch,
        compiler_params=_cparams("arbitrary", "arbitrary"),
    )(proj, proj, proj, proj, cum4, *x_args)


def _flash_bwd(proj, cum4, o, dog, lp, *, tb, name, host=None):
    T = proj.shape[0]
    D = N_HEADS * HEAD_DIM
    nb = T // tb
    cb = D // LANES
    x_in, x_out, x_shapes, x_scratch, x_args = _host_specs(host)

    def body(q_ref, k_ref, v_ref, g_ref, cum_ref, o_ref, dog_ref, lp_ref,
             dq_ref, dg_ref, dk_ref, dv_ref, dcq_ref, dck_ref,
             kb_ref, vb_ref, dka_ref, dva_ref, dca_ref):
        i = pl.program_id(1)

        @pl.when(i == 0)
        def _():
            kb_ref[...] = k_ref[...].astype(BF16)
            vb_ref[...] = v_ref[...].astype(BF16)
            dka_ref[...] = jnp.zeros_like(dka_ref)
            dva_ref[...] = jnp.zeros_like(dva_ref)
            dca_ref[...] = jnp.zeros_like(dca_ref)

        gate = g_ref[...]
        sg = _sigmoid(gate)
        o = o_ref[...]
        dog = dog_ref[...]
        do = dog * (gate * sg)
        dg_ref[...] = (dog * o * (sg * (1.0 + gate * (1.0 - sg)))).astype(BF16)
        q = q_ref[...] * (HEAD_DIM ** -0.5)
        masks = _head_masks(tb)
        qh = [jnp.where(masks[h], q, 0.0).astype(BF16) for h in range(2)]
        doh = [jnp.where(masks[h], do, 0.0).astype(BF16) for h in range(2)]
        delta = [jnp.sum(jnp.where(masks[h], do * o, 0.0), axis=-1, keepdims=True) for h in range(2)]
        lph = [lp_ref[h][:, 0:1] for h in range(2)]

        def step(kbi, carry, masked):
            k0 = pl.multiple_of(kbi * tb, tb)
            kblk = kb_ref[pl.ds(k0, tb), :]
            vblk = vb_ref[pl.ds(k0, tb), :]
            new, dk, dv = [], None, None
            for h in range(2):
                acc, rs = carry[h]
                s = lax.dot_general(qh[h], kblk, (((1,), (1,)), ((), ())), preferred_element_type=F32)
                p = jnp.exp(s - cum_ref[0, h, pl.ds(kbi, 1), :] - lph[h])
                if masked:
                    p = jnp.where(_causal(i, kbi, tb, tb), p, 0.0)
                dp = lax.dot_general(doh[h], vblk, (((1,), (1,)), ((), ())),
                                     preferred_element_type=F32)
                ds = p * (dp - delta[h])
                pb, dsb = p.astype(BF16), ds.astype(BF16)
                dv_h = lax.dot_general(pb, doh[h], (((0,), (0,)), ((), ())),
                                       preferred_element_type=F32)
                dk_h = lax.dot_general(dsb, qh[h], (((0,), (0,)), ((), ())),
                                       preferred_element_type=F32)
                dv = dv_h if dv is None else dv + dv_h
                dk = dk_h if dk is None else dk + dk_h
                dca_ref[h, pl.ds(kbi, 1), :] -= jnp.sum(ds, axis=0, keepdims=True)
                new.append((acc + jnp.dot(dsb, kblk, preferred_element_type=F32),
                            rs + jnp.sum(ds, axis=-1, keepdims=True)))
            dka_ref[pl.ds(k0, tb), :] += dk
            dva_ref[pl.ds(k0, tb), :] += dv
            return tuple(new)

        init1 = (jnp.zeros((tb, LANES), F32), jnp.zeros((tb, 1), F32))
        carry = lax.fori_loop(0, i, lambda kbi, c: step(kbi, c, False), (init1, init1))
        dqs = []
        for h, (acc, rs) in enumerate(step(i, carry, True)):
            dqs.append(acc)
            dcq_ref[0, 0, pl.ds(h, 1), :] = jnp.broadcast_to(rs, (tb, LANES)).T[0:1, :]
        dq_ref[...] = (jnp.where(masks[0], dqs[0], dqs[1]) * (HEAD_DIM ** -0.5)).astype(BF16)

        @pl.when(i == nb - 1)
        def _():
            dk_ref[...] = dka_ref[...].astype(BF16)
            dv_ref[...] = dva_ref[...].astype(BF16)
            dck_ref[0] = dca_ref[...]

    blk = pl.BlockSpec((tb, LANES), lambda j, i: (i, j))
    full = pl.BlockSpec((T, LANES), lambda j, i: (0, j))
    body = _hosted(body, 8, 6, 5, host, (N_PAIRS, nb))
    return pl.pallas_call(
        body, name=name, grid=(N_PAIRS, nb),
        in_specs=[blk,
                  pl.BlockSpec((T, LANES), lambda j, i: (0, cb + j)),
                  pl.BlockSpec((T, LANES), lambda j, i: (0, 2 * cb + j)),
                  pl.BlockSpec((tb, LANES), lambda j, i: (i, 3 * cb + j)),
                  pl.BlockSpec((1, 2, nb, tb), lambda j, i: (j, 0, 0, 0)),
                  blk, blk, pl.BlockSpec((2, tb, LANES), lambda j, i: (j, i, 0))] + x_in,
        out_specs=[blk, blk, full, full,
                   pl.BlockSpec((1, 1, 2, tb), lambda j, i: (j, i, 0, 0)),
                   pl.BlockSpec((1, 2, nb, tb), lambda j, i: (j, 0, 0, 0))] + x_out,
        out_shape=[jax.ShapeDtypeStruct((T, D), BF16)] * 4
        + [jax.ShapeDtypeStruct((N_PAIRS, nb, 2, tb), F32),
           jax.ShapeDtypeStruct((N_PAIRS, 2, nb, tb), F32)] + x_shapes,
        scratch_shapes=[pltpu.VMEM((T, LANES), BF16), pltpu.VMEM((T, LANES), BF16),
                        pltpu.VMEM((T, LANES), F32), pltpu.VMEM((T, LANES), F32),
                        pltpu.VMEM((2, nb, tb), F32)] + x_scratch,
        compiler_params=_cparams("arbitrary", "arbitrary"),
    )(proj, proj, proj, proj, cum4, o, dog, lp, *x_args)


def _cumsum_fwd(proj, bf_row, *, tt, name):
    T = proj.shape[0]
    cb = (proj.shape[1] - LANES) // LANES

    def body(f_ref, b_ref, out_ref, carry_ref):
        i = pl.program_id(0)

        @pl.when(i == 0)
        def _():
            carry_ref[...] = jnp.zeros_like(carry_ref)

        ls = -_softplus(-(f_ref[...] + b_ref[...]))
        tri = (lax.broadcasted_iota(jnp.int32, (tt, tt), 0)
               >= lax.broadcasted_iota(jnp.int32, (tt, tt), 1)).astype(F32)
        cum = jnp.dot(tri, ls, preferred_element_type=F32,
                      precision=lax.Precision.HIGHEST) + carry_ref[...]
        carry_ref[...] = cum[tt - 1:tt, :]
        out_ref[...] = cum.T

    return pl.pallas_call(
        body, name=name, grid=(T // tt,),
        in_specs=[pl.BlockSpec((tt, LANES), lambda i: (i, cb)),
                  pl.BlockSpec((1, LANES), lambda i: (0, 0))],
        out_specs=pl.BlockSpec((LANES, tt), lambda i: (0, i)),
        out_shape=jax.ShapeDtypeStruct((LANES, T), F32),
        scratch_shapes=[pltpu.VMEM((1, LANES), F32)],
        compiler_params=_cparams("arbitrary"),
    )(proj, bf_row)


def _cumsum_bwd(dcum_t, proj, bf_row, *, tt, name):
    T = proj.shape[0]
    cb = (proj.shape[1] - LANES) // LANES
    nt = T // tt

    def body(dc_ref, f_ref, b_ref, df_ref, db_ref, carry_ref):
        i = pl.program_id(0)

        @pl.when(i == 0)
        def _():
            carry_ref[...] = jnp.zeros_like(carry_ref)
            db_ref[...] = jnp.zeros_like(db_ref)

        dc = dc_ref[...].T
        tri = (lax.broadcasted_iota(jnp.int32, (tt, tt), 0)
               <= lax.broadcasted_iota(jnp.int32, (tt, tt), 1)).astype(F32)
        rev = jnp.dot(tri, dc, preferred_element_type=F32,
                      precision=lax.Precision.HIGHEST) + carry_ref[...]
        carry_ref[...] = rev[0:1, :]
        df = rev * _sigmoid(-(f_ref[...] + b_ref[...]))
        df_ref[...] = df.astype(BF16)
        db_ref[...] += jnp.sum(df, axis=0, keepdims=True)

    return pl.pallas_call(
        body, name=name, grid=(nt,),
        in_specs=[pl.BlockSpec((LANES, tt), lambda i: (0, nt - 1 - i)),
                  pl.BlockSpec((tt, LANES), lambda i: (nt - 1 - i, cb)),
                  pl.BlockSpec((1, LANES), lambda i: (0, 0))],
        out_specs=[pl.BlockSpec((tt, LANES), lambda i: (nt - 1 - i, 0)),
                   pl.BlockSpec((1, LANES), lambda i: (0, 0))],
        out_shape=[jax.ShapeDtypeStruct((T, LANES), BF16), jax.ShapeDtypeStruct((1, LANES), F32)],
        scratch_shapes=[pltpu.VMEM((1, LANES), F32)],
        compiler_params=_cparams("arbitrary"),
    )(dcum_t, proj, bf_row)


def _rows_down(x, before, sh):
    if sh == 0:
        return x
    rolled = pltpu.roll(x, sh, axis=0)
    row = lax.broadcasted_iota(jnp.int32, (SUBLANES, x.shape[1]), 0)
    head = jnp.where(row < sh, pltpu.roll(before, sh, axis=0), rolled[:SUBLANES])
    return jnp.concatenate([head, rolled[SUBLANES:]], axis=0)


def _rows_up(x, after, sh):
    if sh == 0:
        return x
    tt = x.shape[0]
    rolled = pltpu.roll(x, tt - sh, axis=0)
    row = lax.broadcasted_iota(jnp.int32, (SUBLANES, x.shape[1]), 0)
    tail = jnp.where(row >= SUBLANES - sh, pltpu.roll(after, SUBLANES - sh, axis=0),
                     rolled[tt - SUBLANES:])
    return jnp.concatenate([rolled[:tt - SUBLANES], tail], axis=0)


def _rg_gates(u0, before, small_ref, wa_ref, wi_ref):
    taps = [_rows_down(u0, before, CONV_WIDTH - 1 - tap) for tap in range(CONV_WIDTH)]
    u = small_ref[4:5, :]
    for tap in range(CONV_WIDTH):
        u = u + taps[tap] * small_ref[tap:tap + 1, :]
    pa, pi = [], []
    for n in range(RNN_BLOCKS):
        ub = u[:, n * RNN_BLOCK_WIDTH:(n + 1) * RNN_BLOCK_WIDTH].astype(BF16)
        pa.append(jnp.dot(ub, wa_ref[n], preferred_element_type=F32))
        pi.append(jnp.dot(ub, wi_ref[n], preferred_element_type=F32))
    r = _sigmoid(jnp.concatenate(pa, axis=-1) + small_ref[5:6, :])
    ig = _sigmoid(jnp.concatenate(pi, axis=-1) + small_ref[6:7, :])
    spl = _softplus(-small_ref[7:8, :])
    log_a = (-LRU_C) * r * spl
    a = jnp.exp(log_a)
    s2 = jnp.tanh(-log_a) * (a * a + 1.0)
    inv_s = lax.rsqrt(s2)
    s = jnp.where(s2 > 0.0, s2 * inv_s, 0.0)
    return u, taps, r, ig, spl, a, s, inv_s


def _rg_fwd(proj, small, wa, wi, *, tt, name):
    T = proj.shape[0]
    D = RNN_BLOCKS * RNN_BLOCK_WIDTH
    hb = tt // SUBLANES

    def body(u0_ref, halo_ref, g_ref, small_ref, wa_ref, wi_ref, h_ref, y_ref,
             a_ref, b_ref, carry_ref):
        i = pl.program_id(0)

        @pl.when(i == 0)
        def _():
            carry_ref[...] = jnp.zeros_like(carry_ref)

        before = jnp.where(i == 0, 0.0, halo_ref[...])
        u, _, r, ig, spl, a, s, _ = _rg_gates(u0_ref[...], before, small_ref, wa_ref, wi_ref)
        a_ref[...] = a
        b_ref[...] = s * (ig * u)

        def row(t, h):
            h = a_ref[pl.ds(t, 1), :] * h + b_ref[pl.ds(t, 1), :]
            h_ref[pl.ds(t, 1), :] = h
            return h

        carry_ref[...] = lax.fori_loop(0, tt, row, carry_ref[...])
        gate = g_ref[...]
        y_ref[...] = (h_ref[...] * (gate * _sigmoid(gate))).astype(BF16)

    return pl.pallas_call(
        body, name=name, grid=(T // tt,),
        in_specs=[pl.BlockSpec((tt, D), lambda i: (i, 0)),
                  pl.BlockSpec((SUBLANES, D), lambda i: (jnp.maximum(i * hb - 1, 0), 0)),
                  pl.BlockSpec((tt, D), lambda i: (i, 1)),
                  pl.BlockSpec((SUBLANES, D), lambda i: (0, 0)),
                  pl.BlockSpec((RNN_BLOCKS, RNN_BLOCK_WIDTH, RNN_BLOCK_WIDTH), lambda i: (0, 0, 0)),
                  pl.BlockSpec((RNN_BLOCKS, RNN_BLOCK_WIDTH, RNN_BLOCK_WIDTH), lambda i: (0, 0, 0))],
        out_specs=[pl.BlockSpec((tt, D), lambda i: (i, 0)), pl.BlockSpec((tt, D), lambda i: (i, 0))],
        out_shape=[jax.ShapeDtypeStruct((T, D), F32), jax.ShapeDtypeStruct((T, D), BF16)],
        scratch_shapes=[pltpu.VMEM((tt, D), F32), pltpu.VMEM((tt, D), F32),
                        pltpu.VMEM((1, D), F32)],
        compiler_params=_cparams("arbitrary"),
    )(proj, proj, proj, small, wa, wi)


def _rg_bwd(proj, hs, dy, small, wa, wi, *, tt, name):
    T = proj.shape[0]
    D = RNN_BLOCKS * RNN_BLOCK_WIDTH
    W = RNN_BLOCK_WIDTH
    hb = tt // SUBLANES
    nt = T // tt

    def body(u0_ref, uhalo_ref, g_ref, h_ref, hhalo_ref, dy_ref, small_ref, wa_ref, wi_ref,
             dp_ref, dwa_ref, dwi_ref, ds_ref,
             a_ref, g_s_ref, dunext_ref, carry_ref):
        i = pl.program_id(0)
        first_chunk = i == nt - 1

        @pl.when(i == 0)
        def _():
            carry_ref[...] = jnp.zeros_like(carry_ref)
            dunext_ref[...] = jnp.zeros_like(dunext_ref)
            dwa_ref[...] = jnp.zeros_like(dwa_ref)
            dwi_ref[...] = jnp.zeros_like(dwi_ref)
            ds_ref[...] = jnp.zeros_like(ds_ref)

        u_before = jnp.where(first_chunk, 0.0, uhalo_ref[...])
        h_before = jnp.where(first_chunk, 0.0, hhalo_ref[...])
        u, taps, r, ig, spl, a, s, inv_s = _rg_gates(u0_ref[...], u_before, small_ref, wa_ref,
                                                     wi_ref)
        gate = g_ref[...]
        sg = _sigmoid(gate)
        dy = dy_ref[...]
        dp_ref[:, D:] = (dy * h_ref[...] * (sg * (1.0 + gate * (1.0 - sg)))).astype(BF16)
        a_ref[...] = a
        g_s_ref[...] = dy * (gate * sg)

        def row(k, c):
            t = tt - 1 - k
            g = g_s_ref[pl.ds(t, 1), :] + c
            g_s_ref[pl.ds(t, 1), :] = g
            return a_ref[pl.ds(t, 1), :] * g

        carry_ref[...] = lax.fori_loop(0, tt, row, carry_ref[...])
        g = g_s_ref[...]
        h_prev = _rows_down(h_ref[...], h_before, 1)
        iu = ig * u
        d_iu = g * s
        dlog_a = (g * h_prev) * a - (g * iu) * (a * a) * inv_s
        dpre_a = (dlog_a * ((-LRU_C) * spl)) * r * (1.0 - r)
        dpre_i = (d_iu * u) * ig * (1.0 - ig)
        dlam = jnp.sum(dlog_a * r, axis=0, keepdims=True) * (LRU_C * _sigmoid(-small_ref[7:8, :]))
        du_parts = []
        for n in range(RNN_BLOCKS):
            sl = slice(n * W, (n + 1) * W)
            ub = u[:, sl].astype(BF16)
            da_n = dpre_a[:, sl].astype(BF16)
            di_n = dpre_i[:, sl].astype(BF16)
            dwa_ref[n] += lax.dot_general(ub, da_n, (((0,), (0,)), ((), ())),
                                          preferred_element_type=F32)
            dwi_ref[n] += lax.dot_general(ub, di_n, (((0,), (0,)), ((), ())),
                                          preferred_element_type=F32)
            du_parts.append(
                lax.dot_general(da_n, wa_ref[n], (((1,), (1,)), ((), ())), preferred_element_type=F32)
                + lax.dot_general(di_n, wi_ref[n], (((1,), (1,)), ((), ())), preferred_element_type=F32))
        du = d_iu * ig + jnp.concatenate(du_parts, axis=-1)
        for tap in range(CONV_WIDTH):
            ds_ref[tap:tap + 1, :] += jnp.sum(du * taps[tap], axis=0, keepdims=True)
        ds_ref[4:5, :] += jnp.sum(du, axis=0, keepdims=True)
        ds_ref[5:6, :] += jnp.sum(dpre_a, axis=0, keepdims=True)
        ds_ref[6:7, :] += jnp.sum(dpre_i, axis=0, keepdims=True)
        ds_ref[7:8, :] += dlam
        du_after = dunext_ref[...]
        du0 = jnp.zeros((tt, D), F32)
        for tap in range(CONV_WIDTH):
            du0 = du0 + _rows_up(du, du_after, CONV_WIDTH - 1 - tap) * small_ref[tap:tap + 1, :]
        dp_ref[:, :D] = du0.astype(BF16)
        dunext_ref[...] = du[0:SUBLANES, :]

    rev = lambda i: nt - 1 - i
    wspec = pl.BlockSpec((RNN_BLOCKS, W, W), lambda i: (0, 0, 0))
    return pl.pallas_call(
        body, name=name, grid=(nt,),
        in_specs=[pl.BlockSpec((tt, D), lambda i: (rev(i), 0)),
                  pl.BlockSpec((SUBLANES, D), lambda i: (jnp.maximum(rev(i) * hb - 1, 0), 0)),
                  pl.BlockSpec((tt, D), lambda i: (rev(i), 1)),
                  pl.BlockSpec((tt, D), lambda i: (rev(i), 0)),
                  pl.BlockSpec((SUBLANES, D), lambda i: (jnp.maximum(rev(i) * hb - 1, 0), 0)),
                  pl.BlockSpec((tt, D), lambda i: (rev(i), 0)),
                  pl.BlockSpec((SUBLANES, D), lambda i: (0, 0)),
                  wspec, wspec],
        out_specs=[pl.BlockSpec((tt, 2 * D), lambda i: (rev(i), 0)),
                   wspec, wspec, pl.BlockSpec((SUBLANES, D), lambda i: (0, 0))],
        out_shape=[jax.ShapeDtypeStruct((T, 2 * D), BF16),
                   jax.ShapeDtypeStruct((RNN_BLOCKS, W, W), F32),
                   jax.ShapeDtypeStruct((RNN_BLOCKS, W, W), F32),
                   jax.ShapeDtypeStruct((SUBLANES, D), F32)],
        scratch_shapes=[pltpu.VMEM((tt, D), F32), pltpu.VMEM((tt, D), F32),
                        pltpu.VMEM((SUBLANES, D), F32), pltpu.VMEM((1, D), F32)],
        compiler_params=_cparams("arbitrary"),
    )(proj, proj, proj, hs, hs, dy, small, wa, wi)


def _out_ln(a, w, x, g, b, *, tt, name):
    T, D = x.shape
    K = a.shape[1]

    def body(a_ref, w_ref, x_ref, g_ref, b_ref, y_ref, yb_ref, zh_ref, rs_ref):
        h = jnp.dot(a_ref[...].astype(BF16), w_ref[...].astype(BF16), preferred_element_type=F32)
        z = ALPHA * x_ref[...] + h
        mu = jnp.mean(z, axis=-1, keepdims=True)
        zc = z - mu
        rstd = lax.rsqrt(jnp.mean(zc * zc, axis=-1, keepdims=True) + LN_EPS)
        zh = zc * rstd
        zh_ref[...] = zh
        rs_ref[...] = rstd
        y = zh * g_ref[...] + b_ref[...]
        y_ref[...] = y
        yb_ref[...] = y.astype(BF16)

    blk = pl.BlockSpec((tt, D), lambda i: (i, 0))
    row = pl.BlockSpec((1, D), lambda i: (0, 0))
    return pl.pallas_call(
        body, name=name, grid=(T // tt,),
        in_specs=[pl.BlockSpec((tt, K), lambda i: (i, 0)), pl.BlockSpec((K, D), lambda i: (0, 0)),
                  blk, row, row],
        out_specs=[blk, blk, blk, pl.BlockSpec((tt, 1), lambda i: (i, 0))],
        out_shape=[jax.ShapeDtypeStruct((T, D), F32), jax.ShapeDtypeStruct((T, D), BF16),
                   jax.ShapeDtypeStruct((T, D), F32), jax.ShapeDtypeStruct((T, 1), F32)],
        compiler_params=_cparams("parallel"),
    )(a, w, x, g, b)


def _ln_bwd_tile(dy, zh_ref, rs_ref, g_ref, dz_ref, dzb_ref, dg_ref, db_ref, first):
    @pl.when(first)
    def _():
        dg_ref[...] = jnp.zeros_like(dg_ref)
        db_ref[...] = jnp.zeros_like(db_ref)

    zh = zh_ref[...]
    dg_ref[...] += jnp.sum(dy * zh, axis=0, keepdims=True)
    db_ref[...] += jnp.sum(dy, axis=0, keepdims=True)
    dzh = dy * g_ref[...]
    m1 = jnp.mean(dzh, axis=-1, keepdims=True)
    m2 = jnp.mean(dzh * zh, axis=-1, keepdims=True)
    dz = rs_ref[...] * (dzh - m1 - zh * m2)
    dz_ref[...] = dz
    dzb_ref[...] = dz.astype(BF16)


def _ln_bwd_specs(T, D, tt):
    blk = pl.BlockSpec((tt, D), lambda i: (i, 0))
    row = pl.BlockSpec((1, D), lambda i: (0, 0))
    return ([blk, pl.BlockSpec((tt, 1), lambda i: (i, 0)), row], [blk, blk, row, row],
            [jax.ShapeDtypeStruct((T, D), F32), jax.ShapeDtypeStruct((T, D), BF16),
             jax.ShapeDtypeStruct((1, D), F32), jax.ShapeDtypeStruct((1, D), F32)])


def _loss_ln_bwd(y, tgt, zh, rstd, g, *, tt, name):
    T, D = y.shape
    ln_in, ln_out, ln_shapes = _ln_bwd_specs(T, D, tt)

    def body(y_ref, t_ref, zh_ref, rs_ref, g_ref, l_ref, dz_ref, dzb_ref, dg_ref, db_ref):
        first = pl.program_id(0) == 0

        @pl.when(first)
        def _():
            l_ref[...] = jnp.zeros_like(l_ref)

        e = y_ref[...] - t_ref[...]
        l_ref[...] += jnp.sum(e * e, axis=0, keepdims=True) * (0.5 / D)
        _ln_bwd_tile(e * (1.0 / D), zh_ref, rs_ref, g_ref, dz_ref, dzb_ref, dg_ref, db_ref, first)

    blk = pl.BlockSpec((tt, D), lambda i: (i, 0))
    return pl.pallas_call(
        body, name=name, grid=(T // tt,),
        in_specs=[blk, blk] + ln_in,
        out_specs=[pl.BlockSpec((1, D), lambda i: (0, 0))] + ln_out,
        out_shape=[jax.ShapeDtypeStruct((1, D), F32)] + ln_shapes,
        compiler_params=_cparams("arbitrary"),
    )(y, tgt, zh, rstd, g)


def _dx_ln_bwd(a, b, add, zh, rstd, g, *, tm, name):
    T, D = add.shape
    na = len(a)
    K = sum(p.shape[1] for p in a)
    ln_in, ln_out, ln_shapes = _ln_bwd_specs(T, D, tm)

    def body(*refs):
        a_refs, b_ref, add_ref = refs[:na], refs[na], refs[na + 1]
        av = [r[...].astype(BF16) for r in a_refs]
        av = av[0] if na == 1 else jnp.concatenate(av, axis=1)
        dy = lax.dot_general(av, b_ref[...].astype(BF16), (((1,), (1,)), ((), ())),
                             preferred_element_type=F32) + ALPHA * add_ref[...]
        _ln_bwd_tile(dy, *refs[na + 2:], pl.program_id(0) == 0)

    return pl.pallas_call(
        body, name=name, grid=(T // tm,),
        in_specs=[pl.BlockSpec((tm, p.shape[1]), lambda i: (i, 0)) for p in a]
        + [pl.BlockSpec((D, K), lambda i: (0, 0)), pl.BlockSpec((tm, D), lambda i: (i, 0))] + ln_in,
        out_specs=ln_out, out_shape=ln_shapes,
        compiler_params=_cparams("arbitrary"),
    )(*a, b, add, zh, rstd, g)


def _row_tile(rows, target):
    best = SUBLANES
    for t in range(SUBLANES, target + 1, SUBLANES):
        if rows % t == 0:
            best = t
    return best


def _add_own(g, recv, c_idx, *, tr, name):
    _, M, R, C = g.shape

    def body(c_ref, g_ref, r_ref, o_ref, ob_ref):
        s = g_ref[0] + r_ref[...]
        o_ref[...] = s
        ob_ref[...] = s.astype(BF16)

    blk = pl.BlockSpec((1, tr, C), lambda k, i, c: (k, i, 0))
    return pl.pallas_call(
        body, name=name,
        grid_spec=pltpu.PrefetchScalarGridSpec(
            num_scalar_prefetch=1, grid=(M, R // tr),
            in_specs=[pl.BlockSpec((1, 1, tr, C), lambda k, i, c: (c[0], k, i, 0)), blk],
            out_specs=[blk, blk]),
        out_shape=[jax.ShapeDtypeStruct((M, R, C), F32), jax.ShapeDtypeStruct((M, R, C), BF16)],
        compiler_params=_cparams("parallel", "parallel"),
    )(c_idx, g, recv)


def _adamw_math(g, w_ref, m_ref, v_ref, g_ref, d_ref, nm_ref, nv_ref):
    nm = ADAM_B1 * m_ref[...] + (1.0 - ADAM_B1) * g
    nv = ADAM_B2 * v_ref[...] + (1.0 - ADAM_B2) * (g * g)
    m_hat = nm / (1.0 - ADAM_B1 ** ADAM_STEP)
    v_hat = nv / (1.0 - ADAM_B2 ** ADAM_STEP)
    g_ref[...] = g
    nm_ref[...] = nm
    nv_ref[...] = nv
    d_ref[...] = (-ADAM_LR) * (m_hat / (jnp.sqrt(v_hat) + ADAM_EPS) + ADAM_WD * w_ref[...])


def _adamw(parts, w, m, v, *, tr, name):
    n, R, C = parts.shape
    tr = min(tr, R)

    def body(p_ref, w_ref, m_ref, v_ref, *out_refs):
        g = p_ref[0]
        for k in range(1, n):
            g = g + p_ref[k]
        _adamw_math(g, w_ref, m_ref, v_ref, *out_refs)

    blk = pl.BlockSpec((tr, C), lambda i: (i, 0))
    out = jax.ShapeDtypeStruct((R, C), F32)
    return pl.pallas_call(
        body, name=name, grid=(R // tr,),
        in_specs=[pl.BlockSpec((n, tr, C), lambda i: (0, i, 0)), blk, blk, blk],
        out_specs=[blk, blk, blk, blk], out_shape=[out, out, out, out],
        compiler_params=_cparams("parallel"),
    )(parts, w, m, v)


def _adamw_shard(parts_by_layer, place, w, m, v, *, tr, name):
    L, R, C = w.shape
    flat = [(l, a, pick) for l, parts in enumerate(parts_by_layer) for a, pick in parts]
    n = len(flat)

    def body(place_ref, *refs):
        w_ref, m_ref, v_ref = refs[n:n + 3]
        for layer in range(L):
            @pl.when(pl.program_id(0) == layer)
            def _(layer=layer):
                g = None
                for (l, _, _), r in zip(flat, refs[:n]):
                    if l == layer:
                        blk = r[(0,) * (len(r.shape) - 3)].astype(F32)
                        g = blk if g is None else g + blk
                _adamw_math(g, w_ref, m_ref, v_ref, *refs[n + 3:])

    blk = pl.BlockSpec((1, tr, C), lambda ly, i, s: (ly, i, 0))

    def part_spec(l, a, pick):
        return pl.BlockSpec((1,) * (a.ndim - 2) + (tr, C),
                            lambda ly, i, s: (*pick(s), jnp.where(ly == l, i, 0), 0))

    out = jax.ShapeDtypeStruct(w.shape, F32)
    return pl.pallas_call(
        body, name=name,
        grid_spec=pltpu.PrefetchScalarGridSpec(
            num_scalar_prefetch=1, grid=(L, R // tr),
            in_specs=[part_spec(*f) for f in flat] + [blk, blk, blk],
            out_specs=[blk, blk, blk, blk]),
        out_shape=[out, out, out, out],
        compiler_params=_cparams("arbitrary", "arbitrary"),
    )(place, *[a for _, a, _ in flat], w, m, v)


def _two_stage_parts(h, recv):
    return [(h, lambda s: (s[0], 0))] + [(recv, lambda s, d=d: (s[0] ^ d, 0)) for d in (1, 2, 3)]


def _direct_parts(g, recv):
    return [(g, lambda s: (s[1], s[0]))] + [
        (recv, lambda s, a=p // 4, d=p % 4: (s[1] ^ a, s[0] ^ d)) for p in range(1, 8)]


SHARD_AXIS = dict(attn_w_in=1, attn_w_out=0, rnn_w_in=1, rnn_w_out=0, rnn_w_a=1, rnn_w_i=1,
                  rnn_conv_w=1, rnn_conv_b=0, rnn_b_a=0, rnn_b_i=0, rnn_lambda=0)
RNN_ROWED = ("rnn_w_out", "rnn_w_a", "rnn_w_i")
SMALL = ("rnn_conv_w", "rnn_conv_b", "rnn_b_a", "rnn_b_i", "rnn_lambda")
PACK_C = 1024


def _elems(shape):
    n = 1
    for s in shape:
        n *= s
    return n


def _pack_rows(p, idx, dtype):
    parts = [p[k][idx].astype(dtype).reshape(-1, PACK_C) for k in RNN_ROWED]
    small = jnp.concatenate([p[k][idx].reshape(-1) for k in SMALL])
    tile_rows = SUBLANES * (4 // jnp.dtype(dtype).itemsize)
    if dtype == BF16:
        small = lax.bitcast_convert_type(small, BF16)
    small = small.reshape(-1, PACK_C)
    parts.append(jnp.pad(small, ((0, tile_rows - small.shape[0]), (0, 0))))
    return jnp.concatenate(parts, axis=0)


def _unpack_rows(flat, shapes):
    lead = flat.shape[:-2]
    out, r = {}, 0
    for k in RNN_ROWED:
        n = _elems(shapes[k]) // PACK_C
        out[k] = flat[..., r:r + n, :].reshape(lead + shapes[k])
        r += n
    n_small = sum(_elems(shapes[k]) for k in SMALL)
    small = flat[..., r:r + n_small // PACK_C, :].reshape(lead + (-1,))
    o = 0
    for k in SMALL:
        n = _elems(shapes[k])
        out[k] = small[..., o:o + n].reshape(lead + shapes[k])
        o += n
    return out


def _join_columns(g, width, *, tr, name):
    _, _, R, S = g.shape

    def body(*refs):
        o_ref = refs[8]
        parts = [refs[r][0, 0].astype(F32) for r in range(8)]
        if width > 8 * S:
            parts.append(jnp.zeros((tr, width - 8 * S), F32))
        o_ref[...] = jnp.concatenate(parts, axis=-1).astype(o_ref.dtype)

    def shard(r):
        return pl.BlockSpec((1, 1, tr, S), lambda i: (r % 2, r // 2, i, 0))

    return pl.pallas_call(
        body, name=name, grid=(R // tr,),
        in_specs=[shard(r) for r in range(8)],
        out_specs=pl.BlockSpec((tr, width), lambda i: (i, 0)),
        out_shape=jax.ShapeDtypeStruct((R, width), g.dtype),
        compiler_params=_cparams("parallel"),
    )(*([g] * 8))


def _split_columns(parts, S, *, tr, name):
    R = parts[0].shape[0]
    n = len(parts)

    def body(*refs):
        o_ref = refs[n]
        x = jnp.concatenate([r[...] for r in refs[:n]], axis=1)
        for r in range(8):
            o_ref[r % 2, r // 2] = x[:, r * S:(r + 1) * S]

    return pl.pallas_call(
        body, name=name, grid=(R // tr,),
        in_specs=[pl.BlockSpec((tr, p.shape[1]), lambda i: (i, 0)) for p in parts],
        out_specs=pl.BlockSpec((2, 4, tr, S), lambda i: (0, 0, i, 0)),
        out_shape=jax.ShapeDtypeStruct((2, 4, R, S), parts[0].dtype),
        compiler_params=_cparams("parallel"),
    )(*parts)


def _to_full(g, k, sh):
    ax, nd = SHARD_AXIS[k], len(sh)
    perm = tuple(range(2, 2 + ax)) + (1, 0) + tuple(range(2 + ax, 2 + nd))
    return g.transpose(perm).reshape(sh[:ax] + (8 * sh[ax],) + sh[ax + 1:])


def _from_full(full, k, sh):
    ax, nd = SHARD_AXIS[k], len(sh)
    t = full.reshape(sh[:ax] + (4, 2, sh[ax]) + sh[ax + 1:])
    return t.transpose((ax + 1, ax) + tuple(range(ax)) + tuple(range(ax + 2, nd + 2)))


def _unpack_gathered_rows(g, shapes):
    out, r = {}, 0
    for k in RNN_ROWED:
        n = _elems(shapes[k]) // PACK_C
        out[k] = _to_full(g[:, :, r:r + n].reshape((2, 4) + shapes[k]), k, shapes[k])
        r += n
    n_small = sum(_elems(shapes[k]) for k in SMALL)
    nr = 2 * n_small // PACK_C
    small = lax.bitcast_convert_type(g[:, :, r:r + nr].reshape(2, 4, n_small, 2), F32)
    o = 0
    for k in SMALL:
        n = _elems(shapes[k])
        out[k] = _to_full(small[:, :, o:o + n].reshape((2, 4) + shapes[k]), k, shapes[k])
        o += n
    return out


def _pack_grad_rows(full, shapes):
    parts = [_from_full(full[k], k, shapes[k]).reshape(2, 4, -1, PACK_C) for k in RNN_ROWED]
    small = jnp.concatenate(
        [_from_full(full[k], k, shapes[k]).reshape(2, 4, -1) for k in SMALL], axis=-1)
    small = small.reshape(2, 4, -1, PACK_C)
    parts.append(jnp.pad(small, ((0, 0), (0, 0), (0, SUBLANES - small.shape[2]), (0, 0))))
    return jnp.concatenate(parts, axis=2)


def kernel(x, ln_g, ln_b, attn_w_in, attn_b_f, attn_w_out, rnn_w_in, rnn_conv_w, rnn_conv_b, rnn_w_a, rnn_b_a, rnn_w_i, rnn_b_i, rnn_lambda, rnn_w_out, loss_target, m_ln_g, m_ln_b, m_attn_w_in, m_attn_b_f, m_attn_w_out, m_rnn_w_in, m_rnn_conv_w, m_rnn_conv_b, m_rnn_w_a, m_rnn_b_a, m_rnn_w_i, m_rnn_b_i, m_rnn_lambda, m_rnn_w_out, v_ln_g, v_ln_b, v_attn_w_in, v_attn_b_f, v_attn_w_out, v_rnn_w_in, v_rnn_conv_w, v_rnn_conv_b, v_rnn_w_a, v_rnn_b_a, v_rnn_w_i, v_rnn_b_i, v_rnn_lambda, v_rnn_w_out):
    w_loc = dict(attn_w_in=attn_w_in, attn_w_out=attn_w_out, rnn_w_in=rnn_w_in, rnn_w_a=rnn_w_a,
                 rnn_w_i=rnn_w_i, rnn_w_out=rnn_w_out, rnn_conv_w=rnn_conv_w, rnn_conv_b=rnn_conv_b,
                 rnn_b_a=rnn_b_a, rnn_b_i=rnn_b_i, rnn_lambda=rnn_lambda)
    m_loc = dict(attn_w_in=m_attn_w_in, attn_w_out=m_attn_w_out, rnn_w_in=m_rnn_w_in,
                 rnn_w_a=m_rnn_w_a, rnn_w_i=m_rnn_w_i, rnn_w_out=m_rnn_w_out,
                 rnn_conv_w=m_rnn_conv_w, rnn_conv_b=m_rnn_conv_b, rnn_b_a=m_rnn_b_a,
                 rnn_b_i=m_rnn_b_i, rnn_lambda=m_rnn_lambda)
    v_loc = dict(attn_w_in=v_attn_w_in, attn_w_out=v_attn_w_out, rnn_w_in=v_rnn_w_in,
                 rnn_w_a=v_rnn_w_a, rnn_w_i=v_rnn_w_i, rnn_w_out=v_rnn_w_out,
                 rnn_conv_w=v_rnn_conv_w, rnn_conv_b=v_rnn_conv_b, rnn_b_a=v_rnn_b_a,
                 rnn_b_i=v_rnn_b_i, rnn_lambda=v_rnn_lambda)
    shapes = {k: tuple(a.shape[1:]) for k, a in w_loc.items()}
    T, D = x.shape[1], x.shape[2]
    n_f = attn_b_f.shape[1]
    tb = min(1024, T)
    tb_bwd = min(512, T)
    tt_rg = min(128, T)
    tt_ln = min(256, T)
    c_idx = lax.axis_index("c").astype(jnp.int32).reshape(1)
    me_idx = (2 * lax.axis_index("x") + lax.axis_index("y")).astype(jnp.int32).reshape(1)
    place = jnp.concatenate([me_idx, c_idx])

    def attn_w_in_full(g_in, idx):
        return _join_columns(g_in, 4 * D + LANES, tr=256, name=f"a_join{idx}")

    def attn_w_out_full(g_out):
        return _to_full(g_out, "attn_w_out", shapes["attn_w_out"])

    def rnn_weights(g_in, g_rows, idx):
        w = _unpack_gathered_rows(g_rows, shapes)
        w["rnn_w_in"] = _join_columns(g_in, 2 * D, tr=256, name=f"r_join{idx}")
        w["small"] = jnp.concatenate([w["rnn_conv_w"], w["rnn_conv_b"][None], w["rnn_b_a"][None],
                                      w["rnn_b_i"][None], w["rnn_lambda"][None]])
        return w

    g0 = _ag_c(_run_exchange(_Exchange("gather", [attn_w_in[0].astype(BF16)]), "ag_w0_xy"),
               "ag_w0_c")
    later = _Exchange("gather8", [attn_w_in[1].astype(BF16)] + [
        a for i in range(2) for a in (attn_w_out[i].astype(BF16), rnn_w_in[i].astype(BF16),
                                      _pack_rows(w_loc, i, BF16))])
    w_attn_in, w_attn_out, w_rnn = [attn_w_in_full(g0[0], 0), None], [None, None], [None, None]
    bf_rows = jnp.pad(attn_b_f, ((0, 0), (0, LANES - n_f)))[:, None, :]

    xs, xb, saved = [x[0]], [x[0]], []
    for layer in range(DEPTH):
        idx, xl, xm = layer // 2, xs[-1], xb[-1]
        if layer % 2 == 0:
            proj = _matmul(xm, w_attn_in[idx], trans_b=False, tm=512, tn=1408,
                           name=f"a_proj{layer}")
            cum_t = _cumsum_fwd(proj, bf_rows[idx], tt=min(512, T), name=f"a_cum{layer}")
            cum2 = cum_t[:N_HEADS].reshape(N_PAIRS, 2, T)
            o, og, lp, *got = _flash_fwd(proj, cum2.reshape(N_PAIRS, 2, T // tb, tb), tb=tb,
                                         name=f"a_fwd{layer}", host=later if layer == 0 else None)
            cum4 = cum2.reshape(N_PAIRS, 2, T // tb_bwd, tb_bwd)
            if layer == 0:
                w_attn_in[1] = attn_w_in_full(got[0], 1)
                w_attn_out = [attn_w_out_full(got[1 + 3 * i]) for i in range(2)]
                w_rnn = [rnn_weights(got[2 + 3 * i], got[3 + 3 * i], i) for i in range(2)]
            branch, w_out = og, w_attn_out[idx]
            saved.append((proj, cum4, o, og, lp))
        else:
            w = w_rnn[idx]
            proj = _matmul(xm, w["rnn_w_in"], trans_b=False, tm=512, tn=1024,
                           name=f"r_proj{layer}")
            hs, yr = _rg_fwd(proj, w["small"], w["rnn_w_a"], w["rnn_w_i"], tt=tt_rg,
                             name=f"r_fwd{layer}")
            branch, w_out = yr, w["rnn_w_out"]
            saved.append((proj, hs, yr))
        y, yb, zh, rstd = _out_ln(branch, w_out, xl, ln_g[layer][None], ln_b[layer][None],
                                  tt=512, name=f"out_ln{layer}")
        saved[-1] = saved[-1] + (zh, rstd)
        xs.append(y)
        xb.append(yb)

    def ln_below(layer):
        return saved[layer][-2:] + (ln_g[layer][None],)

    loss_lanes, *ln_grads = _loss_ln_bwd(xs[-1], loss_target[0], *ln_below(DEPTH - 1), tt=tt_ln,
                                         name="loss_ln_bwd")
    loss = lax.psum(jnp.sum(loss_lanes), ("x", "y", "c"))

    def reduce_pair(gs, layer):
        recv = _rs_c(gs, f"rs_c{layer}")
        outs = [_add_own(g, r, c_idx, tr=_row_tile(g.shape[2], 512), name=f"rs_add{layer}_{n}")
                for n, (g, r) in enumerate(zip(gs, recv))]
        return [o[0][:, None] for o in outs], [o[1][:, None] for o in outs]

    def rep_pack(lg, lb, bf):
        rows = jnp.concatenate([lg, lb, jnp.pad(bf.reshape(1, -1), ((0, 0), (0, D - 2 * n_f)))])
        return jnp.pad(rows, ((0, 16 - rows.shape[0]), (0, 0)))

    part, got_parts = [None] * DEPTH, [None] * DEPTH
    d_ln_g, d_ln_b, d_bf = [None] * DEPTH, [None] * DEPTH, [None, None]
    for layer in reversed(range(DEPTH)):
        idx, xm = layer // 2, xb[layer]
        dz, dzb, dg, db = ln_grads
        d_ln_g[layer], d_ln_b[layer] = dg[0], db[0]
        if layer % 2 == 0:
            w_in, w_out = w_attn_in[idx], w_attn_out[idx]
            proj, cum4, o, og, lp = saved[layer][:5]
            dog = _matmul(dzb, w_out, trans_b=True, tm=512, tn=1024, name=f"a_dog{layer}")
            dwo = _matmul_tn(og, dzb, tm=512, tn=1024, tk=1024, name=f"a_dwo{layer}")
            g_out = _from_full(dwo, "attn_w_out", shapes["attn_w_out"])
            riders = [l for l in range(layer + 1, DEPTH) if got_parts[l] is None]
            early = [g_out] if layer == 0 else []
            host = _Exchange("scatter8", [g for l in riders for g in part[l]] + early)
            dq, dgate, dk, dv, dcum_q, dcum_k, *got = _flash_bwd(proj, cum4, o, dog, lp, tb=tb_bwd,
                                                                 name=f"a_bwd{layer}", host=host)
            for l in riders:
                got_parts[l], got = got[:len(part[l])], got[len(part[l]):]
            dcum_t = (dcum_q.transpose(0, 2, 1, 3) + dcum_k).reshape(N_HEADS, T)
            dcum_t = jnp.pad(dcum_t, ((0, LANES - N_HEADS), (0, 0)))
            df, dbf = _cumsum_bwd(dcum_t, proj, bf_rows[idx], tt=min(512, T), name=f"a_dcum{layer}")
            d_bf[idx] = dbf[0, :n_f]
            dproj = [dq, dk, dv, dgate, df]
            rep_host = _Exchange("gather8", [rep_pack(jnp.stack(d_ln_g), jnp.stack(d_ln_b),
                                                      jnp.stack(d_bf))]) if layer == 0 else None
            dwi = _matmul_tn_parts(xm, dproj, tm=512, tk=1024, name=f"a_dwi{layer}", host=rep_host)
            if layer == 0:
                dwi, rep = dwi[:-1], dwi[-1]
            g_in = _split_columns(dwi, shapes["attn_w_in"][1], tr=256, name=f"a_split{layer}")
            if layer > 0:
                part[layer] = [g_in, g_out]
                ln_grads = _dx_ln_bwd(dproj, w_in, dz, *ln_below(layer - 1), tm=256,
                                      name=f"a_dx{layer}")
            else:
                half, narrow = reduce_pair([g_in], layer)
                dy, recv = _matmul(dproj, w_in, trans_b=True, tm=512, tn=1024, name=f"a_dx{layer}",
                                   add=dz, add_scale=ALPHA, host=_Exchange("scatter", narrow))
                last_parts = [_two_stage_parts(half[0], recv), _direct_parts(g_out, got[0])]
        else:
            w = w_rnn[idx]
            proj, hs, yr = saved[layer][:3]
            dyr = _matmul(dzb, w["rnn_w_out"], trans_b=True, tm=512, tn=1024, name=f"r_dy{layer}")
            dwo = _matmul_tn(yr, dzb, tm=512, tn=1024, tk=1024, name=f"r_dwo{layer}")
            dproj, dwa, dwi_, dsm = _rg_bwd(proj, hs, dyr, w["small"], w["rnn_w_a"], w["rnn_w_i"],
                                            tt=tt_rg, name=f"r_bwd{layer}")
            dwin = _matmul_tn(xm, dproj, tm=512, tn=2048, tk=1024, name=f"r_dwi{layer}")
            ln_grads = _dx_ln_bwd([dproj], w["rnn_w_in"], dz, *ln_below(layer - 1), tm=512,
                                  name=f"r_dx{layer}")
            full = dict(rnn_w_out=dwo, rnn_w_a=dwa, rnn_w_i=dwi_, rnn_conv_w=dsm[0:4],
                        rnn_conv_b=dsm[4], rnn_b_a=dsm[5], rnn_b_i=dsm[6], rnn_lambda=dsm[7])
            part[layer] = [_split_columns([dwin], shapes["rnn_w_in"][1], tr=256,
                                          name=f"r_split{layer}"),
                           _pack_grad_rows(full, shapes)]
    grad_x = dy[None]

    def grad_parts(layer, n):
        return last_parts[n] if layer == 0 else _direct_parts(part[layer][n], got_parts[layer][n])

    def update(k, n, wmv):
        layers = [2 * idx + (0 if k.startswith("attn") else 1) for idx in range(2)]
        return _adamw_shard([grad_parts(layer, n) for layer in layers], place, *wmv,
                            tr=_row_tile(wmv[0].shape[1], 256), name=f"adamw_{k}")

    shard_outs = [dict() for _ in range(4)]
    for k, n in (("attn_w_in", 0), ("attn_w_out", 1), ("rnn_w_in", 0)):
        for j, a in enumerate(update(k, n, (w_loc[k], m_loc[k], v_loc[k]))):
            shard_outs[j][k] = a
    rows_wmv = [jnp.stack([_pack_rows(d, idx, F32) for idx in range(2)]) for d in (w_loc, m_loc, v_loc)]
    for j, a in enumerate(update("rnn_rows", 1, rows_wmv)):
        shard_outs[j].update(_unpack_rows(a, shapes))
    g_sh, d_sh, nm_sh, nv_sh = shard_outs

    rg, rd, rm, rv = _adamw(rep.reshape(8, 16, D), rep_pack(ln_g, ln_b, attn_b_f),
                            rep_pack(m_ln_g, m_ln_b, m_attn_b_f),
                            rep_pack(v_ln_g, v_ln_b, v_attn_b_f), tr=16, name="adamw_rep")

    def rep_unpack(a):
        return dict(ln_g=a[0:DEPTH], ln_b=a[DEPTH:2 * DEPTH],
                    attn_b_f=a[2 * DEPTH, :2 * n_f].reshape(2, n_f))

    order = ("ln_g", "ln_b", "attn_w_in", "attn_b_f", "attn_w_out", "rnn_w_in", "rnn_conv_w",
             "rnn_conv_b", "rnn_w_a", "rnn_b_a", "rnn_w_i", "rnn_b_i", "rnn_lambda", "rnn_w_out")
    outs = [loss, grad_x]
    for sh, rp in ((g_sh, rg), (d_sh, rd), (nm_sh, rm), (nv_sh, rv)):
        allp = {**sh, **rep_unpack(rp)}
        outs.extend(allp[k] for k in order)
    return tuple(outs)
```

```python
import jax
import jax.numpy as jnp
from jax import lax
from jax.experimental import pallas as pl
from jax.experimental.pallas import tpu as pltpu

F32 = jnp.float32
BF16 = jnp.bfloat16

DEPTH = 4
N_HEADS = 16
HEAD_DIM = 64
N_PAIRS = N_HEADS // 2
RNN_BLOCKS = 4
RNN_BLOCK_WIDTH = 256
CONV_WIDTH = 4
LRU_C = 8.0
ALPHA = (2.0 * DEPTH) ** 0.25
LN_EPS = 1e-5
ADAM_LR, ADAM_B1, ADAM_B2, ADAM_EPS, ADAM_WD, ADAM_STEP = 0.001, 0.9, 0.999, 1e-8, 0.01, 10

LANES = 128
SUBLANES = 8
VMEM_LIMIT = 48 * 1024 * 1024

MESH = pl.DeviceIdType.MESH
HBM_SPEC = pl.BlockSpec(memory_space=pltpu.HBM)


def _cparams(*sem):
    return pltpu.CompilerParams(dimension_semantics=sem, vmem_limit_bytes=VMEM_LIMIT)


def _sigmoid(x):
    return 1.0 / (1.0 + jnp.exp(-x))


def _softplus(x):
    return jnp.maximum(x, 0.0) + jnp.log(1.0 + jnp.exp(-jnp.abs(x)))


D2D_CHUNKS = 16
ICI_CHUNKS = 8


def _row_chunks(rows, dtype, k):
    unit = SUBLANES * (4 // jnp.dtype(dtype).itemsize)
    assert rows % unit == 0
    units = rows // unit
    k = max(1, min(k, units))
    base, rem = divmod(units, k)
    out, r = [], 0
    for i in range(k):
        n = (base + (1 if i < rem else 0)) * unit
        out.append((r, n))
        r += n
    return out


def _chunks(shape, dtype, k):
    if len(shape) == 2:
        return [(pl.ds(r0, n),) for r0, n in _row_chunks(shape[0], dtype, k)]
    per = max(1, k // shape[0])
    return [(l, pl.ds(r0, n)) for l in range(shape[0]) for r0, n in _row_chunks(shape[1], dtype, per)]


def _mesh_place():
    x, y, c = lax.axis_index("x"), lax.axis_index("y"), lax.axis_index("c")
    return x, y, c, 2 * x + y


def _chip_peer(x, y, c, d):
    px, py = x ^ (d >> 1), y ^ (d & 1)
    return (px, py, c), 2 * px + py


def _remote(src, dst, send_sem, recv_sem, dev):
    return pltpu.make_async_remote_copy(src_ref=src, dst_ref=dst, send_sem=send_sem,
                                        recv_sem=recv_sem, device_id=dev, device_id_type=MESH)


def _comm_call(body, name, ins, out_shapes, n_sems, aliases=None):
    n = len(ins)
    return pl.pallas_call(
        body, name=name,
        out_shape=out_shapes, in_specs=[HBM_SPEC] * n, out_specs=[HBM_SPEC] * n,
        input_output_aliases=aliases or {},
        scratch_shapes=[pltpu.SemaphoreType.DMA((n_sems, n)), pltpu.SemaphoreType.DMA((n_sems, n))],
    )(*ins)


class _Exchange:
    def __init__(self, kind, arrays):
        self.kind, self.arrays, self.n = kind, list(arrays), len(arrays)
        self.is_gather, self.all8 = kind.startswith("gather"), kind.endswith("8")
        k = ICI_CHUNKS // 4 if self.all8 else ICI_CHUNKS
        if self.is_gather:
            self.chunks = [_chunks(a.shape, a.dtype, k) for a in arrays]
            self.out_shapes = [jax.ShapeDtypeStruct((2, 4) + tuple(a.shape), a.dtype) for a in arrays]
        else:
            lead = 2 if self.all8 else 1
            self.chunks = [_chunks(a.shape[lead:], a.dtype, k) for a in arrays]
            self.out_shapes = [jax.ShapeDtypeStruct(a.shape, a.dtype) for a in arrays]
        self.peers = list(range(1, 8 if self.all8 else 4))
        n_sems = len(self.peers) + 1
        self.sem_shapes = [pltpu.SemaphoreType.DMA((n_sems, self.n)),
                           pltpu.SemaphoreType.DMA((n_sems, self.n))]

    def _peer(self, x, y, c, me, p):
        a, d = p // 4, p % 4
        px, py = x ^ (d >> 1), y ^ (d & 1)
        pc = 1 - c if a else c
        if self.all8:
            return (px, py, pc), (pc, 2 * px + py), (c, me)
        return (px, py, pc), (2 * px + py,), (me,)

    def _blocks(self, srcs, outs, o, c, me, theirs, mine):
        if self.kind == "gather":
            return srcs[o], outs[o].at[(c,) + mine], outs[o].at[(c,) + theirs]
        if self.kind == "gather8":
            return srcs[o], outs[o].at[mine], outs[o].at[theirs]
        return srcs[o].at[theirs], outs[o].at[mine], outs[o].at[theirs]

    def start(self, srcs, outs, send_sems, recv_sems):
        x, y, c, me = _mesh_place()
        if self.is_gather:
            for o in range(self.n):
                for idx in self.chunks[o]:
                    pltpu.make_async_copy(srcs[o].at[idx], outs[o].at[(c, me) + idx],
                                          send_sems.at[0, o]).start()
        for p in self.peers:
            dev, theirs, mine = self._peer(x, y, c, me, p)
            for o in range(self.n):
                src, dst, _ = self._blocks(srcs, outs, o, c, me, theirs, mine)
                for idx in self.chunks[o]:
                    _remote(src.at[idx], dst.at[idx], send_sems.at[p, o], recv_sems.at[p, o],
                            dev).start()

    def wait(self, srcs, outs, send_sems, recv_sems):
        x, y, c, me = _mesh_place()
        for wait_recv in (True, False):
            for p in self.peers:
                dev, theirs, mine = self._peer(x, y, c, me, p)
                for o in range(self.n):
                    src, _, land = self._blocks(srcs, outs, o, c, me, theirs, mine)
                    cp = _remote(src, land, send_sems.at[p, o], recv_sems.at[p, o], dev)
                    cp.wait_recv() if wait_recv else cp.wait_send()
        if self.is_gather:
            for o in range(self.n):
                pltpu.make_async_copy(srcs[o], outs[o].at[c, me], send_sems.at[0, o]).wait()


def _run_exchange(ex, name):
    n = ex.n

    def body(*refs):
        srcs, outs, send_sems, recv_sems = refs[:n], refs[n:2 * n], refs[2 * n], refs[2 * n + 1]
        ex.start(srcs, outs, send_sems, recv_sems)
        ex.wait(srcs, outs, send_sems, recv_sems)

    return _comm_call(body, name, ex.arrays, ex.out_shapes, len(ex.peers) + 1)


def _ag_c(bufs, name):
    n = len(bufs)
    chunks = [_chunks(b.shape[2:], b.dtype, D2D_CHUNKS // 4) for b in bufs]

    def body(*refs):
        srcs, outs, send_sems, recv_sems = refs[:n], refs[n:2 * n], refs[2 * n], refs[2 * n + 1]
        x, y, c, _ = _mesh_place()
        sib = (x, y, 1 - c)
        for o in range(n):
            for k in range(4):
                for idx in chunks[o]:
                    _remote(srcs[o].at[(c, k) + idx], outs[o].at[(c, k) + idx],
                            send_sems.at[0, o], recv_sems.at[0, o], sib).start()
        for o in range(n):
            _remote(srcs[o].at[c], outs[o].at[1 - c], send_sems.at[0, o], recv_sems.at[0, o],
                    sib).wait_recv()
        for o in range(n):
            _remote(srcs[o].at[c], outs[o].at[1 - c], send_sems.at[0, o], recv_sems.at[0, o],
                    sib).wait_send()

    shapes = [jax.ShapeDtypeStruct(b.shape, b.dtype) for b in bufs]
    return _comm_call(body, name, bufs, shapes, 1, aliases={i: i for i in range(n)})


def _rs_c(gs, name):
    n = len(gs)
    chunks = [_chunks(g.shape[2:], g.dtype, max(1, D2D_CHUNKS // g.shape[1])) for g in gs]

    def body(*refs):
        srcs, outs, send_sems, recv_sems = refs[:n], refs[n:2 * n], refs[2 * n], refs[2 * n + 1]
        x, y, c, _ = _mesh_place()
        sib = (x, y, 1 - c)
        for o in range(n):
            for k in range(gs[o].shape[1]):
                for idx in chunks[o]:
                    _remote(srcs[o].at[(1 - c, k) + idx], outs[o].at[(k,) + idx],
                            send_sems.at[0, o], recv_sems.at[0, o], sib).start()
        for o in range(n):
            _remote(srcs[o].at[1 - c], outs[o], send_sems.at[0, o], recv_sems.at[0, o],
                    sib).wait_recv()
        for o in range(n):
            _remote(srcs[o].at[1 - c], outs[o], send_sems.at[0, o], recv_sems.at[0, o],
                    sib).wait_send()

    shapes = [jax.ShapeDtypeStruct(g.shape[1:], g.dtype) for g in gs]
    return _comm_call(body, name, gs, shapes, 1)


def _matmul(a, b, *, trans_b, tm, tn, name, add=None, add_scale=1.0, host=None):
    a_parts = list(a) if isinstance(a, (list, tuple)) else [a]
    M, K = a_parts[0].shape[0], sum(p.shape[1] for p in a_parts)
    N = b.shape[0] if trans_b else b.shape[1]
    tm, tn = min(tm, M), min(tn, N)
    assert M % tm == 0 and N % tn == 0
    dn = (((1,), (1,)), ((), ())) if trans_b else (((1,), (0,)), ((), ()))
    na = len(a_parts)

    def body(*refs):
        a_refs, b_ref, o_ref = refs[:na], refs[na], refs[-1]
        av = [r[...].astype(BF16) for r in a_refs]
        av = av[0] if na == 1 else jnp.concatenate(av, axis=1)
        r = lax.dot_general(av, b_ref[...].astype(BF16), dn, preferred_element_type=F32)
        if add is not None:
            r = r + add_scale * refs[na + 1][...]
        o_ref[...] = r

    b_spec = (pl.BlockSpec((tn, K), lambda j, i: (j, 0)) if trans_b
              else pl.BlockSpec((K, tn), lambda j, i: (0, j)))
    in_specs = [pl.BlockSpec((tm, p.shape[1]), lambda j, i: (i, 0)) for p in a_parts] + [b_spec]
    args = a_parts + [b]
    if add is not None:
        in_specs.append(pl.BlockSpec((tm, tn), lambda j, i: (i, j)))
        args.append(add)
    grid = (N // tn, M // tm)
    x_in, x_out, x_shapes, x_scratch, x_args = _host_specs(host)
    body = _hosted(body, len(args), 1, 0, host, grid)
    outs = pl.pallas_call(
        body, name=name, grid=grid,
        in_specs=in_specs + x_in,
        out_specs=[pl.BlockSpec((tm, tn), lambda j, i: (i, j))] + x_out,
        out_shape=[jax.ShapeDtypeStruct((M, N), F32)] + x_shapes,
        scratch_shapes=x_scratch,
        compiler_params=_cparams(*(("arbitrary",) * 2 if host else ("parallel",) * 2)),
    )(*args, *x_args)
    return outs if host else outs[0]


def _matmul_tn(a, b, *, tm, tn, tk, name):
    T, M = a.shape
    N = b.shape[1]
    tm, tn, tk = min(tm, M), min(tn, N), min(tk, T)
    assert M % tm == 0 and N % tn == 0 and T % tk == 0

    def body(a_ref, b_ref, o_ref):
        @pl.when(pl.program_id(2) == 0)
        def _():
            o_ref[...] = jnp.zeros_like(o_ref)

        o_ref[...] += lax.dot_general(a_ref[...].astype(BF16), b_ref[...].astype(BF16),
                                      (((0,), (0,)), ((), ())), preferred_element_type=F32)

    return pl.pallas_call(
        body, name=name, grid=(M // tm, N // tn, T // tk),
        in_specs=[pl.BlockSpec((tk, tm), lambda i, j, k: (k, i)),
                  pl.BlockSpec((tk, tn), lambda i, j, k: (k, j))],
        out_specs=pl.BlockSpec((tm, tn), lambda i, j, k: (i, j)),
        out_shape=jax.ShapeDtypeStruct((M, N), F32),
        compiler_params=_cparams("parallel", "parallel", "arbitrary"),
    )(a, b)


def _matmul_tn_parts(a, parts, *, tm, tk, name, host=None):
    T, M = a.shape
    tm, tk = min(tm, M), min(tk, T)
    assert M % tm == 0 and T % tk == 0
    n = len(parts)
    grid = (M // tm, T // tk)
    x_in, x_out, x_shapes, x_scratch, x_args = _host_specs(host)

    def body(*refs):
        a_ref, b_refs, o_refs = refs[0], refs[1:1 + n], refs[1 + n:]
        av = a_ref[...].astype(BF16)
        for b_ref, o_ref in zip(b_refs, o_refs):
            @pl.when(pl.program_id(1) == 0)
            def _(o_ref=o_ref):
                o_ref[...] = jnp.zeros_like(o_ref)

            o_ref[...] += lax.dot_general(av, b_ref[...].astype(BF16), (((0,), (0,)), ((), ())),
                                          preferred_element_type=F32)

    body = _hosted(body, 1 + n, n, 0, host, grid)
    return pl.pallas_call(
        body, name=name, grid=grid,
        in_specs=[pl.BlockSpec((tk, tm), lambda i, k: (k, i))]
        + [pl.BlockSpec((tk, p.shape[1]), lambda i, k: (k, 0)) for p in parts] + x_in,
        out_specs=[pl.BlockSpec((tm, p.shape[1]), lambda i, k: (i, 0)) for p in parts] + x_out,
        out_shape=[jax.ShapeDtypeStruct((M, p.shape[1]), F32) for p in parts] + x_shapes,
        scratch_shapes=x_scratch,
        compiler_params=_cparams("arbitrary" if host else "parallel", "arbitrary"),
    )(a, *parts, *x_args)


def _head_masks(rows):
    lane = lax.broadcasted_iota(jnp.int32, (rows, LANES), 1)
    return lane < HEAD_DIM, lane >= HEAD_DIM


def _causal(i_q, i_k, tq, tk):
    row = i_q * tq + lax.broadcasted_iota(jnp.int32, (tq, tk), 0)
    col = i_k * tk + lax.broadcasted_iota(jnp.int32, (tq, tk), 1)
    return row >= col


def _hosted(body, n_in, n_out, n_scratch, host, grid):
    if host is None:
        return body
    nx = host.n

    def wrapped(*refs):
        ins, xsrcs = refs[:n_in], refs[n_in:n_in + nx]
        outs = refs[n_in + nx:n_in + nx + n_out]
        xouts = refs[n_in + nx + n_out:n_in + 2 * nx + n_out]
        scratch = refs[n_in + 2 * nx + n_out:n_in + 2 * nx + n_out + n_scratch]
        xsems = refs[n_in + 2 * nx + n_out + n_scratch:]
        step = pl.program_id(0) * grid[1] + pl.program_id(1)

        @pl.when(step == 0)
        def _():
            host.start(xsrcs, xouts, *xsems)

        body(*ins, *outs, *scratch)

        @pl.when(step == grid[0] * grid[1] - 1)
        def _():
            host.wait(xsrcs, xouts, *xsems)

    return wrapped


def _host_specs(host):
    if host is None:
        return [], [], [], [], []
    return ([HBM_SPEC] * host.n, [HBM_SPEC] * host.n, host.out_shapes, host.sem_shapes, host.arrays)


def _flash_fwd(proj, cum4, *, tb, name, host=None):
    T = proj.shape[0]
    D = N_HEADS * HEAD_DIM
    nb = T // tb
    cb = D // LANES
    x_in, x_out, x_shapes, x_scratch, x_args = _host_specs(host)

    def body(q_ref, k_ref, v_ref, g_ref, cum_ref, o_ref, og_ref, lp_ref, kb_ref, vb_ref):
        i = pl.program_id(1)

        @pl.when(i == 0)
        def _():
            kb_ref[...] = k_ref[...].astype(BF16)
            vb_ref[...] = v_ref[...].astype(BF16)

        q = q_ref[...] * (HEAD_DIM ** -0.5)
        masks = _head_masks(tb)
        qh = [jnp.where(masks[h], q, 0.0).astype(BF16) for h in range(2)]
        cref = [cum_ref[0, h, pl.ds(i, 1), :][:, 0:1] for h in range(2)]

        def tile(kbi, r0, nr, nk, carry, first_row):
            k0 = pl.multiple_of(kbi * tb, tb)
            kblk = kb_ref[pl.ds(k0, nk), :]
            vblk = vb_ref[pl.ds(k0, nk), :]
            new = []
            for h in range(2):
                m, l, acc = carry[h]
                s = lax.dot_general(qh[h][r0:r0 + nr], kblk, (((1,), (1,)), ((), ())),
                                    preferred_element_type=F32)
                s = s + (cref[h] - cum_ref[0, h, pl.ds(kbi, 1), :][:, 0:nk])
                if first_row is not None:
                    row = first_row + lax.broadcasted_iota(jnp.int32, (nr, nk), 0)
                    s = jnp.where(row >= lax.broadcasted_iota(jnp.int32, (nr, nk), 1), s, -jnp.inf)
                m_new = jnp.maximum(m, jnp.max(s, axis=-1, keepdims=True))
                alpha = jnp.exp(m - m_new)
                p = jnp.exp(s - m_new)
                l = alpha * l + jnp.sum(p, axis=-1, keepdims=True)
                acc = alpha * acc + jnp.dot(p.astype(BF16), vblk, preferred_element_type=F32)
                new.append((m_new, l, acc))
            return tuple(new)

        init1 = (jnp.full((tb, 1), -jnp.inf, F32), jnp.zeros((tb, 1), F32),
                 jnp.zeros((tb, LANES), F32))
        carry = lax.fori_loop(0, i, lambda kbi, c: tile(kbi, 0, tb, tb, c, None), (init1, init1))
        hb = tb // 2
        upper = tile(i, 0, hb, hb, tuple(tuple(a[:hb] for a in c) for c in carry), 0)
        lower = tile(i, hb, hb, tb, tuple(tuple(a[hb:] for a in c) for c in carry), hb)
        outs = []
        for h, (m, l, acc) in enumerate(
                tuple(jnp.concatenate([u, w], axis=0) for u, w in zip(upper[h], lower[h]))
                for h in range(2)):
            outs.append(acc / l)
            lp_ref[h] = jnp.broadcast_to(m + jnp.log(l) - cref[h], (tb, LANES))
        o = jnp.where(masks[0], outs[0], outs[1])
        o_ref[...] = o
        gate = g_ref[...]
        og_ref[...] = (o * (gate * _sigmoid(gate))).astype(BF16)

    body = _hosted(body, 5, 3, 2, host, (N_PAIRS, nb))
    return pl.pallas_call(
        body, name=name, grid=(N_PAIRS, nb),
        in_specs=[pl.BlockSpec((tb, LANES), lambda j, i: (i, j)),
                  pl.BlockSpec((T, LANES), lambda j, i: (0, cb + j)),
                  pl.BlockSpec((T, LANES), lambda j, i: (0, 2 * cb + j)),
                  pl.BlockSpec((tb, LANES), lambda j, i: (i, 3 * cb + j)),
                  pl.BlockSpec((1, 2, nb, tb), lambda j, i: (j, 0, 0, 0))] + x_in,
        out_specs=[pl.BlockSpec((tb, LANES), lambda j, i: (i, j)),
                   pl.BlockSpec((tb, LANES), lambda j, i: (i, j)),
                   pl.BlockSpec((2, tb, LANES), lambda j, i: (j, i, 0))] + x_out,
        out_shape=[jax.ShapeDtypeStruct((T, D), F32), jax.ShapeDtypeStruct((T, D), BF16),
                   jax.ShapeDtypeStruct((N_HEADS, T, LANES), F32)] + x_shapes,
        scratch_shapes=[pltpu.VMEM((T, LANES), BF16), pltpu.VMEM((T, LANES), BF16)] + x_scratch,
        compiler_params=_cparams("arbitrary", "arbitrary"),
    )(proj, proj, proj, proj, cum4, *x_args)


def _flash_bwd(proj, cum4, o, dog, lp, *, tb, name, host=None):
    T = proj.shape[0]
    D = N_HEADS * HEAD_DIM
    nb = T // tb
    cb = D // LANES
    x_in, x_out, x_shapes, x_scratch, x_args = _host_specs(host)

    def body(q_ref, k_ref, v_ref, g_ref, cum_ref, o_ref, dog_ref, lp_ref,
             dq_ref, dg_ref, dk_ref, dv_ref, dcq_ref, dck_ref,
             kb_ref, vb_ref, dka_ref, dva_ref, dca_ref):
        i = pl.program_id(1)

        @pl.when(i == 0)
        def _():
            kb_ref[...] = k_ref[...].astype(BF16)
            vb_ref[...] = v_ref[...].astype(BF16)
            dka_ref[...] = jnp.zeros_like(dka_ref)
            dva_ref[...] = jnp.zeros_like(dva_ref)
            dca_ref[...] = jnp.zeros_like(dca_ref)

        gate = g_ref[...]
        sg = _sigmoid(gate)
        o = o_ref[...]
        dog = dog_ref[...]
        do = dog * (gate * sg)
        dg_ref[...] = (dog * o * (sg * (1.0 + gate * (1.0 - sg)))).astype(BF16)
        q = q_ref[...] * (HEAD_DIM ** -0.5)
        masks = _head_masks(tb)
        qh = [jnp.where(masks[h], q, 0.0).astype(BF16) for h in range(2)]
        doh = [jnp.where(masks[h], do, 0.0).astype(BF16) for h in range(2)]
        delta = [jnp.sum(jnp.where(masks[h], do * o, 0.0), axis=-1, keepdims=True) for h in range(2)]
        lph = [lp_ref[h][:, 0:1] for h in range(2)]

        def step(kbi, carry, masked):
            k0 = pl.multiple_of(kbi * tb, tb)
            kblk = kb_ref[pl.ds(k0, tb), :]
            vblk = vb_ref[pl.ds(k0, tb), :]
            new, dk, dv = [], None, None
            for h in range(2):
                acc, rs = carry[h]
                s = lax.dot_general(qh[h], kblk, (((1,), (1,)), ((), ())), preferred_element_type=F32)
                p = jnp.exp(s - cum_ref[0, h, pl.ds(kbi, 1), :] - lph[h])
                if masked:
                    p = jnp.where(_causal(i, kbi, tb, tb), p, 0.0)
                dp = lax.dot_general(doh[h], vblk, (((1,), (1,)), ((), ())),
                                     preferred_element_type=F32)
                ds = p * (dp - delta[h])
                pb, dsb = p.astype(BF16), ds.astype(BF16)
                dv_h = lax.dot_general(pb, doh[h], (((0,), (0,)), ((), ())),
                                       preferred_element_type=F32)
                dk_h = lax.dot_general(dsb, qh[h], (((0,), (0,)), ((), ())),
                                       preferred_element_type=F32)
                dv = dv_h if dv is None else dv + dv_h
                dk = dk_h if dk is None else dk + dk_h
                dca_ref[h, pl.ds(kbi, 1), :] -= jnp.sum(ds, axis=0, keepdims=True)
                new.append((acc + jnp.dot(dsb, kblk, preferred_element_type=F32),
                            rs + jnp.sum(ds, axis=-1, keepdims=True)))
            dka_ref[pl.ds(k0, tb), :] += dk
            dva_ref[pl.ds(k0, tb), :] += dv
            return tuple(new)

        init1 = (jnp.zeros((tb, LANES), F32), jnp.zeros((tb, 1), F32))
        carry = lax.fori_loop(0, i, lambda kbi, c: step(kbi, c, False), (init1, init1))
        dqs = []
        for h, (acc, rs) in enumerate(step(i, carry, True)):
            dqs.append(acc)
            dcq_ref[0, 0, pl.ds(h, 1), :] = jnp.broadcast_to(rs, (tb, LANES)).T[0:1, :]
        dq_ref[...] = (jnp.where(masks[0], dqs[0], dqs[1]) * (HEAD_DIM ** -0.5)).astype(BF16)

        @pl.when(i == nb - 1)
        def _():
            dk_ref[...] = dka_ref[...].astype(BF16)
            dv_ref[...] = dva_ref[...].astype(BF16)
            dck_ref[0] = dca_ref[...]

    blk = pl.BlockSpec((tb, LANES), lambda j, i: (i, j))
    full = pl.BlockSpec((T, LANES), lambda j, i: (0, j))
    body = _hosted(body, 8, 6, 5, host, (N_PAIRS, nb))
    return pl.pallas_call(
        body, name=name, grid=(N_PAIRS, nb),
        in_specs=[blk,
                  pl.BlockSpec((T, LANES), lambda j, i: (0, cb + j)),
                  pl.BlockSpec((T, LANES), lambda j, i: (0, 2 * cb + j)),
                  pl.BlockSpec((tb, LANES), lambda j, i: (i, 3 * cb + j)),
                  pl.BlockSpec((1, 2, nb, tb), lambda j, i: (j, 0, 0, 0)),
                  blk, blk, pl.BlockSpec((2, tb, LANES), lambda j, i: (j, i, 0))] + x_in,
        out_specs=[blk, blk, full, full,
                   pl.BlockSpec((1, 1, 2, tb), lambda j, i: (j, i, 0, 0)),
                   pl.BlockSpec((1, 2, nb, tb), lambda j, i: (j, 0, 0, 0))] + x_out,
        out_shape=[jax.ShapeDtypeStruct((T, D), BF16)] * 4
        + [jax.ShapeDtypeStruct((N_PAIRS, nb, 2, tb), F32),
           jax.ShapeDtypeStruct((N_PAIRS, 2, nb, tb), F32)] + x_shapes,
        scratch_shapes=[pltpu.VMEM((T, LANES), BF16), pltpu.VMEM((T, LANES), BF16),
                        pltpu.VMEM((T, LANES), F32), pltpu.VMEM((T, LANES), F32),
                        pltpu.VMEM((2, nb, tb), F32)] + x_scratch,
        compiler_params=_cparams("arbitrary", "arbitrary"),
    )(proj, proj, proj, proj, cum4, o, dog, lp, *x_args)


def _cumsum_fwd(proj, bf_row, *, tt, name):
    T = proj.shape[0]
    cb = (proj.shape[1] - LANES) // LANES

    def body(f_ref, b_ref, out_ref, carry_ref):
        i = pl.program_id(0)

        @pl.when(i == 0)
        def _():
            carry_ref[...] = jnp.zeros_like(carry_ref)

        ls = -_softplus(-(f_ref[...] + b_ref[...]))
        tri = (lax.broadcasted_iota(jnp.int32, (tt, tt), 0)
               >= lax.broadcasted_iota(jnp.int32, (tt, tt), 1)).astype(F32)
        cum = jnp.dot(tri, ls, preferred_element_type=F32,
                      precision=lax.Precision.HIGHEST) + carry_ref[...]
        carry_ref[...] = cum[tt - 1:tt, :]
        out_ref[...] = cum.T

    return pl.pallas_call(
        body, name=name, grid=(T // tt,),
        in_specs=[pl.BlockSpec((tt, LANES), lambda i: (i, cb)),
                  pl.BlockSpec((1, LANES), lambda i: (0, 0))],
        out_specs=pl.BlockSpec((LANES, tt), lambda i: (0, i)),
        out_shape=jax.ShapeDtypeStruct((LANES, T), F32),
        scratch_shapes=[pltpu.VMEM((1, LANES), F32)],
        compiler_params=_cparams("arbitrary"),
    )(proj, bf_row)


def _cumsum_bwd(dcum_t, proj, bf_row, *, tt, name):
    T = proj.shape[0]
    cb = (proj.shape[1] - LANES) // LANES
    nt = T // tt

    def body(dc_ref, f_ref, b_ref, df_ref, db_ref, carry_ref):
        i = pl.program_id(0)

        @pl.when(i == 0)
        def _():
            carry_ref[...] = jnp.zeros_like(carry_ref)
            db_ref[...] = jnp.zeros_like(db_ref)

        dc = dc_ref[...].T
        tri = (lax.broadcasted_iota(jnp.int32, (tt, tt), 0)
               <= lax.broadcasted_iota(jnp.int32, (tt, tt), 1)).astype(F32)
        rev = jnp.dot(tri, dc, preferred_element_type=F32,
                      precision=lax.Precision.HIGHEST) + carry_ref[...]
        carry_ref[...] = rev[0:1, :]
        df = rev * _sigmoid(-(f_ref[...] + b_ref[...]))
        df_ref[...] = df.astype(BF16)
        db_ref[...] += jnp.sum(df, axis=0, keepdims=True)

    return pl.pallas_call(
        body, name=name, grid=(nt,),
        in_specs=[pl.BlockSpec((LANES, tt), lambda i: (0, nt - 1 - i)),
                  pl.BlockSpec((tt, LANES), lambda i: (nt - 1 - i, cb)),
                  pl.BlockSpec((1, LANES), lambda i: (0, 0))],
        out_specs=[pl.BlockSpec((tt, LANES), lambda i: (nt - 1 - i, 0)),
                   pl.BlockSpec((1, LANES), lambda i: (0, 0))],
        out_shape=[jax.ShapeDtypeStruct((T, LANES), BF16), jax.ShapeDtypeStruct((1, LANES), F32)],
        scratch_shapes=[pltpu.VMEM((1, LANES), F32)],
        compiler_params=_cparams("arbitrary"),
    )(dcum_t, proj, bf_row)


def _rows_down(x, before, sh):
    if sh == 0:
        return x
    rolled = pltpu.roll(x, sh, axis=0)
    row = lax.broadcasted_iota(jnp.int32, (SUBLANES, x.shape[1]), 0)
    head = jnp.where(row < sh, pltpu.roll(before, sh, axis=0), rolled[:SUBLANES])
    return jnp.concatenate([head, rolled[SUBLANES:]], axis=0)


def _rows_up(x, after, sh):
    if sh == 0:
        return x
    tt = x.shape[0]
    rolled = pltpu.roll(x, tt - sh, axis=0)
    row = lax.broadcasted_iota(jnp.int32, (SUBLANES, x.shape[1]), 0)
    tail = jnp.where(row >= SUBLANES - sh, pltpu.roll(after, SUBLANES - sh, axis=0),
                     rolled[tt - SUBLANES:])
    return jnp.concatenate([rolled[:tt - SUBLANES], tail], axis=0)


def _rg_gates(u0, before, small_ref, wa_ref, wi_ref):
    taps = [_rows_down(u0, before, CONV_WIDTH - 1 - tap) for tap in range(CONV_WIDTH)]
    u = small_ref[4:5, :]
    for tap in range(CONV_WIDTH):
        u = u + taps[tap] * small_ref[tap:tap + 1, :]
    pa, pi = [], []
    for n in range(RNN_BLOCKS):
        ub = u[:, n * RNN_BLOCK_WIDTH:(n + 1) * RNN_BLOCK_WIDTH].astype(BF16)
        pa.append(jnp.dot(ub, wa_ref[n], preferred_element_type=F32))
        pi.append(jnp.dot(ub, wi_ref[n], preferred_element_type=F32))
    r = _sigmoid(jnp.concatenate(pa, axis=-1) + small_ref[5:6, :])
    ig = _sigmoid(jnp.concatenate(pi, axis=-1) + small_ref[6:7, :])
    spl = _softplus(-small_ref[7:8, :])
    log_a = (-LRU_C) * r * spl
    a = jnp.exp(log_a)
    s2 = jnp.tanh(-log_a) * (a * a + 1.0)
    inv_s = lax.rsqrt(s2)
    s = jnp.where(s2 > 0.0, s2 * inv_s, 0.0)
    return u, taps, r, ig, spl, a, s, inv_s


def _rg_fwd(proj, small, wa, wi, *, tt, name):
    T = proj.shape[0]
    D = RNN_BLOCKS * RNN_BLOCK_WIDTH
    hb = tt // SUBLANES

    def body(u0_ref, halo_ref, g_ref, small_ref, wa_ref, wi_ref, h_ref, y_ref,
             a_ref, b_ref, carry_ref):
        i = pl.program_id(0)

        @pl.when(i == 0)
        def _():
            carry_ref[...] = jnp.zeros_like(carry_ref)

        before = jnp.where(i == 0, 0.0, halo_ref[...])
        u, _, r, ig, spl, a, s, _ = _rg_gates(u0_ref[...], before, small_ref, wa_ref, wi_ref)
        a_ref[...] = a
        b_ref[...] = s * (ig * u)

        def row(t, h):
            h = a_ref[pl.ds(t, 1), :] * h + b_ref[pl.ds(t, 1), :]
            h_ref[pl.ds(t, 1), :] = h
            return h

        carry_ref[...] = lax.fori_loop(0, tt, row, carry_ref[...])
        gate = g_ref[...]
        y_ref[...] = (h_ref[...] * (gate * _sigmoid(gate))).astype(BF16)

    return pl.pallas_call(
        body, name=name, grid=(T // tt,),
        in_specs=[pl.BlockSpec((tt, D), lambda i: (i, 0)),
                  pl.BlockSpec((SUBLANES, D), lambda i: (jnp.maximum(i * hb - 1, 0), 0)),
                  pl.BlockSpec((tt, D), lambda i: (i, 1)),
                  pl.BlockSpec((SUBLANES, D), lambda i: (0, 0)),
                  pl.BlockSpec((RNN_BLOCKS, RNN_BLOCK_WIDTH, RNN_BLOCK_WIDTH), lambda i: (0, 0, 0)),
                  pl.BlockSpec((RNN_BLOCKS, RNN_BLOCK_WIDTH, RNN_BLOCK_WIDTH), lambda i: (0, 0, 0))],
        out_specs=[pl.BlockSpec((tt, D), lambda i: (i, 0)), pl.BlockSpec((tt, D), lambda i: (i, 0))],
        out_shape=[jax.ShapeDtypeStruct((T, D), F32), jax.ShapeDtypeStruct((T, D), BF16)],
        scratch_shapes=[pltpu.VMEM((tt, D), F32), pltpu.VMEM((tt, D), F32),
                        pltpu.VMEM((1, D), F32)],
        compiler_params=_cparams("arbitrary"),
    )(proj, proj, proj, small, wa, wi)


def _rg_bwd(proj, hs, dy, small, wa, wi, *, tt, name):
    T = proj.shape[0]
    D = RNN_BLOCKS * RNN_BLOCK_WIDTH
    W = RNN_BLOCK_WIDTH
    hb = tt // SUBLANES
    nt = T // tt

    def body(u0_ref, uhalo_ref, g_ref, h_ref, hhalo_ref, dy_ref, small_ref, wa_ref, wi_ref,
             dp_ref, dwa_ref, dwi_ref, ds_ref,
             a_ref, g_s_ref, dunext_ref, carry_ref):
        i = pl.program_id(0)
        first_chunk = i == nt - 1

        @pl.when(i == 0)
        def _():
            carry_ref[...] = jnp.zeros_like(carry_ref)
            dunext_ref[...] = jnp.zeros_like(dunext_ref)
            dwa_ref[...] = jnp.zeros_like(dwa_ref)
            dwi_ref[...] = jnp.zeros_like(dwi_ref)
            ds_ref[...] = jnp.zeros_like(ds_ref)

        u_before = jnp.where(first_chunk, 0.0, uhalo_ref[...])
        h_before = jnp.where(first_chunk, 0.0, hhalo_ref[...])
        u, taps, r, ig, spl, a, s, inv_s = _rg_gates(u0_ref[...], u_before, small_ref, wa_ref,
                                                     wi_ref)
        gate = g_ref[...]
        sg = _sigmoid(gate)
        dy = dy_ref[...]
        dp_ref[:, D:] = (dy * h_ref[...] * (sg * (1.0 + gate * (1.0 - sg)))).astype(BF16)
        a_ref[...] = a
        g_s_ref[...] = dy * (gate * sg)

        def row(k, c):
            t = tt - 1 - k
            g = g_s_ref[pl.ds(t, 1), :] + c
            g_s_ref[pl.ds(t, 1), :] = g
            return a_ref[pl.ds(t, 1), :] * g

        carry_ref[...] = lax.fori_loop(0, tt, row, carry_ref[...])
        g = g_s_ref[...]
        h_prev = _rows_down(h_ref[...], h_before, 1)
        iu = ig * u
        d_iu = g * s
        dlog_a = (g * h_prev) * a - (g * iu) * (a * a) * inv_s
        dpre_a = (dlog_a * ((-LRU_C) * spl)) * r * (1.0 - r)
        dpre_i = (d_iu * u) * ig * (1.0 - ig)
        dlam = jnp.sum(dlog_a * r, axis=0, keepdims=True) * (LRU_C * _sigmoid(-small_ref[7:8, :]))
        du_parts = []
        for n in range(RNN_BLOCKS):
            sl = slice(n * W, (n + 1) * W)
            ub = u[:, sl].astype(BF16)
            da_n = dpre_a[:, sl].astype(BF16)
            di_n = dpre_i[:, sl].astype(BF16)
            dwa_ref[n] += lax.dot_general(ub, da_n, (((0,), (0,)), ((), ())),
                                          preferred_element_type=F32)
            dwi_ref[n] += lax.dot_general(ub, di_n, (((0,), (0,)), ((), ())),
                                          preferred_element_type=F32)
            du_parts.append(
                lax.dot_general(da_n, wa_ref[n], (((1,), (1,)), ((), ())), preferred_element_type=F32)
                + lax.dot_general(di_n, wi_ref[n], (((1,), (1,)), ((), ())), preferred_element_type=F32))
        du = d_iu * ig + jnp.concatenate(du_parts, axis=-1)
        for tap in range(CONV_WIDTH):
            ds_ref[tap:tap + 1, :] += jnp.sum(du * taps[tap], axis=0, keepdims=True)
        ds_ref[4:5, :] += jnp.sum(du, axis=0, keepdims=True)
        ds_ref[5:6, :] += jnp.sum(dpre_a, axis=0, keepdims=True)
        ds_ref[6:7, :] += jnp.sum(dpre_i, axis=0, keepdims=True)
        ds_ref[7:8, :] += dlam
        du_after = dunext_ref[...]
        du0 = jnp.zeros((tt, D), F32)
        for tap in range(CONV_WIDTH):
            du0 = du0 + _rows_up(du, du_after, CONV_WIDTH - 1 - tap) * small_ref[tap:tap + 1, :]
        dp_ref[:, :D] = du0.astype(BF16)
        dunext_ref[...] = du[0:SUBLANES, :]

    rev = lambda i: nt - 1 - i
    wspec = pl.BlockSpec((RNN_BLOCKS, W, W), lambda i: (0, 0, 0))
    return pl.pallas_call(
        body, name=name, grid=(nt,),
        in_specs=[pl.BlockSpec((tt, D), lambda i: (rev(i), 0)),
                  pl.BlockSpec((SUBLANES, D), lambda i: (jnp.maximum(rev(i) * hb - 1, 0), 0)),
                  pl.BlockSpec((tt, D), lambda i: (rev(i), 1)),
                  pl.BlockSpec((tt, D), lambda i: (rev(i), 0)),
                  pl.BlockSpec((SUBLANES, D), lambda i: (jnp.maximum(rev(i) * hb - 1, 0), 0)),
                  pl.BlockSpec((tt, D), lambda i: (rev(i), 0)),
                  pl.BlockSpec((SUBLANES, D), lambda i: (0, 0)),
                  wspec, wspec],
        out_specs=[pl.BlockSpec((tt, 2 * D), lambda i: (rev(i), 0)),
                   wspec, wspec, pl.BlockSpec((SUBLANES, D), lambda i: (0, 0))],
        out_shape=[jax.ShapeDtypeStruct((T, 2 * D), BF16),
                   jax.ShapeDtypeStruct((RNN_BLOCKS, W, W), F32),
                   jax.ShapeDtypeStruct((RNN_BLOCKS, W, W), F32),
                   jax.ShapeDtypeStruct((SUBLANES, D), F32)],
        scratch_shapes=[pltpu.VMEM((tt, D), F32), pltpu.VMEM((tt, D), F32),
                        pltpu.VMEM((SUBLANES, D), F32), pltpu.VMEM((1, D), F32)],
        compiler_params=_cparams("arbitrary"),
    )(proj, proj, proj, hs, hs, dy, small, wa, wi)


def _out_ln(a, w, x, g, b, *, tt, name):
    T, D = x.shape
    K = a.shape[1]

    def body(a_ref, w_ref, x_ref, g_ref, b_ref, y_ref, yb_ref, zh_ref, rs_ref):
        h = jnp.dot(a_ref[...].astype(BF16), w_ref[...].astype(BF16), preferred_element_type=F32)
        z = ALPHA * x_ref[...] + h
        mu = jnp.mean(z, axis=-1, keepdims=True)
        zc = z - mu
        rstd = lax.rsqrt(jnp.mean(zc * zc, axis=-1, keepdims=True) + LN_EPS)
        zh = zc * rstd
        zh_ref[...] = zh
        rs_ref[...] = rstd
        y = zh * g_ref[...] + b_ref[...]
        y_ref[...] = y
        yb_ref[...] = y.astype(BF16)

    blk = pl.BlockSpec((tt, D), lambda i: (i, 0))
    row = pl.BlockSpec((1, D), lambda i: (0, 0))
    return pl.pallas_call(
        body, name=name, grid=(T // tt,),
        in_specs=[pl.BlockSpec((tt, K), lambda i: (i, 0)), pl.BlockSpec((K, D), lambda i: (0, 0)),
                  blk, row, row],
        out_specs=[blk, blk, blk, pl.BlockSpec((tt, 1), lambda i: (i, 0))],
        out_shape=[jax.ShapeDtypeStruct((T, D), F32), jax.ShapeDtypeStruct((T, D), BF16),
                   jax.ShapeDtypeStruct((T, D), F32), jax.ShapeDtypeStruct((T, 1), F32)],
        compiler_params=_cparams("parallel"),
    )(a, w, x, g, b)


def _ln_bwd_tile(dy, zh_ref, rs_ref, g_ref, dz_ref, dzb_ref, dg_ref, db_ref, first):
    @pl.when(first)
    def _():
        dg_ref[...] = jnp.zeros_like(dg_ref)
        db_ref[...] = jnp.zeros_like(db_ref)

    zh = zh_ref[...]
    dg_ref[...] += jnp.sum(dy * zh, axis=0, keepdims=True)
    db_ref[...] += jnp.sum(dy, axis=0, keepdims=True)
    dzh = dy * g_ref[...]
    m1 = jnp.mean(dzh, axis=-1, keepdims=True)
    m2 = jnp.mean(dzh * zh, axis=-1, keepdims=True)
    dz = rs_ref[...] * (dzh - m1 - zh * m2)
    dz_ref[...] = dz
    dzb_ref[...] = dz.astype(BF16)


def _ln_bwd_specs(T, D, tt):
    blk = pl.BlockSpec((tt, D), lambda i: (i, 0))
    row = pl.BlockSpec((1, D), lambda i: (0, 0))
    return ([blk, pl.BlockSpec((tt, 1), lambda i: (i, 0)), row], [blk, blk, row, row],
            [jax.ShapeDtypeStruct((T, D), F32), jax.ShapeDtypeStruct((T, D), BF16),
             jax.ShapeDtypeStruct((1, D), F32), jax.ShapeDtypeStruct((1, D), F32)])


def _loss_ln_bwd(y, tgt, zh, rstd, g, *, tt, name):
    T, D = y.shape
    ln_in, ln_out, ln_shapes = _ln_bwd_specs(T, D, tt)

    def body(y_ref, t_ref, zh_ref, rs_ref, g_ref, l_ref, dz_ref, dzb_ref, dg_ref, db_ref):
        first = pl.program_id(0) == 0

        @pl.when(first)
        def _():
            l_ref[...] = jnp.zeros_like(l_ref)

        e = y_ref[...] - t_ref[...]
        l_ref[...] += jnp.sum(e * e, axis=0, keepdims=True) * (0.5 / D)
        _ln_bwd_tile(e * (1.0 / D), zh_ref, rs_ref, g_ref, dz_ref, dzb_ref, dg_ref, db_ref, first)

    blk = pl.BlockSpec((tt, D), lambda i: (i, 0))
    return pl.pallas_call(
        body, name=name, grid=(T // tt,),
        in_specs=[blk, blk] + ln_in,
        out_specs=[pl.BlockSpec((1, D), lambda i: (0, 0))] + ln_out,
        out_shape=[jax.ShapeDtypeStruct((1, D), F32)] + ln_shapes,
        compiler_params=_cparams("arbitrary"),
    )(y, tgt, zh, rstd, g)


def _dx_ln_bwd(a, b, add, zh, rstd, g, *, tm, name):
    T, D = add.shape
    na = len(a)
    K = sum(p.shape[1] for p in a)
    ln_in, ln_out, ln_shapes = _ln_bwd_specs(T, D, tm)

    def body(*refs):
        a_refs, b_ref, add_ref = refs[:na], refs[na], refs[na + 1]
        av = [r[...].astype(BF16) for r in a_refs]
        av = av[0] if na == 1 else jnp.concatenate(av, axis=1)
        dy = lax.dot_general(av, b_ref[...].astype(BF16), (((1,), (1,)), ((), ())),
                             preferred_element_type=F32) + ALPHA * add_ref[...]
        _ln_bwd_tile(dy, *refs[na + 2:], pl.program_id(0) == 0)

    return pl.pallas_call(
        body, name=name, grid=(T // tm,),
        in_specs=[pl.BlockSpec((tm, p.shape[1]), lambda i: (i, 0)) for p in a]
        + [pl.BlockSpec((D, K), lambda i: (0, 0)), pl.BlockSpec((tm, D), lambda i: (i, 0))] + ln_in,
        out_specs=ln_out, out_shape=ln_shapes,
        compiler_params=_cparams("arbitrary"),
    )(*a, b, add, zh, rstd, g)


def _row_tile(rows, target):
    best = SUBLANES
    for t in range(SUBLANES, target + 1, SUBLANES):
        if rows % t == 0:
            best = t
    return best


def _add_own(g, recv, c_idx, *, tr, name):
    _, M, R, C = g.shape

    def body(c_ref, g_ref, r_ref, o_ref, ob_ref):
        s = g_ref[0] + r_ref[...]
        o_ref[...] = s
        ob_ref[...] = s.astype(BF16)

    blk = pl.BlockSpec((1, tr, C), lambda k, i, c: (k, i, 0))
    return pl.pallas_call(
        body, name=name,
        grid_spec=pltpu.PrefetchScalarGridSpec(
            num_scalar_prefetch=1, grid=(M, R // tr),
            in_specs=[pl.BlockSpec((1, 1, tr, C), lambda k, i, c: (c[0], k, i, 0)), blk],
            out_specs=[blk, blk]),
        out_shape=[jax.ShapeDtypeStruct((M, R, C), F32), jax.ShapeDtypeStruct((M, R, C), BF16)],
        compiler_params=_cparams("parallel", "parallel"),
    )(c_idx, g, recv)


def _adamw_math(g, w_ref, m_ref, v_ref, g_ref, d_ref, nm_ref, nv_ref):
    nm = ADAM_B1 * m_ref[...] + (1.0 - ADAM_B1) * g
    nv = ADAM_B2 * v_ref[...] + (1.0 - ADAM_B2) * (g * g)
    m_hat = nm / (1.0 - ADAM_B1 ** ADAM_STEP)
    v_hat = nv / (1.0 - ADAM_B2 ** ADAM_STEP)
    g_ref[...] = g
    nm_ref[...] = nm
    nv_ref[...] = nv
    d_ref[...] = (-ADAM_LR) * (m_hat / (jnp.sqrt(v_hat) + ADAM_EPS) + ADAM_WD * w_ref[...])


def _adamw(parts, w, m, v, *, tr, name):
    n, R, C = parts.shape
    tr = min(tr, R)

    def body(p_ref, w_ref, m_ref, v_ref, *out_refs):
        g = p_ref[0]
        for k in range(1, n):
            g = g + p_ref[k]
        _adamw_math(g, w_ref, m_ref, v_ref, *out_refs)

    blk = pl.BlockSpec((tr, C), lambda i: (i, 0))
    out = jax.ShapeDtypeStruct((R, C), F32)
    return pl.pallas_call(
        body, name=name, grid=(R // tr,),
        in_specs=[pl.BlockSpec((n, tr, C), lambda i: (0, i, 0)), blk, blk, blk],
        out_specs=[blk, blk, blk, blk], out_shape=[out, out, out, out],
        compiler_params=_cparams("parallel"),
    )(parts, w, m, v)


def _adamw_shard(parts_by_layer, place, w, m, v, *, tr, name):
    L, R, C = w.shape
    flat = [(l, a, pick) for l, parts in enumerate(parts_by_layer) for a, pick in parts]
    n = len(flat)

    def body(place_ref, *refs):
        w_ref, m_ref, v_ref = refs[n:n + 3]
        for layer in range(L):
            @pl.when(pl.program_id(0) == layer)
            def _(layer=layer):
                g = None
                for (l, _, _), r in zip(flat, refs[:n]):
                    if l == layer:
                        blk = r[(0,) * (len(r.shape) - 3)].astype(F32)
                        g = blk if g is None else g + blk
                _adamw_math(g, w_ref, m_ref, v_ref, *refs[n + 3:])

    blk = pl.BlockSpec((1, tr, C), lambda ly, i, s: (ly, i, 0))

    def part_spec(l, a, pick):
        return pl.BlockSpec((1,) * (a.ndim - 2) + (tr, C),
                            lambda ly, i, s: (*pick(s), jnp.where(ly == l, i, 0), 0))

    out = jax.ShapeDtypeStruct(w.shape, F32)
    return pl.pallas_call(
        body, name=name,
        grid_spec=pltpu.PrefetchScalarGridSpec(
            num_scalar_prefetch=1, grid=(L, R // tr),
            in_specs=[part_spec(*f) for f in flat] + [blk, blk, blk],
            out_specs=[blk, blk, blk, blk]),
        out_shape=[out, out, out, out],
        compiler_params=_cparams("arbitrary", "arbitrary"),
    )(place, *[a for _, a, _ in flat], w, m, v)


def _two_stage_parts(h, recv):
    return [(h, lambda s: (s[0], 0))] + [(recv, lambda s, d=d: (s[0] ^ d, 0)) for d in (1, 2, 3)]


def _direct_parts(g, recv):
    return [(g, lambda s: (s[1], s[0]))] + [
        (recv, lambda s, a=p // 4, d=p % 4: (s[1] ^ a, s[0] ^ d)) for p in range(1, 8)]


SHARD_AXIS = dict(attn_w_in=1, attn_w_out=0, rnn_w_in=1, rnn_w_out=0, rnn_w_a=1, rnn_w_i=1,
                  rnn_conv_w=1, rnn_conv_b=0, rnn_b_a=0, rnn_b_i=0, rnn_lambda=0)
RNN_ROWED = ("rnn_w_out", "rnn_w_a", "rnn_w_i")
SMALL = ("rnn_conv_w", "rnn_conv_b", "rnn_b_a", "rnn_b_i", "rnn_lambda")
PACK_C = 1024


def _elems(shape):
    n = 1
    for s in shape:
        n *= s
    return n


def _pack_rows(p, idx, dtype):
    parts = [p[k][idx].astype(dtype).reshape(-1, PACK_C) for k in RNN_ROWED]
    small = jnp.concatenate([p[k][idx].reshape(-1) for k in SMALL])
    tile_rows = SUBLANES * (4 // jnp.dtype(dtype).itemsize)
    if dtype == BF16:
        small = lax.bitcast_convert_type(small, BF16)
    small = small.reshape(-1, PACK_C)
    parts.append(jnp.pad(small, ((0, tile_rows - small.shape[0]), (0, 0))))
    return jnp.concatenate(parts, axis=0)


def _unpack_rows(flat, shapes):
    lead = flat.shape[:-2]
    out, r = {}, 0
    for k in RNN_ROWED:
        n = _elems(shapes[k]) // PACK_C
        out[k] = flat[..., r:r + n, :].reshape(lead + shapes[k])
        r += n
    n_small = sum(_elems(shapes[k]) for k in SMALL)
    small = flat[..., r:r + n_small // PACK_C, :].reshape(lead + (-1,))
    o = 0
    for k in SMALL:
        n = _elems(shapes[k])
        out[k] = small[..., o:o + n].reshape(lead + shapes[k])
        o += n
    return out


def _join_columns(g, width, *, tr, name):
    _, _, R, S = g.shape

    def body(*refs):
        o_ref = refs[8]
        parts = [refs[r][0, 0].astype(F32) for r in range(8)]
        if width > 8 * S:
            parts.append(jnp.zeros((tr, width - 8 * S), F32))
        o_ref[...] = jnp.concatenate(parts, axis=-1).astype(o_ref.dtype)

    def shard(r):
        return pl.BlockSpec((1, 1, tr, S), lambda i: (r % 2, r // 2, i, 0))

    return pl.pallas_call(
        body, name=name, grid=(R // tr,),
        in_specs=[shard(r) for r in range(8)],
        out_specs=pl.BlockSpec((tr, width), lambda i: (i, 0)),
        out_shape=jax.ShapeDtypeStruct((R, width), g.dtype),
        compiler_params=_cparams("parallel"),
    )(*([g] * 8))


def _split_columns(parts, S, *, tr, name):
    R = parts[0].shape[0]
    n = len(parts)

    def body(*refs):
        o_ref = refs[n]
        x = jnp.concatenate([r[...] for r in refs[:n]], axis=1)
        for r in range(8):
            o_ref[r % 2, r // 2] = x[:, r * S:(r + 1) * S]

    return pl.pallas_call(
        body, name=name, grid=(R // tr,),
        in_specs=[pl.BlockSpec((tr, p.shape[1]), lambda i: (i, 0)) for p in parts],
        out_specs=pl.BlockSpec((2, 4, tr, S), lambda i: (0, 0, i, 0)),
        out_shape=jax.ShapeDtypeStruct((2, 4, R, S), parts[0].dtype),
        compiler_params=_cparams("parallel"),
    )(*parts)


def _to_full(g, k, sh):
    ax, nd = SHARD_AXIS[k], len(sh)
    perm = tuple(range(2, 2 + ax)) + (1, 0) + tuple(range(2 + ax, 2 + nd))
    return g.transpose(perm).reshape(sh[:ax] + (8 * sh[ax],) + sh[ax + 1:])


def _from_full(full, k, sh):
    ax, nd = SHARD_AXIS[k], len(sh)
    t = full.reshape(sh[:ax] + (4, 2, sh[ax]) + sh[ax + 1:])
    return t.transpose((ax + 1, ax) + tuple(range(ax)) + tuple(range(ax + 2, nd + 2)))


def _unpack_gathered_rows(g, shapes):
    out, r = {}, 0
    for k in RNN_ROWED:
        n = _elems(shapes[k]) // PACK_C
        out[k] = _to_full(g[:, :, r:r + n].reshape((2, 4) + shapes[k]), k, shapes[k])
        r += n
    n_small = sum(_elems(shapes[k]) for k in SMALL)
    nr = 2 * n_small // PACK_C
    small = lax.bitcast_convert_type(g[:, :, r:r + nr].reshape(2, 4, n_small, 2), F32)
    o = 0
    for k in SMALL:
        n = _elems(shapes[k])
        out[k] = _to_full(small[:, :, o:o + n].reshape((2, 4) + shapes[k]), k, shapes[k])
        o += n
    return out


def _pack_grad_rows(full, shapes):
    parts = [_from_full(full[k], k, shapes[k]).reshape(2, 4, -1, PACK_C) for k in RNN_ROWED]
    small = jnp.concatenate(
        [_from_full(full[k], k, shapes[k]).reshape(2, 4, -1) for k in SMALL], axis=-1)
    small = small.reshape(2, 4, -1, PACK_C)
    parts.append(jnp.pad(small, ((0, 0), (0, 0), (0, SUBLANES - small.shape[2]), (0, 0))))
    return jnp.concatenate(parts, axis=2)


def kernel(x, ln_g, ln_b, attn_w_in, attn_b_f, attn_w_out, rnn_w_in, rnn_conv_w, rnn_conv_b, rnn_w_a, rnn_b_a, rnn_w_i, rnn_b_i, rnn_lambda, rnn_w_out, loss_target, m_ln_g, m_ln_b, m_attn_w_in, m_attn_b_f, m_attn_w_out, m_rnn_w_in, m_rnn_conv_w, m_rnn_conv_b, m_rnn_w_a, m_rnn_b_a, m_rnn_w_i, m_rnn_b_i, m_rnn_lambda, m_rnn_w_out, v_ln_g, v_ln_b, v_attn_w_in, v_attn_b_f, v_attn_w_out, v_rnn_w_in, v_rnn_conv_w, v_rnn_conv_b, v_rnn_w_a, v_rnn_b_a, v_rnn_w_i, v_rnn_b_i, v_rnn_lambda, v_rnn_w_out):
    w_loc = dict(attn_w_in=attn_w_in, attn_w_out=attn_w_out, rnn_w_in=rnn_w_in, rnn_w_a=rnn_w_a,
                 rnn_w_i=rnn_w_i, rnn_w_out=rnn_w_out, rnn_conv_w=rnn_conv_w, rnn_conv_b=rnn_conv_b,
                 rnn_b_a=rnn_b_a, rnn_b_i=rnn_b_i, rnn_lambda=rnn_lambda)
    m_loc = dict(attn_w_in=m_attn_w_in, attn_w_out=m_attn_w_out, rnn_w_in=m_rnn_w_in,
                 rnn_w_a=m_rnn_w_a, rnn_w_i=m_rnn_w_i, rnn_w_out=m_rnn_w_out,
                 rnn_conv_w=m_rnn_conv_w, rnn_conv_b=m_rnn_conv_b, rnn_b_a=m_rnn_b_a,
                 rnn_b_i=m_rnn_b_i, rnn_lambda=m_rnn_lambda)
    v_loc = dict(attn_w_in=v_attn_w_in, attn_w_out=v_attn_w_out, rnn_w_in=v_rnn_w_in,
                 rnn_w_a=v_rnn_w_a, rnn_w_i=v_rnn_w_i, rnn_w_out=v_rnn_w_out,
                 rnn_conv_w=v_rnn_conv_w, rnn_conv_b=v_rnn_conv_b, rnn_b_a=v_rnn_b_a,
                 rnn_b_i=v_rnn_b_i, rnn_lambda=v_rnn_lambda)
    shapes = {k: tuple(a.shape[1:]) for k, a in w_loc.items()}
    T, D = x.shape[1], x.shape[2]
    n_f = attn_b_f.shape[1]
    tb = min(1024, T)
    tb_bwd = min(512, T)
    tt_rg = min(128, T)
    tt_ln = min(256, T)
    c_idx = lax.axis_index("c").astype(jnp.int32).reshape(1)
    me_idx = (2 * lax.axis_index("x") + lax.axis_index("y")).astype(jnp.int32).reshape(1)
    place = jnp.concatenate([me_idx, c_idx])

    def attn_w_in_full(g_in, idx):
        return _join_columns(g_in, 4 * D + LANES, tr=256, name=f"a_join{idx}")

    def attn_w_out_full(g_out):
        return _to_full(g_out, "attn_w_out", shapes["attn_w_out"])

    def rnn_weights(g_in, g_rows, idx):
        w = _unpack_gathered_rows(g_rows, shapes)
        w["rnn_w_in"] = _join_columns(g_in, 2 * D, tr=256, name=f"r_join{idx}")
        w["small"] = jnp.concatenate([w["rnn_conv_w"], w["rnn_conv_b"][None], w["rnn_b_a"][None],
                                      w["rnn_b_i"][None], w["rnn_lambda"][None]])
        return w

    g0 = _ag_c(_run_exchange(_Exchange("gather", [attn_w_in[0].astype(BF16)]), "ag_w0_xy"),
               "ag_w0_c")
    later = _Exchange("gather8", [attn_w_in[1].astype(BF16)] + [
        a for i in range(2) for a in (attn_w_out[i].astype(BF16), rnn_w_in[i].astype(BF16),
                                      _pack_rows(w_loc, i, BF16))])
    w_attn_in, w_attn_out, w_rnn = [attn_w_in_full(g0[0], 0), None], [None, None], [None, None]
    bf_rows = jnp.pad(attn_b_f, ((0, 0), (0, LANES - n_f)))[:, None, :]

    xs, xb, saved = [x[0]], [x[0]], []
    for layer in range(DEPTH):
        idx, xl, xm = layer // 2, xs[-1], xb[-1]
        if layer % 2 == 0:
            proj = _matmul(xm, w_attn_in[idx], trans_b=False, tm=512, tn=1408,
                           name=f"a_proj{layer}")
            cum_t = _cumsum_fwd(proj, bf_rows[idx], tt=min(512, T), name=f"a_cum{layer}")
            cum2 = cum_t[:N_HEADS].reshape(N_PAIRS, 2, T)
            o, og, lp, *got = _flash_fwd(proj, cum2.reshape(N_PAIRS, 2, T // tb, tb), tb=tb,
                                         name=f"a_fwd{layer}", host=later if layer == 0 else None)
            cum4 = cum2.reshape(N_PAIRS, 2, T // tb_bwd, tb_bwd)
            if layer == 0:
                w_attn_in[1] = attn_w_in_full(got[0], 1)
                w_attn_out = [attn_w_out_full(got[1 + 3 * i]) for i in range(2)]
                w_rnn = [rnn_weights(got[2 + 3 * i], got[3 + 3 * i], i) for i in range(2)]
            branch, w_out = og, w_attn_out[idx]
            saved.append((proj, cum4, o, og, lp))
        else:
            w = w_rnn[idx]
            proj = _matmul(xm, w["rnn_w_in"], trans_b=False, tm=512, tn=1024,
                           name=f"r_proj{layer}")
            hs, yr = _rg_fwd(proj, w["small"], w["rnn_w_a"], w["rnn_w_i"], tt=tt_rg,
                             name=f"r_fwd{layer}")
            branch, w_out = yr, w["rnn_w_out"]
            saved.append((proj, hs, yr))
        y, yb, zh, rstd = _out_ln(branch, w_out, xl, ln_g[layer][None], ln_b[layer][None],
                                  tt=512, name=f"out_ln{layer}")
        saved[-1] = saved[-1] + (zh, rstd)
        xs.append(y)
        xb.append(yb)

    def ln_below(layer):
        return saved[layer][-2:] + (ln_g[layer][None],)

    loss_lanes, *ln_grads = _loss_ln_bwd(xs[-1], loss_target[0], *ln_below(DEPTH - 1), tt=tt_ln,
                                         name="loss_ln_bwd")
    loss_part = jnp.sum(loss_lanes)

    def reduce_pair(gs, layer):
        recv = _rs_c(gs, f"rs_c{layer}")
        outs = [_add_own(g, r, c_idx, tr=_row_tile(g.shape[2], 512), name=f"rs_add{layer}_{n}")
                for n, (g, r) in enumerate(zip(gs, recv))]
        return [o[0][:, None] for o in outs], [o[1][:, None] for o in outs]

    def rep_pack(lg, lb, bf, scalar=0.0):
        rows = jnp.concatenate([lg, lb, jnp.pad(bf.reshape(1, -1), ((0, 0), (0, D - 2 * n_f))),
                                jnp.broadcast_to(jnp.asarray(scalar, F32), (1, D))])
        return jnp.pad(rows, ((0, 16 - rows.shape[0]), (0, 0)))

    SCALAR_ROW = 2 * DEPTH + 1

    part, got_parts = [None] * DEPTH, [None] * DEPTH
    d_ln_g, d_ln_b, d_bf = [None] * DEPTH, [None] * DEPTH, [None, None]
    for layer in reversed(range(DEPTH)):
        idx, xm = layer // 2, xb[layer]
        dz, dzb, dg, db = ln_grads
        d_ln_g[layer], d_ln_b[layer] = dg[0], db[0]
        if layer % 2 == 0:
            w_in, w_out = w_attn_in[idx], w_attn_out[idx]
            proj, cum4, o, og, lp = saved[layer][:5]
            dog = _matmul(dzb, w_out, trans_b=True, tm=512, tn=1024, name=f"a_dog{layer}")
            dwo = _matmul_tn(og, dzb, tm=512, tn=1024, tk=1024, name=f"a_dwo{layer}")
            g_out = _from_full(dwo, "attn_w_out", shapes["attn_w_out"])
            riders = [l for l in range(layer + 1, DEPTH) if got_parts[l] is None]
            early = [g_out] if layer == 0 else []
            host = _Exchange("scatter8", [g for l in riders for g in part[l]] + early)
            dq, dgate, dk, dv, dcum_q, dcum_k, *got = _flash_bwd(proj, cum4, o, dog, lp, tb=tb_bwd,
                                                                 name=f"a_bwd{layer}", host=host)
            for l in riders:
                got_parts[l], got = got[:len(part[l])], got[len(part[l]):]
            dcum_t = (dcum_q.transpose(0, 2, 1, 3) + dcum_k).reshape(N_HEADS, T)
            dcum_t = jnp.pad(dcum_t, ((0, LANES - N_HEADS), (0, 0)))
            df, dbf = _cumsum_bwd(dcum_t, proj, bf_rows[idx], tt=min(512, T), name=f"a_dcum{layer}")
            d_bf[idx] = dbf[0, :n_f]
            dproj = [dq, dk, dv, dgate, df]
            rep_host = _Exchange("gather8", [rep_pack(jnp.stack(d_ln_g), jnp.stack(d_ln_b),
                                                      jnp.stack(d_bf), loss_part)]) if layer == 0 else None
            dwi = _matmul_tn_parts(xm, dproj, tm=512, tk=1024, name=f"a_dwi{layer}", host=rep_host)
            if layer == 0:
                dwi, rep = dwi[:-1], dwi[-1]
            g_in = _split_columns(dwi, shapes["attn_w_in"][1], tr=256, name=f"a_split{layer}")
            if layer > 0:
                part[layer] = [g_in, g_out]
                ln_grads = _dx_ln_bwd(dproj, w_in, dz, *ln_below(layer - 1), tm=256,
                                      name=f"a_dx{layer}")
            else:
                half, narrow = reduce_pair([g_in], layer)
                dy, recv = _matmul(dproj, w_in, trans_b=True, tm=512, tn=1024, name=f"a_dx{layer}",
                                   add=dz, add_scale=ALPHA, host=_Exchange("scatter", narrow))
                last_parts = [_two_stage_parts(half[0], recv), _direct_parts(g_out, got[0])]
        else:
            w = w_rnn[idx]
            proj, hs, yr = saved[layer][:3]
            dyr = _matmul(dzb, w["rnn_w_out"], trans_b=True, tm=512, tn=1024, name=f"r_dy{layer}")
            dwo = _matmul_tn(yr, dzb, tm=512, tn=1024, tk=1024, name=f"r_dwo{layer}")
            dproj, dwa, dwi_, dsm = _rg_bwd(proj, hs, dyr, w["small"], w["rnn_w_a"], w["rnn_w_i"],
                                            tt=tt_rg, name=f"r_bwd{layer}")
            dwin = _matmul_tn(xm, dproj, tm=512, tn=2048, tk=1024, name=f"r_dwi{layer}")
            ln_grads = _dx_ln_bwd([dproj], w["rnn_w_in"], dz, *ln_below(layer - 1), tm=512,
                                  name=f"r_dx{layer}")
            full = dict(rnn_w_out=dwo, rnn_w_a=dwa, rnn_w_i=dwi_, rnn_conv_w=dsm[0:4],
                        rnn_conv_b=dsm[4], rnn_b_a=dsm[5], rnn_b_i=dsm[6], rnn_lambda=dsm[7])
            part[layer] = [_split_columns([dwin], shapes["rnn_w_in"][1], tr=256,
                                          name=f"r_split{layer}"),
                           _pack_grad_rows(full, shapes)]
    grad_x = dy[None]

    def grad_parts(layer, n):
        return last_parts[n] if layer == 0 else _direct_parts(part[layer][n], got_parts[layer][n])

    def update(k, n, wmv):
        layers = [2 * idx + (0 if k.startswith("attn") else 1) for idx in range(2)]
        return _adamw_shard([grad_parts(layer, n) for layer in layers], place, *wmv,
                            tr=_row_tile(wmv[0].shape[1], 256), name=f"adamw_{k}")

    shard_outs = [dict() for _ in range(4)]
    for k, n in (("attn_w_in", 0), ("attn_w_out", 1), ("rnn_w_in", 0)):
        for j, a in enumerate(update(k, n, (w_loc[k], m_loc[k], v_loc[k]))):
            shard_outs[j][k] = a
    rows_wmv = [jnp.stack([_pack_rows(d, idx, F32) for idx in range(2)]) for d in (w_loc, m_loc, v_loc)]
    for j, a in enumerate(update("rnn_rows", 1, rows_wmv)):
        shard_outs[j].update(_unpack_rows(a, shapes))
    g_sh, d_sh, nm_sh, nv_sh = shard_outs

    rg, rd, rm, rv = _adamw(rep.reshape(8, 16, D), rep_pack(ln_g, ln_b, attn_b_f),
                            rep_pack(m_ln_g, m_ln_b, m_attn_b_f),
                            rep_pack(v_ln_g, v_ln_b, v_attn_b_f), tr=16, name="adamw_rep")
    loss = rg[SCALAR_ROW, 0]

    def rep_unpack(a):
        return dict(ln_g=a[0:DEPTH], ln_b=a[DEPTH:2 * DEPTH],
                    attn_b_f=a[2 * DEPTH, :2 * n_f].reshape(2, n_f))

    order = ("ln_g", "ln_b", "attn_w_in", "attn_b_f", "attn_w_out", "rnn_w_in", "rnn_conv_w",
             "rnn_conv_b", "rnn_w_a", "rnn_b_a", "rnn_w_i", "rnn_b_i", "rnn_lambda", "rnn_w_out")
    outs = [loss, grad_x]
    for sh, rp in ((g_sh, rg), (d_sh, rd), (nm_sh, rm), (nv_sh, rv)):
        allp = {**sh, **rep_unpack(rp)}
        outs.extend(allp[k] for k in order)
    return tuple(outs)
```

```python
import jax
import jax.numpy as jnp
from jax import lax
from jax.experimental import pallas as pl
from jax.experimental.pallas import tpu as pltpu

F32 = jnp.float32
BF16 = jnp.bfloat16

DEPTH = 4
N_HEADS = 16
HEAD_DIM = 64
N_PAIRS = N_HEADS // 2
RNN_BLOCKS = 4
RNN_BLOCK_WIDTH = 256
CONV_WIDTH = 4
LRU_C = 8.0
ALPHA = (2.0 * DEPTH) ** 0.25
LN_EPS = 1e-5
ADAM_LR, ADAM_B1, ADAM_B2, ADAM_EPS, ADAM_WD, ADAM_STEP = 0.001, 0.9, 0.999, 1e-8, 0.01, 10

LANES = 128
SUBLANES = 8
VMEM_LIMIT = 48 * 1024 * 1024

MESH = pl.DeviceIdType.MESH
HBM_SPEC = pl.BlockSpec(memory_space=pltpu.HBM)


def _cparams(*sem):
    return pltpu.CompilerParams(dimension_semantics=sem, vmem_limit_bytes=VMEM_LIMIT)


def _sigmoid(x):
    return 1.0 / (1.0 + jnp.exp(-x))


def _softplus(x):
    return jnp.maximum(x, 0.0) + jnp.log(1.0 + jnp.exp(-jnp.abs(x)))


D2D_CHUNKS = 16
ICI_CHUNKS = 8


def _row_chunks(rows, dtype, k):
    unit = SUBLANES * (4 // jnp.dtype(dtype).itemsize)
    assert rows % unit == 0
    units = rows // unit
    k = max(1, min(k, units))
    base, rem = divmod(units, k)
    out, r = [], 0
    for i in range(k):
        n = (base + (1 if i < rem else 0)) * unit
        out.append((r, n))
        r += n
    return out


def _chunks(shape, dtype, k):
    if len(shape) == 2:
        return [(pl.ds(r0, n),) for r0, n in _row_chunks(shape[0], dtype, k)]
    per = max(1, k // shape[0])
    return [(l, pl.ds(r0, n)) for l in range(shape[0]) for r0, n in _row_chunks(shape[1], dtype, per)]


def _mesh_place():
    x, y, c = lax.axis_index("x"), lax.axis_index("y"), lax.axis_index("c")
    return x, y, c, 2 * x + y


def _chip_peer(x, y, c, d):
    px, py = x ^ (d >> 1), y ^ (d & 1)
    return (px, py, c), 2 * px + py


def _remote(src, dst, send_sem, recv_sem, dev):
    return pltpu.make_async_remote_copy(src_ref=src, dst_ref=dst, send_sem=send_sem,
                                        recv_sem=recv_sem, device_id=dev, device_id_type=MESH)


def _comm_call(body, name, ins, out_shapes, n_sems, aliases=None):
    n = len(ins)
    return pl.pallas_call(
        body, name=name,
        out_shape=out_shapes, in_specs=[HBM_SPEC] * n, out_specs=[HBM_SPEC] * n,
        input_output_aliases=aliases or {},
        scratch_shapes=[pltpu.SemaphoreType.DMA((n_sems, n)), pltpu.SemaphoreType.DMA((n_sems, n))],
    )(*ins)


class _Exchange:
    def __init__(self, kind, arrays):
        self.kind, self.arrays, self.n = kind, list(arrays), len(arrays)
        self.is_gather, self.all8 = kind.startswith("gather"), kind.endswith("8")
        k = ICI_CHUNKS // 4 if self.all8 else ICI_CHUNKS
        if self.is_gather:
            self.chunks = [_chunks(a.shape, a.dtype, k) for a in arrays]
            self.out_shapes = [jax.ShapeDtypeStruct((2, 4) + tuple(a.shape), a.dtype) for a in arrays]
        else:
            lead = 2 if self.all8 else 1
            self.chunks = [_chunks(a.shape[lead:], a.dtype, k) for a in arrays]
            self.out_shapes = [jax.ShapeDtypeStruct(a.shape, a.dtype) for a in arrays]
        self.peers = list(range(1, 8 if self.all8 else 4))
        n_sems = len(self.peers) + 1
        self.sem_shapes = [pltpu.SemaphoreType.DMA((n_sems, self.n)),
                           pltpu.SemaphoreType.DMA((n_sems, self.n))]

    def _peer(self, x, y, c, me, p):
        a, d = p // 4, p % 4
        px, py = x ^ (d >> 1), y ^ (d & 1)
        pc = 1 - c if a else c
        if self.all8:
            return (px, py, pc), (pc, 2 * px + py), (c, me)
        return (px, py, pc), (2 * px + py,), (me,)

    def _blocks(self, srcs, outs, o, c, me, theirs, mine):
        if self.kind == "gather":
            return srcs[o], outs[o].at[(c,) + mine], outs[o].at[(c,) + theirs]
        if self.kind == "gather8":
            return srcs[o], outs[o].at[mine], outs[o].at[theirs]
        return srcs[o].at[theirs], outs[o].at[mine], outs[o].at[theirs]

    def start(self, srcs, outs, send_sems, recv_sems):
        x, y, c, me = _mesh_place()
        if self.is_gather:
            for o in range(self.n):
                for idx in self.chunks[o]:
                    pltpu.make_async_copy(srcs[o].at[idx], outs[o].at[(c, me) + idx],
                                          send_sems.at[0, o]).start()
        for p in self.peers:
            dev, theirs, mine = self._peer(x, y, c, me, p)
            for o in range(self.n):
                src, dst, _ = self._blocks(srcs, outs, o, c, me, theirs, mine)
                for idx in self.chunks[o]:
                    _remote(src.at[idx], dst.at[idx], send_sems.at[p, o], recv_sems.at[p, o],
                            dev).start()

    def wait(self, srcs, outs, send_sems, recv_sems):
        x, y, c, me = _mesh_place()
        for wait_recv in (True, False):
            for p in self.peers:
                dev, theirs, mine = self._peer(x, y, c, me, p)
                for o in range(self.n):
                    src, _, land = self._blocks(srcs, outs, o, c, me, theirs, mine)
                    cp = _remote(src, land, send_sems.at[p, o], recv_sems.at[p, o], dev)
                    cp.wait_recv() if wait_recv else cp.wait_send()
        if self.is_gather:
            for o in range(self.n):
                pltpu.make_async_copy(srcs[o], outs[o].at[c, me], send_sems.at[0, o]).wait()


def _run_exchange(ex, name):
    n = ex.n

    def body(*refs):
        srcs, outs, send_sems, recv_sems = refs[:n], refs[n:2 * n], refs[2 * n], refs[2 * n + 1]
        ex.start(srcs, outs, send_sems, recv_sems)
        ex.wait(srcs, outs, send_sems, recv_sems)

    return _comm_call(body, name, ex.arrays, ex.out_shapes, len(ex.peers) + 1)


def _ag_c(bufs, name):
    n = len(bufs)
    chunks = [_chunks(b.shape[2:], b.dtype, D2D_CHUNKS // 4) for b in bufs]

    def body(*refs):
        srcs, outs, send_sems, recv_sems = refs[:n], refs[n:2 * n], refs[2 * n], refs[2 * n + 1]
        x, y, c, _ = _mesh_place()
        sib = (x, y, 1 - c)
        for o in range(n):
            for k in range(4):
                for idx in chunks[o]:
                    _remote(srcs[o].at[(c, k) + idx], outs[o].at[(c, k) + idx],
                            send_sems.at[0, o], recv_sems.at[0, o], sib).start()
        for o in range(n):
            _remote(srcs[o].at[c], outs[o].at[1 - c], send_sems.at[0, o], recv_sems.at[0, o],
                    sib).wait_recv()
        for o in range(n):
            _remote(srcs[o].at[c], outs[o].at[1 - c], send_sems.at[0, o], recv_sems.at[0, o],
                    sib).wait_send()

    shapes = [jax.ShapeDtypeStruct(b.shape, b.dtype) for b in bufs]
    return _comm_call(body, name, bufs, shapes, 1, aliases={i: i for i in range(n)})


def _rs_c(gs, name):
    n = len(gs)
    chunks = [_chunks(g.shape[2:], g.dtype, max(1, D2D_CHUNKS // g.shape[1])) for g in gs]

    def body(*refs):
        srcs, outs, send_sems, recv_sems = refs[:n], refs[n:2 * n], refs[2 * n], refs[2 * n + 1]
        x, y, c, _ = _mesh_place()
        sib = (x, y, 1 - c)
        for o in range(n):
            for k in range(gs[o].shape[1]):
                for idx in chunks[o]:
                    _remote(srcs[o].at[(1 - c, k) + idx], outs[o].at[(k,) + idx],
                            send_sems.at[0, o], recv_sems.at[0, o], sib).start()
        for o in range(n):
            _remote(srcs[o].at[1 - c], outs[o], send_sems.at[0, o], recv_sems.at[0, o],
                    sib).wait_recv()
        for o in range(n):
            _remote(srcs[o].at[1 - c], outs[o], send_sems.at[0, o], recv_sems.at[0, o],
                    sib).wait_send()

    shapes = [jax.ShapeDtypeStruct(g.shape[1:], g.dtype) for g in gs]
    return _comm_call(body, name, gs, shapes, 1)


def _matmul(a, b, *, trans_b, tm, tn, name, add=None, add_scale=1.0, host=None):
    a_parts = list(a) if isinstance(a, (list, tuple)) else [a]
    M, K = a_parts[0].shape[0], sum(p.shape[1] for p in a_parts)
    N = b.shape[0] if trans_b else b.shape[1]
    tm, tn = min(tm, M), min(tn, N)
    assert M % tm == 0 and N % tn == 0
    dn = (((1,), (1,)), ((), ())) if trans_b else (((1,), (0,)), ((), ()))
    na = len(a_parts)

    def body(*refs):
        a_refs, b_ref, o_ref = refs[:na], refs[na], refs[-1]
        av = [r[...].astype(BF16) for r in a_refs]
        av = av[0] if na == 1 else jnp.concatenate(av, axis=1)
        r = lax.dot_general(av, b_ref[...].astype(BF16), dn, preferred_element_type=F32)
        if add is not None:
            r = r + add_scale * refs[na + 1][...]
        o_ref[...] = r

    b_spec = (pl.BlockSpec((tn, K), lambda j, i: (j, 0)) if trans_b
              else pl.BlockSpec((K, tn), lambda j, i: (0, j)))
    in_specs = [pl.BlockSpec((tm, p.shape[1]), lambda j, i: (i, 0)) for p in a_parts] + [b_spec]
    args = a_parts + [b]
    if add is not None:
        in_specs.append(pl.BlockSpec((tm, tn), lambda j, i: (i, j)))
        args.append(add)
    grid = (N // tn, M // tm)
    x_in, x_out, x_shapes, x_scratch, x_args = _host_specs(host)
    body = _hosted(body, len(args), 1, 0, host, grid)
    outs = pl.pallas_call(
        body, name=name, grid=grid,
        in_specs=in_specs + x_in,
        out_specs=[pl.BlockSpec((tm, tn), lambda j, i: (i, j))] + x_out,
        out_shape=[jax.ShapeDtypeStruct((M, N), F32)] + x_shapes,
        scratch_shapes=x_scratch,
        compiler_params=_cparams(*(("arbitrary",) * 2 if host else ("parallel",) * 2)),
    )(*args, *x_args)
    return outs if host else outs[0]


def _matmul_tn(a, b, *, tm, tn, tk, name):
    T, M = a.shape
    N = b.shape[1]
    tm, tn, tk = min(tm, M), min(tn, N), min(tk, T)
    assert M % tm == 0 and N % tn == 0 and T % tk == 0

    def body(a_ref, b_ref, o_ref):
        @pl.when(pl.program_id(2) == 0)
        def _():
            o_ref[...] = jnp.zeros_like(o_ref)

        o_ref[...] += lax.dot_general(a_ref[...].astype(BF16), b_ref[...].astype(BF16),
                                      (((0,), (0,)), ((), ())), preferred_element_type=F32)

    return pl.pallas_call(
        body, name=name, grid=(M // tm, N // tn, T // tk),
        in_specs=[pl.BlockSpec((tk, tm), lambda i, j, k: (k, i)),
                  pl.BlockSpec((tk, tn), lambda i, j, k: (k, j))],
        out_specs=pl.BlockSpec((tm, tn), lambda i, j, k: (i, j)),
        out_shape=jax.ShapeDtypeStruct((M, N), F32),
        compiler_params=_cparams("parallel", "parallel", "arbitrary"),
    )(a, b)


def _matmul_tn_parts(a, parts, *, tm, tk, name, host=None):
    T, M = a.shape
    tm, tk = min(tm, M), min(tk, T)
    assert M % tm == 0 and T % tk == 0
    n = len(parts)
    grid = (M // tm, T // tk)
    x_in, x_out, x_shapes, x_scratch, x_args = _host_specs(host)

    def body(*refs):
        a_ref, b_refs, o_refs = refs[0], refs[1:1 + n], refs[1 + n:]
        av = a_ref[...].astype(BF16)
        for b_ref, o_ref in zip(b_refs, o_refs):
            @pl.when(pl.program_id(1) == 0)
            def _(o_ref=o_ref):
                o_ref[...] = jnp.zeros_like(o_ref)

            o_ref[...] += lax.dot_general(av, b_ref[...].astype(BF16), (((0,), (0,)), ((), ())),
                                          preferred_element_type=F32)

    body = _hosted(body, 1 + n, n, 0, host, grid)
    return pl.pallas_call(
        body, name=name, grid=grid,
        in_specs=[pl.BlockSpec((tk, tm), lambda i, k: (k, i))]
        + [pl.BlockSpec((tk, p.shape[1]), lambda i, k: (k, 0)) for p in parts] + x_in,
        out_specs=[pl.BlockSpec((tm, p.shape[1]), lambda i, k: (i, 0)) for p in parts] + x_out,
        out_shape=[jax.ShapeDtypeStruct((M, p.shape[1]), F32) for p in parts] + x_shapes,
        scratch_shapes=x_scratch,
        compiler_params=_cparams("arbitrary" if host else "parallel", "arbitrary"),
    )(a, *parts, *x_args)


def _head_masks(rows):
    lane = lax.broadcasted_iota(jnp.int32, (rows, LANES), 1)
    return lane < HEAD_DIM, lane >= HEAD_DIM


def _causal(i_q, i_k, tq, tk):
    row = i_q * tq + lax.broadcasted_iota(jnp.int32, (tq, tk), 0)
    col = i_k * tk + lax.broadcasted_iota(jnp.int32, (tq, tk), 1)
    return row >= col


def _hosted(body, n_in, n_out, n_scratch, host, grid):
    if host is None:
        return body
    nx = host.n

    def wrapped(*refs):
        ins, xsrcs = refs[:n_in], refs[n_in:n_in + nx]
        outs = refs[n_in + nx:n_in + nx + n_out]
        xouts = refs[n_in + nx + n_out:n_in + 2 * nx + n_out]
        scratch = refs[n_in + 2 * nx + n_out:n_in + 2 * nx + n_out + n_scratch]
        xsems = refs[n_in + 2 * nx + n_out + n_scratch:]
        step = pl.program_id(0) * grid[1] + pl.program_id(1)

        @pl.when(step == 0)
        def _():
            host.start(xsrcs, xouts, *xsems)

        body(*ins, *outs, *scratch)

        @pl.when(step == grid[0] * grid[1] - 1)
        def _():
            host.wait(xsrcs, xouts, *xsems)

    return wrapped


def _host_specs(host):
    if host is None:
        return [], [], [], [], []
    return ([HBM_SPEC] * host.n, [HBM_SPEC] * host.n, host.out_shapes, host.sem_shapes, host.arrays)


def _flash_fwd(proj, cum4, *, tb, name, host=None):
    T = proj.shape[0]
    D = N_HEADS * HEAD_DIM
    nb = T // tb
    cb = D // LANES
    x_in, x_out, x_shapes, x_scratch, x_args = _host_specs(host)

    def body(q_ref, k_ref, v_ref, g_ref, cum_ref, o_ref, og_ref, lp_ref, kb_ref, vb_ref):
        i = pl.program_id(1)

        @pl.when(i == 0)
        def _():
            kb_ref[...] = k_ref[...].astype(BF16)
            vb_ref[...] = v_ref[...].astype(BF16)

        q = q_ref[...] * (HEAD_DIM ** -0.5)
        masks = _head_masks(tb)
        qh = [jnp.where(masks[h], q, 0.0).astype(BF16) for h in range(2)]
        cref = [cum_ref[0, h, pl.ds(i, 1), :][:, 0:1] for h in range(2)]

        def tile(kbi, r0, nr, nk, carry, first_row):
            k0 = pl.multiple_of(kbi * tb, tb)
            kblk = kb_ref[pl.ds(k0, nk), :]
            vblk = vb_ref[pl.ds(k0, nk), :]
            new = []
            for h in range(2):
                m, l, acc = carry[h]
                s = lax.dot_general(qh[h][r0:r0 + nr], kblk, (((1,), (1,)), ((), ())),
                                    preferred_element_type=F32)
                s = s + (cref[h] - cum_ref[0, h, pl.ds(kbi, 1), :][:, 0:nk])
                if first_row is not None:
                    row = first_row + lax.broadcasted_iota(jnp.int32, (nr, nk), 0)
                    s = jnp.where(row >= lax.broadcasted_iota(jnp.int32, (nr, nk), 1), s, -jnp.inf)
                m_new = jnp.maximum(m, jnp.max(s, axis=-1, keepdims=True))
                alpha = jnp.exp(m - m_new)
                p = jnp.exp(s - m_new)
                l = alpha * l + jnp.sum(p, axis=-1, keepdims=True)
                acc = alpha * acc + jnp.dot(p.astype(BF16), vblk, preferred_element_type=F32)
                new.append((m_new, l, acc))
            return tuple(new)

        init1 = (jnp.full((tb, 1), -jnp.inf, F32), jnp.zeros((tb, 1), F32),
                 jnp.zeros((tb, LANES), F32))
        carry = lax.fori_loop(0, i, lambda kbi, c: tile(kbi, 0, tb, tb, c, None), (init1, init1))
        hb = tb // 2
        upper = tile(i, 0, hb, hb, tuple(tuple(a[:hb] for a in c) for c in carry), 0)
        lower = tile(i, hb, hb, tb, tuple(tuple(a[hb:] for a in c) for c in carry), hb)
        outs = []
        for h, (m, l, acc) in enumerate(
                tuple(jnp.concatenate([u, w], axis=0) for u, w in zip(upper[h], lower[h]))
                for h in range(2)):
            outs.append(acc / l)
            lp_ref[h] = jnp.broadcast_to(m + jnp.log(l) - cref[h], (tb, LANES))
        o = jnp.where(masks[0], outs[0], outs[1])
        o_ref[...] = o
        gate = g_ref[...]
        og_ref[...] = (o * (gate * _sigmoid(gate))).astype(BF16)

    body = _hosted(body, 5, 3, 2, host, (N_PAIRS, nb))
    return pl.pallas_call(
        body, name=name, grid=(N_PAIRS, nb),
        in_specs=[pl.BlockSpec((tb, LANES), lambda j, i: (i, j)),
                  pl.BlockSpec((T, LANES), lambda j, i: (0, cb + j)),
                  pl.BlockSpec((T, LANES), lambda j, i: (0, 2 * cb + j)),
                  pl.BlockSpec((tb, LANES), lambda j, i: (i, 3 * cb + j)),
                  pl.BlockSpec((1, 2, nb, tb), lambda j, i: (j, 0, 0, 0))] + x_in,
        out_specs=[pl.BlockSpec((tb, LANES), lambda j, i: (i, j)),
                   pl.BlockSpec((tb, LANES), lambda j, i: (i, j)),
                   pl.BlockSpec((2, tb, LANES), lambda j, i: (j, i, 0))] + x_out,
        out_shape=[jax.ShapeDtypeStruct((T, D), F32), jax.ShapeDtypeStruct((T, D), BF16),
                   jax.ShapeDtypeStruct((N_HEADS, T, LANES), F32)] + x_shapes,
        scratch_shapes=[pltpu.VMEM((T, LANES), BF16), pltpu.VMEM((T, LANES), BF16)] + x_scratch,
        compiler_params=_cparams("arbitrary", "arbitrary"),
    )(proj, proj, proj, proj, cum4, *x_args)


def _flash_bwd(proj, cum4, o, dog, lp, *, tb, name, host=None):
    T = proj.shape[0]
    D = N_HEADS * HEAD_DIM
    nb = T // tb
    cb = D // LANES
    x_in, x_out, x_shapes, x_scratch, x_args = _host_specs(host)

    def body(q_ref, k_ref, v_ref, g_ref, cum_ref, o_ref, dog_ref, lp_ref,
             dq_ref, dg_ref, dk_ref, dv_ref, dcq_ref, dck_ref,
             kb_ref, vb_ref, dka_ref, dva_ref, dca_ref):
        i = pl.program_id(1)

        @pl.when(i == 0)
        def _():
            kb_ref[...] = k_ref[...].astype(BF16)
            vb_ref[...] = v_ref[...].astype(BF16)
            dka_ref[...] = jnp.zeros_like(dka_ref)
            dva_ref[...] = jnp.zeros_like(dva_ref)
            dca_ref[...] = jnp.zeros_like(dca_ref)

        gate = g_ref[...]
        sg = _sigmoid(gate)
        o = o_ref[...]
        dog = dog_ref[...]
        do = dog * (gate * sg)
        dg_ref[...] = (dog * o * (sg * (1.0 + gate * (1.0 - sg)))).astype(BF16)
        q = q_ref[...] * (HEAD_DIM ** -0.5)
        masks = _head_masks(tb)
        qh = [jnp.where(masks[h], q, 0.0).astype(BF16) for h in range(2)]
        doh = [jnp.where(masks[h], do, 0.0).astype(BF16) for h in range(2)]
        delta = [jnp.sum(jnp.where(masks[h], do * o, 0.0), axis=-1, keepdims=True) for h in range(2)]
        lph = [lp_ref[h][:, 0:1] for h in range(2)]

        def step(kbi, carry, masked):
            k0 = pl.multiple_of(kbi * tb, tb)
            kblk = kb_ref[pl.ds(k0, tb), :]
            vblk = vb_ref[pl.ds(k0, tb), :]
            new, dk, dv = [], None, None
            for h in range(2):
                acc, rs = carry[h]
                s = lax.dot_general(qh[h], kblk, (((1,), (1,)), ((), ())), preferred_element_type=F32)
                p = jnp.exp(s - cum_ref[0, h, pl.ds(kbi, 1), :] - lph[h])
                if masked:
                    p = jnp.where(_causal(i, kbi, tb, tb), p, 0.0)
                dp = lax.dot_general(doh[h], vblk, (((1,), (1,)), ((), ())),
                                     preferred_element_type=F32)
                ds = p * (dp - delta[h])
                pb, dsb = p.astype(BF16), ds.astype(BF16)
                dv_h = lax.dot_general(pb, doh[h], (((0,), (0,)), ((), ())),
                                       preferred_element_type=F32)
                dk_h = lax.dot_general(dsb, qh[h], (((0,), (0,)), ((), ())),
                                       preferred_element_type=F32)
                dv = dv_h if dv is None else dv + dv_h
                dk = dk_h if dk is None else dk + dk_h
                dca_ref[h, pl.ds(kbi, 1), :] -= jnp.sum(ds, axis=0, keepdims=True)
                new.append((acc + jnp.dot(dsb, kblk, preferred_element_type=F32),
                            rs + jnp.sum(ds, axis=-1, keepdims=True)))
            dka_ref[pl.ds(k0, tb), :] += dk
            dva_ref[pl.ds(k0, tb), :] += dv
            return tuple(new)

        init1 = (jnp.zeros((tb, LANES), F32), jnp.zeros((tb, 1), F32))
        carry = lax.fori_loop(0, i, lambda kbi, c: step(kbi, c, False), (init1, init1))
        dqs = []
        for h, (acc, rs) in enumerate(step(i, carry, True)):
            dqs.append(acc)
            dcq_ref[0, 0, pl.ds(h, 1), :] = jnp.broadcast_to(rs, (tb, LANES)).T[0:1, :]
        dq_ref[...] = (jnp.where(masks[0], dqs[0], dqs[1]) * (HEAD_DIM ** -0.5)).astype(BF16)

        @pl.when(i == nb - 1)
        def _():
            dk_ref[...] = dka_ref[...].astype(BF16)
            dv_ref[...] = dva_ref[...].astype(BF16)
            dck_ref[0] = dca_ref[...]

    blk = pl.BlockSpec((tb, LANES), lambda j, i: (i, j))
    full = pl.BlockSpec((T, LANES), lambda j, i: (0, j))
    body = _hosted(body, 8, 6, 5, host, (N_PAIRS, nb))
    return pl.pallas_call(
        body, name=name, grid=(N_PAIRS, nb),
        in_specs=[blk,
                  pl.BlockSpec((T, LANES), lambda j, i: (0, cb + j)),
                  pl.BlockSpec((T, LANES), lambda j, i: (0, 2 * cb + j)),
                  pl.BlockSpec((tb, LANES), lambda j, i: (i, 3 * cb + j)),
                  pl.BlockSpec((1, 2, nb, tb), lambda j, i: (j, 0, 0, 0)),
                  blk, blk, pl.BlockSpec((2, tb, LANES), lambda j, i: (j, i, 0))] + x_in,
        out_specs=[blk, blk, full, full,
                   pl.BlockSpec((1, 1, 2, tb), lambda j, i: (j, i, 0, 0)),
                   pl.BlockSpec((1, 2, nb, tb), lambda j, i: (j, 0, 0, 0))] + x_out,
        out_shape=[jax.ShapeDtypeStruct((T, D), BF16)] * 4
        + [jax.ShapeDtypeStruct((N_PAIRS, nb, 2, tb), F32),
           jax.ShapeDtypeStruct((N_PAIRS, 2, nb, tb), F32)] + x_shapes,
        scratch_shapes=[pltpu.VMEM((T, LANES), BF16), pltpu.VMEM((T, LANES), BF16),
                        pltpu.VMEM((T, LANES), F32), pltpu.VMEM((T, LANES), F32),
                        pltpu.VMEM((2, nb, tb), F32)] + x_scratch,
        compiler_params=_cparams("arbitrary", "arbitrary"),
    )(proj, proj, proj, proj, cum4, o, dog, lp, *x_args)


def _cumsum_fwd(proj, bf_row, *, tt, name):
    T = proj.shape[0]
    cb = (proj.shape[1] - LANES) // LANES

    def body(f_ref, b_ref, out_ref, carry_ref):
        i = pl.program_id(0)

        @pl.when(i == 0)
        def _():
            carry_ref[...] = jnp.zeros_like(carry_ref)

        ls = -_softplus(-(f_ref[...] + b_ref[...]))
        tri = (lax.broadcasted_iota(jnp.int32, (tt, tt), 0)
               >= lax.broadcasted_iota(jnp.int32, (tt, tt), 1)).astype(F32)
        cum = jnp.dot(tri, ls, preferred_element_type=F32,
                      precision=lax.Precision.HIGHEST) + carry_ref[...]
        carry_ref[...] = cum[tt - 1:tt, :]
        out_ref[...] = cum.T

    return pl.pallas_call(
        body, name=name, grid=(T // tt,),
        in_specs=[pl.BlockSpec((tt, LANES), lambda i: (i, cb)),
                  pl.BlockSpec((1, LANES), lambda i: (0, 0))],
        out_specs=pl.BlockSpec((LANES, tt), lambda i: (0, i)),
        out_shape=jax.ShapeDtypeStruct((LANES, T), F32),
        scratch_shapes=[pltpu.VMEM((1, LANES), F32)],
        compiler_params=_cparams("arbitrary"),
    )(proj, bf_row)


def _cumsum_bwd(dcum_t, proj, bf_row, *, tt, name):
    T = proj.shape[0]
    cb = (proj.shape[1] - LANES) // LANES
    nt = T // tt

    def body(dc_ref, f_ref, b_ref, df_ref, db_ref, carry_ref):
        i = pl.program_id(0)

        @pl.when(i == 0)
        def _():
            carry_ref[...] = jnp.zeros_like(carry_ref)
            db_ref[...] = jnp.zeros_like(db_ref)

        dc = dc_ref[...].T
        tri = (lax.broadcasted_iota(jnp.int32, (tt, tt), 0)
               <= lax.broadcasted_iota(jnp.int32, (tt, tt), 1)).astype(F32)
        rev = jnp.dot(tri, dc, preferred_element_type=F32,
                      precision=lax.Precision.HIGHEST) + carry_ref[...]
        carry_ref[...] = rev[0:1, :]
        df = rev * _sigmoid(-(f_ref[...] + b_ref[...]))
        df_ref[...] = df.astype(BF16)
        db_ref[...] += jnp.sum(df, axis=0, keepdims=True)

    return pl.pallas_call(
        body, name=name, grid=(nt,),
        in_specs=[pl.BlockSpec((LANES, tt), lambda i: (0, nt - 1 - i)),
                  pl.BlockSpec((tt, LANES), lambda i: (nt - 1 - i, cb)),
                  pl.BlockSpec((1, LANES), lambda i: (0, 0))],
        out_specs=[pl.BlockSpec((tt, LANES), lambda i: (nt - 1 - i, 0)),
                   pl.BlockSpec((1, LANES), lambda i: (0, 0))],
        out_shape=[jax.ShapeDtypeStruct((T, LANES), BF16), jax.ShapeDtypeStruct((1, LANES), F32)],
        scratch_shapes=[pltpu.VMEM((1, LANES), F32)],
        compiler_params=_cparams("arbitrary"),
    )(dcum_t, proj, bf_row)


def _rows_down(x, before, sh):
    if sh == 0:
        return x
    rolled = pltpu.roll(x, sh, axis=0)
    row = lax.broadcasted_iota(jnp.int32, (SUBLANES, x.shape[1]), 0)
    head = jnp.where(row < sh, pltpu.roll(before, sh, axis=0), rolled[:SUBLANES])
    return jnp.concatenate([head, rolled[SUBLANES:]], axis=0)


def _rows_up(x, after, sh):
    if sh == 0:
        return x
    tt = x.shape[0]
    rolled = pltpu.roll(x, tt - sh, axis=0)
    row = lax.broadcasted_iota(jnp.int32, (SUBLANES, x.shape[1]), 0)
    tail = jnp.where(row >= SUBLANES - sh, pltpu.roll(after, SUBLANES - sh, axis=0),
                     rolled[tt - SUBLANES:])
    return jnp.concatenate([rolled[:tt - SUBLANES], tail], axis=0)


def _shift_rows(x, sh, fill, up):
    tt, D = x.shape
    if sh % SUBLANES == 0:
        pad = jnp.full((sh, D), fill, x.dtype)
        return jnp.concatenate([x[sh:], pad] if up else [pad, x[:tt - sh]], axis=0)
    edge = jnp.full((SUBLANES, D), fill, x.dtype)
    return _rows_up(x, edge, sh) if up else _rows_down(x, edge, sh)


def _linear_scan(a, b, carry, up=False):
    sh = 1
    while sh < a.shape[0]:
        b = b + a * _shift_rows(b, sh, 0.0, up)
        a = a * _shift_rows(a, sh, 1.0, up)
        sh *= 2
    return a * carry + b


def _rg_gates(u0, before, small_ref, wa_ref, wi_ref):
    taps = [_rows_down(u0, before, CONV_WIDTH - 1 - tap) for tap in range(CONV_WIDTH)]
    u = small_ref[4:5, :]
    for tap in range(CONV_WIDTH):
        u = u + taps[tap] * small_ref[tap:tap + 1, :]
    pa, pi = [], []
    for n in range(RNN_BLOCKS):
        ub = u[:, n * RNN_BLOCK_WIDTH:(n + 1) * RNN_BLOCK_WIDTH].astype(BF16)
        pa.append(jnp.dot(ub, wa_ref[n], preferred_element_type=F32))
        pi.append(jnp.dot(ub, wi_ref[n], preferred_element_type=F32))
    r = _sigmoid(jnp.concatenate(pa, axis=-1) + small_ref[5:6, :])
    ig = _sigmoid(jnp.concatenate(pi, axis=-1) + small_ref[6:7, :])
    spl = _softplus(-small_ref[7:8, :])
    log_a = (-LRU_C) * r * spl
    a = jnp.exp(log_a)
    s2 = jnp.tanh(-log_a) * (a * a + 1.0)
    inv_s = lax.rsqrt(s2)
    s = jnp.where(s2 > 0.0, s2 * inv_s, 0.0)
    return u, taps, r, ig, spl, a, s, inv_s


def _rg_fwd(proj, small, wa, wi, *, tt, name):
    T = proj.shape[0]
    D = RNN_BLOCKS * RNN_BLOCK_WIDTH
    hb = tt // SUBLANES

    def body(u0_ref, halo_ref, g_ref, small_ref, wa_ref, wi_ref, h_ref, y_ref, carry_ref):
        i = pl.program_id(0)

        @pl.when(i == 0)
        def _():
            carry_ref[...] = jnp.zeros_like(carry_ref)

        before = jnp.where(i == 0, 0.0, halo_ref[...])
        u, _, r, ig, spl, a, s, _ = _rg_gates(u0_ref[...], before, small_ref, wa_ref, wi_ref)
        h = _linear_scan(a, s * (ig * u), carry_ref[...])
        carry_ref[...] = h[tt - 1:tt]
        h_ref[...] = h
        gate = g_ref[...]
        y_ref[...] = (h * (gate * _sigmoid(gate))).astype(BF16)

    return pl.pallas_call(
        body, name=name, grid=(T // tt,),
        in_specs=[pl.BlockSpec((tt, D), lambda i: (i, 0)),
                  pl.BlockSpec((SUBLANES, D), lambda i: (jnp.maximum(i * hb - 1, 0), 0)),
                  pl.BlockSpec((tt, D), lambda i: (i, 1)),
                  pl.BlockSpec((SUBLANES, D), lambda i: (0, 0)),
                  pl.BlockSpec((RNN_BLOCKS, RNN_BLOCK_WIDTH, RNN_BLOCK_WIDTH), lambda i: (0, 0, 0)),
                  pl.BlockSpec((RNN_BLOCKS, RNN_BLOCK_WIDTH, RNN_BLOCK_WIDTH), lambda i: (0, 0, 0))],
        out_specs=[pl.BlockSpec((tt, D), lambda i: (i, 0)), pl.BlockSpec((tt, D), lambda i: (i, 0))],
        out_shape=[jax.ShapeDtypeStruct((T, D), F32), jax.ShapeDtypeStruct((T, D), BF16)],
        scratch_shapes=[pltpu.VMEM((1, D), F32)],
        compiler_params=_cparams("arbitrary"),
    )(proj, proj, proj, small, wa, wi)


def _rg_bwd(proj, hs, dy, small, wa, wi, *, tt, name):
    T = proj.shape[0]
    D = RNN_BLOCKS * RNN_BLOCK_WIDTH
    W = RNN_BLOCK_WIDTH
    hb = tt // SUBLANES
    nt = T // tt

    def body(u0_ref, uhalo_ref, g_ref, h_ref, hhalo_ref, dy_ref, small_ref, wa_ref, wi_ref,
             dp_ref, dwa_ref, dwi_ref, ds_ref, dunext_ref, carry_ref):
        i = pl.program_id(0)
        first_chunk = i == nt - 1

        @pl.when(i == 0)
        def _():
            carry_ref[...] = jnp.zeros_like(carry_ref)
            dunext_ref[...] = jnp.zeros_like(dunext_ref)
            dwa_ref[...] = jnp.zeros_like(dwa_ref)
            dwi_ref[...] = jnp.zeros_like(dwi_ref)
            ds_ref[...] = jnp.zeros_like(ds_ref)

        u_before = jnp.where(first_chunk, 0.0, uhalo_ref[...])
        h_before = jnp.where(first_chunk, 0.0, hhalo_ref[...])
        u, taps, r, ig, spl, a, s, inv_s = _rg_gates(u0_ref[...], u_before, small_ref, wa_ref,
                                                     wi_ref)
        gate = g_ref[...]
        sg = _sigmoid(gate)
        dy = dy_ref[...]
        dp_ref[:, D:] = (dy * h_ref[...] * (sg * (1.0 + gate * (1.0 - sg)))).astype(BF16)
        dy_h = dy * (gate * sg)
        carry = carry_ref[...]
        x = _linear_scan(a, a * dy_h, carry, up=True)
        carry_ref[...] = x[0:1]
        g = dy_h + _rows_up(x, jnp.broadcast_to(carry, (SUBLANES, D)), 1)
        h_prev = _rows_down(h_ref[...], h_before, 1)
        iu = ig * u
        d_iu = g * s
        dlog_a = (g * h_prev) * a - (g * iu) * (a * a) * inv_s
        dpre_a = (dlog_a * ((-LRU_C) * spl)) * r * (1.0 - r)
        dpre_i = (d_iu * u) * ig * (1.0 - ig)
        dlam = jnp.sum(dlog_a * r, axis=0, keepdims=True) * (LRU_C * _sigmoid(-small_ref[7:8, :]))
        du_parts = []
        for n in range(RNN_BLOCKS):
            sl = slice(n * W, (n + 1) * W)
            ub = u[:, sl].astype(BF16)
            da_n = dpre_a[:, sl].astype(BF16)
            di_n = dpre_i[:, sl].astype(BF16)
            dwa_ref[n] += lax.dot_general(ub, da_n, (((0,), (0,)), ((), ())),
                                          preferred_element_type=F32)
            dwi_ref[n] += lax.dot_general(ub, di_n, (((0,), (0,)), ((), ())),
                                          preferred_element_type=F32)
            du_parts.append(
                lax.dot_general(da_n, wa_ref[n], (((1,), (1,)), ((), ())), preferred_element_type=F32)
                + lax.dot_general(di_n, wi_ref[n], (((1,), (1,)), ((), ())), preferred_element_type=F32))
        du = d_iu * ig + jnp.concatenate(du_parts, axis=-1)
        for tap in range(CONV_WIDTH):
            ds_ref[tap:tap + 1, :] += jnp.sum(du * taps[tap], axis=0, keepdims=True)
        ds_ref[4:5, :] += jnp.sum(du, axis=0, keepdims=True)
        ds_ref[5:6, :] += jnp.sum(dpre_a, axis=0, keepdims=True)
        ds_ref[6:7, :] += jnp.sum(dpre_i, axis=0, keepdims=True)
        ds_ref[7:8, :] += dlam
        du_after = dunext_ref[...]
        du0 = jnp.zeros((tt, D), F32)
        for tap in range(CONV_WIDTH):
            du0 = du0 + _rows_up(du, du_after, CONV_WIDTH - 1 - tap) * small_ref[tap:tap + 1, :]
        dp_ref[:, :D] = du0.astype(BF16)
        dunext_ref[...] = du[0:SUBLANES, :]

    rev = lambda i: nt - 1 - i
    wspec = pl.BlockSpec((RNN_BLOCKS, W, W), lambda i: (0, 0, 0))
    return pl.pallas_call(
        body, name=name, grid=(nt,),
        in_specs=[pl.BlockSpec((tt, D), lambda i: (rev(i), 0)),
                  pl.BlockSpec((SUBLANES, D), lambda i: (jnp.maximum(rev(i) * hb - 1, 0), 0)),
                  pl.BlockSpec((tt, D), lambda i: (rev(i), 1)),
                  pl.BlockSpec((tt, D), lambda i: (rev(i), 0)),
                  pl.BlockSpec((SUBLANES, D), lambda i: (jnp.maximum(rev(i) * hb - 1, 0), 0)),
                  pl.BlockSpec((tt, D), lambda i: (rev(i), 0)),
                  pl.BlockSpec((SUBLANES, D), lambda i: (0, 0)),
                  wspec, wspec],
        out_specs=[pl.BlockSpec((tt, 2 * D), lambda i: (rev(i), 0)),
                   wspec, wspec, pl.BlockSpec((SUBLANES, D), lambda i: (0, 0))],
        out_shape=[jax.ShapeDtypeStruct((T, 2 * D), BF16),
                   jax.ShapeDtypeStruct((RNN_BLOCKS, W, W), F32),
                   jax.ShapeDtypeStruct((RNN_BLOCKS, W, W), F32),
                   jax.ShapeDtypeStruct((SUBLANES, D), F32)],
        scratch_shapes=[pltpu.VMEM((SUBLANES, D), F32), pltpu.VMEM((1, D), F32)],
        compiler_params=_cparams("arbitrary"),
    )(proj, proj, proj, hs, hs, dy, small, wa, wi)


def _out_ln(a, w, x, g, b, *, tt, name):
    T, D = x.shape
    K = a.shape[1]

    def body(a_ref, w_ref, x_ref, g_ref, b_ref, y_ref, yb_ref, zh_ref, rs_ref):
        h = jnp.dot(a_ref[...].astype(BF16), w_ref[...].astype(BF16), preferred_element_type=F32)
        z = ALPHA * x_ref[...] + h
        mu = jnp.mean(z, axis=-1, keepdims=True)
        zc = z - mu
        rstd = lax.rsqrt(jnp.mean(zc * zc, axis=-1, keepdims=True) + LN_EPS)
        zh = zc * rstd
        zh_ref[...] = zh
        rs_ref[...] = rstd
        y = zh * g_ref[...] + b_ref[...]
        y_ref[...] = y
        yb_ref[...] = y.astype(BF16)

    blk = pl.BlockSpec((tt, D), lambda i: (i, 0))
    row = pl.BlockSpec((1, D), lambda i: (0, 0))
    return pl.pallas_call(
        body, name=name, grid=(T // tt,),
        in_specs=[pl.BlockSpec((tt, K), lambda i: (i, 0)), pl.BlockSpec((K, D), lambda i: (0, 0)),
                  blk, row, row],
        out_specs=[blk, blk, blk, pl.BlockSpec((tt, 1), lambda i: (i, 0))],
        out_shape=[jax.ShapeDtypeStruct((T, D), F32), jax.ShapeDtypeStruct((T, D), BF16),
                   jax.ShapeDtypeStruct((T, D), F32), jax.ShapeDtypeStruct((T, 1), F32)],
        compiler_params=_cparams("parallel"),
    )(a, w, x, g, b)


def _ln_bwd_tile(dy, zh_ref, rs_ref, g_ref, dz_ref, dzb_ref, dg_ref, db_ref, first):
    @pl.when(first)
    def _():
        dg_ref[...] = jnp.zeros_like(dg_ref)
        db_ref[...] = jnp.zeros_like(db_ref)

    zh = zh_ref[...]
    dg_ref[...] += jnp.sum(dy * zh, axis=0, keepdims=True)
    db_ref[...] += jnp.sum(dy, axis=0, keepdims=True)
    dzh = dy * g_ref[...]
    m1 = jnp.mean(dzh, axis=-1, keepdims=True)
    m2 = jnp.mean(dzh * zh, axis=-1, keepdims=True)
    dz = rs_ref[...] * (dzh - m1 - zh * m2)
    dz_ref[...] = dz
    dzb_ref[...] = dz.astype(BF16)


def _ln_bwd_specs(T, D, tt):
    blk = pl.BlockSpec((tt, D), lambda i: (i, 0))
    row = pl.BlockSpec((1, D), lambda i: (0, 0))
    return ([blk, pl.BlockSpec((tt, 1), lambda i: (i, 0)), row], [blk, blk, row, row],
            [jax.ShapeDtypeStruct((T, D), F32), jax.ShapeDtypeStruct((T, D), BF16),
             jax.ShapeDtypeStruct((1, D), F32), jax.ShapeDtypeStruct((1, D), F32)])


def _loss_ln_bwd(y, tgt, zh, rstd, g, *, tt, name):
    T, D = y.shape
    ln_in, ln_out, ln_shapes = _ln_bwd_specs(T, D, tt)

    def body(y_ref, t_ref, zh_ref, rs_ref, g_ref, l_ref, dz_ref, dzb_ref, dg_ref, db_ref):
        first = pl.program_id(0) == 0

        @pl.when(first)
        def _():
            l_ref[...] = jnp.zeros_like(l_ref)

        e = y_ref[...] - t_ref[...]
        l_ref[...] += jnp.sum(e * e, axis=0, keepdims=True) * (0.5 / D)
        _ln_bwd_tile(e * (1.0 / D), zh_ref, rs_ref, g_ref, dz_ref, dzb_ref, dg_ref, db_ref, first)

    blk = pl.BlockSpec((tt, D), lambda i: (i, 0))
    return pl.pallas_call(
        body, name=name, grid=(T // tt,),
        in_specs=[blk, blk] + ln_in,
        out_specs=[pl.BlockSpec((1, D), lambda i: (0, 0))] + ln_out,
        out_shape=[jax.ShapeDtypeStruct((1, D), F32)] + ln_shapes,
        compiler_params=_cparams("arbitrary"),
    )(y, tgt, zh, rstd, g)


def _dx_ln_bwd(a, b, add, zh, rstd, g, *, tm, name):
    T, D = add.shape
    na = len(a)
    K = sum(p.shape[1] for p in a)
    ln_in, ln_out, ln_shapes = _ln_bwd_specs(T, D, tm)

    def body(*refs):
        a_refs, b_ref, add_ref = refs[:na], refs[na], refs[na + 1]
        av = [r[...].astype(BF16) for r in a_refs]
        av = av[0] if na == 1 else jnp.concatenate(av, axis=1)
        dy = lax.dot_general(av, b_ref[...].astype(BF16), (((1,), (1,)), ((), ())),
                             preferred_element_type=F32) + ALPHA * add_ref[...]
        _ln_bwd_tile(dy, *refs[na + 2:], pl.program_id(0) == 0)

    return pl.pallas_call(
        body, name=name, grid=(T // tm,),
        in_specs=[pl.BlockSpec((tm, p.shape[1]), lambda i: (i, 0)) for p in a]
        + [pl.BlockSpec((D, K), lambda i: (0, 0)), pl.BlockSpec((tm, D), lambda i: (i, 0))] + ln_in,
        out_specs=ln_out, out_shape=ln_shapes,
        compiler_params=_cparams("arbitrary"),
    )(*a, b, add, zh, rstd, g)


def _row_tile(rows, target):
    best = SUBLANES
    for t in range(SUBLANES, target + 1, SUBLANES):
        if rows % t == 0:
            best = t
    return best


def _add_own(g, recv, c_idx, *, tr, name):
    _, M, R, C = g.shape

    def body(c_ref, g_ref, r_ref, o_ref, ob_ref):
        s = g_ref[0] + r_ref[...]
        o_ref[...] = s
        ob_ref[...] = s.astype(BF16)

    blk = pl.BlockSpec((1, tr, C), lambda k, i, c: (k, i, 0))
    return pl.pallas_call(
        body, name=name,
        grid_spec=pltpu.PrefetchScalarGridSpec(
            num_scalar_prefetch=1, grid=(M, R // tr),
            in_specs=[pl.BlockSpec((1, 1, tr, C), lambda k, i, c: (c[0], k, i, 0)), blk],
            out_specs=[blk, blk]),
        out_shape=[jax.ShapeDtypeStruct((M, R, C), F32), jax.ShapeDtypeStruct((M, R, C), BF16)],
        compiler_params=_cparams("parallel", "parallel"),
    )(c_idx, g, recv)


def _adamw_math(g, w_ref, m_ref, v_ref, g_ref, d_ref, nm_ref, nv_ref):
    nm = ADAM_B1 * m_ref[...] + (1.0 - ADAM_B1) * g
    nv = ADAM_B2 * v_ref[...] + (1.0 - ADAM_B2) * (g * g)
    m_hat = nm / (1.0 - ADAM_B1 ** ADAM_STEP)
    v_hat = nv / (1.0 - ADAM_B2 ** ADAM_STEP)
    g_ref[...] = g
    nm_ref[...] = nm
    nv_ref[...] = nv
    d_ref[...] = (-ADAM_LR) * (m_hat / (jnp.sqrt(v_hat) + ADAM_EPS) + ADAM_WD * w_ref[...])


def _adamw(parts, w, m, v, *, tr, name):
    n, R, C = parts.shape
    tr = min(tr, R)

    def body(p_ref, w_ref, m_ref, v_ref, *out_refs):
        g = p_ref[0]
        for k in range(1, n):
            g = g + p_ref[k]
        _adamw_math(g, w_ref, m_ref, v_ref, *out_refs)

    blk = pl.BlockSpec((tr, C), lambda i: (i, 0))
    out = jax.ShapeDtypeStruct((R, C), F32)
    return pl.pallas_call(
        body, name=name, grid=(R // tr,),
        in_specs=[pl.BlockSpec((n, tr, C), lambda i: (0, i, 0)), blk, blk, blk],
        out_specs=[blk, blk, blk, blk], out_shape=[out, out, out, out],
        compiler_params=_cparams("parallel"),
    )(parts, w, m, v)


def _adamw_shard(parts_by_layer, place, w, m, v, *, tr, name):
    L, R, C = w.shape
    flat = [(l, a, pick) for l, parts in enumerate(parts_by_layer) for a, pick in parts]
    n = len(flat)

    def body(place_ref, *refs):
        w_ref, m_ref, v_ref = refs[n:n + 3]
        for layer in range(L):
            @pl.when(pl.program_id(0) == layer)
            def _(layer=layer):
                g = None
                for (l, _, _), r in zip(flat, refs[:n]):
                    if l == layer:
                        blk = r[(0,) * (len(r.shape) - 3)].astype(F32)
                        g = blk if g is None else g + blk
                _adamw_math(g, w_ref, m_ref, v_ref, *refs[n + 3:])

    blk = pl.BlockSpec((1, tr, C), lambda ly, i, s: (ly, i, 0))

    def part_spec(l, a, pick):
        return pl.BlockSpec((1,) * (a.ndim - 2) + (tr, C),
                            lambda ly, i, s: (*pick(s), jnp.where(ly == l, i, 0), 0))

    out = jax.ShapeDtypeStruct(w.shape, F32)
    return pl.pallas_call(
        body, name=name,
        grid_spec=pltpu.PrefetchScalarGridSpec(
            num_scalar_prefetch=1, grid=(L, R // tr),
            in_specs=[part_spec(*f) for f in flat] + [blk, blk, blk],
            out_specs=[blk, blk, blk, blk]),
        out_shape=[out, out, out, out],
        compiler_params=_cparams("arbitrary", "arbitrary"),
    )(place, *[a for _, a, _ in flat], w, m, v)


def _two_stage_parts(h, recv):
    return [(h, lambda s: (s[0], 0))] + [(recv, lambda s, d=d: (s[0] ^ d, 0)) for d in (1, 2, 3)]


def _direct_parts(g, recv):
    return [(g, lambda s: (s[1], s[0]))] + [
        (recv, lambda s, a=p // 4, d=p % 4: (s[1] ^ a, s[0] ^ d)) for p in range(1, 8)]


SHARD_AXIS = dict(attn_w_in=1, attn_w_out=0, rnn_w_in=1, rnn_w_out=0, rnn_w_a=1, rnn_w_i=1,
                  rnn_conv_w=1, rnn_conv_b=0, rnn_b_a=0, rnn_b_i=0, rnn_lambda=0)
RNN_ROWED = ("rnn_w_out", "rnn_w_a", "rnn_w_i")
SMALL = ("rnn_conv_w", "rnn_conv_b", "rnn_b_a", "rnn_b_i", "rnn_lambda")
PACK_C = 1024


def _elems(shape):
    n = 1
    for s in shape:
        n *= s
    return n


def _pack_rows(p, idx, dtype):
    parts = [p[k][idx].astype(dtype).reshape(-1, PACK_C) for k in RNN_ROWED]
    small = jnp.concatenate([p[k][idx].reshape(-1) for k in SMALL])
    tile_rows = SUBLANES * (4 // jnp.dtype(dtype).itemsize)
    if dtype == BF16:
        small = lax.bitcast_convert_type(small, BF16)
    small = small.reshape(-1, PACK_C)
    parts.append(jnp.pad(small, ((0, tile_rows - small.shape[0]), (0, 0))))
    return jnp.concatenate(parts, axis=0)


def _unpack_rows(flat, shapes):
    lead = flat.shape[:-2]
    out, r = {}, 0
    for k in RNN_ROWED:
        n = _elems(shapes[k]) // PACK_C
        out[k] = flat[..., r:r + n, :].reshape(lead + shapes[k])
        r += n
    n_small = sum(_elems(shapes[k]) for k in SMALL)
    small = flat[..., r:r + n_small // PACK_C, :].reshape(lead + (-1,))
    o = 0
    for k in SMALL:
        n = _elems(shapes[k])
        out[k] = small[..., o:o + n].reshape(lead + shapes[k])
        o += n
    return out


def _join_columns(g, width, *, tr, name):
    _, _, R, S = g.shape

    def body(*refs):
        o_ref = refs[8]
        parts = [refs[r][0, 0].astype(F32) for r in range(8)]
        if width > 8 * S:
            parts.append(jnp.zeros((tr, width - 8 * S), F32))
        o_ref[...] = jnp.concatenate(parts, axis=-1).astype(o_ref.dtype)

    def shard(r):
        return pl.BlockSpec((1, 1, tr, S), lambda i: (r % 2, r // 2, i, 0))

    return pl.pallas_call(
        body, name=name, grid=(R // tr,),
        in_specs=[shard(r) for r in range(8)],
        out_specs=pl.BlockSpec((tr, width), lambda i: (i, 0)),
        out_shape=jax.ShapeDtypeStruct((R, width), g.dtype),
        compiler_params=_cparams("parallel"),
    )(*([g] * 8))


def _split_columns(parts, S, *, tr, name):
    R = parts[0].shape[0]
    n = len(parts)

    def body(*refs):
        o_ref = refs[n]
        x = jnp.concatenate([r[...] for r in refs[:n]], axis=1)
        for r in range(8):
            o_ref[r % 2, r // 2] = x[:, r * S:(r + 1) * S]

    return pl.pallas_call(
        body, name=name, grid=(R // tr,),
        in_specs=[pl.BlockSpec((tr, p.shape[1]), lambda i: (i, 0)) for p in parts],
        out_specs=pl.BlockSpec((2, 4, tr, S), lambda i: (0, 0, i, 0)),
        out_shape=jax.ShapeDtypeStruct((2, 4, R, S), parts[0].dtype),
        compiler_params=_cparams("parallel"),
    )(*parts)


def _to_full(g, k, sh):
    ax, nd = SHARD_AXIS[k], len(sh)
    perm = tuple(range(2, 2 + ax)) + (1, 0) + tuple(range(2 + ax, 2 + nd))
    return g.transpose(perm).reshape(sh[:ax] + (8 * sh[ax],) + sh[ax + 1:])


def _from_full(full, k, sh):
    ax, nd = SHARD_AXIS[k], len(sh)
    t = full.reshape(sh[:ax] + (4, 2, sh[ax]) + sh[ax + 1:])
    return t.transpose((ax + 1, ax) + tuple(range(ax)) + tuple(range(ax + 2, nd + 2)))


def _unpack_gathered_rows(g, shapes):
    out, r = {}, 0
    for k in RNN_ROWED:
        n = _elems(shapes[k]) // PACK_C
        out[k] = _to_full(g[:, :, r:r + n].reshape((2, 4) + shapes[k]), k, shapes[k])
        r += n
    n_small = sum(_elems(shapes[k]) for k in SMALL)
    nr = 2 * n_small // PACK_C
    small = lax.bitcast_convert_type(g[:, :, r:r + nr].reshape(2, 4, n_small, 2), F32)
    o = 0
    for k in SMALL:
        n = _elems(shapes[k])
        out[k] = _to_full(small[:, :, o:o + n].reshape((2, 4) + shapes[k]), k, shapes[k])
        o += n
    return out


def _pack_grad_rows(full, shapes):
    parts = [_from_full(full[k], k, shapes[k]).reshape(2, 4, -1, PACK_C) for k in RNN_ROWED]
    small = jnp.concatenate(
        [_from_full(full[k], k, shapes[k]).reshape(2, 4, -1) for k in SMALL], axis=-1)
    small = small.reshape(2, 4, -1, PACK_C)
    parts.append(jnp.pad(small, ((0, 0), (0, 0), (0, SUBLANES - small.shape[2]), (0, 0))))
    return jnp.concatenate(parts, axis=2)


def kernel(x, ln_g, ln_b, attn_w_in, attn_b_f, attn_w_out, rnn_w_in, rnn_conv_w, rnn_conv_b, rnn_w_a, rnn_b_a, rnn_w_i, rnn_b_i, rnn_lambda, rnn_w_out, loss_target, m_ln_g, m_ln_b, m_attn_w_in, m_attn_b_f, m_attn_w_out, m_rnn_w_in, m_rnn_conv_w, m_rnn_conv_b, m_rnn_w_a, m_rnn_b_a, m_rnn_w_i, m_rnn_b_i, m_rnn_lambda, m_rnn_w_out, v_ln_g, v_ln_b, v_attn_w_in, v_attn_b_f, v_attn_w_out, v_rnn_w_in, v_rnn_conv_w, v_rnn_conv_b, v_rnn_w_a, v_rnn_b_a, v_rnn_w_i, v_rnn_b_i, v_rnn_lambda, v_rnn_w_out):
    w_loc = dict(attn_w_in=attn_w_in, attn_w_out=attn_w_out, rnn_w_in=rnn_w_in, rnn_w_a=rnn_w_a,
                 rnn_w_i=rnn_w_i, rnn_w_out=rnn_w_out, rnn_conv_w=rnn_conv_w, rnn_conv_b=rnn_conv_b,
                 rnn_b_a=rnn_b_a, rnn_b_i=rnn_b_i, rnn_lambda=rnn_lambda)
    m_loc = dict(attn_w_in=m_attn_w_in, attn_w_out=m_attn_w_out, rnn_w_in=m_rnn_w_in,
                 rnn_w_a=m_rnn_w_a, rnn_w_i=m_rnn_w_i, rnn_w_out=m_rnn_w_out,
                 rnn_conv_w=m_rnn_conv_w, rnn_conv_b=m_rnn_conv_b, rnn_b_a=m_rnn_b_a,
                 rnn_b_i=m_rnn_b_i, rnn_lambda=m_rnn_lambda)
    v_loc = dict(attn_w_in=v_attn_w_in, attn_w_out=v_attn_w_out, rnn_w_in=v_rnn_w_in,
                 rnn_w_a=v_rnn_w_a, rnn_w_i=v_rnn_w_i, rnn_w_out=v_rnn_w_out,
                 rnn_conv_w=v_rnn_conv_w, rnn_conv_b=v_rnn_conv_b, rnn_b_a=v_rnn_b_a,
                 rnn_b_i=v_rnn_b_i, rnn_lambda=v_rnn_lambda)
    shapes = {k: tuple(a.shape[1:]) for k, a in w_loc.items()}
    T, D = x.shape[1], x.shape[2]
    n_f = attn_b_f.shape[1]
    tb = min(1024, T)
    tb_bwd = min(512, T)
    tt_rg = min(128, T)
    tt_ln = min(256, T)
    c_idx = lax.axis_index("c").astype(jnp.int32).reshape(1)
    me_idx = (2 * lax.axis_index("x") + lax.axis_index("y")).astype(jnp.int32).reshape(1)
    place = jnp.concatenate([me_idx, c_idx])

    def attn_w_in_full(g_in, idx):
        return _join_columns(g_in, 4 * D + LANES, tr=256, name=f"a_join{idx}")

    def attn_w_out_full(g_out):
        return _to_full(g_out, "attn_w_out", shapes["attn_w_out"])

    def rnn_weights(g_in, g_rows, idx):
        w = _unpack_gathered_rows(g_rows, shapes)
        w["rnn_w_in"] = _join_columns(g_in, 2 * D, tr=256, name=f"r_join{idx}")
        w["small"] = jnp.concatenate([w["rnn_conv_w"], w["rnn_conv_b"][None], w["rnn_b_a"][None],
                                      w["rnn_b_i"][None], w["rnn_lambda"][None]])
        return w

    g0 = _ag_c(_run_exchange(_Exchange("gather", [attn_w_in[0].astype(BF16)]), "ag_w0_xy"),
               "ag_w0_c")
    later = _Exchange("gather8", [attn_w_in[1].astype(BF16)] + [
        a for i in range(2) for a in (attn_w_out[i].astype(BF16), rnn_w_in[i].astype(BF16),
                                      _pack_rows(w_loc, i, BF16))])
    w_attn_in, w_attn_out, w_rnn = [attn_w_in_full(g0[0], 0), None], [None, None], [None, None]
    bf_rows = jnp.pad(attn_b_f, ((0, 0), (0, LANES - n_f)))[:, None, :]

    xs, xb, saved = [x[0]], [x[0]], []
    for layer in range(DEPTH):
        idx, xl, xm = layer // 2, xs[-1], xb[-1]
        if layer % 2 == 0:
            proj = _matmul(xm, w_attn_in[idx], trans_b=False, tm=512, tn=1408,
                           name=f"a_proj{layer}")
            cum_t = _cumsum_fwd(proj, bf_rows[idx], tt=min(512, T), name=f"a_cum{layer}")
            cum2 = cum_t[:N_HEADS].reshape(N_PAIRS, 2, T)
            o, og, lp, *got = _flash_fwd(proj, cum2.reshape(N_PAIRS, 2, T // tb, tb), tb=tb,
                                         name=f"a_fwd{layer}", host=later if layer == 0 else None)
            cum4 = cum2.reshape(N_PAIRS, 2, T // tb_bwd, tb_bwd)
            if layer == 0:
                w_attn_in[1] = attn_w_in_full(got[0], 1)
                w_attn_out = [attn_w_out_full(got[1 + 3 * i]) for i in range(2)]
                w_rnn = [rnn_weights(got[2 + 3 * i], got[3 + 3 * i], i) for i in range(2)]
            branch, w_out = og, w_attn_out[idx]
            saved.append((proj, cum4, o, og, lp))
        else:
            w = w_rnn[idx]
            proj = _matmul(xm, w["rnn_w_in"], trans_b=False, tm=512, tn=1024,
                           name=f"r_proj{layer}")
            hs, yr = _rg_fwd(proj, w["small"], w["rnn_w_a"], w["rnn_w_i"], tt=tt_rg,
                             name=f"r_fwd{layer}")
            branch, w_out = yr, w["rnn_w_out"]
            saved.append((proj, hs, yr))
        y, yb, zh, rstd = _out_ln(branch, w_out, xl, ln_g[layer][None], ln_b[layer][None],
                                  tt=512, name=f"out_ln{layer}")
        saved[-1] = saved[-1] + (zh, rstd)
        xs.append(y)
        xb.append(yb)

    def ln_below(layer):
        return saved[layer][-2:] + (ln_g[layer][None],)

    loss_lanes, *ln_grads = _loss_ln_bwd(xs[-1], loss_target[0], *ln_below(DEPTH - 1), tt=tt_ln,
                                         name="loss_ln_bwd")
    loss_part = jnp.sum(loss_lanes)

    def reduce_pair(gs, layer):
        recv = _rs_c(gs, f"rs_c{layer}")
        outs = [_add_own(g, r, c_idx, tr=_row_tile(g.shape[2], 512), name=f"rs_add{layer}_{n}")
                for n, (g, r) in enumerate(zip(gs, recv))]
        return [o[0][:, None] for o in outs], [o[1][:, None] for o in outs]

    def rep_pack(lg, lb, bf, scalar=0.0):
        rows = jnp.concatenate([lg, lb, jnp.pad(bf.reshape(1, -1), ((0, 0), (0, D - 2 * n_f))),
                                jnp.broadcast_to(jnp.asarray(scalar, F32), (1, D))])
        return jnp.pad(rows, ((0, 16 - rows.shape[0]), (0, 0)))

    SCALAR_ROW = 2 * DEPTH + 1

    part, got_parts = [None] * DEPTH, [None] * DEPTH
    d_ln_g, d_ln_b, d_bf = [None] * DEPTH, [None] * DEPTH, [None, None]
    for layer in reversed(range(DEPTH)):
        idx, xm = layer // 2, xb[layer]
        dz, dzb, dg, db = ln_grads
        d_ln_g[layer], d_ln_b[layer] = dg[0], db[0]
        if layer % 2 == 0:
            w_in, w_out = w_attn_in[idx], w_attn_out[idx]
            proj, cum4, o, og, lp = saved[layer][:5]
            dog = _matmul(dzb, w_out, trans_b=True, tm=512, tn=1024, name=f"a_dog{layer}")
            dwo = _matmul_tn(og, dzb, tm=512, tn=1024, tk=1024, name=f"a_dwo{layer}")
            g_out = _from_full(dwo, "attn_w_out", shapes["attn_w_out"])
            riders = [l for l in range(layer + 1, DEPTH) if got_parts[l] is None]
            early = [g_out] if layer == 0 else []
            host = _Exchange("scatter8", [g for l in riders for g in part[l]] + early)
            dq, dgate, dk, dv, dcum_q, dcum_k, *got = _flash_bwd(proj, cum4, o, dog, lp, tb=tb_bwd,
                                                                 name=f"a_bwd{layer}", host=host)
            for l in riders:
                got_parts[l], got = got[:len(part[l])], got[len(part[l]):]
            dcum_t = (dcum_q.transpose(0, 2, 1, 3) + dcum_k).reshape(N_HEADS, T)
            dcum_t = jnp.pad(dcum_t, ((0, LANES - N_HEADS), (0, 0)))
            df, dbf = _cumsum_bwd(dcum_t, proj, bf_rows[idx], tt=min(512, T), name=f"a_dcum{layer}")
            d_bf[idx] = dbf[0, :n_f]
            dproj = [dq, dk, dv, dgate, df]
            rep_host = _Exchange("gather8", [rep_pack(jnp.stack(d_ln_g), jnp.stack(d_ln_b),
                                                      jnp.stack(d_bf), loss_part)]) if layer == 0 else None
            dwi = _matmul_tn_parts(xm, dproj, tm=512, tk=1024, name=f"a_dwi{layer}", host=rep_host)
            if layer == 0:
                dwi, rep = dwi[:-1], dwi[-1]
            g_in = _split_columns(dwi, shapes["attn_w_in"][1], tr=256, name=f"a_split{layer}")
            if layer > 0:
                part[layer] = [g_in, g_out]
                ln_grads = _dx_ln_bwd(dproj, w_in, dz, *ln_below(layer - 1), tm=256,
                                      name=f"a_dx{layer}")
            else:
                half, narrow = reduce_pair([g_in], layer)
                dy, recv = _matmul(dproj, w_in, trans_b=True, tm=512, tn=1024, name=f"a_dx{layer}",
                                   add=dz, add_scale=ALPHA, host=_Exchange("scatter", narrow))
                last_parts = [_two_stage_parts(half[0], recv), _direct_parts(g_out, got[0])]
        else:
            w = w_rnn[idx]
            proj, hs, yr = saved[layer][:3]
            dyr = _matmul(dzb, w["rnn_w_out"], trans_b=True, tm=512, tn=1024, name=f"r_dy{layer}")
            dwo = _matmul_tn(yr, dzb, tm=512, tn=1024, tk=1024, name=f"r_dwo{layer}")
            dproj, dwa, dwi_, dsm = _rg_bwd(proj, hs, dyr, w["small"], w["rnn_w_a"], w["rnn_w_i"],
                                            tt=tt_rg, name=f"r_bwd{layer}")
            dwin = _matmul_tn(xm, dproj, tm=512, tn=2048, tk=1024, name=f"r_dwi{layer}")
            ln_grads = _dx_ln_bwd([dproj], w["rnn_w_in"], dz, *ln_below(layer - 1), tm=512,
                                  name=f"r_dx{layer}")
            full = dict(rnn_w_out=dwo, rnn_w_a=dwa, rnn_w_i=dwi_, rnn_conv_w=dsm[0:4],
                        rnn_conv_b=dsm[4], rnn_b_a=dsm[5], rnn_b_i=dsm[6], rnn_lambda=dsm[7])
            part[layer] = [_split_columns([dwin], shapes["rnn_w_in"][1], tr=256,
                                          name=f"r_split{layer}"),
                           _pack_grad_rows(full, shapes)]
    grad_x = dy[None]

    def grad_parts(layer, n):
        return last_parts[n] if layer == 0 else _direct_parts(part[layer][n], got_parts[layer][n])

    def update(k, n, wmv):
        layers = [2 * idx + (0 if k.startswith("attn") else 1) for idx in range(2)]
        return _adamw_shard([grad_parts(layer, n) for layer in layers], place, *wmv,
                            tr=_row_tile(wmv[0].shape[1], 256), name=f"adamw_{k}")

    shard_outs = [dict() for _ in range(4)]
    for k, n in (("attn_w_in", 0), ("attn_w_out", 1), ("rnn_w_in", 0)):
        for j, a in enumerate(update(k, n, (w_loc[k], m_loc[k], v_loc[k]))):
            shard_outs[j][k] = a
    rows_wmv = [jnp.stack([_pack_rows(d, idx, F32) for idx in range(2)]) for d in (w_loc, m_loc, v_loc)]
    for j, a in enumerate(update("rnn_rows", 1, rows_wmv)):
        shard_outs[j].update(_unpack_rows(a, shapes))
    g_sh, d_sh, nm_sh, nv_sh = shard_outs

    rg, rd, rm, rv = _adamw(rep.reshape(8, 16, D), rep_pack(ln_g, ln_b, attn_b_f),
                            rep_pack(m_ln_g, m_ln_b, m_attn_b_f),
                            rep_pack(v_ln_g, v_ln_b, v_attn_b_f), tr=16, name="adamw_rep")
    loss = rg[SCALAR_ROW, 0]

    def rep_unpack(a):
        return dict(ln_g=a[0:DEPTH], ln_b=a[DEPTH:2 * DEPTH],
                    attn_b_f=a[2 * DEPTH, :2 * n_f].reshape(2, n_f))

    order = ("ln_g", "ln_b", "attn_w_in", "attn_b_f", "attn_w_out", "rnn_w_in", "rnn_conv_w",
             "rnn_conv_b", "rnn_w_a", "rnn_b_a", "rnn_w_i", "rnn_b_i", "rnn_lambda", "rnn_w_out")
    outs = [loss, grad_x]
    for sh, rp in ((g_sh, rg), (d_sh, rd), (nm_sh, rm), (nv_sh, rv)):
        allp = {**sh, **rep_unpack(rp)}
        outs.extend(allp[k] for k in order)
    return tuple(outs)
```

```python
import jax
import jax.numpy as jnp
from jax import lax
from jax.experimental import pallas as pl
from jax.experimental.pallas import tpu as pltpu

F32 = jnp.float32
BF16 = jnp.bfloat16

DEPTH = 4
N_HEADS = 16
HEAD_DIM = 64
N_PAIRS = N_HEADS // 2
RNN_BLOCKS = 4
RNN_BLOCK_WIDTH = 256
CONV_WIDTH = 4
LRU_C = 8.0
ALPHA = (2.0 * DEPTH) ** 0.25
LN_EPS = 1e-5
ADAM_LR, ADAM_B1, ADAM_B2, ADAM_EPS, ADAM_WD, ADAM_STEP = 0.001, 0.9, 0.999, 1e-8, 0.01, 10

LANES = 128
SUBLANES = 8
VMEM_LIMIT = 48 * 1024 * 1024

MESH = pl.DeviceIdType.MESH
HBM_SPEC = pl.BlockSpec(memory_space=pltpu.HBM)


def _cparams(*sem):
    return pltpu.CompilerParams(dimension_semantics=sem, vmem_limit_bytes=VMEM_LIMIT)


def _sigmoid(x):
    return 1.0 / (1.0 + jnp.exp(-x))


def _softplus(x):
    return jnp.maximum(x, 0.0) + jnp.log(1.0 + jnp.exp(-jnp.abs(x)))


D2D_CHUNKS = 16
ICI_CHUNKS = 8


def _row_chunks(rows, dtype, k):
    unit = SUBLANES * (4 // jnp.dtype(dtype).itemsize)
    assert rows % unit == 0
    units = rows // unit
    k = max(1, min(k, units))
    base, rem = divmod(units, k)
    out, r = [], 0
    for i in range(k):
        n = (base + (1 if i < rem else 0)) * unit
        out.append((r, n))
        r += n
    return out


def _chunks(shape, dtype, k):
    if len(shape) == 2:
        return [(pl.ds(r0, n),) for r0, n in _row_chunks(shape[0], dtype, k)]
    per = max(1, k // shape[0])
    return [(l, pl.ds(r0, n)) for l in range(shape[0]) for r0, n in _row_chunks(shape[1], dtype, per)]


def _mesh_place():
    x, y, c = lax.axis_index("x"), lax.axis_index("y"), lax.axis_index("c")
    return x, y, c, 2 * x + y


def _chip_peer(x, y, c, d):
    px, py = x ^ (d >> 1), y ^ (d & 1)
    return (px, py, c), 2 * px + py


def _remote(src, dst, send_sem, recv_sem, dev):
    return pltpu.make_async_remote_copy(src_ref=src, dst_ref=dst, send_sem=send_sem,
                                        recv_sem=recv_sem, device_id=dev, device_id_type=MESH)


def _comm_call(body, name, ins, out_shapes, n_sems, aliases=None):
    n = len(ins)
    return pl.pallas_call(
        body, name=name,
        out_shape=out_shapes, in_specs=[HBM_SPEC] * n, out_specs=[HBM_SPEC] * n,
        input_output_aliases=aliases or {},
        scratch_shapes=[pltpu.SemaphoreType.DMA((n_sems, n)), pltpu.SemaphoreType.DMA((n_sems, n))],
    )(*ins)


class _Exchange:
    def __init__(self, kind, arrays):
        self.kind, self.arrays, self.n = kind, list(arrays), len(arrays)
        self.is_gather, self.all8 = kind.startswith("gather"), kind.endswith("8")
        k = ICI_CHUNKS // 4 if self.all8 else ICI_CHUNKS
        if self.is_gather:
            self.chunks = [_chunks(a.shape, a.dtype, k) for a in arrays]
            self.out_shapes = [jax.ShapeDtypeStruct((2, 4) + tuple(a.shape), a.dtype) for a in arrays]
        else:
            lead = 2 if self.all8 else 1
            self.chunks = [_chunks(a.shape[lead:], a.dtype, k) for a in arrays]
            self.out_shapes = [jax.ShapeDtypeStruct(a.shape, a.dtype) for a in arrays]
        self.peers = list(range(1, 8 if self.all8 else 4))
        n_sems = len(self.peers) + 1
        self.sem_shapes = [pltpu.SemaphoreType.DMA((n_sems, self.n)),
                           pltpu.SemaphoreType.DMA((n_sems, self.n))]

    def _peer(self, x, y, c, me, p):
        a, d = p // 4, p % 4
        px, py = x ^ (d >> 1), y ^ (d & 1)
        pc = 1 - c if a else c
        if self.all8:
            return (px, py, pc), (pc, 2 * px + py), (c, me)
        return (px, py, pc), (2 * px + py,), (me,)

    def _blocks(self, srcs, outs, o, c, me, theirs, mine):
        if self.kind == "gather":
            return srcs[o], outs[o].at[(c,) + mine], outs[o].at[(c,) + theirs]
        if self.kind == "gather8":
            return srcs[o], outs[o].at[mine], outs[o].at[theirs]
        return srcs[o].at[theirs], outs[o].at[mine], outs[o].at[theirs]

    def start(self, srcs, outs, send_sems, recv_sems):
        x, y, c, me = _mesh_place()
        if self.is_gather:
            for o in range(self.n):
                for idx in self.chunks[o]:
                    pltpu.make_async_copy(srcs[o].at[idx], outs[o].at[(c, me) + idx],
                                          send_sems.at[0, o]).start()
        for p in self.peers:
            dev, theirs, mine = self._peer(x, y, c, me, p)
            for o in range(self.n):
                src, dst, _ = self._blocks(srcs, outs, o, c, me, theirs, mine)
                for idx in self.chunks[o]:
                    _remote(src.at[idx], dst.at[idx], send_sems.at[p, o], recv_sems.at[p, o],
                            dev).start()

    def wait(self, srcs, outs, send_sems, recv_sems):
        x, y, c, me = _mesh_place()
        for wait_recv in (True, False):
            for p in self.peers:
                dev, theirs, mine = self._peer(x, y, c, me, p)
                for o in range(self.n):
                    src, _, land = self._blocks(srcs, outs, o, c, me, theirs, mine)
                    cp = _remote(src, land, send_sems.at[p, o], recv_sems.at[p, o], dev)
                    cp.wait_recv() if wait_recv else cp.wait_send()
        if self.is_gather:
            for o in range(self.n):
                pltpu.make_async_copy(srcs[o], outs[o].at[c, me], send_sems.at[0, o]).wait()


def _run_exchange(ex, name):
    n = ex.n

    def body(*refs):
        srcs, outs, send_sems, recv_sems = refs[:n], refs[n:2 * n], refs[2 * n], refs[2 * n + 1]
        ex.start(srcs, outs, send_sems, recv_sems)
        ex.wait(srcs, outs, send_sems, recv_sems)

    return _comm_call(body, name, ex.arrays, ex.out_shapes, len(ex.peers) + 1)


def _ag_c(bufs, name):
    n = len(bufs)
    chunks = [_chunks(b.shape[2:], b.dtype, D2D_CHUNKS // 4) for b in bufs]

    def body(*refs):
        srcs, outs, send_sems, recv_sems = refs[:n], refs[n:2 * n], refs[2 * n], refs[2 * n + 1]
        x, y, c, _ = _mesh_place()
        sib = (x, y, 1 - c)
        for o in range(n):
            for k in range(4):
                for idx in chunks[o]:
                    _remote(srcs[o].at[(c, k) + idx], outs[o].at[(c, k) + idx],
                            send_sems.at[0, o], recv_sems.at[0, o], sib).start()
        for o in range(n):
            _remote(srcs[o].at[c], outs[o].at[1 - c], send_sems.at[0, o], recv_sems.at[0, o],
                    sib).wait_recv()
        for o in range(n):
            _remote(srcs[o].at[c], outs[o].at[1 - c], send_sems.at[0, o], recv_sems.at[0, o],
                    sib).wait_send()

    shapes = [jax.ShapeDtypeStruct(b.shape, b.dtype) for b in bufs]
    return _comm_call(body, name, bufs, shapes, 1, aliases={i: i for i in range(n)})


def _rs_c(gs, name):
    n = len(gs)
    chunks = [_chunks(g.shape[2:], g.dtype, max(1, D2D_CHUNKS // g.shape[1])) for g in gs]

    def body(*refs):
        srcs, outs, send_sems, recv_sems = refs[:n], refs[n:2 * n], refs[2 * n], refs[2 * n + 1]
        x, y, c, _ = _mesh_place()
        sib = (x, y, 1 - c)
        for o in range(n):
            for k in range(gs[o].shape[1]):
                for idx in chunks[o]:
                    _remote(srcs[o].at[(1 - c, k) + idx], outs[o].at[(k,) + idx],
                            send_sems.at[0, o], recv_sems.at[0, o], sib).start()
        for o in range(n):
            _remote(srcs[o].at[1 - c], outs[o], send_sems.at[0, o], recv_sems.at[0, o],
                    sib).wait_recv()
        for o in range(n):
            _remote(srcs[o].at[1 - c], outs[o], send_sems.at[0, o], recv_sems.at[0, o],
                    sib).wait_send()

    shapes = [jax.ShapeDtypeStruct(g.shape[1:], g.dtype) for g in gs]
    return _comm_call(body, name, gs, shapes, 1)


def _matmul(a, b, *, trans_b, tm, tn, name, add=None, add_scale=1.0, host=None):
    a_parts = list(a) if isinstance(a, (list, tuple)) else [a]
    M, K = a_parts[0].shape[0], sum(p.shape[1] for p in a_parts)
    N = b.shape[0] if trans_b else b.shape[1]
    tm, tn = min(tm, M), min(tn, N)
    assert M % tm == 0 and N % tn == 0
    dn = (((1,), (1,)), ((), ())) if trans_b else (((1,), (0,)), ((), ()))
    na = len(a_parts)

    def body(*refs):
        a_refs, b_ref, o_ref = refs[:na], refs[na], refs[-1]
        av = [r[...].astype(BF16) for r in a_refs]
        av = av[0] if na == 1 else jnp.concatenate(av, axis=1)
        r = lax.dot_general(av, b_ref[...].astype(BF16), dn, preferred_element_type=F32)
        if add is not None:
            r = r + add_scale * refs[na + 1][...]
        o_ref[...] = r

    b_spec = (pl.BlockSpec((tn, K), lambda j, i: (j, 0)) if trans_b
              else pl.BlockSpec((K, tn), lambda j, i: (0, j)))
    in_specs = [pl.BlockSpec((tm, p.shape[1]), lambda j, i: (i, 0)) for p in a_parts] + [b_spec]
    args = a_parts + [b]
    if add is not None:
        in_specs.append(pl.BlockSpec((tm, tn), lambda j, i: (i, j)))
        args.append(add)
    grid = (N // tn, M // tm)
    x_in, x_out, x_shapes, x_scratch, x_args = _host_specs(host)
    body = _hosted(body, len(args), 1, 0, host, grid)
    outs = pl.pallas_call(
        body, name=name, grid=grid,
        in_specs=in_specs + x_in,
        out_specs=[pl.BlockSpec((tm, tn), lambda j, i: (i, j))] + x_out,
        out_shape=[jax.ShapeDtypeStruct((M, N), F32)] + x_shapes,
        scratch_shapes=x_scratch,
        compiler_params=_cparams(*(("arbitrary",) * 2 if host else ("parallel",) * 2)),
    )(*args, *x_args)
    return outs if host else outs[0]


def _matmul_tn(a, b, *, tm, tn, tk, name):
    T, M = a.shape
    N = b.shape[1]
    tm, tn, tk = min(tm, M), min(tn, N), min(tk, T)
    assert M % tm == 0 and N % tn == 0 and T % tk == 0

    def body(a_ref, b_ref, o_ref):
        @pl.when(pl.program_id(2) == 0)
        def _():
            o_ref[...] = jnp.zeros_like(o_ref)

        o_ref[...] += lax.dot_general(a_ref[...].astype(BF16), b_ref[...].astype(BF16),
                                      (((0,), (0,)), ((), ())), preferred_element_type=F32)

    return pl.pallas_call(
        body, name=name, grid=(M // tm, N // tn, T // tk),
        in_specs=[pl.BlockSpec((tk, tm), lambda i, j, k: (k, i)),
                  pl.BlockSpec((tk, tn), lambda i, j, k: (k, j))],
        out_specs=pl.BlockSpec((tm, tn), lambda i, j, k: (i, j)),
        out_shape=jax.ShapeDtypeStruct((M, N), F32),
        compiler_params=_cparams("parallel", "parallel", "arbitrary"),
    )(a, b)


def _matmul_tn_parts(a, parts, *, tm, tk, name, host=None):
    T, M = a.shape
    tm, tk = min(tm, M), min(tk, T)
    assert M % tm == 0 and T % tk == 0
    n = len(parts)
    grid = (M // tm, T // tk)
    x_in, x_out, x_shapes, x_scratch, x_args = _host_specs(host)

    def body(*refs):
        a_ref, b_refs, o_refs = refs[0], refs[1:1 + n], refs[1 + n:]
        av = a_ref[...].astype(BF16)
        for b_ref, o_ref in zip(b_refs, o_refs):
            @pl.when(pl.program_id(1) == 0)
            def _(o_ref=o_ref):
                o_ref[...] = jnp.zeros_like(o_ref)

            o_ref[...] += lax.dot_general(av, b_ref[...].astype(BF16), (((0,), (0,)), ((), ())),
                                          preferred_element_type=F32)

    body = _hosted(body, 1 + n, n, 0, host, grid)
    return pl.pallas_call(
        body, name=name, grid=grid,
        in_specs=[pl.BlockSpec((tk, tm), lambda i, k: (k, i))]
        + [pl.BlockSpec((tk, p.shape[1]), lambda i, k: (k, 0)) for p in parts] + x_in,
        out_specs=[pl.BlockSpec((tm, p.shape[1]), lambda i, k: (i, 0)) for p in parts] + x_out,
        out_shape=[jax.ShapeDtypeStruct((M, p.shape[1]), F32) for p in parts] + x_shapes,
        scratch_shapes=x_scratch,
        compiler_params=_cparams("arbitrary" if host else "parallel", "arbitrary"),
    )(a, *parts, *x_args)


def _head_masks(rows):
    lane = lax.broadcasted_iota(jnp.int32, (rows, LANES), 1)
    return lane < HEAD_DIM, lane >= HEAD_DIM


def _causal(i_q, i_k, tq, tk):
    row = i_q * tq + lax.broadcasted_iota(jnp.int32, (tq, tk), 0)
    col = i_k * tk + lax.broadcasted_iota(jnp.int32, (tq, tk), 1)
    return row >= col


def _hosted(body, n_in, n_out, n_scratch, host, grid):
    if host is None:
        return body
    nx = host.n

    def wrapped(*refs):
        ins, xsrcs = refs[:n_in], refs[n_in:n_in + nx]
        outs = refs[n_in + nx:n_in + nx + n_out]
        xouts = refs[n_in + nx + n_out:n_in + 2 * nx + n_out]
        scratch = refs[n_in + 2 * nx + n_out:n_in + 2 * nx + n_out + n_scratch]
        xsems = refs[n_in + 2 * nx + n_out + n_scratch:]
        step = pl.program_id(0) * grid[1] + pl.program_id(1)

        @pl.when(step == 0)
        def _():
            host.start(xsrcs, xouts, *xsems)

        body(*ins, *outs, *scratch)

        @pl.when(step == grid[0] * grid[1] - 1)
        def _():
            host.wait(xsrcs, xouts, *xsems)

    return wrapped


def _host_specs(host):
    if host is None:
        return [], [], [], [], []
    return ([HBM_SPEC] * host.n, [HBM_SPEC] * host.n, host.out_shapes, host.sem_shapes, host.arrays)


def _flash_fwd(proj, cum4, *, tb, name, host=None):
    T = proj.shape[0]
    D = N_HEADS * HEAD_DIM
    nb = T // tb
    cb = D // LANES
    x_in, x_out, x_shapes, x_scratch, x_args = _host_specs(host)

    def body(q_ref, k_ref, v_ref, g_ref, cum_ref, o_ref, og_ref, lp_ref, kb_ref, vb_ref):
        i = pl.program_id(1)

        @pl.when(i == 0)
        def _():
            kb_ref[...] = k_ref[...].astype(BF16)
            vb_ref[...] = v_ref[...].astype(BF16)

        q = q_ref[...] * (HEAD_DIM ** -0.5)
        masks = _head_masks(tb)
        qh = [jnp.where(masks[h], q, 0.0).astype(BF16) for h in range(2)]
        cref = [cum_ref[0, h, pl.ds(i, 1), :][:, 0:1] for h in range(2)]

        def tile(kbi, r0, nr, nk, carry, first_row):
            k0 = pl.multiple_of(kbi * tb, tb)
            kblk = kb_ref[pl.ds(k0, nk), :]
            vblk = vb_ref[pl.ds(k0, nk), :]
            new = []
            for h in range(2):
                m, l, acc = carry[h]
                s = lax.dot_general(qh[h][r0:r0 + nr], kblk, (((1,), (1,)), ((), ())),
                                    preferred_element_type=F32)
                s = s + (cref[h] - cum_ref[0, h, pl.ds(kbi, 1), :][:, 0:nk])
                if first_row is not None:
                    row = first_row + lax.broadcasted_iota(jnp.int32, (nr, nk), 0)
                    s = jnp.where(row >= lax.broadcasted_iota(jnp.int32, (nr, nk), 1), s, -jnp.inf)
                m_new = jnp.maximum(m, jnp.max(s, axis=-1, keepdims=True))
                alpha = jnp.exp(m - m_new)
                p = jnp.exp(s - m_new)
                l = alpha * l + jnp.sum(p, axis=-1, keepdims=True)
                acc = alpha * acc + jnp.dot(p.astype(BF16), vblk, preferred_element_type=F32)
                new.append((m_new, l, acc))
            return tuple(new)

        init1 = (jnp.full((tb, 1), -jnp.inf, F32), jnp.zeros((tb, 1), F32),
                 jnp.zeros((tb, LANES), F32))
        carry = lax.fori_loop(0, i, lambda kbi, c: tile(kbi, 0, tb, tb, c, None), (init1, init1))
        hb = tb // 2
        upper = tile(i, 0, hb, hb, tuple(tuple(a[:hb] for a in c) for c in carry), 0)
        lower = tile(i, hb, hb, tb, tuple(tuple(a[hb:] for a in c) for c in carry), hb)
        outs = []
        for h, (m, l, acc) in enumerate(
                tuple(jnp.concatenate([u, w], axis=0) for u, w in zip(upper[h], lower[h]))
                for h in range(2)):
            outs.append(acc / l)
            lp_ref[h] = jnp.broadcast_to(m + jnp.log(l) - cref[h], (tb, LANES))
        o = jnp.where(masks[0], outs[0], outs[1])
        o_ref[...] = o
        gate = g_ref[...]
        og_ref[...] = (o * (gate * _sigmoid(gate))).astype(BF16)

    body = _hosted(body, 5, 3, 2, host, (N_PAIRS, nb))
    return pl.pallas_call(
        body, name=name, grid=(N_PAIRS, nb),
        in_specs=[pl.BlockSpec((tb, LANES), lambda j, i: (i, j)),
                  pl.BlockSpec((T, LANES), lambda j, i: (0, cb + j)),
                  pl.BlockSpec((T, LANES), lambda j, i: (0, 2 * cb + j)),
                  pl.BlockSpec((tb, LANES), lambda j, i: (i, 3 * cb + j)),
                  pl.BlockSpec((1, 2, nb, tb), lambda j, i: (j, 0, 0, 0))] + x_in,
        out_specs=[pl.BlockSpec((tb, LANES), lambda j, i: (i, j)),
                   pl.BlockSpec((tb, LANES), lambda j, i: (i, j)),
                   pl.BlockSpec((2, tb, LANES), lambda j, i: (j, i, 0))] + x_out,
        out_shape=[jax.ShapeDtypeStruct((T, D), F32), jax.ShapeDtypeStruct((T, D), BF16),
                   jax.ShapeDtypeStruct((N_HEADS, T, LANES), F32)] + x_shapes,
        scratch_shapes=[pltpu.VMEM((T, LANES), BF16), pltpu.VMEM((T, LANES), BF16)] + x_scratch,
        compiler_params=_cparams("arbitrary", "arbitrary"),
    )(proj, proj, proj, proj, cum4, *x_args)


def _flash_bwd(proj, cum4, o, dog, lp, *, tb, name, host=None):
    T = proj.shape[0]
    D = N_HEADS * HEAD_DIM
    nb = T // tb
    cb = D // LANES
    x_in, x_out, x_shapes, x_scratch, x_args = _host_specs(host)

    def body(q_ref, k_ref, v_ref, g_ref, cum_ref, o_ref, dog_ref, lp_ref,
             dq_ref, dg_ref, dk_ref, dv_ref, dcq_ref, dck_ref,
             kb_ref, vb_ref, dka_ref, dva_ref, dca_ref):
        i = pl.program_id(1)

        @pl.when(i == 0)
        def _():
            kb_ref[...] = k_ref[...].astype(BF16)
            vb_ref[...] = v_ref[...].astype(BF16)
            dka_ref[...] = jnp.zeros_like(dka_ref)
            dva_ref[...] = jnp.zeros_like(dva_ref)
            dca_ref[...] = jnp.zeros_like(dca_ref)

        gate = g_ref[...]
        sg = _sigmoid(gate)
        o = o_ref[...]
        dog = dog_ref[...]
        do = dog * (gate * sg)
        dg_ref[...] = (dog * o * (sg * (1.0 + gate * (1.0 - sg)))).astype(BF16)
        q = q_ref[...] * (HEAD_DIM ** -0.5)
        masks = _head_masks(tb)
        qh = [jnp.where(masks[h], q, 0.0).astype(BF16) for h in range(2)]
        doh = [jnp.where(masks[h], do, 0.0).astype(BF16) for h in range(2)]
        delta = [jnp.sum(jnp.where(masks[h], do * o, 0.0), axis=-1, keepdims=True) for h in range(2)]
        lph = [lp_ref[h][:, 0:1] for h in range(2)]

        def step(kbi, carry, masked):
            k0 = pl.multiple_of(kbi * tb, tb)
            kblk = kb_ref[pl.ds(k0, tb), :]
            vblk = vb_ref[pl.ds(k0, tb), :]
            new, dk, dv = [], None, None
            for h in range(2):
                acc, rs = carry[h]
                s = lax.dot_general(qh[h], kblk, (((1,), (1,)), ((), ())), preferred_element_type=F32)
                p = jnp.exp(s - cum_ref[0, h, pl.ds(kbi, 1), :] - lph[h])
                if masked:
                    p = jnp.where(_causal(i, kbi, tb, tb), p, 0.0)
                dp = lax.dot_general(doh[h], vblk, (((1,), (1,)), ((), ())),
                                     preferred_element_type=F32)
                ds = p * (dp - delta[h])
                pb, dsb = p.astype(BF16), ds.astype(BF16)
                dv_h = lax.dot_general(pb, doh[h], (((0,), (0,)), ((), ())),
                                       preferred_element_type=F32)
                dk_h = lax.dot_general(dsb, qh[h], (((0,), (0,)), ((), ())),
                                       preferred_element_type=F32)
                dv = dv_h if dv is None else dv + dv_h
                dk = dk_h if dk is None else dk + dk_h
                dca_ref[h, pl.ds(kbi, 1), :] -= jnp.sum(ds, axis=0, keepdims=True)
                new.append((acc + jnp.dot(dsb, kblk, preferred_element_type=F32),
                            rs + jnp.sum(ds, axis=-1, keepdims=True)))
            dka_ref[pl.ds(k0, tb), :] += dk
            dva_ref[pl.ds(k0, tb), :] += dv
            return tuple(new)

        init1 = (jnp.zeros((tb, LANES), F32), jnp.zeros((tb, 1), F32))
        carry = lax.fori_loop(0, i, lambda kbi, c: step(kbi, c, False), (init1, init1))
        dqs = []
        for h, (acc, rs) in enumerate(step(i, carry, True)):
            dqs.append(acc)
            dcq_ref[0, 0, pl.ds(h, 1), :] = jnp.broadcast_to(rs, (tb, LANES)).T[0:1, :]
        dq_ref[...] = (jnp.where(masks[0], dqs[0], dqs[1]) * (HEAD_DIM ** -0.5)).astype(BF16)

        @pl.when(i == nb - 1)
        def _():
            dk_ref[...] = dka_ref[...].astype(BF16)
            dv_ref[...] = dva_ref[...].astype(BF16)
            dck_ref[0] = dca_ref[...]

    blk = pl.BlockSpec((tb, LANES), lambda j, i: (i, j))
    full = pl.BlockSpec((T, LANES), lambda j, i: (0, j))
    body = _hosted(body, 8, 6, 5, host, (N_PAIRS, nb))
    return pl.pallas_call(
        body, name=name, grid=(N_PAIRS, nb),
        in_specs=[blk,
                  pl.BlockSpec((T, LANES), lambda j, i: (0, cb + j)),
                  pl.BlockSpec((T, LANES), lambda j, i: (0, 2 * cb + j)),
                  pl.BlockSpec((tb, LANES), lambda j, i: (i, 3 * cb + j)),
                  pl.BlockSpec((1, 2, nb, tb), lambda j, i: (j, 0, 0, 0)),
                  blk, blk, pl.BlockSpec((2, tb, LANES), lambda j, i: (j, i, 0))] + x_in,
        out_specs=[blk, blk, full, full,
                   pl.BlockSpec((1, 1, 2, tb), lambda j, i: (j, i, 0, 0)),
                   pl.BlockSpec((1, 2, nb, tb), lambda j, i: (j, 0, 0, 0))] + x_out,
        out_shape=[jax.ShapeDtypeStruct((T, D), BF16)] * 4
        + [jax.ShapeDtypeStruct((N_PAIRS, nb, 2, tb), F32),
           jax.ShapeDtypeStruct((N_PAIRS, 2, nb, tb), F32)] + x_shapes,
        scratch_shapes=[pltpu.VMEM((T, LANES), BF16), pltpu.VMEM((T, LANES), BF16),
                        pltpu.VMEM((T, LANES), F32), pltpu.VMEM((T, LANES), F32),
                        pltpu.VMEM((2, nb, tb), F32)] + x_scratch,
        compiler_params=_cparams("arbitrary", "arbitrary"),
    )(proj, proj, proj, proj, cum4, o, dog, lp, *x_args)


def _cumsum_fwd(proj, bf_row, *, tt, name):
    T = proj.shape[0]
    cb = (proj.shape[1] - LANES) // LANES

    def body(f_ref, b_ref, out_ref, carry_ref):
        i = pl.program_id(0)

        @pl.when(i == 0)
        def _():
            carry_ref[...] = jnp.zeros_like(carry_ref)

        ls = -_softplus(-(f_ref[...] + b_ref[...]))
        tri = (lax.broadcasted_iota(jnp.int32, (tt, tt), 0)
               >= lax.broadcasted_iota(jnp.int32, (tt, tt), 1)).astype(F32)
        cum = jnp.dot(tri, ls, preferred_element_type=F32,
                      precision=lax.Precision.HIGHEST) + carry_ref[...]
        carry_ref[...] = cum[tt - 1:tt, :]
        out_ref[...] = cum.T

    return pl.pallas_call(
        body, name=name, grid=(T // tt,),
        in_specs=[pl.BlockSpec((tt, LANES), lambda i: (i, cb)),
                  pl.BlockSpec((1, LANES), lambda i: (0, 0))],
        out_specs=pl.BlockSpec((LANES, tt), lambda i: (0, i)),
        out_shape=jax.ShapeDtypeStruct((LANES, T), F32),
        scratch_shapes=[pltpu.VMEM((1, LANES), F32)],
        compiler_params=_cparams("arbitrary"),
    )(proj, bf_row)


def _cumsum_bwd(dcum_t, proj, bf_row, *, tt, name):
    T = proj.shape[0]
    cb = (proj.shape[1] - LANES) // LANES
    nt = T // tt

    def body(dc_ref, f_ref, b_ref, df_ref, db_ref, carry_ref):
        i = pl.program_id(0)

        @pl.when(i == 0)
        def _():
            carry_ref[...] = jnp.zeros_like(carry_ref)
            db_ref[...] = jnp.zeros_like(db_ref)

        dc = dc_ref[...].T
        tri = (lax.broadcasted_iota(jnp.int32, (tt, tt), 0)
               <= lax.broadcasted_iota(jnp.int32, (tt, tt), 1)).astype(F32)
        rev = jnp.dot(tri, dc, preferred_element_type=F32,
                      precision=lax.Precision.HIGHEST) + carry_ref[...]
        carry_ref[...] = rev[0:1, :]
        df = rev * _sigmoid(-(f_ref[...] + b_ref[...]))
        df_ref[...] = df.astype(BF16)
        db_ref[...] += jnp.sum(df, axis=0, keepdims=True)

    return pl.pallas_call(
        body, name=name, grid=(nt,),
        in_specs=[pl.BlockSpec((LANES, tt), lambda i: (0, nt - 1 - i)),
                  pl.BlockSpec((tt, LANES), lambda i: (nt - 1 - i, cb)),
                  pl.BlockSpec((1, LANES), lambda i: (0, 0))],
        out_specs=[pl.BlockSpec((tt, LANES), lambda i: (nt - 1 - i, 0)),
                   pl.BlockSpec((1, LANES), lambda i: (0, 0))],
        out_shape=[jax.ShapeDtypeStruct((T, LANES), BF16), jax.ShapeDtypeStruct((1, LANES), F32)],
        scratch_shapes=[pltpu.VMEM((1, LANES), F32)],
        compiler_params=_cparams("arbitrary"),
    )(dcum_t, proj, bf_row)


def _rows_down(x, before, sh):
    if sh == 0:
        return x
    rolled = pltpu.roll(x, sh, axis=0)
    row = lax.broadcasted_iota(jnp.int32, (SUBLANES, x.shape[1]), 0)
    head = jnp.where(row < sh, pltpu.roll(before, sh, axis=0), rolled[:SUBLANES])
    return jnp.concatenate([head, rolled[SUBLANES:]], axis=0)


def _rows_up(x, after, sh):
    if sh == 0:
        return x
    tt = x.shape[0]
    rolled = pltpu.roll(x, tt - sh, axis=0)
    row = lax.broadcasted_iota(jnp.int32, (SUBLANES, x.shape[1]), 0)
    tail = jnp.where(row >= SUBLANES - sh, pltpu.roll(after, SUBLANES - sh, axis=0),
                     rolled[tt - SUBLANES:])
    return jnp.concatenate([rolled[:tt - SUBLANES], tail], axis=0)


def _linear_scan(a, b, carry, up=False):
    tt, D = a.shape
    in_tile = lax.broadcasted_iota(jnp.int32, (tt, D), 0) % SUBLANES

    def shifted(x, sh, fill):
        if up:
            return jnp.where(in_tile >= SUBLANES - sh, fill, pltpu.roll(x, tt - sh, axis=0))
        return jnp.where(in_tile < sh, fill, pltpu.roll(x, sh, axis=0))

    sh = 1
    while sh < SUBLANES:
        b = b + a * shifted(b, sh, 0.0)
        a = a * shifted(a, sh, 1.0)
        sh *= 2
    tiles = range(0, tt, SUBLANES)
    out, edge = [], carry
    for r0 in (reversed(tiles) if up else tiles):
        h = a[r0:r0 + SUBLANES] * edge + b[r0:r0 + SUBLANES]
        edge = h[0:1] if up else h[SUBLANES - 1:SUBLANES]
        out.append(h)
    return jnp.concatenate(out[::-1] if up else out, axis=0)


def _rg_gates(u0, before, small_ref, wa_ref, wi_ref):
    taps = [_rows_down(u0, before, CONV_WIDTH - 1 - tap) for tap in range(CONV_WIDTH)]
    u = small_ref[4:5, :]
    for tap in range(CONV_WIDTH):
        u = u + taps[tap] * small_ref[tap:tap + 1, :]
    pa, pi = [], []
    for n in range(RNN_BLOCKS):
        ub = u[:, n * RNN_BLOCK_WIDTH:(n + 1) * RNN_BLOCK_WIDTH].astype(BF16)
        pa.append(jnp.dot(ub, wa_ref[n], preferred_element_type=F32))
        pi.append(jnp.dot(ub, wi_ref[n], preferred_element_type=F32))
    r = _sigmoid(jnp.concatenate(pa, axis=-1) + small_ref[5:6, :])
    ig = _sigmoid(jnp.concatenate(pi, axis=-1) + small_ref[6:7, :])
    spl = _softplus(-small_ref[7:8, :])
    log_a = (-LRU_C) * r * spl
    a = jnp.exp(log_a)
    s2 = jnp.tanh(-log_a) * (a * a + 1.0)
    inv_s = lax.rsqrt(s2)
    s = jnp.where(s2 > 0.0, s2 * inv_s, 0.0)
    return u, taps, r, ig, spl, a, s, inv_s


def _rg_fwd(proj, small, wa, wi, *, tt, name):
    T = proj.shape[0]
    D = RNN_BLOCKS * RNN_BLOCK_WIDTH
    hb = tt // SUBLANES

    def body(u0_ref, halo_ref, g_ref, small_ref, wa_ref, wi_ref, h_ref, y_ref, carry_ref):
        i = pl.program_id(0)

        @pl.when(i == 0)
        def _():
            carry_ref[...] = jnp.zeros_like(carry_ref)

        before = jnp.where(i == 0, 0.0, halo_ref[...])
        u, _, r, ig, spl, a, s, _ = _rg_gates(u0_ref[...], before, small_ref, wa_ref, wi_ref)
        h = _linear_scan(a, s * (ig * u), carry_ref[...])
        carry_ref[...] = h[tt - 1:tt]
        h_ref[...] = h
        gate = g_ref[...]
        y_ref[...] = (h * (gate * _sigmoid(gate))).astype(BF16)

    return pl.pallas_call(
        body, name=name, grid=(T // tt,),
        in_specs=[pl.BlockSpec((tt, D), lambda i: (i, 0)),
                  pl.BlockSpec((SUBLANES, D), lambda i: (jnp.maximum(i * hb - 1, 0), 0)),
                  pl.BlockSpec((tt, D), lambda i: (i, 1)),
                  pl.BlockSpec((SUBLANES, D), lambda i: (0, 0)),
                  pl.BlockSpec((RNN_BLOCKS, RNN_BLOCK_WIDTH, RNN_BLOCK_WIDTH), lambda i: (0, 0, 0)),
                  pl.BlockSpec((RNN_BLOCKS, RNN_BLOCK_WIDTH, RNN_BLOCK_WIDTH), lambda i: (0, 0, 0))],
        out_specs=[pl.BlockSpec((tt, D), lambda i: (i, 0)), pl.BlockSpec((tt, D), lambda i: (i, 0))],
        out_shape=[jax.ShapeDtypeStruct((T, D), F32), jax.ShapeDtypeStruct((T, D), BF16)],
        scratch_shapes=[pltpu.VMEM((1, D), F32)],
        compiler_params=_cparams("arbitrary"),
    )(proj, proj, proj, small, wa, wi)


def _rg_bwd(proj, hs, dy, small, wa, wi, *, tt, name):
    T = proj.shape[0]
    D = RNN_BLOCKS * RNN_BLOCK_WIDTH
    W = RNN_BLOCK_WIDTH
    hb = tt // SUBLANES
    nt = T // tt

    def body(u0_ref, uhalo_ref, g_ref, h_ref, hhalo_ref, dy_ref, small_ref, wa_ref, wi_ref,
             dp_ref, dwa_ref, dwi_ref, ds_ref, dunext_ref, carry_ref):
        i = pl.program_id(0)
        first_chunk = i == nt - 1

        @pl.when(i == 0)
        def _():
            carry_ref[...] = jnp.zeros_like(carry_ref)
            dunext_ref[...] = jnp.zeros_like(dunext_ref)
            dwa_ref[...] = jnp.zeros_like(dwa_ref)
            dwi_ref[...] = jnp.zeros_like(dwi_ref)
            ds_ref[...] = jnp.zeros_like(ds_ref)

        u_before = jnp.where(first_chunk, 0.0, uhalo_ref[...])
        h_before = jnp.where(first_chunk, 0.0, hhalo_ref[...])
        u, taps, r, ig, spl, a, s, inv_s = _rg_gates(u0_ref[...], u_before, small_ref, wa_ref,
                                                     wi_ref)
        gate = g_ref[...]
        sg = _sigmoid(gate)
        dy = dy_ref[...]
        dp_ref[:, D:] = (dy * h_ref[...] * (sg * (1.0 + gate * (1.0 - sg)))).astype(BF16)
        dy_h = dy * (gate * sg)
        carry = carry_ref[...]
        x = _linear_scan(a, a * dy_h, carry, up=True)
        carry_ref[...] = x[0:1]
        g = dy_h + _rows_up(x, jnp.broadcast_to(carry, (SUBLANES, D)), 1)
        h_prev = _rows_down(h_ref[...], h_before, 1)
        iu = ig * u
        d_iu = g * s
        dlog_a = (g * h_prev) * a - (g * iu) * (a * a) * inv_s
        dpre_a = (dlog_a * ((-LRU_C) * spl)) * r * (1.0 - r)
        dpre_i = (d_iu * u) * ig * (1.0 - ig)
        dlam = jnp.sum(dlog_a * r, axis=0, keepdims=True) * (LRU_C * _sigmoid(-small_ref[7:8, :]))
        du_parts = []
        for n in range(RNN_BLOCKS):
            sl = slice(n * W, (n + 1) * W)
            ub = u[:, sl].astype(BF16)
            da_n = dpre_a[:, sl].astype(BF16)
            di_n = dpre_i[:, sl].astype(BF16)
            dwa_ref[n] += lax.dot_general(ub, da_n, (((0,), (0,)), ((), ())),
                                          preferred_element_type=F32)
            dwi_ref[n] += lax.dot_general(ub, di_n, (((0,), (0,)), ((), ())),
                                          preferred_element_type=F32)
            du_parts.append(
                lax.dot_general(da_n, wa_ref[n], (((1,), (1,)), ((), ())), preferred_element_type=F32)
                + lax.dot_general(di_n, wi_ref[n], (((1,), (1,)), ((), ())), preferred_element_type=F32))
        du = d_iu * ig + jnp.concatenate(du_parts, axis=-1)
        for tap in range(CONV_WIDTH):
            ds_ref[tap:tap + 1, :] += jnp.sum(du * taps[tap], axis=0, keepdims=True)
        ds_ref[4:5, :] += jnp.sum(du, axis=0, keepdims=True)
        ds_ref[5:6, :] += jnp.sum(dpre_a, axis=0, keepdims=True)
        ds_ref[6:7, :] += jnp.sum(dpre_i, axis=0, keepdims=True)
        ds_ref[7:8, :] += dlam
        du_after = dunext_ref[...]
        du0 = jnp.zeros((tt, D), F32)
        for tap in range(CONV_WIDTH):
            du0 = du0 + _rows_up(du, du_after, CONV_WIDTH - 1 - tap) * small_ref[tap:tap + 1, :]
        dp_ref[:, :D] = du0.astype(BF16)
        dunext_ref[...] = du[0:SUBLANES, :]

    rev = lambda i: nt - 1 - i
    wspec = pl.BlockSpec((RNN_BLOCKS, W, W), lambda i: (0, 0, 0))
    return pl.pallas_call(
        body, name=name, grid=(nt,),
        in_specs=[pl.BlockSpec((tt, D), lambda i: (rev(i), 0)),
                  pl.BlockSpec((SUBLANES, D), lambda i: (jnp.maximum(rev(i) * hb - 1, 0), 0)),
                  pl.BlockSpec((tt, D), lambda i: (rev(i), 1)),
                  pl.BlockSpec((tt, D), lambda i: (rev(i), 0)),
                  pl.BlockSpec((SUBLANES, D), lambda i: (jnp.maximum(rev(i) * hb - 1, 0), 0)),
                  pl.BlockSpec((tt, D), lambda i: (rev(i), 0)),
                  pl.BlockSpec((SUBLANES, D), lambda i: (0, 0)),
                  wspec, wspec],
        out_specs=[pl.BlockSpec((tt, 2 * D), lambda i: (rev(i), 0)),
                   wspec, wspec, pl.BlockSpec((SUBLANES, D), lambda i: (0, 0))],
        out_shape=[jax.ShapeDtypeStruct((T, 2 * D), BF16),
                   jax.ShapeDtypeStruct((RNN_BLOCKS, W, W), F32),
                   jax.ShapeDtypeStruct((RNN_BLOCKS, W, W), F32),
                   jax.ShapeDtypeStruct((SUBLANES, D), F32)],
        scratch_shapes=[pltpu.VMEM((SUBLANES, D), F32), pltpu.VMEM((1, D), F32)],
        compiler_params=_cparams("arbitrary"),
    )(proj, proj, proj, hs, hs, dy, small, wa, wi)


def _out_ln(a, w, x, g, b, *, tt, name):
    T, D = x.shape
    K = a.shape[1]

    def body(a_ref, w_ref, x_ref, g_ref, b_ref, y_ref, yb_ref, zh_ref, rs_ref):
        h = jnp.dot(a_ref[...].astype(BF16), w_ref[...].astype(BF16), preferred_element_type=F32)
        z = ALPHA * x_ref[...] + h
        mu = jnp.mean(z, axis=-1, keepdims=True)
        zc = z - mu
        rstd = lax.rsqrt(jnp.mean(zc * zc, axis=-1, keepdims=True) + LN_EPS)
        zh = zc * rstd
        zh_ref[...] = zh
        rs_ref[...] = rstd
        y = zh * g_ref[...] + b_ref[...]
        y_ref[...] = y
        yb_ref[...] = y.astype(BF16)

    blk = pl.BlockSpec((tt, D), lambda i: (i, 0))
    row = pl.BlockSpec((1, D), lambda i: (0, 0))
    return pl.pallas_call(
        body, name=name, grid=(T // tt,),
        in_specs=[pl.BlockSpec((tt, K), lambda i: (i, 0)), pl.BlockSpec((K, D), lambda i: (0, 0)),
                  blk, row, row],
        out_specs=[blk, blk, blk, pl.BlockSpec((tt, 1), lambda i: (i, 0))],
        out_shape=[jax.ShapeDtypeStruct((T, D), F32), jax.ShapeDtypeStruct((T, D), BF16),
                   jax.ShapeDtypeStruct((T, D), F32), jax.ShapeDtypeStruct((T, 1), F32)],
        compiler_params=_cparams("parallel"),
    )(a, w, x, g, b)


def _ln_bwd_tile(dy, zh_ref, rs_ref, g_ref, dz_ref, dzb_ref, dg_ref, db_ref, first):
    @pl.when(first)
    def _():
        dg_ref[...] = jnp.zeros_like(dg_ref)
        db_ref[...] = jnp.zeros_like(db_ref)

    zh = zh_ref[...]
    dg_ref[...] += jnp.sum(dy * zh, axis=0, keepdims=True)
    db_ref[...] += jnp.sum(dy, axis=0, keepdims=True)
    dzh = dy * g_ref[...]
    m1 = jnp.mean(dzh, axis=-1, keepdims=True)
    m2 = jnp.mean(dzh * zh, axis=-1, keepdims=True)
    dz = rs_ref[...] * (dzh - m1 - zh * m2)
    dz_ref[...] = dz
    dzb_ref[...] = dz.astype(BF16)


def _ln_bwd_specs(T, D, tt):
    blk = pl.BlockSpec((tt, D), lambda i: (i, 0))
    row = pl.BlockSpec((1, D), lambda i: (0, 0))
    return ([blk, pl.BlockSpec((tt, 1), lambda i: (i, 0)), row], [blk, blk, row, row],
            [jax.ShapeDtypeStruct((T, D), F32), jax.ShapeDtypeStruct((T, D), BF16),
             jax.ShapeDtypeStruct((1, D), F32), jax.ShapeDtypeStruct((1, D), F32)])


def _loss_ln_bwd(y, tgt, zh, rstd, g, *, tt, name):
    T, D = y.shape
    ln_in, ln_out, ln_shapes = _ln_bwd_specs(T, D, tt)

    def body(y_ref, t_ref, zh_ref, rs_ref, g_ref, l_ref, dz_ref, dzb_ref, dg_ref, db_ref):
        first = pl.program_id(0) == 0

        @pl.when(first)
        def _():
            l_ref[...] = jnp.zeros_like(l_ref)

        e = y_ref[...] - t_ref[...]
        l_ref[...] += jnp.sum(e * e, axis=0, keepdims=True) * (0.5 / D)
        _ln_bwd_tile(e * (1.0 / D), zh_ref, rs_ref, g_ref, dz_ref, dzb_ref, dg_ref, db_ref, first)

    blk = pl.BlockSpec((tt, D), lambda i: (i, 0))
    return pl.pallas_call(
        body, name=name, grid=(T // tt,),
        in_specs=[blk, blk] + ln_in,
        out_specs=[pl.BlockSpec((1, D), lambda i: (0, 0))] + ln_out,
        out_shape=[jax.ShapeDtypeStruct((1, D), F32)] + ln_shapes,
        compiler_params=_cparams("arbitrary"),
    )(y, tgt, zh, rstd, g)


def _dx_ln_bwd(a, b, add, zh, rstd, g, *, tm, name):
    T, D = add.shape
    na = len(a)
    K = sum(p.shape[1] for p in a)
    ln_in, ln_out, ln_shapes = _ln_bwd_specs(T, D, tm)

    def body(*refs):
        a_refs, b_ref, add_ref = refs[:na], refs[na], refs[na + 1]
        av = [r[...].astype(BF16) for r in a_refs]
        av = av[0] if na == 1 else jnp.concatenate(av, axis=1)
        dy = lax.dot_general(av, b_ref[...].astype(BF16), (((1,), (1,)), ((), ())),
                             preferred_element_type=F32) + ALPHA * add_ref[...]
        _ln_bwd_tile(dy, *refs[na + 2:], pl.program_id(0) == 0)

    return pl.pallas_call(
        body, name=name, grid=(T // tm,),
        in_specs=[pl.BlockSpec((tm, p.shape[1]), lambda i: (i, 0)) for p in a]
        + [pl.BlockSpec((D, K), lambda i: (0, 0)), pl.BlockSpec((tm, D), lambda i: (i, 0))] + ln_in,
        out_specs=ln_out, out_shape=ln_shapes,
        compiler_params=_cparams("arbitrary"),
    )(*a, b, add, zh, rstd, g)


def _row_tile(rows, target):
    best = SUBLANES
    for t in range(SUBLANES, target + 1, SUBLANES):
        if rows % t == 0:
            best = t
    return best


def _add_own(g, recv, c_idx, *, tr, name):
    _, M, R, C = g.shape

    def body(c_ref, g_ref, r_ref, o_ref, ob_ref):
        s = g_ref[0] + r_ref[...]
        o_ref[...] = s
        ob_ref[...] = s.astype(BF16)

    blk = pl.BlockSpec((1, tr, C), lambda k, i, c: (k, i, 0))
    return pl.pallas_call(
        body, name=name,
        grid_spec=pltpu.PrefetchScalarGridSpec(
            num_scalar_prefetch=1, grid=(M, R // tr),
            in_specs=[pl.BlockSpec((1, 1, tr, C), lambda k, i, c: (c[0], k, i, 0)), blk],
            out_specs=[blk, blk]),
        out_shape=[jax.ShapeDtypeStruct((M, R, C), F32), jax.ShapeDtypeStruct((M, R, C), BF16)],
        compiler_params=_cparams("parallel", "parallel"),
    )(c_idx, g, recv)


def _adamw_math(g, w_ref, m_ref, v_ref, g_ref, d_ref, nm_ref, nv_ref):
    nm = ADAM_B1 * m_ref[...] + (1.0 - ADAM_B1) * g
    nv = ADAM_B2 * v_ref[...] + (1.0 - ADAM_B2) * (g * g)
    m_hat = nm / (1.0 - ADAM_B1 ** ADAM_STEP)
    v_hat = nv / (1.0 - ADAM_B2 ** ADAM_STEP)
    g_ref[...] = g
    nm_ref[...] = nm
    nv_ref[...] = nv
    d_ref[...] = (-ADAM_LR) * (m_hat / (jnp.sqrt(v_hat) + ADAM_EPS) + ADAM_WD * w_ref[...])


def _adamw(parts, w, m, v, *, tr, name):
    n, R, C = parts.shape
    tr = min(tr, R)

    def body(p_ref, w_ref, m_ref, v_ref, *out_refs):
        g = p_ref[0]
        for k in range(1, n):
            g = g + p_ref[k]
        _adamw_math(g, w_ref, m_ref, v_ref, *out_refs)

    blk = pl.BlockSpec((tr, C), lambda i: (i, 0))
    out = jax.ShapeDtypeStruct((R, C), F32)
    return pl.pallas_call(
        body, name=name, grid=(R // tr,),
        in_specs=[pl.BlockSpec((n, tr, C), lambda i: (0, i, 0)), blk, blk, blk],
        out_specs=[blk, blk, blk, blk], out_shape=[out, out, out, out],
        compiler_params=_cparams("parallel"),
    )(parts, w, m, v)


def _adamw_shard(parts_by_layer, place, w, m, v, *, tr, name):
    L, R, C = w.shape
    flat = [(l, a, pick) for l, parts in enumerate(parts_by_layer) for a, pick in parts]
    n = len(flat)

    def body(place_ref, *refs):
        w_ref, m_ref, v_ref = refs[n:n + 3]
        for layer in range(L):
            @pl.when(pl.program_id(0) == layer)
            def _(layer=layer):
                g = None
                for (l, _, _), r in zip(flat, refs[:n]):
                    if l == layer:
                        blk = r[(0,) * (len(r.shape) - 3)].astype(F32)
                        g = blk if g is None else g + blk
                _adamw_math(g, w_ref, m_ref, v_ref, *refs[n + 3:])

    blk = pl.BlockSpec((1, tr, C), lambda ly, i, s: (ly, i, 0))

    def part_spec(l, a, pick):
        return pl.BlockSpec((1,) * (a.ndim - 2) + (tr, C),
                            lambda ly, i, s: (*pick(s), jnp.where(ly == l, i, 0), 0))

    out = jax.ShapeDtypeStruct(w.shape, F32)
    return pl.pallas_call(
        body, name=name,
        grid_spec=pltpu.PrefetchScalarGridSpec(
            num_scalar_prefetch=1, grid=(L, R // tr),
            in_specs=[part_spec(*f) for f in flat] + [blk, blk, blk],
            out_specs=[blk, blk, blk, blk]),
        out_shape=[out, out, out, out],
        compiler_params=_cparams("arbitrary", "arbitrary"),
    )(place, *[a for _, a, _ in flat], w, m, v)


def _two_stage_parts(h, recv):
    return [(h, lambda s: (s[0], 0))] + [(recv, lambda s, d=d: (s[0] ^ d, 0)) for d in (1, 2, 3)]


def _direct_parts(g, recv):
    return [(g, lambda s: (s[1], s[0]))] + [
        (recv, lambda s, a=p // 4, d=p % 4: (s[1] ^ a, s[0] ^ d)) for p in range(1, 8)]


SHARD_AXIS = dict(attn_w_in=1, attn_w_out=0, rnn_w_in=1, rnn_w_out=0, rnn_w_a=1, rnn_w_i=1,
                  rnn_conv_w=1, rnn_conv_b=0, rnn_b_a=0, rnn_b_i=0, rnn_lambda=0)
RNN_ROWED = ("rnn_w_out", "rnn_w_a", "rnn_w_i")
SMALL = ("rnn_conv_w", "rnn_conv_b", "rnn_b_a", "rnn_b_i", "rnn_lambda")
PACK_C = 1024


def _elems(shape):
    n = 1
    for s in shape:
        n *= s
    return n


def _pack_rows(p, idx, dtype):
    parts = [p[k][idx].astype(dtype).reshape(-1, PACK_C) for k in RNN_ROWED]
    small = jnp.concatenate([p[k][idx].reshape(-1) for k in SMALL])
    tile_rows = SUBLANES * (4 // jnp.dtype(dtype).itemsize)
    if dtype == BF16:
        small = lax.bitcast_convert_type(small, BF16)
    small = small.reshape(-1, PACK_C)
    parts.append(jnp.pad(small, ((0, tile_rows - small.shape[0]), (0, 0))))
    return jnp.concatenate(parts, axis=0)


def _unpack_rows(flat, shapes):
    lead = flat.shape[:-2]
    out, r = {}, 0
    for k in RNN_ROWED:
        n = _elems(shapes[k]) // PACK_C
        out[k] = flat[..., r:r + n, :].reshape(lead + shapes[k])
        r += n
    n_small = sum(_elems(shapes[k]) for k in SMALL)
    small = flat[..., r:r + n_small // PACK_C, :].reshape(lead + (-1,))
    o = 0
    for k in SMALL:
        n = _elems(shapes[k])
        out[k] = small[..., o:o + n].reshape(lead + shapes[k])
        o += n
    return out


def _join_columns(g, width, *, tr, name):
    _, _, R, S = g.shape

    def body(*refs):
        o_ref = refs[8]
        parts = [refs[r][0, 0].astype(F32) for r in range(8)]
        if width > 8 * S:
            parts.append(jnp.zeros((tr, width - 8 * S), F32))
        o_ref[...] = jnp.concatenate(parts, axis=-1).astype(o_ref.dtype)

    def shard(r):
        return pl.BlockSpec((1, 1, tr, S), lambda i: (r % 2, r // 2, i, 0))

    return pl.pallas_call(
        body, name=name, grid=(R // tr,),
        in_specs=[shard(r) for r in range(8)],
        out_specs=pl.BlockSpec((tr, width), lambda i: (i, 0)),
        out_shape=jax.ShapeDtypeStruct((R, width), g.dtype),
        compiler_params=_cparams("parallel"),
    )(*([g] * 8))


def _split_columns(parts, S, *, tr, name):
    R = parts[0].shape[0]
    n = len(parts)

    def body(*refs):
        o_ref = refs[n]
        x = jnp.concatenate([r[...] for r in refs[:n]], axis=1)
        for r in range(8):
            o_ref[r % 2, r // 2] = x[:, r * S:(r + 1) * S]

    return pl.pallas_call(
        body, name=name, grid=(R // tr,),
        in_specs=[pl.BlockSpec((tr, p.shape[1]), lambda i: (i, 0)) for p in parts],
        out_specs=pl.BlockSpec((2, 4, tr, S), lambda i: (0, 0, i, 0)),
        out_shape=jax.ShapeDtypeStruct((2, 4, R, S), parts[0].dtype),
        compiler_params=_cparams("parallel"),
    )(*parts)


def _to_full(g, k, sh):
    ax, nd = SHARD_AXIS[k], len(sh)
    perm = tuple(range(2, 2 + ax)) + (1, 0) + tuple(range(2 + ax, 2 + nd))
    return g.transpose(perm).reshape(sh[:ax] + (8 * sh[ax],) + sh[ax + 1:])


def _from_full(full, k, sh):
    ax, nd = SHARD_AXIS[k], len(sh)
    t = full.reshape(sh[:ax] + (4, 2, sh[ax]) + sh[ax + 1:])
    return t.transpose((ax + 1, ax) + tuple(range(ax)) + tuple(range(ax + 2, nd + 2)))


def _unpack_gathered_rows(g, shapes):
    out, r = {}, 0
    for k in RNN_ROWED:
        n = _elems(shapes[k]) // PACK_C
        out[k] = _to_full(g[:, :, r:r + n].reshape((2, 4) + shapes[k]), k, shapes[k])
        r += n
    n_small = sum(_elems(shapes[k]) for k in SMALL)
    nr = 2 * n_small // PACK_C
    small = lax.bitcast_convert_type(g[:, :, r:r + nr].reshape(2, 4, n_small, 2), F32)
    o = 0
    for k in SMALL:
        n = _elems(shapes[k])
        out[k] = _to_full(small[:, :, o:o + n].reshape((2, 4) + shapes[k]), k, shapes[k])
        o += n
    return out


def _pack_grad_rows(full, shapes):
    parts = [_from_full(full[k], k, shapes[k]).reshape(2, 4, -1, PACK_C) for k in RNN_ROWED]
    small = jnp.concatenate(
        [_from_full(full[k], k, shapes[k]).reshape(2, 4, -1) for k in SMALL], axis=-1)
    small = small.reshape(2, 4, -1, PACK_C)
    parts.append(jnp.pad(small, ((0, 0), (0, 0), (0, SUBLANES - small.shape[2]), (0, 0))))
    return jnp.concatenate(parts, axis=2)


def kernel(x, ln_g, ln_b, attn_w_in, attn_b_f, attn_w_out, rnn_w_in, rnn_conv_w, rnn_conv_b, rnn_w_a, rnn_b_a, rnn_w_i, rnn_b_i, rnn_lambda, rnn_w_out, loss_target, m_ln_g, m_ln_b, m_attn_w_in, m_attn_b_f, m_attn_w_out, m_rnn_w_in, m_rnn_conv_w, m_rnn_conv_b, m_rnn_w_a, m_rnn_b_a, m_rnn_w_i, m_rnn_b_i, m_rnn_lambda, m_rnn_w_out, v_ln_g, v_ln_b, v_attn_w_in, v_attn_b_f, v_attn_w_out, v_rnn_w_in, v_rnn_conv_w, v_rnn_conv_b, v_rnn_w_a, v_rnn_b_a, v_rnn_w_i, v_rnn_b_i, v_rnn_lambda, v_rnn_w_out):
    w_loc = dict(attn_w_in=attn_w_in, attn_w_out=attn_w_out, rnn_w_in=rnn_w_in, rnn_w_a=rnn_w_a,
                 rnn_w_i=rnn_w_i, rnn_w_out=rnn_w_out, rnn_conv_w=rnn_conv_w, rnn_conv_b=rnn_conv_b,
                 rnn_b_a=rnn_b_a, rnn_b_i=rnn_b_i, rnn_lambda=rnn_lambda)
    m_loc = dict(attn_w_in=m_attn_w_in, attn_w_out=m_attn_w_out, rnn_w_in=m_rnn_w_in,
                 rnn_w_a=m_rnn_w_a, rnn_w_i=m_rnn_w_i, rnn_w_out=m_rnn_w_out,
                 rnn_conv_w=m_rnn_conv_w, rnn_conv_b=m_rnn_conv_b, rnn_b_a=m_rnn_b_a,
                 rnn_b_i=m_rnn_b_i, rnn_lambda=m_rnn_lambda)
    v_loc = dict(attn_w_in=v_attn_w_in, attn_w_out=v_attn_w_out, rnn_w_in=v_rnn_w_in,
                 rnn_w_a=v_rnn_w_a, rnn_w_i=v_rnn_w_i, rnn_w_out=v_rnn_w_out,
                 rnn_conv_w=v_rnn_conv_w, rnn_conv_b=v_rnn_conv_b, rnn_b_a=v_rnn_b_a,
                 rnn_b_i=v_rnn_b_i, rnn_lambda=v_rnn_lambda)
    shapes = {k: tuple(a.shape[1:]) for k, a in w_loc.items()}
    T, D = x.shape[1], x.shape[2]
    n_f = attn_b_f.shape[1]
    tb = min(1024, T)
    tb_bwd = min(512, T)
    tt_rg = min(128, T)
    tt_ln = min(256, T)
    c_idx = lax.axis_index("c").astype(jnp.int32).reshape(1)
    me_idx = (2 * lax.axis_index("x") + lax.axis_index("y")).astype(jnp.int32).reshape(1)
    place = jnp.concatenate([me_idx, c_idx])

    def attn_w_in_full(g_in, idx):
        return _join_columns(g_in, 4 * D + LANES, tr=256, name=f"a_join{idx}")

    def attn_w_out_full(g_out):
        return _to_full(g_out, "attn_w_out", shapes["attn_w_out"])

    def rnn_weights(g_in, g_rows, idx):
        w = _unpack_gathered_rows(g_rows, shapes)
        w["rnn_w_in"] = _join_columns(g_in, 2 * D, tr=256, name=f"r_join{idx}")
        w["small"] = jnp.concatenate([w["rnn_conv_w"], w["rnn_conv_b"][None], w["rnn_b_a"][None],
                                      w["rnn_b_i"][None], w["rnn_lambda"][None]])
        return w

    g0 = _ag_c(_run_exchange(_Exchange("gather", [attn_w_in[0].astype(BF16)]), "ag_w0_xy"),
               "ag_w0_c")
    later = _Exchange("gather8", [attn_w_in[1].astype(BF16)] + [
        a for i in range(2) for a in (attn_w_out[i].astype(BF16), rnn_w_in[i].astype(BF16),
                                      _pack_rows(w_loc, i, BF16))])
    w_attn_in, w_attn_out, w_rnn = [attn_w_in_full(g0[0], 0), None], [None, None], [None, None]
    bf_rows = jnp.pad(attn_b_f, ((0, 0), (0, LANES - n_f)))[:, None, :]

    xs, xb, saved = [x[0]], [x[0]], []
    for layer in range(DEPTH):
        idx, xl, xm = layer // 2, xs[-1], xb[-1]
        if layer % 2 == 0:
            proj = _matmul(xm, w_attn_in[idx], trans_b=False, tm=512, tn=1408,
                           name=f"a_proj{layer}")
            cum_t = _cumsum_fwd(proj, bf_rows[idx], tt=min(512, T), name=f"a_cum{layer}")
            cum2 = cum_t[:N_HEADS].reshape(N_PAIRS, 2, T)
            o, og, lp, *got = _flash_fwd(proj, cum2.reshape(N_PAIRS, 2, T // tb, tb), tb=tb,
                                         name=f"a_fwd{layer}", host=later if layer == 0 else None)
            cum4 = cum2.reshape(N_PAIRS, 2, T // tb_bwd, tb_bwd)
            if layer == 0:
                w_attn_in[1] = attn_w_in_full(got[0], 1)
                w_attn_out = [attn_w_out_full(got[1 + 3 * i]) for i in range(2)]
                w_rnn = [rnn_weights(got[2 + 3 * i], got[3 + 3 * i], i) for i in range(2)]
            branch, w_out = og, w_attn_out[idx]
            saved.append((proj, cum4, o, og, lp))
        else:
            w = w_rnn[idx]
            proj = _matmul(xm, w["rnn_w_in"], trans_b=False, tm=512, tn=1024,
                           name=f"r_proj{layer}")
            hs, yr = _rg_fwd(proj, w["small"], w["rnn_w_a"], w["rnn_w_i"], tt=tt_rg,
                             name=f"r_fwd{layer}")
            branch, w_out = yr, w["rnn_w_out"]
            saved.append((proj, hs, yr))
        y, yb, zh, rstd = _out_ln(branch, w_out, xl, ln_g[layer][None], ln_b[layer][None],
                                  tt=512, name=f"out_ln{layer}")
        saved[-1] = saved[-1] + (zh, rstd)
        xs.append(y)
        xb.append(yb)

    def ln_below(layer):
        return saved[layer][-2:] + (ln_g[layer][None],)

    loss_lanes, *ln_grads = _loss_ln_bwd(xs[-1], loss_target[0], *ln_below(DEPTH - 1), tt=tt_ln,
                                         name="loss_ln_bwd")
    loss_part = jnp.sum(loss_lanes)

    def reduce_pair(gs, layer):
        recv = _rs_c(gs, f"rs_c{layer}")
        outs = [_add_own(g, r, c_idx, tr=_row_tile(g.shape[2], 512), name=f"rs_add{layer}_{n}")
                for n, (g, r) in enumerate(zip(gs, recv))]
        return [o[0][:, None] for o in outs], [o[1][:, None] for o in outs]

    def rep_pack(lg, lb, bf, scalar=0.0):
        rows = jnp.concatenate([lg, lb, jnp.pad(bf.reshape(1, -1), ((0, 0), (0, D - 2 * n_f))),
                                jnp.broadcast_to(jnp.asarray(scalar, F32), (1, D))])
        return jnp.pad(rows, ((0, 16 - rows.shape[0]), (0, 0)))

    SCALAR_ROW = 2 * DEPTH + 1

    part, got_parts = [None] * DEPTH, [None] * DEPTH
    d_ln_g, d_ln_b, d_bf = [None] * DEPTH, [None] * DEPTH, [None, None]
    for layer in reversed(range(DEPTH)):
        idx, xm = layer // 2, xb[layer]
        dz, dzb, dg, db = ln_grads
        d_ln_g[layer], d_ln_b[layer] = dg[0], db[0]
        if layer % 2 == 0:
            w_in, w_out = w_attn_in[idx], w_attn_out[idx]
            proj, cum4, o, og, lp = saved[layer][:5]
            dog = _matmul(dzb, w_out, trans_b=True, tm=512, tn=1024, name=f"a_dog{layer}")
            dwo = _matmul_tn(og, dzb, tm=512, tn=1024, tk=1024, name=f"a_dwo{layer}")
            g_out = _from_full(dwo, "attn_w_out", shapes["attn_w_out"])
            riders = [l for l in range(layer + 1, DEPTH) if got_parts[l] is None]
            early = [g_out] if layer == 0 else []
            host = _Exchange("scatter8", [g for l in riders for g in part[l]] + early)
            dq, dgate, dk, dv, dcum_q, dcum_k, *got = _flash_bwd(proj, cum4, o, dog, lp, tb=tb_bwd,
                                                                 name=f"a_bwd{layer}", host=host)
            for l in riders:
                got_parts[l], got = got[:len(part[l])], got[len(part[l]):]
            dcum_t = (dcum_q.transpose(0, 2, 1, 3) + dcum_k).reshape(N_HEADS, T)
            dcum_t = jnp.pad(dcum_t, ((0, LANES - N_HEADS), (0, 0)))
            df, dbf = _cumsum_bwd(dcum_t, proj, bf_rows[idx], tt=min(512, T), name=f"a_dcum{layer}")
            d_bf[idx] = dbf[0, :n_f]
            dproj = [dq, dk, dv, dgate, df]
            rep_host = _Exchange("gather8", [rep_pack(jnp.stack(d_ln_g), jnp.stack(d_ln_b),
                                                      jnp.stack(d_bf), loss_part)]) if layer == 0 else None
            dwi = _matmul_tn_parts(xm, dproj, tm=512, tk=1024, name=f"a_dwi{layer}", host=rep_host)
            if layer == 0:
                dwi, rep = dwi[:-1], dwi[-1]
            g_in = _split_columns(dwi, shapes["attn_w_in"][1], tr=256, name=f"a_split{layer}")
            if layer > 0:
                part[layer] = [g_in, g_out]
                ln_grads = _dx_ln_bwd(dproj, w_in, dz, *ln_below(layer - 1), tm=256,
                                      name=f"a_dx{layer}")
            else:
                half, narrow = reduce_pair([g_in], layer)
                dy, recv = _matmul(dproj, w_in, trans_b=True, tm=512, tn=1024, name=f"a_dx{layer}",
                                   add=dz, add_scale=ALPHA, host=_Exchange("scatter", narrow))
                last_parts = [_two_stage_parts(half[0], recv), _direct_parts(g_out, got[0])]
        else:
            w = w_rnn[idx]
            proj, hs, yr = saved[layer][:3]
            dyr = _matmul(dzb, w["rnn_w_out"], trans_b=True, tm=512, tn=1024, name=f"r_dy{layer}")
            dwo = _matmul_tn(yr, dzb, tm=512, tn=1024, tk=1024, name=f"r_dwo{layer}")
            dproj, dwa, dwi_, dsm = _rg_bwd(proj, hs, dyr, w["small"], w["rnn_w_a"], w["rnn_w_i"],
                                            tt=tt_rg, name=f"r_bwd{layer}")
            dwin = _matmul_tn(xm, dproj, tm=512, tn=2048, tk=1024, name=f"r_dwi{layer}")
            ln_grads = _dx_ln_bwd([dproj], w["rnn_w_in"], dz, *ln_below(layer - 1), tm=512,
                                  name=f"r_dx{layer}")
            full = dict(rnn_w_out=dwo, rnn_w_a=dwa, rnn_w_i=dwi_, rnn_conv_w=dsm[0:4],
                        rnn_conv_b=dsm[4], rnn_b_a=dsm[5], rnn_b_i=dsm[6], rnn_lambda=dsm[7])
            part[layer] = [_split_columns([dwin], shapes["rnn_w_in"][1], tr=256,
                                          name=f"r_split{layer}"),
                           _pack_grad_rows(full, shapes)]
    grad_x = dy[None]

    def grad_parts(layer, n):
        return last_parts[n] if layer == 0 else _direct_parts(part[layer][n], got_parts[layer][n])

    def update(k, n, wmv):
        layers = [2 * idx + (0 if k.startswith("attn") else 1) for idx in range(2)]
        return _adamw_shard([grad_parts(layer, n) for layer in layers], place, *wmv,
                            tr=_row_tile(wmv[0].shape[1], 256), name=f"adamw_{k}")

    shard_outs = [dict() for _ in range(4)]
    for k, n in (("attn_w_in", 0), ("attn_w_out", 1), ("rnn_w_in", 0)):
        for j, a in enumerate(update(k, n, (w_loc[k], m_loc[k], v_loc[k]))):
            shard_outs[j][k] = a
    rows_wmv = [jnp.stack([_pack_rows(d, idx, F32) for idx in range(2)]) for d in (w_loc, m_loc, v_loc)]
    for j, a in enumerate(update("rnn_rows", 1, rows_wmv)):
        shard_outs[j].update(_unpack_rows(a, shapes))
    g_sh, d_sh, nm_sh, nv_sh = shard_outs

    rg, rd, rm, rv = _adamw(rep.reshape(8, 16, D), rep_pack(ln_g, ln_b, attn_b_f),
                            rep_pack(m_ln_g, m_ln_b, m_attn_b_f),
                            rep_pack(v_ln_g, v_ln_b, v_attn_b_f), tr=16, name="adamw_rep")
    loss = rg[SCALAR_ROW, 0]

    def rep_unpack(a):
        return dict(ln_g=a[0:DEPTH], ln_b=a[DEPTH:2 * DEPTH],
                    attn_b_f=a[2 * DEPTH, :2 * n_f].reshape(2, n_f))

    order = ("ln_g", "ln_b", "attn_w_in", "attn_b_f", "attn_w_out", "rnn_w_in", "rnn_conv_w",
             "rnn_conv_b", "rnn_w_a", "rnn_b_a", "rnn_w_i", "rnn_b_i", "rnn_lambda", "rnn_w_out")
    outs = [loss, grad_x]
    for sh, rp in ((g_sh, rg), (d_sh, rd), (nm_sh, rm), (nv_sh, rv)):
        allp = {**sh, **rep_unpack(rp)}
        outs.extend(allp[k] for k in order)
    return tuple(outs)
```
